```python
import jax, jax.numpy as jnp
from jax import lax
import numpy as np

D_MODEL = 1024
BATCH = 8
SEQ = 4096
DEPTH = 1

CHUNK = 64
Q_BLOCK = 128
HEAD_DIM = 64
N_SB_HEADS = 8
N_FOX_HEADS = 8
SB_WIDTH = N_SB_HEADS * HEAD_DIM
FOX_WIDTH = N_FOX_HEADS * HEAD_DIM
D_FF = -(-(8 * D_MODEL) // (3 * 256)) * 256
N_COND = 6
LN_EPS = 1e-5
DEEPNORM_ALPHA = (2 * DEPTH) ** 0.25
DEEPNORM_BETA = (8 * DEPTH) ** -0.25
OFF_SB = 0
OFF_FOX = OFF_SB + 3 * SB_WIDTH
OFF_FGATE = OFF_FOX + 3 * FOX_WIDTH
OFF_BGATE = OFF_FGATE + N_FOX_HEADS
IN_COLS = OFF_BGATE + 2 * D_MODEL

kernel_name = "hybrid_stickbreak_fox_gated_block"


def layer_norm(x, gain=None, bias=None):
    xf = x.astype(jnp.float32)
    mu = jnp.mean(xf, axis=-1, keepdims=True)
    var = jnp.mean(jnp.square(xf - mu), axis=-1, keepdims=True)
    y = (xf - mu) * lax.rsqrt(var + LN_EPS)
    if gain is not None:
        y = y * gain.astype(jnp.float32) + bias.astype(jnp.float32)
    return y.astype(x.dtype)


def modulate(x, shift, scale):
    return layer_norm(x) * (1.0 + scale[:, None, :]) + shift[:, None, :]


def split_heads(t, n_heads):
    b, s, _ = t.shape
    return t.reshape(b, s, n_heads, HEAD_DIM).transpose(0, 2, 1, 3)


def merge_heads(t):
    b, h, s, d = t.shape
    return t.transpose(0, 2, 1, 3).reshape(b, s, h * d)


def stick_breaking_attention(q, k, v):
    seq = q.shape[2]
    scale = HEAD_DIM ** -0.5
    outs = []
    for i in range(seq // Q_BLOCK):
        q0 = i * Q_BLOCK
        kv_len = q0 + Q_BLOCK
        qb = q[:, :, q0:kv_len]
        kb = k[:, :, :kv_len]
        vb = v[:, :, :kv_len]
        z = jnp.einsum('bhqd,bhkd->bhqk', qb, kb, preferred_element_type=jnp.float32) * scale
        qpos = q0 + jnp.arange(Q_BLOCK)[:, None]
        kpos = jnp.arange(kv_len)[None, :]
        mask = kpos < qpos
        log_1m = jnp.where(mask, jax.nn.log_sigmoid(-z), 0.0)
        suffix = lax.cumsum(log_1m, axis=3, reverse=True) - log_1m
        a = jnp.where(mask, jnp.exp(jax.nn.log_sigmoid(z) + suffix), 0.0)
        outs.append(jnp.einsum('bhqk,bhkd->bhqd', a.astype(v.dtype), vb))
    return jnp.concatenate(outs, axis=2)


def forgetting_attention(q, k, v, f_cum):
    seq = q.shape[2]
    scale = HEAD_DIM ** -0.5
    outs = []
    for i in range(seq // Q_BLOCK):
        q0 = i * Q_BLOCK
        kv_len = q0 + Q_BLOCK
        qb = q[:, :, q0:kv_len]
        kb = k[:, :, :kv_len]
        vb = v[:, :, :kv_len]
        z = (jnp.einsum('bhqd,bhkd->bhqk', qb, kb, preferred_element_type=jnp.float32) * scale
             + f_cum[:, :, q0:kv_len, None] - f_cum[:, :, None, :kv_len])
        qpos = q0 + jnp.arange(Q_BLOCK)[:, None]
        kpos = jnp.arange(kv_len)[None, :]
        p = jax.nn.softmax(jnp.where(kpos <= qpos, z, -jnp.inf), axis=-1)
        outs.append(jnp.einsum('bhqk,bhkd->bhqd', p.astype(v.dtype), vb))
    return jnp.concatenate(outs, axis=2)


def _fwd_setup_inputs(seed: int = 0) -> dict:
    key = jax.random.key(seed)
    ks = jax.random.split(key, 20)
    f32 = jnp.float32
    nrm = lambda k, shape, s: (jax.random.normal(k, shape, f32) * s).astype(f32)
    return {
        "x": nrm(ks[0], (BATCH, SEQ, D_MODEL), 1.0),
        "c": nrm(ks[1], (BATCH, D_MODEL), 1.0),
        "w_ada": nrm(ks[2], (DEPTH, D_MODEL, N_COND * D_MODEL), 0.5 * D_MODEL ** -0.5),
        "b_ada": nrm(ks[3], (DEPTH, N_COND * D_MODEL), 0.02),
        "w_in": nrm(ks[4], (DEPTH, D_MODEL, IN_COLS), D_MODEL ** -0.5),
        "b_gate": nrm(ks[5], (DEPTH, 2 * D_MODEL), 0.02),
        "b_forget": 3.0 + nrm(ks[6], (DEPTH, N_FOX_HEADS), 1.0),
        "w_sb_out": nrm(ks[7], (DEPTH, SB_WIDTH, D_MODEL), SB_WIDTH ** -0.5),
        "w_fox_out": nrm(ks[8], (DEPTH, FOX_WIDTH, D_MODEL), FOX_WIDTH ** -0.5),
        "w_o": nrm(ks[9], (DEPTH, D_MODEL, D_MODEL), DEEPNORM_BETA * D_MODEL ** -0.5),
        "ln1_g": 1.0 + nrm(ks[10], (DEPTH, D_MODEL), 0.02),
        "ln1_b": nrm(ks[11], (DEPTH, D_MODEL), 0.02),
        "w_ffn_gate": nrm(ks[12], (DEPTH, D_MODEL, D_FF), D_MODEL ** -0.5),
        "w_ffn_up": nrm(ks[13], (DEPTH, D_MODEL, D_FF), D_MODEL ** -0.5),
        "w_ffn_down": nrm(ks[14], (DEPTH, D_FF, D_MODEL), DEEPNORM_BETA * D_FF ** -0.5),
        "ln2_g": 1.0 + nrm(ks[15], (DEPTH, D_MODEL), 0.02),
        "ln2_b": nrm(ks[16], (DEPTH, D_MODEL), 0.02),
    }


def _fwd_reference(x, c, w_ada, b_ada, w_in, b_gate, b_forget, w_sb_out, w_fox_out, w_o,
              ln1_g, ln1_b, w_ffn_gate, w_ffn_up, w_ffn_down, ln2_g, ln2_b):
    c_act = jax.nn.silu(c)
    for l in range(DEPTH):
        ada = c_act @ w_ada[l] + b_ada[l]
        sh1, sc1, g1, sh2, sc2, g2 = jnp.split(ada, N_COND, axis=-1)

        u = modulate(x, sh1, sc1)
        proj = u @ w_in[l]
        q_sb = split_heads(proj[..., OFF_SB:OFF_SB + SB_WIDTH], N_SB_HEADS)
        k_sb = split_heads(proj[..., OFF_SB + SB_WIDTH:OFF_SB + 2 * SB_WIDTH], N_SB_HEADS)
        v_sb = split_heads(proj[..., OFF_SB + 2 * SB_WIDTH:OFF_FOX], N_SB_HEADS)
        q_fx = split_heads(proj[..., OFF_FOX:OFF_FOX + FOX_WIDTH], N_FOX_HEADS)
        k_fx = split_heads(proj[..., OFF_FOX + FOX_WIDTH:OFF_FOX + 2 * FOX_WIDTH], N_FOX_HEADS)
        v_fx = split_heads(proj[..., OFF_FOX + 2 * FOX_WIDTH:OFF_FGATE], N_FOX_HEADS)
        f_logit = proj[..., OFF_FGATE:OFF_BGATE].astype(jnp.float32) + b_forget[l].astype(jnp.float32)
        f_cum = jnp.cumsum(jax.nn.log_sigmoid(f_logit), axis=1).transpose(0, 2, 1)
        gate_logit = proj[..., OFF_BGATE:] + b_gate[l]
        g_sb = jax.nn.sigmoid(gate_logit[..., :D_MODEL])
        g_fx = jax.nn.sigmoid(gate_logit[..., D_MODEL:])

        y_sb = merge_heads(stick_breaking_attention(q_sb, k_sb, v_sb)) @ w_sb_out[l]
        y_fx = merge_heads(forgetting_attention(q_fx, k_fx, v_fx, f_cum)) @ w_fox_out[l]
        mix = (g_sb * y_sb + g_fx * y_fx) @ w_o[l]
        x = layer_norm(DEEPNORM_ALPHA * x + g1[:, None, :] * mix, ln1_g[l], ln1_b[l])

        u = modulate(x, sh2, sc2)
        h = (jax.nn.silu(u @ w_ffn_gate[l]) * (u @ w_ffn_up[l])) @ w_ffn_down[l]
        x = layer_norm(DEEPNORM_ALPHA * x + g2[:, None, :] * h, ln2_g[l], ln2_b[l])
    return x


import jax as _jax
import jax.numpy as _jnp

TWIN_FORMAT = 'train_step'
FWD_PARAMS = ['x', 'c', 'w_ada', 'b_ada', 'w_in', 'b_gate', 'b_forget', 'w_sb_out', 'w_fox_out', 'w_o', 'ln1_g', 'ln1_b', 'w_ffn_gate', 'w_ffn_up', 'w_ffn_down', 'ln2_g', 'ln2_b']
TWIN_WEIGHTS = ['w_ada', 'b_ada', 'w_in', 'b_gate', 'b_forget', 'w_sb_out', 'w_fox_out', 'w_o', 'ln1_g', 'ln1_b', 'w_ffn_gate', 'w_ffn_up', 'w_ffn_down', 'ln2_g', 'ln2_b']
TWIN_DIFF_INPUT = 'x'
TWIN_INPUTS = ['x', 'c', 'w_ada', 'b_ada', 'w_in', 'b_gate', 'b_forget', 'w_sb_out', 'w_fox_out', 'w_o', 'ln1_g', 'ln1_b', 'w_ffn_gate', 'w_ffn_up', 'w_ffn_down', 'ln2_g', 'ln2_b', 'loss_target', 'm_w_ada', 'm_b_ada', 'm_w_in', 'm_b_gate', 'm_b_forget', 'm_w_sb_out', 'm_w_fox_out', 'm_w_o', 'm_ln1_g', 'm_ln1_b', 'm_w_ffn_gate', 'm_w_ffn_up', 'm_w_ffn_down', 'm_ln2_g', 'm_ln2_b', 'v_w_ada', 'v_b_ada', 'v_w_in', 'v_b_gate', 'v_b_forget', 'v_w_sb_out', 'v_w_fox_out', 'v_w_o', 'v_ln1_g', 'v_ln1_b', 'v_w_ffn_gate', 'v_w_ffn_up', 'v_w_ffn_down', 'v_ln2_g', 'v_ln2_b']
TWIN_OUTPUTS = ['loss', 'grad_x', 'grad_w_ada', 'grad_b_ada', 'grad_w_in', 'grad_b_gate', 'grad_b_forget', 'grad_w_sb_out', 'grad_w_fox_out', 'grad_w_o', 'grad_ln1_g', 'grad_ln1_b', 'grad_w_ffn_gate', 'grad_w_ffn_up', 'grad_w_ffn_down', 'grad_ln2_g', 'grad_ln2_b', 'delta_w_ada', 'delta_b_ada', 'delta_w_in', 'delta_b_gate', 'delta_b_forget', 'delta_w_sb_out', 'delta_w_fox_out', 'delta_w_o', 'delta_ln1_g', 'delta_ln1_b', 'delta_w_ffn_gate', 'delta_w_ffn_up', 'delta_w_ffn_down', 'delta_ln2_g', 'delta_ln2_b', 'new_m_w_ada', 'new_m_b_ada', 'new_m_w_in', 'new_m_b_gate', 'new_m_b_forget', 'new_m_w_sb_out', 'new_m_w_fox_out', 'new_m_w_o', 'new_m_ln1_g', 'new_m_ln1_b', 'new_m_w_ffn_gate', 'new_m_w_ffn_up', 'new_m_w_ffn_down', 'new_m_ln2_g', 'new_m_ln2_b', 'new_v_w_ada', 'new_v_b_ada', 'new_v_w_in', 'new_v_b_gate', 'new_v_b_forget', 'new_v_w_sb_out', 'new_v_w_fox_out', 'new_v_w_o', 'new_v_ln1_g', 'new_v_ln1_b', 'new_v_w_ffn_gate', 'new_v_w_ffn_up', 'new_v_w_ffn_down', 'new_v_ln2_g', 'new_v_ln2_b']
TWIN_LEAF_KINDS = {'loss': 'loss', 'grad_x': 'grad_x', 'grad_w_ada': 'grad_w', 'grad_b_ada': 'grad_w', 'grad_w_in': 'grad_w', 'grad_b_gate': 'grad_w', 'grad_b_forget': 'grad_w', 'grad_w_sb_out': 'grad_w', 'grad_w_fox_out': 'grad_w', 'grad_w_o': 'grad_w', 'grad_ln1_g': 'grad_w', 'grad_ln1_b': 'grad_w', 'grad_w_ffn_gate': 'grad_w', 'grad_w_ffn_up': 'grad_w', 'grad_w_ffn_down': 'grad_w', 'grad_ln2_g': 'grad_w', 'grad_ln2_b': 'grad_w', 'delta_w_ada': 'delta_w', 'delta_b_ada': 'delta_w', 'delta_w_in': 'delta_w', 'delta_b_gate': 'delta_w', 'delta_b_forget': 'delta_w', 'delta_w_sb_out': 'delta_w', 'delta_w_fox_out': 'delta_w', 'delta_w_o': 'delta_w', 'delta_ln1_g': 'delta_w', 'delta_ln1_b': 'delta_w', 'delta_w_ffn_gate': 'delta_w', 'delta_w_ffn_up': 'delta_w', 'delta_w_ffn_down': 'delta_w', 'delta_ln2_g': 'delta_w', 'delta_ln2_b': 'delta_w', 'new_m_w_ada': 'new_m', 'new_m_b_ada': 'new_m', 'new_m_w_in': 'new_m', 'new_m_b_gate': 'new_m', 'new_m_b_forget': 'new_m', 'new_m_w_sb_out': 'new_m', 'new_m_w_fox_out': 'new_m', 'new_m_w_o': 'new_m', 'new_m_ln1_g': 'new_m', 'new_m_ln1_b': 'new_m', 'new_m_w_ffn_gate': 'new_m', 'new_m_w_ffn_up': 'new_m', 'new_m_w_ffn_down': 'new_m', 'new_m_ln2_g': 'new_m', 'new_m_ln2_b': 'new_m', 'new_v_w_ada': 'new_v', 'new_v_b_ada': 'new_v', 'new_v_w_in': 'new_v', 'new_v_b_gate': 'new_v', 'new_v_b_forget': 'new_v', 'new_v_w_sb_out': 'new_v', 'new_v_w_fox_out': 'new_v', 'new_v_w_o': 'new_v', 'new_v_ln1_g': 'new_v', 'new_v_ln1_b': 'new_v', 'new_v_w_ffn_gate': 'new_v', 'new_v_w_ffn_up': 'new_v', 'new_v_w_ffn_down': 'new_v', 'new_v_ln2_g': 'new_v', 'new_v_ln2_b': 'new_v'}


def _forward(args):
    return _fwd_reference(*[args[k] for k in FWD_PARAMS])


def _output_shape():
    def fwd():
        inp = _fwd_setup_inputs(0)
        return _fwd_reference(*[inp[k] for k in FWD_PARAMS])
    out = _jax.eval_shape(fwd)
    return out.shape, out.dtype

N_MICROBATCH = 1
ADAM_LR = 0.001
ADAM_B1 = 0.9
ADAM_B2 = 0.999
ADAM_EPS = 1e-08
ADAM_WD = 0.01
ADAM_STEP = 10
PER_EXAMPLE_BATCH_AXIS = {'x': 0, 'c': 0, 'loss_target': 0}
SHARED_INPUTS = []
_WEIGHT_DTYPES = {'w_ada': _jnp.float32, 'b_ada': _jnp.float32, 'w_in': _jnp.float32, 'b_gate': _jnp.float32, 'b_forget': _jnp.float32, 'w_sb_out': _jnp.float32, 'w_fox_out': _jnp.float32, 'w_o': _jnp.float32, 'ln1_g': _jnp.float32, 'ln1_b': _jnp.float32, 'w_ffn_gate': _jnp.float32, 'w_ffn_up': _jnp.float32, 'w_ffn_down': _jnp.float32, 'ln2_g': _jnp.float32, 'ln2_b': _jnp.float32}
MOMENT_SCALE = {'w_ada': 2.317230e-02, 'b_ada': 4.123911e-02, 'w_in': 7.967194e-03, 'b_gate': 3.658549e-03, 'b_forget': 2.741867e-02, 'w_sb_out': 1.174083e-02, 'w_fox_out': 7.300575e-03, 'w_o': 2.319884e-02, 'ln1_g': 1.196925e+00, 'ln1_b': 5.682200e-01, 'w_ffn_gate': 1.207636e-02, 'w_ffn_up': 1.175388e-02, 'w_ffn_down': 3.268323e-02, 'ln2_g': 3.200025e+01, 'ln2_b': 8.633620e-01}


def _to_microbatches(a, axis):
    t = _jnp.moveaxis(a, axis, 0)
    t = t.reshape((N_MICROBATCH, t.shape[0] // N_MICROBATCH) + t.shape[1:])
    return _jnp.moveaxis(t, 1, axis + 1)


def setup_inputs(seed: int = 0) -> dict:
    inp = _fwd_setup_inputs(seed)
    key = _jax.random.fold_in(_jax.random.key(seed), 7919)
    shape, _ = _output_shape()
    out = dict(inp)
    out["loss_target"] = _jax.random.normal(_jax.random.fold_in(key, 0), shape, _jnp.float32)
    for i, name in enumerate(TWIN_WEIGHTS):
        w = inp[name].astype(_jnp.float32)
        if MOMENT_SCALE is None:
            s = _jnp.sqrt(_jnp.mean(_jnp.square(w)) + 1e-30)
        else:
            s = MOMENT_SCALE[name]
        km, kv = _jax.random.split(_jax.random.fold_in(key, i + 1))
        out[name] = w
        out["m_" + name] = s * _jax.random.normal(km, w.shape, _jnp.float32)
        out["v_" + name] = (s * s) * _jax.random.uniform(kv, w.shape, _jnp.float32, 0.5, 1.5)
    if N_MICROBATCH > 1:
        for name, axis in PER_EXAMPLE_BATCH_AXIS.items():
            out[name] = _to_microbatches(out[name], axis)
    return {'x': out['x'], 'c': out['c'], 'w_ada': out['w_ada'], 'b_ada': out['b_ada'], 'w_in': out['w_in'], 'b_gate': out['b_gate'], 'b_forget': out['b_forget'], 'w_sb_out': out['w_sb_out'], 'w_fox_out': out['w_fox_out'], 'w_o': out['w_o'], 'ln1_g': out['ln1_g'], 'ln1_b': out['ln1_b'], 'w_ffn_gate': out['w_ffn_gate'], 'w_ffn_up': out['w_ffn_up'], 'w_ffn_down': out['w_ffn_down'], 'ln2_g': out['ln2_g'], 'ln2_b': out['ln2_b'], 'loss_target': out['loss_target'], 'm_w_ada': out['m_w_ada'], 'm_b_ada': out['m_b_ada'], 'm_w_in': out['m_w_in'], 'm_b_gate': out['m_b_gate'], 'm_b_forget': out['m_b_forget'], 'm_w_sb_out': out['m_w_sb_out'], 'm_w_fox_out': out['m_w_fox_out'], 'm_w_o': out['m_w_o'], 'm_ln1_g': out['m_ln1_g'], 'm_ln1_b': out['m_ln1_b'], 'm_w_ffn_gate': out['m_w_ffn_gate'], 'm_w_ffn_up': out['m_w_ffn_up'], 'm_w_ffn_down': out['m_w_ffn_down'], 'm_ln2_g': out['m_ln2_g'], 'm_ln2_b': out['m_ln2_b'], 'v_w_ada': out['v_w_ada'], 'v_b_ada': out['v_b_ada'], 'v_w_in': out['v_w_in'], 'v_b_gate': out['v_b_gate'], 'v_b_forget': out['v_b_forget'], 'v_w_sb_out': out['v_w_sb_out'], 'v_w_fox_out': out['v_w_fox_out'], 'v_w_o': out['v_w_o'], 'v_ln1_g': out['v_ln1_g'], 'v_ln1_b': out['v_ln1_b'], 'v_w_ffn_gate': out['v_w_ffn_gate'], 'v_w_ffn_up': out['v_w_ffn_up'], 'v_w_ffn_down': out['v_w_ffn_down'], 'v_ln2_g': out['v_ln2_g'], 'v_ln2_b': out['v_ln2_b']}


def _loss(weights, diff, rest, loss_target):
    with _jax.named_scope("forward"):
        args = {**rest, TWIN_DIFF_INPUT: diff, **{k: w.astype(_WEIGHT_DTYPES[k]) for k, w in weights.items()}}
        y = _forward(args)
    with _jax.named_scope("loss_head"):
        err = _jnp.square(y.astype(_jnp.float32) - loss_target)
        return 0.5 * _jnp.sum(_jnp.mean(err, axis=-1)) if err.ndim else 0.5 * err


def _adamw(w, g, m, v):
    m = ADAM_B1 * m + (1.0 - ADAM_B1) * g
    v = ADAM_B2 * v + (1.0 - ADAM_B2) * _jnp.square(g)
    m_hat = m / (1.0 - ADAM_B1 ** ADAM_STEP)
    v_hat = v / (1.0 - ADAM_B2 ** ADAM_STEP)
    delta = -ADAM_LR * (m_hat / (_jnp.sqrt(v_hat) + ADAM_EPS) + ADAM_WD * w)
    return delta, m, v


def reference(x, c, w_ada, b_ada, w_in, b_gate, b_forget, w_sb_out, w_fox_out, w_o, ln1_g, ln1_b, w_ffn_gate, w_ffn_up, w_ffn_down, ln2_g, ln2_b, loss_target, m_w_ada, m_b_ada, m_w_in, m_b_gate, m_b_forget, m_w_sb_out, m_w_fox_out, m_w_o, m_ln1_g, m_ln1_b, m_w_ffn_gate, m_w_ffn_up, m_w_ffn_down, m_ln2_g, m_ln2_b, v_w_ada, v_b_ada, v_w_in, v_b_gate, v_b_forget, v_w_sb_out, v_w_fox_out, v_w_o, v_ln1_g, v_ln1_b, v_w_ffn_gate, v_w_ffn_up, v_w_ffn_down, v_ln2_g, v_ln2_b):
    given = dict(x=x, c=c, w_ada=w_ada, b_ada=b_ada, w_in=w_in, b_gate=b_gate, b_forget=b_forget, w_sb_out=w_sb_out, w_fox_out=w_fox_out, w_o=w_o, ln1_g=ln1_g, ln1_b=ln1_b, w_ffn_gate=w_ffn_gate, w_ffn_up=w_ffn_up, w_ffn_down=w_ffn_down, ln2_g=ln2_g, ln2_b=ln2_b, loss_target=loss_target, m_w_ada=m_w_ada, m_b_ada=m_b_ada, m_w_in=m_w_in, m_b_gate=m_b_gate, m_b_forget=m_b_forget, m_w_sb_out=m_w_sb_out, m_w_fox_out=m_w_fox_out, m_w_o=m_w_o, m_ln1_g=m_ln1_g, m_ln1_b=m_ln1_b, m_w_ffn_gate=m_w_ffn_gate, m_w_ffn_up=m_w_ffn_up, m_w_ffn_down=m_w_ffn_down, m_ln2_g=m_ln2_g, m_ln2_b=m_ln2_b, v_w_ada=v_w_ada, v_b_ada=v_b_ada, v_w_in=v_w_in, v_b_gate=v_b_gate, v_b_forget=v_b_forget, v_w_sb_out=v_w_sb_out, v_w_fox_out=v_w_fox_out, v_w_o=v_w_o, v_ln1_g=v_ln1_g, v_ln1_b=v_ln1_b, v_w_ffn_gate=v_w_ffn_gate, v_w_ffn_up=v_w_ffn_up, v_w_ffn_down=v_w_ffn_down, v_ln2_g=v_ln2_g, v_ln2_b=v_ln2_b)
    weights = {n: given[n] for n in TWIN_WEIGHTS}
    shared = {n: given[n] for n in SHARED_INPUTS}
    per_example = {n: given[n] for n in ['x', 'c']}
    grad_fn = _jax.value_and_grad(_loss, argnums=(0, 1))

    def one_microbatch(ex, loss_target):
        ex = dict(ex)
        diff = ex.pop(TWIN_DIFF_INPUT)
        return grad_fn(weights, diff, {**shared, **ex}, loss_target)

    if N_MICROBATCH == 1:
        loss, (grad_w, grad_x) = one_microbatch(per_example, given["loss_target"])
    else:
        def body(carry, xs):
            loss_sum, grad_sum = carry
            l_k, (gw_k, gx_k) = one_microbatch(xs[0], xs[1])
            with _jax.named_scope("update"):
                return (loss_sum + l_k, _jax.tree.map(_jnp.add, grad_sum, gw_k)), gx_k

        init = (_jnp.zeros((), _jnp.float32), _jax.tree.map(_jnp.zeros_like, weights))
        (loss, grad_w), grad_x = _jax.lax.scan(body, init, (per_example, given["loss_target"]))
    with _jax.named_scope("update"):
        delta_w, new_m, new_v = {}, {}, {}
        for n in TWIN_WEIGHTS:
            delta_w[n], new_m[n], new_v[n] = _adamw(weights[n], grad_w[n], given["m_" + n], given["v_" + n])
    return (loss, grad_x, *[grad_w[n] for n in TWIN_WEIGHTS], *[delta_w[n] for n in TWIN_WEIGHTS],
            *[new_m[n] for n in TWIN_WEIGHTS], *[new_v[n] for n in TWIN_WEIGHTS])
```

```python
import functools

import jax
import jax.numpy as jnp
import numpy as np
from jax import lax
from jax.experimental import pallas as pl
from jax.experimental.pallas import tpu as pltpu

F32 = jnp.float32
BF16 = jnp.bfloat16

HEAD_DIM = 64
PAIR = 2 * HEAD_DIM
LN_EPS = 1e-5
ALPHA = 2.0 ** 0.25
ADAM_LR, ADAM_B1, ADAM_B2, ADAM_EPS, ADAM_WD, ADAM_STEP = 0.001, 0.9, 0.999, 1e-08, 0.01, 10
N_DEV = 8
VMEM_LIMIT = 56 * 1024 * 1024
MESH = pl.DeviceIdType.MESH


def _dot(a, b, ca=1, cb=0):
    return lax.dot_general(a, b, (((ca,), (cb,)), ((), ())), preferred_element_type=F32)


def _pick(n, cands):
    for t in cands:
        if n % t == 0:
            return t
    return n


def _params(sem):
    return pltpu.CompilerParams(dimension_semantics=sem, vmem_limit_bytes=VMEM_LIMIT)


def _mm(pairs, mode, out_dtype, name, bias=None, act=None, silu_a=False,
        tm_c=(512, 256, 128), tn_c=(512, 384, 256, 128)):
    norm = []
    for p in pairs:
        a, b = p[0], p[1]
        kdim_a = a.shape[0] if mode == 'tn' else a.shape[1]
        K, ka, kb = (p[2], p[3], p[4]) if len(p) > 2 else (kdim_a, 0, 0)
        norm.append((a, b, K, ka, kb))
    a0, b0 = norm[0][0], norm[0][1]
    M = a0.shape[1] if mode == 'tn' else a0.shape[0]
    N = b0.shape[0] if mode == 'nt' else b0.shape[1]
    tm, tn = _pick(M, tm_c), _pick(N, tn_c)
    n_pairs = len(norm)

    in_specs, args = [], []
    for a, b, K, ka, kb in norm:
        if mode == 'tn':
            in_specs.append(pl.BlockSpec((K, tm), lambda i, j, ka=ka: (ka, i)))
        else:
            in_specs.append(pl.BlockSpec((tm, K), lambda i, j, ka=ka: (i, ka)))
        if mode == 'nt':
            in_specs.append(pl.BlockSpec((tn, K), lambda i, j, kb=kb: (j, kb)))
        else:
            in_specs.append(pl.BlockSpec((K, tn), lambda i, j, kb=kb: (kb, j)))
        args += [a, b]
    if bias is not None:
        in_specs.append(pl.BlockSpec((1, tn), lambda i, j: (0, j)))
        args.append(bias)

    ca = 0 if mode == 'tn' else 1
    cb = 1 if mode == 'nt' else 0

    def body(*refs):
        o_ref = refs[-1]
        acc = None
        for p in range(n_pairs):
            av = refs[2 * p][...]
            if silu_a:
                av = av / (1.0 + jnp.exp(-av))
            d = _dot(av.astype(BF16), refs[2 * p + 1][...].astype(BF16), ca, cb)
            acc = d if acc is None else acc + d
        if bias is not None:
            acc = acc + refs[2 * n_pairs][...]
        if act == 'sigmoid':
            acc = 1.0 / (1.0 + jnp.exp(-acc))
        o_ref[...] = acc.astype(out_dtype)

    return pl.pallas_call(
        body, name=name, grid=(M // tm, N // tn), in_specs=in_specs,
        out_specs=pl.BlockSpec((tm, tn), lambda i, j: (i, j)),
        out_shape=jax.ShapeDtypeStruct((M, N), out_dtype),
        compiler_params=_params(("parallel", "parallel")),
    )(*args)


def _rows_call(body, name, row_ins, vec_ins, row_outs, acc_outs, ts):
    S = row_ins[0].shape[0]
    in_specs = [pl.BlockSpec((ts, a.shape[1]), lambda i: (i, 0)) for a in row_ins]
    in_specs += [pl.BlockSpec(a.shape, lambda i: (0, 0)) for a in vec_ins]
    out_specs = [pl.BlockSpec((ts, c), lambda i: (i, 0)) for c, _ in row_outs]
    out_specs += [pl.BlockSpec(s, lambda i: (0, 0)) for s in acc_outs]
    out_shape = [jax.ShapeDtypeStruct((S, c), dt) for c, dt in row_outs]
    out_shape += [jax.ShapeDtypeStruct(s, F32) for s in acc_outs]
    return pl.pallas_call(
        body, name=name, grid=(S // ts,), in_specs=in_specs, out_specs=out_specs, out_shape=out_shape,
        compiler_params=_params(("arbitrary",)),
    )(*row_ins, *vec_ins)


def _ln_stats(v):
    mu = jnp.mean(v, axis=-1, keepdims=True)
    d = v - mu
    var = jnp.mean(d * d, axis=-1, keepdims=True)
    rstd = lax.rsqrt(var + LN_EPS)
    return d * rstd, rstd


def _ln_bwd(dxhat, xhat, rstd):
    m1 = jnp.mean(dxhat, axis=-1, keepdims=True)
    m2 = jnp.mean(dxhat * xhat, axis=-1, keepdims=True)
    return rstd * (dxhat - m1 - xhat * m2)


def _colsum(v):
    return jnp.sum(v, axis=0, keepdims=True)


def _ln_mod(x, ada8, ts):
    D = x.shape[1]

    def body(x_ref, v_ref, u_ref):
        xhat, _ = _ln_stats(x_ref[...])
        u_ref[...] = (xhat * (1.0 + v_ref[1:2, :]) + v_ref[0:1, :]).astype(BF16)

    return _rows_call(body, "ln_mod", [x], [ada8], [(D, BF16)], [], ts)[0]


def _gate_mix(g_sb, g_fx, y_sb, y_fx, ts):
    D = y_sb.shape[1]

    def body(gs, gf, ys, yf, o_ref):
        o_ref[...] = (gs[...] * ys[...] + gf[...] * yf[...]).astype(BF16)

    return _rows_call(body, "gate_mix", [g_sb, g_fx, y_sb, y_fx], [], [(D, BF16)], [], ts)[0]


def _post_attn(x, mix, ada8, lnp8, ts):
    D = x.shape[1]

    def body(x_ref, mix_ref, v_ref, p_ref, x1_ref, u2_ref):
        r1 = ALPHA * x_ref[...] + v_ref[2:3, :] * mix_ref[...]
        xhat, _ = _ln_stats(r1)
        x1 = xhat * p_ref[0:1, :] + p_ref[1:2, :]
        x1_ref[...] = x1
        xh1, _ = _ln_stats(x1)
        u2_ref[...] = (xh1 * (1.0 + v_ref[4:5, :]) + v_ref[3:4, :]).astype(BF16)

    return _rows_call(body, "post_attn", [x, mix], [ada8, lnp8], [(D, F32), (D, BF16)], [], ts)


def _loss_head(x1, h, target, ada8, lnp8, ts):
    D = x1.shape[1]

    def body(x1_ref, h_ref, t_ref, v_ref, p_ref, dr2_ref, dh_ref, st_ref):
        @pl.when(pl.program_id(0) == 0)
        def _():
            st_ref[...] = jnp.zeros_like(st_ref)

        hv = h_ref[...]
        g2 = v_ref[5:6, :]
        r2 = ALPHA * x1_ref[...] + g2 * hv
        xhat, rstd = _ln_stats(r2)
        y = xhat * p_ref[2:3, :] + p_ref[3:4, :]
        err = y - t_ref[...]
        dy = err * (1.0 / D)
        dr2 = _ln_bwd(dy * p_ref[2:3, :], xhat, rstd)
        dr2_ref[...] = dr2
        dh_ref[...] = (dr2 * g2).astype(BF16)
        st_ref[0:1, :] += _colsum(err * err) * (0.5 / D)
        st_ref[1:2, :] += _colsum(dy * xhat)
        st_ref[2:3, :] += _colsum(dy)
        st_ref[3:4, :] += _colsum(dr2 * hv)

    return _rows_call(body, "loss_head", [x1, h, target], [ada8, lnp8], [(D, F32), (D, BF16)], [(8, D)], ts)


def _mid_bwd(du2, x1, dr2, mix, x, ada8, lnp8, ts):
    D = x.shape[1]

    def body(du2_ref, x1_ref, dr2_ref, mix_ref, x_ref, v_ref, p_ref, dr1_ref, dmix_ref, st_ref):
        @pl.when(pl.program_id(0) == 0)
        def _():
            st_ref[...] = jnp.zeros_like(st_ref)

        du2v = du2_ref[...]
        xh1, rstd1 = _ln_stats(x1_ref[...])
        dx1 = ALPHA * dr2_ref[...] + _ln_bwd(du2v * (1.0 + v_ref[4:5, :]), xh1, rstd1)
        mixv = mix_ref[...]
        g1 = v_ref[2:3, :]
        r1 = ALPHA * x_ref[...] + g1 * mixv
        xhr, rstdr = _ln_stats(r1)
        dr1 = _ln_bwd(dx1 * p_ref[0:1, :], xhr, rstdr)
        dr1_ref[...] = dr1
        dmix_ref[...] = (dr1 * g1).astype(BF16)
        st_ref[0:1, :] += _colsum(du2v * xh1)
        st_ref[1:2, :] += _colsum(du2v)
        st_ref[2:3, :] += _colsum(dx1 * xhr)
        st_ref[3:4, :] += _colsum(dx1)
        st_ref[4:5, :] += _colsum(dr1 * mixv)

    return _rows_call(body, "mid_bwd", [du2, x1, dr2, mix, x], [ada8, lnp8], [(D, F32), (D, BF16)], [(8, D)], ts)


def _first_bwd(du1, x, dr1, ada8, ts):
    D = x.shape[1]

    def body(du1_ref, x_ref, dr1_ref, v_ref, gx_ref, st_ref):
        @pl.when(pl.program_id(0) == 0)
        def _():
            st_ref[...] = jnp.zeros_like(st_ref)

        du1v = du1_ref[...]
        xh0, rstd0 = _ln_stats(x_ref[...])
        gx_ref[...] = ALPHA * dr1_ref[...] + _ln_bwd(du1v * (1.0 + v_ref[1:2, :]), xh0, rstd0)
        st_ref[0:1, :] += _colsum(du1v * xh0)
        st_ref[1:2, :] += _colsum(du1v)

    return _rows_call(body, "first_bwd", [du1, x, dr1], [ada8], [(D, F32)], [(8, D)], ts)


def _split3(v):
    hi = v.astype(BF16)
    r = v - hi.astype(F32)
    mid = r.astype(BF16)
    lo = (r - mid.astype(F32)).astype(BF16)
    return hi, mid, lo


def _fgate_fwd(f, bf_pad, tb):
    S = f.shape[0]

    def body(f_ref, b_ref, fc_ref, carry):
        @pl.when(pl.program_id(0) == 0)
        def _():
            carry[...] = jnp.zeros_like(carry)

        z = f_ref[...] + b_ref[...]
        ls = jnp.minimum(z, 0.0) - jnp.log(1.0 + jnp.exp(-jnp.abs(z)))
        r = lax.broadcasted_iota(jnp.int32, (tb, tb), 0)
        c = lax.broadcasted_iota(jnp.int32, (tb, tb), 1)
        tri = (c <= r).astype(BF16)
        hi, mid, lo = _split3(ls)
        cs = _dot(tri, hi) + _dot(tri, mid) + _dot(tri, lo) + carry[...]
        fc_ref[...] = cs
        carry[...] = cs[tb - 1:tb, :]

    return pl.pallas_call(
        body, name="fgate_fwd", grid=(S // tb,),
        in_specs=[pl.BlockSpec((tb, 128), lambda i: (i, 0)), pl.BlockSpec((1, 128), lambda i: (0, 0))],
        out_specs=pl.BlockSpec((tb, 128), lambda i: (i, 0)),
        out_shape=jax.ShapeDtypeStruct((S, 128), F32),
        scratch_shapes=[pltpu.VMEM((1, 128), F32)],
        compiler_params=_params(("arbitrary",)),
    )(f, bf_pad)


def _fgate_bwd(dfc, f, bf_pad, tb):
    S = f.shape[0]
    nb = S // tb

    def body(d_ref, f_ref, b_ref, df_ref, gb_ref, carry):
        @pl.when(pl.program_id(0) == 0)
        def _():
            carry[...] = jnp.zeros_like(carry)
            gb_ref[...] = jnp.zeros_like(gb_ref)

        r = lax.broadcasted_iota(jnp.int32, (tb, tb), 0)
        c = lax.broadcasted_iota(jnp.int32, (tb, tb), 1)
        tri = (c >= r).astype(BF16)
        hi, mid, lo = _split3(d_ref[...])
        rs = _dot(tri, hi) + _dot(tri, mid) + _dot(tri, lo) + carry[...]
        carry[...] = rs[0:1, :]
        z = f_ref[...] + b_ref[...]
        df = rs * (1.0 / (1.0 + jnp.exp(z)))
        df_ref[...] = df
        gb_ref[0:1, :] += _colsum(df)

    return pl.pallas_call(
        body, name="fgate_bwd", grid=(nb,),
        in_specs=[pl.BlockSpec((tb, 128), lambda i: (nb - 1 - i, 0)),
                  pl.BlockSpec((tb, 128), lambda i: (nb - 1 - i, 0)),
                  pl.BlockSpec((1, 128), lambda i: (0, 0))],
        out_specs=[pl.BlockSpec((tb, 128), lambda i: (nb - 1 - i, 0)), pl.BlockSpec((8, 128), lambda i: (0, 0))],
        out_shape=[jax.ShapeDtypeStruct((S, 128), F32), jax.ShapeDtypeStruct((8, 128), F32)],
        scratch_shapes=[pltpu.VMEM((1, 128), F32)],
        compiler_params=_params(("arbitrary",)),
    )(dfc, f, bf_pad)


def _split2(v):
    hi = v.astype(BF16)
    lo = (v - hi.astype(F32)).astype(BF16)
    return hi, lo


def _head_masks():
    lane = lax.broadcasted_iota(jnp.int32, (1, PAIR), 1)
    m0 = lane < HEAD_DIM
    return m0, jnp.logical_not(m0)


def _sel(mask, v):
    return jnp.where(mask, v, jnp.zeros_like(v))


def _softplus(z):
    return jnp.maximum(z, 0.0) + jnp.log(1.0 + jnp.exp(-jnp.abs(z)))


def _qkv_specs(S, tq, n_pairs, base):
    return [pl.BlockSpec((tq, PAIR), lambda p, i: (i, base + p)),
            pl.BlockSpec((S, PAIR), lambda p, i: (0, base + n_pairs + p)),
            pl.BlockSpec((S, PAIR), lambda p, i: (0, base + 2 * n_pairs + p))]


def _sb_fwd(qkv, n_pairs, base, tq):
    S = qkv.shape[0]
    tk = tq
    scale = HEAD_DIM ** -0.5

    def body(q_ref, k_ref, v_ref, o_ref, t_ref, acc_ref):
        i = pl.program_id(1)
        masks = _head_masks()
        q2 = q_ref[...]
        qm = [_sel(m, q2) for m in masks]
        rowpos = i * tq + lax.broadcasted_iota(jnp.int32, (tq, tk), 0)
        colin = lax.broadcasted_iota(jnp.int32, (tq, tk), 1)
        upper = (lax.broadcasted_iota(jnp.int32, (tk, tk), 0) > lax.broadcasted_iota(jnp.int32, (tk, tk), 1)).astype(BF16)
        acc_ref[...] = jnp.zeros_like(acc_ref)

        def step(jj, carry):
            j = i - jj
            off = pl.multiple_of(j * tk, tk)
            k2 = k_ref[pl.ds(off, tk), :]
            v2 = v_ref[pl.ds(off, tk), :]
            mask = (j * tk + colin) < rowpos
            out = None
            new = []
            for h in range(2):
                z = _dot(qm[h], k2, 1, 1) * scale
                sp = _softplus(z)
                lg = jnp.where(mask, -sp, 0.0)
                hi, lo = _split2(lg)
                suf = _dot(hi, upper) + _dot(lo, upper)
                a = jnp.where(mask, jnp.exp(z - sp + suf + carry[h]), 0.0)
                d = _dot(a.astype(BF16), _sel(masks[h], v2))
                out = d if out is None else out + d
                new.append(carry[h] + jnp.sum(lg, axis=-1, keepdims=True))
            acc_ref[...] += out
            return tuple(new)

        zero = jnp.zeros((tq, 1), F32)
        r0, r1 = lax.fori_loop(0, i + 1, step, (zero, zero))
        o_ref[...] = acc_ref[...].astype(BF16)
        t_ref[...] = jnp.where(masks[0], r0, r1)

    W = n_pairs * PAIR
    return pl.pallas_call(
        body, name="sb_fwd", grid=(n_pairs, S // tq), in_specs=_qkv_specs(S, tq, n_pairs, base),
        out_specs=[pl.BlockSpec((tq, PAIR), lambda p, i: (i, p)), pl.BlockSpec((tq, PAIR), lambda p, i: (i, p))],
        out_shape=[jax.ShapeDtypeStruct((S, W), BF16), jax.ShapeDtypeStruct((S, W), F32)],
        scratch_shapes=[pltpu.VMEM((tq, PAIR), F32)],
        compiler_params=_params(("parallel", "arbitrary")),
    )(qkv, qkv, qkv)


def _sb_bwd(qkv, tot, do, n_pairs, base, tq):
    S = qkv.shape[0]
    tk = tq
    nq = S // tq
    scale = HEAD_DIM ** -0.5

    def body(q_ref, k_ref, v_ref, t_ref, do_ref, dq_ref, dk_ref, dv_ref, dq_acc, dk_acc, dv_acc):
        i = pl.program_id(1)
        masks = _head_masks()

        @pl.when(i == 0)
        def _():
            dk_acc[...] = jnp.zeros_like(dk_acc)
            dv_acc[...] = jnp.zeros_like(dv_acc)

        q2 = q_ref[...]
        do2 = do_ref[...]
        qm = [_sel(m, q2) for m in masks]
        dom = [_sel(m, do2) for m in masks]
        t2 = t_ref[...]
        tot_h = [t2[:, 0:1], t2[:, HEAD_DIM:HEAD_DIM + 1]]
        rowpos = i * tq + lax.broadcasted_iota(jnp.int32, (tq, tk), 0)
        colin = lax.broadcasted_iota(jnp.int32, (tq, tk), 1)
        r_i = lax.broadcasted_iota(jnp.int32, (tk, tk), 0)
        c_i = lax.broadcasted_iota(jnp.int32, (tk, tk), 1)
        upper = (r_i > c_i).astype(BF16)
        lower = (r_i < c_i).astype(BF16)
        dq_acc[...] = jnp.zeros_like(dq_acc)

        def step(j, carry):
            off = pl.multiple_of(j * tk, tk)
            k2 = k_ref[pl.ds(off, tk), :]
            v2 = v_ref[pl.ds(off, tk), :]
            mask = (j * tk + colin) < rowpos
            dq = None
            dk = None
            dv = None
            new = []
            for h in range(2):
                cum_l, cum_g = carry[2 * h], carry[2 * h + 1]
                z = _dot(qm[h], k2, 1, 1) * scale
                sp = _softplus(z)
                lg = jnp.where(mask, -sp, 0.0)
                hi, lo = _split2(lg)
                suf = _dot(hi, upper) + _dot(lo, upper)
                row_l = jnp.sum(lg, axis=-1, keepdims=True)
                later = tot_h[h] - cum_l - row_l
                a = jnp.where(mask, jnp.exp(z - sp + suf + later), 0.0)
                da = _dot(dom[h], v2, 1, 1)
                g = da * a
                ghi, glo = _split2(g)
                pre = _dot(ghi, lower) + _dot(glo, lower) + cum_g
                one_m_beta = jnp.exp(-sp)
                dz = jnp.where(mask, g * one_m_beta - (1.0 - one_m_beta) * pre, 0.0)
                dzb = (dz * scale).astype(BF16)
                d1 = _dot(dzb, _sel(masks[h], k2))
                d2 = _dot(dzb, qm[h], 0, 0)
                d3 = _dot(a.astype(BF16), dom[h], 0, 0)
                dq = d1 if dq is None else dq + d1
                dk = d2 if dk is None else dk + d2
                dv = d3 if dv is None else dv + d3
                new += [cum_l + row_l, cum_g + jnp.sum(g, axis=-1, keepdims=True)]
            dq_acc[...] += dq
            dk_acc[pl.ds(off, tk), :] += dk
            dv_acc[pl.ds(off, tk), :] += dv
            return tuple(new)

        zero = jnp.zeros((tq, 1), F32)
        lax.fori_loop(0, i + 1, step, (zero, zero, zero, zero))
        dq_ref[...] = dq_acc[...].astype(BF16)

        @pl.when(i == nq - 1)
        def _():
            dk_ref[...] = dk_acc[...].astype(BF16)
            dv_ref[...] = dv_acc[...].astype(BF16)

    W = n_pairs * PAIR
    in_specs = _qkv_specs(S, tq, n_pairs, base) + [
        pl.BlockSpec((tq, PAIR), lambda p, i: (i, p)),
        pl.BlockSpec((tq, PAIR), lambda p, i: (i, p))]
    out_specs = [pl.BlockSpec((tq, PAIR), lambda p, i: (i, p)),
                 pl.BlockSpec((S, PAIR), lambda p, i: (0, p)),
                 pl.BlockSpec((S, PAIR), lambda p, i: (0, p))]
    dq, dk, dv = pl.pallas_call(
        body, name="sb_bwd", grid=(n_pairs, nq), in_specs=in_specs, out_specs=out_specs,
        out_shape=[jax.ShapeDtypeStruct((S, W), BF16)] * 3,
        scratch_shapes=[pltpu.VMEM((tq, PAIR), F32), pltpu.VMEM((S, PAIR), F32), pltpu.VMEM((S, PAIR), F32)],
        compiler_params=_params(("parallel", "arbitrary")),
    )(qkv, qkv, qkv, tot, do)
    return dq, dk, dv


NEG = -1e30


def _fox_specs(S, tq, n_pairs, base):
    return _qkv_specs(S, tq, n_pairs, base) + [
        pl.BlockSpec((tq, PAIR), lambda p, i: (i, p)),
        pl.BlockSpec((1, S // tq, 8, tq), lambda p, i: (p, 0, 0, 0))]


def _fox_fwd(qkv, fcx, fcr, n_pairs, base, tq):
    S = qkv.shape[0]
    tk = tq
    scale = HEAD_DIM ** -0.5

    def body(q_ref, k_ref, v_ref, fq_ref, fk_ref, o_ref, lse_ref, acc_ref):
        i = pl.program_id(1)
        masks = _head_masks()
        q2 = q_ref[...]
        qm = [_sel(m, q2) for m in masks]
        fq2 = fq_ref[...]
        fq = [fq2[:, 0:1], fq2[:, HEAD_DIM:HEAD_DIM + 1]]
        rowpos = i * tq + lax.broadcasted_iota(jnp.int32, (tq, tk), 0)
        colin = lax.broadcasted_iota(jnp.int32, (tq, tk), 1)
        acc_ref[...] = jnp.zeros_like(acc_ref)

        def step(j, carry):
            off = pl.multiple_of(j * tk, tk)
            k2 = k_ref[pl.ds(off, tk), :]
            v2 = v_ref[pl.ds(off, tk), :]
            fk2 = fk_ref[0, j]
            mask = (j * tk + colin) <= rowpos
            out = None
            new = []
            alphas = []
            for h in range(2):
                m_old, l_old = carry[2 * h], carry[2 * h + 1]
                s = _dot(qm[h], k2, 1, 1) * scale + fq[h] - fk2[h:h + 1, :]
                s = jnp.where(mask, s, NEG)
                m_new = jnp.maximum(m_old, jnp.max(s, axis=-1, keepdims=True))
                p = jnp.exp(s - m_new)
                alpha = jnp.exp(m_old - m_new)
                alphas.append(alpha)
                d = _dot(p.astype(BF16), _sel(masks[h], v2))
                out = d if out is None else out + d
                new += [m_new, alpha * l_old + jnp.sum(p, axis=-1, keepdims=True)]
            acc_ref[...] = acc_ref[...] * jnp.where(masks[0], alphas[0], alphas[1]) + out
            return tuple(new)

        zero = jnp.zeros((tq, 1), F32)
        neg = jnp.full((tq, 1), NEG, F32)
        m0, l0, m1, l1 = lax.fori_loop(0, i + 1, step, (neg, zero, neg, zero))
        o_ref[...] = (acc_ref[...] / jnp.where(masks[0], l0, l1)).astype(BF16)
        lse_ref[...] = jnp.where(masks[0], m0 + jnp.log(l0), m1 + jnp.log(l1))

    W = n_pairs * PAIR
    return pl.pallas_call(
        body, name="fox_fwd", grid=(n_pairs, S // tq), in_specs=_fox_specs(S, tq, n_pairs, base),
        out_specs=[pl.BlockSpec((tq, PAIR), lambda p, i: (i, p)), pl.BlockSpec((tq, PAIR), lambda p, i: (i, p))],
        out_shape=[jax.ShapeDtypeStruct((S, W), BF16), jax.ShapeDtypeStruct((S, W), F32)],
        scratch_shapes=[pltpu.VMEM((tq, PAIR), F32)],
        compiler_params=_params(("parallel", "arbitrary")),
    )(qkv, qkv, qkv, fcx, fcr)


def _fox_bwd(qkv, fcx, fcr, o, lse, do, n_pairs, base, tq):
    S = qkv.shape[0]
    tk = tq
    nq = S // tq
    scale = HEAD_DIM ** -0.5

    def body(q_ref, k_ref, v_ref, fq_ref, fk_ref, o_ref, lse_ref, do_ref,
             dq_ref, dk_ref, dv_ref, dfq_ref, dfk_ref, dq_acc, dk_acc, dv_acc):
        i = pl.program_id(1)
        masks = _head_masks()

        @pl.when(i == 0)
        def _():
            dk_acc[...] = jnp.zeros_like(dk_acc)
            dv_acc[...] = jnp.zeros_like(dv_acc)
            dfk_ref[...] = jnp.zeros_like(dfk_ref)

        q2 = q_ref[...]
        do2 = do_ref[...]
        qm = [_sel(m, q2) for m in masks]
        dom = [_sel(m, do2) for m in masks]
        fq2 = fq_ref[...]
        fq = [fq2[:, 0:1], fq2[:, HEAD_DIM:HEAD_DIM + 1]]
        l2 = lse_ref[...]
        lse_h = [l2[:, 0:1], l2[:, HEAD_DIM:HEAD_DIM + 1]]
        prod = do2.astype(F32) * o_ref[...].astype(F32)
        delta = [jnp.sum(jnp.where(m, prod, 0.0), axis=-1, keepdims=True) for m in masks]
        rowpos = i * tq + lax.broadcasted_iota(jnp.int32, (tq, tk), 0)
        colin = lax.broadcasted_iota(jnp.int32, (tq, tk), 1)
        dq_acc[...] = jnp.zeros_like(dq_acc)

        def step(j, carry):
            off = pl.multiple_of(j * tk, tk)
            k2 = k_ref[pl.ds(off, tk), :]
            v2 = v_ref[pl.ds(off, tk), :]
            fk2 = fk_ref[0, j]
            mask = (j * tk + colin) <= rowpos
            dq = None
            dk = None
            dv = None
            new = []
            dfk_rows = []
            for h in range(2):
                s = _dot(qm[h], k2, 1, 1) * scale + fq[h] - fk2[h:h + 1, :]
                p = jnp.where(mask, jnp.exp(s - lse_h[h]), 0.0)
                dp = _dot(dom[h], v2, 1, 1)
                ds = p * (dp - delta[h])
                dsb = (ds * scale).astype(BF16)
                d1 = _dot(dsb, _sel(masks[h], k2))
                d2 = _dot(dsb, qm[h], 0, 0)
                d3 = _dot(p.astype(BF16), dom[h], 0, 0)
                dq = d1 if dq is None else dq + d1
                dk = d2 if dk is None else dk + d2
                dv = d3 if dv is None else dv + d3
                new.append(carry[h] + jnp.sum(ds, axis=-1, keepdims=True))
                dfk_rows.append(jnp.sum(ds, axis=0, keepdims=True))
            dq_acc[...] += dq
            dk_acc[pl.ds(off, tk), :] += dk
            dv_acc[pl.ds(off, tk), :] += dv
            dfk_ref[0, j, 0:1, :] += dfk_rows[0]
            dfk_ref[0, j, 1:2, :] += dfk_rows[1]
            return tuple(new)

        zero = jnp.zeros((tq, 1), F32)
        r0, r1 = lax.fori_loop(0, i + 1, step, (zero, zero))
        dq_ref[...] = dq_acc[...].astype(BF16)
        dfq_ref[...] = jnp.where(masks[0], r0, r1)

        @pl.when(i == nq - 1)
        def _():
            dk_ref[...] = dk_acc[...].astype(BF16)
            dv_ref[...] = dv_acc[...].astype(BF16)

    W = n_pairs * PAIR
    in_specs = _fox_specs(S, tq, n_pairs, base) + [
        pl.BlockSpec((tq, PAIR), lambda p, i: (i, p)),
        pl.BlockSpec((tq, PAIR), lambda p, i: (i, p)),
        pl.BlockSpec((tq, PAIR), lambda p, i: (i, p))]
    out_specs = [pl.BlockSpec((tq, PAIR), lambda p, i: (i, p)),
                 pl.BlockSpec((S, PAIR), lambda p, i: (0, p)),
                 pl.BlockSpec((S, PAIR), lambda p, i: (0, p)),
                 pl.BlockSpec((tq, PAIR), lambda p, i: (i, p)),
                 pl.BlockSpec((1, nq, 8, tk), lambda p, i: (p, 0, 0, 0))]
    return pl.pallas_call(
        body, name="fox_bwd", grid=(n_pairs, nq), in_specs=in_specs, out_specs=out_specs,
        out_shape=[jax.ShapeDtypeStruct((S, W), BF16)] * 3
        + [jax.ShapeDtypeStruct((S, W), F32), jax.ShapeDtypeStruct((n_pairs, nq, 8, tk), F32)],
        scratch_shapes=[pltpu.VMEM((tq, PAIR), F32), pltpu.VMEM((S, PAIR), F32), pltpu.VMEM((S, PAIR), F32)],
        compiler_params=_params(("parallel", "arbitrary")),
    )(qkv, qkv, qkv, fcx, fcr, o, lse, do)


def _swiglu_fwd(u2, wg, wu):
    S, D = u2.shape
    FF = wg.shape[1]
    tm, tn = _pick(S, (512, 256, 128)), _pick(FF, (512, 384, 256, 128))

    def body(u_ref, g_ref, w_ref, a_ref, b_ref, h_ref):
        u = u_ref[...]
        a = _dot(u, g_ref[...])
        b = _dot(u, w_ref[...])
        a_ref[...] = a.astype(BF16)
        b_ref[...] = b.astype(BF16)
        h_ref[...] = (a / (1.0 + jnp.exp(-a)) * b).astype(BF16)

    spec_o = pl.BlockSpec((tm, tn), lambda i, j: (i, j))
    return pl.pallas_call(
        body, name="swiglu_fwd", grid=(S // tm, FF // tn),
        in_specs=[pl.BlockSpec((tm, D), lambda i, j: (i, 0)),
                  pl.BlockSpec((D, tn), lambda i, j: (0, j)),
                  pl.BlockSpec((D, tn), lambda i, j: (0, j))],
        out_specs=[spec_o] * 3, out_shape=[jax.ShapeDtypeStruct((S, FF), BF16)] * 3,
        compiler_params=_params(("parallel", "parallel")),
    )(u2, wg, wu)


def _swiglu_bwd(dh, wd, a, b):
    S, D = dh.shape
    FF = wd.shape[0]
    tm, tn = _pick(S, (512, 256, 128)), _pick(FF, (512, 384, 256, 128))

    def body(dh_ref, w_ref, a_ref, b_ref, da_ref, db_ref):
        dhin = _dot(dh_ref[...], w_ref[...], 1, 1)
        av = a_ref[...].astype(F32)
        bv = b_ref[...].astype(F32)
        sig = 1.0 / (1.0 + jnp.exp(-av))
        da_ref[...] = (dhin * bv * (sig * (1.0 + av * (1.0 - sig)))).astype(BF16)
        db_ref[...] = (dhin * (av * sig)).astype(BF16)

    spec_o = pl.BlockSpec((tm, tn), lambda i, j: (i, j))
    return pl.pallas_call(
        body, name="swiglu_bwd", grid=(S // tm, FF // tn),
        in_specs=[pl.BlockSpec((tm, D), lambda i, j: (i, 0)),
                  pl.BlockSpec((tn, D), lambda i, j: (j, 0)), spec_o, spec_o],
        out_specs=[spec_o] * 2, out_shape=[jax.ShapeDtypeStruct((S, FF), BF16)] * 2,
        compiler_params=_params(("parallel", "parallel")),
    )(dh, wd, a, b)


def _gate_bwd(dmix, wo, g_sb, g_fx, y_sb, y_fx):
    S, D = dmix.shape
    tm, tn = _pick(S, (512, 256, 128)), _pick(D, (512, 256, 128))

    def body(dm_ref, w_ref, gs_ref, gf_ref, ys_ref, yf_ref, dys_ref, dyf_ref, dls_ref, dlf_ref, bs_ref, bf_ref):
        @pl.when(pl.program_id(1) == 0)
        def _():
            bs_ref[...] = jnp.zeros_like(bs_ref)
            bf_ref[...] = jnp.zeros_like(bf_ref)

        dmi = _dot(dm_ref[...], w_ref[...], 1, 1)
        gs, gf = gs_ref[...], gf_ref[...]
        dys_ref[...] = (dmi * gs).astype(BF16)
        dyf_ref[...] = (dmi * gf).astype(BF16)
        dls = dmi * ys_ref[...] * gs * (1.0 - gs)
        dlf = dmi * yf_ref[...] * gf * (1.0 - gf)
        dls_ref[...] = dls.astype(BF16)
        dlf_ref[...] = dlf.astype(BF16)
        bs_ref[0:1, :] += _colsum(dls)
        bf_ref[0:1, :] += _colsum(dlf)

    t = pl.BlockSpec((tm, tn), lambda j, i: (i, j))
    accs = pl.BlockSpec((8, tn), lambda j, i: (0, j))
    return pl.pallas_call(
        body, name="gate_bwd", grid=(D // tn, S // tm),
        in_specs=[pl.BlockSpec((tm, D), lambda j, i: (i, 0)),
                  pl.BlockSpec((tn, D), lambda j, i: (j, 0)), t, t, t, t],
        out_specs=[t, t, t, t, accs, accs],
        out_shape=[jax.ShapeDtypeStruct((S, D), BF16)] * 4 + [jax.ShapeDtypeStruct((8, D), F32)] * 2,
        compiler_params=_params(("parallel", "arbitrary")),
    )(dmix, wo, g_sb, g_fx, y_sb, y_fx)


def _local_step(x, target, ada8, lnp8, bg_sb, bg_fx, bf_pad, wqkv, wf, wgs, wgf, wsb, wfx, wo, wfg, wfu, wfd):
    S, D = x.shape
    W = wsb.shape[0]
    n_pairs = W // PAIR
    n_heads = W // HEAD_DIM
    ts = _pick(S, (256, 128))
    tq = _pick(S, (256, 128))

    u1 = _ln_mod(x, ada8, ts)
    qkv = _mm([(u1, wqkv)], 'nn', BF16, "in_qkv")
    f = _mm([(u1, wf)], 'nn', F32, "in_f")
    g_sb = _mm([(u1, wgs)], 'nn', F32, "in_gsb", bias=bg_sb, act='sigmoid')
    g_fx = _mm([(u1, wgf)], 'nn', F32, "in_gfx", bias=bg_fx, act='sigmoid')
    fc = _fgate_fwd(f, bf_pad, _pick(S, (512, 256, 128)))
    fch = fc[:, :n_heads]
    fcx = jnp.repeat(fch, HEAD_DIM, axis=1)
    nq = S // tq
    fcr = jnp.pad(fch.T.reshape(n_pairs, 2, nq, tq).transpose(0, 2, 1, 3),
                  ((0, 0), (0, 0), (0, 6), (0, 0)))
    o_sb, tot = _sb_fwd(qkv, n_pairs, 0, tq)
    o_fx, lse = _fox_fwd(qkv, fcx, fcr, n_pairs, 3 * n_pairs, tq)
    y_sb = _mm([(o_sb, wsb)], 'nn', F32, "out_sb")
    y_fx = _mm([(o_fx, wfx)], 'nn', F32, "out_fx")
    mix_in = _gate_mix(g_sb, g_fx, y_sb, y_fx, ts)
    mix = _mm([(mix_in, wo)], 'nn', F32, "out_o")
    x1, u2 = _post_attn(x, mix, ada8, lnp8, ts)
    a, b, hin = _swiglu_fwd(u2, wfg, wfu)
    h = _mm([(hin, wfd)], 'nn', F32, "ffn_down")
    dr2, dh, st_loss = _loss_head(x1, h, target, ada8, lnp8, ts)

    da, db = _swiglu_bwd(dh, wfd, a, b)
    g_wfd = _mm([(hin, dh)], 'tn', F32, "g_ffn_down")
    du2 = _mm([(da, wfg), (db, wfu)], 'nt', F32, "d_u2")
    g_wfg = _mm([(u2, da)], 'tn', F32, "g_ffn_gate")
    g_wfu = _mm([(u2, db)], 'tn', F32, "g_ffn_up")
    dr1, dmix, st_mid = _mid_bwd(du2, x1, dr2, mix, x, ada8, lnp8, ts)
    dys, dyf, dls, dlf, gb_sb, gb_fx = _gate_bwd(dmix, wo, g_sb, g_fx, y_sb, y_fx)
    g_wo = _mm([(mix_in, dmix)], 'tn', F32, "g_w_o")
    do_sb = _mm([(dys, wsb)], 'nt', BF16, "d_o_sb")
    do_fx = _mm([(dyf, wfx)], 'nt', BF16, "d_o_fx")
    g_wsb = _mm([(o_sb, dys)], 'tn', F32, "g_sb_out")
    g_wfx = _mm([(o_fx, dyf)], 'tn', F32, "g_fox_out")
    dq_s, dk_s, dv_s = _sb_bwd(qkv, tot, do_sb, n_pairs, 0, tq)
    dq_f, dk_f, dv_f, dfq, dfk = _fox_bwd(qkv, fcx, fcr, o_fx, lse, do_fx, n_pairs, 3 * n_pairs, tq)
    dfc = dfq[:, ::HEAD_DIM] - dfk[:, :, :2, :].transpose(0, 2, 1, 3).reshape(n_heads, S).T
    dfc = jnp.pad(dfc, ((0, 0), (0, 128 - n_heads)))
    df, gb_f = _fgate_bwd(dfc, f, bf_pad, _pick(S, (512, 256, 128)))
    grads = [dq_s, dk_s, dv_s, dq_f, dk_f, dv_f]
    du1 = _mm([(g, wqkv, W, 0, n) for n, g in enumerate(grads)] + [(df, wf), (dls, wgs), (dlf, wgf)],
              'nt', F32, "d_u1")
    g_wqkv = [_mm([(u1, g)], 'tn', F32, "g_in_%d" % n) for n, g in enumerate(grads)]
    g_wf = _mm([(u1, df)], 'tn', F32, "g_in_f")
    g_wgs = _mm([(u1, dls)], 'tn', F32, "g_in_gsb")
    g_wgf = _mm([(u1, dlf)], 'tn', F32, "g_in_gfx")
    gx, st_first = _first_bwd(du1, x, dr1, ada8, ts)

    wgrads = dict(qkv=g_wqkv, f=g_wf, gs=g_wgs, gf=g_wgf, sb=g_wsb, fx=g_wfx, o=g_wo, fg=g_wfg, fu=g_wfu, fd=g_wfd)
    stats = dict(loss=st_loss, mid=st_mid, first=st_first, gb_sb=gb_sb, gb_fx=gb_fx, gb_f=gb_f)
    return gx, wgrads, stats


def _position():
    x, y, c = lax.axis_index("x"), lax.axis_index("y"), lax.axis_index("c")
    return x, y, c, 4 * x + 2 * y + c


def _flip(x, y, c, k):
    px = 1 - x if k & 4 else x
    py = 1 - y if k & 2 else y
    pc = 1 - c if k & 1 else c
    return (px, py, pc), 4 * px + 2 * py + pc


def _all_gather_small(v, name):
    r, n = v.shape

    def body(x_ref, out_ref, send_sems, recv_sems, local_sem):
        x, y, c, me = _position()
        mine = pltpu.make_async_copy(x_ref, out_ref.at[me], local_sem)
        mine.start()
        sends = []
        for k in range(1, N_DEV):
            peer, _ = _flip(x, y, c, k)
            cp = pltpu.make_async_remote_copy(
                src_ref=x_ref, dst_ref=out_ref.at[me], send_sem=send_sems.at[k - 1], recv_sem=recv_sems.at[k - 1],
                device_id=peer, device_id_type=MESH)
            cp.start()
            sends.append(cp)
        for k in range(1, N_DEV):
            peer, slot = _flip(x, y, c, k)
            pltpu.make_async_remote_copy(
                src_ref=x_ref, dst_ref=out_ref.at[slot], send_sem=send_sems.at[k - 1], recv_sem=recv_sems.at[k - 1],
                device_id=peer, device_id_type=MESH).wait_recv()
        for cp in sends:
            cp.wait_send()
        mine.wait()

    return pl.pallas_call(
        body, name=name, out_shape=jax.ShapeDtypeStruct((N_DEV, r, n), v.dtype),
        in_specs=[pl.BlockSpec(memory_space=pltpu.VMEM)], out_specs=pl.BlockSpec(memory_space=pltpu.VMEM),
        scratch_shapes=[pltpu.SemaphoreType.DMA((N_DEV - 1,)), pltpu.SemaphoreType.DMA((N_DEV - 1,)),
                        pltpu.SemaphoreType.DMA],
    )(v)


def _all_gather_weights(packed):
    R, C = packed.shape

    def body(x_ref, out_ref, send_sems, recv_sems, local_sem):
        x, y, c, me = _position()
        sibling, sib_slot = _flip(x, y, c, 1)
        mine = pltpu.make_async_copy(x_ref, out_ref.at[me], local_sem)
        mine.start()

        def copy(k, slot, to, src=None):
            return pltpu.make_async_remote_copy(
                src_ref=out_ref.at[slot] if src is None else src, dst_ref=out_ref.at[slot],
                send_sem=send_sems.at[k], recv_sem=recv_sems.at[k], device_id=to, device_id_type=MESH)

        first = [copy(0, me, sibling, src=x_ref)]
        chips = (4, 2, 6)
        for n, k in enumerate(chips):
            peer, _ = _flip(x, y, c, k)
            first.append(copy(1 + n, me, peer, src=x_ref))
        for cp in first:
            cp.start()
        passed = []
        for n, k in enumerate(chips):
            peer, slot = _flip(x, y, c, k)
            copy(1 + n, slot, peer).wait_recv()
            cp = copy(4 + n, slot, sibling)
            cp.start()
            passed.append(cp)
        copy(0, sib_slot, sibling).wait_recv()
        for n, k in enumerate(chips):
            _, slot = _flip(x, y, c, k | 1)
            copy(4 + n, slot, sibling).wait_recv()
        for cp in first + passed:
            cp.wait_send()
        mine.wait()

    return pl.pallas_call(
        body, name="all_gather_weights", out_shape=jax.ShapeDtypeStruct((N_DEV, R, C), packed.dtype),
        in_specs=[pl.BlockSpec(memory_space=pl.ANY)], out_specs=pl.BlockSpec(memory_space=pl.ANY),
        scratch_shapes=[pltpu.SemaphoreType.DMA((7,)), pltpu.SemaphoreType.DMA((7,)), pltpu.SemaphoreType.DMA],
    )(packed)


def _exchange_grads(gpack):
    _, R, C = gpack.shape

    def body(g_ref, out_ref, send_sems, recv_sems, local_sem):
        x, y, c, me = _position()
        mine = pltpu.make_async_copy(g_ref.at[me], out_ref.at[me], local_sem)
        mine.start()
        sends = []
        for k in range(1, N_DEV):
            peer, slot = _flip(x, y, c, k)
            cp = pltpu.make_async_remote_copy(
                src_ref=g_ref.at[slot], dst_ref=out_ref.at[me], send_sem=send_sems.at[k - 1],
                recv_sem=recv_sems.at[k - 1], device_id=peer, device_id_type=MESH)
            cp.start()
            sends.append(cp)
        for k in range(1, N_DEV):
            peer, slot = _flip(x, y, c, k)
            pltpu.make_async_remote_copy(
                src_ref=g_ref.at[slot], dst_ref=out_ref.at[slot], send_sem=send_sems.at[k - 1],
                recv_sem=recv_sems.at[k - 1], device_id=peer, device_id_type=MESH).wait_recv()
        for cp in sends:
            cp.wait_send()
        mine.wait()

    return pl.pallas_call(
        body, name="exchange_grads", out_shape=jax.ShapeDtypeStruct((N_DEV, R, C), gpack.dtype),
        in_specs=[pl.BlockSpec(memory_space=pl.ANY)], out_specs=pl.BlockSpec(memory_space=pl.ANY),
        scratch_shapes=[pltpu.SemaphoreType.DMA((N_DEV - 1,)), pltpu.SemaphoreType.DMA((N_DEV - 1,)),
                        pltpu.SemaphoreType.DMA],
    )(gpack)


def _sum_slots(recv, name, tr):
    n, R, C = recv.shape

    def body(r_ref, o_ref):
        acc = r_ref[0].astype(F32)
        for s in range(1, n):
            acc = acc + r_ref[s].astype(F32)
        o_ref[...] = acc

    return pl.pallas_call(
        body, name=name, grid=(R // tr,), in_specs=[pl.BlockSpec((n, tr, C), lambda i: (0, i, 0))],
        out_specs=pl.BlockSpec((tr, C), lambda i: (i, 0)), out_shape=jax.ShapeDtypeStruct((R, C), F32),
        compiler_params=_params(("parallel",)),
    )(recv)


def _sum_stats(st_all, loss_row):
    n, r, D = st_all.shape

    def body(s_ref, o_ref, l_ref):
        acc = s_ref[0]
        for d in range(1, n):
            acc = acc + s_ref[d]
        o_ref[...] = acc
        l_ref[...] = jnp.zeros((8, 128), F32) + jnp.sum(acc[loss_row:loss_row + 1, :], axis=-1, keepdims=True)

    return pl.pallas_call(
        body, name="sum_stats", out_shape=[jax.ShapeDtypeStruct((r, D), F32), jax.ShapeDtypeStruct((8, 128), F32)],
    )(st_all)


def _adamw(w, g, m, v, name):
    R, C = w.shape
    tr = _pick(R, (256, 176, 128, 64, 32, 16, 8))
    c1 = 1.0 / (1.0 - ADAM_B1 ** ADAM_STEP)
    c2 = 1.0 / (1.0 - ADAM_B2 ** ADAM_STEP)

    def body(w_ref, g_ref, m_ref, v_ref, d_ref, nm_ref, nv_ref):
        gv = g_ref[...]
        nm = ADAM_B1 * m_ref[...] + (1.0 - ADAM_B1) * gv
        nv = ADAM_B2 * v_ref[...] + (1.0 - ADAM_B2) * (gv * gv)
        nm_ref[...] = nm
        nv_ref[...] = nv
        d_ref[...] = -ADAM_LR * ((nm * c1) / (jnp.sqrt(nv * c2) + ADAM_EPS) + ADAM_WD * w_ref[...])

    spec = pl.BlockSpec((tr, C), lambda i: (i, 0))
    return pl.pallas_call(
        body, name=name, grid=(R // tr,), in_specs=[spec] * 4, out_specs=[spec] * 3,
        out_shape=[jax.ShapeDtypeStruct((R, C), F32)] * 3, compiler_params=_params(("parallel",)),
    )(w, g, m, v)


def _round16(n):
    return -(-n // 16) * 16


def _pack_layout(D, in_cols, ff, W):
    parts = [("in", D * (in_cols // N_DEV) // D), ("fg", ff // N_DEV), ("fu", ff // N_DEV),
             ("sb", W * (D // N_DEV) // D), ("fx", W * (D // N_DEV) // D), ("o", D // N_DEV), ("fd", ff // N_DEV)]
    layout, off = {}, 0
    for nm, rows in parts:
        layout[nm] = (off, rows)
        off += _round16(rows)
    return layout, off


def _rows_of(a, D, rows):
    a = a.reshape(rows, D)
    return jnp.pad(a, ((0, _round16(rows) - rows), (0, 0)))


def _cols_to_dest(g, D):
    K, N = g.shape
    n = N // N_DEV
    return g.reshape(K, N_DEV, n).transpose(1, 0, 2).reshape(N_DEV, K * n // D, D)


def _cols_from_src(blocks, K, n):
    return blocks.reshape(N_DEV, K, n).transpose(1, 0, 2).reshape(K, N_DEV * n)


def _pad_rows16(a):
    rows = a.shape[1]
    return jnp.pad(a, ((0, 0), (0, _round16(rows) - rows), (0, 0)))


def kernel(x, c, w_ada, b_ada, w_in, b_gate, b_forget, w_sb_out, w_fox_out, w_o, ln1_g, ln1_b, w_ffn_gate, w_ffn_up, w_ffn_down, ln2_g, ln2_b, loss_target, m_w_ada, m_b_ada, m_w_in, m_b_gate, m_b_forget, m_w_sb_out, m_w_fox_out, m_w_o, m_ln1_g, m_ln1_b, m_w_ffn_gate, m_w_ffn_up, m_w_ffn_down, m_ln2_g, m_ln2_b, v_w_ada, v_b_ada, v_w_in, v_b_gate, v_b_forget, v_w_sb_out, v_w_fox_out, v_w_o, v_ln1_g, v_ln1_b, v_w_ffn_gate, v_w_ffn_up, v_w_ffn_down, v_ln2_g, v_ln2_b):
    S, D = x.shape[1], x.shape[2]
    W = w_sb_out.shape[1]
    n_heads = b_forget.shape[1]
    ff = w_ffn_down.shape[1] * N_DEV
    in_loc = w_in.shape[2]
    in_cols = in_loc * N_DEV
    ada_loc = w_ada.shape[2]
    n_cond = ada_loc * N_DEV // D
    assert w_ada.shape[0] == 1 and n_cond == 6 and in_cols == 6 * W + n_heads + 2 * D and n_heads <= 128
    me = 4 * lax.axis_index("x") + 2 * lax.axis_index("y") + lax.axis_index("c")

    c_all = _all_gather_small(c, "gather_c").reshape(N_DEV, D)
    c16 = jnp.pad(c_all, ((0, 16 - N_DEV), (0, 0)))
    b_cols = lax.dynamic_slice(b_ada, (0, me * ada_loc), (1, ada_loc))
    ada_cols = _mm([(c16, w_ada[0])], 'nn', F32, "ada_fwd", bias=b_cols, silu_a=True)[:N_DEV]
    ada_all = _all_gather_small(ada_cols, "gather_ada")
    ada_me = lax.dynamic_index_in_dim(ada_all, me, axis=1, keepdims=False)
    ada8 = jnp.pad(ada_me.reshape(n_cond, D), ((0, 8 - n_cond), (0, 0)))
    lnp8 = jnp.concatenate([ln1_g, ln1_b, ln2_g, ln2_b, jnp.zeros((4, D), F32)], axis=0)

    layout, R = _pack_layout(D, in_cols, ff, W)
    shards = dict(**{"in": w_in[0]}, fg=w_ffn_gate[0], fu=w_ffn_up[0], sb=w_sb_out[0], fx=w_fox_out[0], o=w_o[0],
                  fd=w_ffn_down[0])
    packed = jnp.concatenate([_rows_of(shards[nm].astype(BF16), D, layout[nm][1]) for nm in layout], axis=0)
    gathered = _all_gather_weights(packed)

    def part(nm):
        off, rows = layout[nm]
        return gathered[:, off:off + rows, :]

    w_in_full = _cols_from_src(part("in"), D, in_loc)
    wqkv = w_in_full[:, :6 * W]
    wf = jnp.pad(w_in_full[:, 6 * W:6 * W + n_heads], ((0, 0), (0, 128 - n_heads)))
    wgs = w_in_full[:, 6 * W + n_heads:6 * W + n_heads + D]
    wgf = w_in_full[:, 6 * W + n_heads + D:]
    wfg = _cols_from_src(part("fg"), D, ff // N_DEV)
    wfu = _cols_from_src(part("fu"), D, ff // N_DEV)
    wsb = _cols_from_src(part("sb"), W, D // N_DEV)
    wfx = _cols_from_src(part("fx"), W, D // N_DEV)
    wo = part("o").reshape(D, D)
    wfd = part("fd").reshape(ff, D)
    bf_pad = jnp.pad(b_forget, ((0, 0), (0, 128 - n_heads)))

    gx, wg, st = _local_step(x[0], loss_target[0], ada8, lnp8, b_gate[:, :D], b_gate[:, D:], bf_pad,
                             wqkv, wf, wgs, wgf, wsb, wfx, wo, wfg, wfu, wfd)

    g_in = jnp.concatenate(wg["qkv"] + [wg["f"][:, :n_heads], wg["gs"], wg["gf"]], axis=1)
    dest = {"in": _cols_to_dest(g_in, D), "fg": _cols_to_dest(wg["fg"], D), "fu": _cols_to_dest(wg["fu"], D),
            "sb": _cols_to_dest(wg["sb"], D), "fx": _cols_to_dest(wg["fx"], D),
            "o": wg["o"].reshape(N_DEV, D // N_DEV, D), "fd": wg["fd"].reshape(N_DEV, ff // N_DEV, D)}
    gpack = jnp.concatenate([_pad_rows16(dest[nm].astype(BF16)) for nm in layout], axis=1)
    recv = _exchange_grads(gpack)
    gsum = _sum_slots(recv, "sum_grads", _pick(R, (512, 656, 256, 128, 64, 16)))

    def gshard(nm, shape):
        off, rows = layout[nm]
        return gsum[off:off + rows].reshape(shape)

    zrow = jnp.zeros((1, D), F32)
    gb_f_row = jnp.pad(st["gb_f"][0:1], ((0, 0), (0, D - 128)))
    stats16 = jnp.concatenate([
        st["first"][1:2], st["first"][0:1], st["mid"][4:5], st["mid"][1:2], st["mid"][0:1], st["loss"][3:4],
        st["mid"][2:3], st["mid"][3:4], st["loss"][1:2], st["loss"][2:3], st["gb_sb"][0:1], st["gb_fx"][0:1],
        st["loss"][0:1], gb_f_row, zrow, zrow], axis=0)
    st_all = _all_gather_small(stats16, "gather_stats")
    st_sum, loss_blk = _sum_stats(st_all, 12)
    loss = loss_blk[0, 0]

    d_ada_all = st_all[:, :n_cond, :].reshape(N_DEV, n_cond * D)
    d_cols = lax.dynamic_slice(d_ada_all, (0, me * ada_loc), (N_DEV, ada_loc))
    d16 = jnp.pad(d_cols, ((0, 16 - N_DEV), (0, 0)))
    g_w_ada = _mm([(c16, d16)], 'tn', F32, "ada_wgrad", silu_a=True)

    small_w = jnp.concatenate([b_ada.reshape(n_cond, D), ln1_g, ln1_b, ln2_g, ln2_b, b_gate.reshape(2, D), zrow,
                               jnp.pad(b_forget, ((0, 0), (0, D - n_heads))), zrow, zrow], axis=0)
    small_m = jnp.concatenate([m_b_ada.reshape(n_cond, D), m_ln1_g, m_ln1_b, m_ln2_g, m_ln2_b, m_b_gate.reshape(2, D),
                               zrow, jnp.pad(m_b_forget, ((0, 0), (0, D - n_heads))), zrow, zrow], axis=0)
    small_v = jnp.concatenate([v_b_ada.reshape(n_cond, D), v_ln1_g, v_ln1_b, v_ln2_g, v_ln2_b, v_b_gate.reshape(2, D),
                               zrow, jnp.pad(v_b_forget, ((0, 0), (0, D - n_heads))), zrow, zrow], axis=0)
    sm = _adamw(small_w, st_sum, small_m, small_v, "adamw_small")

    def small(a, nm):
        if nm == "b_ada":
            return a[0:n_cond].reshape(1, n_cond * D)
        if nm == "b_gate":
            return a[10:12].reshape(1, 2 * D)
        if nm == "b_forget":
            return a[13:14, :n_heads]
        row = {"ln1_g": 6, "ln1_b": 7, "ln2_g": 8, "ln2_b": 9}[nm]
        return a[row:row + 1]

    big = {
        "w_ada": (w_ada[0], g_w_ada, m_w_ada[0], v_w_ada[0]),
        "w_in": (w_in[0], gshard("in", w_in.shape[1:]), m_w_in[0], v_w_in[0]),
        "w_sb_out": (w_sb_out[0], gshard("sb", w_sb_out.shape[1:]), m_w_sb_out[0], v_w_sb_out[0]),
        "w_fox_out": (w_fox_out[0], gshard("fx", w_fox_out.shape[1:]), m_w_fox_out[0], v_w_fox_out[0]),
        "w_o": (w_o[0], gshard("o", w_o.shape[1:]), m_w_o[0], v_w_o[0]),
        "w_ffn_gate": (w_ffn_gate[0], gshard("fg", w_ffn_gate.shape[1:]), m_w_ffn_gate[0], v_w_ffn_gate[0]),
        "w_ffn_up": (w_ffn_up[0], gshard("fu", w_ffn_up.shape[1:]), m_w_ffn_up[0], v_w_ffn_up[0]),
        "w_ffn_down": (w_ffn_down[0], gshard("fd", w_ffn_down.shape[1:]), m_w_ffn_down[0], v_w_ffn_down[0]),
    }
    order = ["w_ada", "b_ada", "w_in", "b_gate", "b_forget", "w_sb_out", "w_fox_out", "w_o", "ln1_g", "ln1_b",
             "w_ffn_gate", "w_ffn_up", "w_ffn_down", "ln2_g", "ln2_b"]
    grads, deltas, new_ms, new_vs = [], [], [], []
    for nm in order:
        if nm in big:
            w, g, m, v = big[nm]
            d, nm_, nv_ = _adamw(w, g, m, v, "adamw_" + nm)
            grads.append(g[None])
            deltas.append(d[None])
            new_ms.append(nm_[None])
            new_vs.append(nv_[None])
        else:
            grads.append(small(st_sum, nm))
            deltas.append(small(sm[0], nm))
            new_ms.append(small(sm[1], nm))
            new_vs.append(small(sm[2], nm))
    return (loss, gx[None], *grads, *deltas, *new_ms, *new_vs)
```

```python
import functools

import jax
import jax.numpy as jnp
import numpy as np
from jax import lax
from jax.experimental import pallas as pl
from jax.experimental.pallas import tpu as pltpu

F32 = jnp.float32
BF16 = jnp.bfloat16

HEAD_DIM = 64
PAIR = 2 * HEAD_DIM
LN_EPS = 1e-5
ALPHA = 2.0 ** 0.25
ADAM_LR, ADAM_B1, ADAM_B2, ADAM_EPS, ADAM_WD, ADAM_STEP = 0.001, 0.9, 0.999, 1e-08, 0.01, 10
N_DEV = 8
VMEM_LIMIT = 56 * 1024 * 1024
MESH = pl.DeviceIdType.MESH


def _dot(a, b, ca=1, cb=0):
    return lax.dot_general(a, b, (((ca,), (cb,)), ((), ())), preferred_element_type=F32)


def _pick(n, cands):
    for t in cands:
        if n % t == 0:
            return t
    return n


def _params(sem):
    return pltpu.CompilerParams(dimension_semantics=sem, vmem_limit_bytes=VMEM_LIMIT)


def _mm(pairs, mode, out_dtype, name, bias=None, act=None, silu_a=False,
        tm_c=(512, 256, 128), tn_c=(512, 384, 256, 128)):
    norm = []
    for p in pairs:
        a, b = p[0], p[1]
        kdim_a = a.shape[0] if mode == 'tn' else a.shape[1]
        K, ka, kb = (p[2], p[3], p[4]) if len(p) > 2 else (kdim_a, 0, 0)
        norm.append((a, b, K, ka, kb))
    a0, b0 = norm[0][0], norm[0][1]
    M = a0.shape[1] if mode == 'tn' else a0.shape[0]
    N = b0.shape[0] if mode == 'nt' else b0.shape[1]
    tm, tn = _pick(M, tm_c), _pick(N, tn_c)
    n_pairs = len(norm)

    in_specs, args = [], []
    for a, b, K, ka, kb in norm:
        if mode == 'tn':
            in_specs.append(pl.BlockSpec((K, tm), lambda i, j, ka=ka: (ka, i)))
        else:
            in_specs.append(pl.BlockSpec((tm, K), lambda i, j, ka=ka: (i, ka)))
        if mode == 'nt':
            in_specs.append(pl.BlockSpec((tn, K), lambda i, j, kb=kb: (j, kb)))
        else:
            in_specs.append(pl.BlockSpec((K, tn), lambda i, j, kb=kb: (kb, j)))
        args += [a, b]
    if bias is not None:
        in_specs.append(pl.BlockSpec((1, tn), lambda i, j: (0, j)))
        args.append(bias)

    ca = 0 if mode == 'tn' else 1
    cb = 1 if mode == 'nt' else 0

    def body(*refs):
        o_ref = refs[-1]
        acc = None
        for p in range(n_pairs):
            av = refs[2 * p][...]
            if silu_a:
                av = av / (1.0 + jnp.exp(-av))
            d = _dot(av.astype(BF16), refs[2 * p + 1][...].astype(BF16), ca, cb)
            acc = d if acc is None else acc + d
        if bias is not None:
            acc = acc + refs[2 * n_pairs][...]
        if act == 'sigmoid':
            acc = 1.0 / (1.0 + jnp.exp(-acc))
        o_ref[...] = acc.astype(out_dtype)

    return pl.pallas_call(
        body, name=name, grid=(M // tm, N // tn), in_specs=in_specs,
        out_specs=pl.BlockSpec((tm, tn), lambda i, j: (i, j)),
        out_shape=jax.ShapeDtypeStruct((M, N), out_dtype),
        compiler_params=_params(("parallel", "parallel")),
    )(*args)


def _rows_call(body, name, row_ins, vec_ins, row_outs, acc_outs, ts):
    S = row_ins[0].shape[0]
    in_specs = [pl.BlockSpec((ts, a.shape[1]), lambda i: (i, 0)) for a in row_ins]
    in_specs += [pl.BlockSpec(a.shape, lambda i: (0, 0)) for a in vec_ins]
    out_specs = [pl.BlockSpec((ts, c), lambda i: (i, 0)) for c, _ in row_outs]
    out_specs += [pl.BlockSpec(s, lambda i: (0, 0)) for s in acc_outs]
    out_shape = [jax.ShapeDtypeStruct((S, c), dt) for c, dt in row_outs]
    out_shape += [jax.ShapeDtypeStruct(s, F32) for s in acc_outs]
    return pl.pallas_call(
        body, name=name, grid=(S // ts,), in_specs=in_specs, out_specs=out_specs, out_shape=out_shape,
        compiler_params=_params(("arbitrary",)),
    )(*row_ins, *vec_ins)


def _ln_stats(v):
    mu = jnp.mean(v, axis=-1, keepdims=True)
    d = v - mu
    var = jnp.mean(d * d, axis=-1, keepdims=True)
    rstd = lax.rsqrt(var + LN_EPS)
    return d * rstd, rstd


def _ln_bwd(dxhat, xhat, rstd):
    m1 = jnp.mean(dxhat, axis=-1, keepdims=True)
    m2 = jnp.mean(dxhat * xhat, axis=-1, keepdims=True)
    return rstd * (dxhat - m1 - xhat * m2)


def _colsum(v):
    return jnp.sum(v, axis=0, keepdims=True)


def _ln_mod(x, ada8, ts):
    D = x.shape[1]

    def body(x_ref, v_ref, u_ref):
        xhat, _ = _ln_stats(x_ref[...])
        u_ref[...] = (xhat * (1.0 + v_ref[1:2, :]) + v_ref[0:1, :]).astype(BF16)

    return _rows_call(body, "ln_mod", [x], [ada8], [(D, BF16)], [], ts)[0]


def _gate_mix(g_sb, g_fx, y_sb, y_fx, ts):
    D = y_sb.shape[1]

    def body(gs, gf, ys, yf, o_ref):
        o_ref[...] = (gs[...] * ys[...] + gf[...] * yf[...]).astype(BF16)

    return _rows_call(body, "gate_mix", [g_sb, g_fx, y_sb, y_fx], [], [(D, BF16)], [], ts)[0]


def _post_attn(x, mix, ada8, lnp8, ts):
    D = x.shape[1]

    def body(x_ref, mix_ref, v_ref, p_ref, x1_ref, u2_ref):
        r1 = ALPHA * x_ref[...] + v_ref[2:3, :] * mix_ref[...]
        xhat, _ = _ln_stats(r1)
        x1 = xhat * p_ref[0:1, :] + p_ref[1:2, :]
        x1_ref[...] = x1
        xh1, _ = _ln_stats(x1)
        u2_ref[...] = (xh1 * (1.0 + v_ref[4:5, :]) + v_ref[3:4, :]).astype(BF16)

    return _rows_call(body, "post_attn", [x, mix], [ada8, lnp8], [(D, F32), (D, BF16)], [], ts)


def _loss_head(x1, h, target, ada8, lnp8, ts):
    D = x1.shape[1]

    def body(x1_ref, h_ref, t_ref, v_ref, p_ref, dr2_ref, dh_ref, st_ref):
        @pl.when(pl.program_id(0) == 0)
        def _():
            st_ref[...] = jnp.zeros_like(st_ref)

        hv = h_ref[...]
        g2 = v_ref[5:6, :]
        r2 = ALPHA * x1_ref[...] + g2 * hv
        xhat, rstd = _ln_stats(r2)
        y = xhat * p_ref[2:3, :] + p_ref[3:4, :]
        err = y - t_ref[...]
        dy = err * (1.0 / D)
        dr2 = _ln_bwd(dy * p_ref[2:3, :], xhat, rstd)
        dr2_ref[...] = dr2
        dh_ref[...] = (dr2 * g2).astype(BF16)
        st_ref[0:1, :] += _colsum(err * err) * (0.5 / D)
        st_ref[1:2, :] += _colsum(dy * xhat)
        st_ref[2:3, :] += _colsum(dy)
        st_ref[3:4, :] += _colsum(dr2 * hv)

    return _rows_call(body, "loss_head", [x1, h, target], [ada8, lnp8], [(D, F32), (D, BF16)], [(8, D)], ts)


def _mid_bwd(du2, x1, dr2, mix, x, ada8, lnp8, ts):
    D = x.shape[1]

    def body(du2_ref, x1_ref, dr2_ref, mix_ref, x_ref, v_ref, p_ref, dr1_ref, dmix_ref, st_ref):
        @pl.when(pl.program_id(0) == 0)
        def _():
            st_ref[...] = jnp.zeros_like(st_ref)

        du2v = du2_ref[...]
        xh1, rstd1 = _ln_stats(x1_ref[...])
        dx1 = ALPHA * dr2_ref[...] + _ln_bwd(du2v * (1.0 + v_ref[4:5, :]), xh1, rstd1)
        mixv = mix_ref[...]
        g1 = v_ref[2:3, :]
        r1 = ALPHA * x_ref[...] + g1 * mixv
        xhr, rstdr = _ln_stats(r1)
        dr1 = _ln_bwd(dx1 * p_ref[0:1, :], xhr, rstdr)
        dr1_ref[...] = dr1
        dmix_ref[...] = (dr1 * g1).astype(BF16)
        st_ref[0:1, :] += _colsum(du2v * xh1)
        st_ref[1:2, :] += _colsum(du2v)
        st_ref[2:3, :] += _colsum(dx1 * xhr)
        st_ref[3:4, :] += _colsum(dx1)
        st_ref[4:5, :] += _colsum(dr1 * mixv)

    return _rows_call(body, "mid_bwd", [du2, x1, dr2, mix, x], [ada8, lnp8], [(D, F32), (D, BF16)], [(8, D)], ts)


def _first_bwd(du1, x, dr1, ada8, ts):
    D = x.shape[1]

    def body(du1_ref, x_ref, dr1_ref, v_ref, gx_ref, st_ref):
        @pl.when(pl.program_id(0) == 0)
        def _():
            st_ref[...] = jnp.zeros_like(st_ref)

        du1v = du1_ref[...]
        xh0, rstd0 = _ln_stats(x_ref[...])
        gx_ref[...] = ALPHA * dr1_ref[...] + _ln_bwd(du1v * (1.0 + v_ref[1:2, :]), xh0, rstd0)
        st_ref[0:1, :] += _colsum(du1v * xh0)
        st_ref[1:2, :] += _colsum(du1v)

    return _rows_call(body, "first_bwd", [du1, x, dr1], [ada8], [(D, F32)], [(8, D)], ts)


def _split3(v):
    hi = v.astype(BF16)
    r = v - hi.astype(F32)
    mid = r.astype(BF16)
    lo = (r - mid.astype(F32)).astype(BF16)
    return hi, mid, lo


def _fgate_fwd(f, bf_pad, tb):
    S = f.shape[0]

    def body(f_ref, b_ref, fc_ref, carry):
        @pl.when(pl.program_id(0) == 0)
        def _():
            carry[...] = jnp.zeros_like(carry)

        z = f_ref[...] + b_ref[...]
        ls = jnp.minimum(z, 0.0) - jnp.log(1.0 + jnp.exp(-jnp.abs(z)))
        r = lax.broadcasted_iota(jnp.int32, (tb, tb), 0)
        c = lax.broadcasted_iota(jnp.int32, (tb, tb), 1)
        tri = (c <= r).astype(BF16)
        hi, mid, lo = _split3(ls)
        cs = _dot(tri, hi) + _dot(tri, mid) + _dot(tri, lo) + carry[...]
        fc_ref[...] = cs
        carry[...] = cs[tb - 1:tb, :]

    return pl.pallas_call(
        body, name="fgate_fwd", grid=(S // tb,),
        in_specs=[pl.BlockSpec((tb, 128), lambda i: (i, 0)), pl.BlockSpec((1, 128), lambda i: (0, 0))],
        out_specs=pl.BlockSpec((tb, 128), lambda i: (i, 0)),
        out_shape=jax.ShapeDtypeStruct((S, 128), F32),
        scratch_shapes=[pltpu.VMEM((1, 128), F32)],
        compiler_params=_params(("arbitrary",)),
    )(f, bf_pad)


def _fgate_bwd(dfc, f, bf_pad, tb):
    S = f.shape[0]
    nb = S // tb

    def body(d_ref, f_ref, b_ref, df_ref, gb_ref, carry):
        @pl.when(pl.program_id(0) == 0)
        def _():
            carry[...] = jnp.zeros_like(carry)
            gb_ref[...] = jnp.zeros_like(gb_ref)

        r = lax.broadcasted_iota(jnp.int32, (tb, tb), 0)
        c = lax.broadcasted_iota(jnp.int32, (tb, tb), 1)
        tri = (c >= r).astype(BF16)
        hi, mid, lo = _split3(d_ref[...])
        rs = _dot(tri, hi) + _dot(tri, mid) + _dot(tri, lo) + carry[...]
        carry[...] = rs[0:1, :]
        z = f_ref[...] + b_ref[...]
        df = rs * (1.0 / (1.0 + jnp.exp(z)))
        df_ref[...] = df
        gb_ref[0:1, :] += _colsum(df)

    return pl.pallas_call(
        body, name="fgate_bwd", grid=(nb,),
        in_specs=[pl.BlockSpec((tb, 128), lambda i: (nb - 1 - i, 0)),
                  pl.BlockSpec((tb, 128), lambda i: (nb - 1 - i, 0)),
                  pl.BlockSpec((1, 128), lambda i: (0, 0))],
        out_specs=[pl.BlockSpec((tb, 128), lambda i: (nb - 1 - i, 0)), pl.BlockSpec((8, 128), lambda i: (0, 0))],
        out_shape=[jax.ShapeDtypeStruct((S, 128), F32), jax.ShapeDtypeStruct((8, 128), F32)],
        scratch_shapes=[pltpu.VMEM((1, 128), F32)],
        compiler_params=_params(("arbitrary",)),
    )(dfc, f, bf_pad)


def _split2(v):
    hi = v.astype(BF16)
    lo = (v - hi.astype(F32)).astype(BF16)
    return hi, lo


def _head_masks():
    lane = lax.broadcasted_iota(jnp.int32, (1, PAIR), 1)
    m0 = lane < HEAD_DIM
    return m0, jnp.logical_not(m0)


def _sel(mask, v):
    return jnp.where(mask, v, jnp.zeros_like(v))


def _softplus(z):
    return jnp.maximum(z, 0.0) + jnp.log(1.0 + jnp.exp(-jnp.abs(z)))


def _qkv_specs(S, tq, n_pairs, base):
    return [pl.BlockSpec((tq, PAIR), lambda p, i: (i, base + p)),
            pl.BlockSpec((S, PAIR), lambda p, i: (0, base + n_pairs + p)),
            pl.BlockSpec((S, PAIR), lambda p, i: (0, base + 2 * n_pairs + p))]


def _sb_fwd(qkv, n_pairs, base, tq):
    S = qkv.shape[0]
    tk = tq
    scale = HEAD_DIM ** -0.5

    def body(q_ref, k_ref, v_ref, o_ref, t_ref, acc_ref):
        i = pl.program_id(1)
        masks = _head_masks()
        q2 = q_ref[...]
        qm = [_sel(m, q2) for m in masks]
        rowpos = i * tq + lax.broadcasted_iota(jnp.int32, (tq, tk), 0)
        colin = lax.broadcasted_iota(jnp.int32, (tq, tk), 1)
        upper = (lax.broadcasted_iota(jnp.int32, (tk, tk), 0) > lax.broadcasted_iota(jnp.int32, (tk, tk), 1)).astype(BF16)
        acc_ref[...] = jnp.zeros_like(acc_ref)

        def step(jj, carry):
            j = i - jj
            off = pl.multiple_of(j * tk, tk)
            k2 = k_ref[pl.ds(off, tk), :]
            v2 = v_ref[pl.ds(off, tk), :]
            mask = (j * tk + colin) < rowpos
            out = None
            new = []
            for h in range(2):
                z = _dot(qm[h], k2, 1, 1) * scale
                sp = _softplus(z)
                lg = jnp.where(mask, -sp, 0.0)
                hi, lo = _split2(lg)
                suf = _dot(hi, upper) + _dot(lo, upper)
                a = jnp.where(mask, jnp.exp(z - sp + suf + carry[h]), 0.0)
                d = _dot(a.astype(BF16), _sel(masks[h], v2))
                out = d if out is None else out + d
                new.append(carry[h] + jnp.sum(lg, axis=-1, keepdims=True))
            acc_ref[...] += out
            return tuple(new)

        zero = jnp.zeros((tq, 1), F32)
        r0, r1 = lax.fori_loop(0, i + 1, step, (zero, zero))
        o_ref[...] = acc_ref[...].astype(BF16)
        t_ref[...] = jnp.where(masks[0], r0, r1)

    W = n_pairs * PAIR
    return pl.pallas_call(
        body, name="sb_fwd", grid=(n_pairs, S // tq), in_specs=_qkv_specs(S, tq, n_pairs, base),
        out_specs=[pl.BlockSpec((tq, PAIR), lambda p, i: (i, p)), pl.BlockSpec((tq, PAIR), lambda p, i: (i, p))],
        out_shape=[jax.ShapeDtypeStruct((S, W), BF16), jax.ShapeDtypeStruct((S, W), F32)],
        scratch_shapes=[pltpu.VMEM((tq, PAIR), F32)],
        compiler_params=_params(("parallel", "arbitrary")),
    )(qkv, qkv, qkv)


def _sb_bwd(qkv, tot, do, n_pairs, base, tq):
    S = qkv.shape[0]
    tk = tq
    nq = S // tq
    scale = HEAD_DIM ** -0.5

    def body(q_ref, k_ref, v_ref, t_ref, do_ref, dq_ref, dk_ref, dv_ref, dq_acc, dk_acc, dv_acc):
        i = pl.program_id(1)
        masks = _head_masks()

        @pl.when(i == 0)
        def _():
            dk_acc[...] = jnp.zeros_like(dk_acc)
            dv_acc[...] = jnp.zeros_like(dv_acc)

        q2 = q_ref[...]
        do2 = do_ref[...]
        qm = [_sel(m, q2) for m in masks]
        dom = [_sel(m, do2) for m in masks]
        t2 = t_ref[...]
        tot_h = [t2[:, 0:1], t2[:, HEAD_DIM:HEAD_DIM + 1]]
        rowpos = i * tq + lax.broadcasted_iota(jnp.int32, (tq, tk), 0)
        colin = lax.broadcasted_iota(jnp.int32, (tq, tk), 1)
        r_i = lax.broadcasted_iota(jnp.int32, (tk, tk), 0)
        c_i = lax.broadcasted_iota(jnp.int32, (tk, tk), 1)
        upper = (r_i > c_i).astype(BF16)
        lower = (r_i < c_i).astype(BF16)
        dq_acc[...] = jnp.zeros_like(dq_acc)

        def step(j, carry):
            off = pl.multiple_of(j * tk, tk)
            k2 = k_ref[pl.ds(off, tk), :]
            v2 = v_ref[pl.ds(off, tk), :]
            mask = (j * tk + colin) < rowpos
            dq = None
            dk = None
            dv = None
            new = []
            for h in range(2):
                cum_l, cum_g = carry[2 * h], carry[2 * h + 1]
                z = _dot(qm[h], k2, 1, 1) * scale
                sp = _softplus(z)
                lg = jnp.where(mask, -sp, 0.0)
                hi, lo = _split2(lg)
                suf = _dot(hi, upper) + _dot(lo, upper)
                row_l = jnp.sum(lg, axis=-1, keepdims=True)
                later = tot_h[h] - cum_l - row_l
                a = jnp.where(mask, jnp.exp(z - sp + suf + later), 0.0)
                da = _dot(dom[h], v2, 1, 1)
                g = da * a
                ghi, glo = _split2(g)
                pre = _dot(ghi, lower) + _dot(glo, lower) + cum_g
                one_m_beta = jnp.exp(-sp)
                dz = jnp.where(mask, g * one_m_beta - (1.0 - one_m_beta) * pre, 0.0)
                dzb = (dz * scale).astype(BF16)
                d1 = _dot(dzb, _sel(masks[h], k2))
                d2 = _dot(dzb, qm[h], 0, 0)
                d3 = _dot(a.astype(BF16), dom[h], 0, 0)
                dq = d1 if dq is None else dq + d1
                dk = d2 if dk is None else dk + d2
                dv = d3 if dv is None else dv + d3
                new += [cum_l + row_l, cum_g + jnp.sum(g, axis=-1, keepdims=True)]
            dq_acc[...] += dq
            dk_acc[pl.ds(off, tk), :] += dk
            dv_acc[pl.ds(off, tk), :] += dv
            return tuple(new)

        zero = jnp.zeros((tq, 1), F32)
        lax.fori_loop(0, i + 1, step, (zero, zero, zero, zero))
        dq_ref[...] = dq_acc[...].astype(BF16)

        @pl.when(i == nq - 1)
        def _():
            dk_ref[...] = dk_acc[...].astype(BF16)
            dv_ref[...] = dv_acc[...].astype(BF16)

    W = n_pairs * PAIR
    in_specs = _qkv_specs(S, tq, n_pairs, base) + [
        pl.BlockSpec((tq, PAIR), lambda p, i: (i, p)),
        pl.BlockSpec((tq, PAIR), lambda p, i: (i, p))]
    out_specs = [pl.BlockSpec((tq, PAIR), lambda p, i: (i, p)),
                 pl.BlockSpec((S, PAIR), lambda p, i: (0, p)),
                 pl.BlockSpec((S, PAIR), lambda p, i: (0, p))]
    dq, dk, dv = pl.pallas_call(
        body, name="sb_bwd", grid=(n_pairs, nq), in_specs=in_specs, out_specs=out_specs,
        out_shape=[jax.ShapeDtypeStruct((S, W), BF16)] * 3,
        scratch_shapes=[pltpu.VMEM((tq, PAIR), F32), pltpu.VMEM((S, PAIR), F32), pltpu.VMEM((S, PAIR), F32)],
        compiler_params=_params(("parallel", "arbitrary")),
    )(qkv, qkv, qkv, tot, do)
    return dq, dk, dv


NEG = -1e30


def _fox_specs(S, tq, n_pairs, base):
    return _qkv_specs(S, tq, n_pairs, base) + [
        pl.BlockSpec((tq, PAIR), lambda p, i: (i, p)),
        pl.BlockSpec((1, S // tq, 8, tq), lambda p, i: (p, 0, 0, 0))]


def _fox_fwd(qkv, fcx, fcr, n_pairs, base, tq):
    S = qkv.shape[0]
    tk = tq
    scale = HEAD_DIM ** -0.5

    def body(q_ref, k_ref, v_ref, fq_ref, fk_ref, o_ref, lse_ref, acc_ref):
        i = pl.program_id(1)
        masks = _head_masks()
        q2 = q_ref[...]
        qm = [_sel(m, q2) for m in masks]
        fq2 = fq_ref[...]
        fq = [fq2[:, 0:1], fq2[:, HEAD_DIM:HEAD_DIM + 1]]
        rowpos = i * tq + lax.broadcasted_iota(jnp.int32, (tq, tk), 0)
        colin = lax.broadcasted_iota(jnp.int32, (tq, tk), 1)
        acc_ref[...] = jnp.zeros_like(acc_ref)

        def step(j, carry):
            off = pl.multiple_of(j * tk, tk)
            k2 = k_ref[pl.ds(off, tk), :]
            v2 = v_ref[pl.ds(off, tk), :]
            fk2 = fk_ref[0, j]
            mask = (j * tk + colin) <= rowpos
            out = None
            new = []
            alphas = []
            for h in range(2):
                m_old, l_old = carry[2 * h], carry[2 * h + 1]
                s = _dot(qm[h], k2, 1, 1) * scale + fq[h] - fk2[h:h + 1, :]
                s = jnp.where(mask, s, NEG)
                m_new = jnp.maximum(m_old, jnp.max(s, axis=-1, keepdims=True))
                p = jnp.exp(s - m_new)
                alpha = jnp.exp(m_old - m_new)
                alphas.append(alpha)
                d = _dot(p.astype(BF16), _sel(masks[h], v2))
                out = d if out is None else out + d
                new += [m_new, alpha * l_old + jnp.sum(p, axis=-1, keepdims=True)]
            acc_ref[...] = acc_ref[...] * jnp.where(masks[0], alphas[0], alphas[1]) + out
            return tuple(new)

        zero = jnp.zeros((tq, 1), F32)
        neg = jnp.full((tq, 1), NEG, F32)
        m0, l0, m1, l1 = lax.fori_loop(0, i + 1, step, (neg, zero, neg, zero))
        o_ref[...] = (acc_ref[...] / jnp.where(masks[0], l0, l1)).astype(BF16)
        lse_ref[...] = jnp.where(masks[0], m0 + jnp.log(l0), m1 + jnp.log(l1))

    W = n_pairs * PAIR
    return pl.pallas_call(
        body, name="fox_fwd", grid=(n_pairs, S // tq), in_specs=_fox_specs(S, tq, n_pairs, base),
        out_specs=[pl.BlockSpec((tq, PAIR), lambda p, i: (i, p)), pl.BlockSpec((tq, PAIR), lambda p, i: (i, p))],
        out_shape=[jax.ShapeDtypeStruct((S, W), BF16), jax.ShapeDtypeStruct((S, W), F32)],
        scratch_shapes=[pltpu.VMEM((tq, PAIR), F32)],
        compiler_params=_params(("parallel", "arbitrary")),
    )(qkv, qkv, qkv, fcx, fcr)


def _fox_bwd(qkv, fcx, fcr, o, lse, do, n_pairs, base, tq):
    S = qkv.shape[0]
    tk = tq
    nq = S // tq
    scale = HEAD_DIM ** -0.5

    def body(q_ref, k_ref, v_ref, fq_ref, fk_ref, o_ref, lse_ref, do_ref,
             dq_ref, dk_ref, dv_ref, dfq_ref, dfk_ref, dq_acc, dk_acc, dv_acc):
        i = pl.program_id(1)
        masks = _head_masks()

        @pl.when(i == 0)
        def _():
            dk_acc[...] = jnp.zeros_like(dk_acc)
            dv_acc[...] = jnp.zeros_like(dv_acc)
            dfk_ref[...] = jnp.zeros_like(dfk_ref)

        q2 = q_ref[...]
        do2 = do_ref[...]
        qm = [_sel(m, q2) for m in masks]
        dom = [_sel(m, do2) for m in masks]
        fq2 = fq_ref[...]
        fq = [fq2[:, 0:1], fq2[:, HEAD_DIM:HEAD_DIM + 1]]
        l2 = lse_ref[...]
        lse_h = [l2[:, 0:1], l2[:, HEAD_DIM:HEAD_DIM + 1]]
        prod = do2.astype(F32) * o_ref[...].astype(F32)
        delta = [jnp.sum(jnp.where(m, prod, 0.0), axis=-1, keepdims=True) for m in masks]
        rowpos = i * tq + lax.broadcasted_iota(jnp.int32, (tq, tk), 0)
        colin = lax.broadcasted_iota(jnp.int32, (tq, tk), 1)
        dq_acc[...] = jnp.zeros_like(dq_acc)

        def step(j, carry):
            off = pl.multiple_of(j * tk, tk)
            k2 = k_ref[pl.ds(off, tk), :]
            v2 = v_ref[pl.ds(off, tk), :]
            fk2 = fk_ref[0, j]
            mask = (j * tk + colin) <= rowpos
            dq = None
            dk = None
            dv = None
            new = []
            dfk_rows = []
            for h in range(2):
                s = _dot(qm[h], k2, 1, 1) * scale + fq[h] - fk2[h:h + 1, :]
                p = jnp.where(mask, jnp.exp(s - lse_h[h]), 0.0)
                dp = _dot(dom[h], v2, 1, 1)
                ds = p * (dp - delta[h])
                dsb = (ds * scale).astype(BF16)
                d1 = _dot(dsb, _sel(masks[h], k2))
                d2 = _dot(dsb, qm[h], 0, 0)
                d3 = _dot(p.astype(BF16), dom[h], 0, 0)
                dq = d1 if dq is None else dq + d1
                dk = d2 if dk is None else dk + d2
                dv = d3 if dv is None else dv + d3
                new.append(carry[h] + jnp.sum(ds, axis=-1, keepdims=True))
                dfk_rows.append(jnp.sum(ds, axis=0, keepdims=True))
            dq_acc[...] += dq
            dk_acc[pl.ds(off, tk), :] += dk
            dv_acc[pl.ds(off, tk), :] += dv
            dfk_ref[0, j, 0:1, :] += dfk_rows[0]
            dfk_ref[0, j, 1:2, :] += dfk_rows[1]
            return tuple(new)

        zero = jnp.zeros((tq, 1), F32)
        r0, r1 = lax.fori_loop(0, i + 1, step, (zero, zero))
        dq_ref[...] = dq_acc[...].astype(BF16)
        dfq_ref[...] = jnp.where(masks[0], r0, r1)

        @pl.when(i == nq - 1)
        def _():
            dk_ref[...] = dk_acc[...].astype(BF16)
            dv_ref[...] = dv_acc[...].astype(BF16)

    W = n_pairs * PAIR
    in_specs = _fox_specs(S, tq, n_pairs, base) + [
        pl.BlockSpec((tq, PAIR), lambda p, i: (i, p)),
        pl.BlockSpec((tq, PAIR), lambda p, i: (i, p)),
        pl.BlockSpec((tq, PAIR), lambda p, i: (i, p))]
    out_specs = [pl.BlockSpec((tq, PAIR), lambda p, i: (i, p)),
                 pl.BlockSpec((S, PAIR), lambda p, i: (0, p)),
                 pl.BlockSpec((S, PAIR), lambda p, i: (0, p)),
                 pl.BlockSpec((tq, PAIR), lambda p, i: (i, p)),
                 pl.BlockSpec((1, nq, 8, tk), lambda p, i: (p, 0, 0, 0))]
    return pl.pallas_call(
        body, name="fox_bwd", grid=(n_pairs, nq), in_specs=in_specs, out_specs=out_specs,
        out_shape=[jax.ShapeDtypeStruct((S, W), BF16)] * 3
        + [jax.ShapeDtypeStruct((S, W), F32), jax.ShapeDtypeStruct((n_pairs, nq, 8, tk), F32)],
        scratch_shapes=[pltpu.VMEM((tq, PAIR), F32), pltpu.VMEM((S, PAIR), F32), pltpu.VMEM((S, PAIR), F32)],
        compiler_params=_params(("parallel", "arbitrary")),
    )(qkv, qkv, qkv, fcx, fcr, o, lse, do)


RC = 32


def _chunks(n_rows, fn):
    for ci in range(n_rows // RC):
        fn(ci * RC)


def _wide(v, tk):
    return v if tk == 128 else jnp.tile(v, (1, tk // 128))


def _rep(col):
    return jnp.broadcast_to(col, (col.shape[0], 128))


def _per_head(blk, masks):
    sw = pltpu.roll(blk, HEAD_DIM, axis=1)
    return jnp.where(masks[0], blk, sw), jnp.where(masks[0], sw, blk)


def _fill_masked(dst_ref, src_ref, masks, mul=None, ones_lane=None):
    v = src_ref[...]
    if mul is not None:
        v = v * mul
    lane = lax.broadcasted_iota(jnp.int32, (1, PAIR), 1)
    for h in range(2):
        m = _sel(masks[h], v)
        if ones_lane is not None:
            m = jnp.where(lane == ones_lane[h], jnp.ones_like(m), m)
        dst_ref[h] = m


def _tri(tk, cmp):
    r = lax.broadcasted_iota(jnp.int32, (tk, tk), 0)
    c = lax.broadcasted_iota(jnp.int32, (tk, tk), 1)
    return cmp(r, c).astype(BF16)


def _diag_mask(r0, tk, strict):
    row = r0 + lax.broadcasted_iota(jnp.int32, (RC, tk), 0)
    col = lax.broadcasted_iota(jnp.int32, (RC, tk), 1)
    return (col < row) if strict else (col <= row)


def _tiles(i, tile):
    def step(jj, carry):
        tile(2 * jj, False, 0)
        tile(2 * jj + 1, False, 1)
        return carry
    lax.fori_loop(0, i // 2, step, 0)

    @pl.when(i % 2 == 1)
    def _():
        tile(i - 1, False, 0)
    tile(i, True, 1)


def _tiles_reversed(i, tile):
    tile(i, True, 1)

    def step(jj, carry):
        tile(i - 1 - 2 * jj, False, 0)
        tile(i - 2 - 2 * jj, False, 1)
        return carry
    lax.fori_loop(0, i // 2, step, 0)

    @pl.when(i % 2 == 1)
    def _():
        tile(0, False, 0)


def _sb_fwd2(qkv, n_pairs, base, tq):
    S = qkv.shape[0]
    tk = tq
    scale = HEAD_DIM ** -0.5

    def body(q_ref, k_ref, v_ref, o_ref, t_ref, z_ref, hi_ref, lo_ref, suf_ref, p_ref, r_ref, acc_ref, vm_ref):
        i = pl.program_id(1)
        masks = _head_masks()

        @pl.when(i == 0)
        def _():
            _fill_masked(vm_ref, v_ref, masks)

        q2 = q_ref[...] * scale
        qm = [_sel(m, q2) for m in masks]
        incl = _tri(tk, lambda r, c: r >= c)
        r_ref[...] = jnp.zeros_like(r_ref)
        acc_ref[...] = jnp.zeros_like(acc_ref)

        def tile(j, diag, slot):
            off = pl.multiple_of(j * tk, tk)
            k2 = k_ref[pl.ds(off, tk), :]
            v2 = v_ref[pl.ds(off, tk), :]
            for h in range(2):
                z_ref[2 * slot + h] = _dot(qm[h], k2, 1, 1)
            for h in range(2):
                def split(r0, h=h):
                    rows = pl.ds(r0, RC)
                    lg = -_softplus(z_ref[2 * slot + h, rows, :])
                    if diag:
                        lg = jnp.where(_diag_mask(r0, tk, True), lg, 0.0)
                    hi, lo = _split2(lg)
                    hi_ref[2 * slot + h, rows, :] = hi
                    lo_ref[2 * slot + h, rows, :] = lo
                _chunks(tq, split)
            for h in range(2):
                suf_ref[2 * slot + h] = _dot(hi_ref[2 * slot + h], incl) + _dot(lo_ref[2 * slot + h], incl)
            for h in range(2):
                def weights(r0, h=h):
                    rows = pl.ds(r0, RC)
                    a = jnp.exp(z_ref[2 * slot + h, rows, :] + suf_ref[2 * slot + h, rows, :] + _wide(r_ref[h, rows, :], tk))
                    if diag:
                        a = jnp.where(_diag_mask(r0, tk, True), a, 0.0)
                    p_ref[2 * slot + h, rows, :] = a.astype(BF16)
                _chunks(tq, weights)
            keys = pl.ds(off, tk)
            acc_ref[...] += _dot(p_ref[2 * slot], vm_ref[0, keys, :]) + _dot(p_ref[2 * slot + 1], vm_ref[1, keys, :])
            for h in range(2):
                r_ref[h] += _rep(suf_ref[2 * slot + h, :, 0:1])

        _tiles_reversed(i, tile)
        o_ref[...] = acc_ref[...].astype(BF16)
        t_ref[...] = jnp.where(masks[0], r_ref[0], r_ref[1])

    W = n_pairs * PAIR
    return pl.pallas_call(
        body, name="sb_fwd", grid=(n_pairs, S // tq), in_specs=_qkv_specs(S, tq, n_pairs, base),
        out_specs=[pl.BlockSpec((tq, PAIR), lambda p, i: (i, p)), pl.BlockSpec((tq, PAIR), lambda p, i: (i, p))],
        out_shape=[jax.ShapeDtypeStruct((S, W), BF16), jax.ShapeDtypeStruct((S, W), F32)],
        scratch_shapes=[pltpu.VMEM((4, tq, tk), F32), pltpu.VMEM((4, tq, tk), BF16), pltpu.VMEM((4, tq, tk), BF16),
                        pltpu.VMEM((4, tq, tk), F32), pltpu.VMEM((4, tq, tk), BF16), pltpu.VMEM((2, tq, 128), F32),
                        pltpu.VMEM((tq, PAIR), F32), pltpu.VMEM((2, S, PAIR), BF16)],
        compiler_params=_params(("parallel", "arbitrary")),
    )(qkv, qkv, qkv)


def _sb_bwd2(qkv, tot, do, n_pairs, base, tq):
    S = qkv.shape[0]
    tk = tq
    nq = S // tq
    scale = HEAD_DIM ** -0.5

    def body(q_ref, k_ref, v_ref, t_ref, do_ref, dq_ref, dk_ref, dv_ref,
             z_ref, g_ref, omb_ref, cum_ref, hi_ref, lo_ref, a_ref, dz_ref,
             later_ref, cl_ref, cg_ref, tot_ref, dq_acc, dk_acc, dv_acc, ks_ref):
        i = pl.program_id(1)
        masks = _head_masks()

        @pl.when(i == 0)
        def _():
            dk_acc[...] = jnp.zeros_like(dk_acc)
            dv_acc[...] = jnp.zeros_like(dv_acc)
            _fill_masked(ks_ref, k_ref, masks, mul=scale)

        q2 = q_ref[...] * scale
        do2 = do_ref[...]
        qm = [_sel(m, q2) for m in masks]
        dom = [_sel(m, do2) for m in masks]
        t0, t1 = _per_head(t_ref[...], masks)
        tot_ref[0] = t0
        tot_ref[1] = t1
        suffix = _tri(tk, lambda r, c: r >= c)
        prefix = _tri(tk, lambda r, c: r <= c)
        cl_ref[...] = jnp.zeros_like(cl_ref)
        cg_ref[...] = jnp.zeros_like(cg_ref)
        dq_acc[...] = jnp.zeros_like(dq_acc)

        def tile(j, diag, slot):
            off = pl.multiple_of(j * tk, tk)
            k2 = k_ref[pl.ds(off, tk), :]
            v2 = v_ref[pl.ds(off, tk), :]
            for h in range(2):
                z_ref[2 * slot + h] = _dot(qm[h], k2, 1, 1)
                g_ref[2 * slot + h] = _dot(dom[h], v2, 1, 1)
            for h in range(2):
                def split(r0, h=h):
                    rows = pl.ds(r0, RC)
                    sp = _softplus(z_ref[2 * slot + h, rows, :])
                    omb_ref[2 * slot + h, rows, :] = jnp.exp(-sp)
                    lg = -sp
                    if diag:
                        lg = jnp.where(_diag_mask(r0, tk, True), lg, 0.0)
                    hi, lo = _split2(lg)
                    hi_ref[2 * slot + h, rows, :] = hi
                    lo_ref[2 * slot + h, rows, :] = lo
                _chunks(tq, split)
            for h in range(2):
                cum_ref[2 * slot + h] = _dot(hi_ref[2 * slot + h], suffix) + _dot(lo_ref[2 * slot + h], suffix)
            for h in range(2):
                row_l = _rep(cum_ref[2 * slot + h, :, 0:1])
                later_ref[2 * slot + h] = tot_ref[h] - cl_ref[h] - row_l
                cl_ref[h] += row_l
            for h in range(2):
                def weights(r0, h=h):
                    rows = pl.ds(r0, RC)
                    a = jnp.exp(z_ref[2 * slot + h, rows, :] + cum_ref[2 * slot + h, rows, :] + _wide(later_ref[2 * slot + h, rows, :], tk))
                    if diag:
                        a = jnp.where(_diag_mask(r0, tk, True), a, 0.0)
                    g = g_ref[2 * slot + h, rows, :] * a
                    g_ref[2 * slot + h, rows, :] = g
                    a_ref[2 * slot + h, rows, :] = a.astype(BF16)
                    hi, lo = _split2(g)
                    hi_ref[2 * slot + h, rows, :] = hi
                    lo_ref[2 * slot + h, rows, :] = lo
                _chunks(tq, weights)
            for h in range(2):
                cum_ref[2 * slot + h] = _dot(hi_ref[2 * slot + h], prefix) + _dot(lo_ref[2 * slot + h], prefix)
            for h in range(2):
                def dscore(r0, h=h):
                    rows = pl.ds(r0, RC)
                    g = g_ref[2 * slot + h, rows, :]
                    before = cum_ref[2 * slot + h, rows, :] - g + _wide(cg_ref[h, rows, :], tk)
                    omb = omb_ref[2 * slot + h, rows, :]
                    dz = g * omb - (1.0 - omb) * before
                    if diag:
                        dz = jnp.where(_diag_mask(r0, tk, True), dz, 0.0)
                    dz_ref[2 * slot + h, rows, :] = dz.astype(BF16)
                _chunks(tq, dscore)
            for h in range(2):
                cg_ref[h] += _rep(cum_ref[2 * slot + h, :, tk - 1:tk])
            keys = pl.ds(off, tk)
            dq_acc[...] += _dot(dz_ref[2 * slot], ks_ref[0, keys, :]) + _dot(dz_ref[2 * slot + 1], ks_ref[1, keys, :])
            dk_acc[pl.ds(off, tk), :] += _dot(dz_ref[2 * slot], qm[0], 0, 0) + _dot(dz_ref[2 * slot + 1], qm[1], 0, 0)
            dv_acc[pl.ds(off, tk), :] += _dot(a_ref[2 * slot], dom[0], 0, 0) + _dot(a_ref[2 * slot + 1], dom[1], 0, 0)

        _tiles(i, tile)
        dq_ref[...] = dq_acc[...].astype(BF16)

        @pl.when(i == nq - 1)
        def _():
            dk_ref[...] = dk_acc[...].astype(BF16)
            dv_ref[...] = dv_acc[...].astype(BF16)

    W = n_pairs * PAIR
    in_specs = _qkv_specs(S, tq, n_pairs, base) + [
        pl.BlockSpec((tq, PAIR), lambda p, i: (i, p)),
        pl.BlockSpec((tq, PAIR), lambda p, i: (i, p))]
    out_specs = [pl.BlockSpec((tq, PAIR), lambda p, i: (i, p)),
                 pl.BlockSpec((S, PAIR), lambda p, i: (0, p)),
                 pl.BlockSpec((S, PAIR), lambda p, i: (0, p))]
    big, stat = (4, tq, tk), (2, tq, 128)
    return pl.pallas_call(
        body, name="sb_bwd", grid=(n_pairs, nq), in_specs=in_specs, out_specs=out_specs,
        out_shape=[jax.ShapeDtypeStruct((S, W), BF16)] * 3,
        scratch_shapes=[pltpu.VMEM(big, F32)] * 4 + [pltpu.VMEM(big, BF16)] * 4
        + [pltpu.VMEM((4, tq, 128), F32)] + [pltpu.VMEM(stat, F32)] * 3
        + [pltpu.VMEM((tq, PAIR), F32), pltpu.VMEM((S, PAIR), F32), pltpu.VMEM((S, PAIR), F32),
           pltpu.VMEM((2, S, PAIR), BF16)],
        compiler_params=_params(("parallel", "arbitrary")),
    )(qkv, qkv, qkv, tot, do)


def _fox_fwd2(qkv, fcx, fcr, n_pairs, base, tq):
    S = qkv.shape[0]
    tk = tq
    scale = HEAD_DIM ** -0.5
    spare = (HEAD_DIM, 0)

    def body(q_ref, k_ref, v_ref, fq_ref, fk_ref, o_ref, lse_ref, s_ref, p_ref, m_ref, al_ref, fqr_ref, acc_ref, vm_ref):
        i = pl.program_id(1)
        masks = _head_masks()

        @pl.when(i == 0)
        def _():
            _fill_masked(vm_ref, v_ref, masks, ones_lane=spare)

        q2 = q_ref[...] * scale
        qm = [_sel(m, q2) for m in masks]
        f0, f1 = _per_head(fq_ref[...], masks)
        fqr_ref[0] = f0
        fqr_ref[1] = f1
        m_ref[...] = jnp.full(m_ref.shape, NEG, F32)
        acc_ref[...] = jnp.zeros_like(acc_ref)

        def tile(j, diag, slot):
            off = pl.multiple_of(j * tk, tk)
            k2 = k_ref[pl.ds(off, tk), :]
            v2 = v_ref[pl.ds(off, tk), :]
            fk2 = fk_ref[0, j]
            for h in range(2):
                s_ref[2 * slot + h] = _dot(qm[h], k2, 1, 1)
            for h in range(2):
                fk_row = fk2[h:h + 1, :]

                def probs(r0, h=h, fk_row=fk_row):
                    rows = pl.ds(r0, RC)
                    sv = s_ref[2 * slot + h, rows, :] - fk_row
                    if diag:
                        sv = jnp.where(_diag_mask(r0, tk, False), sv, NEG)
                    fq = fqr_ref[h, rows, :]
                    m_prev = m_ref[h, rows, :]
                    m_new = jnp.maximum(m_prev, jnp.max(sv, axis=-1, keepdims=True) + fq)
                    p_ref[2 * slot + h, rows, :] = jnp.exp(sv + _wide(fq - m_new, tk)).astype(BF16)
                    al_ref[2 * slot + h, rows, :] = jnp.exp(m_prev - m_new)
                    m_ref[h, rows, :] = m_new
                _chunks(tq, probs)
            for h in range(2):
                acc_ref[h] = acc_ref[h] * al_ref[2 * slot + h] + _dot(p_ref[2 * slot + h], vm_ref[h, pl.ds(off, tk), :])

        _tiles(i, tile)
        a0, a1 = acc_ref[0], acc_ref[1]
        l0 = _rep(a0[:, spare[0]:spare[0] + 1])
        l1 = _rep(a1[:, spare[1]:spare[1] + 1])
        o_ref[...] = jnp.where(masks[0], a0 / l0, a1 / l1).astype(BF16)
        lse_ref[...] = jnp.where(masks[0], m_ref[0] + jnp.log(l0), m_ref[1] + jnp.log(l1))

    W = n_pairs * PAIR
    return pl.pallas_call(
        body, name="fox_fwd", grid=(n_pairs, S // tq), in_specs=_fox_specs(S, tq, n_pairs, base),
        out_specs=[pl.BlockSpec((tq, PAIR), lambda p, i: (i, p)), pl.BlockSpec((tq, PAIR), lambda p, i: (i, p))],
        out_shape=[jax.ShapeDtypeStruct((S, W), BF16), jax.ShapeDtypeStruct((S, W), F32)],
        scratch_shapes=[pltpu.VMEM((4, tq, tk), F32), pltpu.VMEM((4, tq, tk), BF16), pltpu.VMEM((2, tq, 128), F32),
                        pltpu.VMEM((4, tq, 128), F32), pltpu.VMEM((2, tq, 128), F32), pltpu.VMEM((2, tq, 128), F32),
                        pltpu.VMEM((2, S, PAIR), BF16)],
        compiler_params=_params(("parallel", "arbitrary")),
    )(qkv, qkv, qkv, fcx, fcr)


def _fox_bwd2(qkv, fcx, fcr, o, lse, do, n_pairs, base, tq):
    S = qkv.shape[0]
    tk = tq
    nq = S // tq
    scale = HEAD_DIM ** -0.5

    def body(q_ref, k_ref, v_ref, fq_ref, fk_ref, o_ref, lse_ref, do_ref,
             dq_ref, dk_ref, dv_ref, dfq_ref, dfk_ref,
             s_ref, dp_ref, p_ref, ds_ref, row_ref, dl_ref, dfq_acc, col_ref, dq_acc, dk_acc, dv_acc, ks_ref):
        i = pl.program_id(1)
        masks = _head_masks()

        @pl.when(i == 0)
        def _():
            dk_acc[...] = jnp.zeros_like(dk_acc)
            dv_acc[...] = jnp.zeros_like(dv_acc)
            dfk_ref[...] = jnp.zeros_like(dfk_ref)
            _fill_masked(ks_ref, k_ref, masks, mul=scale)

        q2 = q_ref[...] * scale
        do2 = do_ref[...]
        qm = [_sel(m, q2) for m in masks]
        dom = [_sel(m, do2) for m in masks]
        f0, f1 = _per_head(fq_ref[...], masks)
        l0, l1 = _per_head(lse_ref[...], masks)
        row_ref[0] = f0 - l0
        row_ref[1] = f1 - l1
        prod = do2.astype(F32) * o_ref[...].astype(F32)
        for h in range(2):
            dl_ref[h] = _rep(jnp.sum(jnp.where(masks[h], prod, 0.0), axis=-1, keepdims=True))
        dfq_acc[...] = jnp.zeros_like(dfq_acc)
        dq_acc[...] = jnp.zeros_like(dq_acc)

        def tile(j, diag, slot):
            off = pl.multiple_of(j * tk, tk)
            k2 = k_ref[pl.ds(off, tk), :]
            v2 = v_ref[pl.ds(off, tk), :]
            fk2 = fk_ref[0, j]
            for h in range(2):
                s_ref[2 * slot + h] = _dot(qm[h], k2, 1, 1)
                dp_ref[2 * slot + h] = _dot(dom[h], v2, 1, 1)
            for h in range(2):
                col_ref[2 * slot + h] = jnp.zeros((8, tk), F32)
                fk_row = fk2[h:h + 1, :]

                def dscore(r0, h=h, fk_row=fk_row):
                    rows = pl.ds(r0, RC)
                    p = jnp.exp(s_ref[2 * slot + h, rows, :] - fk_row + _wide(row_ref[h, rows, :], tk))
                    if diag:
                        p = jnp.where(_diag_mask(r0, tk, False), p, 0.0)
                    ds = p * (dp_ref[2 * slot + h, rows, :] - _wide(dl_ref[h, rows, :], tk))
                    p_ref[2 * slot + h, rows, :] = p.astype(BF16)
                    ds_ref[2 * slot + h, rows, :] = ds.astype(BF16)
                    dfq_acc[h, rows, :] += _rep(jnp.sum(ds, axis=-1, keepdims=True))
                    col_ref[2 * slot + h] += jnp.sum(ds.reshape(RC // 8, 8, tk), axis=0)
                _chunks(tq, dscore)
            keys = pl.ds(off, tk)
            dq_acc[...] += _dot(ds_ref[2 * slot], ks_ref[0, keys, :]) + _dot(ds_ref[2 * slot + 1], ks_ref[1, keys, :])
            dk_acc[pl.ds(off, tk), :] += _dot(ds_ref[2 * slot], qm[0], 0, 0) + _dot(ds_ref[2 * slot + 1], qm[1], 0, 0)
            dv_acc[pl.ds(off, tk), :] += _dot(p_ref[2 * slot], dom[0], 0, 0) + _dot(p_ref[2 * slot + 1], dom[1], 0, 0)
            for h in range(2):
                dfk_ref[0, j, h:h + 1, :] += jnp.sum(col_ref[2 * slot + h], axis=0, keepdims=True)

        _tiles(i, tile)
        dq_ref[...] = dq_acc[...].astype(BF16)
        dfq_ref[...] = jnp.where(masks[0], dfq_acc[0], dfq_acc[1])

        @pl.when(i == nq - 1)
        def _():
            dk_ref[...] = dk_acc[...].astype(BF16)
            dv_ref[...] = dv_acc[...].astype(BF16)

    W = n_pairs * PAIR
    in_specs = _fox_specs(S, tq, n_pairs, base) + [
        pl.BlockSpec((tq, PAIR), lambda p, i: (i, p)),
        pl.BlockSpec((tq, PAIR), lambda p, i: (i, p)),
        pl.BlockSpec((tq, PAIR), lambda p, i: (i, p))]
    out_specs = [pl.BlockSpec((tq, PAIR), lambda p, i: (i, p)),
                 pl.BlockSpec((S, PAIR), lambda p, i: (0, p)),
                 pl.BlockSpec((S, PAIR), lambda p, i: (0, p)),
                 pl.BlockSpec((tq, PAIR), lambda p, i: (i, p)),
                 pl.BlockSpec((1, nq, 8, tk), lambda p, i: (p, 0, 0, 0))]
    return pl.pallas_call(
        body, name="fox_bwd", grid=(n_pairs, nq), in_specs=in_specs, out_specs=out_specs,
        out_shape=[jax.ShapeDtypeStruct((S, W), BF16)] * 3
        + [jax.ShapeDtypeStruct((S, W), F32), jax.ShapeDtypeStruct((n_pairs, nq, 8, tk), F32)],
        scratch_shapes=[pltpu.VMEM((4, tq, tk), F32)] * 2 + [pltpu.VMEM((4, tq, tk), BF16)] * 2
        + [pltpu.VMEM((2, tq, 128), F32)] * 3 + [pltpu.VMEM((4, 8, tk), F32)]
        + [pltpu.VMEM((tq, PAIR), F32), pltpu.VMEM((S, PAIR), F32), pltpu.VMEM((S, PAIR), F32),
           pltpu.VMEM((2, S, PAIR), BF16)],
        compiler_params=_params(("parallel", "arbitrary")),
    )(qkv, qkv, qkv, fcx, fcr, o, lse, do)


def _swiglu_fwd(u2, wg, wu):
    S, D = u2.shape
    FF = wg.shape[1]
    tm, tn = _pick(S, (512, 256, 128)), _pick(FF, (512, 384, 256, 128))

    def body(u_ref, g_ref, w_ref, a_ref, b_ref, h_ref):
        u = u_ref[...]
        a = _dot(u, g_ref[...])
        b = _dot(u, w_ref[...])
        a_ref[...] = a.astype(BF16)
        b_ref[...] = b.astype(BF16)
        h_ref[...] = (a / (1.0 + jnp.exp(-a)) * b).astype(BF16)

    spec_o = pl.BlockSpec((tm, tn), lambda i, j: (i, j))
    return pl.pallas_call(
        body, name="swiglu_fwd", grid=(S // tm, FF // tn),
        in_specs=[pl.BlockSpec((tm, D), lambda i, j: (i, 0)),
                  pl.BlockSpec((D, tn), lambda i, j: (0, j)),
                  pl.BlockSpec((D, tn), lambda i, j: (0, j))],
        out_specs=[spec_o] * 3, out_shape=[jax.ShapeDtypeStruct((S, FF), BF16)] * 3,
        compiler_params=_params(("parallel", "parallel")),
    )(u2, wg, wu)


def _swiglu_bwd(dh, wd, a, b):
    S, D = dh.shape
    FF = wd.shape[0]
    tm, tn = _pick(S, (512, 256, 128)), _pick(FF, (512, 384, 256, 128))

    def body(dh_ref, w_ref, a_ref, b_ref, da_ref, db_ref):
        dhin = _dot(dh_ref[...], w_ref[...], 1, 1)
        av = a_ref[...].astype(F32)
        bv = b_ref[...].astype(F32)
        sig = 1.0 / (1.0 + jnp.exp(-av))
        da_ref[...] = (dhin * bv * (sig * (1.0 + av * (1.0 - sig)))).astype(BF16)
        db_ref[...] = (dhin * (av * sig)).astype(BF16)

    spec_o = pl.BlockSpec((tm, tn), lambda i, j: (i, j))
    return pl.pallas_call(
        body, name="swiglu_bwd", grid=(S // tm, FF // tn),
        in_specs=[pl.BlockSpec((tm, D), lambda i, j: (i, 0)),
                  pl.BlockSpec((tn, D), lambda i, j: (j, 0)), spec_o, spec_o],
        out_specs=[spec_o] * 2, out_shape=[jax.ShapeDtypeStruct((S, FF), BF16)] * 2,
        compiler_params=_params(("parallel", "parallel")),
    )(dh, wd, a, b)


def _gate_bwd(dmix, wo, g_sb, g_fx, y_sb, y_fx):
    S, D = dmix.shape
    tm, tn = _pick(S, (512, 256, 128)), _pick(D, (512, 256, 128))

    def body(dm_ref, w_ref, gs_ref, gf_ref, ys_ref, yf_ref, dys_ref, dyf_ref, dls_ref, dlf_ref, bs_ref, bf_ref):
        @pl.when(pl.program_id(1) == 0)
        def _():
            bs_ref[...] = jnp.zeros_like(bs_ref)
            bf_ref[...] = jnp.zeros_like(bf_ref)

        dmi = _dot(dm_ref[...], w_ref[...], 1, 1)
        gs, gf = gs_ref[...], gf_ref[...]
        dys_ref[...] = (dmi * gs).astype(BF16)
        dyf_ref[...] = (dmi * gf).astype(BF16)
        dls = dmi * ys_ref[...] * gs * (1.0 - gs)
        dlf = dmi * yf_ref[...] * gf * (1.0 - gf)
        dls_ref[...] = dls.astype(BF16)
        dlf_ref[...] = dlf.astype(BF16)
        bs_ref[0:1, :] += _colsum(dls)
        bf_ref[0:1, :] += _colsum(dlf)

    t = pl.BlockSpec((tm, tn), lambda j, i: (i, j))
    accs = pl.BlockSpec((8, tn), lambda j, i: (0, j))
    return pl.pallas_call(
        body, name="gate_bwd", grid=(D // tn, S // tm),
        in_specs=[pl.BlockSpec((tm, D), lambda j, i: (i, 0)),
                  pl.BlockSpec((tn, D), lambda j, i: (j, 0)), t, t, t, t],
        out_specs=[t, t, t, t, accs, accs],
        out_shape=[jax.ShapeDtypeStruct((S, D), BF16)] * 4 + [jax.ShapeDtypeStruct((8, D), F32)] * 2,
        compiler_params=_params(("parallel", "arbitrary")),
    )(dmix, wo, g_sb, g_fx, y_sb, y_fx)


def _local_step(x, target, ada8, lnp8, bg_sb, bg_fx, bf_pad, wqkv, wf, wgs, wgf, wsb, wfx, wo, wfg, wfu, wfd):
    S, D = x.shape
    W = wsb.shape[0]
    n_pairs = W // PAIR
    n_heads = W // HEAD_DIM
    ts = _pick(S, (256, 128))
    tq = _pick(S, (256, 128))

    u1 = _ln_mod(x, ada8, ts)
    qkv = _mm([(u1, wqkv)], 'nn', BF16, "in_qkv")
    f = _mm([(u1, wf)], 'nn', F32, "in_f")
    g_sb = _mm([(u1, wgs)], 'nn', F32, "in_gsb", bias=bg_sb, act='sigmoid')
    g_fx = _mm([(u1, wgf)], 'nn', F32, "in_gfx", bias=bg_fx, act='sigmoid')
    fc = _fgate_fwd(f, bf_pad, _pick(S, (512, 256, 128)))
    fch = fc[:, :n_heads]
    fcx = jnp.repeat(fch, HEAD_DIM, axis=1)
    nq = S // tq
    fcr = jnp.pad(fch.T.reshape(n_pairs, 2, nq, tq).transpose(0, 2, 1, 3),
                  ((0, 0), (0, 0), (0, 6), (0, 0)))
    o_sb, tot = _sb_fwd2(qkv, n_pairs, 0, tq)
    o_fx, lse = _fox_fwd2(qkv, fcx, fcr, n_pairs, 3 * n_pairs, tq)
    y_sb = _mm([(o_sb, wsb)], 'nn', F32, "out_sb")
    y_fx = _mm([(o_fx, wfx)], 'nn', F32, "out_fx")
    mix_in = _gate_mix(g_sb, g_fx, y_sb, y_fx, ts)
    mix = _mm([(mix_in, wo)], 'nn', F32, "out_o")
    x1, u2 = _post_attn(x, mix, ada8, lnp8, ts)
    a, b, hin = _swiglu_fwd(u2, wfg, wfu)
    h = _mm([(hin, wfd)], 'nn', F32, "ffn_down")
    dr2, dh, st_loss = _loss_head(x1, h, target, ada8, lnp8, ts)

    da, db = _swiglu_bwd(dh, wfd, a, b)
    g_wfd = _mm([(hin, dh)], 'tn', F32, "g_ffn_down")
    du2 = _mm([(da, wfg), (db, wfu)], 'nt', F32, "d_u2")
    g_wfg = _mm([(u2, da)], 'tn', F32, "g_ffn_gate")
    g_wfu = _mm([(u2, db)], 'tn', F32, "g_ffn_up")
    dr1, dmix, st_mid = _mid_bwd(du2, x1, dr2, mix, x, ada8, lnp8, ts)
    dys, dyf, dls, dlf, gb_sb, gb_fx = _gate_bwd(dmix, wo, g_sb, g_fx, y_sb, y_fx)
    g_wo = _mm([(mix_in, dmix)], 'tn', F32, "g_w_o")
    do_sb = _mm([(dys, wsb)], 'nt', BF16, "d_o_sb")
    do_fx = _mm([(dyf, wfx)], 'nt', BF16, "d_o_fx")
    g_wsb = _mm([(o_sb, dys)], 'tn', F32, "g_sb_out")
    g_wfx = _mm([(o_fx, dyf)], 'tn', F32, "g_fox_out")
    dq_s, dk_s, dv_s = _sb_bwd2(qkv, tot, do_sb, n_pairs, 0, tq)
    dq_f, dk_f, dv_f, dfq, dfk = _fox_bwd2(qkv, fcx, fcr, o_fx, lse, do_fx, n_pairs, 3 * n_pairs, tq)
    dfc = dfq[:, ::HEAD_DIM] - dfk[:, :, :2, :].transpose(0, 2, 1, 3).reshape(n_heads, S).T
    dfc = jnp.pad(dfc, ((0, 0), (0, 128 - n_heads)))
    df, gb_f = _fgate_bwd(dfc, f, bf_pad, _pick(S, (512, 256, 128)))
    grads = [dq_s, dk_s, dv_s, dq_f, dk_f, dv_f]
    du1 = _mm([(g, wqkv, W, 0, n) for n, g in enumerate(grads)] + [(df, wf), (dls, wgs), (dlf, wgf)],
              'nt', F32, "d_u1")
    g_wqkv = [_mm([(u1, g)], 'tn', F32, "g_in_%d" % n) for n, g in enumerate(grads)]
    g_wf = _mm([(u1, df)], 'tn', F32, "g_in_f")
    g_wgs = _mm([(u1, dls)], 'tn', F32, "g_in_gsb")
    g_wgf = _mm([(u1, dlf)], 'tn', F32, "g_in_gfx")
    gx, st_first = _first_bwd(du1, x, dr1, ada8, ts)

    wgrads = dict(qkv=g_wqkv, f=g_wf, gs=g_wgs, gf=g_wgf, sb=g_wsb, fx=g_wfx, o=g_wo, fg=g_wfg, fu=g_wfu, fd=g_wfd)
    stats = dict(loss=st_loss, mid=st_mid, first=st_first, gb_sb=gb_sb, gb_fx=gb_fx, gb_f=gb_f)
    return gx, wgrads, stats


def _position():
    x, y, c = lax.axis_index("x"), lax.axis_index("y"), lax.axis_index("c")
    return x, y, c, 4 * x + 2 * y + c


def _flip(x, y, c, k):
    px = 1 - x if k & 4 else x
    py = 1 - y if k & 2 else y
    pc = 1 - c if k & 1 else c
    return (px, py, pc), 4 * px + 2 * py + pc


def _all_gather_small(v, name):
    r, n = v.shape

    def body(x_ref, out_ref, send_sems, recv_sems, local_sem):
        x, y, c, me = _position()
        mine = pltpu.make_async_copy(x_ref, out_ref.at[me], local_sem)
        mine.start()
        sends = []
        for k in range(1, N_DEV):
            peer, _ = _flip(x, y, c, k)
            cp = pltpu.make_async_remote_copy(
                src_ref=x_ref, dst_ref=out_ref.at[me], send_sem=send_sems.at[k - 1], recv_sem=recv_sems.at[k - 1],
                device_id=peer, device_id_type=MESH)
            cp.start()
            sends.append(cp)
        for k in range(1, N_DEV):
            peer, slot = _flip(x, y, c, k)
            pltpu.make_async_remote_copy(
                src_ref=x_ref, dst_ref=out_ref.at[slot], send_sem=send_sems.at[k - 1], recv_sem=recv_sems.at[k - 1],
                device_id=peer, device_id_type=MESH).wait_recv()
        for cp in sends:
            cp.wait_send()
        mine.wait()

    return pl.pallas_call(
        body, name=name, out_shape=jax.ShapeDtypeStruct((N_DEV, r, n), v.dtype),
        in_specs=[pl.BlockSpec(memory_space=pltpu.VMEM)], out_specs=pl.BlockSpec(memory_space=pltpu.VMEM),
        scratch_shapes=[pltpu.SemaphoreType.DMA((N_DEV - 1,)), pltpu.SemaphoreType.DMA((N_DEV - 1,)),
                        pltpu.SemaphoreType.DMA],
    )(v)


def _all_gather_weights(packed):
    R, C = packed.shape

    def body(x_ref, out_ref, send_sems, recv_sems, local_sem):
        x, y, c, me = _position()
        sibling, sib_slot = _flip(x, y, c, 1)
        mine = pltpu.make_async_copy(x_ref, out_ref.at[me], local_sem)
        mine.start()

        def copy(k, slot, to, src=None):
            return pltpu.make_async_remote_copy(
                src_ref=out_ref.at[slot] if src is None else src, dst_ref=out_ref.at[slot],
                send_sem=send_sems.at[k], recv_sem=recv_sems.at[k], device_id=to, device_id_type=MESH)

        first = [copy(0, me, sibling, src=x_ref)]
        chips = (4, 2, 6)
        for n, k in enumerate(chips):
            peer, _ = _flip(x, y, c, k)
            first.append(copy(1 + n, me, peer, src=x_ref))
        for cp in first:
            cp.start()
        passed = []
        for n, k in enumerate(chips):
            peer, slot = _flip(x, y, c, k)
            copy(1 + n, slot, peer).wait_recv()
            cp = copy(4 + n, slot, sibling)
            cp.start()
            passed.append(cp)
        copy(0, sib_slot, sibling).wait_recv()
        for n, k in enumerate(chips):
            _, slot = _flip(x, y, c, k | 1)
            copy(4 + n, slot, sibling).wait_recv()
        for cp in first + passed:
            cp.wait_send()
        mine.wait()

    return pl.pallas_call(
        body, name="all_gather_weights", out_shape=jax.ShapeDtypeStruct((N_DEV, R, C), packed.dtype),
        in_specs=[pl.BlockSpec(memory_space=pl.ANY)], out_specs=pl.BlockSpec(memory_space=pl.ANY),
        scratch_shapes=[pltpu.SemaphoreType.DMA((7,)), pltpu.SemaphoreType.DMA((7,)), pltpu.SemaphoreType.DMA],
    )(packed)


def _exchange_grads(gpack):
    _, R, C = gpack.shape

    def body(g_ref, out_ref, send_sems, recv_sems, local_sem):
        x, y, c, me = _position()
        mine = pltpu.make_async_copy(g_ref.at[me], out_ref.at[me], local_sem)
        mine.start()
        sends = []
        for k in range(1, N_DEV):
            peer, slot = _flip(x, y, c, k)
            cp = pltpu.make_async_remote_copy(
                src_ref=g_ref.at[slot], dst_ref=out_ref.at[me], send_sem=send_sems.at[k - 1],
                recv_sem=recv_sems.at[k - 1], device_id=peer, device_id_type=MESH)
            cp.start()
            sends.append(cp)
        for k in range(1, N_DEV):
            peer, slot = _flip(x, y, c, k)
            pltpu.make_async_remote_copy(
                src_ref=g_ref.at[slot], dst_ref=out_ref.at[slot], send_sem=send_sems.at[k - 1],
                recv_sem=recv_sems.at[k - 1], device_id=peer, device_id_type=MESH).wait_recv()
        for cp in sends:
            cp.wait_send()
        mine.wait()

    return pl.pallas_call(
        body, name="exchange_grads", out_shape=jax.ShapeDtypeStruct((N_DEV, R, C), gpack.dtype),
        in_specs=[pl.BlockSpec(memory_space=pl.ANY)], out_specs=pl.BlockSpec(memory_space=pl.ANY),
        scratch_shapes=[pltpu.SemaphoreType.DMA((N_DEV - 1,)), pltpu.SemaphoreType.DMA((N_DEV - 1,)),
                        pltpu.SemaphoreType.DMA],
    )(gpack)


def _sum_slots(recv, name, tr):
    n, R, C = recv.shape

    def body(r_ref, o_ref):
        acc = r_ref[0].astype(F32)
        for s in range(1, n):
            acc = acc + r_ref[s].astype(F32)
        o_ref[...] = acc

    return pl.pallas_call(
        body, name=name, grid=(R // tr,), in_specs=[pl.BlockSpec((n, tr, C), lambda i: (0, i, 0))],
        out_specs=pl.BlockSpec((tr, C), lambda i: (i, 0)), out_shape=jax.ShapeDtypeStruct((R, C), F32),
        compiler_params=_params(("parallel",)),
    )(recv)


def _sum_stats(st_all, loss_row):
    n, r, D = st_all.shape

    def body(s_ref, o_ref, l_ref):
        acc = s_ref[0]
        for d in range(1, n):
            acc = acc + s_ref[d]
        o_ref[...] = acc
        l_ref[...] = jnp.zeros((8, 128), F32) + jnp.sum(acc[loss_row:loss_row + 1, :], axis=-1, keepdims=True)

    return pl.pallas_call(
        body, name="sum_stats", out_shape=[jax.ShapeDtypeStruct((r, D), F32), jax.ShapeDtypeStruct((8, 128), F32)],
    )(st_all)


def _adamw(w, g, m, v, name):
    R, C = w.shape
    tr = _pick(R, (256, 176, 128, 64, 32, 16, 8))
    c1 = 1.0 / (1.0 - ADAM_B1 ** ADAM_STEP)
    c2 = 1.0 / (1.0 - ADAM_B2 ** ADAM_STEP)

    def body(w_ref, g_ref, m_ref, v_ref, d_ref, nm_ref, nv_ref):
        gv = g_ref[...]
        nm = ADAM_B1 * m_ref[...] + (1.0 - ADAM_B1) * gv
        nv = ADAM_B2 * v_ref[...] + (1.0 - ADAM_B2) * (gv * gv)
        nm_ref[...] = nm
        nv_ref[...] = nv
        d_ref[...] = -ADAM_LR * ((nm * c1) / (jnp.sqrt(nv * c2) + ADAM_EPS) + ADAM_WD * w_ref[...])

    spec = pl.BlockSpec((tr, C), lambda i: (i, 0))
    return pl.pallas_call(
        body, name=name, grid=(R // tr,), in_specs=[spec] * 4, out_specs=[spec] * 3,
        out_shape=[jax.ShapeDtypeStruct((R, C), F32)] * 3, compiler_params=_params(("parallel",)),
    )(w, g, m, v)


def _round16(n):
    return -(-n // 16) * 16


def _pack_layout(D, in_cols, ff, W):
    parts = [("in", D * (in_cols // N_DEV) // D), ("fg", ff // N_DEV), ("fu", ff // N_DEV),
             ("sb", W * (D // N_DEV) // D), ("fx", W * (D // N_DEV) // D), ("o", D // N_DEV), ("fd", ff // N_DEV)]
    layout, off = {}, 0
    for nm, rows in parts:
        layout[nm] = (off, rows)
        off += _round16(rows)
    return layout, off


def _rows_of(a, D, rows):
    a = a.reshape(rows, D)
    return jnp.pad(a, ((0, _round16(rows) - rows), (0, 0)))


def _cols_to_dest(g, D):
    K, N = g.shape
    n = N // N_DEV
    return g.reshape(K, N_DEV, n).transpose(1, 0, 2).reshape(N_DEV, K * n // D, D)


def _cols_from_src(blocks, K, n):
    return blocks.reshape(N_DEV, K, n).transpose(1, 0, 2).reshape(K, N_DEV * n)


def _pad_rows16(a):
    rows = a.shape[1]
    return jnp.pad(a, ((0, 0), (0, _round16(rows) - rows), (0, 0)))


def kernel(x, c, w_ada, b_ada, w_in, b_gate, b_forget, w_sb_out, w_fox_out, w_o, ln1_g, ln1_b, w_ffn_gate, w_ffn_up, w_ffn_down, ln2_g, ln2_b, loss_target, m_w_ada, m_b_ada, m_w_in, m_b_gate, m_b_forget, m_w_sb_out, m_w_fox_out, m_w_o, m_ln1_g, m_ln1_b, m_w_ffn_gate, m_w_ffn_up, m_w_ffn_down, m_ln2_g, m_ln2_b, v_w_ada, v_b_ada, v_w_in, v_b_gate, v_b_forget, v_w_sb_out, v_w_fox_out, v_w_o, v_ln1_g, v_ln1_b, v_w_ffn_gate, v_w_ffn_up, v_w_ffn_down, v_ln2_g, v_ln2_b):
    S, D = x.shape[1], x.shape[2]
    W = w_sb_out.shape[1]
    n_heads = b_forget.shape[1]
    ff = w_ffn_down.shape[1] * N_DEV
    in_loc = w_in.shape[2]
    in_cols = in_loc * N_DEV
    ada_loc = w_ada.shape[2]
    n_cond = ada_loc * N_DEV // D
    assert w_ada.shape[0] == 1 and n_cond == 6 and in_cols == 6 * W + n_heads + 2 * D and n_heads <= 128
    me = 4 * lax.axis_index("x") + 2 * lax.axis_index("y") + lax.axis_index("c")

    c_all = _all_gather_small(c, "gather_c").reshape(N_DEV, D)
    c16 = jnp.pad(c_all, ((0, 16 - N_DEV), (0, 0)))
    b_cols = lax.dynamic_slice(b_ada, (0, me * ada_loc), (1, ada_loc))
    ada_cols = _mm([(c16, w_ada[0])], 'nn', F32, "ada_fwd", bias=b_cols, silu_a=True)[:N_DEV]
    ada_all = _all_gather_small(ada_cols, "gather_ada")
    ada_me = lax.dynamic_index_in_dim(ada_all, me, axis=1, keepdims=False)
    ada8 = jnp.pad(ada_me.reshape(n_cond, D), ((0, 8 - n_cond), (0, 0)))
    lnp8 = jnp.concatenate([ln1_g, ln1_b, ln2_g, ln2_b, jnp.zeros((4, D), F32)], axis=0)

    layout, R = _pack_layout(D, in_cols, ff, W)
    shards = dict(**{"in": w_in[0]}, fg=w_ffn_gate[0], fu=w_ffn_up[0], sb=w_sb_out[0], fx=w_fox_out[0], o=w_o[0],
                  fd=w_ffn_down[0])
    packed = jnp.concatenate([_rows_of(shards[nm].astype(BF16), D, layout[nm][1]) for nm in layout], axis=0)
    gathered = _all_gather_weights(packed)

    def part(nm):
        off, rows = layout[nm]
        return gathered[:, off:off + rows, :]

    w_in_full = _cols_from_src(part("in"), D, in_loc)
    wqkv = w_in_full[:, :6 * W]
    wf = jnp.pad(w_in_full[:, 6 * W:6 * W + n_heads], ((0, 0), (0, 128 - n_heads)))
    wgs = w_in_full[:, 6 * W + n_heads:6 * W + n_heads + D]
    wgf = w_in_full[:, 6 * W + n_heads + D:]
    wfg = _cols_from_src(part("fg"), D, ff // N_DEV)
    wfu = _cols_from_src(part("fu"), D, ff // N_DEV)
    wsb = _cols_from_src(part("sb"), W, D // N_DEV)
    wfx = _cols_from_src(part("fx"), W, D // N_DEV)
    wo = part("o").reshape(D, D)
    wfd = part("fd").reshape(ff, D)
    bf_pad = jnp.pad(b_forget, ((0, 0), (0, 128 - n_heads)))

    gx, wg, st = _local_step(x[0], loss_target[0], ada8, lnp8, b_gate[:, :D], b_gate[:, D:], bf_pad,
                             wqkv, wf, wgs, wgf, wsb, wfx, wo, wfg, wfu, wfd)

    g_in = jnp.concatenate(wg["qkv"] + [wg["f"][:, :n_heads], wg["gs"], wg["gf"]], axis=1)
    dest = {"in": _cols_to_dest(g_in, D), "fg": _cols_to_dest(wg["fg"], D), "fu": _cols_to_dest(wg["fu"], D),
            "sb": _cols_to_dest(wg["sb"], D), "fx": _cols_to_dest(wg["fx"], D),
            "o": wg["o"].reshape(N_DEV, D // N_DEV, D), "fd": wg["fd"].reshape(N_DEV, ff // N_DEV, D)}
    gpack = jnp.concatenate([_pad_rows16(dest[nm].astype(BF16)) for nm in layout], axis=1)
    recv = _exchange_grads(gpack)
    gsum = _sum_slots(recv, "sum_grads", _pick(R, (512, 656, 256, 128, 64, 16)))

    def gshard(nm, shape):
        off, rows = layout[nm]
        return gsum[off:off + rows].reshape(shape)

    zrow = jnp.zeros((1, D), F32)
    gb_f_row = jnp.pad(st["gb_f"][0:1], ((0, 0), (0, D - 128)))
    stats16 = jnp.concatenate([
        st["first"][1:2], st["first"][0:1], st["mid"][4:5], st["mid"][1:2], st["mid"][0:1], st["loss"][3:4],
        st["mid"][2:3], st["mid"][3:4], st["loss"][1:2], st["loss"][2:3], st["gb_sb"][0:1], st["gb_fx"][0:1],
        st["loss"][0:1], gb_f_row, zrow, zrow], axis=0)
    st_all = _all_gather_small(stats16, "gather_stats")
    st_sum, loss_blk = _sum_stats(st_all, 12)
    loss = loss_blk[0, 0]

    d_ada_all = st_all[:, :n_cond, :].reshape(N_DEV, n_cond * D)
    d_cols = lax.dynamic_slice(d_ada_all, (0, me * ada_loc), (N_DEV, ada_loc))
    d16 = jnp.pad(d_cols, ((0, 16 - N_DEV), (0, 0)))
    g_w_ada = _mm([(c16, d16)], 'tn', F32, "ada_wgrad", silu_a=True)

    small_w = jnp.concatenate([b_ada.reshape(n_cond, D), ln1_g, ln1_b, ln2_g, ln2_b, b_gate.reshape(2, D), zrow,
                               jnp.pad(b_forget, ((0, 0), (0, D - n_heads))), zrow, zrow], axis=0)
    small_m = jnp.concatenate([m_b_ada.reshape(n_cond, D), m_ln1_g, m_ln1_b, m_ln2_g, m_ln2_b, m_b_gate.reshape(2, D),
                               zrow, jnp.pad(m_b_forget, ((0, 0), (0, D - n_heads))), zrow, zrow], axis=0)
    small_v = jnp.concatenate([v_b_ada.reshape(n_cond, D), v_ln1_g, v_ln1_b, v_ln2_g, v_ln2_b, v_b_gate.reshape(2, D),
                               zrow, jnp.pad(v_b_forget, ((0, 0), (0, D - n_heads))), zrow, zrow], axis=0)
    sm = _adamw(small_w, st_sum, small_m, small_v, "adamw_small")

    def small(a, nm):
        if nm == "b_ada":
            return a[0:n_cond].reshape(1, n_cond * D)
        if nm == "b_gate":
            return a[10:12].reshape(1, 2 * D)
        if nm == "b_forget":
            return a[13:14, :n_heads]
        row = {"ln1_g": 6, "ln1_b": 7, "ln2_g": 8, "ln2_b": 9}[nm]
        return a[row:row + 1]

    big = {
        "w_ada": (w_ada[0], g_w_ada, m_w_ada[0], v_w_ada[0]),
        "w_in": (w_in[0], gshard("in", w_in.shape[1:]), m_w_in[0], v_w_in[0]),
        "w_sb_out": (w_sb_out[0], gshard("sb", w_sb_out.shape[1:]), m_w_sb_out[0], v_w_sb_out[0]),
        "w_fox_out": (w_fox_out[0], gshard("fx", w_fox_out.shape[1:]), m_w_fox_out[0], v_w_fox_out[0]),
        "w_o": (w_o[0], gshard("o", w_o.shape[1:]), m_w_o[0], v_w_o[0]),
        "w_ffn_gate": (w_ffn_gate[0], gshard("fg", w_ffn_gate.shape[1:]), m_w_ffn_gate[0], v_w_ffn_gate[0]),
        "w_ffn_up": (w_ffn_up[0], gshard("fu", w_ffn_up.shape[1:]), m_w_ffn_up[0], v_w_ffn_up[0]),
        "w_ffn_down": (w_ffn_down[0], gshard("fd", w_ffn_down.shape[1:]), m_w_ffn_down[0], v_w_ffn_down[0]),
    }
    order = ["w_ada", "b_ada", "w_in", "b_gate", "b_forget", "w_sb_out", "w_fox_out", "w_o", "ln1_g", "ln1_b",
             "w_ffn_gate", "w_ffn_up", "w_ffn_down", "ln2_g", "ln2_b"]
    grads, deltas, new_ms, new_vs = [], [], [], []
    for nm in order:
        if nm in big:
            w, g, m, v = big[nm]
            d, nm_, nv_ = _adamw(w, g, m, v, "adamw_" + nm)
            grads.append(g[None])
            deltas.append(d[None])
            new_ms.append(nm_[None])
            new_vs.append(nv_[None])
        else:
            grads.append(small(st_sum, nm))
            deltas.append(small(sm[0], nm))
            new_ms.append(small(sm[1], nm))
            new_vs.append(small(sm[2], nm))
    return (loss, gx[None], *grads, *deltas, *new_ms, *new_vs)
```

```python
import functools

import jax
import jax.numpy as jnp
import numpy as np
from jax import lax
from jax.experimental import pallas as pl
from jax.experimental.pallas import tpu as pltpu

F32 = jnp.float32
BF16 = jnp.bfloat16

HEAD_DIM = 64
PAIR = 2 * HEAD_DIM
LN_EPS = 1e-5
ALPHA = 2.0 ** 0.25
ADAM_LR, ADAM_B1, ADAM_B2, ADAM_EPS, ADAM_WD, ADAM_STEP = 0.001, 0.9, 0.999, 1e-08, 0.01, 10
N_DEV = 8
VMEM_LIMIT = 56 * 1024 * 1024
MESH = pl.DeviceIdType.MESH


def _dot(a, b, ca=1, cb=0):
    return lax.dot_general(a, b, (((ca,), (cb,)), ((), ())), preferred_element_type=F32)


def _pick(n, cands):
    for t in cands:
        if n % t == 0:
            return t
    return n


def _params(sem):
    return pltpu.CompilerParams(dimension_semantics=sem, vmem_limit_bytes=VMEM_LIMIT)


def _mm(pairs, mode, out_dtype, name, bias=None, act=None, silu_a=False,
        tm_c=(512, 256, 128), tn_c=(512, 384, 256, 128)):
    norm = []
    for p in pairs:
        a, b = p[0], p[1]
        kdim_a = a.shape[0] if mode == 'tn' else a.shape[1]
        K, ka, kb = (p[2], p[3], p[4]) if len(p) > 2 else (kdim_a, 0, 0)
        norm.append((a, b, K, ka, kb))
    a0, b0 = norm[0][0], norm[0][1]
    M = a0.shape[1] if mode == 'tn' else a0.shape[0]
    N = b0.shape[0] if mode == 'nt' else b0.shape[1]
    tm, tn = _pick(M, tm_c), _pick(N, tn_c)
    n_pairs = len(norm)

    in_specs, args = [], []
    for a, b, K, ka, kb in norm:
        if mode == 'tn':
            in_specs.append(pl.BlockSpec((K, tm), lambda i, j, ka=ka: (ka, i)))
        else:
            in_specs.append(pl.BlockSpec((tm, K), lambda i, j, ka=ka: (i, ka)))
        if mode == 'nt':
            in_specs.append(pl.BlockSpec((tn, K), lambda i, j, kb=kb: (j, kb)))
        else:
            in_specs.append(pl.BlockSpec((K, tn), lambda i, j, kb=kb: (kb, j)))
        args += [a, b]
    if bias is not None:
        in_specs.append(pl.BlockSpec((1, tn), lambda i, j: (0, j)))
        args.append(bias)

    ca = 0 if mode == 'tn' else 1
    cb = 1 if mode == 'nt' else 0

    def body(*refs):
        o_ref = refs[-1]
        acc = None
        for p in range(n_pairs):
            av = refs[2 * p][...]
            if silu_a:
                av = av / (1.0 + jnp.exp(-av))
            d = _dot(av.astype(BF16), refs[2 * p + 1][...].astype(BF16), ca, cb)
            acc = d if acc is None else acc + d
        if bias is not None:
            acc = acc + refs[2 * n_pairs][...]
        if act == 'sigmoid':
            acc = 1.0 / (1.0 + jnp.exp(-acc))
        o_ref[...] = acc.astype(out_dtype)

    return pl.pallas_call(
        body, name=name, grid=(M // tm, N // tn), in_specs=in_specs,
        out_specs=pl.BlockSpec((tm, tn), lambda i, j: (i, j)),
        out_shape=jax.ShapeDtypeStruct((M, N), out_dtype),
        compiler_params=_params(("parallel", "parallel")),
    )(*args)


def _rows_call(body, name, row_ins, vec_ins, row_outs, acc_outs, ts):
    S = row_ins[0].shape[0]
    in_specs = [pl.BlockSpec((ts, a.shape[1]), lambda i: (i, 0)) for a in row_ins]
    in_specs += [pl.BlockSpec(a.shape, lambda i: (0, 0)) for a in vec_ins]
    out_specs = [pl.BlockSpec((ts, c), lambda i: (i, 0)) for c, _ in row_outs]
    out_specs += [pl.BlockSpec(s, lambda i: (0, 0)) for s in acc_outs]
    out_shape = [jax.ShapeDtypeStruct((S, c), dt) for c, dt in row_outs]
    out_shape += [jax.ShapeDtypeStruct(s, F32) for s in acc_outs]
    return pl.pallas_call(
        body, name=name, grid=(S // ts,), in_specs=in_specs, out_specs=out_specs, out_shape=out_shape,
        compiler_params=_params(("arbitrary",)),
    )(*row_ins, *vec_ins)


def _ln_stats(v):
    mu = jnp.mean(v, axis=-1, keepdims=True)
    d = v - mu
    var = jnp.mean(d * d, axis=-1, keepdims=True)
    rstd = lax.rsqrt(var + LN_EPS)
    return d * rstd, rstd


def _ln_bwd(dxhat, xhat, rstd):
    m1 = jnp.mean(dxhat, axis=-1, keepdims=True)
    m2 = jnp.mean(dxhat * xhat, axis=-1, keepdims=True)
    return rstd * (dxhat - m1 - xhat * m2)


def _colsum(v):
    return jnp.sum(v, axis=0, keepdims=True)


def _ln_mod(x, ada8, ts):
    D = x.shape[1]

    def body(x_ref, v_ref, u_ref):
        xhat, _ = _ln_stats(x_ref[...])
        u_ref[...] = (xhat * (1.0 + v_ref[1:2, :]) + v_ref[0:1, :]).astype(BF16)

    return _rows_call(body, "ln_mod", [x], [ada8], [(D, BF16)], [], ts)[0]


def _gate_mix(g_sb, g_fx, y_sb, y_fx, ts):
    D = y_sb.shape[1]

    def body(gs, gf, ys, yf, o_ref):
        o_ref[...] = (gs[...] * ys[...] + gf[...] * yf[...]).astype(BF16)

    return _rows_call(body, "gate_mix", [g_sb, g_fx, y_sb, y_fx], [], [(D, BF16)], [], ts)[0]


def _post_attn(x, mix, ada8, lnp8, ts):
    D = x.shape[1]

    def body(x_ref, mix_ref, v_ref, p_ref, x1_ref, u2_ref):
        r1 = ALPHA * x_ref[...] + v_ref[2:3, :] * mix_ref[...]
        xhat, _ = _ln_stats(r1)
        x1 = xhat * p_ref[0:1, :] + p_ref[1:2, :]
        x1_ref[...] = x1
        xh1, _ = _ln_stats(x1)
        u2_ref[...] = (xh1 * (1.0 + v_ref[4:5, :]) + v_ref[3:4, :]).astype(BF16)

    return _rows_call(body, "post_attn", [x, mix], [ada8, lnp8], [(D, F32), (D, BF16)], [], ts)


def _loss_head(x1, h, target, ada8, lnp8, ts):
    D = x1.shape[1]

    def body(x1_ref, h_ref, t_ref, v_ref, p_ref, dr2_ref, dh_ref, st_ref):
        @pl.when(pl.program_id(0) == 0)
        def _():
            st_ref[...] = jnp.zeros_like(st_ref)

        hv = h_ref[...]
        g2 = v_ref[5:6, :]
        r2 = ALPHA * x1_ref[...] + g2 * hv
        xhat, rstd = _ln_stats(r2)
        y = xhat * p_ref[2:3, :] + p_ref[3:4, :]
        err = y - t_ref[...]
        dy = err * (1.0 / D)
        dr2 = _ln_bwd(dy * p_ref[2:3, :], xhat, rstd)
        dr2_ref[...] = dr2
        dh_ref[...] = (dr2 * g2).astype(BF16)
        st_ref[0:1, :] += _colsum(err * err) * (0.5 / D)
        st_ref[1:2, :] += _colsum(dy * xhat)
        st_ref[2:3, :] += _colsum(dy)
        st_ref[3:4, :] += _colsum(dr2 * hv)

    return _rows_call(body, "loss_head", [x1, h, target], [ada8, lnp8], [(D, F32), (D, BF16)], [(8, D)], ts)


def _mid_bwd(du2, x1, dr2, mix, x, ada8, lnp8, ts):
    D = x.shape[1]

    def body(du2_ref, x1_ref, dr2_ref, mix_ref, x_ref, v_ref, p_ref, dr1_ref, dmix_ref, st_ref):
        @pl.when(pl.program_id(0) == 0)
        def _():
            st_ref[...] = jnp.zeros_like(st_ref)

        du2v = du2_ref[...]
        xh1, rstd1 = _ln_stats(x1_ref[...])
        dx1 = ALPHA * dr2_ref[...] + _ln_bwd(du2v * (1.0 + v_ref[4:5, :]), xh1, rstd1)
        mixv = mix_ref[...]
        g1 = v_ref[2:3, :]
        r1 = ALPHA * x_ref[...] + g1 * mixv
        xhr, rstdr = _ln_stats(r1)
        dr1 = _ln_bwd(dx1 * p_ref[0:1, :], xhr, rstdr)
        dr1_ref[...] = dr1
        dmix_ref[...] = (dr1 * g1).astype(BF16)
        st_ref[0:1, :] += _colsum(du2v * xh1)
        st_ref[1:2, :] += _colsum(du2v)
        st_ref[2:3, :] += _colsum(dx1 * xhr)
        st_ref[3:4, :] += _colsum(dx1)
        st_ref[4:5, :] += _colsum(dr1 * mixv)

    return _rows_call(body, "mid_bwd", [du2, x1, dr2, mix, x], [ada8, lnp8], [(D, F32), (D, BF16)], [(8, D)], ts)


def _first_bwd(du1, x, dr1, ada8, ts):
    D = x.shape[1]

    def body(du1_ref, x_ref, dr1_ref, v_ref, gx_ref, st_ref):
        @pl.when(pl.program_id(0) == 0)
        def _():
            st_ref[...] = jnp.zeros_like(st_ref)

        du1v = du1_ref[...]
        xh0, rstd0 = _ln_stats(x_ref[...])
        gx_ref[...] = ALPHA * dr1_ref[...] + _ln_bwd(du1v * (1.0 + v_ref[1:2, :]), xh0, rstd0)
        st_ref[0:1, :] += _colsum(du1v * xh0)
        st_ref[1:2, :] += _colsum(du1v)

    return _rows_call(body, "first_bwd", [du1, x, dr1], [ada8], [(D, F32)], [(8, D)], ts)


def _split3(v):
    hi = v.astype(BF16)
    r = v - hi.astype(F32)
    mid = r.astype(BF16)
    lo = (r - mid.astype(F32)).astype(BF16)
    return hi, mid, lo


def _fgate_fwd(f, bf_pad, tb):
    S = f.shape[0]

    def body(f_ref, b_ref, fc_ref, carry):
        @pl.when(pl.program_id(0) == 0)
        def _():
            carry[...] = jnp.zeros_like(carry)

        z = f_ref[...] + b_ref[...]
        ls = jnp.minimum(z, 0.0) - jnp.log(1.0 + jnp.exp(-jnp.abs(z)))
        r = lax.broadcasted_iota(jnp.int32, (tb, tb), 0)
        c = lax.broadcasted_iota(jnp.int32, (tb, tb), 1)
        tri = (c <= r).astype(BF16)
        hi, mid, lo = _split3(ls)
        cs = _dot(tri, hi) + _dot(tri, mid) + _dot(tri, lo) + carry[...]
        fc_ref[...] = cs
        carry[...] = cs[tb - 1:tb, :]

    return pl.pallas_call(
        body, name="fgate_fwd", grid=(S // tb,),
        in_specs=[pl.BlockSpec((tb, 128), lambda i: (i, 0)), pl.BlockSpec((1, 128), lambda i: (0, 0))],
        out_specs=pl.BlockSpec((tb, 128), lambda i: (i, 0)),
        out_shape=jax.ShapeDtypeStruct((S, 128), F32),
        scratch_shapes=[pltpu.VMEM((1, 128), F32)],
        compiler_params=_params(("arbitrary",)),
    )(f, bf_pad)


def _fgate_bwd(dfc, f, bf_pad, tb):
    S = f.shape[0]
    nb = S // tb

    def body(d_ref, f_ref, b_ref, df_ref, gb_ref, carry):
        @pl.when(pl.program_id(0) == 0)
        def _():
            carry[...] = jnp.zeros_like(carry)
            gb_ref[...] = jnp.zeros_like(gb_ref)

        r = lax.broadcasted_iota(jnp.int32, (tb, tb), 0)
        c = lax.broadcasted_iota(jnp.int32, (tb, tb), 1)
        tri = (c >= r).astype(BF16)
        hi, mid, lo = _split3(d_ref[...])
        rs = _dot(tri, hi) + _dot(tri, mid) + _dot(tri, lo) + carry[...]
        carry[...] = rs[0:1, :]
        z = f_ref[...] + b_ref[...]
        df = rs * (1.0 / (1.0 + jnp.exp(z)))
        df_ref[...] = df
        gb_ref[0:1, :] += _colsum(df)

    return pl.pallas_call(
        body, name="fgate_bwd", grid=(nb,),
        in_specs=[pl.BlockSpec((tb, 128), lambda i: (nb - 1 - i, 0)),
                  pl.BlockSpec((tb, 128), lambda i: (nb - 1 - i, 0)),
                  pl.BlockSpec((1, 128), lambda i: (0, 0))],
        out_specs=[pl.BlockSpec((tb, 128), lambda i: (nb - 1 - i, 0)), pl.BlockSpec((8, 128), lambda i: (0, 0))],
        out_shape=[jax.ShapeDtypeStruct((S, 128), F32), jax.ShapeDtypeStruct((8, 128), F32)],
        scratch_shapes=[pltpu.VMEM((1, 128), F32)],
        compiler_params=_params(("arbitrary",)),
    )(dfc, f, bf_pad)


def _split2(v):
    hi = v.astype(BF16)
    lo = (v - hi.astype(F32)).astype(BF16)
    return hi, lo


def _head_masks():
    lane = lax.broadcasted_iota(jnp.int32, (1, PAIR), 1)
    m0 = lane < HEAD_DIM
    return m0, jnp.logical_not(m0)


def _sel(mask, v):
    return jnp.where(mask, v, jnp.zeros_like(v))


def _softplus(z):
    return jnp.maximum(z, 0.0) + jnp.log(1.0 + jnp.exp(-jnp.abs(z)))


def _qkv_specs(S, tq, n_pairs, base):
    return [pl.BlockSpec((tq, PAIR), lambda p, i: (i, base + p)),
            pl.BlockSpec((S, PAIR), lambda p, i: (0, base + n_pairs + p)),
            pl.BlockSpec((S, PAIR), lambda p, i: (0, base + 2 * n_pairs + p))]


def _sb_fwd(qkv, n_pairs, base, tq):
    S = qkv.shape[0]
    tk = tq
    scale = HEAD_DIM ** -0.5

    def body(q_ref, k_ref, v_ref, o_ref, t_ref, acc_ref):
        i = pl.program_id(1)
        masks = _head_masks()
        q2 = q_ref[...]
        qm = [_sel(m, q2) for m in masks]
        rowpos = i * tq + lax.broadcasted_iota(jnp.int32, (tq, tk), 0)
        colin = lax.broadcasted_iota(jnp.int32, (tq, tk), 1)
        upper = (lax.broadcasted_iota(jnp.int32, (tk, tk), 0) > lax.broadcasted_iota(jnp.int32, (tk, tk), 1)).astype(BF16)
        acc_ref[...] = jnp.zeros_like(acc_ref)

        def step(jj, carry):
            j = i - jj
            off = pl.multiple_of(j * tk, tk)
            k2 = k_ref[pl.ds(off, tk), :]
            v2 = v_ref[pl.ds(off, tk), :]
            mask = (j * tk + colin) < rowpos
            out = None
            new = []
            for h in heads:
                z = _dot(qm[h], k2, 1, 1) * scale
                sp = _softplus(z)
                lg = jnp.where(mask, -sp, 0.0)
                hi, lo = _split2(lg)
                suf = _dot(hi, upper) + _dot(lo, upper)
                a = jnp.where(mask, jnp.exp(z - sp + suf + carry[h]), 0.0)
                d = _dot(a.astype(BF16), _sel(masks[h], v2))
                out = d if out is None else out + d
                new.append(carry[h] + jnp.sum(lg, axis=-1, keepdims=True))
            acc_ref[...] += out
            return tuple(new)

        zero = jnp.zeros((tq, 1), F32)
        r0, r1 = lax.fori_loop(0, i + 1, step, (zero, zero))
        o_ref[...] = acc_ref[...].astype(BF16)
        t_ref[...] = jnp.where(masks[0], r0, r1)

    W = n_pairs * PAIR
    return pl.pallas_call(
        body, name="sb_fwd", grid=(n_pairs, S // tq), in_specs=_qkv_specs(S, tq, n_pairs, base),
        out_specs=[pl.BlockSpec((tq, PAIR), lambda p, i: (i, p)), pl.BlockSpec((tq, PAIR), lambda p, i: (i, p))],
        out_shape=[jax.ShapeDtypeStruct((S, W), BF16), jax.ShapeDtypeStruct((S, W), F32)],
        scratch_shapes=[pltpu.VMEM((tq, PAIR), F32)],
        compiler_params=_params(("parallel", "arbitrary")),
    )(qkv, qkv, qkv)


def _sb_bwd(qkv, tot, do, n_pairs, base, tq):
    S = qkv.shape[0]
    tk = tq
    nq = S // tq
    scale = HEAD_DIM ** -0.5

    def body(q_ref, k_ref, v_ref, t_ref, do_ref, dq_ref, dk_ref, dv_ref, dq_acc, dk_acc, dv_acc):
        i = pl.program_id(1)
        masks = _head_masks()

        @pl.when(i == 0)
        def _():
            dk_acc[...] = jnp.zeros_like(dk_acc)
            dv_acc[...] = jnp.zeros_like(dv_acc)

        q2 = q_ref[...]
        do2 = do_ref[...]
        qm = [_sel(m, q2) for m in masks]
        dom = [_sel(m, do2) for m in masks]
        t2 = t_ref[...]
        tot_h = [t2[:, 0:1], t2[:, HEAD_DIM:HEAD_DIM + 1]]
        rowpos = i * tq + lax.broadcasted_iota(jnp.int32, (tq, tk), 0)
        colin = lax.broadcasted_iota(jnp.int32, (tq, tk), 1)
        r_i = lax.broadcasted_iota(jnp.int32, (tk, tk), 0)
        c_i = lax.broadcasted_iota(jnp.int32, (tk, tk), 1)
        upper = (r_i > c_i).astype(BF16)
        lower = (r_i < c_i).astype(BF16)
        dq_acc[...] = jnp.zeros_like(dq_acc)

        def step(j, carry):
            off = pl.multiple_of(j * tk, tk)
            k2 = k_ref[pl.ds(off, tk), :]
            v2 = v_ref[pl.ds(off, tk), :]
            mask = (j * tk + colin) < rowpos
            dq = None
            dk = None
            dv = None
            new = []
            for h in heads:
                cum_l, cum_g = carry[2 * h], carry[2 * h + 1]
                z = _dot(qm[h], k2, 1, 1) * scale
                sp = _softplus(z)
                lg = jnp.where(mask, -sp, 0.0)
                hi, lo = _split2(lg)
                suf = _dot(hi, upper) + _dot(lo, upper)
                row_l = jnp.sum(lg, axis=-1, keepdims=True)
                later = tot_h[h] - cum_l - row_l
                a = jnp.where(mask, jnp.exp(z - sp + suf + later), 0.0)
                da = _dot(dom[h], v2, 1, 1)
                g = da * a
                ghi, glo = _split2(g)
                pre = _dot(ghi, lower) + _dot(glo, lower) + cum_g
                one_m_beta = jnp.exp(-sp)
                dz = jnp.where(mask, g * one_m_beta - (1.0 - one_m_beta) * pre, 0.0)
                dzb = (dz * scale).astype(BF16)
                d1 = _dot(dzb, _sel(masks[h], k2))
                d2 = _dot(dzb, qm[h], 0, 0)
                d3 = _dot(a.astype(BF16), dom[h], 0, 0)
                dq = d1 if dq is None else dq + d1
                dk = d2 if dk is None else dk + d2
                dv = d3 if dv is None else dv + d3
                new += [cum_l + row_l, cum_g + jnp.sum(g, axis=-1, keepdims=True)]
            dq_acc[...] += dq
            dk_acc[pl.ds(off, tk), :] += dk
            dv_acc[pl.ds(off, tk), :] += dv
            return tuple(new)

        zero = jnp.zeros((tq, 1), F32)
        lax.fori_loop(0, i + 1, step, (zero, zero, zero, zero))
        dq_ref[...] = dq_acc[...].astype(BF16)

        @pl.when(i == nq - 1)
        def _():
            dk_ref[...] = dk_acc[...].astype(BF16)
            dv_ref[...] = dv_acc[...].astype(BF16)

    W = n_pairs * PAIR
    in_specs = _qkv_specs(S, tq, n_pairs, base) + [
        pl.BlockSpec((tq, PAIR), lambda p, i: (i, p)),
        pl.BlockSpec((tq, PAIR), lambda p, i: (i, p))]
    out_specs = [pl.BlockSpec((tq, PAIR), lambda p, i: (i, p)),
                 pl.BlockSpec((S, PAIR), lambda p, i: (0, p)),
                 pl.BlockSpec((S, PAIR), lambda p, i: (0, p))]
    dq, dk, dv = pl.pallas_call(
        body, name="sb_bwd", grid=(n_pairs, nq), in_specs=in_specs, out_specs=out_specs,
        out_shape=[jax.ShapeDtypeStruct((S, W), BF16)] * 3,
        scratch_shapes=[pltpu.VMEM((tq, PAIR), F32), pltpu.VMEM((S, PAIR), F32), pltpu.VMEM((S, PAIR), F32)],
        compiler_params=_params(("parallel", "arbitrary")),
    )(qkv, qkv, qkv, tot, do)
    return dq, dk, dv


NEG = -1e30


def _fox_specs(S, tq, n_pairs, base):
    return _qkv_specs(S, tq, n_pairs, base) + [
        pl.BlockSpec((tq, PAIR), lambda p, i: (i, p)),
        pl.BlockSpec((1, S // tq, 8, tq), lambda p, i: (p, 0, 0, 0))]


def _fox_fwd(qkv, fcx, fcr, n_pairs, base, tq):
    S = qkv.shape[0]
    tk = tq
    scale = HEAD_DIM ** -0.5

    def body(q_ref, k_ref, v_ref, fq_ref, fk_ref, o_ref, lse_ref, acc_ref):
        i = pl.program_id(1)
        masks = _head_masks()
        q2 = q_ref[...]
        qm = [_sel(m, q2) for m in masks]
        fq2 = fq_ref[...]
        fq = [fq2[:, 0:1], fq2[:, HEAD_DIM:HEAD_DIM + 1]]
        rowpos = i * tq + lax.broadcasted_iota(jnp.int32, (tq, tk), 0)
        colin = lax.broadcasted_iota(jnp.int32, (tq, tk), 1)
        acc_ref[...] = jnp.zeros_like(acc_ref)

        def step(j, carry):
            off = pl.multiple_of(j * tk, tk)
            k2 = k_ref[pl.ds(off, tk), :]
            v2 = v_ref[pl.ds(off, tk), :]
            fk2 = fk_ref[0, j]
            mask = (j * tk + colin) <= rowpos
            out = None
            new = []
            alphas = []
            for h in heads:
                m_old, l_old = carry[2 * h], carry[2 * h + 1]
                s = _dot(qm[h], k2, 1, 1) * scale + fq[h] - fk2[h:h + 1, :]
                s = jnp.where(mask, s, NEG)
                m_new = jnp.maximum(m_old, jnp.max(s, axis=-1, keepdims=True))
                p = jnp.exp(s - m_new)
                alpha = jnp.exp(m_old - m_new)
                alphas.append(alpha)
                d = _dot(p.astype(BF16), _sel(masks[h], v2))
                out = d if out is None else out + d
                new += [m_new, alpha * l_old + jnp.sum(p, axis=-1, keepdims=True)]
            acc_ref[...] = acc_ref[...] * jnp.where(masks[0], alphas[0], alphas[1]) + out
            return tuple(new)

        zero = jnp.zeros((tq, 1), F32)
        neg = jnp.full((tq, 1), NEG, F32)
        m0, l0, m1, l1 = lax.fori_loop(0, i + 1, step, (neg, zero, neg, zero))
        o_ref[...] = (acc_ref[...] / jnp.where(masks[0], l0, l1)).astype(BF16)
        lse_ref[...] = jnp.where(masks[0], m0 + jnp.log(l0), m1 + jnp.log(l1))

    W = n_pairs * PAIR
    return pl.pallas_call(
        body, name="fox_fwd", grid=(n_pairs, S // tq), in_specs=_fox_specs(S, tq, n_pairs, base),
        out_specs=[pl.BlockSpec((tq, PAIR), lambda p, i: (i, p)), pl.BlockSpec((tq, PAIR), lambda p, i: (i, p))],
        out_shape=[jax.ShapeDtypeStruct((S, W), BF16), jax.ShapeDtypeStruct((S, W), F32)],
        scratch_shapes=[pltpu.VMEM((tq, PAIR), F32)],
        compiler_params=_params(("parallel", "arbitrary")),
    )(qkv, qkv, qkv, fcx, fcr)


def _fox_bwd(qkv, fcx, fcr, o, lse, do, n_pairs, base, tq):
    S = qkv.shape[0]
    tk = tq
    nq = S // tq
    scale = HEAD_DIM ** -0.5

    def body(q_ref, k_ref, v_ref, fq_ref, fk_ref, o_ref, lse_ref, do_ref,
             dq_ref, dk_ref, dv_ref, dfq_ref, dfk_ref, dq_acc, dk_acc, dv_acc):
        i = pl.program_id(1)
        masks = _head_masks()

        @pl.when(i == 0)
        def _():
            dk_acc[...] = jnp.zeros_like(dk_acc)
            dv_acc[...] = jnp.zeros_like(dv_acc)
            dfk_ref[...] = jnp.zeros_like(dfk_ref)

        q2 = q_ref[...]
        do2 = do_ref[...]
        qm = [_sel(m, q2) for m in masks]
        dom = [_sel(m, do2) for m in masks]
        fq2 = fq_ref[...]
        fq = [fq2[:, 0:1], fq2[:, HEAD_DIM:HEAD_DIM + 1]]
        l2 = lse_ref[...]
        lse_h = [l2[:, 0:1], l2[:, HEAD_DIM:HEAD_DIM + 1]]
        prod = do2.astype(F32) * o_ref[...].astype(F32)
        delta = [jnp.sum(jnp.where(m, prod, 0.0), axis=-1, keepdims=True) for m in masks]
        rowpos = i * tq + lax.broadcasted_iota(jnp.int32, (tq, tk), 0)
        colin = lax.broadcasted_iota(jnp.int32, (tq, tk), 1)
        dq_acc[...] = jnp.zeros_like(dq_acc)

        def step(j, carry):
            off = pl.multiple_of(j * tk, tk)
            k2 = k_ref[pl.ds(off, tk), :]
            v2 = v_ref[pl.ds(off, tk), :]
            fk2 = fk_ref[0, j]
            mask = (j * tk + colin) <= rowpos
            dq = None
            dk = None
            dv = None
            new = []
            dfk_rows = []
            for h in heads:
                s = _dot(qm[h], k2, 1, 1) * scale + fq[h] - fk2[h:h + 1, :]
                p = jnp.where(mask, jnp.exp(s - lse_h[h]), 0.0)
                dp = _dot(dom[h], v2, 1, 1)
                ds = p * (dp - delta[h])
                dsb = (ds * scale).astype(BF16)
                d1 = _dot(dsb, _sel(masks[h], k2))
                d2 = _dot(dsb, qm[h], 0, 0)
                d3 = _dot(p.astype(BF16), dom[h], 0, 0)
                dq = d1 if dq is None else dq + d1
                dk = d2 if dk is None else dk + d2
                dv = d3 if dv is None else dv + d3
                new.append(carry[h] + jnp.sum(ds, axis=-1, keepdims=True))
                dfk_rows.append(jnp.sum(ds, axis=0, keepdims=True))
            dq_acc[...] += dq
            dk_acc[pl.ds(off, tk), :] += dk
            dv_acc[pl.ds(off, tk), :] += dv
            dfk_ref[0, j, 0:1, :] += dfk_rows[0]
            dfk_ref[0, j, 1:2, :] += dfk_rows[1]
            return tuple(new)

        zero = jnp.zeros((tq, 1), F32)
        r0, r1 = lax.fori_loop(0, i + 1, step, (zero, zero))
        dq_ref[...] = dq_acc[...].astype(BF16)
        dfq_ref[...] = jnp.where(masks[0], r0, r1)

        @pl.when(i == nq - 1)
        def _():
            dk_ref[...] = dk_acc[...].astype(BF16)
            dv_ref[...] = dv_acc[...].astype(BF16)

    W = n_pairs * PAIR
    in_specs = _fox_specs(S, tq, n_pairs, base) + [
        pl.BlockSpec((tq, PAIR), lambda p, i: (i, p)),
        pl.BlockSpec((tq, PAIR), lambda p, i: (i, p)),
        pl.BlockSpec((tq, PAIR), lambda p, i: (i, p))]
    out_specs = [pl.BlockSpec((tq, PAIR), lambda p, i: (i, p)),
                 pl.BlockSpec((S, PAIR), lambda p, i: (0, p)),
                 pl.BlockSpec((S, PAIR), lambda p, i: (0, p)),
                 pl.BlockSpec((tq, PAIR), lambda p, i: (i, p)),
                 pl.BlockSpec((1, nq, 8, tk), lambda p, i: (p, 0, 0, 0))]
    return pl.pallas_call(
        body, name="fox_bwd", grid=(n_pairs, nq), in_specs=in_specs, out_specs=out_specs,
        out_shape=[jax.ShapeDtypeStruct((S, W), BF16)] * 3
        + [jax.ShapeDtypeStruct((S, W), F32), jax.ShapeDtypeStruct((n_pairs, nq, 8, tk), F32)],
        scratch_shapes=[pltpu.VMEM((tq, PAIR), F32), pltpu.VMEM((S, PAIR), F32), pltpu.VMEM((S, PAIR), F32)],
        compiler_params=_params(("parallel", "arbitrary")),
    )(qkv, qkv, qkv, fcx, fcr, o, lse, do)


RC = 32


def _chunks(n_rows, fn):
    for ci in range(n_rows // RC):
        fn(ci * RC)


def _wide(v, tk):
    return v if tk == 128 else jnp.tile(v, (1, tk // 128))


def _rep(col):
    return jnp.broadcast_to(col, (col.shape[0], 128))


def _per_head(blk, masks):
    sw = pltpu.roll(blk, HEAD_DIM, axis=1)
    return jnp.where(masks[0], blk, sw), jnp.where(masks[0], sw, blk)


def _fill_masked(dst_ref, src_ref, masks, mul=None, ones_lane=None):
    v = src_ref[...]
    if mul is not None:
        v = v * mul
    lane = lax.broadcasted_iota(jnp.int32, (1, PAIR), 1)
    for h in range(2):
        m = _sel(masks[h], v)
        if ones_lane is not None:
            m = jnp.where(lane == ones_lane[h], jnp.ones_like(m), m)
        dst_ref[h] = m


def _tri(tk, cmp):
    r = lax.broadcasted_iota(jnp.int32, (tk, tk), 0)
    c = lax.broadcasted_iota(jnp.int32, (tk, tk), 1)
    return cmp(r, c).astype(BF16)


def _diag_mask(r0, tk, strict):
    row = r0 + lax.broadcasted_iota(jnp.int32, (RC, tk), 0)
    col = lax.broadcasted_iota(jnp.int32, (RC, tk), 1)
    return (col < row) if strict else (col <= row)


def _staggered(bodies):
    active, waiting = [], list(bodies)
    while waiting or active:
        if waiting:
            active.append(waiting.pop(0))
        for g in list(active):
            try:
                next(g)
            except StopIteration:
                active.remove(g)


def _streams(tile, j, diag, slot):
    return [tile(j, diag, slot, (0, 1))]


def _tiles(i, tile):
    def step(jj, carry):
        _staggered(_streams(tile, 2 * jj, False, 0) + _streams(tile, 2 * jj + 1, False, 1))
        return carry
    lax.fori_loop(0, i // 2, step, 0)

    @pl.when(i % 2 == 1)
    def _():
        _staggered(_streams(tile, i - 1, False, 0))
    _staggered(_streams(tile, i, True, 1))


def _tiles_reversed(i, tile):
    _staggered(_streams(tile, i, True, 1))

    def step(jj, carry):
        _staggered(_streams(tile, i - 1 - 2 * jj, False, 0) + _streams(tile, i - 2 - 2 * jj, False, 1))
        return carry
    lax.fori_loop(0, i // 2, step, 0)

    @pl.when(i % 2 == 1)
    def _():
        _staggered(_streams(tile, 0, False, 0))


def _sb_fwd2(qkv, n_pairs, base, tq):
    S = qkv.shape[0]
    tk = tq
    scale = HEAD_DIM ** -0.5

    def body(q_ref, k_ref, v_ref, o_ref, t_ref, z_ref, hi_ref, lo_ref, suf_ref, p_ref, r_ref, acc_ref, vm_ref):
        i = pl.program_id(1)
        masks = _head_masks()

        @pl.when(i == 0)
        def _():
            _fill_masked(vm_ref, v_ref, masks)

        q2 = q_ref[...] * scale
        qm = [_sel(m, q2) for m in masks]
        incl = _tri(tk, lambda r, c: r >= c)
        r_ref[...] = jnp.zeros_like(r_ref)
        acc_ref[...] = jnp.zeros_like(acc_ref)

        def tile(j, diag, slot, heads):
            off = pl.multiple_of(j * tk, tk)
            k2 = k_ref[pl.ds(off, tk), :]
            v2 = v_ref[pl.ds(off, tk), :]
            for h in heads:
                z_ref[2 * slot + h] = _dot(qm[h], k2, 1, 1)
            yield
            for h in heads:
                def split(r0, h=h):
                    rows = pl.ds(r0, RC)
                    lg = -_softplus(z_ref[2 * slot + h, rows, :])
                    if diag:
                        lg = jnp.where(_diag_mask(r0, tk, True), lg, 0.0)
                    hi, lo = _split2(lg)
                    hi_ref[2 * slot + h, rows, :] = hi
                    lo_ref[2 * slot + h, rows, :] = lo
                _chunks(tq, split)
            yield
            for h in heads:
                suf_ref[2 * slot + h] = _dot(hi_ref[2 * slot + h], incl) + _dot(lo_ref[2 * slot + h], incl)
            yield
            for h in heads:
                def weights(r0, h=h):
                    rows = pl.ds(r0, RC)
                    a = jnp.exp(z_ref[2 * slot + h, rows, :] + suf_ref[2 * slot + h, rows, :] + _wide(r_ref[h, rows, :], tk))
                    if diag:
                        a = jnp.where(_diag_mask(r0, tk, True), a, 0.0)
                    p_ref[2 * slot + h, rows, :] = a.astype(BF16)
                _chunks(tq, weights)
            yield
            keys = pl.ds(off, tk)
            for h in heads:
                acc_ref[...] += _dot(p_ref[2 * slot + h], vm_ref[h, keys, :])
            for h in heads:
                r_ref[h] += _rep(suf_ref[2 * slot + h, :, 0:1])

        _tiles_reversed(i, tile)
        o_ref[...] = acc_ref[...].astype(BF16)
        t_ref[...] = jnp.where(masks[0], r_ref[0], r_ref[1])

    W = n_pairs * PAIR
    return pl.pallas_call(
        body, name="sb_fwd", grid=(n_pairs, S // tq), in_specs=_qkv_specs(S, tq, n_pairs, base),
        out_specs=[pl.BlockSpec((tq, PAIR), lambda p, i: (i, p)), pl.BlockSpec((tq, PAIR), lambda p, i: (i, p))],
        out_shape=[jax.ShapeDtypeStruct((S, W), BF16), jax.ShapeDtypeStruct((S, W), F32)],
        scratch_shapes=[pltpu.VMEM((4, tq, tk), F32), pltpu.VMEM((4, tq, tk), BF16), pltpu.VMEM((4, tq, tk), BF16),
                        pltpu.VMEM((4, tq, tk), F32), pltpu.VMEM((4, tq, tk), BF16), pltpu.VMEM((2, tq, 128), F32),
                        pltpu.VMEM((tq, PAIR), F32), pltpu.VMEM((2, S, PAIR), BF16)],
        compiler_params=_params(("parallel", "arbitrary")),
    )(qkv, qkv, qkv)


def _sb_bwd2(qkv, tot, do, after, n_pairs, base, tq):
    S = qkv.shape[0]
    tk = tq
    nq = S // tq
    scale = HEAD_DIM ** -0.5

    def body(q_ref, k_ref, v_ref, t_ref, do_ref, after_ref, dq_ref, dk_ref, dv_ref,
             z_ref, g_ref, omb_ref, cum_ref, hi_ref, lo_ref, a_ref, dz_ref,
             later_ref, cl_ref, cg_ref, tot_ref, dq_acc, dk_acc, dv_acc, ks_ref):
        i = pl.program_id(1)
        masks = _head_masks()

        @pl.when(i == 0)
        def _():
            dk_acc[...] = jnp.zeros_like(dk_acc)
            dv_acc[...] = jnp.zeros_like(dv_acc)
            _fill_masked(ks_ref, k_ref, masks, mul=scale)

        q2 = q_ref[...] * scale
        do2 = do_ref[...]
        qm = [_sel(m, q2) for m in masks]
        dom = [_sel(m, do2) for m in masks]
        t0, t1 = _per_head(t_ref[...], masks)
        tot_ref[0] = t0
        tot_ref[1] = t1
        suffix = _tri(tk, lambda r, c: r >= c)
        prefix = _tri(tk, lambda r, c: r <= c)
        cl_ref[...] = jnp.zeros_like(cl_ref)
        cg_ref[...] = jnp.zeros_like(cg_ref)
        dq_acc[...] = jnp.zeros_like(dq_acc)

        def tile(j, diag, slot, heads):
            off = pl.multiple_of(j * tk, tk)
            k2 = k_ref[pl.ds(off, tk), :]
            v2 = v_ref[pl.ds(off, tk), :]
            for h in heads:
                z_ref[2 * slot + h] = _dot(qm[h], k2, 1, 1)
                g_ref[2 * slot + h] = _dot(dom[h], v2, 1, 1)
            yield
            for h in heads:
                def split(r0, h=h):
                    rows = pl.ds(r0, RC)
                    sp = _softplus(z_ref[2 * slot + h, rows, :])
                    omb_ref[2 * slot + h, rows, :] = jnp.exp(-sp)
                    lg = -sp
                    if diag:
                        lg = jnp.where(_diag_mask(r0, tk, True), lg, 0.0)
                    hi, lo = _split2(lg)
                    hi_ref[2 * slot + h, rows, :] = hi
                    lo_ref[2 * slot + h, rows, :] = lo
                _chunks(tq, split)
            yield
            for h in heads:
                cum_ref[2 * slot + h] = _dot(hi_ref[2 * slot + h], suffix) + _dot(lo_ref[2 * slot + h], suffix)
            for h in heads:
                row_l = _rep(cum_ref[2 * slot + h, :, 0:1])
                later_ref[2 * slot + h] = tot_ref[h] - cl_ref[h] - row_l
                cl_ref[h] += row_l
            yield
            for h in heads:
                def weights(r0, h=h):
                    rows = pl.ds(r0, RC)
                    a = jnp.exp(z_ref[2 * slot + h, rows, :] + cum_ref[2 * slot + h, rows, :] + _wide(later_ref[2 * slot + h, rows, :], tk))
                    if diag:
                        a = jnp.where(_diag_mask(r0, tk, True), a, 0.0)
                    g = g_ref[2 * slot + h, rows, :] * a
                    g_ref[2 * slot + h, rows, :] = g
                    a_ref[2 * slot + h, rows, :] = a.astype(BF16)
                    hi, lo = _split2(g)
                    hi_ref[2 * slot + h, rows, :] = hi
                    lo_ref[2 * slot + h, rows, :] = lo
                _chunks(tq, weights)
            yield
            for h in heads:
                cum_ref[2 * slot + h] = _dot(hi_ref[2 * slot + h], prefix) + _dot(lo_ref[2 * slot + h], prefix)
            yield
            for h in heads:
                def dscore(r0, h=h):
                    rows = pl.ds(r0, RC)
                    g = g_ref[2 * slot + h, rows, :]
                    before = cum_ref[2 * slot + h, rows, :] - g + _wide(cg_ref[h, rows, :], tk)
                    omb = omb_ref[2 * slot + h, rows, :]
                    dz = g * omb - (1.0 - omb) * before
                    if diag:
                        dz = jnp.where(_diag_mask(r0, tk, True), dz, 0.0)
                    dz_ref[2 * slot + h, rows, :] = dz.astype(BF16)
                _chunks(tq, dscore)
            for h in heads:
                cg_ref[h] += _rep(cum_ref[2 * slot + h, :, tk - 1:tk])
            yield
            keys = pl.ds(off, tk)
            for h in heads:
                dq_acc[...] += _dot(dz_ref[2 * slot + h], ks_ref[h, keys, :])
                dk_acc[keys, :] += _dot(dz_ref[2 * slot + h], qm[h], 0, 0)
                dv_acc[keys, :] += _dot(a_ref[2 * slot + h], dom[h], 0, 0)

        _tiles(i, tile)
        dq_ref[...] = dq_acc[...].astype(BF16)

        @pl.when(i == nq - 1)
        def _():
            dk_ref[...] = dk_acc[...].astype(BF16)
            dv_ref[...] = dv_acc[...].astype(BF16)

    W = n_pairs * PAIR
    in_specs = _qkv_specs(S, tq, n_pairs, base) + [
        pl.BlockSpec((tq, PAIR), lambda p, i: (i, p)),
        pl.BlockSpec((tq, PAIR), lambda p, i: (i, p)),
        pl.BlockSpec((8, 128), lambda p, i: (0, 0))]
    out_specs = [pl.BlockSpec((tq, PAIR), lambda p, i: (i, p)),
                 pl.BlockSpec((S, PAIR), lambda p, i: (0, p)),
                 pl.BlockSpec((S, PAIR), lambda p, i: (0, p))]
    big, stat = (4, tq, tk), (2, tq, 128)
    return pl.pallas_call(
        body, name="sb_bwd", grid=(n_pairs, nq), in_specs=in_specs, out_specs=out_specs,
        out_shape=[jax.ShapeDtypeStruct((S, W), BF16)] * 3,
        scratch_shapes=[pltpu.VMEM(big, F32)] * 4 + [pltpu.VMEM(big, BF16)] * 4
        + [pltpu.VMEM((4, tq, 128), F32)] + [pltpu.VMEM(stat, F32)] * 3
        + [pltpu.VMEM((tq, PAIR), F32), pltpu.VMEM((S, PAIR), F32), pltpu.VMEM((S, PAIR), F32),
           pltpu.VMEM((2, S, PAIR), BF16)],
        compiler_params=_params(("parallel", "arbitrary")),
    )(qkv, qkv, qkv, tot, do, after)


def _fox_fwd2(qkv, fcx, fcr, n_pairs, base, tq):
    S = qkv.shape[0]
    tk = tq
    scale = HEAD_DIM ** -0.5
    spare = (HEAD_DIM, 0)

    def body(q_ref, k_ref, v_ref, fq_ref, fk_ref, o_ref, lse_ref, s_ref, p_ref, m_ref, al_ref, fqr_ref, acc_ref, vm_ref):
        i = pl.program_id(1)
        masks = _head_masks()

        @pl.when(i == 0)
        def _():
            _fill_masked(vm_ref, v_ref, masks, ones_lane=spare)

        q2 = q_ref[...] * scale
        qm = [_sel(m, q2) for m in masks]
        f0, f1 = _per_head(fq_ref[...], masks)
        fqr_ref[0] = f0
        fqr_ref[1] = f1
        m_ref[...] = jnp.full(m_ref.shape, NEG, F32)
        acc_ref[...] = jnp.zeros_like(acc_ref)

        def tile(j, diag, slot, heads):
            off = pl.multiple_of(j * tk, tk)
            k2 = k_ref[pl.ds(off, tk), :]
            v2 = v_ref[pl.ds(off, tk), :]
            fk2 = fk_ref[0, j]
            for h in heads:
                s_ref[2 * slot + h] = _dot(qm[h], k2, 1, 1)
            yield
            for h in heads:
                fk_row = fk2[h:h + 1, :]

                def probs(r0, h=h, fk_row=fk_row):
                    rows = pl.ds(r0, RC)
                    sv = s_ref[2 * slot + h, rows, :] - fk_row
                    if diag:
                        sv = jnp.where(_diag_mask(r0, tk, False), sv, NEG)
                    fq = fqr_ref[h, rows, :]
                    m_prev = m_ref[h, rows, :]
                    m_new = jnp.maximum(m_prev, jnp.max(sv, axis=-1, keepdims=True) + fq)
                    p_ref[2 * slot + h, rows, :] = jnp.exp(sv + _wide(fq - m_new, tk)).astype(BF16)
                    al_ref[2 * slot + h, rows, :] = jnp.exp(m_prev - m_new)
                    m_ref[h, rows, :] = m_new
                _chunks(tq, probs)
            yield
            for h in heads:
                acc_ref[h] = acc_ref[h] * al_ref[2 * slot + h] + _dot(p_ref[2 * slot + h], vm_ref[h, pl.ds(off, tk), :])

        _tiles(i, tile)
        a0, a1 = acc_ref[0], acc_ref[1]
        l0 = _rep(a0[:, spare[0]:spare[0] + 1])
        l1 = _rep(a1[:, spare[1]:spare[1] + 1])
        o_ref[...] = jnp.where(masks[0], a0 / l0, a1 / l1).astype(BF16)
        lse_ref[...] = jnp.where(masks[0], m_ref[0] + jnp.log(l0), m_ref[1] + jnp.log(l1))

    W = n_pairs * PAIR
    return pl.pallas_call(
        body, name="fox_fwd", grid=(n_pairs, S // tq), in_specs=_fox_specs(S, tq, n_pairs, base),
        out_specs=[pl.BlockSpec((tq, PAIR), lambda p, i: (i, p)), pl.BlockSpec((tq, PAIR), lambda p, i: (i, p))],
        out_shape=[jax.ShapeDtypeStruct((S, W), BF16), jax.ShapeDtypeStruct((S, W), F32)],
        scratch_shapes=[pltpu.VMEM((4, tq, tk), F32), pltpu.VMEM((4, tq, tk), BF16), pltpu.VMEM((2, tq, 128), F32),
                        pltpu.VMEM((4, tq, 128), F32), pltpu.VMEM((2, tq, 128), F32), pltpu.VMEM((2, tq, 128), F32),
                        pltpu.VMEM((2, S, PAIR), BF16)],
        compiler_params=_params(("parallel", "arbitrary")),
    )(qkv, qkv, qkv, fcx, fcr)


def _fox_bwd2(qkv, fcx, fcr, o, lse, do, n_pairs, base, tq):
    S = qkv.shape[0]
    tk = tq
    nq = S // tq
    scale = HEAD_DIM ** -0.5

    def body(q_ref, k_ref, v_ref, fq_ref, fk_ref, o_ref, lse_ref, do_ref,
             dq_ref, dk_ref, dv_ref, dfq_ref, dfk_ref,
             s_ref, dp_ref, p_ref, ds_ref, row_ref, dl_ref, dfq_acc, col_ref, dq_acc, dk_acc, dv_acc, ks_ref):
        i = pl.program_id(1)
        masks = _head_masks()

        @pl.when(i == 0)
        def _():
            dk_acc[...] = jnp.zeros_like(dk_acc)
            dv_acc[...] = jnp.zeros_like(dv_acc)
            dfk_ref[...] = jnp.zeros_like(dfk_ref)
            _fill_masked(ks_ref, k_ref, masks, mul=scale)

        q2 = q_ref[...] * scale
        do2 = do_ref[...]
        qm = [_sel(m, q2) for m in masks]
        dom = [_sel(m, do2) for m in masks]
        f0, f1 = _per_head(fq_ref[...], masks)
        l0, l1 = _per_head(lse_ref[...], masks)
        row_ref[0] = f0 - l0
        row_ref[1] = f1 - l1
        prod = do2.astype(F32) * o_ref[...].astype(F32)
        for h in range(2):
            dl_ref[h] = _rep(jnp.sum(jnp.where(masks[h], prod, 0.0), axis=-1, keepdims=True))
        dfq_acc[...] = jnp.zeros_like(dfq_acc)
        dq_acc[...] = jnp.zeros_like(dq_acc)

        def tile(j, diag, slot, heads):
            off = pl.multiple_of(j * tk, tk)
            k2 = k_ref[pl.ds(off, tk), :]
            v2 = v_ref[pl.ds(off, tk), :]
            fk2 = fk_ref[0, j]
            for h in heads:
                s_ref[2 * slot + h] = _dot(qm[h], k2, 1, 1)
                dp_ref[2 * slot + h] = _dot(dom[h], v2, 1, 1)
            yield
            for h in heads:
                col_ref[2 * slot + h] = jnp.zeros((8, tk), F32)
                fk_row = fk2[h:h + 1, :]

                def dscore(r0, h=h, fk_row=fk_row):
                    rows = pl.ds(r0, RC)
                    p = jnp.exp(s_ref[2 * slot + h, rows, :] - fk_row + _wide(row_ref[h, rows, :], tk))
                    if diag:
                        p = jnp.where(_diag_mask(r0, tk, False), p, 0.0)
                    ds = p * (dp_ref[2 * slot + h, rows, :] - _wide(dl_ref[h, rows, :], tk))
                    p_ref[2 * slot + h, rows, :] = p.astype(BF16)
                    ds_ref[2 * slot + h, rows, :] = ds.astype(BF16)
                    dfq_acc[h, rows, :] += _rep(jnp.sum(ds, axis=-1, keepdims=True))
                    col_ref[2 * slot + h] += jnp.sum(ds.reshape(RC // 8, 8, tk), axis=0)
                _chunks(tq, dscore)
            yield
            keys = pl.ds(off, tk)
            for h in heads:
                dq_acc[...] += _dot(ds_ref[2 * slot + h], ks_ref[h, keys, :])
                dk_acc[keys, :] += _dot(ds_ref[2 * slot + h], qm[h], 0, 0)
                dv_acc[keys, :] += _dot(p_ref[2 * slot + h], dom[h], 0, 0)
            for h in heads:
                dfk_ref[0, j, h:h + 1, :] += jnp.sum(col_ref[2 * slot + h], axis=0, keepdims=True)

        _tiles(i, tile)
        dq_ref[...] = dq_acc[...].astype(BF16)
        dfq_ref[...] = jnp.where(masks[0], dfq_acc[0], dfq_acc[1])

        @pl.when(i == nq - 1)
        def _():
            dk_ref[...] = dk_acc[...].astype(BF16)
            dv_ref[...] = dv_acc[...].astype(BF16)

    W = n_pairs * PAIR
    in_specs = _fox_specs(S, tq, n_pairs, base) + [
        pl.BlockSpec((tq, PAIR), lambda p, i: (i, p)),
        pl.BlockSpec((tq, PAIR), lambda p, i: (i, p)),
        pl.BlockSpec((tq, PAIR), lambda p, i: (i, p))]
    out_specs = [pl.BlockSpec((tq, PAIR), lambda p, i: (i, p)),
                 pl.BlockSpec((S, PAIR), lambda p, i: (0, p)),
                 pl.BlockSpec((S, PAIR), lambda p, i: (0, p)),
                 pl.BlockSpec((tq, PAIR), lambda p, i: (i, p)),
                 pl.BlockSpec((1, nq, 8, tk), lambda p, i: (p, 0, 0, 0))]
    return pl.pallas_call(
        body, name="fox_bwd", grid=(n_pairs, nq), in_specs=in_specs, out_specs=out_specs,
        out_shape=[jax.ShapeDtypeStruct((S, W), BF16)] * 3
        + [jax.ShapeDtypeStruct((S, W), F32), jax.ShapeDtypeStruct((n_pairs, nq, 8, tk), F32)],
        scratch_shapes=[pltpu.VMEM((4, tq, tk), F32)] * 2 + [pltpu.VMEM((4, tq, tk), BF16)] * 2
        + [pltpu.VMEM((2, tq, 128), F32)] * 3 + [pltpu.VMEM((4, 8, tk), F32)]
        + [pltpu.VMEM((tq, PAIR), F32), pltpu.VMEM((S, PAIR), F32), pltpu.VMEM((S, PAIR), F32),
           pltpu.VMEM((2, S, PAIR), BF16)],
        compiler_params=_params(("parallel", "arbitrary")),
    )(qkv, qkv, qkv, fcx, fcr, o, lse, do)


def _swiglu_fwd(u2, wg, wu):
    S, D = u2.shape
    FF = wg.shape[1]
    tm, tn = _pick(S, (512, 256, 128)), _pick(FF, (512, 384, 256, 128))

    def body(u_ref, g_ref, w_ref, a_ref, b_ref, h_ref):
        u = u_ref[...]
        a = _dot(u, g_ref[...])
        b = _dot(u, w_ref[...])
        a_ref[...] = a.astype(BF16)
        b_ref[...] = b.astype(BF16)
        h_ref[...] = (a / (1.0 + jnp.exp(-a)) * b).astype(BF16)

    spec_o = pl.BlockSpec((tm, tn), lambda i, j: (i, j))
    return pl.pallas_call(
        body, name="swiglu_fwd", grid=(S // tm, FF // tn),
        in_specs=[pl.BlockSpec((tm, D), lambda i, j: (i, 0)),
                  pl.BlockSpec((D, tn), lambda i, j: (0, j)),
                  pl.BlockSpec((D, tn), lambda i, j: (0, j))],
        out_specs=[spec_o] * 3, out_shape=[jax.ShapeDtypeStruct((S, FF), BF16)] * 3,
        compiler_params=_params(("parallel", "parallel")),
    )(u2, wg, wu)


def _swiglu_bwd(dh, wd, a, b):
    S, D = dh.shape
    FF = wd.shape[0]
    tm, tn = _pick(S, (512, 256, 128)), _pick(FF, (512, 384, 256, 128))

    def body(dh_ref, w_ref, a_ref, b_ref, da_ref, db_ref):
        dhin = _dot(dh_ref[...], w_ref[...], 1, 1)
        av = a_ref[...].astype(F32)
        bv = b_ref[...].astype(F32)
        sig = 1.0 / (1.0 + jnp.exp(-av))
        da_ref[...] = (dhin * bv * (sig * (1.0 + av * (1.0 - sig)))).astype(BF16)
        db_ref[...] = (dhin * (av * sig)).astype(BF16)

    spec_o = pl.BlockSpec((tm, tn), lambda i, j: (i, j))
    return pl.pallas_call(
        body, name="swiglu_bwd", grid=(S // tm, FF // tn),
        in_specs=[pl.BlockSpec((tm, D), lambda i, j: (i, 0)),
                  pl.BlockSpec((tn, D), lambda i, j: (j, 0)), spec_o, spec_o],
        out_specs=[spec_o] * 2, out_shape=[jax.ShapeDtypeStruct((S, FF), BF16)] * 2,
        compiler_params=_params(("parallel", "parallel")),
    )(dh, wd, a, b)


def _gate_bwd(dmix, wo, g_sb, g_fx, y_sb, y_fx):
    S, D = dmix.shape
    tm, tn = _pick(S, (512, 256, 128)), _pick(D, (512, 256, 128))

    def body(dm_ref, w_ref, gs_ref, gf_ref, ys_ref, yf_ref, dys_ref, dyf_ref, dls_ref, dlf_ref, bs_ref, bf_ref):
        @pl.when(pl.program_id(1) == 0)
        def _():
            bs_ref[...] = jnp.zeros_like(bs_ref)
            bf_ref[...] = jnp.zeros_like(bf_ref)

        dmi = _dot(dm_ref[...], w_ref[...], 1, 1)
        gs, gf = gs_ref[...], gf_ref[...]
        dys_ref[...] = (dmi * gs).astype(BF16)
        dyf_ref[...] = (dmi * gf).astype(BF16)
        dls = dmi * ys_ref[...] * gs * (1.0 - gs)
        dlf = dmi * yf_ref[...] * gf * (1.0 - gf)
        dls_ref[...] = dls.astype(BF16)
        dlf_ref[...] = dlf.astype(BF16)
        bs_ref[0:1, :] += _colsum(dls)
        bf_ref[0:1, :] += _colsum(dlf)

    t = pl.BlockSpec((tm, tn), lambda j, i: (i, j))
    accs = pl.BlockSpec((8, tn), lambda j, i: (0, j))
    return pl.pallas_call(
        body, name="gate_bwd", grid=(D // tn, S // tm),
        in_specs=[pl.BlockSpec((tm, D), lambda j, i: (i, 0)),
                  pl.BlockSpec((tn, D), lambda j, i: (j, 0)), t, t, t, t],
        out_specs=[t, t, t, t, accs, accs],
        out_shape=[jax.ShapeDtypeStruct((S, D), BF16)] * 4 + [jax.ShapeDtypeStruct((8, D), F32)] * 2,
        compiler_params=_params(("parallel", "arbitrary")),
    )(dmix, wo, g_sb, g_fx, y_sb, y_fx)


def _local_step(x, target, ada8, lnp8, bg_sb, bg_fx, bf_pad, wqkv, wf, wgs, wgf, later_weights, send_early):
    S, D = x.shape
    W = wqkv.shape[1] // 6
    n_pairs = W // PAIR
    n_heads = W // HEAD_DIM
    ts = _pick(S, (256, 128))
    tq = _pick(S, (256, 128))

    u1 = _ln_mod(x, ada8, ts)
    qkv = _mm([(u1, wqkv)], 'nn', BF16, "in_qkv")
    f = _mm([(u1, wf)], 'nn', F32, "in_f")
    g_sb = _mm([(u1, wgs)], 'nn', F32, "in_gsb", bias=bg_sb, act='sigmoid')
    g_fx = _mm([(u1, wgf)], 'nn', F32, "in_gfx", bias=bg_fx, act='sigmoid')
    fc = _fgate_fwd(f, bf_pad, _pick(S, (512, 256, 128)))
    fch = fc[:, :n_heads]
    fcx = jnp.repeat(fch, HEAD_DIM, axis=1)
    nq = S // tq
    fcr = jnp.pad(fch.T.reshape(n_pairs, 2, nq, tq).transpose(0, 2, 1, 3),
                  ((0, 0), (0, 0), (0, 6), (0, 0)))
    o_sb, tot = _sb_fwd2(qkv, n_pairs, 0, tq)
    o_fx, lse = _fox_fwd2(qkv, fcx, fcr, n_pairs, 3 * n_pairs, tq)
    wsb, wfx, wo, wfg, wfu, wfd = later_weights(lse)
    y_sb =_mm([(o_sb, wsb)], 'nn', F32, "out_sb")
    y_fx = _mm([(o_fx, wfx)], 'nn', F32, "out_fx")
    mix_in = _gate_mix(g_sb, g_fx, y_sb, y_fx, ts)
    mix = _mm([(mix_in, wo)], 'nn', F32, "out_o")
    x1, u2 = _post_attn(x, mix, ada8, lnp8, ts)
    a, b, hin = _swiglu_fwd(u2, wfg, wfu)
    h = _mm([(hin, wfd)], 'nn', F32, "ffn_down")
    dr2, dh, st_loss = _loss_head(x1, h, target, ada8, lnp8, ts)

    da, db = _swiglu_bwd(dh, wfd, a, b)
    g_wfd = _mm([(hin, dh)], 'tn', F32, "g_ffn_down")
    du2 = _mm([(da, wfg), (db, wfu)], 'nt', F32, "d_u2")
    g_wfg = _mm([(u2, da)], 'tn', F32, "g_ffn_gate")
    g_wfu = _mm([(u2, db)], 'tn', F32, "g_ffn_up")
    dr1, dmix, st_mid = _mid_bwd(du2, x1, dr2, mix, x, ada8, lnp8, ts)
    dys, dyf, dls, dlf, gb_sb, gb_fx = _gate_bwd(dmix, wo, g_sb, g_fx, y_sb, y_fx)
    g_wo = _mm([(mix_in, dmix)], 'tn', F32, "g_w_o")
    do_sb = _mm([(dys, wsb)], 'nt', BF16, "d_o_sb")
    do_fx = _mm([(dyf, wfx)], 'nt', BF16, "d_o_fx")
    g_wsb = _mm([(o_sb, dys)], 'tn', F32, "g_sb_out")
    g_wfx = _mm([(o_fx, dyf)], 'tn', F32, "g_fox_out")
    sent = send_early(dict(sb=g_wsb, fx=g_wfx, o=g_wo, fg=g_wfg, fu=g_wfu, fd=g_wfd))
    dq_s, dk_s, dv_s = _sb_bwd2(qkv, tot, do_sb, sent, n_pairs, 0, tq)
    dq_f, dk_f, dv_f, dfq, dfk = _fox_bwd2(qkv, fcx, fcr, o_fx, lse, do_fx, n_pairs, 3 * n_pairs, tq)
    dfc = dfq[:, ::HEAD_DIM] - dfk[:, :, :2, :].transpose(0, 2, 1, 3).reshape(n_heads, S).T
    dfc = jnp.pad(dfc, ((0, 0), (0, 128 - n_heads)))
    df, gb_f = _fgate_bwd(dfc, f, bf_pad, _pick(S, (512, 256, 128)))
    grads = [dq_s, dk_s, dv_s, dq_f, dk_f, dv_f]
    du1 = _mm([(g, wqkv, W, 0, n) for n, g in enumerate(grads)] + [(df, wf), (dls, wgs), (dlf, wgf)],
              'nt', F32, "d_u1")
    g_wqkv = [_mm([(u1, g)], 'tn', F32, "g_in_%d" % n) for n, g in enumerate(grads)]
    g_wf = _mm([(u1, df)], 'tn', F32, "g_in_f")
    g_wgs = _mm([(u1, dls)], 'tn', F32, "g_in_gsb")
    g_wgf = _mm([(u1, dlf)], 'tn', F32, "g_in_gfx")
    gx, st_first = _first_bwd(du1, x, dr1, ada8, ts)

    wgrads = dict(qkv=g_wqkv, f=g_wf, gs=g_wgs, gf=g_wgf)
    stats = dict(loss=st_loss, mid=st_mid, first=st_first, gb_sb=gb_sb, gb_fx=gb_fx, gb_f=gb_f)
    return gx, wgrads, stats, dq_f


def _position():
    x, y, c = lax.axis_index("x"), lax.axis_index("y"), lax.axis_index("c")
    return x, y, c, 4 * x + 2 * y + c


def _flip(x, y, c, k):
    px = 1 - x if k & 4 else x
    py = 1 - y if k & 2 else y
    pc = 1 - c if k & 1 else c
    return (px, py, pc), 4 * px + 2 * py + pc


def _all_gather_small(v, name):
    r, n = v.shape

    def body(x_ref, out_ref, send_sems, recv_sems, local_sem):
        x, y, c, me = _position()
        mine = pltpu.make_async_copy(x_ref, out_ref.at[me], local_sem)
        mine.start()
        sends = []
        for k in range(1, N_DEV):
            peer, _ = _flip(x, y, c, k)
            cp = pltpu.make_async_remote_copy(
                src_ref=x_ref, dst_ref=out_ref.at[me], send_sem=send_sems.at[k - 1], recv_sem=recv_sems.at[k - 1],
                device_id=peer, device_id_type=MESH)
            cp.start()
            sends.append(cp)
        for k in range(1, N_DEV):
            peer, slot = _flip(x, y, c, k)
            pltpu.make_async_remote_copy(
                src_ref=x_ref, dst_ref=out_ref.at[slot], send_sem=send_sems.at[k - 1], recv_sem=recv_sems.at[k - 1],
                device_id=peer, device_id_type=MESH).wait_recv()
        for cp in sends:
            cp.wait_send()
        mine.wait()

    return pl.pallas_call(
        body, name=name, out_shape=jax.ShapeDtypeStruct((N_DEV, r, n), v.dtype),
        in_specs=[pl.BlockSpec(memory_space=pltpu.VMEM)], out_specs=pl.BlockSpec(memory_space=pltpu.VMEM),
        scratch_shapes=[pltpu.SemaphoreType.DMA((N_DEV - 1,)), pltpu.SemaphoreType.DMA((N_DEV - 1,)),
                        pltpu.SemaphoreType.DMA],
    )(v)


def _all_gather_weights(packed):
    R, C = packed.shape

    def body(x_ref, out_ref, send_sems, recv_sems, local_sem):
        x, y, c, me = _position()
        sibling, sib_slot = _flip(x, y, c, 1)
        mine = pltpu.make_async_copy(x_ref, out_ref.at[me], local_sem)
        mine.start()

        def copy(k, slot, to, src=None):
            return pltpu.make_async_remote_copy(
                src_ref=out_ref.at[slot] if src is None else src, dst_ref=out_ref.at[slot],
                send_sem=send_sems.at[k], recv_sem=recv_sems.at[k], device_id=to, device_id_type=MESH)

        first = [copy(0, me, sibling, src=x_ref)]
        chips = (4, 2, 6)
        for n, k in enumerate(chips):
            peer, _ = _flip(x, y, c, k)
            first.append(copy(1 + n, me, peer, src=x_ref))
        for cp in first:
            cp.start()
        passed = []
        for n, k in enumerate(chips):
            peer, slot = _flip(x, y, c, k)
            copy(1 + n, slot, peer).wait_recv()
            cp = copy(4 + n, slot, sibling)
            cp.start()
            passed.append(cp)
        copy(0, sib_slot, sibling).wait_recv()
        for n, k in enumerate(chips):
            _, slot = _flip(x, y, c, k | 1)
            copy(4 + n, slot, sibling).wait_recv()
        for cp in first + passed:
            cp.wait_send()
        mine.wait()

    return pl.pallas_call(
        body, name="all_gather_weights", out_shape=jax.ShapeDtypeStruct((N_DEV, R, C), packed.dtype),
        in_specs=[pl.BlockSpec(memory_space=pl.ANY)], out_specs=pl.BlockSpec(memory_space=pl.ANY),
        scratch_shapes=[pltpu.SemaphoreType.DMA((7,)), pltpu.SemaphoreType.DMA((7,)), pltpu.SemaphoreType.DMA],
    )(packed)


def _exchange_grads(gpack):
    _, R, C = gpack.shape

    def body(g_ref, out_ref, send_sems, recv_sems, local_sem):
        x, y, c, me = _position()
        mine = pltpu.make_async_copy(g_ref.at[me], out_ref.at[me], local_sem)
        mine.start()
        sends = []
        for k in range(1, N_DEV):
            peer, slot = _flip(x, y, c, k)
            cp = pltpu.make_async_remote_copy(
                src_ref=g_ref.at[slot], dst_ref=out_ref.at[me], send_sem=send_sems.at[k - 1],
                recv_sem=recv_sems.at[k - 1], device_id=peer, device_id_type=MESH)
            cp.start()
            sends.append(cp)
        for k in range(1, N_DEV):
            peer, slot = _flip(x, y, c, k)
            pltpu.make_async_remote_copy(
                src_ref=g_ref.at[slot], dst_ref=out_ref.at[slot], send_sem=send_sems.at[k - 1],
                recv_sem=recv_sems.at[k - 1], device_id=peer, device_id_type=MESH).wait_recv()
        for cp in sends:
            cp.wait_send()
        mine.wait()

    return pl.pallas_call(
        body, name="exchange_grads", out_shape=jax.ShapeDtypeStruct((N_DEV, R, C), gpack.dtype),
        in_specs=[pl.BlockSpec(memory_space=pl.ANY)], out_specs=pl.BlockSpec(memory_space=pl.ANY),
        scratch_shapes=[pltpu.SemaphoreType.DMA((N_DEV - 1,)), pltpu.SemaphoreType.DMA((N_DEV - 1,)),
                        pltpu.SemaphoreType.DMA],
    )(gpack)


_HBM = pl.BlockSpec(memory_space=pltpu.HBM)
_SEM = pl.BlockSpec(memory_space=pltpu.SEMAPHORE)
_EFFECT = pltpu.SideEffectType.DATAFLOW_SIDE_EFFECTING


def _peer_copies(src_ref, land_ref, send_sems, recv_sems, scatter, receive_side):
    x, y, c, me = _position()
    copies = []
    for k in range(1, N_DEV):
        peer, slot = _flip(x, y, c, k)
        copies.append(pltpu.make_async_remote_copy(
            src_ref=src_ref.at[slot] if scatter else src_ref,
            dst_ref=land_ref.at[slot] if receive_side else land_ref.at[me],
            send_sem=send_sems.at[k - 1], recv_sem=recv_sems.at[k - 1], device_id=peer, device_id_type=MESH))
    return copies


def _exchange_start(src, name, scatter):
    R, C = src.shape[-2:]

    def body(src_ref, land_ref, send_sems, recv_sems, src_thru, land_thru, token):
        for cp in _peer_copies(src_ref, land_ref, send_sems, recv_sems, scatter, False):
            cp.start()
        token[...] = jnp.zeros_like(token)

    land = pltpu.with_memory_space_constraint(lax.empty((N_DEV, R, C), src.dtype), pltpu.HBM)
    return pl.pallas_call(
        body, name=name,
        out_shape=(pltpu.SemaphoreType.DMA((N_DEV - 1,)), pltpu.SemaphoreType.DMA((N_DEV - 1,)),
                   pltpu.HBM(src.shape, src.dtype), pltpu.HBM((N_DEV, R, C), src.dtype),
                   jax.ShapeDtypeStruct((8, 128), F32)),
        in_specs=(_HBM, _HBM), out_specs=(_SEM, _SEM, _HBM, _HBM, pl.BlockSpec(memory_space=pltpu.VMEM)),
        input_output_aliases={0: 2, 1: 3},
        compiler_params=pltpu.CompilerParams(has_side_effects=_EFFECT),
    )(pltpu.with_memory_space_constraint(src, pltpu.HBM), land)


def _exchange_wait(send_sems, recv_sems, src_thru, land_thru, after, name, scatter):
    def body(src_ref, land_ref, send_sems, recv_sems, after_ref, src_dead, got_ref):
        for cp in _peer_copies(src_ref, land_ref, send_sems, recv_sems, scatter, True):
            cp.wait_send()
            cp.wait_recv()

    return pl.pallas_call(
        body, name=name,
        out_shape=(pltpu.HBM(src_thru.shape, src_thru.dtype), pltpu.HBM(land_thru.shape, land_thru.dtype)),
        in_specs=(_HBM, _HBM, _SEM, _SEM, pl.BlockSpec(memory_space=pl.ANY)), out_specs=(_HBM, _HBM),
        input_output_aliases={0: 0, 1: 1},
        compiler_params=pltpu.CompilerParams(has_side_effects=_EFFECT),
    )(src_thru, land_thru, send_sems, recv_sems, after)[1]


def _own_slot(land, own):
    me = 4 * lax.axis_index("x") + 2 * lax.axis_index("y") + lax.axis_index("c")
    return lax.dynamic_update_slice(land, own[None], (me, 0, 0))


def _sum_slots(recv, name, tr):
    n, R, C = recv.shape

    def body(r_ref, o_ref):
        acc = r_ref[0].astype(F32)
        for s in range(1, n):
            acc = acc + r_ref[s].astype(F32)
        o_ref[...] = acc

    return pl.pallas_call(
        body, name=name, grid=(R // tr,), in_specs=[pl.BlockSpec((n, tr, C), lambda i: (0, i, 0))],
        out_specs=pl.BlockSpec((tr, C), lambda i: (i, 0)), out_shape=jax.ShapeDtypeStruct((R, C), F32),
        compiler_params=_params(("parallel",)),
    )(recv)


def _sum_stats(st_all, loss_row):
    n, r, D = st_all.shape

    def body(s_ref, o_ref, l_ref):
        acc = s_ref[0]
        for d in range(1, n):
            acc = acc + s_ref[d]
        o_ref[...] = acc
        l_ref[...] = jnp.zeros((8, 128), F32) + jnp.sum(acc[loss_row:loss_row + 1, :], axis=-1, keepdims=True)

    return pl.pallas_call(
        body, name="sum_stats", out_shape=[jax.ShapeDtypeStruct((r, D), F32), jax.ShapeDtypeStruct((8, 128), F32)],
    )(st_all)


def _adamw(w, g, m, v, name):
    R, C = w.shape
    tr = _pick(R, (256, 176, 128, 64, 32, 16, 8))
    c1 = 1.0 / (1.0 - ADAM_B1 ** ADAM_STEP)
    c2 = 1.0 / (1.0 - ADAM_B2 ** ADAM_STEP)

    def body(w_ref, g_ref, m_ref, v_ref, d_ref, nm_ref, nv_ref):
        gv = g_ref[...]
        nm = ADAM_B1 * m_ref[...] + (1.0 - ADAM_B1) * gv
        nv = ADAM_B2 * v_ref[...] + (1.0 - ADAM_B2) * (gv * gv)
        nm_ref[...] = nm
        nv_ref[...] = nv
        d_ref[...] = -ADAM_LR * ((nm * c1) / (jnp.sqrt(nv * c2) + ADAM_EPS) + ADAM_WD * w_ref[...])

    spec = pl.BlockSpec((tr, C), lambda i: (i, 0))
    return pl.pallas_call(
        body, name=name, grid=(R // tr,), in_specs=[spec] * 4, out_specs=[spec] * 3,
        out_shape=[jax.ShapeDtypeStruct((R, C), F32)] * 3, compiler_params=_params(("parallel",)),
    )(w, g, m, v)


def _round16(n):
    return -(-n // 16) * 16


def _pack_layout(D, in_cols, ff, W):
    parts = [("in", D * (in_cols // N_DEV) // D), ("fg", ff // N_DEV), ("fu", ff // N_DEV),
             ("sb", W * (D // N_DEV) // D), ("fx", W * (D // N_DEV) // D), ("o", D // N_DEV), ("fd", ff // N_DEV)]
    layout, off = {}, 0
    for nm, rows in parts:
        layout[nm] = (off, rows)
        off += _round16(rows)
    return layout, off


def _rows_of(a, D, rows):
    a = a.reshape(rows, D)
    return jnp.pad(a, ((0, _round16(rows) - rows), (0, 0)))


def _cols_to_dest(g, D):
    K, N = g.shape
    n = N // N_DEV
    return g.reshape(K, N_DEV, n).transpose(1, 0, 2).reshape(N_DEV, K * n // D, D)


def _cols_from_src(blocks, K, n):
    return blocks.reshape(N_DEV, K, n).transpose(1, 0, 2).reshape(K, N_DEV * n)


def _pad_rows16(a):
    rows = a.shape[1]
    return jnp.pad(a, ((0, 0), (0, _round16(rows) - rows), (0, 0)))


def kernel(x, c, w_ada, b_ada, w_in, b_gate, b_forget, w_sb_out, w_fox_out, w_o, ln1_g, ln1_b, w_ffn_gate, w_ffn_up, w_ffn_down, ln2_g, ln2_b, loss_target, m_w_ada, m_b_ada, m_w_in, m_b_gate, m_b_forget, m_w_sb_out, m_w_fox_out, m_w_o, m_ln1_g, m_ln1_b, m_w_ffn_gate, m_w_ffn_up, m_w_ffn_down, m_ln2_g, m_ln2_b, v_w_ada, v_b_ada, v_w_in, v_b_gate, v_b_forget, v_w_sb_out, v_w_fox_out, v_w_o, v_ln1_g, v_ln1_b, v_w_ffn_gate, v_w_ffn_up, v_w_ffn_down, v_ln2_g, v_ln2_b):
    S, D = x.shape[1], x.shape[2]
    W = w_sb_out.shape[1]
    n_heads = b_forget.shape[1]
    ff = w_ffn_down.shape[1] * N_DEV
    in_loc = w_in.shape[2]
    in_cols = in_loc * N_DEV
    ada_loc = w_ada.shape[2]
    n_cond = ada_loc * N_DEV // D
    assert w_ada.shape[0] == 1 and n_cond == 6 and in_cols == 6 * W + n_heads + 2 * D and n_heads <= 128
    me = 4 * lax.axis_index("x") + 2 * lax.axis_index("y") + lax.axis_index("c")

    c_all = _all_gather_small(c, "gather_c").reshape(N_DEV, D)
    c16 = jnp.pad(c_all, ((0, 16 - N_DEV), (0, 0)))
    b_cols = lax.dynamic_slice(b_ada, (0, me * ada_loc), (1, ada_loc))
    ada_cols = _mm([(c16, w_ada[0])], 'nn', F32, "ada_fwd", bias=b_cols, silu_a=True)[:N_DEV]
    ada_all = _all_gather_small(ada_cols, "gather_ada")
    ada_me = lax.dynamic_index_in_dim(ada_all, me, axis=1, keepdims=False)
    ada8 = jnp.pad(ada_me.reshape(n_cond, D), ((0, 8 - n_cond), (0, 0)))
    lnp8 = jnp.concatenate([ln1_g, ln1_b, ln2_g, ln2_b, jnp.zeros((4, D), F32)], axis=0)

    layout, R = _pack_layout(D, in_cols, ff, W)
    shards = dict(**{"in": w_in[0]}, fg=w_ffn_gate[0], fu=w_ffn_up[0], sb=w_sb_out[0], fx=w_fox_out[0], o=w_o[0],
                  fd=w_ffn_down[0])
    later = [nm for nm in layout if nm != "in"]
    r_in = _round16(layout["in"][1])
    first = _rows_of(shards["in"].astype(BF16), D, layout["in"][1])
    rest = jnp.concatenate([_rows_of(shards[nm].astype(BF16), D, layout[nm][1]) for nm in later], axis=0)
    gathered_in = _all_gather_weights(first)
    rest_sems = _exchange_start(rest, "gather_rest_start", False)
    ada8 = ada8 + rest_sems[4][0:1, 0:1]

    w_in_full = _cols_from_src(gathered_in[:, :layout["in"][1], :], D, in_loc)
    wqkv = w_in_full[:, :6 * W]
    wf = jnp.pad(w_in_full[:, 6 * W:6 * W + n_heads], ((0, 0), (0, 128 - n_heads)))
    wgs = w_in_full[:, 6 * W + n_heads:6 * W + n_heads + D]
    wgf = w_in_full[:, 6 * W + n_heads + D:]
    bf_pad = jnp.pad(b_forget, ((0, 0), (0, 128 - n_heads)))

    def later_weights(after):
        land = _exchange_wait(*rest_sems[:4], after, "gather_rest_wait", False)
        gathered = _own_slot(land, rest)

        def part(nm):
            off, rows = layout[nm]
            return gathered[:, off - r_in:off - r_in + rows, :]

        return (_cols_from_src(part("sb"), W, D // N_DEV), _cols_from_src(part("fx"), W, D // N_DEV),
                part("o").reshape(D, D), _cols_from_src(part("fg"), D, ff // N_DEV),
                _cols_from_src(part("fu"), D, ff // N_DEV), part("fd").reshape(ff, D))

    early = {}

    def send_early(g):
        dest = {"fg": _cols_to_dest(g["fg"], D), "fu": _cols_to_dest(g["fu"], D), "sb": _cols_to_dest(g["sb"], D),
                "fx": _cols_to_dest(g["fx"], D), "o": g["o"].reshape(N_DEV, D // N_DEV, D),
                "fd": g["fd"].reshape(N_DEV, ff // N_DEV, D)}
        early["pack"] = jnp.concatenate([_pad_rows16(dest[nm].astype(BF16)) for nm in later], axis=1)
        early["sems"] = _exchange_start(early["pack"], "grads_rest_start", True)
        return early["sems"][4]

    gx, wg, st, last = _local_step(x[0], loss_target[0], ada8, lnp8, b_gate[:, :D], b_gate[:, D:], bf_pad,
                                   wqkv, wf, wgs, wgf, later_weights, send_early)

    land = _exchange_wait(*early["sems"][:4], last, "grads_rest_wait", True)
    own = lax.dynamic_index_in_dim(early["pack"], me, axis=0, keepdims=False)
    gsum_rest = _sum_slots(_own_slot(land, own), "sum_grads_rest", _pick(R - r_in, (512, 656, 256, 128, 64, 16)))
    g_in = jnp.concatenate(wg["qkv"] + [wg["f"][:, :n_heads], wg["gs"], wg["gf"]], axis=1)
    recv_in = _exchange_grads(_pad_rows16(_cols_to_dest(g_in, D).astype(BF16)))
    gsum_in = _sum_slots(recv_in, "sum_grads_in", _pick(r_in, (512, 656, 256, 128, 64, 16)))

    def gshard(nm, shape):
        off, rows = layout[nm]
        if nm == "in":
            return gsum_in[:rows].reshape(shape)
        return gsum_rest[off - r_in:off - r_in + rows].reshape(shape)

    zrow = jnp.zeros((1, D), F32)
    gb_f_row = jnp.pad(st["gb_f"][0:1], ((0, 0), (0, D - 128)))
    stats16 = jnp.concatenate([
        st["first"][1:2], st["first"][0:1], st["mid"][4:5], st["mid"][1:2], st["mid"][0:1], st["loss"][3:4],
        st["mid"][2:3], st["mid"][3:4], st["loss"][1:2], st["loss"][2:3], st["gb_sb"][0:1], st["gb_fx"][0:1],
        st["loss"][0:1], gb_f_row, zrow, zrow], axis=0)
    st_all = _all_gather_small(stats16, "gather_stats")
    st_sum, loss_blk = _sum_stats(st_all, 12)
    loss = loss_blk[0, 0]

    d_ada_all = st_all[:, :n_cond, :].reshape(N_DEV, n_cond * D)
    d_cols = lax.dynamic_slice(d_ada_all, (0, me * ada_loc), (N_DEV, ada_loc))
    d16 = jnp.pad(d_cols, ((0, 16 - N_DEV), (0, 0)))
    g_w_ada = _mm([(c16, d16)], 'tn', F32, "ada_wgrad", silu_a=True)

    small_w = jnp.concatenate([b_ada.reshape(n_cond, D), ln1_g, ln1_b, ln2_g, ln2_b, b_gate.reshape(2, D), zrow,
                               jnp.pad(b_forget, ((0, 0), (0, D - n_heads))), zrow, zrow], axis=0)
    small_m = jnp.concatenate([m_b_ada.reshape(n_cond, D), m_ln1_g, m_ln1_b, m_ln2_g, m_ln2_b, m_b_gate.reshape(2, D),
                               zrow, jnp.pad(m_b_forget, ((0, 0), (0, D - n_heads))), zrow, zrow], axis=0)
    small_v = jnp.concatenate([v_b_ada.reshape(n_cond, D), v_ln1_g, v_ln1_b, v_ln2_g, v_ln2_b, v_b_gate.reshape(2, D),
                               zrow, jnp.pad(v_b_forget, ((0, 0), (0, D - n_heads))), zrow, zrow], axis=0)
    sm = _adamw(small_w, st_sum, small_m, small_v, "adamw_small")

    def small(a, nm):
        if nm == "b_ada":
            return a[0:n_cond].reshape(1, n_cond * D)
        if nm == "b_gate":
            return a[10:12].reshape(1, 2 * D)
        if nm == "b_forget":
            return a[13:14, :n_heads]
        row = {"ln1_g": 6, "ln1_b": 7, "ln2_g": 8, "ln2_b": 9}[nm]
        return a[row:row + 1]

    big = {
        "w_ada": (w_ada[0], g_w_ada, m_w_ada[0], v_w_ada[0]),
        "w_in": (w_in[0], gshard("in", w_in.shape[1:]), m_w_in[0], v_w_in[0]),
        "w_sb_out": (w_sb_out[0], gshard("sb", w_sb_out.shape[1:]), m_w_sb_out[0], v_w_sb_out[0]),
        "w_fox_out": (w_fox_out[0], gshard("fx", w_fox_out.shape[1:]), m_w_fox_out[0], v_w_fox_out[0]),
        "w_o": (w_o[0], gshard("o", w_o.shape[1:]), m_w_o[0], v_w_o[0]),
        "w_ffn_gate": (w_ffn_gate[0], gshard("fg", w_ffn_gate.shape[1:]), m_w_ffn_gate[0], v_w_ffn_gate[0]),
        "w_ffn_up": (w_ffn_up[0], gshard("fu", w_ffn_up.shape[1:]), m_w_ffn_up[0], v_w_ffn_up[0]),
        "w_ffn_down": (w_ffn_down[0], gshard("fd", w_ffn_down.shape[1:]), m_w_ffn_down[0], v_w_ffn_down[0]),
    }
    order = ["w_ada", "b_ada", "w_in", "b_gate", "b_forget", "w_sb_out", "w_fox_out", "w_o", "ln1_g", "ln1_b",
             "w_ffn_gate", "w_ffn_up", "w_ffn_down", "ln2_g", "ln2_b"]
    grads, deltas, new_ms, new_vs = [], [], [], []
    for nm in order:
        if nm in big:
            w, g, m, v = big[nm]
            d, nm_, nv_ = _adamw(w, g, m, v, "adamw_" + nm)
            grads.append(g[None])
            deltas.append(d[None])
            new_ms.append(nm_[None])
            new_vs.append(nv_[None])
        else:
            grads.append(small(st_sum, nm))
            deltas.append(small(sm[0], nm))
            new_ms.append(small(sm[1], nm))
            new_vs.append(small(sm[2], nm))
    return (loss, gx[None], *grads, *deltas, *new_ms, *new_vs)
```

```python
import functools

import jax
import jax.numpy as jnp
import numpy as np
from jax import lax
from jax.experimental import pallas as pl
from jax.experimental.pallas import tpu as pltpu

F32 = jnp.float32
BF16 = jnp.bfloat16

HEAD_DIM = 64
PAIR = 2 * HEAD_DIM
LN_EPS = 1e-5
ALPHA = 2.0 ** 0.25
ADAM_LR, ADAM_B1, ADAM_B2, ADAM_EPS, ADAM_WD, ADAM_STEP = 0.001, 0.9, 0.999, 1e-08, 0.01, 10
N_DEV = 8
VMEM_LIMIT = 56 * 1024 * 1024
MESH = pl.DeviceIdType.MESH


def _dot(a, b, ca=1, cb=0):
    return lax.dot_general(a, b, (((ca,), (cb,)), ((), ())), preferred_element_type=F32)


def _pick(n, cands):
    for t in cands:
        if n % t == 0:
            return t
    return n


def _params(sem):
    return pltpu.CompilerParams(dimension_semantics=sem, vmem_limit_bytes=VMEM_LIMIT)


MM_BLOCK_BYTES = 40 * 1024 * 1024
LANES = 128


def _divisors(n, cap):
    ds = [d for d in range(LANES, min(n, cap) + 1, LANES) if n % d == 0]
    return sorted(ds, reverse=True) or [n]


def _mm_tiles(M, N, a_row_bytes, b_row_bytes, out_itemsize):
    best = None
    for tm in _divisors(M, 1024):
        for tn in _divisors(N, 2048):
            need = 2 * (tm * a_row_bytes + tn * b_row_bytes + tm * tn * out_itemsize) + tm * tn * 4
            if need <= MM_BLOCK_BYTES and (best is None or (tm * tn, tm) > (best[0] * best[1], best[0])):
                best = (tm, tn)
    assert best is not None, (M, N, a_row_bytes, b_row_bytes)
    return best


def _mm(pairs, mode, out_dtype, name, bias=None, act=None, silu_a=False):
    norm = []
    for p in pairs:
        a, b = p[0], p[1]
        kdim_a = a.shape[0] if mode == 'tn' else a.shape[1]
        K, ka, kb = (p[2], p[3], p[4]) if len(p) > 2 else (kdim_a, 0, 0)
        norm.append((a, b, K, ka, kb))
    a0, b0 = norm[0][0], norm[0][1]
    M = a0.shape[1] if mode == 'tn' else a0.shape[0]
    N = b0.shape[0] if mode == 'nt' else b0.shape[1]
    tm, tn = _mm_tiles(M, N, sum(K * a.dtype.itemsize for a, _, K, _, _ in norm),
                       sum(K * b.dtype.itemsize for _, b, K, _, _ in norm), jnp.dtype(out_dtype).itemsize)
    n_pairs = len(norm)

    in_specs, args = [], []
    for a, b, K, ka, kb in norm:
        if mode == 'tn':
            in_specs.append(pl.BlockSpec((K, tm), lambda i, j, ka=ka: (ka, i)))
        else:
            in_specs.append(pl.BlockSpec((tm, K), lambda i, j, ka=ka: (i, ka)))
        if mode == 'nt':
            in_specs.append(pl.BlockSpec((tn, K), lambda i, j, kb=kb: (j, kb)))
        else:
            in_specs.append(pl.BlockSpec((K, tn), lambda i, j, kb=kb: (kb, j)))
        args += [a, b]
    if bias is not None:
        in_specs.append(pl.BlockSpec((1, tn), lambda i, j: (0, j)))
        args.append(bias)

    ca = 0 if mode == 'tn' else 1
    cb = 1 if mode == 'nt' else 0

    def body(*refs):
        o_ref = refs[-1]
        acc = None
        for p in range(n_pairs):
            av = refs[2 * p][...]
            if silu_a:
                av = av / (1.0 + jnp.exp(-av))
            d = _dot(av.astype(BF16), refs[2 * p + 1][...].astype(BF16), ca, cb)
            acc = d if acc is None else acc + d
        if bias is not None:
            acc = acc + refs[2 * n_pairs][...]
        if act == 'sigmoid':
            acc = 1.0 / (1.0 + jnp.exp(-acc))
        o_ref[...] = acc.astype(out_dtype)

    return pl.pallas_call(
        body, name=name, grid=(M // tm, N // tn), in_specs=in_specs,
        out_specs=pl.BlockSpec((tm, tn), lambda i, j: (i, j)),
        out_shape=jax.ShapeDtypeStruct((M, N), out_dtype),
        compiler_params=_params(("parallel", "parallel")),
    )(*args)


def _rows_call(body, name, row_ins, vec_ins, row_outs, acc_outs, ts):
    S = row_ins[0].shape[0]
    in_specs = [pl.BlockSpec((ts, a.shape[1]), lambda i: (i, 0)) for a in row_ins]
    in_specs += [pl.BlockSpec(a.shape, lambda i: (0, 0)) for a in vec_ins]
    out_specs = [pl.BlockSpec((ts, c), lambda i: (i, 0)) for c, _ in row_outs]
    out_specs += [pl.BlockSpec(s, lambda i: (0, 0)) for s in acc_outs]
    out_shape = [jax.ShapeDtypeStruct((S, c), dt) for c, dt in row_outs]
    out_shape += [jax.ShapeDtypeStruct(s, F32) for s in acc_outs]
    return pl.pallas_call(
        body, name=name, grid=(S // ts,), in_specs=in_specs, out_specs=out_specs, out_shape=out_shape,
        compiler_params=_params(("arbitrary",)),
    )(*row_ins, *vec_ins)


def _ln_stats(v):
    mu = jnp.mean(v, axis=-1, keepdims=True)
    d = v - mu
    var = jnp.mean(d * d, axis=-1, keepdims=True)
    rstd = lax.rsqrt(var + LN_EPS)
    return d * rstd, rstd


def _ln_bwd(dxhat, xhat, rstd):
    m1 = jnp.mean(dxhat, axis=-1, keepdims=True)
    m2 = jnp.mean(dxhat * xhat, axis=-1, keepdims=True)
    return rstd * (dxhat - m1 - xhat * m2)


def _colsum(v):
    return jnp.sum(v, axis=0, keepdims=True)


def _ln_mod(x, ada8, ts):
    D = x.shape[1]

    def body(x_ref, v_ref, u_ref):
        xhat, _ = _ln_stats(x_ref[...])
        u_ref[...] = (xhat * (1.0 + v_ref[1:2, :]) + v_ref[0:1, :]).astype(BF16)

    return _rows_call(body, "ln_mod", [x], [ada8], [(D, BF16)], [], ts)[0]


def _gate_mix(g_sb, g_fx, y_sb, y_fx, ts):
    D = y_sb.shape[1]

    def body(gs, gf, ys, yf, o_ref):
        o_ref[...] = (gs[...] * ys[...] + gf[...] * yf[...]).astype(BF16)

    return _rows_call(body, "gate_mix", [g_sb, g_fx, y_sb, y_fx], [], [(D, BF16)], [], ts)[0]


def _post_attn(x, mix, ada8, lnp8, ts):
    D = x.shape[1]

    def body(x_ref, mix_ref, v_ref, p_ref, x1_ref, u2_ref):
        r1 = ALPHA * x_ref[...] + v_ref[2:3, :] * mix_ref[...]
        xhat, _ = _ln_stats(r1)
        x1 = xhat * p_ref[0:1, :] + p_ref[1:2, :]
        x1_ref[...] = x1
        xh1, _ = _ln_stats(x1)
        u2_ref[...] = (xh1 * (1.0 + v_ref[4:5, :]) + v_ref[3:4, :]).astype(BF16)

    return _rows_call(body, "post_attn", [x, mix], [ada8, lnp8], [(D, F32), (D, BF16)], [], ts)


def _loss_head(x1, h, target, ada8, lnp8, ts):
    D = x1.shape[1]

    def body(x1_ref, h_ref, t_ref, v_ref, p_ref, dr2_ref, dh_ref, st_ref):
        @pl.when(pl.program_id(0) == 0)
        def _():
            st_ref[...] = jnp.zeros_like(st_ref)

        hv = h_ref[...]
        g2 = v_ref[5:6, :]
        r2 = ALPHA * x1_ref[...] + g2 * hv
        xhat, rstd = _ln_stats(r2)
        y = xhat * p_ref[2:3, :] + p_ref[3:4, :]
        err = y - t_ref[...]
        dy = err * (1.0 / D)
        dr2 = _ln_bwd(dy * p_ref[2:3, :], xhat, rstd)
        dr2_ref[...] = dr2
        dh_ref[...] = (dr2 * g2).astype(BF16)
        st_ref[0:1, :] += _colsum(err * err) * (0.5 / D)
        st_ref[1:2, :] += _colsum(dy * xhat)
        st_ref[2:3, :] += _colsum(dy)
        st_ref[3:4, :] += _colsum(dr2 * hv)

    return _rows_call(body, "loss_head", [x1, h, target], [ada8, lnp8], [(D, F32), (D, BF16)], [(8, D)], ts)


def _mid_bwd(du2, x1, dr2, mix, x, ada8, lnp8, ts):
    D = x.shape[1]

    def body(du2_ref, x1_ref, dr2_ref, mix_ref, x_ref, v_ref, p_ref, dr1_ref, dmix_ref, st_ref):
        @pl.when(pl.program_id(0) == 0)
        def _():
            st_ref[...] = jnp.zeros_like(st_ref)

        du2v = du2_ref[...]
        xh1, rstd1 = _ln_stats(x1_ref[...])
        dx1 = ALPHA * dr2_ref[...] + _ln_bwd(du2v * (1.0 + v_ref[4:5, :]), xh1, rstd1)
        mixv = mix_ref[...]
        g1 = v_ref[2:3, :]
        r1 = ALPHA * x_ref[...] + g1 * mixv
        xhr, rstdr = _ln_stats(r1)
        dr1 = _ln_bwd(dx1 * p_ref[0:1, :], xhr, rstdr)
        dr1_ref[...] = dr1
        dmix_ref[...] = (dr1 * g1).astype(BF16)
        st_ref[0:1, :] += _colsum(du2v * xh1)
        st_ref[1:2, :] += _colsum(du2v)
        st_ref[2:3, :] += _colsum(dx1 * xhr)
        st_ref[3:4, :] += _colsum(dx1)
        st_ref[4:5, :] += _colsum(dr1 * mixv)

    return _rows_call(body, "mid_bwd", [du2, x1, dr2, mix, x], [ada8, lnp8], [(D, F32), (D, BF16)], [(8, D)], ts)


def _first_bwd(du1, x, dr1, ada8, ts):
    D = x.shape[1]

    def body(du1_ref, x_ref, dr1_ref, v_ref, gx_ref, st_ref):
        @pl.when(pl.program_id(0) == 0)
        def _():
            st_ref[...] = jnp.zeros_like(st_ref)

        du1v = du1_ref[...]
        xh0, rstd0 = _ln_stats(x_ref[...])
        gx_ref[...] = ALPHA * dr1_ref[...] + _ln_bwd(du1v * (1.0 + v_ref[1:2, :]), xh0, rstd0)
        st_ref[0:1, :] += _colsum(du1v * xh0)
        st_ref[1:2, :] += _colsum(du1v)

    return _rows_call(body, "first_bwd", [du1, x, dr1], [ada8], [(D, F32)], [(8, D)], ts)


def _split3(v):
    hi = v.astype(BF16)
    r = v - hi.astype(F32)
    mid = r.astype(BF16)
    lo = (r - mid.astype(F32)).astype(BF16)
    return hi, mid, lo


def _fgate_fwd(f, bf_pad, tb):
    S = f.shape[0]

    def body(f_ref, b_ref, fc_ref, carry):
        @pl.when(pl.program_id(0) == 0)
        def _():
            carry[...] = jnp.zeros_like(carry)

        z = f_ref[...] + b_ref[...]
        ls = jnp.minimum(z, 0.0) - jnp.log(1.0 + jnp.exp(-jnp.abs(z)))
        r = lax.broadcasted_iota(jnp.int32, (tb, tb), 0)
        c = lax.broadcasted_iota(jnp.int32, (tb, tb), 1)
        tri = (c <= r).astype(BF16)
        hi, mid, lo = _split3(ls)
        cs = _dot(tri, hi) + _dot(tri, mid) + _dot(tri, lo) + carry[...]
        fc_ref[...] = cs
        carry[...] = cs[tb - 1:tb, :]

    return pl.pallas_call(
        body, name="fgate_fwd", grid=(S // tb,),
        in_specs=[pl.BlockSpec((tb, 128), lambda i: (i, 0)), pl.BlockSpec((1, 128), lambda i: (0, 0))],
        out_specs=pl.BlockSpec((tb, 128), lambda i: (i, 0)),
        out_shape=jax.ShapeDtypeStruct((S, 128), F32),
        scratch_shapes=[pltpu.VMEM((1, 128), F32)],
        compiler_params=_params(("arbitrary",)),
    )(f, bf_pad)


def _fgate_bwd(dfc, f, bf_pad, tb):
    S = f.shape[0]
    nb = S // tb

    def body(d_ref, f_ref, b_ref, df_ref, gb_ref, carry):
        @pl.when(pl.program_id(0) == 0)
        def _():
            carry[...] = jnp.zeros_like(carry)
            gb_ref[...] = jnp.zeros_like(gb_ref)

        r = lax.broadcasted_iota(jnp.int32, (tb, tb), 0)
        c = lax.broadcasted_iota(jnp.int32, (tb, tb), 1)
        tri = (c >= r).astype(BF16)
        hi, mid, lo = _split3(d_ref[...])
        rs = _dot(tri, hi) + _dot(tri, mid) + _dot(tri, lo) + carry[...]
        carry[...] = rs[0:1, :]
        z = f_ref[...] + b_ref[...]
        df = rs * (1.0 / (1.0 + jnp.exp(z)))
        df_ref[...] = df
        gb_ref[0:1, :] += _colsum(df)

    return pl.pallas_call(
        body, name="fgate_bwd", grid=(nb,),
        in_specs=[pl.BlockSpec((tb, 128), lambda i: (nb - 1 - i, 0)),
                  pl.BlockSpec((tb, 128), lambda i: (nb - 1 - i, 0)),
                  pl.BlockSpec((1, 128), lambda i: (0, 0))],
        out_specs=[pl.BlockSpec((tb, 128), lambda i: (nb - 1 - i, 0)), pl.BlockSpec((8, 128), lambda i: (0, 0))],
        out_shape=[jax.ShapeDtypeStruct((S, 128), F32), jax.ShapeDtypeStruct((8, 128), F32)],
        scratch_shapes=[pltpu.VMEM((1, 128), F32)],
        compiler_params=_params(("arbitrary",)),
    )(dfc, f, bf_pad)


def _split2(v):
    hi = v.astype(BF16)
    lo = (v - hi.astype(F32)).astype(BF16)
    return hi, lo


def _head_masks():
    lane = lax.broadcasted_iota(jnp.int32, (1, PAIR), 1)
    m0 = lane < HEAD_DIM
    return m0, jnp.logical_not(m0)


def _sel(mask, v):
    return jnp.where(mask, v, jnp.zeros_like(v))


def _softplus(z):
    return jnp.maximum(z, 0.0) + jnp.log(1.0 + jnp.exp(-jnp.abs(z)))


def _qkv_specs(S, tq, n_pairs, base):
    return [pl.BlockSpec((tq, PAIR), lambda p, i: (i, base + p)),
            pl.BlockSpec((S, PAIR), lambda p, i: (0, base + n_pairs + p)),
            pl.BlockSpec((S, PAIR), lambda p, i: (0, base + 2 * n_pairs + p))]


def _sb_fwd(qkv, n_pairs, base, tq):
    S = qkv.shape[0]
    tk = tq
    scale = HEAD_DIM ** -0.5

    def body(q_ref, k_ref, v_ref, o_ref, t_ref, acc_ref):
        i = pl.program_id(1)
        masks = _head_masks()
        q2 = q_ref[...]
        qm = [_sel(m, q2) for m in masks]
        rowpos = i * tq + lax.broadcasted_iota(jnp.int32, (tq, tk), 0)
        colin = lax.broadcasted_iota(jnp.int32, (tq, tk), 1)
        upper = (lax.broadcasted_iota(jnp.int32, (tk, tk), 0) > lax.broadcasted_iota(jnp.int32, (tk, tk), 1)).astype(BF16)
        acc_ref[...] = jnp.zeros_like(acc_ref)

        def step(jj, carry):
            j = i - jj
            off = pl.multiple_of(j * tk, tk)
            k2 = k_ref[pl.ds(off, tk), :]
            v2 = v_ref[pl.ds(off, tk), :]
            mask = (j * tk + colin) < rowpos
            out = None
            new = []
            for h in heads:
                z = _dot(qm[h], k2, 1, 1) * scale
                sp = _softplus(z)
                lg = jnp.where(mask, -sp, 0.0)
                hi, lo = _split2(lg)
                suf = _dot(hi, upper) + _dot(lo, upper)
                a = jnp.where(mask, jnp.exp(z - sp + suf + carry[h]), 0.0)
                d = _dot(a.astype(BF16), _sel(masks[h], v2))
                out = d if out is None else out + d
                new.append(carry[h] + jnp.sum(lg, axis=-1, keepdims=True))
            acc_ref[...] += out
            return tuple(new)

        zero = jnp.zeros((tq, 1), F32)
        r0, r1 = lax.fori_loop(0, i + 1, step, (zero, zero))
        o_ref[...] = acc_ref[...].astype(BF16)
        t_ref[...] = jnp.where(masks[0], r0, r1)

    W = n_pairs * PAIR
    return pl.pallas_call(
        body, name="sb_fwd", grid=(n_pairs, S // tq), in_specs=_qkv_specs(S, tq, n_pairs, base),
        out_specs=[pl.BlockSpec((tq, PAIR), lambda p, i: (i, p)), pl.BlockSpec((tq, PAIR), lambda p, i: (i, p))],
        out_shape=[jax.ShapeDtypeStruct((S, W), BF16), jax.ShapeDtypeStruct((S, W), F32)],
        scratch_shapes=[pltpu.VMEM((tq, PAIR), F32)],
        compiler_params=_params(("parallel", "arbitrary")),
    )(qkv, qkv, qkv)


def _sb_bwd(qkv, tot, do, n_pairs, base, tq):
    S = qkv.shape[0]
    tk = tq
    nq = S // tq
    scale = HEAD_DIM ** -0.5

    def body(q_ref, k_ref, v_ref, t_ref, do_ref, dq_ref, dk_ref, dv_ref, dq_acc, dk_acc, dv_acc):
        i = pl.program_id(1)
        masks = _head_masks()

        @pl.when(i == 0)
        def _():
            dk_acc[...] = jnp.zeros_like(dk_acc)
            dv_acc[...] = jnp.zeros_like(dv_acc)

        q2 = q_ref[...]
        do2 = do_ref[...]
        qm = [_sel(m, q2) for m in masks]
        dom = [_sel(m, do2) for m in masks]
        t2 = t_ref[...]
        tot_h = [t2[:, 0:1], t2[:, HEAD_DIM:HEAD_DIM + 1]]
        rowpos = i * tq + lax.broadcasted_iota(jnp.int32, (tq, tk), 0)
        colin = lax.broadcasted_iota(jnp.int32, (tq, tk), 1)
        r_i = lax.broadcasted_iota(jnp.int32, (tk, tk), 0)
        c_i = lax.broadcasted_iota(jnp.int32, (tk, tk), 1)
        upper = (r_i > c_i).astype(BF16)
        lower = (r_i < c_i).astype(BF16)
        dq_acc[...] = jnp.zeros_like(dq_acc)

        def step(j, carry):
            off = pl.multiple_of(j * tk, tk)
            k2 = k_ref[pl.ds(off, tk), :]
            v2 = v_ref[pl.ds(off, tk), :]
            mask = (j * tk + colin) < rowpos
            dq = None
            dk = None
            dv = None
            new = []
            for h in heads:
                cum_l, cum_g = carry[2 * h], carry[2 * h + 1]
                z = _dot(qm[h], k2, 1, 1) * scale
                sp = _softplus(z)
                lg = jnp.where(mask, -sp, 0.0)
                hi, lo = _split2(lg)
                suf = _dot(hi, upper) + _dot(lo, upper)
                row_l = jnp.sum(lg, axis=-1, keepdims=True)
                later = tot_h[h] - cum_l - row_l
                a = jnp.where(mask, jnp.exp(z - sp + suf + later), 0.0)
                da = _dot(dom[h], v2, 1, 1)
                g = da * a
                ghi, glo = _split2(g)
                pre = _dot(ghi, lower) + _dot(glo, lower) + cum_g
                one_m_beta = jnp.exp(-sp)
                dz = jnp.where(mask, g * one_m_beta - (1.0 - one_m_beta) * pre, 0.0)
                dzb = (dz * scale).astype(BF16)
                d1 = _dot(dzb, _sel(masks[h], k2))
                d2 = _dot(dzb, qm[h], 0, 0)
                d3 = _dot(a.astype(BF16), dom[h], 0, 0)
                dq = d1 if dq is None else dq + d1
                dk = d2 if dk is None else dk + d2
                dv = d3 if dv is None else dv + d3
                new += [cum_l + row_l, cum_g + jnp.sum(g, axis=-1, keepdims=True)]
            dq_acc[...] += dq
            dk_acc[pl.ds(off, tk), :] += dk
            dv_acc[pl.ds(off, tk), :] += dv
            return tuple(new)

        zero = jnp.zeros((tq, 1), F32)
        lax.fori_loop(0, i + 1, step, (zero, zero, zero, zero))
        dq_ref[...] = dq_acc[...].astype(BF16)

        @pl.when(i == nq - 1)
        def _():
            dk_ref[...] = dk_acc[...].astype(BF16)
            dv_ref[...] = dv_acc[...].astype(BF16)

    W = n_pairs * PAIR
    in_specs = _qkv_specs(S, tq, n_pairs, base) + [
        pl.BlockSpec((tq, PAIR), lambda p, i: (i, p)),
        pl.BlockSpec((tq, PAIR), lambda p, i: (i, p))]
    out_specs = [pl.BlockSpec((tq, PAIR), lambda p, i: (i, p)),
                 pl.BlockSpec((S, PAIR), lambda p, i: (0, p)),
                 pl.BlockSpec((S, PAIR), lambda p, i: (0, p))]
    dq, dk, dv = pl.pallas_call(
        body, name="sb_bwd", grid=(n_pairs, nq), in_specs=in_specs, out_specs=out_specs,
        out_shape=[jax.ShapeDtypeStruct((S, W), BF16)] * 3,
        scratch_shapes=[pltpu.VMEM((tq, PAIR), F32), pltpu.VMEM((S, PAIR), F32), pltpu.VMEM((S, PAIR), F32)],
        compiler_params=_params(("parallel", "arbitrary")),
    )(qkv, qkv, qkv, tot, do)
    return dq, dk, dv


NEG = -1e30


def _fox_specs(S, tq, n_pairs, base):
    return _qkv_specs(S, tq, n_pairs, base) + [
        pl.BlockSpec((tq, PAIR), lambda p, i: (i, p)),
        pl.BlockSpec((1, S // tq, 8, tq), lambda p, i: (p, 0, 0, 0))]


def _fox_fwd(qkv, fcx, fcr, n_pairs, base, tq):
    S = qkv.shape[0]
    tk = tq
    scale = HEAD_DIM ** -0.5

    def body(q_ref, k_ref, v_ref, fq_ref, fk_ref, o_ref, lse_ref, acc_ref):
        i = pl.program_id(1)
        masks = _head_masks()
        q2 = q_ref[...]
        qm = [_sel(m, q2) for m in masks]
        fq2 = fq_ref[...]
        fq = [fq2[:, 0:1], fq2[:, HEAD_DIM:HEAD_DIM + 1]]
        rowpos = i * tq + lax.broadcasted_iota(jnp.int32, (tq, tk), 0)
        colin = lax.broadcasted_iota(jnp.int32, (tq, tk), 1)
        acc_ref[...] = jnp.zeros_like(acc_ref)

        def step(j, carry):
            off = pl.multiple_of(j * tk, tk)
            k2 = k_ref[pl.ds(off, tk), :]
            v2 = v_ref[pl.ds(off, tk), :]
            fk2 = fk_ref[0, j]
            mask = (j * tk + colin) <= rowpos
            out = None
            new = []
            alphas = []
            for h in heads:
                m_old, l_old = carry[2 * h], carry[2 * h + 1]
                s = _dot(qm[h], k2, 1, 1) * scale + fq[h] - fk2[h:h + 1, :]
                s = jnp.where(mask, s, NEG)
                m_new = jnp.maximum(m_old, jnp.max(s, axis=-1, keepdims=True))
                p = jnp.exp(s - m_new)
                alpha = jnp.exp(m_old - m_new)
                alphas.append(alpha)
                d = _dot(p.astype(BF16), _sel(masks[h], v2))
                out = d if out is None else out + d
                new += [m_new, alpha * l_old + jnp.sum(p, axis=-1, keepdims=True)]
            acc_ref[...] = acc_ref[...] * jnp.where(masks[0], alphas[0], alphas[1]) + out
            return tuple(new)

        zero = jnp.zeros((tq, 1), F32)
        neg = jnp.full((tq, 1), NEG, F32)
        m0, l0, m1, l1 = lax.fori_loop(0, i + 1, step, (neg, zero, neg, zero))
        o_ref[...] = (acc_ref[...] / jnp.where(masks[0], l0, l1)).astype(BF16)
        lse_ref[...] = jnp.where(masks[0], m0 + jnp.log(l0), m1 + jnp.log(l1))

    W = n_pairs * PAIR
    return pl.pallas_call(
        body, name="fox_fwd", grid=(n_pairs, S // tq), in_specs=_fox_specs(S, tq, n_pairs, base),
        out_specs=[pl.BlockSpec((tq, PAIR), lambda p, i: (i, p)), pl.BlockSpec((tq, PAIR), lambda p, i: (i, p))],
        out_shape=[jax.ShapeDtypeStruct((S, W), BF16), jax.ShapeDtypeStruct((S, W), F32)],
        scratch_shapes=[pltpu.VMEM((tq, PAIR), F32)],
        compiler_params=_params(("parallel", "arbitrary")),
    )(qkv, qkv, qkv, fcx, fcr)


def _fox_bwd(qkv, fcx, fcr, o, lse, do, n_pairs, base, tq):
    S = qkv.shape[0]
    tk = tq
    nq = S // tq
    scale = HEAD_DIM ** -0.5

    def body(q_ref, k_ref, v_ref, fq_ref, fk_ref, o_ref, lse_ref, do_ref,
             dq_ref, dk_ref, dv_ref, dfq_ref, dfk_ref, dq_acc, dk_acc, dv_acc):
        i = pl.program_id(1)
        masks = _head_masks()

        @pl.when(i == 0)
        def _():
            dk_acc[...] = jnp.zeros_like(dk_acc)
            dv_acc[...] = jnp.zeros_like(dv_acc)
            dfk_ref[...] = jnp.zeros_like(dfk_ref)

        q2 = q_ref[...]
        do2 = do_ref[...]
        qm = [_sel(m, q2) for m in masks]
        dom = [_sel(m, do2) for m in masks]
        fq2 = fq_ref[...]
        fq = [fq2[:, 0:1], fq2[:, HEAD_DIM:HEAD_DIM + 1]]
        l2 = lse_ref[...]
        lse_h = [l2[:, 0:1], l2[:, HEAD_DIM:HEAD_DIM + 1]]
        prod = do2.astype(F32) * o_ref[...].astype(F32)
        delta = [jnp.sum(jnp.where(m, prod, 0.0), axis=-1, keepdims=True) for m in masks]
        rowpos = i * tq + lax.broadcasted_iota(jnp.int32, (tq, tk), 0)
        colin = lax.broadcasted_iota(jnp.int32, (tq, tk), 1)
        dq_acc[...] = jnp.zeros_like(dq_acc)

        def step(j, carry):
            off = pl.multiple_of(j * tk, tk)
            k2 = k_ref[pl.ds(off, tk), :]
            v2 = v_ref[pl.ds(off, tk), :]
            fk2 = fk_ref[0, j]
            mask = (j * tk + colin) <= rowpos
            dq = None
            dk = None
            dv = None
            new = []
            dfk_rows = []
            for h in heads:
                s = _dot(qm[h], k2, 1, 1) * scale + fq[h] - fk2[h:h + 1, :]
                p = jnp.where(mask, jnp.exp(s - lse_h[h]), 0.0)
                dp = _dot(dom[h], v2, 1, 1)
                ds = p * (dp - delta[h])
                dsb = (ds * scale).astype(BF16)
                d1 = _dot(dsb, _sel(masks[h], k2))
                d2 = _dot(dsb, qm[h], 0, 0)
                d3 = _dot(p.astype(BF16), dom[h], 0, 0)
                dq = d1 if dq is None else dq + d1
                dk = d2 if dk is None else dk + d2
                dv = d3 if dv is None else dv + d3
                new.append(carry[h] + jnp.sum(ds, axis=-1, keepdims=True))
                dfk_rows.append(jnp.sum(ds, axis=0, keepdims=True))
            dq_acc[...] += dq
            dk_acc[pl.ds(off, tk), :] += dk
            dv_acc[pl.ds(off, tk), :] += dv
            dfk_ref[0, j, 0:1, :] += dfk_rows[0]
            dfk_ref[0, j, 1:2, :] += dfk_rows[1]
            return tuple(new)

        zero = jnp.zeros((tq, 1), F32)
        r0, r1 = lax.fori_loop(0, i + 1, step, (zero, zero))
        dq_ref[...] = dq_acc[...].astype(BF16)
        dfq_ref[...] = jnp.where(masks[0], r0, r1)

        @pl.when(i == nq - 1)
        def _():
            dk_ref[...] = dk_acc[...].astype(BF16)
            dv_ref[...] = dv_acc[...].astype(BF16)

    W = n_pairs * PAIR
    in_specs = _fox_specs(S, tq, n_pairs, base) + [
        pl.BlockSpec((tq, PAIR), lambda p, i: (i, p)),
        pl.BlockSpec((tq, PAIR), lambda p, i: (i, p)),
        pl.BlockSpec((tq, PAIR), lambda p, i: (i, p))]
    out_specs = [pl.BlockSpec((tq, PAIR), lambda p, i: (i, p)),
                 pl.BlockSpec((S, PAIR), lambda p, i: (0, p)),
                 pl.BlockSpec((S, PAIR), lambda p, i: (0, p)),
                 pl.BlockSpec((tq, PAIR), lambda p, i: (i, p)),
                 pl.BlockSpec((1, nq, 8, tk), lambda p, i: (p, 0, 0, 0))]
    return pl.pallas_call(
        body, name="fox_bwd", grid=(n_pairs, nq), in_specs=in_specs, out_specs=out_specs,
        out_shape=[jax.ShapeDtypeStruct((S, W), BF16)] * 3
        + [jax.ShapeDtypeStruct((S, W), F32), jax.ShapeDtypeStruct((n_pairs, nq, 8, tk), F32)],
        scratch_shapes=[pltpu.VMEM((tq, PAIR), F32), pltpu.VMEM((S, PAIR), F32), pltpu.VMEM((S, PAIR), F32)],
        compiler_params=_params(("parallel", "arbitrary")),
    )(qkv, qkv, qkv, fcx, fcr, o, lse, do)


RC = 32


def _chunks(n_rows, fn):
    for ci in range(n_rows // RC):
        fn(ci * RC)


def _wide(v, tk):
    return v if tk == 128 else jnp.tile(v, (1, tk // 128))


def _rep(col):
    return jnp.broadcast_to(col, (col.shape[0], 128))


def _per_head(blk, masks):
    sw = pltpu.roll(blk, HEAD_DIM, axis=1)
    return jnp.where(masks[0], blk, sw), jnp.where(masks[0], sw, blk)


def _fill_masked(dst_ref, src_ref, masks, mul=None, ones_lane=None):
    v = src_ref[...]
    if mul is not None:
        v = v * mul
    lane = lax.broadcasted_iota(jnp.int32, (1, PAIR), 1)
    for h in range(2):
        m = _sel(masks[h], v)
        if ones_lane is not None:
            m = jnp.where(lane == ones_lane[h], jnp.ones_like(m), m)
        dst_ref[h] = m


def _tri(tk, cmp):
    r = lax.broadcasted_iota(jnp.int32, (tk, tk), 0)
    c = lax.broadcasted_iota(jnp.int32, (tk, tk), 1)
    return cmp(r, c).astype(BF16)


def _diag_mask(r0, tk, strict):
    row = r0 + lax.broadcasted_iota(jnp.int32, (RC, tk), 0)
    col = lax.broadcasted_iota(jnp.int32, (RC, tk), 1)
    return (col < row) if strict else (col <= row)


def _staggered(bodies):
    active, waiting = [], list(bodies)
    while waiting or active:
        if waiting:
            active.append(waiting.pop(0))
        for g in list(active):
            try:
                next(g)
            except StopIteration:
                active.remove(g)


def _streams(tile, j, diag, slot):
    return [tile(j, diag, slot, (0, 1))]


def _tiles(i, tile):
    def step(jj, carry):
        _staggered(_streams(tile, 2 * jj, False, 0) + _streams(tile, 2 * jj + 1, False, 1))
        return carry
    lax.fori_loop(0, i // 2, step, 0)

    @pl.when(i % 2 == 1)
    def _():
        _staggered(_streams(tile, i - 1, False, 0))
    _staggered(_streams(tile, i, True, 1))


def _tiles_reversed(i, tile):
    _staggered(_streams(tile, i, True, 1))

    def step(jj, carry):
        _staggered(_streams(tile, i - 1 - 2 * jj, False, 0) + _streams(tile, i - 2 - 2 * jj, False, 1))
        return carry
    lax.fori_loop(0, i // 2, step, 0)

    @pl.when(i % 2 == 1)
    def _():
        _staggered(_streams(tile, 0, False, 0))


def _sb_fwd2(qkv, n_pairs, base, tq):
    S = qkv.shape[0]
    tk = tq
    scale = HEAD_DIM ** -0.5

    def body(q_ref, k_ref, v_ref, o_ref, t_ref, z_ref, hi_ref, suf_ref, p_ref, r_ref, acc_ref, vm_ref):
        i = pl.program_id(1)
        masks = _head_masks()

        @pl.when(i == 0)
        def _():
            _fill_masked(vm_ref, v_ref, masks)

        q2 = q_ref[...] * scale
        qm = [_sel(m, q2) for m in masks]
        incl = _tri(tk, lambda r, c: r >= c)
        r_ref[...] = jnp.zeros_like(r_ref)
        acc_ref[...] = jnp.zeros_like(acc_ref)

        def tile(j, diag, slot, heads):
            off = pl.multiple_of(j * tk, tk)
            k2 = k_ref[pl.ds(off, tk), :]
            v2 = v_ref[pl.ds(off, tk), :]
            for h in heads:
                z_ref[2 * slot + h] = _dot(qm[h], k2, 1, 1)
            yield
            for h in heads:
                def split(r0, h=h):
                    rows = pl.ds(r0, RC)
                    lg = -_softplus(z_ref[2 * slot + h, rows, :])
                    if diag:
                        lg = jnp.where(_diag_mask(r0, tk, True), lg, 0.0)
                    hi_ref[2 * slot + h, rows, :] = lg.astype(BF16)
                _chunks(tq, split)
            yield
            for h in heads:
                suf_ref[2 * slot + h] = _dot(hi_ref[2 * slot + h], incl)
            yield
            for h in heads:
                def weights(r0, h=h):
                    rows = pl.ds(r0, RC)
                    a = jnp.exp(z_ref[2 * slot + h, rows, :] + suf_ref[2 * slot + h, rows, :] + _wide(r_ref[h, rows, :], tk))
                    if diag:
                        a = jnp.where(_diag_mask(r0, tk, True), a, 0.0)
                    p_ref[2 * slot + h, rows, :] = a.astype(BF16)
                _chunks(tq, weights)
            yield
            keys = pl.ds(off, tk)
            for h in heads:
                acc_ref[...] += _dot(p_ref[2 * slot + h], vm_ref[h, keys, :])
            for h in heads:
                r_ref[h] += _rep(suf_ref[2 * slot + h, :, 0:1])

        _tiles_reversed(i, tile)
        o_ref[...] = acc_ref[...].astype(BF16)
        t_ref[...] = jnp.where(masks[0], r_ref[0], r_ref[1])

    W = n_pairs * PAIR
    return pl.pallas_call(
        body, name="sb_fwd", grid=(n_pairs, S // tq), in_specs=_qkv_specs(S, tq, n_pairs, base),
        out_specs=[pl.BlockSpec((tq, PAIR), lambda p, i: (i, p)), pl.BlockSpec((tq, PAIR), lambda p, i: (i, p))],
        out_shape=[jax.ShapeDtypeStruct((S, W), BF16), jax.ShapeDtypeStruct((S, W), F32)],
        scratch_shapes=[pltpu.VMEM((4, tq, tk), F32), pltpu.VMEM((4, tq, tk), BF16),
                        pltpu.VMEM((4, tq, tk), F32), pltpu.VMEM((4, tq, tk), BF16), pltpu.VMEM((2, tq, 128), F32),
                        pltpu.VMEM((tq, PAIR), F32), pltpu.VMEM((2, S, PAIR), BF16)],
        compiler_params=_params(("parallel", "arbitrary")),
    )(qkv, qkv, qkv)


def _sb_bwd2(qkv, tot, do, after, n_pairs, base, tq):
    S = qkv.shape[0]
    tk = tq
    nq = S // tq
    scale = HEAD_DIM ** -0.5

    def body(q_ref, k_ref, v_ref, t_ref, do_ref, after_ref, dq_ref, dk_ref, dv_ref,
             z_ref, g_ref, omb_ref, cum_ref, hi_ref, a_ref, dz_ref,
             later_ref, cl_ref, cg_ref, tot_ref, dq_acc, dk_acc, dv_acc, ks_ref):
        i = pl.program_id(1)
        masks = _head_masks()

        @pl.when(i == 0)
        def _():
            dk_acc[...] = jnp.zeros_like(dk_acc)
            dv_acc[...] = jnp.zeros_like(dv_acc)
            _fill_masked(ks_ref, k_ref, masks, mul=scale)

        q2 = q_ref[...] * scale
        do2 = do_ref[...]
        qm = [_sel(m, q2) for m in masks]
        dom = [_sel(m, do2) for m in masks]
        t0, t1 = _per_head(t_ref[...], masks)
        tot_ref[0] = t0
        tot_ref[1] = t1
        suffix = _tri(tk, lambda r, c: r >= c)
        prefix = _tri(tk, lambda r, c: r <= c)
        cl_ref[...] = jnp.zeros_like(cl_ref)
        cg_ref[...] = jnp.zeros_like(cg_ref)
        dq_acc[...] = jnp.zeros_like(dq_acc)

        def tile(j, diag, slot, heads):
            off = pl.multiple_of(j * tk, tk)
            k2 = k_ref[pl.ds(off, tk), :]
            v2 = v_ref[pl.ds(off, tk), :]
            for h in heads:
                z_ref[2 * slot + h] = _dot(qm[h], k2, 1, 1)
                g_ref[2 * slot + h] = _dot(dom[h], v2, 1, 1)
            yield
            for h in heads:
                def split(r0, h=h):
                    rows = pl.ds(r0, RC)
                    sp = _softplus(z_ref[2 * slot + h, rows, :])
                    omb_ref[2 * slot + h, rows, :] = jnp.exp(-sp)
                    lg = -sp
                    if diag:
                        lg = jnp.where(_diag_mask(r0, tk, True), lg, 0.0)
                    hi_ref[2 * slot + h, rows, :] = lg.astype(BF16)
                _chunks(tq, split)
            yield
            for h in heads:
                cum_ref[2 * slot + h] = _dot(hi_ref[2 * slot + h], suffix)
            for h in heads:
                row_l = _rep(cum_ref[2 * slot + h, :, 0:1])
                later_ref[2 * slot + h] = tot_ref[h] - cl_ref[h] - row_l
                cl_ref[h] += row_l
            yield
            for h in heads:
                def weights(r0, h=h):
                    rows = pl.ds(r0, RC)
                    a = jnp.exp(z_ref[2 * slot + h, rows, :] + cum_ref[2 * slot + h, rows, :] + _wide(later_ref[2 * slot + h, rows, :], tk))
                    if diag:
                        a = jnp.where(_diag_mask(r0, tk, True), a, 0.0)
                    g = g_ref[2 * slot + h, rows, :] * a
                    g_ref[2 * slot + h, rows, :] = g
                    a_ref[2 * slot + h, rows, :] = a.astype(BF16)
                    hi_ref[2 * slot + h, rows, :] = g.astype(BF16)
                _chunks(tq, weights)
            yield
            for h in heads:
                cum_ref[2 * slot + h] = _dot(hi_ref[2 * slot + h], prefix)
            yield
            for h in heads:
                def dscore(r0, h=h):
                    rows = pl.ds(r0, RC)
                    g = g_ref[2 * slot + h, rows, :]
                    before = cum_ref[2 * slot + h, rows, :] - g + _wide(cg_ref[h, rows, :], tk)
                    omb = omb_ref[2 * slot + h, rows, :]
                    dz = g * omb - (1.0 - omb) * before
                    if diag:
                        dz = jnp.where(_diag_mask(r0, tk, True), dz, 0.0)
                    dz_ref[2 * slot + h, rows, :] = dz.astype(BF16)
                _chunks(tq, dscore)
            for h in heads:
                cg_ref[h] += _rep(cum_ref[2 * slot + h, :, tk - 1:tk])
            yield
            keys = pl.ds(off, tk)
            for h in heads:
                dq_acc[...] += _dot(dz_ref[2 * slot + h], ks_ref[h, keys, :])
                dk_acc[keys, :] += _dot(dz_ref[2 * slot + h], qm[h], 0, 0)
                dv_acc[keys, :] += _dot(a_ref[2 * slot + h], dom[h], 0, 0)

        _tiles(i, tile)
        dq_ref[...] = dq_acc[...].astype(BF16)

        @pl.when(i == nq - 1)
        def _():
            dk_ref[...] = dk_acc[...].astype(BF16)
            dv_ref[...] = dv_acc[...].astype(BF16)

    W = n_pairs * PAIR
    in_specs = _qkv_specs(S, tq, n_pairs, base) + [
        pl.BlockSpec((tq, PAIR), lambda p, i: (i, p)),
        pl.BlockSpec((tq, PAIR), lambda p, i: (i, p)),
        pl.BlockSpec((8, 128), lambda p, i: (0, 0))]
    out_specs = [pl.BlockSpec((tq, PAIR), lambda p, i: (i, p)),
                 pl.BlockSpec((S, PAIR), lambda p, i: (0, p)),
                 pl.BlockSpec((S, PAIR), lambda p, i: (0, p))]
    big, stat = (4, tq, tk), (2, tq, 128)
    return pl.pallas_call(
        body, name="sb_bwd", grid=(n_pairs, nq), in_specs=in_specs, out_specs=out_specs,
        out_shape=[jax.ShapeDtypeStruct((S, W), BF16)] * 3,
        scratch_shapes=[pltpu.VMEM(big, F32)] * 4 + [pltpu.VMEM(big, BF16)] * 3
        + [pltpu.VMEM((4, tq, 128), F32)] + [pltpu.VMEM(stat, F32)] * 3
        + [pltpu.VMEM((tq, PAIR), F32), pltpu.VMEM((S, PAIR), F32), pltpu.VMEM((S, PAIR), F32),
           pltpu.VMEM((2, S, PAIR), BF16)],
        compiler_params=_params(("parallel", "arbitrary")),
    )(qkv, qkv, qkv, tot, do, after)


def _fox_fwd2(qkv, fcx, fcr, n_pairs, base, tq):
    S = qkv.shape[0]
    tk = tq
    scale = HEAD_DIM ** -0.5
    spare = (HEAD_DIM, 0)

    def body(q_ref, k_ref, v_ref, fq_ref, fk_ref, o_ref, lse_ref, s_ref, p_ref, m_ref, al_ref, fqr_ref, acc_ref, vm_ref):
        i = pl.program_id(1)
        masks = _head_masks()

        @pl.when(i == 0)
        def _():
            _fill_masked(vm_ref, v_ref, masks, ones_lane=spare)

        q2 = q_ref[...] * scale
        qm = [_sel(m, q2) for m in masks]
        f0, f1 = _per_head(fq_ref[...], masks)
        fqr_ref[0] = f0
        fqr_ref[1] = f1
        m_ref[...] = jnp.full(m_ref.shape, NEG, F32)
        acc_ref[...] = jnp.zeros_like(acc_ref)

        def tile(j, diag, slot, heads):
            off = pl.multiple_of(j * tk, tk)
            k2 = k_ref[pl.ds(off, tk), :]
            v2 = v_ref[pl.ds(off, tk), :]
            fk2 = fk_ref[0, j]
            for h in heads:
                s_ref[2 * slot + h] = _dot(qm[h], k2, 1, 1)
            yield
            for h in heads:
                fk_row = fk2[h:h + 1, :]

                def probs(r0, h=h, fk_row=fk_row):
                    rows = pl.ds(r0, RC)
                    sv = s_ref[2 * slot + h, rows, :] - fk_row
                    if diag:
                        sv = jnp.where(_diag_mask(r0, tk, False), sv, NEG)
                    fq = fqr_ref[h, rows, :]
                    m_prev = m_ref[h, rows, :]
                    m_new = jnp.maximum(m_prev, jnp.max(sv, axis=-1, keepdims=True) + fq)
                    p_ref[2 * slot + h, rows, :] = jnp.exp(sv + _wide(fq - m_new, tk)).astype(BF16)
                    al_ref[2 * slot + h, rows, :] = jnp.exp(m_prev - m_new)
                    m_ref[h, rows, :] = m_new
                _chunks(tq, probs)
            yield
            for h in heads:
                acc_ref[h] = acc_ref[h] * al_ref[2 * slot + h] + _dot(p_ref[2 * slot + h], vm_ref[h, pl.ds(off, tk), :])

        _tiles(i, tile)
        a0, a1 = acc_ref[0], acc_ref[1]
        l0 = _rep(a0[:, spare[0]:spare[0] + 1])
        l1 = _rep(a1[:, spare[1]:spare[1] + 1])
        o_ref[...] = jnp.where(masks[0], a0 / l0, a1 / l1).astype(BF16)
        lse_ref[...] = jnp.where(masks[0], m_ref[0] + jnp.log(l0), m_ref[1] + jnp.log(l1))

    W = n_pairs * PAIR
    return pl.pallas_call(
        body, name="fox_fwd", grid=(n_pairs, S // tq), in_specs=_fox_specs(S, tq, n_pairs, base),
        out_specs=[pl.BlockSpec((tq, PAIR), lambda p, i: (i, p)), pl.BlockSpec((tq, PAIR), lambda p, i: (i, p))],
        out_shape=[jax.ShapeDtypeStruct((S, W), BF16), jax.ShapeDtypeStruct((S, W), F32)],
        scratch_shapes=[pltpu.VMEM((4, tq, tk), F32), pltpu.VMEM((4, tq, tk), BF16), pltpu.VMEM((2, tq, 128), F32),
                        pltpu.VMEM((4, tq, 128), F32), pltpu.VMEM((2, tq, 128), F32), pltpu.VMEM((2, tq, 128), F32),
                        pltpu.VMEM((2, S, PAIR), BF16)],
        compiler_params=_params(("parallel", "arbitrary")),
    )(qkv, qkv, qkv, fcx, fcr)


def _fox_bwd2(qkv, fcx, fcr, o, lse, do, n_pairs, base, tq):
    S = qkv.shape[0]
    tk = tq
    nq = S // tq
    scale = HEAD_DIM ** -0.5

    def body(q_ref, k_ref, v_ref, fq_ref, fk_ref, o_ref, lse_ref, do_ref,
             dq_ref, dk_ref, dv_ref, dfq_ref, dfk_ref,
             s_ref, dp_ref, p_ref, ds_ref, row_ref, dl_ref, dfq_acc, col_ref, dq_acc, dk_acc, dv_acc, ks_ref):
        i = pl.program_id(1)
        masks = _head_masks()

        @pl.when(i == 0)
        def _():
            dk_acc[...] = jnp.zeros_like(dk_acc)
            dv_acc[...] = jnp.zeros_like(dv_acc)
            dfk_ref[...] = jnp.zeros_like(dfk_ref)
            _fill_masked(ks_ref, k_ref, masks, mul=scale)

        q2 = q_ref[...] * scale
        do2 = do_ref[...]
        qm = [_sel(m, q2) for m in masks]
        dom = [_sel(m, do2) for m in masks]
        f0, f1 = _per_head(fq_ref[...], masks)
        l0, l1 = _per_head(lse_ref[...], masks)
        row_ref[0] = f0 - l0
        row_ref[1] = f1 - l1
        prod = do2.astype(F32) * o_ref[...].astype(F32)
        for h in range(2):
            dl_ref[h] = _rep(jnp.sum(jnp.where(masks[h], prod, 0.0), axis=-1, keepdims=True))
        dfq_acc[...] = jnp.zeros_like(dfq_acc)
        dq_acc[...] = jnp.zeros_like(dq_acc)

        def tile(j, diag, slot, heads):
            off = pl.multiple_of(j * tk, tk)
            k2 = k_ref[pl.ds(off, tk), :]
            v2 = v_ref[pl.ds(off, tk), :]
            fk2 = fk_ref[0, j]
            for h in heads:
                s_ref[2 * slot + h] = _dot(qm[h], k2, 1, 1)
                dp_ref[2 * slot + h] = _dot(dom[h], v2, 1, 1)
            yield
            for h in heads:
                col_ref[2 * slot + h] = jnp.zeros((8, tk), F32)
                fk_row = fk2[h:h + 1, :]

                def dscore(r0, h=h, fk_row=fk_row):
                    rows = pl.ds(r0, RC)
                    p = jnp.exp(s_ref[2 * slot + h, rows, :] - fk_row + _wide(row_ref[h, rows, :], tk))
                    if diag:
                        p = jnp.where(_diag_mask(r0, tk, False), p, 0.0)
                    ds = p * (dp_ref[2 * slot + h, rows, :] - _wide(dl_ref[h, rows, :], tk))
                    p_ref[2 * slot + h, rows, :] = p.astype(BF16)
                    ds_ref[2 * slot + h, rows, :] = ds.astype(BF16)
                    dfq_acc[h, rows, :] += _rep(jnp.sum(ds, axis=-1, keepdims=True))
                    col_ref[2 * slot + h] += jnp.sum(ds.reshape(RC // 8, 8, tk), axis=0)
                _chunks(tq, dscore)
            yield
            keys = pl.ds(off, tk)
            for h in heads:
                dq_acc[...] += _dot(ds_ref[2 * slot + h], ks_ref[h, keys, :])
                dk_acc[keys, :] += _dot(ds_ref[2 * slot + h], qm[h], 0, 0)
                dv_acc[keys, :] += _dot(p_ref[2 * slot + h], dom[h], 0, 0)
            for h in heads:
                dfk_ref[0, j, h:h + 1, :] += jnp.sum(col_ref[2 * slot + h], axis=0, keepdims=True)

        _tiles(i, tile)
        dq_ref[...] = dq_acc[...].astype(BF16)
        dfq_ref[...] = jnp.where(masks[0], dfq_acc[0], dfq_acc[1])

        @pl.when(i == nq - 1)
        def _():
            dk_ref[...] = dk_acc[...].astype(BF16)
            dv_ref[...] = dv_acc[...].astype(BF16)

    W = n_pairs * PAIR
    in_specs = _fox_specs(S, tq, n_pairs, base) + [
        pl.BlockSpec((tq, PAIR), lambda p, i: (i, p)),
        pl.BlockSpec((tq, PAIR), lambda p, i: (i, p)),
        pl.BlockSpec((tq, PAIR), lambda p, i: (i, p))]
    out_specs = [pl.BlockSpec((tq, PAIR), lambda p, i: (i, p)),
                 pl.BlockSpec((S, PAIR), lambda p, i: (0, p)),
                 pl.BlockSpec((S, PAIR), lambda p, i: (0, p)),
                 pl.BlockSpec((tq, PAIR), lambda p, i: (i, p)),
                 pl.BlockSpec((1, nq, 8, tk), lambda p, i: (p, 0, 0, 0))]
    return pl.pallas_call(
        body, name="fox_bwd", grid=(n_pairs, nq), in_specs=in_specs, out_specs=out_specs,
        out_shape=[jax.ShapeDtypeStruct((S, W), BF16)] * 3
        + [jax.ShapeDtypeStruct((S, W), F32), jax.ShapeDtypeStruct((n_pairs, nq, 8, tk), F32)],
        scratch_shapes=[pltpu.VMEM((4, tq, tk), F32)] * 2 + [pltpu.VMEM((4, tq, tk), BF16)] * 2
        + [pltpu.VMEM((2, tq, 128), F32)] * 3 + [pltpu.VMEM((4, 8, tk), F32)]
        + [pltpu.VMEM((tq, PAIR), F32), pltpu.VMEM((S, PAIR), F32), pltpu.VMEM((S, PAIR), F32),
           pltpu.VMEM((2, S, PAIR), BF16)],
        compiler_params=_params(("parallel", "arbitrary")),
    )(qkv, qkv, qkv, fcx, fcr, o, lse, do)


def _swiglu_fwd(u2, wg, wu):
    S, D = u2.shape
    FF = wg.shape[1]
    tm, tn = _pick(S, (512, 256, 128)), _divisors(FF, 1536)[0]

    def body(u_ref, g_ref, w_ref, a_ref, b_ref, h_ref):
        u = u_ref[...]
        a = _dot(u, g_ref[...])
        b = _dot(u, w_ref[...])
        a_ref[...] = a.astype(BF16)
        b_ref[...] = b.astype(BF16)
        h_ref[...] = (a / (1.0 + jnp.exp(-a)) * b).astype(BF16)

    spec_o = pl.BlockSpec((tm, tn), lambda i, j: (i, j))
    return pl.pallas_call(
        body, name="swiglu_fwd", grid=(S // tm, FF // tn),
        in_specs=[pl.BlockSpec((tm, D), lambda i, j: (i, 0)),
                  pl.BlockSpec((D, tn), lambda i, j: (0, j)),
                  pl.BlockSpec((D, tn), lambda i, j: (0, j))],
        out_specs=[spec_o] * 3, out_shape=[jax.ShapeDtypeStruct((S, FF), BF16)] * 3,
        compiler_params=_params(("parallel", "parallel")),
    )(u2, wg, wu)


def _swiglu_bwd(dh, wd, a, b):
    S, D = dh.shape
    FF = wd.shape[0]
    tm, tn = _pick(S, (512, 256, 128)), _divisors(FF, 1536)[0]

    def body(dh_ref, w_ref, a_ref, b_ref, da_ref, db_ref):
        dhin = _dot(dh_ref[...], w_ref[...], 1, 1)
        av = a_ref[...].astype(F32)
        bv = b_ref[...].astype(F32)
        sig = 1.0 / (1.0 + jnp.exp(-av))
        da_ref[...] = (dhin * bv * (sig * (1.0 + av * (1.0 - sig)))).astype(BF16)
        db_ref[...] = (dhin * (av * sig)).astype(BF16)

    spec_o = pl.BlockSpec((tm, tn), lambda i, j: (i, j))
    return pl.pallas_call(
        body, name="swiglu_bwd", grid=(S // tm, FF // tn),
        in_specs=[pl.BlockSpec((tm, D), lambda i, j: (i, 0)),
                  pl.BlockSpec((tn, D), lambda i, j: (j, 0)), spec_o, spec_o],
        out_specs=[spec_o] * 2, out_shape=[jax.ShapeDtypeStruct((S, FF), BF16)] * 2,
        compiler_params=_params(("parallel", "parallel")),
    )(dh, wd, a, b)


def _gate_bwd(dmix, wo, g_sb, g_fx, y_sb, y_fx):
    S, D = dmix.shape
    tm, tn = _pick(S, (512, 256, 128)), _divisors(D, 1024)[0]

    def body(dm_ref, w_ref, gs_ref, gf_ref, ys_ref, yf_ref, dys_ref, dyf_ref, dls_ref, dlf_ref, bs_ref, bf_ref):
        @pl.when(pl.program_id(1) == 0)
        def _():
            bs_ref[...] = jnp.zeros_like(bs_ref)
            bf_ref[...] = jnp.zeros_like(bf_ref)

        dmi = _dot(dm_ref[...], w_ref[...], 1, 1)
        gs, gf = gs_ref[...], gf_ref[...]
        dys_ref[...] = (dmi * gs).astype(BF16)
        dyf_ref[...] = (dmi * gf).astype(BF16)
        dls = dmi * ys_ref[...] * gs * (1.0 - gs)
        dlf = dmi * yf_ref[...] * gf * (1.0 - gf)
        dls_ref[...] = dls.astype(BF16)
        dlf_ref[...] = dlf.astype(BF16)
        bs_ref[0:1, :] += _colsum(dls)
        bf_ref[0:1, :] += _colsum(dlf)

    t = pl.BlockSpec((tm, tn), lambda j, i: (i, j))
    accs = pl.BlockSpec((8, tn), lambda j, i: (0, j))
    return pl.pallas_call(
        body, name="gate_bwd", grid=(D // tn, S // tm),
        in_specs=[pl.BlockSpec((tm, D), lambda j, i: (i, 0)),
                  pl.BlockSpec((tn, D), lambda j, i: (j, 0)), t, t, t, t],
        out_specs=[t, t, t, t, accs, accs],
        out_shape=[jax.ShapeDtypeStruct((S, D), BF16)] * 4 + [jax.ShapeDtypeStruct((8, D), F32)] * 2,
        compiler_params=_params(("parallel", "arbitrary")),
    )(dmix, wo, g_sb, g_fx, y_sb, y_fx)


def _local_step(x, target, ada8, lnp8, bg_sb, bg_fx, bf_pad, wqkv, wf, wgs, wgf, later_weights, send_early):
    S, D = x.shape
    W = wqkv.shape[1] // 6
    n_pairs = W // PAIR
    n_heads = W // HEAD_DIM
    ts = _pick(S, (256, 128))
    tq = _pick(S, (256, 128))

    u1 = _ln_mod(x, ada8, ts)
    qkv = _mm([(u1, wqkv)], 'nn', BF16, "in_qkv")
    f = _mm([(u1, wf)], 'nn', F32, "in_f")
    g_sb = _mm([(u1, wgs)], 'nn', F32, "in_gsb", bias=bg_sb, act='sigmoid')
    g_fx = _mm([(u1, wgf)], 'nn', F32, "in_gfx", bias=bg_fx, act='sigmoid')
    fc = _fgate_fwd(f, bf_pad, _pick(S, (512, 256, 128)))
    fch = fc[:, :n_heads]
    fcx = jnp.repeat(fch, HEAD_DIM, axis=1)
    nq = S // tq
    fcr = jnp.pad(fch.T.reshape(n_pairs, 2, nq, tq).transpose(0, 2, 1, 3),
                  ((0, 0), (0, 0), (0, 6), (0, 0)))
    o_sb, tot = _sb_fwd2(qkv, n_pairs, 0, tq)
    o_fx, lse = _fox_fwd2(qkv, fcx, fcr, n_pairs, 3 * n_pairs, tq)
    wsb, wfx, wo, wfg, wfu, wfd = later_weights(lse)
    y_sb =_mm([(o_sb, wsb)], 'nn', F32, "out_sb")
    y_fx = _mm([(o_fx, wfx)], 'nn', F32, "out_fx")
    mix_in = _gate_mix(g_sb, g_fx, y_sb, y_fx, ts)
    mix = _mm([(mix_in, wo)], 'nn', F32, "out_o")
    x1, u2 = _post_attn(x, mix, ada8, lnp8, ts)
    a, b, hin = _swiglu_fwd(u2, wfg, wfu)
    h = _mm([(hin, wfd)], 'nn', F32, "ffn_down")
    dr2, dh, st_loss = _loss_head(x1, h, target, ada8, lnp8, ts)

    da, db = _swiglu_bwd(dh, wfd, a, b)
    g_wfd = _mm([(hin, dh)], 'tn', F32, "g_ffn_down")
    du2 = _mm([(da, wfg), (db, wfu)], 'nt', F32, "d_u2")
    g_wfg = _mm([(u2, da)], 'tn', F32, "g_ffn_gate")
    g_wfu = _mm([(u2, db)], 'tn', F32, "g_ffn_up")
    dr1, dmix, st_mid = _mid_bwd(du2, x1, dr2, mix, x, ada8, lnp8, ts)
    dys, dyf, dls, dlf, gb_sb, gb_fx = _gate_bwd(dmix, wo, g_sb, g_fx, y_sb, y_fx)
    g_wo = _mm([(mix_in, dmix)], 'tn', F32, "g_w_o")
    do_sb = _mm([(dys, wsb)], 'nt', BF16, "d_o_sb")
    do_fx = _mm([(dyf, wfx)], 'nt', BF16, "d_o_fx")
    g_wsb = _mm([(o_sb, dys)], 'tn', F32, "g_sb_out")
    g_wfx = _mm([(o_fx, dyf)], 'tn', F32, "g_fox_out")
    sent = send_early(dict(sb=g_wsb, fx=g_wfx, o=g_wo, fg=g_wfg, fu=g_wfu, fd=g_wfd))
    dq_s, dk_s, dv_s = _sb_bwd2(qkv, tot, do_sb, sent, n_pairs, 0, tq)
    dq_f, dk_f, dv_f, dfq, dfk = _fox_bwd2(qkv, fcx, fcr, o_fx, lse, do_fx, n_pairs, 3 * n_pairs, tq)
    dfc = dfq[:, ::HEAD_DIM] - dfk[:, :, :2, :].transpose(0, 2, 1, 3).reshape(n_heads, S).T
    dfc = jnp.pad(dfc, ((0, 0), (0, 128 - n_heads)))
    df, gb_f = _fgate_bwd(dfc, f, bf_pad, _pick(S, (512, 256, 128)))
    grads = [dq_s, dk_s, dv_s, dq_f, dk_f, dv_f]
    du1 = _mm([(g, wqkv, W, 0, n) for n, g in enumerate(grads)] + [(df, wf), (dls, wgs), (dlf, wgf)],
              'nt', F32, "d_u1")
    g_wqkv = [_mm([(u1, g)], 'tn', F32, "g_in_%d" % n) for n, g in enumerate(grads)]
    g_wf = _mm([(u1, df)], 'tn', F32, "g_in_f")
    g_wgs = _mm([(u1, dls)], 'tn', F32, "g_in_gsb")
    g_wgf = _mm([(u1, dlf)], 'tn', F32, "g_in_gfx")
    gx, st_first = _first_bwd(du1, x, dr1, ada8, ts)

    wgrads = dict(qkv=g_wqkv, f=g_wf, gs=g_wgs, gf=g_wgf)
    stats = dict(loss=st_loss, mid=st_mid, first=st_first, gb_sb=gb_sb, gb_fx=gb_fx, gb_f=gb_f)
    return gx, wgrads, stats, dq_f


def _position():
    x, y, c = lax.axis_index("x"), lax.axis_index("y"), lax.axis_index("c")
    return x, y, c, 4 * x + 2 * y + c


def _flip(x, y, c, k):
    px = 1 - x if k & 4 else x
    py = 1 - y if k & 2 else y
    pc = 1 - c if k & 1 else c
    return (px, py, pc), 4 * px + 2 * py + pc


def _all_gather_small(v, name):
    r, n = v.shape

    def body(x_ref, out_ref, send_sems, recv_sems, local_sem):
        x, y, c, me = _position()
        mine = pltpu.make_async_copy(x_ref, out_ref.at[me], local_sem)
        mine.start()
        sends = []
        for k in range(1, N_DEV):
            peer, _ = _flip(x, y, c, k)
            cp = pltpu.make_async_remote_copy(
                src_ref=x_ref, dst_ref=out_ref.at[me], send_sem=send_sems.at[k - 1], recv_sem=recv_sems.at[k - 1],
                device_id=peer, device_id_type=MESH)
            cp.start()
            sends.append(cp)
        for k in range(1, N_DEV):
            peer, slot = _flip(x, y, c, k)
            pltpu.make_async_remote_copy(
                src_ref=x_ref, dst_ref=out_ref.at[slot], send_sem=send_sems.at[k - 1], recv_sem=recv_sems.at[k - 1],
                device_id=peer, device_id_type=MESH).wait_recv()
        for cp in sends:
            cp.wait_send()
        mine.wait()

    return pl.pallas_call(
        body, name=name, out_shape=jax.ShapeDtypeStruct((N_DEV, r, n), v.dtype),
        in_specs=[pl.BlockSpec(memory_space=pltpu.VMEM)], out_specs=pl.BlockSpec(memory_space=pltpu.VMEM),
        scratch_shapes=[pltpu.SemaphoreType.DMA((N_DEV - 1,)), pltpu.SemaphoreType.DMA((N_DEV - 1,)),
                        pltpu.SemaphoreType.DMA],
    )(v)


def _all_gather_weights(packed):
    R, C = packed.shape

    def body(x_ref, out_ref, send_sems, recv_sems, local_sem):
        x, y, c, me = _position()
        sibling, sib_slot = _flip(x, y, c, 1)
        mine = pltpu.make_async_copy(x_ref, out_ref.at[me], local_sem)
        mine.start()

        def copy(k, slot, to, src=None):
            return pltpu.make_async_remote_copy(
                src_ref=out_ref.at[slot] if src is None else src, dst_ref=out_ref.at[slot],
                send_sem=send_sems.at[k], recv_sem=recv_sems.at[k], device_id=to, device_id_type=MESH)

        first = [copy(0, me, sibling, src=x_ref)]
        chips = (4, 2, 6)
        for n, k in enumerate(chips):
            peer, _ = _flip(x, y, c, k)
            first.append(copy(1 + n, me, peer, src=x_ref))
        for cp in first:
            cp.start()
        passed = []
        for n, k in enumerate(chips):
            peer, slot = _flip(x, y, c, k)
            copy(1 + n, slot, peer).wait_recv()
            cp = copy(4 + n, slot, sibling)
            cp.start()
            passed.append(cp)
        copy(0, sib_slot, sibling).wait_recv()
        for n, k in enumerate(chips):
            _, slot = _flip(x, y, c, k | 1)
            copy(4 + n, slot, sibling).wait_recv()
        for cp in first + passed:
            cp.wait_send()
        mine.wait()

    return pl.pallas_call(
        body, name="all_gather_weights", out_shape=jax.ShapeDtypeStruct((N_DEV, R, C), packed.dtype),
        in_specs=[pl.BlockSpec(memory_space=pl.ANY)], out_specs=pl.BlockSpec(memory_space=pl.ANY),
        scratch_shapes=[pltpu.SemaphoreType.DMA((7,)), pltpu.SemaphoreType.DMA((7,)), pltpu.SemaphoreType.DMA],
    )(packed)


def _exchange_grads(gpack):
    _, R, C = gpack.shape

    def body(g_ref, out_ref, send_sems, recv_sems, local_sem):
        x, y, c, me = _position()
        mine = pltpu.make_async_copy(g_ref.at[me], out_ref.at[me], local_sem)
        mine.start()
        sends = []
        for k in range(1, N_DEV):
            peer, slot = _flip(x, y, c, k)
            cp = pltpu.make_async_remote_copy(
                src_ref=g_ref.at[slot], dst_ref=out_ref.at[me], send_sem=send_sems.at[k - 1],
                recv_sem=recv_sems.at[k - 1], device_id=peer, device_id_type=MESH)
            cp.start()
            sends.append(cp)
        for k in range(1, N_DEV):
            peer, slot = _flip(x, y, c, k)
            pltpu.make_async_remote_copy(
                src_ref=g_ref.at[slot], dst_ref=out_ref.at[slot], send_sem=send_sems.at[k - 1],
                recv_sem=recv_sems.at[k - 1], device_id=peer, device_id_type=MESH).wait_recv()
        for cp in sends:
            cp.wait_send()
        mine.wait()

    return pl.pallas_call(
        body, name="exchange_grads", out_shape=jax.ShapeDtypeStruct((N_DEV, R, C), gpack.dtype),
        in_specs=[pl.BlockSpec(memory_space=pl.ANY)], out_specs=pl.BlockSpec(memory_space=pl.ANY),
        scratch_shapes=[pltpu.SemaphoreType.DMA((N_DEV - 1,)), pltpu.SemaphoreType.DMA((N_DEV - 1,)),
                        pltpu.SemaphoreType.DMA],
    )(gpack)


_HBM = pl.BlockSpec(memory_space=pltpu.HBM)
_SEM = pl.BlockSpec(memory_space=pltpu.SEMAPHORE)
_EFFECT = pltpu.SideEffectType.DATAFLOW_SIDE_EFFECTING


def _peer_copies(src_ref, land_ref, send_sems, recv_sems, scatter, receive_side):
    x, y, c, me = _position()
    copies = []
    for k in range(1, N_DEV):
        peer, slot = _flip(x, y, c, k)
        copies.append(pltpu.make_async_remote_copy(
            src_ref=src_ref.at[slot] if scatter else src_ref,
            dst_ref=land_ref.at[slot] if receive_side else land_ref.at[me],
            send_sem=send_sems.at[k - 1], recv_sem=recv_sems.at[k - 1], device_id=peer, device_id_type=MESH))
    return copies


def _exchange_start(src, name, scatter):
    R, C = src.shape[-2:]

    def body(src_ref, land_ref, send_sems, recv_sems, src_thru, land_thru, token):
        for cp in _peer_copies(src_ref, land_ref, send_sems, recv_sems, scatter, False):
            cp.start()
        token[...] = jnp.zeros_like(token)

    land = pltpu.with_memory_space_constraint(lax.empty((N_DEV, R, C), src.dtype), pltpu.HBM)
    return pl.pallas_call(
        body, name=name,
        out_shape=(pltpu.SemaphoreType.DMA((N_DEV - 1,)), pltpu.SemaphoreType.DMA((N_DEV - 1,)),
                   pltpu.HBM(src.shape, src.dtype), pltpu.HBM((N_DEV, R, C), src.dtype),
                   jax.ShapeDtypeStruct((8, 128), F32)),
        in_specs=(_HBM, _HBM), out_specs=(_SEM, _SEM, _HBM, _HBM, pl.BlockSpec(memory_space=pltpu.VMEM)),
        input_output_aliases={0: 2, 1: 3},
        compiler_params=pltpu.CompilerParams(has_side_effects=_EFFECT),
    )(pltpu.with_memory_space_constraint(src, pltpu.HBM), land)


def _exchange_wait(send_sems, recv_sems, src_thru, land_thru, after, name, scatter):
    def body(src_ref, land_ref, send_sems, recv_sems, after_ref, src_dead, got_ref):
        for cp in _peer_copies(src_ref, land_ref, send_sems, recv_sems, scatter, True):
            cp.wait_send()
            cp.wait_recv()

    return pl.pallas_call(
        body, name=name,
        out_shape=(pltpu.HBM(src_thru.shape, src_thru.dtype), pltpu.HBM(land_thru.shape, land_thru.dtype)),
        in_specs=(_HBM, _HBM, _SEM, _SEM, pl.BlockSpec(memory_space=pl.ANY)), out_specs=(_HBM, _HBM),
        input_output_aliases={0: 0, 1: 1},
        compiler_params=pltpu.CompilerParams(has_side_effects=_EFFECT),
    )(src_thru, land_thru, send_sems, recv_sems, after)


def _own_slot(land, own):
    me = 4 * lax.axis_index("x") + 2 * lax.axis_index("y") + lax.axis_index("c")
    return lax.dynamic_update_slice(land, own[None], (me, 0, 0))


def _sum_slots(recv, name, tr):
    n, R, C = recv.shape

    def body(r_ref, o_ref):
        acc = r_ref[0].astype(F32)
        for s in range(1, n):
            acc = acc + r_ref[s].astype(F32)
        o_ref[...] = acc

    return pl.pallas_call(
        body, name=name, grid=(R // tr,), in_specs=[pl.BlockSpec((n, tr, C), lambda i: (0, i, 0))],
        out_specs=pl.BlockSpec((tr, C), lambda i: (i, 0)), out_shape=jax.ShapeDtypeStruct((R, C), F32),
        compiler_params=_params(("parallel",)),
    )(recv)


def _sum_stats(st_all, loss_row):
    n, r, D = st_all.shape

    def body(s_ref, o_ref, l_ref):
        acc = s_ref[0]
        for d in range(1, n):
            acc = acc + s_ref[d]
        o_ref[...] = acc
        l_ref[...] = jnp.zeros((8, 128), F32) + jnp.sum(acc[loss_row:loss_row + 1, :], axis=-1, keepdims=True)

    return pl.pallas_call(
        body, name="sum_stats", out_shape=[jax.ShapeDtypeStruct((r, D), F32), jax.ShapeDtypeStruct((8, 128), F32)],
    )(st_all)


def _adamw(w, g, m, v, name):
    R, C = w.shape
    tr = _pick(R, (256, 176, 128, 64, 32, 16, 8))
    c1 = 1.0 / (1.0 - ADAM_B1 ** ADAM_STEP)
    c2 = 1.0 / (1.0 - ADAM_B2 ** ADAM_STEP)

    def body(w_ref, g_ref, m_ref, v_ref, d_ref, nm_ref, nv_ref):
        gv = g_ref[...]
        nm = ADAM_B1 * m_ref[...] + (1.0 - ADAM_B1) * gv
        nv = ADAM_B2 * v_ref[...] + (1.0 - ADAM_B2) * (gv * gv)
        nm_ref[...] = nm
        nv_ref[...] = nv
        d_ref[...] = -ADAM_LR * ((nm * c1) / (jnp.sqrt(nv * c2) + ADAM_EPS) + ADAM_WD * w_ref[...])

    spec = pl.BlockSpec((tr, C), lambda i: (i, 0))
    return pl.pallas_call(
        body, name=name, grid=(R // tr,), in_specs=[spec] * 4, out_specs=[spec] * 3,
        out_shape=[jax.ShapeDtypeStruct((R, C), F32)] * 3, compiler_params=_params(("parallel",)),
    )(w, g, m, v)


def _round16(n):
    return -(-n // 16) * 16


def _pack_layout(D, in_cols, ff, W):
    parts = [("in", D * (in_cols // N_DEV) // D), ("fg", ff // N_DEV), ("fu", ff // N_DEV),
             ("sb", W * (D // N_DEV) // D), ("fx", W * (D // N_DEV) // D), ("o", D // N_DEV), ("fd", ff // N_DEV)]
    layout, off = {}, 0
    for nm, rows in parts:
        layout[nm] = (off, rows)
        off += _round16(rows)
    return layout, off


def _rows_of(a, D, rows):
    a = a.reshape(rows, D)
    return jnp.pad(a, ((0, _round16(rows) - rows), (0, 0)))


def _cols_to_dest(g, D):
    K, N = g.shape
    n = N // N_DEV
    return g.reshape(K, N_DEV, n).transpose(1, 0, 2).reshape(N_DEV, K * n // D, D)


def _cols_from_src(blocks, K, n):
    return blocks.reshape(N_DEV, K, n).transpose(1, 0, 2).reshape(K, N_DEV * n)


def _pad_rows16(a):
    rows = a.shape[1]
    return jnp.pad(a, ((0, 0), (0, _round16(rows) - rows), (0, 0)))


def kernel(x, c, w_ada, b_ada, w_in, b_gate, b_forget, w_sb_out, w_fox_out, w_o, ln1_g, ln1_b, w_ffn_gate, w_ffn_up, w_ffn_down, ln2_g, ln2_b, loss_target, m_w_ada, m_b_ada, m_w_in, m_b_gate, m_b_forget, m_w_sb_out, m_w_fox_out, m_w_o, m_ln1_g, m_ln1_b, m_w_ffn_gate, m_w_ffn_up, m_w_ffn_down, m_ln2_g, m_ln2_b, v_w_ada, v_b_ada, v_w_in, v_b_gate, v_b_forget, v_w_sb_out, v_w_fox_out, v_w_o, v_ln1_g, v_ln1_b, v_w_ffn_gate, v_w_ffn_up, v_w_ffn_down, v_ln2_g, v_ln2_b):
    S, D = x.shape[1], x.shape[2]
    W = w_sb_out.shape[1]
    n_heads = b_forget.shape[1]
    ff = w_ffn_down.shape[1] * N_DEV
    in_loc = w_in.shape[2]
    in_cols = in_loc * N_DEV
    ada_loc = w_ada.shape[2]
    n_cond = ada_loc * N_DEV // D
    assert w_ada.shape[0] == 1 and n_cond == 6 and in_cols == 6 * W + n_heads + 2 * D and n_heads <= 128
    me = 4 * lax.axis_index("x") + 2 * lax.axis_index("y") + lax.axis_index("c")

    c_all = _all_gather_small(c, "gather_c").reshape(N_DEV, D)
    c16 = jnp.pad(c_all, ((0, 16 - N_DEV), (0, 0)))
    b_cols = lax.dynamic_slice(b_ada, (0, me * ada_loc), (1, ada_loc))
    ada_cols = _mm([(c16, w_ada[0])], 'nn', F32, "ada_fwd", bias=b_cols, silu_a=True)[:N_DEV]
    ada_all = _all_gather_small(ada_cols, "gather_ada")
    ada_me = lax.dynamic_index_in_dim(ada_all, me, axis=1, keepdims=False)
    ada8 = jnp.pad(ada_me.reshape(n_cond, D), ((0, 8 - n_cond), (0, 0)))
    lnp8 = jnp.concatenate([ln1_g, ln1_b, ln2_g, ln2_b, jnp.zeros((4, D), F32)], axis=0)

    layout, R = _pack_layout(D, in_cols, ff, W)
    shards = dict(**{"in": w_in[0]}, fg=w_ffn_gate[0], fu=w_ffn_up[0], sb=w_sb_out[0], fx=w_fox_out[0], o=w_o[0],
                  fd=w_ffn_down[0])
    later = [nm for nm in layout if nm != "in"]
    r_in = _round16(layout["in"][1])
    first = _rows_of(shards["in"].astype(BF16), D, layout["in"][1])
    rest = jnp.concatenate([_rows_of(shards[nm].astype(BF16), D, layout[nm][1]) for nm in later], axis=0)
    gathered_in = _all_gather_weights(first)
    rest_sems = _exchange_start(rest, "gather_rest_start", False)
    ada8 = ada8 + rest_sems[4][0:1, 0:1]

    w_in_full = _cols_from_src(gathered_in[:, :layout["in"][1], :], D, in_loc)
    wqkv = w_in_full[:, :6 * W]
    wf = jnp.pad(w_in_full[:, 6 * W:6 * W + n_heads], ((0, 0), (0, 128 - n_heads)))
    wgs = w_in_full[:, 6 * W + n_heads:6 * W + n_heads + D]
    wgf = w_in_full[:, 6 * W + n_heads + D:]
    bf_pad = jnp.pad(b_forget, ((0, 0), (0, 128 - n_heads)))

    def later_weights(after):
        mine, land = _exchange_wait(*rest_sems[:4], after, "gather_rest_wait", False)
        gathered = _own_slot(land, mine)

        def part(nm):
            off, rows = layout[nm]
            return gathered[:, off - r_in:off - r_in + rows, :]

        return (_cols_from_src(part("sb"), W, D // N_DEV), _cols_from_src(part("fx"), W, D // N_DEV),
                part("o").reshape(D, D), _cols_from_src(part("fg"), D, ff // N_DEV),
                _cols_from_src(part("fu"), D, ff // N_DEV), part("fd").reshape(ff, D))

    early = {}

    def send_early(g):
        dest = {"fg": _cols_to_dest(g["fg"], D), "fu": _cols_to_dest(g["fu"], D), "sb": _cols_to_dest(g["sb"], D),
                "fx": _cols_to_dest(g["fx"], D), "o": g["o"].reshape(N_DEV, D // N_DEV, D),
                "fd": g["fd"].reshape(N_DEV, ff // N_DEV, D)}
        pack = jnp.concatenate([_pad_rows16(dest[nm].astype(BF16)) for nm in later], axis=1)
        early["sems"] = _exchange_start(pack, "grads_rest_start", True)
        return early["sems"][4]

    gx, wg, st, last = _local_step(x[0], loss_target[0], ada8, lnp8, b_gate[:, :D], b_gate[:, D:], bf_pad,
                                   wqkv, wf, wgs, wgf, later_weights, send_early)

    pack, land = _exchange_wait(*early["sems"][:4], last, "grads_rest_wait", True)
    own = lax.dynamic_index_in_dim(pack, me, axis=0, keepdims=False)
    gsum_rest = _sum_slots(_own_slot(land, own), "sum_grads_rest", _pick(R - r_in, (512, 656, 256, 128, 64, 16)))
    g_in = jnp.concatenate(wg["qkv"] + [wg["f"][:, :n_heads], wg["gs"], wg["gf"]], axis=1)
    recv_in = _exchange_grads(_pad_rows16(_cols_to_dest(g_in, D).astype(BF16)))
    gsum_in = _sum_slots(recv_in, "sum_grads_in", _pick(r_in, (512, 656, 256, 128, 64, 16)))

    def gshard(nm, shape):
        off, rows = layout[nm]
        if nm == "in":
            return gsum_in[:rows].reshape(shape)
        return gsum_rest[off - r_in:off - r_in + rows].reshape(shape)

    zrow = jnp.zeros((1, D), F32)
    gb_f_row = jnp.pad(st["gb_f"][0:1], ((0, 0), (0, D - 128)))
    stats16 = jnp.concatenate([
        st["first"][1:2], st["first"][0:1], st["mid"][4:5], st["mid"][1:2], st["mid"][0:1], st["loss"][3:4],
        st["mid"][2:3], st["mid"][3:4], st["loss"][1:2], st["loss"][2:3], st["gb_sb"][0:1], st["gb_fx"][0:1],
        st["loss"][0:1], gb_f_row, zrow, zrow], axis=0)
    st_all = _all_gather_small(stats16, "gather_stats")
    st_sum, loss_blk = _sum_stats(st_all, 12)
    loss = loss_blk[0, 0]

    d_ada_all = st_all[:, :n_cond, :].reshape(N_DEV, n_cond * D)
    d_cols = lax.dynamic_slice(d_ada_all, (0, me * ada_loc), (N_DEV, ada_loc))
    d16 = jnp.pad(d_cols, ((0, 16 - N_DEV), (0, 0)))
    g_w_ada = _mm([(c16, d16)], 'tn', F32, "ada_wgrad", silu_a=True)

    small_w = jnp.concatenate([b_ada.reshape(n_cond, D), ln1_g, ln1_b, ln2_g, ln2_b, b_gate.reshape(2, D), zrow,
                               jnp.pad(b_forget, ((0, 0), (0, D - n_heads))), zrow, zrow], axis=0)
    small_m = jnp.concatenate([m_b_ada.reshape(n_cond, D), m_ln1_g, m_ln1_b, m_ln2_g, m_ln2_b, m_b_gate.reshape(2, D),
                               zrow, jnp.pad(m_b_forget, ((0, 0), (0, D - n_heads))), zrow, zrow], axis=0)
    small_v = jnp.concatenate([v_b_ada.reshape(n_cond, D), v_ln1_g, v_ln1_b, v_ln2_g, v_ln2_b, v_b_gate.reshape(2, D),
                               zrow, jnp.pad(v_b_forget, ((0, 0), (0, D - n_heads))), zrow, zrow], axis=0)
    sm = _adamw(small_w, st_sum, small_m, small_v, "adamw_small")

    def small(a, nm):
        if nm == "b_ada":
            return a[0:n_cond].reshape(1, n_cond * D)
        if nm == "b_gate":
            return a[10:12].reshape(1, 2 * D)
        if nm == "b_forget":
            return a[13:14, :n_heads]
        row = {"ln1_g": 6, "ln1_b": 7, "ln2_g": 8, "ln2_b": 9}[nm]
        return a[row:row + 1]

    big = {
        "w_ada": (w_ada[0], g_w_ada, m_w_ada[0], v_w_ada[0]),
        "w_in": (w_in[0], gshard("in", w_in.shape[1:]), m_w_in[0], v_w_in[0]),
        "w_sb_out": (w_sb_out[0], gshard("sb", w_sb_out.shape[1:]), m_w_sb_out[0], v_w_sb_out[0]),
        "w_fox_out": (w_fox_out[0], gshard("fx", w_fox_out.shape[1:]), m_w_fox_out[0], v_w_fox_out[0]),
        "w_o": (w_o[0], gshard("o", w_o.shape[1:]), m_w_o[0], v_w_o[0]),
        "w_ffn_gate": (w_ffn_gate[0], gshard("fg", w_ffn_gate.shape[1:]), m_w_ffn_gate[0], v_w_ffn_gate[0]),
        "w_ffn_up": (w_ffn_up[0], gshard("fu", w_ffn_up.shape[1:]), m_w_ffn_up[0], v_w_ffn_up[0]),
        "w_ffn_down": (w_ffn_down[0], gshard("fd", w_ffn_down.shape[1:]), m_w_ffn_down[0], v_w_ffn_down[0]),
    }
    order = ["w_ada", "b_ada", "w_in", "b_gate", "b_forget", "w_sb_out", "w_fox_out", "w_o", "ln1_g", "ln1_b",
             "w_ffn_gate", "w_ffn_up", "w_ffn_down", "ln2_g", "ln2_b"]
    grads, deltas, new_ms, new_vs = [], [], [], []
    for nm in order:
        if nm in big:
            w, g, m, v = big[nm]
            d, nm_, nv_ = _adamw(w, g, m, v, "adamw_" + nm)
            grads.append(g[None])
            deltas.append(d[None])
            new_ms.append(nm_[None])
            new_vs.append(nv_[None])
        else:
            grads.append(small(st_sum, nm))
            deltas.append(small(sm[0], nm))
            new_ms.append(small(sm[1], nm))
            new_vs.append(small(sm[2], nm))
    return (loss, gx[None], *grads, *deltas, *new_ms, *new_vs)
```

```python
import functools

import jax
import jax.numpy as jnp
import numpy as np
from jax import lax
from jax.experimental import pallas as pl
from jax.experimental.pallas import tpu as pltpu

F32 = jnp.float32
BF16 = jnp.bfloat16

HEAD_DIM = 64
PAIR = 2 * HEAD_DIM
LN_EPS = 1e-5
ALPHA = 2.0 ** 0.25
ADAM_LR, ADAM_B1, ADAM_B2, ADAM_EPS, ADAM_WD, ADAM_STEP = 0.001, 0.9, 0.999, 1e-08, 0.01, 10
N_DEV = 8
VMEM_LIMIT = 56 * 1024 * 1024
MESH = pl.DeviceIdType.MESH


def _dot(a, b, ca=1, cb=0):
    return lax.dot_general(a, b, (((ca,), (cb,)), ((), ())), preferred_element_type=F32)


def _pick(n, cands):
    for t in cands:
        if n % t == 0:
            return t
    return n


def _params(sem):
    return pltpu.CompilerParams(dimension_semantics=sem, vmem_limit_bytes=VMEM_LIMIT)


MM_BLOCK_BYTES = 40 * 1024 * 1024
LANES = 128


def _divisors(n, cap):
    ds = [d for d in range(LANES, min(n, cap) + 1, LANES) if n % d == 0]
    return sorted(ds, reverse=True) or [n]


def _mm_tiles(M, N, a_row_bytes, b_row_bytes, out_itemsize):
    best = None
    for tm in _divisors(M, 1024):
        for tn in _divisors(N, 2048):
            need = 2 * (tm * a_row_bytes + tn * b_row_bytes + tm * tn * out_itemsize) + tm * tn * 4
            if need <= MM_BLOCK_BYTES and (best is None or (tm * tn, tm) > (best[0] * best[1], best[0])):
                best = (tm, tn)
    assert best is not None, (M, N, a_row_bytes, b_row_bytes)
    return best


def _mm(pairs, mode, out_dtype, name, bias=None, act=None, silu_a=False):
    norm = []
    for p in pairs:
        a, b = p[0], p[1]
        kdim_a = a.shape[0] if mode == 'tn' else a.shape[1]
        K, ka, kb = (p[2], p[3], p[4]) if len(p) > 2 else (kdim_a, 0, 0)
        norm.append((a, b, K, ka, kb))
    a0, b0 = norm[0][0], norm[0][1]
    M = a0.shape[1] if mode == 'tn' else a0.shape[0]
    N = b0.shape[0] if mode == 'nt' else b0.shape[1]
    tm, tn = _mm_tiles(M, N, sum(K * a.dtype.itemsize for a, _, K, _, _ in norm),
                       sum(K * b.dtype.itemsize for _, b, K, _, _ in norm), jnp.dtype(out_dtype).itemsize)
    n_pairs = len(norm)

    in_specs, args = [], []
    for a, b, K, ka, kb in norm:
        if mode == 'tn':
            in_specs.append(pl.BlockSpec((K, tm), lambda i, j, ka=ka: (ka, i)))
        else:
            in_specs.append(pl.BlockSpec((tm, K), lambda i, j, ka=ka: (i, ka)))
        if mode == 'nt':
            in_specs.append(pl.BlockSpec((tn, K), lambda i, j, kb=kb: (j, kb)))
        else:
            in_specs.append(pl.BlockSpec((K, tn), lambda i, j, kb=kb: (kb, j)))
        args += [a, b]
    if bias is not None:
        in_specs.append(pl.BlockSpec((1, tn), lambda i, j: (0, j)))
        args.append(bias)

    ca = 0 if mode == 'tn' else 1
    cb = 1 if mode == 'nt' else 0

    def body(*refs):
        o_ref = refs[-1]
        acc = None
        for p in range(n_pairs):
            av = refs[2 * p][...]
            if silu_a:
                av = av / (1.0 + jnp.exp(-av))
            d = _dot(av.astype(BF16), refs[2 * p + 1][...].astype(BF16), ca, cb)
            acc = d if acc is None else acc + d
        if bias is not None:
            acc = acc + refs[2 * n_pairs][...]
        if act == 'sigmoid':
            acc = 1.0 / (1.0 + jnp.exp(-acc))
        o_ref[...] = acc.astype(out_dtype)

    return pl.pallas_call(
        body, name=name, grid=(M // tm, N // tn), in_specs=in_specs,
        out_specs=pl.BlockSpec((tm, tn), lambda i, j: (i, j)),
        out_shape=jax.ShapeDtypeStruct((M, N), out_dtype),
        compiler_params=_params(("parallel", "parallel")),
    )(*args)


def _rows_call(body, name, row_ins, vec_ins, row_outs, acc_outs, ts):
    S = row_ins[0].shape[0]
    in_specs = [pl.BlockSpec((ts, a.shape[1]), lambda i: (i, 0)) for a in row_ins]
    in_specs += [pl.BlockSpec(a.shape, lambda i: (0, 0)) for a in vec_ins]
    out_specs = [pl.BlockSpec((ts, c), lambda i: (i, 0)) for c, _ in row_outs]
    out_specs += [pl.BlockSpec(s, lambda i: (0, 0)) for s in acc_outs]
    out_shape = [jax.ShapeDtypeStruct((S, c), dt) for c, dt in row_outs]
    out_shape += [jax.ShapeDtypeStruct(s, F32) for s in acc_outs]
    return pl.pallas_call(
        body, name=name, grid=(S // ts,), in_specs=in_specs, out_specs=out_specs, out_shape=out_shape,
        compiler_params=_params(("arbitrary",)),
    )(*row_ins, *vec_ins)


def _ln_stats(v):
    mu = jnp.mean(v, axis=-1, keepdims=True)
    d = v - mu
    var = jnp.mean(d * d, axis=-1, keepdims=True)
    rstd = lax.rsqrt(var + LN_EPS)
    return d * rstd, rstd


def _ln_bwd(dxhat, xhat, rstd):
    m1 = jnp.mean(dxhat, axis=-1, keepdims=True)
    m2 = jnp.mean(dxhat * xhat, axis=-1, keepdims=True)
    return rstd * (dxhat - m1 - xhat * m2)


def _colsum(v):
    return jnp.sum(v, axis=0, keepdims=True)


def _ln_mod(x, ada8, ts):
    D = x.shape[1]

    def body(x_ref, v_ref, u_ref):
        xhat, _ = _ln_stats(x_ref[...])
        u_ref[...] = (xhat * (1.0 + v_ref[1:2, :]) + v_ref[0:1, :]).astype(BF16)

    return _rows_call(body, "ln_mod", [x], [ada8], [(D, BF16)], [], ts)[0]


def _gate_mix(g_sb, g_fx, y_sb, y_fx, ts):
    D = y_sb.shape[1]

    def body(gs, gf, ys, yf, o_ref):
        o_ref[...] = (gs[...] * ys[...] + gf[...] * yf[...]).astype(BF16)

    return _rows_call(body, "gate_mix", [g_sb, g_fx, y_sb, y_fx], [], [(D, BF16)], [], ts)[0]


def _post_attn(x, mix, ada8, lnp8, ts):
    D = x.shape[1]

    def body(x_ref, mix_ref, v_ref, p_ref, x1_ref, u2_ref):
        r1 = ALPHA * x_ref[...] + v_ref[2:3, :] * mix_ref[...]
        xhat, _ = _ln_stats(r1)
        x1 = xhat * p_ref[0:1, :] + p_ref[1:2, :]
        x1_ref[...] = x1
        xh1, _ = _ln_stats(x1)
        u2_ref[...] = (xh1 * (1.0 + v_ref[4:5, :]) + v_ref[3:4, :]).astype(BF16)

    return _rows_call(body, "post_attn", [x, mix], [ada8, lnp8], [(D, F32), (D, BF16)], [], ts)


def _loss_head(x1, h, target, ada8, lnp8, ts):
    D = x1.shape[1]

    def body(x1_ref, h_ref, t_ref, v_ref, p_ref, dr2_ref, dh_ref, st_ref):
        @pl.when(pl.program_id(0) == 0)
        def _():
            st_ref[...] = jnp.zeros_like(st_ref)

        hv = h_ref[...]
        g2 = v_ref[5:6, :]
        r2 = ALPHA * x1_ref[...] + g2 * hv
        xhat, rstd = _ln_stats(r2)
        y = xhat * p_ref[2:3, :] + p_ref[3:4, :]
        err = y - t_ref[...]
        dy = err * (1.0 / D)
        dr2 = _ln_bwd(dy * p_ref[2:3, :], xhat, rstd)
        dr2_ref[...] = dr2
        dh_ref[...] = (dr2 * g2).astype(BF16)
        st_ref[0:1, :] += _colsum(err * err) * (0.5 / D)
        st_ref[1:2, :] += _colsum(dy * xhat)
        st_ref[2:3, :] += _colsum(dy)
        st_ref[3:4, :] += _colsum(dr2 * hv)

    return _rows_call(body, "loss_head", [x1, h, target], [ada8, lnp8], [(D, F32), (D, BF16)], [(8, D)], ts)


def _mid_bwd(du2, x1, dr2, mix, x, ada8, lnp8, ts):
    D = x.shape[1]

    def body(du2_ref, x1_ref, dr2_ref, mix_ref, x_ref, v_ref, p_ref, dr1_ref, dmix_ref, st_ref):
        @pl.when(pl.program_id(0) == 0)
        def _():
            st_ref[...] = jnp.zeros_like(st_ref)

        du2v = du2_ref[...]
        xh1, rstd1 = _ln_stats(x1_ref[...])
        dx1 = ALPHA * dr2_ref[...] + _ln_bwd(du2v * (1.0 + v_ref[4:5, :]), xh1, rstd1)
        mixv = mix_ref[...]
        g1 = v_ref[2:3, :]
        r1 = ALPHA * x_ref[...] + g1 * mixv
        xhr, rstdr = _ln_stats(r1)
        dr1 = _ln_bwd(dx1 * p_ref[0:1, :], xhr, rstdr)
        dr1_ref[...] = dr1
        dmix_ref[...] = (dr1 * g1).astype(BF16)
        st_ref[0:1, :] += _colsum(du2v * xh1)
        st_ref[1:2, :] += _colsum(du2v)
        st_ref[2:3, :] += _colsum(dx1 * xhr)
        st_ref[3:4, :] += _colsum(dx1)
        st_ref[4:5, :] += _colsum(dr1 * mixv)

    return _rows_call(body, "mid_bwd", [du2, x1, dr2, mix, x], [ada8, lnp8], [(D, F32), (D, BF16)], [(8, D)], ts)


def _first_bwd(du1, x, dr1, ada8, ts):
    D = x.shape[1]

    def body(du1_ref, x_ref, dr1_ref, v_ref, gx_ref, st_ref):
        @pl.when(pl.program_id(0) == 0)
        def _():
            st_ref[...] = jnp.zeros_like(st_ref)

        du1v = du1_ref[...]
        xh0, rstd0 = _ln_stats(x_ref[...])
        gx_ref[...] = ALPHA * dr1_ref[...] + _ln_bwd(du1v * (1.0 + v_ref[1:2, :]), xh0, rstd0)
        st_ref[0:1, :] += _colsum(du1v * xh0)
        st_ref[1:2, :] += _colsum(du1v)

    return _rows_call(body, "first_bwd", [du1, x, dr1], [ada8], [(D, F32)], [(8, D)], ts)


def _split3(v):
    hi = v.astype(BF16)
    r = v - hi.astype(F32)
    mid = r.astype(BF16)
    lo = (r - mid.astype(F32)).astype(BF16)
    return hi, mid, lo


def _fgate_fwd(f, bf_pad, tb):
    S = f.shape[0]

    def body(f_ref, b_ref, fc_ref, carry):
        @pl.when(pl.program_id(0) == 0)
        def _():
            carry[...] = jnp.zeros_like(carry)

        z = f_ref[...] + b_ref[...]
        ls = jnp.minimum(z, 0.0) - jnp.log(1.0 + jnp.exp(-jnp.abs(z)))
        r = lax.broadcasted_iota(jnp.int32, (tb, tb), 0)
        c = lax.broadcasted_iota(jnp.int32, (tb, tb), 1)
        tri = (c <= r).astype(BF16)
        hi, mid, lo = _split3(ls)
        cs = _dot(tri, hi) + _dot(tri, mid) + _dot(tri, lo) + carry[...]
        fc_ref[...] = cs
        carry[...] = cs[tb - 1:tb, :]

    return pl.pallas_call(
        body, name="fgate_fwd", grid=(S // tb,),
        in_specs=[pl.BlockSpec((tb, 128), lambda i: (i, 0)), pl.BlockSpec((1, 128), lambda i: (0, 0))],
        out_specs=pl.BlockSpec((tb, 128), lambda i: (i, 0)),
        out_shape=jax.ShapeDtypeStruct((S, 128), F32),
        scratch_shapes=[pltpu.VMEM((1, 128), F32)],
        compiler_params=_params(("arbitrary",)),
    )(f, bf_pad)


def _fgate_bwd(dfc, f, bf_pad, tb):
    S = f.shape[0]
    nb = S // tb

    def body(d_ref, f_ref, b_ref, df_ref, gb_ref, carry):
        @pl.when(pl.program_id(0) == 0)
        def _():
            carry[...] = jnp.zeros_like(carry)
            gb_ref[...] = jnp.zeros_like(gb_ref)

        r = lax.broadcasted_iota(jnp.int32, (tb, tb), 0)
        c = lax.broadcasted_iota(jnp.int32, (tb, tb), 1)
        tri = (c >= r).astype(BF16)
        hi, mid, lo = _split3(d_ref[...])
        rs = _dot(tri, hi) + _dot(tri, mid) + _dot(tri, lo) + carry[...]
        carry[...] = rs[0:1, :]
        z = f_ref[...] + b_ref[...]
        df = rs * (1.0 / (1.0 + jnp.exp(z)))
        df_ref[...] = df
        gb_ref[0:1, :] += _colsum(df)

    return pl.pallas_call(
        body, name="fgate_bwd", grid=(nb,),
        in_specs=[pl.BlockSpec((tb, 128), lambda i: (nb - 1 - i, 0)),
                  pl.BlockSpec((tb, 128), lambda i: (nb - 1 - i, 0)),
                  pl.BlockSpec((1, 128), lambda i: (0, 0))],
        out_specs=[pl.BlockSpec((tb, 128), lambda i: (nb - 1 - i, 0)), pl.BlockSpec((8, 128), lambda i: (0, 0))],
        out_shape=[jax.ShapeDtypeStruct((S, 128), F32), jax.ShapeDtypeStruct((8, 128), F32)],
        scratch_shapes=[pltpu.VMEM((1, 128), F32)],
        compiler_params=_params(("arbitrary",)),
    )(dfc, f, bf_pad)


def _split2(v):
    hi = v.astype(BF16)
    lo = (v - hi.astype(F32)).astype(BF16)
    return hi, lo


def _head_masks():
    lane = lax.broadcasted_iota(jnp.int32, (1, PAIR), 1)
    m0 = lane < HEAD_DIM
    return m0, jnp.logical_not(m0)


def _sel(mask, v):
    return jnp.where(mask, v, jnp.zeros_like(v))


def _softplus(z):
    return jnp.maximum(z, 0.0) + jnp.log(1.0 + jnp.exp(-jnp.abs(z)))


def _qkv_specs(S, tq, n_pairs, base):
    return [pl.BlockSpec((tq, PAIR), lambda p, i: (i, base + p)),
            pl.BlockSpec((S, PAIR), lambda p, i: (0, base + n_pairs + p)),
            pl.BlockSpec((S, PAIR), lambda p, i: (0, base + 2 * n_pairs + p))]


def _sb_fwd(qkv, n_pairs, base, tq):
    S = qkv.shape[0]
    tk = tq
    scale = HEAD_DIM ** -0.5

    def body(q_ref, k_ref, v_ref, o_ref, t_ref, acc_ref):
        i = pl.program_id(1)
        masks = _head_masks()
        q2 = q_ref[...]
        qm = [_sel(m, q2) for m in masks]
        rowpos = i * tq + lax.broadcasted_iota(jnp.int32, (tq, tk), 0)
        colin = lax.broadcasted_iota(jnp.int32, (tq, tk), 1)
        upper = (lax.broadcasted_iota(jnp.int32, (tk, tk), 0) > lax.broadcasted_iota(jnp.int32, (tk, tk), 1)).astype(BF16)
        acc_ref[...] = jnp.zeros_like(acc_ref)

        def step(jj, carry):
            j = i - jj
            off = pl.multiple_of(j * tk, tk)
            k2 = k_ref[pl.ds(off, tk), :]
            v2 = v_ref[pl.ds(off, tk), :]
            mask = (j * tk + colin) < rowpos
            out = None
            new = []
            for h in heads:
                z = _dot(qm[h], k2, 1, 1) * scale
                sp = _softplus(z)
                lg = jnp.where(mask, -sp, 0.0)
                hi, lo = _split2(lg)
                suf = _dot(hi, upper) + _dot(lo, upper)
                a = jnp.where(mask, jnp.exp(z - sp + suf + carry[h]), 0.0)
                d = _dot(a.astype(BF16), _sel(masks[h], v2))
                out = d if out is None else out + d
                new.append(carry[h] + jnp.sum(lg, axis=-1, keepdims=True))
            acc_ref[...] += out
            return tuple(new)

        zero = jnp.zeros((tq, 1), F32)
        r0, r1 = lax.fori_loop(0, i + 1, step, (zero, zero))
        o_ref[...] = acc_ref[...].astype(BF16)
        t_ref[...] = jnp.where(masks[0], r0, r1)

    W = n_pairs * PAIR
    return pl.pallas_call(
        body, name="sb_fwd", grid=(n_pairs, S // tq), in_specs=_qkv_specs(S, tq, n_pairs, base),
        out_specs=[pl.BlockSpec((tq, PAIR), lambda p, i: (i, p)), pl.BlockSpec((tq, PAIR), lambda p, i: (i, p))],
        out_shape=[jax.ShapeDtypeStruct((S, W), BF16), jax.ShapeDtypeStruct((S, W), F32)],
        scratch_shapes=[pltpu.VMEM((tq, PAIR), F32)],
        compiler_params=_params(("parallel", "arbitrary")),
    )(qkv, qkv, qkv)


def _sb_bwd(qkv, tot, do, n_pairs, base, tq):
    S = qkv.shape[0]
    tk = tq
    nq = S // tq
    scale = HEAD_DIM ** -0.5

    def body(q_ref, k_ref, v_ref, t_ref, do_ref, dq_ref, dk_ref, dv_ref, dq_acc, dk_acc, dv_acc):
        i = pl.program_id(1)
        masks = _head_masks()

        @pl.when(i == 0)
        def _():
            dk_acc[...] = jnp.zeros_like(dk_acc)
            dv_acc[...] = jnp.zeros_like(dv_acc)

        q2 = q_ref[...]
        do2 = do_ref[...]
        qm = [_sel(m, q2) for m in masks]
        dom = [_sel(m, do2) for m in masks]
        t2 = t_ref[...]
        tot_h = [t2[:, 0:1], t2[:, HEAD_DIM:HEAD_DIM + 1]]
        rowpos = i * tq + lax.broadcasted_iota(jnp.int32, (tq, tk), 0)
        colin = lax.broadcasted_iota(jnp.int32, (tq, tk), 1)
        r_i = lax.broadcasted_iota(jnp.int32, (tk, tk), 0)
        c_i = lax.broadcasted_iota(jnp.int32, (tk, tk), 1)
        upper = (r_i > c_i).astype(BF16)
        lower = (r_i < c_i).astype(BF16)
        dq_acc[...] = jnp.zeros_like(dq_acc)

        def step(j, carry):
            off = pl.multiple_of(j * tk, tk)
            k2 = k_ref[pl.ds(off, tk), :]
            v2 = v_ref[pl.ds(off, tk), :]
            mask = (j * tk + colin) < rowpos
            dq = None
            dk = None
            dv = None
            new = []
            for h in heads:
                cum_l, cum_g = carry[2 * h], carry[2 * h + 1]
                z = _dot(qm[h], k2, 1, 1) * scale
                sp = _softplus(z)
                lg = jnp.where(mask, -sp, 0.0)
                hi, lo = _split2(lg)
                suf = _dot(hi, upper) + _dot(lo, upper)
                row_l = jnp.sum(lg, axis=-1, keepdims=True)
                later = tot_h[h] - cum_l - row_l
                a = jnp.where(mask, jnp.exp(z - sp + suf + later), 0.0)
                da = _dot(dom[h], v2, 1, 1)
                g = da * a
                ghi, glo = _split2(g)
                pre = _dot(ghi, lower) + _dot(glo, lower) + cum_g
                one_m_beta = jnp.exp(-sp)
                dz = jnp.where(mask, g * one_m_beta - (1.0 - one_m_beta) * pre, 0.0)
                dzb = (dz * scale).astype(BF16)
                d1 = _dot(dzb, _sel(masks[h], k2))
                d2 = _dot(dzb, qm[h], 0, 0)
                d3 = _dot(a.astype(BF16), dom[h], 0, 0)
                dq = d1 if dq is None else dq + d1
                dk = d2 if dk is None else dk + d2
                dv = d3 if dv is None else dv + d3
                new += [cum_l + row_l, cum_g + jnp.sum(g, axis=-1, keepdims=True)]
            dq_acc[...] += dq
            dk_acc[pl.ds(off, tk), :] += dk
            dv_acc[pl.ds(off, tk), :] += dv
            return tuple(new)

        zero = jnp.zeros((tq, 1), F32)
        lax.fori_loop(0, i + 1, step, (zero, zero, zero, zero))
        dq_ref[...] = dq_acc[...].astype(BF16)

        @pl.when(i == nq - 1)
        def _():
            dk_ref[...] = dk_acc[...].astype(BF16)
            dv_ref[...] = dv_acc[...].astype(BF16)

    W = n_pairs * PAIR
    in_specs = _qkv_specs(S, tq, n_pairs, base) + [
        pl.BlockSpec((tq, PAIR), lambda p, i: (i, p)),
        pl.BlockSpec((tq, PAIR), lambda p, i: (i, p))]
    out_specs = [pl.BlockSpec((tq, PAIR), lambda p, i: (i, p)),
                 pl.BlockSpec((S, PAIR), lambda p, i: (0, p)),
                 pl.BlockSpec((S, PAIR), lambda p, i: (0, p))]
    dq, dk, dv = pl.pallas_call(
        body, name="sb_bwd", grid=(n_pairs, nq), in_specs=in_specs, out_specs=out_specs,
        out_shape=[jax.ShapeDtypeStruct((S, W), BF16)] * 3,
        scratch_shapes=[pltpu.VMEM((tq, PAIR), F32), pltpu.VMEM((S, PAIR), F32), pltpu.VMEM((S, PAIR), F32)],
        compiler_params=_params(("parallel", "arbitrary")),
    )(qkv, qkv, qkv, tot, do)
    return dq, dk, dv


NEG = -1e30


def _fox_specs(S, tq, n_pairs, base):
    return _qkv_specs(S, tq, n_pairs, base) + [
        pl.BlockSpec((tq, PAIR), lambda p, i: (i, p)),
        pl.BlockSpec((1, S // tq, 8, tq), lambda p, i: (p, 0, 0, 0))]


def _fox_fwd(qkv, fcx, fcr, n_pairs, base, tq):
    S = qkv.shape[0]
    tk = tq
    scale = HEAD_DIM ** -0.5

    def body(q_ref, k_ref, v_ref, fq_ref, fk_ref, o_ref, lse_ref, acc_ref):
        i = pl.program_id(1)
        masks = _head_masks()
        q2 = q_ref[...]
        qm = [_sel(m, q2) for m in masks]
        fq2 = fq_ref[...]
        fq = [fq2[:, 0:1], fq2[:, HEAD_DIM:HEAD_DIM + 1]]
        rowpos = i * tq + lax.broadcasted_iota(jnp.int32, (tq, tk), 0)
        colin = lax.broadcasted_iota(jnp.int32, (tq, tk), 1)
        acc_ref[...] = jnp.zeros_like(acc_ref)

        def step(j, carry):
            off = pl.multiple_of(j * tk, tk)
            k2 = k_ref[pl.ds(off, tk), :]
            v2 = v_ref[pl.ds(off, tk), :]
            fk2 = fk_ref[0, j]
            mask = (j * tk + colin) <= rowpos
            out = None
            new = []
            alphas = []
            for h in heads:
                m_old, l_old = carry[2 * h], carry[2 * h + 1]
                s = _dot(qm[h], k2, 1, 1) * scale + fq[h] - fk2[h:h + 1, :]
                s = jnp.where(mask, s, NEG)
                m_new = jnp.maximum(m_old, jnp.max(s, axis=-1, keepdims=True))
                p = jnp.exp(s - m_new)
                alpha = jnp.exp(m_old - m_new)
                alphas.append(alpha)
                d = _dot(p.astype(BF16), _sel(masks[h], v2))
                out = d if out is None else out + d
                new += [m_new, alpha * l_old + jnp.sum(p, axis=-1, keepdims=True)]
            acc_ref[...] = acc_ref[...] * jnp.where(masks[0], alphas[0], alphas[1]) + out
            return tuple(new)

        zero = jnp.zeros((tq, 1), F32)
        neg = jnp.full((tq, 1), NEG, F32)
        m0, l0, m1, l1 = lax.fori_loop(0, i + 1, step, (neg, zero, neg, zero))
        o_ref[...] = (acc_ref[...] / jnp.where(masks[0], l0, l1)).astype(BF16)
        lse_ref[...] = jnp.where(masks[0], m0 + jnp.log(l0), m1 + jnp.log(l1))

    W = n_pairs * PAIR
    return pl.pallas_call(
        body, name="fox_fwd", grid=(n_pairs, S // tq), in_specs=_fox_specs(S, tq, n_pairs, base),
        out_specs=[pl.BlockSpec((tq, PAIR), lambda p, i: (i, p)), pl.BlockSpec((tq, PAIR), lambda p, i: (i, p))],
        out_shape=[jax.ShapeDtypeStruct((S, W), BF16), jax.ShapeDtypeStruct((S, W), F32)],
        scratch_shapes=[pltpu.VMEM((tq, PAIR), F32)],
        compiler_params=_params(("parallel", "arbitrary")),
    )(qkv, qkv, qkv, fcx, fcr)


def _fox_bwd(qkv, fcx, fcr, o, lse, do, n_pairs, base, tq):
    S = qkv.shape[0]
    tk = tq
    nq = S // tq
    scale = HEAD_DIM ** -0.5

    def body(q_ref, k_ref, v_ref, fq_ref, fk_ref, o_ref, lse_ref, do_ref,
             dq_ref, dk_ref, dv_ref, dfq_ref, dfk_ref, dq_acc, dk_acc, dv_acc):
        i = pl.program_id(1)
        masks = _head_masks()

        @pl.when(i == 0)
        def _():
            dk_acc[...] = jnp.zeros_like(dk_acc)
            dv_acc[...] = jnp.zeros_like(dv_acc)
            dfk_ref[...] = jnp.zeros_like(dfk_ref)

        q2 = q_ref[...]
        do2 = do_ref[...]
        qm = [_sel(m, q2) for m in masks]
        dom = [_sel(m, do2) for m in masks]
        fq2 = fq_ref[...]
        fq = [fq2[:, 0:1], fq2[:, HEAD_DIM:HEAD_DIM + 1]]
        l2 = lse_ref[...]
        lse_h = [l2[:, 0:1], l2[:, HEAD_DIM:HEAD_DIM + 1]]
        prod = do2.astype(F32) * o_ref[...].astype(F32)
        delta = [jnp.sum(jnp.where(m, prod, 0.0), axis=-1, keepdims=True) for m in masks]
        rowpos = i * tq + lax.broadcasted_iota(jnp.int32, (tq, tk), 0)
        colin = lax.broadcasted_iota(jnp.int32, (tq, tk), 1)
        dq_acc[...] = jnp.zeros_like(dq_acc)

        def step(j, carry):
            off = pl.multiple_of(j * tk, tk)
            k2 = k_ref[pl.ds(off, tk), :]
            v2 = v_ref[pl.ds(off, tk), :]
            fk2 = fk_ref[0, j]
            mask = (j * tk + colin) <= rowpos
            dq = None
            dk = None
            dv = None
            new = []
            dfk_rows = []
            for h in heads:
                s = _dot(qm[h], k2, 1, 1) * scale + fq[h] - fk2[h:h + 1, :]
                p = jnp.where(mask, jnp.exp(s - lse_h[h]), 0.0)
                dp = _dot(dom[h], v2, 1, 1)
                ds = p * (dp - delta[h])
                dsb = (ds * scale).astype(BF16)
                d1 = _dot(dsb, _sel(masks[h], k2))
                d2 = _dot(dsb, qm[h], 0, 0)
                d3 = _dot(p.astype(BF16), dom[h], 0, 0)
                dq = d1 if dq is None else dq + d1
                dk = d2 if dk is None else dk + d2
                dv = d3 if dv is None else dv + d3
                new.append(carry[h] + jnp.sum(ds, axis=-1, keepdims=True))
                dfk_rows.append(jnp.sum(ds, axis=0, keepdims=True))
            dq_acc[...] += dq
            dk_acc[pl.ds(off, tk), :] += dk
            dv_acc[pl.ds(off, tk), :] += dv
            dfk_ref[0, j, 0:1, :] += dfk_rows[0]
            dfk_ref[0, j, 1:2, :] += dfk_rows[1]
            return tuple(new)

        zero = jnp.zeros((tq, 1), F32)
        r0, r1 = lax.fori_loop(0, i + 1, step, (zero, zero))
        dq_ref[...] = dq_acc[...].astype(BF16)
        dfq_ref[...] = jnp.where(masks[0], r0, r1)

        @pl.when(i == nq - 1)
        def _():
            dk_ref[...] = dk_acc[...].astype(BF16)
            dv_ref[...] = dv_acc[...].astype(BF16)

    W = n_pairs * PAIR
    in_specs = _fox_specs(S, tq, n_pairs, base) + [
        pl.BlockSpec((tq, PAIR), lambda p, i: (i, p)),
        pl.BlockSpec((tq, PAIR), lambda p, i: (i, p)),
        pl.BlockSpec((tq, PAIR), lambda p, i: (i, p))]
    out_specs = [pl.BlockSpec((tq, PAIR), lambda p, i: (i, p)),
                 pl.BlockSpec((S, PAIR), lambda p, i: (0, p)),
                 pl.BlockSpec((S, PAIR), lambda p, i: (0, p)),
                 pl.BlockSpec((tq, PAIR), lambda p, i: (i, p)),
                 pl.BlockSpec((1, nq, 8, tk), lambda p, i: (p, 0, 0, 0))]
    return pl.pallas_call(
        body, name="fox_bwd", grid=(n_pairs, nq), in_specs=in_specs, out_specs=out_specs,
        out_shape=[jax.ShapeDtypeStruct((S, W), BF16)] * 3
        + [jax.ShapeDtypeStruct((S, W), F32), jax.ShapeDtypeStruct((n_pairs, nq, 8, tk), F32)],
        scratch_shapes=[pltpu.VMEM((tq, PAIR), F32), pltpu.VMEM((S, PAIR), F32), pltpu.VMEM((S, PAIR), F32)],
        compiler_params=_params(("parallel", "arbitrary")),
    )(qkv, qkv, qkv, fcx, fcr, o, lse, do)


RC = 32
VANISH = -104.0


def _chunks(n_rows, fn):
    for ci in range(n_rows // RC):
        fn(ci * RC)


def _wide(v, tk):
    return v if tk == 128 else jnp.tile(v, (1, tk // 128))


def _rep(col):
    return jnp.broadcast_to(col, (col.shape[0], 128))


def _per_head(blk, masks):
    sw = pltpu.roll(blk, HEAD_DIM, axis=1)
    return jnp.where(masks[0], blk, sw), jnp.where(masks[0], sw, blk)


def _fill_masked(dst_ref, src_ref, masks, mul=None, ones_lane=None):
    v = src_ref[...]
    if mul is not None:
        v = v * mul
    lane = lax.broadcasted_iota(jnp.int32, (1, PAIR), 1)
    for h in range(2):
        m = _sel(masks[h], v)
        if ones_lane is not None:
            m = jnp.where(lane == ones_lane[h], jnp.ones_like(m), m)
        dst_ref[h] = m


def _tri(tk, cmp):
    r = lax.broadcasted_iota(jnp.int32, (tk, tk), 0)
    c = lax.broadcasted_iota(jnp.int32, (tk, tk), 1)
    return cmp(r, c).astype(BF16)


def _diag_mask(r0, tk, strict):
    row = r0 + lax.broadcasted_iota(jnp.int32, (RC, tk), 0)
    col = lax.broadcasted_iota(jnp.int32, (RC, tk), 1)
    return (col < row) if strict else (col <= row)


def _peer_copies(src_ref, land_ref, send_sems, recv_sems, scatter, receive_side):
    x, y, c = lax.axis_index("x"), lax.axis_index("y"), lax.axis_index("c")
    me = 4 * x + 2 * y + c
    copies = []
    for k in range(1, N_DEV):
        px, py, pc = (1 - x if k & 4 else x), (1 - y if k & 2 else y), (1 - c if k & 1 else c)
        slot = 4 * px + 2 * py + pc
        copies.append(pltpu.make_async_remote_copy(
            src_ref=src_ref.at[slot] if scatter else src_ref,
            dst_ref=land_ref.at[slot] if receive_side else land_ref.at[me],
            send_sem=send_sems.at[k - 1], recv_sem=recv_sems.at[k - 1], device_id=(px, py, pc), device_id_type=MESH))
    return copies


def _call_carrying(body, exchange, name, grid, in_specs, out_specs, out_shape, scratch_shapes, args):
    if exchange is None:
        return pl.pallas_call(body, name=name, grid=grid, in_specs=in_specs, out_specs=out_specs, out_shape=out_shape,
                              scratch_shapes=scratch_shapes, compiler_params=_params(("parallel", "arbitrary")))(*args)
    src, scatter = exchange
    n_in, n_out = len(in_specs), len(out_specs)

    def carrying(*refs):
        src_ref, land_ref = refs[n_in], refs[n_in + 1 + n_out]
        send_sems, recv_sems = refs[-2], refs[-1]
        first = jnp.logical_and(pl.program_id(0) == 0, pl.program_id(1) == 0)
        last = jnp.logical_and(pl.program_id(0) == grid[0] - 1, pl.program_id(1) == grid[1] - 1)

        @pl.when(first)
        def _():
            for cp in _peer_copies(src_ref, land_ref, send_sems, recv_sems, scatter, False):
                cp.start()

        body(*refs[:n_in], *refs[n_in + 1:n_in + 1 + n_out], *refs[n_in + 2 + n_out:-2])

        @pl.when(last)
        def _():
            for cp in _peer_copies(src_ref, land_ref, send_sems, recv_sems, scatter, True):
                cp.wait_send()
                cp.wait_recv()

    any_space = pl.BlockSpec(memory_space=pl.ANY)
    land = jax.ShapeDtypeStruct((N_DEV,) + src.shape[-2:], src.dtype)
    return pl.pallas_call(
        carrying, name=name, grid=grid, in_specs=list(in_specs) + [any_space],
        out_specs=list(out_specs) + [any_space], out_shape=list(out_shape) + [land],
        scratch_shapes=list(scratch_shapes) + [pltpu.SemaphoreType.DMA((N_DEV - 1,)), pltpu.SemaphoreType.DMA((N_DEV - 1,))],
        compiler_params=_params(("arbitrary", "arbitrary")))(*args, src)


def _staggered(bodies):
    active, waiting = [], list(bodies)
    while waiting or active:
        if waiting:
            active.append(waiting.pop(0))
        for g in list(active):
            try:
                next(g)
            except StopIteration:
                active.remove(g)


def _streams(tile, j, diag, slot):
    return [tile(j, diag, slot, (0, 1))]


def _tiles(i, tile):
    def step(jj, carry):
        _staggered(_streams(tile, 2 * jj, False, 0) + _streams(tile, 2 * jj + 1, False, 1))
        return carry
    lax.fori_loop(0, i // 2, step, 0)

    @pl.when(i % 2 == 1)
    def _():
        _staggered(_streams(tile, i - 1, False, 0))
    _staggered(_streams(tile, i, True, 1))


def _tiles_reversed(i, tile, keep_going):
    _staggered(_streams(tile, i, True, 1))

    def cond(carry):
        jj, go = carry
        return jnp.logical_and(jj < i // 2, go)

    def step(carry):
        jj, _ = carry
        _staggered(_streams(tile, i - 1 - 2 * jj, False, 0) + _streams(tile, i - 2 - 2 * jj, False, 1))
        return jj + 1, keep_going()

    jj, go = lax.while_loop(cond, step, (jnp.int32(0), keep_going()))

    @pl.when(jnp.logical_and(jnp.logical_and(i % 2 == 1, jj == i // 2), go))
    def _():
        _staggered(_streams(tile, 0, False, 0))


def _sb_fwd2(qkv, n_pairs, base, tq):
    S = qkv.shape[0]
    tk = tq
    scale = HEAD_DIM ** -0.5

    def body(q_ref, k_ref, v_ref, o_ref, t_ref, z_ref, hi_ref, suf_ref, p_ref, r_ref, acc_ref, vm_ref):
        i = pl.program_id(1)
        masks = _head_masks()

        @pl.when(i == 0)
        def _():
            _fill_masked(vm_ref, v_ref, masks)

        q2 = q_ref[...] * scale
        qm = [_sel(m, q2) for m in masks]
        incl = _tri(tk, lambda r, c: r >= c)
        r_ref[...] = jnp.zeros_like(r_ref)
        acc_ref[...] = jnp.zeros_like(acc_ref)

        def tile(j, diag, slot, heads):
            off = pl.multiple_of(j * tk, tk)
            k2 = k_ref[pl.ds(off, tk), :]
            v2 = v_ref[pl.ds(off, tk), :]
            for h in heads:
                z_ref[2 * slot + h] = _dot(qm[h], k2, 1, 1)
            yield
            for h in heads:
                def split(r0, h=h):
                    rows = pl.ds(r0, RC)
                    lg = -_softplus(z_ref[2 * slot + h, rows, :])
                    if diag:
                        lg = jnp.where(_diag_mask(r0, tk, True), lg, 0.0)
                    hi_ref[2 * slot + h, rows, :] = lg.astype(BF16)
                _chunks(tq, split)
            yield
            for h in heads:
                suf_ref[2 * slot + h] = _dot(hi_ref[2 * slot + h], incl)
            yield
            for h in heads:
                def weights(r0, h=h):
                    rows = pl.ds(r0, RC)
                    a = jnp.exp(z_ref[2 * slot + h, rows, :] + suf_ref[2 * slot + h, rows, :] + _wide(r_ref[h, rows, :], tk))
                    if diag:
                        a = jnp.where(_diag_mask(r0, tk, True), a, 0.0)
                    p_ref[2 * slot + h, rows, :] = a.astype(BF16)
                _chunks(tq, weights)
            yield
            keys = pl.ds(off, tk)
            for h in heads:
                acc_ref[...] += _dot(p_ref[2 * slot + h], vm_ref[h, keys, :])
            for h in heads:
                r_ref[h] += _rep(suf_ref[2 * slot + h, :, 0:1])

        _tiles_reversed(i, tile, lambda: jnp.max(r_ref[...]) >= VANISH)
        o_ref[...] = acc_ref[...].astype(BF16)
        t_ref[...] = acc_ref[...]

    W = n_pairs * PAIR
    return pl.pallas_call(
        body, name="sb_fwd", grid=(n_pairs, S // tq), in_specs=_qkv_specs(S, tq, n_pairs, base),
        out_specs=[pl.BlockSpec((tq, PAIR), lambda p, i: (i, p)), pl.BlockSpec((tq, PAIR), lambda p, i: (i, p))],
        out_shape=[jax.ShapeDtypeStruct((S, W), BF16), jax.ShapeDtypeStruct((S, W), F32)],
        scratch_shapes=[pltpu.VMEM((4, tq, tk), F32), pltpu.VMEM((4, tq, tk), BF16),
                        pltpu.VMEM((4, tq, tk), F32), pltpu.VMEM((4, tq, tk), BF16), pltpu.VMEM((2, tq, 128), F32),
                        pltpu.VMEM((tq, PAIR), F32), pltpu.VMEM((2, S, PAIR), BF16)],
        compiler_params=_params(("parallel", "arbitrary")),
    )(qkv, qkv, qkv)


def _sb_bwd2(qkv, o32, do, n_pairs, base, tq):
    S = qkv.shape[0]
    tk = tq
    nq = S // tq
    scale = HEAD_DIM ** -0.5

    def body(q_ref, k_ref, v_ref, o_ref, do_ref, dq_ref, dk_ref, dv_ref,
             z_ref, g_ref, omb_ref, cum_ref, hi_ref, lo_ref, a_ref, dz_ref,
             r_ref, cg_ref, dl_ref, dq_acc, dk_acc, dv_acc, ks_ref):
        i = pl.program_id(1)
        masks = _head_masks()

        @pl.when(i == 0)
        def _():
            dk_acc[...] = jnp.zeros_like(dk_acc)
            dv_acc[...] = jnp.zeros_like(dv_acc)
            _fill_masked(ks_ref, k_ref, masks, mul=scale)

        q2 = q_ref[...] * scale
        do2 = do_ref[...]
        qm = [_sel(m, q2) for m in masks]
        dom = [_sel(m, do2) for m in masks]
        prod = do2.astype(F32) * o_ref[...]
        for h in range(2):
            dl_ref[h] = _rep(jnp.sum(jnp.where(masks[h], prod, 0.0), axis=-1, keepdims=True))
        suffix = _tri(tk, lambda r, c: r >= c)
        r_ref[...] = jnp.zeros_like(r_ref)
        cg_ref[...] = jnp.zeros_like(cg_ref)
        dq_acc[...] = jnp.zeros_like(dq_acc)

        def tile(j, diag, slot, heads):
            off = pl.multiple_of(j * tk, tk)
            k2 = k_ref[pl.ds(off, tk), :]
            v2 = v_ref[pl.ds(off, tk), :]
            for h in heads:
                z_ref[2 * slot + h] = _dot(qm[h], k2, 1, 1)
                g_ref[2 * slot + h] = _dot(dom[h], v2, 1, 1)
            yield
            for h in heads:
                def split(r0, h=h):
                    rows = pl.ds(r0, RC)
                    sp = _softplus(z_ref[2 * slot + h, rows, :])
                    omb_ref[2 * slot + h, rows, :] = jnp.exp(-sp)
                    lg = -sp
                    if diag:
                        lg = jnp.where(_diag_mask(r0, tk, True), lg, 0.0)
                    hi_ref[2 * slot + h, rows, :] = lg.astype(BF16)
                _chunks(tq, split)
            yield
            for h in heads:
                cum_ref[2 * slot + h] = _dot(hi_ref[2 * slot + h], suffix)
            yield
            for h in heads:
                def weights(r0, h=h):
                    rows = pl.ds(r0, RC)
                    a = jnp.exp(z_ref[2 * slot + h, rows, :] + cum_ref[2 * slot + h, rows, :] + _wide(r_ref[h, rows, :], tk))
                    if diag:
                        a = jnp.where(_diag_mask(r0, tk, True), a, 0.0)
                    ab = a.astype(BF16)
                    g = g_ref[2 * slot + h, rows, :] * ab.astype(F32)
                    g_ref[2 * slot + h, rows, :] = g
                    a_ref[2 * slot + h, rows, :] = ab
                    hi, lo = _split2(g)
                    hi_ref[2 * slot + h, rows, :] = hi
                    lo_ref[2 * slot + h, rows, :] = lo
                _chunks(tq, weights)
            for h in heads:
                r_ref[h] += _rep(cum_ref[2 * slot + h, :, 0:1])
            yield
            for h in heads:
                cum_ref[2 * slot + h] = _dot(hi_ref[2 * slot + h], suffix) + _dot(lo_ref[2 * slot + h], suffix)
            yield
            for h in heads:
                def dscore(r0, h=h):
                    rows = pl.ds(r0, RC)
                    g = g_ref[2 * slot + h, rows, :]
                    from_here = cum_ref[2 * slot + h, rows, :] + _wide(cg_ref[h, rows, :], tk)
                    before = _wide(dl_ref[h, rows, :], tk) - from_here
                    omb = omb_ref[2 * slot + h, rows, :]
                    dz = g * omb - (1.0 - omb) * before
                    if diag:
                        dz = jnp.where(_diag_mask(r0, tk, True), dz, 0.0)
                    dz_ref[2 * slot + h, rows, :] = dz.astype(BF16)
                _chunks(tq, dscore)
            for h in heads:
                cg_ref[h] += _rep(cum_ref[2 * slot + h, :, 0:1])
            yield
            keys = pl.ds(off, tk)
            for h in heads:
                dq_acc[...] += _dot(dz_ref[2 * slot + h], ks_ref[h, keys, :])
                dk_acc[keys, :] += _dot(dz_ref[2 * slot + h], qm[h], 0, 0)
                dv_acc[keys, :] += _dot(a_ref[2 * slot + h], dom[h], 0, 0)

        _tiles_reversed(i, tile, lambda: jnp.max(r_ref[...]) >= VANISH)
        dq_ref[...] = dq_acc[...].astype(BF16)

        @pl.when(i == nq - 1)
        def _():
            dk_ref[...] = dk_acc[...].astype(BF16)
            dv_ref[...] = dv_acc[...].astype(BF16)

    W = n_pairs * PAIR
    in_specs = _qkv_specs(S, tq, n_pairs, base) + [
        pl.BlockSpec((tq, PAIR), lambda p, i: (i, p)),
        pl.BlockSpec((tq, PAIR), lambda p, i: (i, p))]
    out_specs = [pl.BlockSpec((tq, PAIR), lambda p, i: (i, p)),
                 pl.BlockSpec((S, PAIR), lambda p, i: (0, p)),
                 pl.BlockSpec((S, PAIR), lambda p, i: (0, p))]
    big, stat = (4, tq, tk), (2, tq, 128)
    return pl.pallas_call(
        body, name="sb_bwd", grid=(n_pairs, nq), in_specs=in_specs, out_specs=out_specs,
        out_shape=[jax.ShapeDtypeStruct((S, W), BF16)] * 3,
        scratch_shapes=[pltpu.VMEM(big, F32)] * 4 + [pltpu.VMEM(big, BF16)] * 4 + [pltpu.VMEM(stat, F32)] * 3
        + [pltpu.VMEM((tq, PAIR), F32), pltpu.VMEM((S, PAIR), F32), pltpu.VMEM((S, PAIR), F32),
           pltpu.VMEM((2, S, PAIR), BF16)],
        compiler_params=_params(("parallel", "arbitrary")),
    )(qkv, qkv, qkv, o32, do)


def _fox_fwd2(qkv, fcx, fcr, n_pairs, base, tq, exchange=None):
    S = qkv.shape[0]
    tk = tq
    scale = HEAD_DIM ** -0.5
    spare = (HEAD_DIM, 0)

    def body(q_ref, k_ref, v_ref, fq_ref, fk_ref, o_ref, lse_ref, s_ref, p_ref, m_ref, al_ref, fqr_ref, acc_ref, vm_ref):
        i = pl.program_id(1)
        masks = _head_masks()

        @pl.when(i == 0)
        def _():
            _fill_masked(vm_ref, v_ref, masks, ones_lane=spare)

        q2 = q_ref[...] * scale
        qm = [_sel(m, q2) for m in masks]
        f0, f1 = _per_head(fq_ref[...], masks)
        fqr_ref[0] = f0
        fqr_ref[1] = f1
        m_ref[...] = jnp.full(m_ref.shape, NEG, F32)
        acc_ref[...] = jnp.zeros_like(acc_ref)

        def tile(j, diag, slot, heads):
            off = pl.multiple_of(j * tk, tk)
            k2 = k_ref[pl.ds(off, tk), :]
            v2 = v_ref[pl.ds(off, tk), :]
            fk2 = fk_ref[0, j]
            for h in heads:
                s_ref[2 * slot + h] = _dot(qm[h], k2, 1, 1)
            yield
            for h in heads:
                fk_row = fk2[h:h + 1, :]

                def probs(r0, h=h, fk_row=fk_row):
                    rows = pl.ds(r0, RC)
                    sv = s_ref[2 * slot + h, rows, :] - fk_row
                    if diag:
                        sv = jnp.where(_diag_mask(r0, tk, False), sv, NEG)
                    fq = fqr_ref[h, rows, :]
                    m_prev = m_ref[h, rows, :]
                    m_new = jnp.maximum(m_prev, jnp.max(sv, axis=-1, keepdims=True) + fq)
                    p_ref[2 * slot + h, rows, :] = jnp.exp(sv + _wide(fq - m_new, tk)).astype(BF16)
                    al_ref[2 * slot + h, rows, :] = jnp.exp(m_prev - m_new)
                    m_ref[h, rows, :] = m_new
                _chunks(tq, probs)
            yield
            for h in heads:
                acc_ref[h] = acc_ref[h] * al_ref[2 * slot + h] + _dot(p_ref[2 * slot + h], vm_ref[h, pl.ds(off, tk), :])

        _tiles(i, tile)
        a0, a1 = acc_ref[0], acc_ref[1]
        l0 = _rep(a0[:, spare[0]:spare[0] + 1])
        l1 = _rep(a1[:, spare[1]:spare[1] + 1])
        o_ref[...] = jnp.where(masks[0], a0 / l0, a1 / l1).astype(BF16)
        lse_ref[...] = jnp.where(masks[0], m_ref[0] + jnp.log(l0), m_ref[1] + jnp.log(l1))

    W = n_pairs * PAIR
    return _call_carrying(
        body, exchange, "fox_fwd", (n_pairs, S // tq), _fox_specs(S, tq, n_pairs, base),
        [pl.BlockSpec((tq, PAIR), lambda p, i: (i, p)), pl.BlockSpec((tq, PAIR), lambda p, i: (i, p))],
        [jax.ShapeDtypeStruct((S, W), BF16), jax.ShapeDtypeStruct((S, W), F32)],
        [pltpu.VMEM((4, tq, tk), F32), pltpu.VMEM((4, tq, tk), BF16), pltpu.VMEM((2, tq, 128), F32),
         pltpu.VMEM((4, tq, 128), F32), pltpu.VMEM((2, tq, 128), F32), pltpu.VMEM((2, tq, 128), F32),
         pltpu.VMEM((2, S, PAIR), BF16)],
        (qkv, qkv, qkv, fcx, fcr))


def _fox_bwd2(qkv, fcx, fcr, o, lse, do, n_pairs, base, tq, exchange=None):
    S = qkv.shape[0]
    tk = tq
    nq = S // tq
    scale = HEAD_DIM ** -0.5

    def body(q_ref, k_ref, v_ref, fq_ref, fk_ref, o_ref, lse_ref, do_ref,
             dq_ref, dk_ref, dv_ref, dfq_ref, dfk_ref,
             s_ref, dp_ref, p_ref, ds_ref, row_ref, dl_ref, dfq_acc, col_ref, dq_acc, dk_acc, dv_acc, ks_ref):
        i = pl.program_id(1)
        masks = _head_masks()

        @pl.when(i == 0)
        def _():
            dk_acc[...] = jnp.zeros_like(dk_acc)
            dv_acc[...] = jnp.zeros_like(dv_acc)
            dfk_ref[...] = jnp.zeros_like(dfk_ref)
            _fill_masked(ks_ref, k_ref, masks, mul=scale)

        q2 = q_ref[...] * scale
        do2 = do_ref[...]
        qm = [_sel(m, q2) for m in masks]
        dom = [_sel(m, do2) for m in masks]
        f0, f1 = _per_head(fq_ref[...], masks)
        l0, l1 = _per_head(lse_ref[...], masks)
        row_ref[0] = f0 - l0
        row_ref[1] = f1 - l1
        prod = do2.astype(F32) * o_ref[...].astype(F32)
        for h in range(2):
            dl_ref[h] = _rep(jnp.sum(jnp.where(masks[h], prod, 0.0), axis=-1, keepdims=True))
        dfq_acc[...] = jnp.zeros_like(dfq_acc)
        dq_acc[...] = jnp.zeros_like(dq_acc)

        def tile(j, diag, slot, heads):
            off = pl.multiple_of(j * tk, tk)
            k2 = k_ref[pl.ds(off, tk), :]
            v2 = v_ref[pl.ds(off, tk), :]
            fk2 = fk_ref[0, j]
            for h in heads:
                s_ref[2 * slot + h] = _dot(qm[h], k2, 1, 1)
                dp_ref[2 * slot + h] = _dot(dom[h], v2, 1, 1)
            yield
            for h in heads:
                col_ref[2 * slot + h] = jnp.zeros((8, tk), F32)
                fk_row = fk2[h:h + 1, :]

                def dscore(r0, h=h, fk_row=fk_row):
                    rows = pl.ds(r0, RC)
                    p = jnp.exp(s_ref[2 * slot + h, rows, :] - fk_row + _wide(row_ref[h, rows, :], tk))
                    if diag:
                        p = jnp.where(_diag_mask(r0, tk, False), p, 0.0)
                    ds = p * (dp_ref[2 * slot + h, rows, :] - _wide(dl_ref[h, rows, :], tk))
                    p_ref[2 * slot + h, rows, :] = p.astype(BF16)
                    ds_ref[2 * slot + h, rows, :] = ds.astype(BF16)
                    dfq_acc[h, rows, :] += _rep(jnp.sum(ds, axis=-1, keepdims=True))
                    col_ref[2 * slot + h] += jnp.sum(ds.reshape(RC // 8, 8, tk), axis=0)
                _chunks(tq, dscore)
            yield
            keys = pl.ds(off, tk)
            for h in heads:
                dq_acc[...] += _dot(ds_ref[2 * slot + h], ks_ref[h, keys, :])
                dk_acc[keys, :] += _dot(ds_ref[2 * slot + h], qm[h], 0, 0)
                dv_acc[keys, :] += _dot(p_ref[2 * slot + h], dom[h], 0, 0)
            for h in heads:
                dfk_ref[0, j, h:h + 1, :] += jnp.sum(col_ref[2 * slot + h], axis=0, keepdims=True)

        _tiles(i, tile)
        dq_ref[...] = dq_acc[...].astype(BF16)
        dfq_ref[...] = jnp.where(masks[0], dfq_acc[0], dfq_acc[1])

        @pl.when(i == nq - 1)
        def _():
            dk_ref[...] = dk_acc[...].astype(BF16)
            dv_ref[...] = dv_acc[...].astype(BF16)

    W = n_pairs * PAIR
    in_specs = _fox_specs(S, tq, n_pairs, base) + [
        pl.BlockSpec((tq, PAIR), lambda p, i: (i, p)),
        pl.BlockSpec((tq, PAIR), lambda p, i: (i, p)),
        pl.BlockSpec((tq, PAIR), lambda p, i: (i, p))]
    out_specs = [pl.BlockSpec((tq, PAIR), lambda p, i: (i, p)),
                 pl.BlockSpec((S, PAIR), lambda p, i: (0, p)),
                 pl.BlockSpec((S, PAIR), lambda p, i: (0, p)),
                 pl.BlockSpec((tq, PAIR), lambda p, i: (i, p)),
                 pl.BlockSpec((1, nq, 8, tk), lambda p, i: (p, 0, 0, 0))]
    return _call_carrying(
        body, exchange, "fox_bwd", (n_pairs, nq), in_specs, out_specs,
        [jax.ShapeDtypeStruct((S, W), BF16)] * 3
        + [jax.ShapeDtypeStruct((S, W), F32), jax.ShapeDtypeStruct((n_pairs, nq, 8, tk), F32)],
        [pltpu.VMEM((4, tq, tk), F32)] * 2 + [pltpu.VMEM((4, tq, tk), BF16)] * 2
        + [pltpu.VMEM((2, tq, 128), F32)] * 3 + [pltpu.VMEM((4, 8, tk), F32)]
        + [pltpu.VMEM((tq, PAIR), F32), pltpu.VMEM((S, PAIR), F32), pltpu.VMEM((S, PAIR), F32),
           pltpu.VMEM((2, S, PAIR), BF16)],
        (qkv, qkv, qkv, fcx, fcr, o, lse, do))


def _swiglu_fwd(u2, wg, wu):
    S, D = u2.shape
    FF = wg.shape[1]
    tm, tn = _pick(S, (512, 256, 128)), _divisors(FF, 1536)[0]

    def body(u_ref, g_ref, w_ref, a_ref, b_ref, h_ref):
        u = u_ref[...]
        a = _dot(u, g_ref[...])
        b = _dot(u, w_ref[...])
        a_ref[...] = a.astype(BF16)
        b_ref[...] = b.astype(BF16)
        h_ref[...] = (a / (1.0 + jnp.exp(-a)) * b).astype(BF16)

    spec_o = pl.BlockSpec((tm, tn), lambda i, j: (i, j))
    return pl.pallas_call(
        body, name="swiglu_fwd", grid=(S // tm, FF // tn),
        in_specs=[pl.BlockSpec((tm, D), lambda i, j: (i, 0)),
                  pl.BlockSpec((D, tn), lambda i, j: (0, j)),
                  pl.BlockSpec((D, tn), lambda i, j: (0, j))],
        out_specs=[spec_o] * 3, out_shape=[jax.ShapeDtypeStruct((S, FF), BF16)] * 3,
        compiler_params=_params(("parallel", "parallel")),
    )(u2, wg, wu)


def _swiglu_bwd(dh, wd, a, b):
    S, D = dh.shape
    FF = wd.shape[0]
    tm, tn = _pick(S, (512, 256, 128)), _divisors(FF, 1536)[0]

    def body(dh_ref, w_ref, a_ref, b_ref, da_ref, db_ref):
        dhin = _dot(dh_ref[...], w_ref[...], 1, 1)
        av = a_ref[...].astype(F32)
        bv = b_ref[...].astype(F32)
        sig = 1.0 / (1.0 + jnp.exp(-av))
        da_ref[...] = (dhin * bv * (sig * (1.0 + av * (1.0 - sig)))).astype(BF16)
        db_ref[...] = (dhin * (av * sig)).astype(BF16)

    spec_o = pl.BlockSpec((tm, tn), lambda i, j: (i, j))
    return pl.pallas_call(
        body, name="swiglu_bwd", grid=(S // tm, FF // tn),
        in_specs=[pl.BlockSpec((tm, D), lambda i, j: (i, 0)),
                  pl.BlockSpec((tn, D), lambda i, j: (j, 0)), spec_o, spec_o],
        out_specs=[spec_o] * 2, out_shape=[jax.ShapeDtypeStruct((S, FF), BF16)] * 2,
        compiler_params=_params(("parallel", "parallel")),
    )(dh, wd, a, b)


def _gate_bwd(dmix, wo, g_sb, g_fx, y_sb, y_fx):
    S, D = dmix.shape
    tm, tn = _pick(S, (512, 256, 128)), _divisors(D, 1024)[0]

    def body(dm_ref, w_ref, gs_ref, gf_ref, ys_ref, yf_ref, dys_ref, dyf_ref, dls_ref, dlf_ref, bs_ref, bf_ref):
        @pl.when(pl.program_id(1) == 0)
        def _():
            bs_ref[...] = jnp.zeros_like(bs_ref)
            bf_ref[...] = jnp.zeros_like(bf_ref)

        dmi = _dot(dm_ref[...], w_ref[...], 1, 1)
        gs, gf = gs_ref[...], gf_ref[...]
        dys_ref[...] = (dmi * gs).astype(BF16)
        dyf_ref[...] = (dmi * gf).astype(BF16)
        dls = dmi * ys_ref[...] * gs * (1.0 - gs)
        dlf = dmi * yf_ref[...] * gf * (1.0 - gf)
        dls_ref[...] = dls.astype(BF16)
        dlf_ref[...] = dlf.astype(BF16)
        bs_ref[0:1, :] += _colsum(dls)
        bf_ref[0:1, :] += _colsum(dlf)

    t = pl.BlockSpec((tm, tn), lambda j, i: (i, j))
    accs = pl.BlockSpec((8, tn), lambda j, i: (0, j))
    return pl.pallas_call(
        body, name="gate_bwd", grid=(D // tn, S // tm),
        in_specs=[pl.BlockSpec((tm, D), lambda j, i: (i, 0)),
                  pl.BlockSpec((tn, D), lambda j, i: (j, 0)), t, t, t, t],
        out_specs=[t, t, t, t, accs, accs],
        out_shape=[jax.ShapeDtypeStruct((S, D), BF16)] * 4 + [jax.ShapeDtypeStruct((8, D), F32)] * 2,
        compiler_params=_params(("parallel", "arbitrary")),
    )(dmix, wo, g_sb, g_fx, y_sb, y_fx)


def _local_step(x, target, ada8, lnp8, bg_sb, bg_fx, bf_pad, wqkv, wf, wgs, wgf, gather, later_weights, pack_early):
    S, D = x.shape
    W = wqkv.shape[1] // 6
    n_pairs = W // PAIR
    n_heads = W // HEAD_DIM
    ts = _pick(S, (256, 128))
    tq = _pick(S, (256, 128))

    u1 = _ln_mod(x, ada8, ts)
    qkv = _mm([(u1, wqkv)], 'nn', BF16, "in_qkv")
    f = _mm([(u1, wf)], 'nn', F32, "in_f")
    g_sb = _mm([(u1, wgs)], 'nn', F32, "in_gsb", bias=bg_sb, act='sigmoid')
    g_fx = _mm([(u1, wgf)], 'nn', F32, "in_gfx", bias=bg_fx, act='sigmoid')
    fc = _fgate_fwd(f, bf_pad, _pick(S, (512, 256, 128)))
    fch = fc[:, :n_heads]
    fcx = jnp.repeat(fch, HEAD_DIM, axis=1)
    nq = S // tq
    fcr = jnp.pad(fch.T.reshape(n_pairs, 2, nq, tq).transpose(0, 2, 1, 3),
                  ((0, 0), (0, 0), (0, 6), (0, 0)))
    o_sb, o_sb32 = _sb_fwd2(qkv, n_pairs, 0, tq)
    o_fx, lse, *zone = _fox_fwd2(qkv, fcx, fcr, n_pairs, 3 * n_pairs, tq, gather)
    wsb, wfx, wo, wfg, wfu, wfd = later_weights(zone[0] if zone else None)
    y_sb =_mm([(o_sb, wsb)], 'nn', F32, "out_sb")
    y_fx = _mm([(o_fx, wfx)], 'nn', F32, "out_fx")
    mix_in = _gate_mix(g_sb, g_fx, y_sb, y_fx, ts)
    mix = _mm([(mix_in, wo)], 'nn', F32, "out_o")
    x1, u2 = _post_attn(x, mix, ada8, lnp8, ts)
    a, b, hin = _swiglu_fwd(u2, wfg, wfu)
    h = _mm([(hin, wfd)], 'nn', F32, "ffn_down")
    dr2, dh, st_loss = _loss_head(x1, h, target, ada8, lnp8, ts)

    da, db = _swiglu_bwd(dh, wfd, a, b)
    g_wfd = _mm([(hin, dh)], 'tn', F32, "g_ffn_down")
    du2 = _mm([(da, wfg), (db, wfu)], 'nt', F32, "d_u2")
    g_wfg = _mm([(u2, da)], 'tn', F32, "g_ffn_gate")
    g_wfu = _mm([(u2, db)], 'tn', F32, "g_ffn_up")
    dr1, dmix, st_mid = _mid_bwd(du2, x1, dr2, mix, x, ada8, lnp8, ts)
    dys, dyf, dls, dlf, gb_sb, gb_fx = _gate_bwd(dmix, wo, g_sb, g_fx, y_sb, y_fx)
    g_wo = _mm([(mix_in, dmix)], 'tn', F32, "g_w_o")
    do_sb = _mm([(dys, wsb)], 'nt', BF16, "d_o_sb")
    do_fx = _mm([(dyf, wfx)], 'nt', BF16, "d_o_fx")
    g_wsb = _mm([(o_sb, dys)], 'tn', F32, "g_sb_out")
    g_wfx = _mm([(o_fx, dyf)], 'tn', F32, "g_fox_out")
    scatter = pack_early(dict(sb=g_wsb, fx=g_wfx, o=g_wo, fg=g_wfg, fu=g_wfu, fd=g_wfd))
    dq_s, dk_s, dv_s = _sb_bwd2(qkv, o_sb32, do_sb, n_pairs, 0, tq)
    dq_f, dk_f, dv_f, dfq, dfk, *zone = _fox_bwd2(qkv, fcx, fcr, o_fx, lse, do_fx, n_pairs, 3 * n_pairs, tq, scatter)
    dfc = dfq[:, ::HEAD_DIM] - dfk[:, :, :2, :].transpose(0, 2, 1, 3).reshape(n_heads, S).T
    dfc = jnp.pad(dfc, ((0, 0), (0, 128 - n_heads)))
    df, gb_f = _fgate_bwd(dfc, f, bf_pad, _pick(S, (512, 256, 128)))
    grads = [dq_s, dk_s, dv_s, dq_f, dk_f, dv_f]
    du1 = _mm([(g, wqkv, W, 0, n) for n, g in enumerate(grads)] + [(df, wf), (dls, wgs), (dlf, wgf)],
              'nt', F32, "d_u1")
    g_wqkv = [_mm([(u1, g)], 'tn', F32, "g_in_%d" % n) for n, g in enumerate(grads)]
    g_wf = _mm([(u1, df)], 'tn', F32, "g_in_f")
    g_wgs = _mm([(u1, dls)], 'tn', F32, "g_in_gsb")
    g_wgf = _mm([(u1, dlf)], 'tn', F32, "g_in_gfx")
    gx, st_first = _first_bwd(du1, x, dr1, ada8, ts)

    wgrads = dict(qkv=g_wqkv, f=g_wf, gs=g_wgs, gf=g_wgf)
    stats = dict(loss=st_loss, mid=st_mid, first=st_first, gb_sb=gb_sb, gb_fx=gb_fx, gb_f=gb_f)
    return gx, wgrads, stats, (scatter, zone[0] if zone else None)


def _position():
    x, y, c = lax.axis_index("x"), lax.axis_index("y"), lax.axis_index("c")
    return x, y, c, 4 * x + 2 * y + c


def _flip(x, y, c, k):
    px = 1 - x if k & 4 else x
    py = 1 - y if k & 2 else y
    pc = 1 - c if k & 1 else c
    return (px, py, pc), 4 * px + 2 * py + pc


def _all_gather_small(v, name):
    r, n = v.shape

    def body(x_ref, out_ref, send_sems, recv_sems, local_sem):
        x, y, c, me = _position()
        mine = pltpu.make_async_copy(x_ref, out_ref.at[me], local_sem)
        mine.start()
        sends = []
        for k in range(1, N_DEV):
            peer, _ = _flip(x, y, c, k)
            cp = pltpu.make_async_remote_copy(
                src_ref=x_ref, dst_ref=out_ref.at[me], send_sem=send_sems.at[k - 1], recv_sem=recv_sems.at[k - 1],
                device_id=peer, device_id_type=MESH)
            cp.start()
            sends.append(cp)
        for k in range(1, N_DEV):
            peer, slot = _flip(x, y, c, k)
            pltpu.make_async_remote_copy(
                src_ref=x_ref, dst_ref=out_ref.at[slot], send_sem=send_sems.at[k - 1], recv_sem=recv_sems.at[k - 1],
                device_id=peer, device_id_type=MESH).wait_recv()
        for cp in sends:
            cp.wait_send()
        mine.wait()

    return pl.pallas_call(
        body, name=name, out_shape=jax.ShapeDtypeStruct((N_DEV, r, n), v.dtype),
        in_specs=[pl.BlockSpec(memory_space=pltpu.VMEM)], out_specs=pl.BlockSpec(memory_space=pltpu.VMEM),
        scratch_shapes=[pltpu.SemaphoreType.DMA((N_DEV - 1,)), pltpu.SemaphoreType.DMA((N_DEV - 1,)),
                        pltpu.SemaphoreType.DMA],
    )(v)


def _all_gather_weights(packed):
    R, C = packed.shape

    def body(x_ref, out_ref, send_sems, recv_sems, local_sem):
        x, y, c, me = _position()
        sibling, sib_slot = _flip(x, y, c, 1)
        mine = pltpu.make_async_copy(x_ref, out_ref.at[me], local_sem)
        mine.start()

        def copy(k, slot, to, src=None):
            return pltpu.make_async_remote_copy(
                src_ref=out_ref.at[slot] if src is None else src, dst_ref=out_ref.at[slot],
                send_sem=send_sems.at[k], recv_sem=recv_sems.at[k], device_id=to, device_id_type=MESH)

        first = [copy(0, me, sibling, src=x_ref)]
        chips = (4, 2, 6)
        for n, k in enumerate(chips):
            peer, _ = _flip(x, y, c, k)
            first.append(copy(1 + n, me, peer, src=x_ref))
        for cp in first:
            cp.start()
        passed = []
        for n, k in enumerate(chips):
            peer, slot = _flip(x, y, c, k)
            copy(1 + n, slot, peer).wait_recv()
            cp = copy(4 + n, slot, sibling)
            cp.start()
            passed.append(cp)
        copy(0, sib_slot, sibling).wait_recv()
        for n, k in enumerate(chips):
            _, slot = _flip(x, y, c, k | 1)
            copy(4 + n, slot, sibling).wait_recv()
        for cp in first + passed:
            cp.wait_send()
        mine.wait()

    return pl.pallas_call(
        body, name="all_gather_weights", out_shape=jax.ShapeDtypeStruct((N_DEV, R, C), packed.dtype),
        in_specs=[pl.BlockSpec(memory_space=pl.ANY)], out_specs=pl.BlockSpec(memory_space=pl.ANY),
        scratch_shapes=[pltpu.SemaphoreType.DMA((7,)), pltpu.SemaphoreType.DMA((7,)), pltpu.SemaphoreType.DMA],
    )(packed)


def _exchange_grads(gpack):
    _, R, C = gpack.shape

    def body(g_ref, out_ref, send_sems, recv_sems, local_sem):
        x, y, c, me = _position()
        mine = pltpu.make_async_copy(g_ref.at[me], out_ref.at[me], local_sem)
        mine.start()
        sends = []
        for k in range(1, N_DEV):
            peer, slot = _flip(x, y, c, k)
            cp = pltpu.make_async_remote_copy(
                src_ref=g_ref.at[slot], dst_ref=out_ref.at[me], send_sem=send_sems.at[k - 1],
                recv_sem=recv_sems.at[k - 1], device_id=peer, device_id_type=MESH)
            cp.start()
            sends.append(cp)
        for k in range(1, N_DEV):
            peer, slot = _flip(x, y, c, k)
            pltpu.make_async_remote_copy(
                src_ref=g_ref.at[slot], dst_ref=out_ref.at[slot], send_sem=send_sems.at[k - 1],
                recv_sem=recv_sems.at[k - 1], device_id=peer, device_id_type=MESH).wait_recv()
        for cp in sends:
            cp.wait_send()
        mine.wait()

    return pl.pallas_call(
        body, name="exchange_grads", out_shape=jax.ShapeDtypeStruct((N_DEV, R, C), gpack.dtype),
        in_specs=[pl.BlockSpec(memory_space=pl.ANY)], out_specs=pl.BlockSpec(memory_space=pl.ANY),
        scratch_shapes=[pltpu.SemaphoreType.DMA((N_DEV - 1,)), pltpu.SemaphoreType.DMA((N_DEV - 1,)),
                        pltpu.SemaphoreType.DMA],
    )(gpack)


def _own_slot(land, own):
    me = 4 * lax.axis_index("x") + 2 * lax.axis_index("y") + lax.axis_index("c")
    return lax.dynamic_update_slice(land, own[None], (me, 0, 0))


def _sum_slots(recv, name, tr):
    n, R, C = recv.shape

    def body(r_ref, o_ref):
        acc = r_ref[0].astype(F32)
        for s in range(1, n):
            acc = acc + r_ref[s].astype(F32)
        o_ref[...] = acc

    return pl.pallas_call(
        body, name=name, grid=(R // tr,), in_specs=[pl.BlockSpec((n, tr, C), lambda i: (0, i, 0))],
        out_specs=pl.BlockSpec((tr, C), lambda i: (i, 0)), out_shape=jax.ShapeDtypeStruct((R, C), F32),
        compiler_params=_params(("parallel",)),
    )(recv)


def _sum_stats(st_all, loss_row):
    n, r, D = st_all.shape

    def body(s_ref, o_ref, l_ref):
        acc = s_ref[0]
        for d in range(1, n):
            acc = acc + s_ref[d]
        o_ref[...] = acc
        l_ref[...] = jnp.zeros((8, 128), F32) + jnp.sum(acc[loss_row:loss_row + 1, :], axis=-1, keepdims=True)

    return pl.pallas_call(
        body, name="sum_stats", out_shape=[jax.ShapeDtypeStruct((r, D), F32), jax.ShapeDtypeStruct((8, 128), F32)],
    )(st_all)


def _adamw(w, g, m, v, name):
    R, C = w.shape
    tr = _pick(R, (256, 176, 128, 64, 32, 16, 8))
    c1 = 1.0 / (1.0 - ADAM_B1 ** ADAM_STEP)
    c2 = 1.0 / (1.0 - ADAM_B2 ** ADAM_STEP)

    def body(w_ref, g_ref, m_ref, v_ref, d_ref, nm_ref, nv_ref):
        gv = g_ref[...]
        nm = ADAM_B1 * m_ref[...] + (1.0 - ADAM_B1) * gv
        nv = ADAM_B2 * v_ref[...] + (1.0 - ADAM_B2) * (gv * gv)
        nm_ref[...] = nm
        nv_ref[...] = nv
        d_ref[...] = -ADAM_LR * ((nm * c1) / (jnp.sqrt(nv * c2) + ADAM_EPS) + ADAM_WD * w_ref[...])

    spec = pl.BlockSpec((tr, C), lambda i: (i, 0))
    return pl.pallas_call(
        body, name=name, grid=(R // tr,), in_specs=[spec] * 4, out_specs=[spec] * 3,
        out_shape=[jax.ShapeDtypeStruct((R, C), F32)] * 3, compiler_params=_params(("parallel",)),
    )(w, g, m, v)


def _round16(n):
    return -(-n // 16) * 16


def _pack_layout(D, in_cols, ff, W):
    parts = [("in", D * (in_cols // N_DEV) // D), ("fg", ff // N_DEV), ("fu", ff // N_DEV),
             ("sb", W * (D // N_DEV) // D), ("fx", W * (D // N_DEV) // D), ("o", D // N_DEV), ("fd", ff // N_DEV)]
    layout, off = {}, 0
    for nm, rows in parts:
        layout[nm] = (off, rows)
        off += _round16(rows)
    return layout, off


def _rows_of(a, D, rows):
    a = a.reshape(rows, D)
    return jnp.pad(a, ((0, _round16(rows) - rows), (0, 0)))


def _cols_to_dest(g, D):
    K, N = g.shape
    n = N // N_DEV
    return g.reshape(K, N_DEV, n).transpose(1, 0, 2).reshape(N_DEV, K * n // D, D)


def _cols_from_src(blocks, K, n):
    return blocks.reshape(N_DEV, K, n).transpose(1, 0, 2).reshape(K, N_DEV * n)


def _pad_rows16(a):
    rows = a.shape[1]
    return jnp.pad(a, ((0, 0), (0, _round16(rows) - rows), (0, 0)))


def kernel(x, c, w_ada, b_ada, w_in, b_gate, b_forget, w_sb_out, w_fox_out, w_o, ln1_g, ln1_b, w_ffn_gate, w_ffn_up, w_ffn_down, ln2_g, ln2_b, loss_target, m_w_ada, m_b_ada, m_w_in, m_b_gate, m_b_forget, m_w_sb_out, m_w_fox_out, m_w_o, m_ln1_g, m_ln1_b, m_w_ffn_gate, m_w_ffn_up, m_w_ffn_down, m_ln2_g, m_ln2_b, v_w_ada, v_b_ada, v_w_in, v_b_gate, v_b_forget, v_w_sb_out, v_w_fox_out, v_w_o, v_ln1_g, v_ln1_b, v_w_ffn_gate, v_w_ffn_up, v_w_ffn_down, v_ln2_g, v_ln2_b):
    S, D = x.shape[1], x.shape[2]
    W = w_sb_out.shape[1]
    n_heads = b_forget.shape[1]
    ff = w_ffn_down.shape[1] * N_DEV
    in_loc = w_in.shape[2]
    in_cols = in_loc * N_DEV
    ada_loc = w_ada.shape[2]
    n_cond = ada_loc * N_DEV // D
    assert w_ada.shape[0] == 1 and n_cond == 6 and in_cols == 6 * W + n_heads + 2 * D and n_heads <= 128
    me = 4 * lax.axis_index("x") + 2 * lax.axis_index("y") + lax.axis_index("c")

    c_all = _all_gather_small(c, "gather_c").reshape(N_DEV, D)
    c16 = jnp.pad(c_all, ((0, 16 - N_DEV), (0, 0)))
    b_cols = lax.dynamic_slice(b_ada, (0, me * ada_loc), (1, ada_loc))
    ada_cols = _mm([(c16, w_ada[0])], 'nn', F32, "ada_fwd", bias=b_cols, silu_a=True)[:N_DEV]
    ada_all = _all_gather_small(ada_cols, "gather_ada")
    ada_me = lax.dynamic_index_in_dim(ada_all, me, axis=1, keepdims=False)
    ada8 = jnp.pad(ada_me.reshape(n_cond, D), ((0, 8 - n_cond), (0, 0)))
    lnp8 = jnp.concatenate([ln1_g, ln1_b, ln2_g, ln2_b, jnp.zeros((4, D), F32)], axis=0)

    layout, R = _pack_layout(D, in_cols, ff, W)
    shards = dict(**{"in": w_in[0]}, fg=w_ffn_gate[0], fu=w_ffn_up[0], sb=w_sb_out[0], fx=w_fox_out[0], o=w_o[0],
                  fd=w_ffn_down[0])
    later = [nm for nm in layout if nm != "in"]
    r_in = _round16(layout["in"][1])
    first = _rows_of(shards["in"].astype(BF16), D, layout["in"][1])
    rest = jnp.concatenate([_rows_of(shards[nm].astype(BF16), D, layout[nm][1]) for nm in later], axis=0)
    gathered_in = _all_gather_weights(first)

    w_in_full = _cols_from_src(gathered_in[:, :layout["in"][1], :], D, in_loc)
    wqkv = w_in_full[:, :6 * W]
    wf = jnp.pad(w_in_full[:, 6 * W:6 * W + n_heads], ((0, 0), (0, 128 - n_heads)))
    wgs = w_in_full[:, 6 * W + n_heads:6 * W + n_heads + D]
    wgf = w_in_full[:, 6 * W + n_heads + D:]
    bf_pad = jnp.pad(b_forget, ((0, 0), (0, 128 - n_heads)))

    def later_weights(zone):
        gathered = _own_slot(zone, rest)

        def part(nm):
            off, rows = layout[nm]
            return gathered[:, off - r_in:off - r_in + rows, :]

        return (_cols_from_src(part("sb"), W, D // N_DEV), _cols_from_src(part("fx"), W, D // N_DEV),
                part("o").reshape(D, D), _cols_from_src(part("fg"), D, ff // N_DEV),
                _cols_from_src(part("fu"), D, ff // N_DEV), part("fd").reshape(ff, D))

    def pack_early(g):
        dest = {"fg": _cols_to_dest(g["fg"], D), "fu": _cols_to_dest(g["fu"], D), "sb": _cols_to_dest(g["sb"], D),
                "fx": _cols_to_dest(g["fx"], D), "o": g["o"].reshape(N_DEV, D // N_DEV, D),
                "fd": g["fd"].reshape(N_DEV, ff // N_DEV, D)}
        return jnp.concatenate([_pad_rows16(dest[nm].astype(BF16)) for nm in later], axis=1), True

    gx, wg, st, ((pack, _), land) = _local_step(
        x[0], loss_target[0], ada8, lnp8, b_gate[:, :D], b_gate[:, D:], bf_pad, wqkv, wf, wgs, wgf,
        (rest, False), later_weights, pack_early)

    own = lax.dynamic_index_in_dim(pack, me, axis=0, keepdims=False)
    gsum_rest = _sum_slots(_own_slot(land, own), "sum_grads_rest", _pick(R - r_in, (512, 656, 256, 128, 64, 16)))
    g_in = jnp.concatenate(wg["qkv"] + [wg["f"][:, :n_heads], wg["gs"], wg["gf"]], axis=1)
    recv_in = _exchange_grads(_pad_rows16(_cols_to_dest(g_in, D).astype(BF16)))
    gsum_in = _sum_slots(recv_in, "sum_grads_in", _pick(r_in, (512, 656, 256, 128, 64, 16)))

    def gshard(nm, shape):
        off, rows = layout[nm]
        if nm == "in":
            return gsum_in[:rows].reshape(shape)
        return gsum_rest[off - r_in:off - r_in + rows].reshape(shape)

    zrow = jnp.zeros((1, D), F32)
    gb_f_row = jnp.pad(st["gb_f"][0:1], ((0, 0), (0, D - 128)))
    stats16 = jnp.concatenate([
        st["first"][1:2], st["first"][0:1], st["mid"][4:5], st["mid"][1:2], st["mid"][0:1], st["loss"][3:4],
        st["mid"][2:3], st["mid"][3:4], st["loss"][1:2], st["loss"][2:3], st["gb_sb"][0:1], st["gb_fx"][0:1],
        st["loss"][0:1], gb_f_row, zrow, zrow], axis=0)
    st_all = _all_gather_small(stats16, "gather_stats")
    st_sum, loss_blk = _sum_stats(st_all, 12)
    loss = loss_blk[0, 0]

    d_ada_all = st_all[:, :n_cond, :].reshape(N_DEV, n_cond * D)
    d_cols = lax.dynamic_slice(d_ada_all, (0, me * ada_loc), (N_DEV, ada_loc))
    d16 = jnp.pad(d_cols, ((0, 16 - N_DEV), (0, 0)))
    g_w_ada = _mm([(c16, d16)], 'tn', F32, "ada_wgrad", silu_a=True)

    small_w = jnp.concatenate([b_ada.reshape(n_cond, D), ln1_g, ln1_b, ln2_g, ln2_b, b_gate.reshape(2, D), zrow,
                               jnp.pad(b_forget, ((0, 0), (0, D - n_heads))), zrow, zrow], axis=0)
    small_m = jnp.concatenate([m_b_ada.reshape(n_cond, D), m_ln1_g, m_ln1_b, m_ln2_g, m_ln2_b, m_b_gate.reshape(2, D),
                               zrow, jnp.pad(m_b_forget, ((0, 0), (0, D - n_heads))), zrow, zrow], axis=0)
    small_v = jnp.concatenate([v_b_ada.reshape(n_cond, D), v_ln1_g, v_ln1_b, v_ln2_g, v_ln2_b, v_b_gate.reshape(2, D),
                               zrow, jnp.pad(v_b_forget, ((0, 0), (0, D - n_heads))), zrow, zrow], axis=0)
    sm = _adamw(small_w, st_sum, small_m, small_v, "adamw_small")

    def small(a, nm):
        if nm == "b_ada":
            return a[0:n_cond].reshape(1, n_cond * D)
        if nm == "b_gate":
            return a[10:12].reshape(1, 2 * D)
        if nm == "b_forget":
            return a[13:14, :n_heads]
        row = {"ln1_g": 6, "ln1_b": 7, "ln2_g": 8, "ln2_b": 9}[nm]
        return a[row:row + 1]

    big = {
        "w_ada": (w_ada[0], g_w_ada, m_w_ada[0], v_w_ada[0]),
        "w_in": (w_in[0], gshard("in", w_in.shape[1:]), m_w_in[0], v_w_in[0]),
        "w_sb_out": (w_sb_out[0], gshard("sb", w_sb_out.shape[1:]), m_w_sb_out[0], v_w_sb_out[0]),
        "w_fox_out": (w_fox_out[0], gshard("fx", w_fox_out.shape[1:]), m_w_fox_out[0], v_w_fox_out[0]),
        "w_o": (w_o[0], gshard("o", w_o.shape[1:]), m_w_o[0], v_w_o[0]),
        "w_ffn_gate": (w_ffn_gate[0], gshard("fg", w_ffn_gate.shape[1:]), m_w_ffn_gate[0], v_w_ffn_gate[0]),
        "w_ffn_up": (w_ffn_up[0], gshard("fu", w_ffn_up.shape[1:]), m_w_ffn_up[0], v_w_ffn_up[0]),
        "w_ffn_down": (w_ffn_down[0], gshard("fd", w_ffn_down.shape[1:]), m_w_ffn_down[0], v_w_ffn_down[0]),
    }
    order = ["w_ada", "b_ada", "w_in", "b_gate", "b_forget", "w_sb_out", "w_fox_out", "w_o", "ln1_g", "ln1_b",
             "w_ffn_gate", "w_ffn_up", "w_ffn_down", "ln2_g", "ln2_b"]
    grads, deltas, new_ms, new_vs = [], [], [], []
    for nm in order:
        if nm in big:
            w, g, m, v = big[nm]
            d, nm_, nv_ = _adamw(w, g, m, v, "adamw_" + nm)
            grads.append(g[None])
            deltas.append(d[None])
            new_ms.append(nm_[None])
            new_vs.append(nv_[None])
        else:
            grads.append(small(st_sum, nm))
            deltas.append(small(sm[0], nm))
            new_ms.append(small(sm[1], nm))
            new_vs.append(small(sm[2], nm))
    return (loss, gx[None], *grads, *deltas, *new_ms, *new_vs)
```

```python
import functools

import jax
import jax.numpy as jnp
import numpy as np
from jax import lax
from jax.experimental import pallas as pl
from jax.experimental.pallas import tpu as pltpu

F32 = jnp.float32
BF16 = jnp.bfloat16

HEAD_DIM = 64
PAIR = 2 * HEAD_DIM
LN_EPS = 1e-5
ALPHA = 2.0 ** 0.25
ADAM_LR, ADAM_B1, ADAM_B2, ADAM_EPS, ADAM_WD, ADAM_STEP = 0.001, 0.9, 0.999, 1e-08, 0.01, 10
N_DEV = 8
VMEM_LIMIT = 56 * 1024 * 1024
MESH = pl.DeviceIdType.MESH


def _dot(a, b, ca=1, cb=0):
    return lax.dot_general(a, b, (((ca,), (cb,)), ((), ())), preferred_element_type=F32)


def _pick(n, cands):
    for t in cands:
        if n % t == 0:
            return t
    return n


def _params(sem):
    return pltpu.CompilerParams(dimension_semantics=sem, vmem_limit_bytes=VMEM_LIMIT)


MM_BLOCK_BYTES = 40 * 1024 * 1024
LANES = 128


def _divisors(n, cap):
    ds = [d for d in range(LANES, min(n, cap) + 1, LANES) if n % d == 0]
    return sorted(ds, reverse=True) or [n]


def _mm_tiles(M, N, a_row_bytes, b_row_bytes, out_itemsize):
    best = None
    for tm in _divisors(M, 1024):
        for tn in _divisors(N, 2048):
            need = 2 * (tm * a_row_bytes + tn * b_row_bytes + tm * tn * out_itemsize) + tm * tn * 4
            if need <= MM_BLOCK_BYTES and (best is None or (tm * tn, tm) > (best[0] * best[1], best[0])):
                best = (tm, tn)
    assert best is not None, (M, N, a_row_bytes, b_row_bytes)
    return best


def _mm(pairs, mode, out_dtype, name, bias=None, act=None, silu_a=False):
    norm = []
    for p in pairs:
        a, b = p[0], p[1]
        kdim_a = a.shape[0] if mode == 'tn' else a.shape[1]
        K, ka, kb = (p[2], p[3], p[4]) if len(p) > 2 else (kdim_a, 0, 0)
        norm.append((a, b, K, ka, kb))
    a0, b0 = norm[0][0], norm[0][1]
    M = a0.shape[1] if mode == 'tn' else a0.shape[0]
    N = b0.shape[0] if mode == 'nt' else b0.shape[1]
    tm, tn = _mm_tiles(M, N, sum(K * a.dtype.itemsize for a, _, K, _, _ in norm),
                       sum(K * b.dtype.itemsize for _, b, K, _, _ in norm), jnp.dtype(out_dtype).itemsize)
    n_pairs = len(norm)

    in_specs, args = [], []
    for a, b, K, ka, kb in norm:
        if mode == 'tn':
            in_specs.append(pl.BlockSpec((K, tm), lambda i, j, ka=ka: (ka, i)))
        else:
            in_specs.append(pl.BlockSpec((tm, K), lambda i, j, ka=ka: (i, ka)))
        if mode == 'nt':
            in_specs.append(pl.BlockSpec((tn, K), lambda i, j, kb=kb: (j, kb)))
        else:
            in_specs.append(pl.BlockSpec((K, tn), lambda i, j, kb=kb: (kb, j)))
        args += [a, b]
    if bias is not None:
        in_specs.append(pl.BlockSpec((1, tn), lambda i, j: (0, j)))
        args.append(bias)

    ca = 0 if mode == 'tn' else 1
    cb = 1 if mode == 'nt' else 0

    def body(*refs):
        o_ref = refs[-1]
        acc = None
        for p in range(n_pairs):
            av = refs[2 * p][...]
            if silu_a:
                av = av / (1.0 + jnp.exp(-av))
            d = _dot(av.astype(BF16), refs[2 * p + 1][...].astype(BF16), ca, cb)
            acc = d if acc is None else acc + d
        if bias is not None:
            acc = acc + refs[2 * n_pairs][...]
        if act == 'sigmoid':
            acc = 1.0 / (1.0 + jnp.exp(-acc))
        o_ref[...] = acc.astype(out_dtype)

    return pl.pallas_call(
        body, name=name, grid=(M // tm, N // tn), in_specs=in_specs,
        out_specs=pl.BlockSpec((tm, tn), lambda i, j: (i, j)),
        out_shape=jax.ShapeDtypeStruct((M, N), out_dtype),
        compiler_params=_params(("parallel", "parallel")),
    )(*args)


def _rows_call(body, name, row_ins, vec_ins, row_outs, acc_outs, ts):
    S = row_ins[0].shape[0]
    in_specs = [pl.BlockSpec((ts, a.shape[1]), lambda i: (i, 0)) for a in row_ins]
    in_specs += [pl.BlockSpec(a.shape, lambda i: (0, 0)) for a in vec_ins]
    out_specs = [pl.BlockSpec((ts, c), lambda i: (i, 0)) for c, _ in row_outs]
    out_specs += [pl.BlockSpec(s, lambda i: (0, 0)) for s in acc_outs]
    out_shape = [jax.ShapeDtypeStruct((S, c), dt) for c, dt in row_outs]
    out_shape += [jax.ShapeDtypeStruct(s, F32) for s in acc_outs]
    return pl.pallas_call(
        body, name=name, grid=(S // ts,), in_specs=in_specs, out_specs=out_specs, out_shape=out_shape,
        compiler_params=_params(("arbitrary",)),
    )(*row_ins, *vec_ins)


def _ln_stats(v):
    mu = jnp.mean(v, axis=-1, keepdims=True)
    d = v - mu
    var = jnp.mean(d * d, axis=-1, keepdims=True)
    rstd = lax.rsqrt(var + LN_EPS)
    return d * rstd, rstd


def _ln_bwd(dxhat, xhat, rstd):
    m1 = jnp.mean(dxhat, axis=-1, keepdims=True)
    m2 = jnp.mean(dxhat * xhat, axis=-1, keepdims=True)
    return rstd * (dxhat - m1 - xhat * m2)


def _colsum(v):
    return jnp.sum(v, axis=0, keepdims=True)


def _ln_mod(x, ada8, ts):
    D = x.shape[1]

    def body(x_ref, v_ref, u_ref):
        xhat, _ = _ln_stats(x_ref[...])
        u_ref[...] = (xhat * (1.0 + v_ref[1:2, :]) + v_ref[0:1, :]).astype(BF16)

    return _rows_call(body, "ln_mod", [x], [ada8], [(D, BF16)], [], ts)[0]


def _gate_mix(g_sb, g_fx, y_sb, y_fx, ts):
    D = y_sb.shape[1]

    def body(gs, gf, ys, yf, o_ref):
        o_ref[...] = (gs[...] * ys[...] + gf[...] * yf[...]).astype(BF16)

    return _rows_call(body, "gate_mix", [g_sb, g_fx, y_sb, y_fx], [], [(D, BF16)], [], ts)[0]


def _post_attn(x, mix, ada8, lnp8, ts):
    D = x.shape[1]

    def body(x_ref, mix_ref, v_ref, p_ref, x1_ref, u2_ref):
        r1 = ALPHA * x_ref[...] + v_ref[2:3, :] * mix_ref[...]
        xhat, _ = _ln_stats(r1)
        x1 = xhat * p_ref[0:1, :] + p_ref[1:2, :]
        x1_ref[...] = x1
        xh1, _ = _ln_stats(x1)
        u2_ref[...] = (xh1 * (1.0 + v_ref[4:5, :]) + v_ref[3:4, :]).astype(BF16)

    return _rows_call(body, "post_attn", [x, mix], [ada8, lnp8], [(D, F32), (D, BF16)], [], ts)


def _loss_head(x1, h, target, ada8, lnp8, ts):
    D = x1.shape[1]

    def body(x1_ref, h_ref, t_ref, v_ref, p_ref, dr2_ref, dh_ref, st_ref):
        @pl.when(pl.program_id(0) == 0)
        def _():
            st_ref[...] = jnp.zeros_like(st_ref)

        hv = h_ref[...]
        g2 = v_ref[5:6, :]
        r2 = ALPHA * x1_ref[...] + g2 * hv
        xhat, rstd = _ln_stats(r2)
        y = xhat * p_ref[2:3, :] + p_ref[3:4, :]
        err = y - t_ref[...]
        dy = err * (1.0 / D)
        dr2 = _ln_bwd(dy * p_ref[2:3, :], xhat, rstd)
        dr2_ref[...] = dr2
        dh_ref[...] = (dr2 * g2).astype(BF16)
        st_ref[0:1, :] += _colsum(err * err) * (0.5 / D)
        st_ref[1:2, :] += _colsum(dy * xhat)
        st_ref[2:3, :] += _colsum(dy)
        st_ref[3:4, :] += _colsum(dr2 * hv)

    return _rows_call(body, "loss_head", [x1, h, target], [ada8, lnp8], [(D, F32), (D, BF16)], [(8, D)], ts)


def _mid_bwd(du2, x1, dr2, mix, x, ada8, lnp8, ts):
    D = x.shape[1]

    def body(du2_ref, x1_ref, dr2_ref, mix_ref, x_ref, v_ref, p_ref, dr1_ref, dmix_ref, st_ref):
        @pl.when(pl.program_id(0) == 0)
        def _():
            st_ref[...] = jnp.zeros_like(st_ref)

        du2v = du2_ref[...]
        xh1, rstd1 = _ln_stats(x1_ref[...])
        dx1 = ALPHA * dr2_ref[...] + _ln_bwd(du2v * (1.0 + v_ref[4:5, :]), xh1, rstd1)
        mixv = mix_ref[...]
        g1 = v_ref[2:3, :]
        r1 = ALPHA * x_ref[...] + g1 * mixv
        xhr, rstdr = _ln_stats(r1)
        dr1 = _ln_bwd(dx1 * p_ref[0:1, :], xhr, rstdr)
        dr1_ref[...] = dr1
        dmix_ref[...] = (dr1 * g1).astype(BF16)
        st_ref[0:1, :] += _colsum(du2v * xh1)
        st_ref[1:2, :] += _colsum(du2v)
        st_ref[2:3, :] += _colsum(dx1 * xhr)
        st_ref[3:4, :] += _colsum(dx1)
        st_ref[4:5, :] += _colsum(dr1 * mixv)

    return _rows_call(body, "mid_bwd", [du2, x1, dr2, mix, x], [ada8, lnp8], [(D, F32), (D, BF16)], [(8, D)], ts)


def _first_bwd(du1, x, dr1, ada8, ts):
    D = x.shape[1]

    def body(du1_ref, x_ref, dr1_ref, v_ref, gx_ref, st_ref):
        @pl.when(pl.program_id(0) == 0)
        def _():
            st_ref[...] = jnp.zeros_like(st_ref)

        du1v = du1_ref[...]
        xh0, rstd0 = _ln_stats(x_ref[...])
        gx_ref[...] = ALPHA * dr1_ref[...] + _ln_bwd(du1v * (1.0 + v_ref[1:2, :]), xh0, rstd0)
        st_ref[0:1, :] += _colsum(du1v * xh0)
        st_ref[1:2, :] += _colsum(du1v)

    return _rows_call(body, "first_bwd", [du1, x, dr1], [ada8], [(D, F32)], [(8, D)], ts)


def _split3(v):
    hi = v.astype(BF16)
    r = v - hi.astype(F32)
    mid = r.astype(BF16)
    lo = (r - mid.astype(F32)).astype(BF16)
    return hi, mid, lo


def _fgate_fwd(f, bf_pad, tb):
    S = f.shape[0]

    def body(f_ref, b_ref, fc_ref, carry):
        @pl.when(pl.program_id(0) == 0)
        def _():
            carry[...] = jnp.zeros_like(carry)

        z = f_ref[...] + b_ref[...]
        ls = jnp.minimum(z, 0.0) - jnp.log(1.0 + jnp.exp(-jnp.abs(z)))
        r = lax.broadcasted_iota(jnp.int32, (tb, tb), 0)
        c = lax.broadcasted_iota(jnp.int32, (tb, tb), 1)
        tri = (c <= r).astype(BF16)
        hi, mid, lo = _split3(ls)
        cs = _dot(tri, hi) + _dot(tri, mid) + _dot(tri, lo) + carry[...]
        fc_ref[...] = cs
        carry[...] = cs[tb - 1:tb, :]

    return pl.pallas_call(
        body, name="fgate_fwd", grid=(S // tb,),
        in_specs=[pl.BlockSpec((tb, 128), lambda i: (i, 0)), pl.BlockSpec((1, 128), lambda i: (0, 0))],
        out_specs=pl.BlockSpec((tb, 128), lambda i: (i, 0)),
        out_shape=jax.ShapeDtypeStruct((S, 128), F32),
        scratch_shapes=[pltpu.VMEM((1, 128), F32)],
        compiler_params=_params(("arbitrary",)),
    )(f, bf_pad)


def _fgate_bwd(dfc, f, bf_pad, tb):
    S = f.shape[0]
    nb = S // tb

    def body(d_ref, f_ref, b_ref, df_ref, gb_ref, carry):
        @pl.when(pl.program_id(0) == 0)
        def _():
            carry[...] = jnp.zeros_like(carry)
            gb_ref[...] = jnp.zeros_like(gb_ref)

        r = lax.broadcasted_iota(jnp.int32, (tb, tb), 0)
        c = lax.broadcasted_iota(jnp.int32, (tb, tb), 1)
        tri = (c >= r).astype(BF16)
        hi, mid, lo = _split3(d_ref[...])
        rs = _dot(tri, hi) + _dot(tri, mid) + _dot(tri, lo) + carry[...]
        carry[...] = rs[0:1, :]
        z = f_ref[...] + b_ref[...]
        df = rs * (1.0 / (1.0 + jnp.exp(z)))
        df_ref[...] = df
        gb_ref[0:1, :] += _colsum(df)

    return pl.pallas_call(
        body, name="fgate_bwd", grid=(nb,),
        in_specs=[pl.BlockSpec((tb, 128), lambda i: (nb - 1 - i, 0)),
                  pl.BlockSpec((tb, 128), lambda i: (nb - 1 - i, 0)),
                  pl.BlockSpec((1, 128), lambda i: (0, 0))],
        out_specs=[pl.BlockSpec((tb, 128), lambda i: (nb - 1 - i, 0)), pl.BlockSpec((8, 128), lambda i: (0, 0))],
        out_shape=[jax.ShapeDtypeStruct((S, 128), F32), jax.ShapeDtypeStruct((8, 128), F32)],
        scratch_shapes=[pltpu.VMEM((1, 128), F32)],
        compiler_params=_params(("arbitrary",)),
    )(dfc, f, bf_pad)


def _split2(v):
    hi = v.astype(BF16)
    lo = (v - hi.astype(F32)).astype(BF16)
    return hi, lo


def _head_masks():
    lane = lax.broadcasted_iota(jnp.int32, (1, PAIR), 1)
    m0 = lane < HEAD_DIM
    return m0, jnp.logical_not(m0)


def _sel(mask, v):
    return jnp.where(mask, v, jnp.zeros_like(v))


def _softplus(z):
    return jnp.maximum(z, 0.0) + jnp.log(1.0 + jnp.exp(-jnp.abs(z)))


def _qkv_specs(S, tq, n_pairs, base):
    return [pl.BlockSpec((tq, PAIR), lambda p, i: (i, base + p)),
            pl.BlockSpec((S, PAIR), lambda p, i: (0, base + n_pairs + p)),
            pl.BlockSpec((S, PAIR), lambda p, i: (0, base + 2 * n_pairs + p))]


def _sb_fwd(qkv, n_pairs, base, tq):
    S = qkv.shape[0]
    tk = tq
    scale = HEAD_DIM ** -0.5

    def body(q_ref, k_ref, v_ref, o_ref, t_ref, acc_ref):
        i = pl.program_id(1)
        masks = _head_masks()
        q2 = q_ref[...]
        qm = [_sel(m, q2) for m in masks]
        rowpos = i * tq + lax.broadcasted_iota(jnp.int32, (tq, tk), 0)
        colin = lax.broadcasted_iota(jnp.int32, (tq, tk), 1)
        upper = (lax.broadcasted_iota(jnp.int32, (tk, tk), 0) > lax.broadcasted_iota(jnp.int32, (tk, tk), 1)).astype(BF16)
        acc_ref[...] = jnp.zeros_like(acc_ref)

        def step(jj, carry):
            j = i - jj
            off = pl.multiple_of(j * tk, tk)
            k2 = k_ref[pl.ds(off, tk), :]
            v2 = v_ref[pl.ds(off, tk), :]
            mask = (j * tk + colin) < rowpos
            out = None
            new = []
            for h in heads:
                z = _dot(qm[h], k2, 1, 1) * scale
                sp = _softplus(z)
                lg = jnp.where(mask, -sp, 0.0)
                hi, lo = _split2(lg)
                suf = _dot(hi, upper) + _dot(lo, upper)
                a = jnp.where(mask, jnp.exp(z - sp + suf + carry[h]), 0.0)
                d = _dot(a.astype(BF16), _sel(masks[h], v2))
                out = d if out is None else out + d
                new.append(carry[h] + jnp.sum(lg, axis=-1, keepdims=True))
            acc_ref[...] += out
            return tuple(new)

        zero = jnp.zeros((tq, 1), F32)
        r0, r1 = lax.fori_loop(0, i + 1, step, (zero, zero))
        o_ref[...] = acc_ref[...].astype(BF16)
        t_ref[...] = jnp.where(masks[0], r0, r1)

    W = n_pairs * PAIR
    return pl.pallas_call(
        body, name="sb_fwd", grid=(n_pairs, S // tq), in_specs=_qkv_specs(S, tq, n_pairs, base),
        out_specs=[pl.BlockSpec((tq, PAIR), lambda p, i: (i, p)), pl.BlockSpec((tq, PAIR), lambda p, i: (i, p))],
        out_shape=[jax.ShapeDtypeStruct((S, W), BF16), jax.ShapeDtypeStruct((S, W), F32)],
        scratch_shapes=[pltpu.VMEM((tq, PAIR), F32)],
        compiler_params=_params(("parallel", "arbitrary")),
    )(qkv, qkv, qkv)


def _sb_bwd(qkv, tot, do, n_pairs, base, tq):
    S = qkv.shape[0]
    tk = tq
    nq = S // tq
    scale = HEAD_DIM ** -0.5

    def body(q_ref, k_ref, v_ref, t_ref, do_ref, dq_ref, dk_ref, dv_ref, dq_acc, dk_acc, dv_acc):
        i = pl.program_id(1)
        masks = _head_masks()

        @pl.when(i == 0)
        def _():
            dk_acc[...] = jnp.zeros_like(dk_acc)
            dv_acc[...] = jnp.zeros_like(dv_acc)

        q2 = q_ref[...]
        do2 = do_ref[...]
        qm = [_sel(m, q2) for m in masks]
        dom = [_sel(m, do2) for m in masks]
        t2 = t_ref[...]
        tot_h = [t2[:, 0:1], t2[:, HEAD_DIM:HEAD_DIM + 1]]
        rowpos = i * tq + lax.broadcasted_iota(jnp.int32, (tq, tk), 0)
        colin = lax.broadcasted_iota(jnp.int32, (tq, tk), 1)
        r_i = lax.broadcasted_iota(jnp.int32, (tk, tk), 0)
        c_i = lax.broadcasted_iota(jnp.int32, (tk, tk), 1)
        upper = (r_i > c_i).astype(BF16)
        lower = (r_i < c_i).astype(BF16)
        dq_acc[...] = jnp.zeros_like(dq_acc)

        def step(j, carry):
            off = pl.multiple_of(j * tk, tk)
            k2 = k_ref[pl.ds(off, tk), :]
            v2 = v_ref[pl.ds(off, tk), :]
            mask = (j * tk + colin) < rowpos
            dq = None
            dk = None
            dv = None
            new = []
            for h in heads:
                cum_l, cum_g = carry[2 * h], carry[2 * h + 1]
                z = _dot(qm[h], k2, 1, 1) * scale
                sp = _softplus(z)
                lg = jnp.where(mask, -sp, 0.0)
                hi, lo = _split2(lg)
                suf = _dot(hi, upper) + _dot(lo, upper)
                row_l = jnp.sum(lg, axis=-1, keepdims=True)
                later = tot_h[h] - cum_l - row_l
                a = jnp.where(mask, jnp.exp(z - sp + suf + later), 0.0)
                da = _dot(dom[h], v2, 1, 1)
                g = da * a
                ghi, glo = _split2(g)
                pre = _dot(ghi, lower) + _dot(glo, lower) + cum_g
                one_m_beta = jnp.exp(-sp)
                dz = jnp.where(mask, g * one_m_beta - (1.0 - one_m_beta) * pre, 0.0)
                dzb = (dz * scale).astype(BF16)
                d1 = _dot(dzb, _sel(masks[h], k2))
                d2 = _dot(dzb, qm[h], 0, 0)
                d3 = _dot(a.astype(BF16), dom[h], 0, 0)
                dq = d1 if dq is None else dq + d1
                dk = d2 if dk is None else dk + d2
                dv = d3 if dv is None else dv + d3
                new += [cum_l + row_l, cum_g + jnp.sum(g, axis=-1, keepdims=True)]
            dq_acc[...] += dq
            dk_acc[pl.ds(off, tk), :] += dk
            dv_acc[pl.ds(off, tk), :] += dv
            return tuple(new)

        zero = jnp.zeros((tq, 1), F32)
        lax.fori_loop(0, i + 1, step, (zero, zero, zero, zero))
        dq_ref[...] = dq_acc[...].astype(BF16)

        @pl.when(i == nq - 1)
        def _():
            dk_ref[...] = dk_acc[...].astype(BF16)
            dv_ref[...] = dv_acc[...].astype(BF16)

    W = n_pairs * PAIR
    in_specs = _qkv_specs(S, tq, n_pairs, base) + [
        pl.BlockSpec((tq, PAIR), lambda p, i: (i, p)),
        pl.BlockSpec((tq, PAIR), lambda p, i: (i, p))]
    out_specs = [pl.BlockSpec((tq, PAIR), lambda p, i: (i, p)),
                 pl.BlockSpec((S, PAIR), lambda p, i: (0, p)),
                 pl.BlockSpec((S, PAIR), lambda p, i: (0, p))]
    dq, dk, dv = pl.pallas_call(
        body, name="sb_bwd", grid=(n_pairs, nq), in_specs=in_specs, out_specs=out_specs,
        out_shape=[jax.ShapeDtypeStruct((S, W), BF16)] * 3,
        scratch_shapes=[pltpu.VMEM((tq, PAIR), F32), pltpu.VMEM((S, PAIR), F32), pltpu.VMEM((S, PAIR), F32)],
        compiler_params=_params(("parallel", "arbitrary")),
    )(qkv, qkv, qkv, tot, do)
    return dq, dk, dv


NEG = -1e30


def _fox_specs(S, tq, n_pairs, base):
    return _qkv_specs(S, tq, n_pairs, base) + [
        pl.BlockSpec((tq, PAIR), lambda p, i: (i, p)),
        pl.BlockSpec((1, S // tq, 8, tq), lambda p, i: (p, 0, 0, 0))]


def _fox_fwd(qkv, fcx, fcr, n_pairs, base, tq):
    S = qkv.shape[0]
    tk = tq
    scale = HEAD_DIM ** -0.5

    def body(q_ref, k_ref, v_ref, fq_ref, fk_ref, o_ref, lse_ref, acc_ref):
        i = pl.program_id(1)
        masks = _head_masks()
        q2 = q_ref[...]
        qm = [_sel(m, q2) for m in masks]
        fq2 = fq_ref[...]
        fq = [fq2[:, 0:1], fq2[:, HEAD_DIM:HEAD_DIM + 1]]
        rowpos = i * tq + lax.broadcasted_iota(jnp.int32, (tq, tk), 0)
        colin = lax.broadcasted_iota(jnp.int32, (tq, tk), 1)
        acc_ref[...] = jnp.zeros_like(acc_ref)

        def step(j, carry):
            off = pl.multiple_of(j * tk, tk)
            k2 = k_ref[pl.ds(off, tk), :]
            v2 = v_ref[pl.ds(off, tk), :]
            fk2 = fk_ref[0, j]
            mask = (j * tk + colin) <= rowpos
            out = None
            new = []
            alphas = []
            for h in heads:
                m_old, l_old = carry[2 * h], carry[2 * h + 1]
                s = _dot(qm[h], k2, 1, 1) * scale + fq[h] - fk2[h:h + 1, :]
                s = jnp.where(mask, s, NEG)
                m_new = jnp.maximum(m_old, jnp.max(s, axis=-1, keepdims=True))
                p = jnp.exp(s - m_new)
                alpha = jnp.exp(m_old - m_new)
                alphas.append(alpha)
                d = _dot(p.astype(BF16), _sel(masks[h], v2))
                out = d if out is None else out + d
                new += [m_new, alpha * l_old + jnp.sum(p, axis=-1, keepdims=True)]
            acc_ref[...] = acc_ref[...] * jnp.where(masks[0], alphas[0], alphas[1]) + out
            return tuple(new)

        zero = jnp.zeros((tq, 1), F32)
        neg = jnp.full((tq, 1), NEG, F32)
        m0, l0, m1, l1 = lax.fori_loop(0, i + 1, step, (neg, zero, neg, zero))
        o_ref[...] = (acc_ref[...] / jnp.where(masks[0], l0, l1)).astype(BF16)
        lse_ref[...] = jnp.where(masks[0], m0 + jnp.log(l0), m1 + jnp.log(l1))

    W = n_pairs * PAIR
    return pl.pallas_call(
        body, name="fox_fwd", grid=(n_pairs, S // tq), in_specs=_fox_specs(S, tq, n_pairs, base),
        out_specs=[pl.BlockSpec((tq, PAIR), lambda p, i: (i, p)), pl.BlockSpec((tq, PAIR), lambda p, i: (i, p))],
        out_shape=[jax.ShapeDtypeStruct((S, W), BF16), jax.ShapeDtypeStruct((S, W), F32)],
        scratch_shapes=[pltpu.VMEM((tq, PAIR), F32)],
        compiler_params=_params(("parallel", "arbitrary")),
    )(qkv, qkv, qkv, fcx, fcr)


def _fox_bwd(qkv, fcx, fcr, o, lse, do, n_pairs, base, tq):
    S = qkv.shape[0]
    tk = tq
    nq = S // tq
    scale = HEAD_DIM ** -0.5

    def body(q_ref, k_ref, v_ref, fq_ref, fk_ref, o_ref, lse_ref, do_ref,
             dq_ref, dk_ref, dv_ref, dfq_ref, dfk_ref, dq_acc, dk_acc, dv_acc):
        i = pl.program_id(1)
        masks = _head_masks()

        @pl.when(i == 0)
        def _():
            dk_acc[...] = jnp.zeros_like(dk_acc)
            dv_acc[...] = jnp.zeros_like(dv_acc)
            dfk_ref[...] = jnp.zeros_like(dfk_ref)

        q2 = q_ref[...]
        do2 = do_ref[...]
        qm = [_sel(m, q2) for m in masks]
        dom = [_sel(m, do2) for m in masks]
        fq2 = fq_ref[...]
        fq = [fq2[:, 0:1], fq2[:, HEAD_DIM:HEAD_DIM + 1]]
        l2 = lse_ref[...]
        lse_h = [l2[:, 0:1], l2[:, HEAD_DIM:HEAD_DIM + 1]]
        prod = do2.astype(F32) * o_ref[...].astype(F32)
        delta = [jnp.sum(jnp.where(m, prod, 0.0), axis=-1, keepdims=True) for m in masks]
        rowpos = i * tq + lax.broadcasted_iota(jnp.int32, (tq, tk), 0)
        colin = lax.broadcasted_iota(jnp.int32, (tq, tk), 1)
        dq_acc[...] = jnp.zeros_like(dq_acc)

        def step(j, carry):
            off = pl.multiple_of(j * tk, tk)
            k2 = k_ref[pl.ds(off, tk), :]
            v2 = v_ref[pl.ds(off, tk), :]
            fk2 = fk_ref[0, j]
            mask = (j * tk + colin) <= rowpos
            dq = None
            dk = None
            dv = None
            new = []
            dfk_rows = []
            for h in heads:
                s = _dot(qm[h], k2, 1, 1) * scale + fq[h] - fk2[h:h + 1, :]
                p = jnp.where(mask, jnp.exp(s - lse_h[h]), 0.0)
                dp = _dot(dom[h], v2, 1, 1)
                ds = p * (dp - delta[h])
                dsb = (ds * scale).astype(BF16)
                d1 = _dot(dsb, _sel(masks[h], k2))
                d2 = _dot(dsb, qm[h], 0, 0)
                d3 = _dot(p.astype(BF16), dom[h], 0, 0)
                dq = d1 if dq is None else dq + d1
                dk = d2 if dk is None else dk + d2
                dv = d3 if dv is None else dv + d3
                new.append(carry[h] + jnp.sum(ds, axis=-1, keepdims=True))
                dfk_rows.append(jnp.sum(ds, axis=0, keepdims=True))
            dq_acc[...] += dq
            dk_acc[pl.ds(off, tk), :] += dk
            dv_acc[pl.ds(off, tk), :] += dv
            dfk_ref[0, j, 0:1, :] += dfk_rows[0]
            dfk_ref[0, j, 1:2, :] += dfk_rows[1]
            return tuple(new)

        zero = jnp.zeros((tq, 1), F32)
        r0, r1 = lax.fori_loop(0, i + 1, step, (zero, zero))
        dq_ref[...] = dq_acc[...].astype(BF16)
        dfq_ref[...] = jnp.where(masks[0], r0, r1)

        @pl.when(i == nq - 1)
        def _():
            dk_ref[...] = dk_acc[...].astype(BF16)
            dv_ref[...] = dv_acc[...].astype(BF16)

    W = n_pairs * PAIR
    in_specs = _fox_specs(S, tq, n_pairs, base) + [
        pl.BlockSpec((tq, PAIR), lambda p, i: (i, p)),
        pl.BlockSpec((tq, PAIR), lambda p, i: (i, p)),
        pl.BlockSpec((tq, PAIR), lambda p, i: (i, p))]
    out_specs = [pl.BlockSpec((tq, PAIR), lambda p, i: (i, p)),
                 pl.BlockSpec((S, PAIR), lambda p, i: (0, p)),
                 pl.BlockSpec((S, PAIR), lambda p, i: (0, p)),
                 pl.BlockSpec((tq, PAIR), lambda p, i: (i, p)),
                 pl.BlockSpec((1, nq, 8, tk), lambda p, i: (p, 0, 0, 0))]
    return pl.pallas_call(
        body, name="fox_bwd", grid=(n_pairs, nq), in_specs=in_specs, out_specs=out_specs,
        out_shape=[jax.ShapeDtypeStruct((S, W), BF16)] * 3
        + [jax.ShapeDtypeStruct((S, W), F32), jax.ShapeDtypeStruct((n_pairs, nq, 8, tk), F32)],
        scratch_shapes=[pltpu.VMEM((tq, PAIR), F32), pltpu.VMEM((S, PAIR), F32), pltpu.VMEM((S, PAIR), F32)],
        compiler_params=_params(("parallel", "arbitrary")),
    )(qkv, qkv, qkv, fcx, fcr, o, lse, do)


RC = 32
VANISH = -104.0


def _chunks(n_rows, fn):
    for ci in range(n_rows // RC):
        fn(ci * RC)


def _wide(v, tk):
    return v if tk == 128 else jnp.tile(v, (1, tk // 128))


def _rep(col):
    return jnp.broadcast_to(col, (col.shape[0], 128))


def _per_head(blk, masks):
    sw = pltpu.roll(blk, HEAD_DIM, axis=1)
    return jnp.where(masks[0], blk, sw), jnp.where(masks[0], sw, blk)


def _fill_masked(dst_ref, src_ref, masks, mul=None, ones_lane=None):
    v = src_ref[...]
    if mul is not None:
        v = v * mul
    lane = lax.broadcasted_iota(jnp.int32, (1, PAIR), 1)
    for h in range(2):
        m = _sel(masks[h], v)
        if ones_lane is not None:
            m = jnp.where(lane == ones_lane[h], jnp.ones_like(m), m)
        dst_ref[h] = m


def _head_norms(v, masks):
    sq = v.astype(F32)
    sq = sq * sq
    return [jnp.sqrt(jnp.sum(jnp.where(m, sq, 0.0), axis=-1, keepdims=True)) for m in masks]


def _largest_key_norm(kmax_ref, k_ref, masks):
    for h, n in enumerate(_head_norms(k_ref[...], masks)):
        kmax_ref[h] = jnp.broadcast_to(jnp.max(n, axis=0, keepdims=True), (8, 128))


def _score_bound(q_scaled, kmax_ref, masks):
    return [_rep(n) * kmax_ref[h][0:1, :] for h, n in enumerate(_head_norms(q_scaled, masks))]


def _tri(tk, cmp):
    r = lax.broadcasted_iota(jnp.int32, (tk, tk), 0)
    c = lax.broadcasted_iota(jnp.int32, (tk, tk), 1)
    return cmp(r, c).astype(BF16)


def _diag_mask(r0, tk, strict):
    row = r0 + lax.broadcasted_iota(jnp.int32, (RC, tk), 0)
    col = lax.broadcasted_iota(jnp.int32, (RC, tk), 1)
    return (col < row) if strict else (col <= row)


def _peer_copies(src_ref, land_ref, send_sems, recv_sems, scatter, receive_side):
    x, y, c = lax.axis_index("x"), lax.axis_index("y"), lax.axis_index("c")
    me = 4 * x + 2 * y + c
    copies = []
    for k in range(1, N_DEV):
        px, py, pc = (1 - x if k & 4 else x), (1 - y if k & 2 else y), (1 - c if k & 1 else c)
        slot = 4 * px + 2 * py + pc
        copies.append(pltpu.make_async_remote_copy(
            src_ref=src_ref.at[slot] if scatter else src_ref,
            dst_ref=land_ref.at[slot] if receive_side else land_ref.at[me],
            send_sem=send_sems.at[k - 1], recv_sem=recv_sems.at[k - 1], device_id=(px, py, pc), device_id_type=MESH))
    return copies


def _call_carrying(body, exchange, name, grid, in_specs, out_specs, out_shape, scratch_shapes, args):
    if exchange is None:
        return pl.pallas_call(body, name=name, grid=grid, in_specs=in_specs, out_specs=out_specs, out_shape=out_shape,
                              scratch_shapes=scratch_shapes, compiler_params=_params(("parallel", "arbitrary")))(*args)
    src, scatter = exchange
    n_in, n_out = len(in_specs), len(out_specs)

    def carrying(*refs):
        src_ref, land_ref = refs[n_in], refs[n_in + 1 + n_out]
        send_sems, recv_sems = refs[-2], refs[-1]
        first = jnp.logical_and(pl.program_id(0) == 0, pl.program_id(1) == 0)
        last = jnp.logical_and(pl.program_id(0) == grid[0] - 1, pl.program_id(1) == grid[1] - 1)

        @pl.when(first)
        def _():
            for cp in _peer_copies(src_ref, land_ref, send_sems, recv_sems, scatter, False):
                cp.start()

        body(*refs[:n_in], *refs[n_in + 1:n_in + 1 + n_out], *refs[n_in + 2 + n_out:-2])

        @pl.when(last)
        def _():
            for cp in _peer_copies(src_ref, land_ref, send_sems, recv_sems, scatter, True):
                cp.wait_send()
                cp.wait_recv()

    any_space = pl.BlockSpec(memory_space=pl.ANY)
    land = jax.ShapeDtypeStruct((N_DEV,) + src.shape[-2:], src.dtype)
    return pl.pallas_call(
        carrying, name=name, grid=grid, in_specs=list(in_specs) + [any_space],
        out_specs=list(out_specs) + [any_space], out_shape=list(out_shape) + [land],
        scratch_shapes=list(scratch_shapes) + [pltpu.SemaphoreType.DMA((N_DEV - 1,)), pltpu.SemaphoreType.DMA((N_DEV - 1,))],
        compiler_params=_params(("arbitrary", "arbitrary")))(*args, src)


def _staggered(bodies):
    active, waiting = [], list(bodies)
    while waiting or active:
        if waiting:
            active.append(waiting.pop(0))
        for g in list(active):
            try:
                next(g)
            except StopIteration:
                active.remove(g)


def _streams(tile, j, diag, slot):
    return [tile(j, diag, slot, (0, 1))]


def _tiles(i, tile):
    def step(jj, carry):
        _staggered(_streams(tile, 2 * jj, False, 0) + _streams(tile, 2 * jj + 1, False, 1))
        return carry
    lax.fori_loop(0, i // 2, step, 0)

    @pl.when(i % 2 == 1)
    def _():
        _staggered(_streams(tile, i - 1, False, 0))
    _staggered(_streams(tile, i, True, 1))


def _tiles_reversed(i, tile, keep_going):
    _staggered(_streams(tile, i, True, 1))

    def cond(carry):
        jj, go = carry
        return jnp.logical_and(jj < i // 2, go)

    def step(carry):
        jj, _ = carry
        _staggered(_streams(tile, i - 1 - 2 * jj, False, 0) + _streams(tile, i - 2 - 2 * jj, False, 1))
        return jj + 1, keep_going(jnp.maximum(i - 3 - 2 * jj, 0))

    jj, go = lax.while_loop(cond, step, (jnp.int32(0), keep_going(jnp.maximum(i - 1, 0))))

    @pl.when(jnp.logical_and(jnp.logical_and(i % 2 == 1, jj == i // 2), go))
    def _():
        _staggered(_streams(tile, 0, False, 0))


def _sb_fwd2(qkv, n_pairs, base, tq):
    S = qkv.shape[0]
    tk = tq
    scale = HEAD_DIM ** -0.5

    def body(q_ref, k_ref, v_ref, o_ref, t_ref, z_ref, hi_ref, suf_ref, p_ref, r_ref, acc_ref, vm_ref):
        i = pl.program_id(1)
        masks = _head_masks()

        @pl.when(i == 0)
        def _():
            _fill_masked(vm_ref, v_ref, masks)

        q2 = q_ref[...] * scale
        qm = [_sel(m, q2) for m in masks]
        incl = _tri(tk, lambda r, c: r >= c)
        r_ref[...] = jnp.zeros_like(r_ref)
        acc_ref[...] = jnp.zeros_like(acc_ref)

        def tile(j, diag, slot, heads):
            off = pl.multiple_of(j * tk, tk)
            k2 = k_ref[pl.ds(off, tk), :]
            v2 = v_ref[pl.ds(off, tk), :]
            for h in heads:
                z_ref[2 * slot + h] = _dot(qm[h], k2, 1, 1)
            yield
            for h in heads:
                def split(r0, h=h):
                    rows = pl.ds(r0, RC)
                    lg = -_softplus(z_ref[2 * slot + h, rows, :])
                    if diag:
                        lg = jnp.where(_diag_mask(r0, tk, True), lg, 0.0)
                    hi_ref[2 * slot + h, rows, :] = lg.astype(BF16)
                _chunks(tq, split)
            yield
            for h in heads:
                suf_ref[2 * slot + h] = _dot(hi_ref[2 * slot + h], incl)
            yield
            for h in heads:
                def weights(r0, h=h):
                    rows = pl.ds(r0, RC)
                    a = jnp.exp(z_ref[2 * slot + h, rows, :] + suf_ref[2 * slot + h, rows, :] + _wide(r_ref[h, rows, :], tk))
                    if diag:
                        a = jnp.where(_diag_mask(r0, tk, True), a, 0.0)
                    p_ref[2 * slot + h, rows, :] = a.astype(BF16)
                _chunks(tq, weights)
            yield
            keys = pl.ds(off, tk)
            for h in heads:
                acc_ref[...] += _dot(p_ref[2 * slot + h], vm_ref[h, keys, :])
            for h in heads:
                r_ref[h] += _rep(suf_ref[2 * slot + h, :, 0:1])

        _tiles_reversed(i, tile, lambda nearest: jnp.max(r_ref[...]) >= VANISH)
        o_ref[...] = acc_ref[...].astype(BF16)
        t_ref[...] = acc_ref[...]

    W = n_pairs * PAIR
    return pl.pallas_call(
        body, name="sb_fwd", grid=(n_pairs, S // tq), in_specs=_qkv_specs(S, tq, n_pairs, base),
        out_specs=[pl.BlockSpec((tq, PAIR), lambda p, i: (i, p)), pl.BlockSpec((tq, PAIR), lambda p, i: (i, p))],
        out_shape=[jax.ShapeDtypeStruct((S, W), BF16), jax.ShapeDtypeStruct((S, W), F32)],
        scratch_shapes=[pltpu.VMEM((4, tq, tk), F32), pltpu.VMEM((4, tq, tk), BF16),
                        pltpu.VMEM((4, tq, tk), F32), pltpu.VMEM((4, tq, tk), BF16), pltpu.VMEM((2, tq, 128), F32),
                        pltpu.VMEM((tq, PAIR), F32), pltpu.VMEM((2, S, PAIR), BF16)],
        compiler_params=_params(("parallel", "arbitrary")),
    )(qkv, qkv, qkv)


def _sb_bwd2(qkv, o32, do, n_pairs, base, tq):
    S = qkv.shape[0]
    tk = tq
    nq = S // tq
    scale = HEAD_DIM ** -0.5

    def body(q_ref, k_ref, v_ref, o_ref, do_ref, dq_ref, dk_ref, dv_ref,
             z_ref, g_ref, omb_ref, cum_ref, hi_ref, lo_ref, a_ref, dz_ref,
             r_ref, cg_ref, dl_ref, dq_acc, dk_acc, dv_acc, ks_ref):
        i = pl.program_id(1)
        masks = _head_masks()

        @pl.when(i == 0)
        def _():
            dk_acc[...] = jnp.zeros_like(dk_acc)
            dv_acc[...] = jnp.zeros_like(dv_acc)
            _fill_masked(ks_ref, k_ref, masks, mul=scale)

        q2 = q_ref[...] * scale
        do2 = do_ref[...]
        qm = [_sel(m, q2) for m in masks]
        dom = [_sel(m, do2) for m in masks]
        prod = do2.astype(F32) * o_ref[...]
        for h in range(2):
            dl_ref[h] = _rep(jnp.sum(jnp.where(masks[h], prod, 0.0), axis=-1, keepdims=True))
        suffix = _tri(tk, lambda r, c: r >= c)
        r_ref[...] = jnp.zeros_like(r_ref)
        cg_ref[...] = jnp.zeros_like(cg_ref)
        dq_acc[...] = jnp.zeros_like(dq_acc)

        def tile(j, diag, slot, heads):
            off = pl.multiple_of(j * tk, tk)
            k2 = k_ref[pl.ds(off, tk), :]
            v2 = v_ref[pl.ds(off, tk), :]
            for h in heads:
                z_ref[2 * slot + h] = _dot(qm[h], k2, 1, 1)
                g_ref[2 * slot + h] = _dot(dom[h], v2, 1, 1)
            yield
            for h in heads:
                def split(r0, h=h):
                    rows = pl.ds(r0, RC)
                    sp = _softplus(z_ref[2 * slot + h, rows, :])
                    omb_ref[2 * slot + h, rows, :] = jnp.exp(-sp)
                    lg = -sp
                    if diag:
                        lg = jnp.where(_diag_mask(r0, tk, True), lg, 0.0)
                    hi_ref[2 * slot + h, rows, :] = lg.astype(BF16)
                _chunks(tq, split)
            yield
            for h in heads:
                cum_ref[2 * slot + h] = _dot(hi_ref[2 * slot + h], suffix)
            yield
            for h in heads:
                def weights(r0, h=h):
                    rows = pl.ds(r0, RC)
                    a = jnp.exp(z_ref[2 * slot + h, rows, :] + cum_ref[2 * slot + h, rows, :] + _wide(r_ref[h, rows, :], tk))
                    if diag:
                        a = jnp.where(_diag_mask(r0, tk, True), a, 0.0)
                    ab = a.astype(BF16)
                    g = g_ref[2 * slot + h, rows, :] * ab.astype(F32)
                    g_ref[2 * slot + h, rows, :] = g
                    a_ref[2 * slot + h, rows, :] = ab
                    hi, lo = _split2(g)
                    hi_ref[2 * slot + h, rows, :] = hi
                    lo_ref[2 * slot + h, rows, :] = lo
                _chunks(tq, weights)
            for h in heads:
                r_ref[h] += _rep(cum_ref[2 * slot + h, :, 0:1])
            yield
            for h in heads:
                cum_ref[2 * slot + h] = _dot(hi_ref[2 * slot + h], suffix) + _dot(lo_ref[2 * slot + h], suffix)
            yield
            for h in heads:
                def dscore(r0, h=h):
                    rows = pl.ds(r0, RC)
                    g = g_ref[2 * slot + h, rows, :]
                    from_here = cum_ref[2 * slot + h, rows, :] + _wide(cg_ref[h, rows, :], tk)
                    before = _wide(dl_ref[h, rows, :], tk) - from_here
                    omb = omb_ref[2 * slot + h, rows, :]
                    dz = g * omb - (1.0 - omb) * before
                    if diag:
                        dz = jnp.where(_diag_mask(r0, tk, True), dz, 0.0)
                    dz_ref[2 * slot + h, rows, :] = dz.astype(BF16)
                _chunks(tq, dscore)
            for h in heads:
                cg_ref[h] += _rep(cum_ref[2 * slot + h, :, 0:1])
            yield
            keys = pl.ds(off, tk)
            for h in heads:
                dq_acc[...] += _dot(dz_ref[2 * slot + h], ks_ref[h, keys, :])
                dk_acc[keys, :] += _dot(dz_ref[2 * slot + h], qm[h], 0, 0)
                dv_acc[keys, :] += _dot(a_ref[2 * slot + h], dom[h], 0, 0)

        _tiles_reversed(i, tile, lambda nearest: jnp.max(r_ref[...]) >= VANISH)
        dq_ref[...] = dq_acc[...].astype(BF16)

        @pl.when(i == nq - 1)
        def _():
            dk_ref[...] = dk_acc[...].astype(BF16)
            dv_ref[...] = dv_acc[...].astype(BF16)

    W = n_pairs * PAIR
    in_specs = _qkv_specs(S, tq, n_pairs, base) + [
        pl.BlockSpec((tq, PAIR), lambda p, i: (i, p)),
        pl.BlockSpec((tq, PAIR), lambda p, i: (i, p))]
    out_specs = [pl.BlockSpec((tq, PAIR), lambda p, i: (i, p)),
                 pl.BlockSpec((S, PAIR), lambda p, i: (0, p)),
                 pl.BlockSpec((S, PAIR), lambda p, i: (0, p))]
    big, stat = (4, tq, tk), (2, tq, 128)
    return pl.pallas_call(
        body, name="sb_bwd", grid=(n_pairs, nq), in_specs=in_specs, out_specs=out_specs,
        out_shape=[jax.ShapeDtypeStruct((S, W), BF16)] * 3,
        scratch_shapes=[pltpu.VMEM(big, F32)] * 4 + [pltpu.VMEM(big, BF16)] * 4 + [pltpu.VMEM(stat, F32)] * 3
        + [pltpu.VMEM((tq, PAIR), F32), pltpu.VMEM((S, PAIR), F32), pltpu.VMEM((S, PAIR), F32),
           pltpu.VMEM((2, S, PAIR), BF16)],
        compiler_params=_params(("parallel", "arbitrary")),
    )(qkv, qkv, qkv, o32, do)


def _fox_fwd2(qkv, fcx, fcr, n_pairs, base, tq, exchange=None):
    S = qkv.shape[0]
    tk = tq
    scale = HEAD_DIM ** -0.5
    spare = (HEAD_DIM, 0)

    def body(q_ref, k_ref, v_ref, fq_ref, fk_ref, o_ref, lse_ref, s_ref, p_ref, m_ref, al_ref, fqr_ref, acc_ref, vm_ref,
             kmax_ref, top_ref):
        i = pl.program_id(1)
        masks = _head_masks()

        @pl.when(i == 0)
        def _():
            _fill_masked(vm_ref, v_ref, masks, ones_lane=spare)
            _largest_key_norm(kmax_ref, k_ref, masks)

        q2 = q_ref[...] * scale
        qm = [_sel(m, q2) for m in masks]
        f0, f1 = _per_head(fq_ref[...], masks)
        fqr_ref[0] = f0
        fqr_ref[1] = f1
        for h, b in enumerate(_score_bound(q2, kmax_ref, masks)):
            top_ref[h] = b + fqr_ref[h]
        m_ref[...] = jnp.full(m_ref.shape, NEG, F32)
        acc_ref[...] = jnp.zeros_like(acc_ref)

        def keep_going(nearest):
            last_fc = fk_ref[0, nearest]
            worst = [jnp.max(top_ref[h] - m_ref[h] - last_fc[h:h + 1, tk - 1:tk]) for h in range(2)]
            return jnp.maximum(worst[0], worst[1]) >= VANISH

        def tile(j, diag, slot, heads):
            off = pl.multiple_of(j * tk, tk)
            k2 = k_ref[pl.ds(off, tk), :]
            v2 = v_ref[pl.ds(off, tk), :]
            fk2 = fk_ref[0, j]
            for h in heads:
                s_ref[2 * slot + h] = _dot(qm[h], k2, 1, 1)
            yield
            for h in heads:
                fk_row = fk2[h:h + 1, :]

                def probs(r0, h=h, fk_row=fk_row):
                    rows = pl.ds(r0, RC)
                    sv = s_ref[2 * slot + h, rows, :] - fk_row
                    if diag:
                        sv = jnp.where(_diag_mask(r0, tk, False), sv, NEG)
                    fq = fqr_ref[h, rows, :]
                    m_prev = m_ref[h, rows, :]
                    m_new = jnp.maximum(m_prev, jnp.max(sv, axis=-1, keepdims=True) + fq)
                    p_ref[2 * slot + h, rows, :] = jnp.exp(sv + _wide(fq - m_new, tk)).astype(BF16)
                    al_ref[2 * slot + h, rows, :] = jnp.exp(m_prev - m_new)
                    m_ref[h, rows, :] = m_new
                _chunks(tq, probs)
            yield
            for h in heads:
                acc_ref[h] = acc_ref[h] * al_ref[2 * slot + h] + _dot(p_ref[2 * slot + h], vm_ref[h, pl.ds(off, tk), :])

        _tiles_reversed(i, tile, keep_going)
        a0, a1 = acc_ref[0], acc_ref[1]
        l0 = _rep(a0[:, spare[0]:spare[0] + 1])
        l1 = _rep(a1[:, spare[1]:spare[1] + 1])
        o_ref[...] = jnp.where(masks[0], a0 / l0, a1 / l1).astype(BF16)
        lse_ref[...] = jnp.where(masks[0], m_ref[0] + jnp.log(l0), m_ref[1] + jnp.log(l1))

    W = n_pairs * PAIR
    return _call_carrying(
        body, exchange, "fox_fwd", (n_pairs, S // tq), _fox_specs(S, tq, n_pairs, base),
        [pl.BlockSpec((tq, PAIR), lambda p, i: (i, p)), pl.BlockSpec((tq, PAIR), lambda p, i: (i, p))],
        [jax.ShapeDtypeStruct((S, W), BF16), jax.ShapeDtypeStruct((S, W), F32)],
        [pltpu.VMEM((4, tq, tk), F32), pltpu.VMEM((4, tq, tk), BF16), pltpu.VMEM((2, tq, 128), F32),
         pltpu.VMEM((4, tq, 128), F32), pltpu.VMEM((2, tq, 128), F32), pltpu.VMEM((2, tq, 128), F32),
         pltpu.VMEM((2, S, PAIR), BF16), pltpu.VMEM((2, 8, 128), F32), pltpu.VMEM((2, tq, 128), F32)],
        (qkv, qkv, qkv, fcx, fcr))


def _fox_bwd2(qkv, fcx, fcr, o, lse, do, n_pairs, base, tq, exchange=None):
    S = qkv.shape[0]
    tk = tq
    nq = S // tq
    scale = HEAD_DIM ** -0.5

    def body(q_ref, k_ref, v_ref, fq_ref, fk_ref, o_ref, lse_ref, do_ref,
             dq_ref, dk_ref, dv_ref, dfq_ref, dfk_ref,
             s_ref, dp_ref, p_ref, ds_ref, row_ref, dl_ref, dfq_acc, col_ref, dq_acc, dk_acc, dv_acc, ks_ref,
             kmax_ref, top_ref):
        i = pl.program_id(1)
        masks = _head_masks()

        @pl.when(i == 0)
        def _():
            dk_acc[...] = jnp.zeros_like(dk_acc)
            dv_acc[...] = jnp.zeros_like(dv_acc)
            dfk_ref[...] = jnp.zeros_like(dfk_ref)
            _fill_masked(ks_ref, k_ref, masks, mul=scale)
            _largest_key_norm(kmax_ref, k_ref, masks)

        q2 = q_ref[...] * scale
        do2 = do_ref[...]
        qm = [_sel(m, q2) for m in masks]
        dom = [_sel(m, do2) for m in masks]
        f0, f1 = _per_head(fq_ref[...], masks)
        l0, l1 = _per_head(lse_ref[...], masks)
        row_ref[0] = f0 - l0
        row_ref[1] = f1 - l1
        for h, b in enumerate(_score_bound(q2, kmax_ref, masks)):
            top_ref[h] = jnp.broadcast_to(jnp.max(b + row_ref[h], axis=0, keepdims=True)[:, 0:1], (8, 128))

        def keep_going(nearest):
            last_fc = fk_ref[0, nearest]
            worst = [jnp.max(top_ref[h][0:1, 0:1] - last_fc[h:h + 1, tk - 1:tk]) for h in range(2)]
            return jnp.maximum(worst[0], worst[1]) >= VANISH
        prod = do2.astype(F32) * o_ref[...].astype(F32)
        for h in range(2):
            dl_ref[h] = _rep(jnp.sum(jnp.where(masks[h], prod, 0.0), axis=-1, keepdims=True))
        dfq_acc[...] = jnp.zeros_like(dfq_acc)
        dq_acc[...] = jnp.zeros_like(dq_acc)

        def tile(j, diag, slot, heads):
            off = pl.multiple_of(j * tk, tk)
            k2 = k_ref[pl.ds(off, tk), :]
            v2 = v_ref[pl.ds(off, tk), :]
            fk2 = fk_ref[0, j]
            for h in heads:
                s_ref[2 * slot + h] = _dot(qm[h], k2, 1, 1)
                dp_ref[2 * slot + h] = _dot(dom[h], v2, 1, 1)
            yield
            for h in heads:
                col_ref[2 * slot + h] = jnp.zeros((8, tk), F32)
                fk_row = fk2[h:h + 1, :]

                def dscore(r0, h=h, fk_row=fk_row):
                    rows = pl.ds(r0, RC)
                    p = jnp.exp(s_ref[2 * slot + h, rows, :] - fk_row + _wide(row_ref[h, rows, :], tk))
                    if diag:
                        p = jnp.where(_diag_mask(r0, tk, False), p, 0.0)
                    ds = p * (dp_ref[2 * slot + h, rows, :] - _wide(dl_ref[h, rows, :], tk))
                    p_ref[2 * slot + h, rows, :] = p.astype(BF16)
                    ds_ref[2 * slot + h, rows, :] = ds.astype(BF16)
                    dfq_acc[h, rows, :] += _rep(jnp.sum(ds, axis=-1, keepdims=True))
                    col_ref[2 * slot + h] += jnp.sum(ds.reshape(RC // 8, 8, tk), axis=0)
                _chunks(tq, dscore)
            yield
            keys = pl.ds(off, tk)
            for h in heads:
                dq_acc[...] += _dot(ds_ref[2 * slot + h], ks_ref[h, keys, :])
                dk_acc[keys, :] += _dot(ds_ref[2 * slot + h], qm[h], 0, 0)
                dv_acc[keys, :] += _dot(p_ref[2 * slot + h], dom[h], 0, 0)
            for h in heads:
                dfk_ref[0, j, h:h + 1, :] += jnp.sum(col_ref[2 * slot + h], axis=0, keepdims=True)

        _tiles_reversed(i, tile, keep_going)
        dq_ref[...] = dq_acc[...].astype(BF16)
        dfq_ref[...] = jnp.where(masks[0], dfq_acc[0], dfq_acc[1])

        @pl.when(i == nq - 1)
        def _():
            dk_ref[...] = dk_acc[...].astype(BF16)
            dv_ref[...] = dv_acc[...].astype(BF16)

    W = n_pairs * PAIR
    in_specs = _fox_specs(S, tq, n_pairs, base) + [
        pl.BlockSpec((tq, PAIR), lambda p, i: (i, p)),
        pl.BlockSpec((tq, PAIR), lambda p, i: (i, p)),
        pl.BlockSpec((tq, PAIR), lambda p, i: (i, p))]
    out_specs = [pl.BlockSpec((tq, PAIR), lambda p, i: (i, p)),
                 pl.BlockSpec((S, PAIR), lambda p, i: (0, p)),
                 pl.BlockSpec((S, PAIR), lambda p, i: (0, p)),
                 pl.BlockSpec((tq, PAIR), lambda p, i: (i, p)),
                 pl.BlockSpec((1, nq, 8, tk), lambda p, i: (p, 0, 0, 0))]
    return _call_carrying(
        body, exchange, "fox_bwd", (n_pairs, nq), in_specs, out_specs,
        [jax.ShapeDtypeStruct((S, W), BF16)] * 3
        + [jax.ShapeDtypeStruct((S, W), F32), jax.ShapeDtypeStruct((n_pairs, nq, 8, tk), F32)],
        [pltpu.VMEM((4, tq, tk), F32)] * 2 + [pltpu.VMEM((4, tq, tk), BF16)] * 2
        + [pltpu.VMEM((2, tq, 128), F32)] * 3 + [pltpu.VMEM((4, 8, tk), F32)]
        + [pltpu.VMEM((tq, PAIR), F32), pltpu.VMEM((S, PAIR), F32), pltpu.VMEM((S, PAIR), F32),
           pltpu.VMEM((2, S, PAIR), BF16), pltpu.VMEM((2, 8, 128), F32), pltpu.VMEM((2, 8, 128), F32)],
        (qkv, qkv, qkv, fcx, fcr, o, lse, do))


def _swiglu_fwd(u2, wg, wu):
    S, D = u2.shape
    FF = wg.shape[1]
    tm, tn = _pick(S, (512, 256, 128)), _divisors(FF, 1536)[0]

    def body(u_ref, g_ref, w_ref, a_ref, b_ref, h_ref):
        u = u_ref[...]
        a = _dot(u, g_ref[...])
        b = _dot(u, w_ref[...])
        a_ref[...] = a.astype(BF16)
        b_ref[...] = b.astype(BF16)
        h_ref[...] = (a / (1.0 + jnp.exp(-a)) * b).astype(BF16)

    spec_o = pl.BlockSpec((tm, tn), lambda i, j: (i, j))
    return pl.pallas_call(
        body, name="swiglu_fwd", grid=(S // tm, FF // tn),
        in_specs=[pl.BlockSpec((tm, D), lambda i, j: (i, 0)),
                  pl.BlockSpec((D, tn), lambda i, j: (0, j)),
                  pl.BlockSpec((D, tn), lambda i, j: (0, j))],
        out_specs=[spec_o] * 3, out_shape=[jax.ShapeDtypeStruct((S, FF), BF16)] * 3,
        compiler_params=_params(("parallel", "parallel")),
    )(u2, wg, wu)


def _swiglu_bwd(dh, wd, a, b):
    S, D = dh.shape
    FF = wd.shape[0]
    tm, tn = _pick(S, (512, 256, 128)), _divisors(FF, 1536)[0]

    def body(dh_ref, w_ref, a_ref, b_ref, da_ref, db_ref):
        dhin = _dot(dh_ref[...], w_ref[...], 1, 1)
        av = a_ref[...].astype(F32)
        bv = b_ref[...].astype(F32)
        sig = 1.0 / (1.0 + jnp.exp(-av))
        da_ref[...] = (dhin * bv * (sig * (1.0 + av * (1.0 - sig)))).astype(BF16)
        db_ref[...] = (dhin * (av * sig)).astype(BF16)

    spec_o = pl.BlockSpec((tm, tn), lambda i, j: (i, j))
    return pl.pallas_call(
        body, name="swiglu_bwd", grid=(S // tm, FF // tn),
        in_specs=[pl.BlockSpec((tm, D), lambda i, j: (i, 0)),
                  pl.BlockSpec((tn, D), lambda i, j: (j, 0)), spec_o, spec_o],
        out_specs=[spec_o] * 2, out_shape=[jax.ShapeDtypeStruct((S, FF), BF16)] * 2,
        compiler_params=_params(("parallel", "parallel")),
    )(dh, wd, a, b)


def _gate_bwd(dmix, wo, g_sb, g_fx, y_sb, y_fx):
    S, D = dmix.shape
    tm, tn = _pick(S, (512, 256, 128)), _divisors(D, 1024)[0]

    def body(dm_ref, w_ref, gs_ref, gf_ref, ys_ref, yf_ref, dys_ref, dyf_ref, dls_ref, dlf_ref, bs_ref, bf_ref):
        @pl.when(pl.program_id(1) == 0)
        def _():
            bs_ref[...] = jnp.zeros_like(bs_ref)
            bf_ref[...] = jnp.zeros_like(bf_ref)

        dmi = _dot(dm_ref[...], w_ref[...], 1, 1)
        gs, gf = gs_ref[...], gf_ref[...]
        dys_ref[...] = (dmi * gs).astype(BF16)
        dyf_ref[...] = (dmi * gf).astype(BF16)
        dls = dmi * ys_ref[...] * gs * (1.0 - gs)
        dlf = dmi * yf_ref[...] * gf * (1.0 - gf)
        dls_ref[...] = dls.astype(BF16)
        dlf_ref[...] = dlf.astype(BF16)
        bs_ref[0:1, :] += _colsum(dls)
        bf_ref[0:1, :] += _colsum(dlf)

    t = pl.BlockSpec((tm, tn), lambda j, i: (i, j))
    accs = pl.BlockSpec((8, tn), lambda j, i: (0, j))
    return pl.pallas_call(
        body, name="gate_bwd", grid=(D // tn, S // tm),
        in_specs=[pl.BlockSpec((tm, D), lambda j, i: (i, 0)),
                  pl.BlockSpec((tn, D), lambda j, i: (j, 0)), t, t, t, t],
        out_specs=[t, t, t, t, accs, accs],
        out_shape=[jax.ShapeDtypeStruct((S, D), BF16)] * 4 + [jax.ShapeDtypeStruct((8, D), F32)] * 2,
        compiler_params=_params(("parallel", "arbitrary")),
    )(dmix, wo, g_sb, g_fx, y_sb, y_fx)


def _local_step(x, target, ada8, lnp8, bg_sb, bg_fx, bf_pad, wqkv, wf, wgs, wgf, gather, later_weights, pack_early):
    S, D = x.shape
    W = wqkv.shape[1] // 6
    n_pairs = W // PAIR
    n_heads = W // HEAD_DIM
    ts = _pick(S, (256, 128))
    tq = _pick(S, (256, 128))

    u1 = _ln_mod(x, ada8, ts)
    qkv = _mm([(u1, wqkv)], 'nn', BF16, "in_qkv")
    f = _mm([(u1, wf)], 'nn', F32, "in_f")
    g_sb = _mm([(u1, wgs)], 'nn', F32, "in_gsb", bias=bg_sb, act='sigmoid')
    g_fx = _mm([(u1, wgf)], 'nn', F32, "in_gfx", bias=bg_fx, act='sigmoid')
    fc = _fgate_fwd(f, bf_pad, _pick(S, (512, 256, 128)))
    fch = fc[:, :n_heads]
    fcx = jnp.repeat(fch, HEAD_DIM, axis=1)
    nq = S // tq
    fcr = jnp.pad(fch.T.reshape(n_pairs, 2, nq, tq).transpose(0, 2, 1, 3),
                  ((0, 0), (0, 0), (0, 6), (0, 0)))
    o_sb, o_sb32 = _sb_fwd2(qkv, n_pairs, 0, tq)
    o_fx, lse, *zone = _fox_fwd2(qkv, fcx, fcr, n_pairs, 3 * n_pairs, tq, gather)
    wsb, wfx, wo, wfg, wfu, wfd = later_weights(zone[0] if zone else None)
    y_sb =_mm([(o_sb, wsb)], 'nn', F32, "out_sb")
    y_fx = _mm([(o_fx, wfx)], 'nn', F32, "out_fx")
    mix_in = _gate_mix(g_sb, g_fx, y_sb, y_fx, ts)
    mix = _mm([(mix_in, wo)], 'nn', F32, "out_o")
    x1, u2 = _post_attn(x, mix, ada8, lnp8, ts)
    a, b, hin = _swiglu_fwd(u2, wfg, wfu)
    h = _mm([(hin, wfd)], 'nn', F32, "ffn_down")
    dr2, dh, st_loss = _loss_head(x1, h, target, ada8, lnp8, ts)

    da, db = _swiglu_bwd(dh, wfd, a, b)
    g_wfd = _mm([(hin, dh)], 'tn', F32, "g_ffn_down")
    du2 = _mm([(da, wfg), (db, wfu)], 'nt', F32, "d_u2")
    g_wfg = _mm([(u2, da)], 'tn', F32, "g_ffn_gate")
    g_wfu = _mm([(u2, db)], 'tn', F32, "g_ffn_up")
    dr1, dmix, st_mid = _mid_bwd(du2, x1, dr2, mix, x, ada8, lnp8, ts)
    dys, dyf, dls, dlf, gb_sb, gb_fx = _gate_bwd(dmix, wo, g_sb, g_fx, y_sb, y_fx)
    g_wo = _mm([(mix_in, dmix)], 'tn', F32, "g_w_o")
    do_sb = _mm([(dys, wsb)], 'nt', BF16, "d_o_sb")
    do_fx = _mm([(dyf, wfx)], 'nt', BF16, "d_o_fx")
    g_wsb = _mm([(o_sb, dys)], 'tn', F32, "g_sb_out")
    g_wfx = _mm([(o_fx, dyf)], 'tn', F32, "g_fox_out")
    scatter = pack_early(dict(sb=g_wsb, fx=g_wfx, o=g_wo, fg=g_wfg, fu=g_wfu, fd=g_wfd))
    dq_s, dk_s, dv_s = _sb_bwd2(qkv, o_sb32, do_sb, n_pairs, 0, tq)
    dq_f, dk_f, dv_f, dfq, dfk, *zone = _fox_bwd2(qkv, fcx, fcr, o_fx, lse, do_fx, n_pairs, 3 * n_pairs, tq, scatter)
    dfc = dfq[:, ::HEAD_DIM] - dfk[:, :, :2, :].transpose(0, 2, 1, 3).reshape(n_heads, S).T
    dfc = jnp.pad(dfc, ((0, 0), (0, 128 - n_heads)))
    df, gb_f = _fgate_bwd(dfc, f, bf_pad, _pick(S, (512, 256, 128)))
    grads = [dq_s, dk_s, dv_s, dq_f, dk_f, dv_f]
    du1 = _mm([(g, wqkv, W, 0, n) for n, g in enumerate(grads)] + [(df, wf), (dls, wgs), (dlf, wgf)],
              'nt', F32, "d_u1")
    g_wqkv = [_mm([(u1, g)], 'tn', F32, "g_in_%d" % n) for n, g in enumerate(grads)]
    g_wf = _mm([(u1, df)], 'tn', F32, "g_in_f")
    g_wgs = _mm([(u1, dls)], 'tn', F32, "g_in_gsb")
    g_wgf = _mm([(u1, dlf)], 'tn', F32, "g_in_gfx")
    gx, st_first = _first_bwd(du1, x, dr1, ada8, ts)

    wgrads = dict(qkv=g_wqkv, f=g_wf, gs=g_wgs, gf=g_wgf)
    stats = dict(loss=st_loss, mid=st_mid, first=st_first, gb_sb=gb_sb, gb_fx=gb_fx, gb_f=gb_f)
    return gx, wgrads, stats, (scatter, zone[0] if zone else None)


def _position():
    x, y, c = lax.axis_index("x"), lax.axis_index("y"), lax.axis_index("c")
    return x, y, c, 4 * x + 2 * y + c


def _flip(x, y, c, k):
    px = 1 - x if k & 4 else x
    py = 1 - y if k & 2 else y
    pc = 1 - c if k & 1 else c
    return (px, py, pc), 4 * px + 2 * py + pc


def _all_gather_small(v, name):
    r, n = v.shape

    def body(x_ref, out_ref, send_sems, recv_sems, local_sem):
        x, y, c, me = _position()
        mine = pltpu.make_async_copy(x_ref, out_ref.at[me], local_sem)
        mine.start()
        sends = []
        for k in range(1, N_DEV):
            peer, _ = _flip(x, y, c, k)
            cp = pltpu.make_async_remote_copy(
                src_ref=x_ref, dst_ref=out_ref.at[me], send_sem=send_sems.at[k - 1], recv_sem=recv_sems.at[k - 1],
                device_id=peer, device_id_type=MESH)
            cp.start()
            sends.append(cp)
        for k in range(1, N_DEV):
            peer, slot = _flip(x, y, c, k)
            pltpu.make_async_remote_copy(
                src_ref=x_ref, dst_ref=out_ref.at[slot], send_sem=send_sems.at[k - 1], recv_sem=recv_sems.at[k - 1],
                device_id=peer, device_id_type=MESH).wait_recv()
        for cp in sends:
            cp.wait_send()
        mine.wait()

    return pl.pallas_call(
        body, name=name, out_shape=jax.ShapeDtypeStruct((N_DEV, r, n), v.dtype),
        in_specs=[pl.BlockSpec(memory_space=pltpu.VMEM)], out_specs=pl.BlockSpec(memory_space=pltpu.VMEM),
        scratch_shapes=[pltpu.SemaphoreType.DMA((N_DEV - 1,)), pltpu.SemaphoreType.DMA((N_DEV - 1,)),
                        pltpu.SemaphoreType.DMA],
    )(v)


def _all_gather_weights(packed):
    R, C = packed.shape

    def body(x_ref, out_ref, send_sems, recv_sems, local_sem):
        x, y, c, me = _position()
        sibling, sib_slot = _flip(x, y, c, 1)
        mine = pltpu.make_async_copy(x_ref, out_ref.at[me], local_sem)
        mine.start()

        def copy(k, slot, to, src=None):
            return pltpu.make_async_remote_copy(
                src_ref=out_ref.at[slot] if src is None else src, dst_ref=out_ref.at[slot],
                send_sem=send_sems.at[k], recv_sem=recv_sems.at[k], device_id=to, device_id_type=MESH)

        first = [copy(0, me, sibling, src=x_ref)]
        chips = (4, 2, 6)
        for n, k in enumerate(chips):
            peer, _ = _flip(x, y, c, k)
            first.append(copy(1 + n, me, peer, src=x_ref))
        for cp in first:
            cp.start()
        passed = []
        for n, k in enumerate(chips):
            peer, slot = _flip(x, y, c, k)
            copy(1 + n, slot, peer).wait_recv()
            cp = copy(4 + n, slot, sibling)
            cp.start()
            passed.append(cp)
        copy(0, sib_slot, sibling).wait_recv()
        for n, k in enumerate(chips):
            _, slot = _flip(x, y, c, k | 1)
            copy(4 + n, slot, sibling).wait_recv()
        for cp in first + passed:
            cp.wait_send()
        mine.wait()

    return pl.pallas_call(
        body, name="all_gather_weights", out_shape=jax.ShapeDtypeStruct((N_DEV, R, C), packed.dtype),
        in_specs=[pl.BlockSpec(memory_space=pl.ANY)], out_specs=pl.BlockSpec(memory_space=pl.ANY),
        scratch_shapes=[pltpu.SemaphoreType.DMA((7,)), pltpu.SemaphoreType.DMA((7,)), pltpu.SemaphoreType.DMA],
    )(packed)


def _exchange_grads(gpack):
    _, R, C = gpack.shape

    def body(g_ref, out_ref, send_sems, recv_sems, local_sem):
        x, y, c, me = _position()
        mine = pltpu.make_async_copy(g_ref.at[me], out_ref.at[me], local_sem)
        mine.start()
        sends = []
        for k in range(1, N_DEV):
            peer, slot = _flip(x, y, c, k)
            cp = pltpu.make_async_remote_copy(
                src_ref=g_ref.at[slot], dst_ref=out_ref.at[me], send_sem=send_sems.at[k - 1],
                recv_sem=recv_sems.at[k - 1], device_id=peer, device_id_type=MESH)
            cp.start()
            sends.append(cp)
        for k in range(1, N_DEV):
            peer, slot = _flip(x, y, c, k)
            pltpu.make_async_remote_copy(
                src_ref=g_ref.at[slot], dst_ref=out_ref.at[slot], send_sem=send_sems.at[k - 1],
                recv_sem=recv_sems.at[k - 1], device_id=peer, device_id_type=MESH).wait_recv()
        for cp in sends:
            cp.wait_send()
        mine.wait()

    return pl.pallas_call(
        body, name="exchange_grads", out_shape=jax.ShapeDtypeStruct((N_DEV, R, C), gpack.dtype),
        in_specs=[pl.BlockSpec(memory_space=pl.ANY)], out_specs=pl.BlockSpec(memory_space=pl.ANY),
        scratch_shapes=[pltpu.SemaphoreType.DMA((N_DEV - 1,)), pltpu.SemaphoreType.DMA((N_DEV - 1,)),
                        pltpu.SemaphoreType.DMA],
    )(gpack)


def _own_slot(land, own):
    me = 4 * lax.axis_index("x") + 2 * lax.axis_index("y") + lax.axis_index("c")
    return lax.dynamic_update_slice(land, own[None], (me, 0, 0))


def _sum_slots(recv, name, tr):
    n, R, C = recv.shape

    def body(r_ref, o_ref):
        acc = r_ref[0].astype(F32)
        for s in range(1, n):
            acc = acc + r_ref[s].astype(F32)
        o_ref[...] = acc

    return pl.pallas_call(
        body, name=name, grid=(R // tr,), in_specs=[pl.BlockSpec((n, tr, C), lambda i: (0, i, 0))],
        out_specs=pl.BlockSpec((tr, C), lambda i: (i, 0)), out_shape=jax.ShapeDtypeStruct((R, C), F32),
        compiler_params=_params(("parallel",)),
    )(recv)


def _sum_stats(st_all, loss_row):
    n, r, D = st_all.shape

    def body(s_ref, o_ref, l_ref):
        acc = s_ref[0]
        for d in range(1, n):
            acc = acc + s_ref[d]
        o_ref[...] = acc
        l_ref[...] = jnp.zeros((8, 128), F32) + jnp.sum(acc[loss_row:loss_row + 1, :], axis=-1, keepdims=True)

    return pl.pallas_call(
        body, name="sum_stats", out_shape=[jax.ShapeDtypeStruct((r, D), F32), jax.ShapeDtypeStruct((8, 128), F32)],
    )(st_all)


def _adamw(w, g, m, v, name):
    R, C = w.shape
    tr = _pick(R, (256, 176, 128, 64, 32, 16, 8))
    c1 = 1.0 / (1.0 - ADAM_B1 ** ADAM_STEP)
    c2 = 1.0 / (1.0 - ADAM_B2 ** ADAM_STEP)

    def body(w_ref, g_ref, m_ref, v_ref, d_ref, nm_ref, nv_ref):
        gv = g_ref[...]
        nm = ADAM_B1 * m_ref[...] + (1.0 - ADAM_B1) * gv
        nv = ADAM_B2 * v_ref[...] + (1.0 - ADAM_B2) * (gv * gv)
        nm_ref[...] = nm
        nv_ref[...] = nv
        d_ref[...] = -ADAM_LR * ((nm * c1) / (jnp.sqrt(nv * c2) + ADAM_EPS) + ADAM_WD * w_ref[...])

    spec = pl.BlockSpec((tr, C), lambda i: (i, 0))
    return pl.pallas_call(
        body, name=name, grid=(R // tr,), in_specs=[spec] * 4, out_specs=[spec] * 3,
        out_shape=[jax.ShapeDtypeStruct((R, C), F32)] * 3, compiler_params=_params(("parallel",)),
    )(w, g, m, v)


def _round16(n):
    return -(-n // 16) * 16


def _pack_layout(D, in_cols, ff, W):
    parts = [("in", D * (in_cols // N_DEV) // D), ("fg", ff // N_DEV), ("fu", ff // N_DEV),
             ("sb", W * (D // N_DEV) // D), ("fx", W * (D // N_DEV) // D), ("o", D // N_DEV), ("fd", ff // N_DEV)]
    layout, off = {}, 0
    for nm, rows in parts:
        layout[nm] = (off, rows)
        off += _round16(rows)
    return layout, off


def _rows_of(a, D, rows):
    a = a.reshape(rows, D)
    return jnp.pad(a, ((0, _round16(rows) - rows), (0, 0)))


def _cols_to_dest(g, D):
    K, N = g.shape
    n = N // N_DEV
    return g.reshape(K, N_DEV, n).transpose(1, 0, 2).reshape(N_DEV, K * n // D, D)


def _cols_from_src(blocks, K, n):
    return blocks.reshape(N_DEV, K, n).transpose(1, 0, 2).reshape(K, N_DEV * n)


def _pad_rows16(a):
    rows = a.shape[1]
    return jnp.pad(a, ((0, 0), (0, _round16(rows) - rows), (0, 0)))


def kernel(x, c, w_ada, b_ada, w_in, b_gate, b_forget, w_sb_out, w_fox_out, w_o, ln1_g, ln1_b, w_ffn_gate, w_ffn_up, w_ffn_down, ln2_g, ln2_b, loss_target, m_w_ada, m_b_ada, m_w_in, m_b_gate, m_b_forget, m_w_sb_out, m_w_fox_out, m_w_o, m_ln1_g, m_ln1_b, m_w_ffn_gate, m_w_ffn_up, m_w_ffn_down, m_ln2_g, m_ln2_b, v_w_ada, v_b_ada, v_w_in, v_b_gate, v_b_forget, v_w_sb_out, v_w_fox_out, v_w_o, v_ln1_g, v_ln1_b, v_w_ffn_gate, v_w_ffn_up, v_w_ffn_down, v_ln2_g, v_ln2_b):
    S, D = x.shape[1], x.shape[2]
    W = w_sb_out.shape[1]
    n_heads = b_forget.shape[1]
    ff = w_ffn_down.shape[1] * N_DEV
    in_loc = w_in.shape[2]
    in_cols = in_loc * N_DEV
    ada_loc = w_ada.shape[2]
    n_cond = ada_loc * N_DEV // D
    assert w_ada.shape[0] == 1 and n_cond == 6 and in_cols == 6 * W + n_heads + 2 * D and n_heads <= 128
    me = 4 * lax.axis_index("x") + 2 * lax.axis_index("y") + lax.axis_index("c")

    c_all = _all_gather_small(c, "gather_c").reshape(N_DEV, D)
    c16 = jnp.pad(c_all, ((0, 16 - N_DEV), (0, 0)))
    b_cols = lax.dynamic_slice(b_ada, (0, me * ada_loc), (1, ada_loc))
    ada_cols = _mm([(c16, w_ada[0])], 'nn', F32, "ada_fwd", bias=b_cols, silu_a=True)[:N_DEV]
    ada_all = _all_gather_small(ada_cols, "gather_ada")
    ada_me = lax.dynamic_index_in_dim(ada_all, me, axis=1, keepdims=False)
    ada8 = jnp.pad(ada_me.reshape(n_cond, D), ((0, 8 - n_cond), (0, 0)))
    lnp8 = jnp.concatenate([ln1_g, ln1_b, ln2_g, ln2_b, jnp.zeros((4, D), F32)], axis=0)

    layout, R = _pack_layout(D, in_cols, ff, W)
    shards = dict(**{"in": w_in[0]}, fg=w_ffn_gate[0], fu=w_ffn_up[0], sb=w_sb_out[0], fx=w_fox_out[0], o=w_o[0],
                  fd=w_ffn_down[0])
    later = [nm for nm in layout if nm != "in"]
    r_in = _round16(layout["in"][1])
    first = _rows_of(shards["in"].astype(BF16), D, layout["in"][1])
    rest = jnp.concatenate([_rows_of(shards[nm].astype(BF16), D, layout[nm][1]) for nm in later], axis=0)
    gathered_in = _all_gather_weights(first)

    w_in_full = _cols_from_src(gathered_in[:, :layout["in"][1], :], D, in_loc)
    wqkv = w_in_full[:, :6 * W]
    wf = jnp.pad(w_in_full[:, 6 * W:6 * W + n_heads], ((0, 0), (0, 128 - n_heads)))
    wgs = w_in_full[:, 6 * W + n_heads:6 * W + n_heads + D]
    wgf = w_in_full[:, 6 * W + n_heads + D:]
    bf_pad = jnp.pad(b_forget, ((0, 0), (0, 128 - n_heads)))

    def later_weights(zone):
        gathered = _own_slot(zone, rest)

        def part(nm):
            off, rows = layout[nm]
            return gathered[:, off - r_in:off - r_in + rows, :]

        return (_cols_from_src(part("sb"), W, D // N_DEV), _cols_from_src(part("fx"), W, D // N_DEV),
                part("o").reshape(D, D), _cols_from_src(part("fg"), D, ff // N_DEV),
                _cols_from_src(part("fu"), D, ff // N_DEV), part("fd").reshape(ff, D))

    def pack_early(g):
        dest = {"fg": _cols_to_dest(g["fg"], D), "fu": _cols_to_dest(g["fu"], D), "sb": _cols_to_dest(g["sb"], D),
                "fx": _cols_to_dest(g["fx"], D), "o": g["o"].reshape(N_DEV, D // N_DEV, D),
                "fd": g["fd"].reshape(N_DEV, ff // N_DEV, D)}
        return jnp.concatenate([_pad_rows16(dest[nm].astype(BF16)) for nm in later], axis=1), True

    gx, wg, st, ((pack, _), land) = _local_step(
        x[0], loss_target[0], ada8, lnp8, b_gate[:, :D], b_gate[:, D:], bf_pad, wqkv, wf, wgs, wgf,
        (rest, False), later_weights, pack_early)

    own = lax.dynamic_index_in_dim(pack, me, axis=0, keepdims=False)
    gsum_rest = _sum_slots(_own_slot(land, own), "sum_grads_rest", _pick(R - r_in, (512, 656, 256, 128, 64, 16)))
    g_in = jnp.concatenate(wg["qkv"] + [wg["f"][:, :n_heads], wg["gs"], wg["gf"]], axis=1)
    recv_in = _exchange_grads(_pad_rows16(_cols_to_dest(g_in, D).astype(BF16)))
    gsum_in = _sum_slots(recv_in, "sum_grads_in", _pick(r_in, (512, 656, 256, 128, 64, 16)))

    def gshard(nm, shape):
        off, rows = layout[nm]
        if nm == "in":
            return gsum_in[:rows].reshape(shape)
        return gsum_rest[off - r_in:off - r_in + rows].reshape(shape)

    zrow = jnp.zeros((1, D), F32)
    gb_f_row = jnp.pad(st["gb_f"][0:1], ((0, 0), (0, D - 128)))
    stats16 = jnp.concatenate([
        st["first"][1:2], st["first"][0:1], st["mid"][4:5], st["mid"][1:2], st["mid"][0:1], st["loss"][3:4],
        st["mid"][2:3], st["mid"][3:4], st["loss"][1:2], st["loss"][2:3], st["gb_sb"][0:1], st["gb_fx"][0:1],
        st["loss"][0:1], gb_f_row, zrow, zrow], axis=0)
    st_all = _all_gather_small(stats16, "gather_stats")
    st_sum, loss_blk = _sum_stats(st_all, 12)
    loss = loss_blk[0, 0]

    d_ada_all = st_all[:, :n_cond, :].reshape(N_DEV, n_cond * D)
    d_cols = lax.dynamic_slice(d_ada_all, (0, me * ada_loc), (N_DEV, ada_loc))
    d16 = jnp.pad(d_cols, ((0, 16 - N_DEV), (0, 0)))
    g_w_ada = _mm([(c16, d16)], 'tn', F32, "ada_wgrad", silu_a=True)

    small_w = jnp.concatenate([b_ada.reshape(n_cond, D), ln1_g, ln1_b, ln2_g, ln2_b, b_gate.reshape(2, D), zrow,
                               jnp.pad(b_forget, ((0, 0), (0, D - n_heads))), zrow, zrow], axis=0)
    small_m = jnp.concatenate([m_b_ada.reshape(n_cond, D), m_ln1_g, m_ln1_b, m_ln2_g, m_ln2_b, m_b_gate.reshape(2, D),
                               zrow, jnp.pad(m_b_forget, ((0, 0), (0, D - n_heads))), zrow, zrow], axis=0)
    small_v = jnp.concatenate([v_b_ada.reshape(n_cond, D), v_ln1_g, v_ln1_b, v_ln2_g, v_ln2_b, v_b_gate.reshape(2, D),
                               zrow, jnp.pad(v_b_forget, ((0, 0), (0, D - n_heads))), zrow, zrow], axis=0)
    sm = _adamw(small_w, st_sum, small_m, small_v, "adamw_small")

    def small(a, nm):
        if nm == "b_ada":
            return a[0:n_cond].reshape(1, n_cond * D)
        if nm == "b_gate":
            return a[10:12].reshape(1, 2 * D)
        if nm == "b_forget":
            return a[13:14, :n_heads]
        row = {"ln1_g": 6, "ln1_b": 7, "ln2_g": 8, "ln2_b": 9}[nm]
        return a[row:row + 1]

    big = {
        "w_ada": (w_ada[0], g_w_ada, m_w_ada[0], v_w_ada[0]),
        "w_in": (w_in[0], gshard("in", w_in.shape[1:]), m_w_in[0], v_w_in[0]),
        "w_sb_out": (w_sb_out[0], gshard("sb", w_sb_out.shape[1:]), m_w_sb_out[0], v_w_sb_out[0]),
        "w_fox_out": (w_fox_out[0], gshard("fx", w_fox_out.shape[1:]), m_w_fox_out[0], v_w_fox_out[0]),
        "w_o": (w_o[0], gshard("o", w_o.shape[1:]), m_w_o[0], v_w_o[0]),
        "w_ffn_gate": (w_ffn_gate[0], gshard("fg", w_ffn_gate.shape[1:]), m_w_ffn_gate[0], v_w_ffn_gate[0]),
        "w_ffn_up": (w_ffn_up[0], gshard("fu", w_ffn_up.shape[1:]), m_w_ffn_up[0], v_w_ffn_up[0]),
        "w_ffn_down": (w_ffn_down[0], gshard("fd", w_ffn_down.shape[1:]), m_w_ffn_down[0], v_w_ffn_down[0]),
    }
    order = ["w_ada", "b_ada", "w_in", "b_gate", "b_forget", "w_sb_out", "w_fox_out", "w_o", "ln1_g", "ln1_b",
             "w_ffn_gate", "w_ffn_up", "w_ffn_down", "ln2_g", "ln2_b"]
    grads, deltas, new_ms, new_vs = [], [], [], []
    for nm in order:
        if nm in big:
            w, g, m, v = big[nm]
            d, nm_, nv_ = _adamw(w, g, m, v, "adamw_" + nm)
            grads.append(g[None])
            deltas.append(d[None])
            new_ms.append(nm_[None])
            new_vs.append(nv_[None])
        else:
            grads.append(small(st_sum, nm))
            deltas.append(small(sm[0], nm))
            new_ms.append(small(sm[1], nm))
            new_vs.append(small(sm[2], nm))
    return (loss, gx[None], *grads, *deltas, *new_ms, *new_vs)
```

```python
import functools

import jax
import jax.numpy as jnp
import numpy as np
from jax import lax
from jax.experimental import pallas as pl
from jax.experimental.pallas import tpu as pltpu

F32 = jnp.float32
BF16 = jnp.bfloat16

HEAD_DIM = 64
PAIR = 2 * HEAD_DIM
LN_EPS = 1e-5
ALPHA = 2.0 ** 0.25
ADAM_LR, ADAM_B1, ADAM_B2, ADAM_EPS, ADAM_WD, ADAM_STEP = 0.001, 0.9, 0.999, 1e-08, 0.01, 10
N_DEV = 8
VMEM_LIMIT = 56 * 1024 * 1024
MESH = pl.DeviceIdType.MESH


def _dot(a, b, ca=1, cb=0):
    return lax.dot_general(a, b, (((ca,), (cb,)), ((), ())), preferred_element_type=F32)


def _pick(n, cands):
    for t in cands:
        if n % t == 0:
            return t
    return n


def _params(sem):
    return pltpu.CompilerParams(dimension_semantics=sem, vmem_limit_bytes=VMEM_LIMIT)


MM_BLOCK_BYTES = 40 * 1024 * 1024
LANES = 128


def _divisors(n, cap):
    ds = [d for d in range(LANES, min(n, cap) + 1, LANES) if n % d == 0]
    return sorted(ds, reverse=True) or [n]


def _mm_tiles(M, N, a_row_bytes, b_row_bytes, out_itemsize):
    best = None
    for tm in _divisors(M, 1024):
        for tn in _divisors(N, 2048):
            need = 2 * (tm * a_row_bytes + tn * b_row_bytes + tm * tn * out_itemsize) + tm * tn * 4
            if need <= MM_BLOCK_BYTES and (best is None or (tm * tn, tm) > (best[0] * best[1], best[0])):
                best = (tm, tn)
    assert best is not None, (M, N, a_row_bytes, b_row_bytes)
    return best


def _mm(pairs, mode, out_dtype, name, bias=None, act=None, silu_a=False, exchange=None):
    norm = []
    for p in pairs:
        a, b = p[0], p[1]
        kdim_a = a.shape[0] if mode == 'tn' else a.shape[1]
        K, ka, kb = (p[2], p[3], p[4]) if len(p) > 2 else (kdim_a, 0, 0)
        norm.append((a, b, K, ka, kb))
    a0, b0 = norm[0][0], norm[0][1]
    M = a0.shape[1] if mode == 'tn' else a0.shape[0]
    N = b0.shape[0] if mode == 'nt' else b0.shape[1]
    tm, tn = _mm_tiles(M, N, sum(K * a.dtype.itemsize for a, _, K, _, _ in norm),
                       sum(K * b.dtype.itemsize for _, b, K, _, _ in norm), jnp.dtype(out_dtype).itemsize)
    n_pairs = len(norm)

    in_specs, args = [], []
    for a, b, K, ka, kb in norm:
        if mode == 'tn':
            in_specs.append(pl.BlockSpec((K, tm), lambda i, j, ka=ka: (ka, i)))
        else:
            in_specs.append(pl.BlockSpec((tm, K), lambda i, j, ka=ka: (i, ka)))
        if mode == 'nt':
            in_specs.append(pl.BlockSpec((tn, K), lambda i, j, kb=kb: (j, kb)))
        else:
            in_specs.append(pl.BlockSpec((K, tn), lambda i, j, kb=kb: (kb, j)))
        args += [a, b]
    if bias is not None:
        in_specs.append(pl.BlockSpec((1, tn), lambda i, j: (0, j)))
        args.append(bias)

    ca = 0 if mode == 'tn' else 1
    cb = 1 if mode == 'nt' else 0

    def body(*refs):
        o_ref = refs[-1]
        acc = None
        for p in range(n_pairs):
            av = refs[2 * p][...]
            if silu_a:
                av = av / (1.0 + jnp.exp(-av))
            d = _dot(av.astype(BF16), refs[2 * p + 1][...].astype(BF16), ca, cb)
            acc = d if acc is None else acc + d
        if bias is not None:
            acc = acc + refs[2 * n_pairs][...]
        if act == 'sigmoid':
            acc = 1.0 / (1.0 + jnp.exp(-acc))
        o_ref[...] = acc.astype(out_dtype)

    out_spec = pl.BlockSpec((tm, tn), lambda i, j: (i, j))
    out_shape = jax.ShapeDtypeStruct((M, N), out_dtype)
    if exchange is not None:
        return _call_carrying(body, exchange, name, (M // tm, N // tn), in_specs, [out_spec], [out_shape], [], args)
    return pl.pallas_call(
        body, name=name, grid=(M // tm, N // tn), in_specs=in_specs, out_specs=out_spec, out_shape=out_shape,
        compiler_params=_params(("parallel", "parallel")),
    )(*args)


def _rows_call(body, name, row_ins, vec_ins, row_outs, acc_outs, ts):
    S = row_ins[0].shape[0]
    in_specs = [pl.BlockSpec((ts, a.shape[1]), lambda i: (i, 0)) for a in row_ins]
    in_specs += [pl.BlockSpec(a.shape, lambda i: (0, 0)) for a in vec_ins]
    out_specs = [pl.BlockSpec((ts, c), lambda i: (i, 0)) for c, _ in row_outs]
    out_specs += [pl.BlockSpec(s, lambda i: (0, 0)) for s in acc_outs]
    out_shape = [jax.ShapeDtypeStruct((S, c), dt) for c, dt in row_outs]
    out_shape += [jax.ShapeDtypeStruct(s, F32) for s in acc_outs]
    return pl.pallas_call(
        body, name=name, grid=(S // ts,), in_specs=in_specs, out_specs=out_specs, out_shape=out_shape,
        compiler_params=_params(("arbitrary",)),
    )(*row_ins, *vec_ins)


def _ln_stats(v):
    mu = jnp.mean(v, axis=-1, keepdims=True)
    d = v - mu
    var = jnp.mean(d * d, axis=-1, keepdims=True)
    rstd = lax.rsqrt(var + LN_EPS)
    return d * rstd, rstd


def _ln_bwd(dxhat, xhat, rstd):
    m1 = jnp.mean(dxhat, axis=-1, keepdims=True)
    m2 = jnp.mean(dxhat * xhat, axis=-1, keepdims=True)
    return rstd * (dxhat - m1 - xhat * m2)


def _colsum(v):
    return jnp.sum(v, axis=0, keepdims=True)


def _ln_mod(x, ada8, ts):
    D = x.shape[1]

    def body(x_ref, v_ref, u_ref):
        xhat, _ = _ln_stats(x_ref[...])
        u_ref[...] = (xhat * (1.0 + v_ref[1:2, :]) + v_ref[0:1, :]).astype(BF16)

    return _rows_call(body, "ln_mod", [x], [ada8], [(D, BF16)], [], ts)[0]


def _gate_mix(g_sb, g_fx, y_sb, y_fx, ts):
    D = y_sb.shape[1]

    def body(gs, gf, ys, yf, o_ref):
        o_ref[...] = (gs[...] * ys[...] + gf[...] * yf[...]).astype(BF16)

    return _rows_call(body, "gate_mix", [g_sb, g_fx, y_sb, y_fx], [], [(D, BF16)], [], ts)[0]


def _post_attn(x, mix, ada8, lnp8, ts):
    D = x.shape[1]

    def body(x_ref, mix_ref, v_ref, p_ref, x1_ref, u2_ref):
        r1 = ALPHA * x_ref[...] + v_ref[2:3, :] * mix_ref[...]
        xhat, _ = _ln_stats(r1)
        x1 = xhat * p_ref[0:1, :] + p_ref[1:2, :]
        x1_ref[...] = x1
        xh1, _ = _ln_stats(x1)
        u2_ref[...] = (xh1 * (1.0 + v_ref[4:5, :]) + v_ref[3:4, :]).astype(BF16)

    return _rows_call(body, "post_attn", [x, mix], [ada8, lnp8], [(D, F32), (D, BF16)], [], ts)


def _loss_head(x1, h, target, ada8, lnp8, ts):
    D = x1.shape[1]

    def body(x1_ref, h_ref, t_ref, v_ref, p_ref, dr2_ref, dh_ref, st_ref):
        @pl.when(pl.program_id(0) == 0)
        def _():
            st_ref[...] = jnp.zeros_like(st_ref)

        hv = h_ref[...]
        g2 = v_ref[5:6, :]
        r2 = ALPHA * x1_ref[...] + g2 * hv
        xhat, rstd = _ln_stats(r2)
        y = xhat * p_ref[2:3, :] + p_ref[3:4, :]
        err = y - t_ref[...]
        dy = err * (1.0 / D)
        dr2 = _ln_bwd(dy * p_ref[2:3, :], xhat, rstd)
        dr2_ref[...] = dr2
        dh_ref[...] = (dr2 * g2).astype(BF16)
        st_ref[0:1, :] += _colsum(err * err) * (0.5 / D)
        st_ref[1:2, :] += _colsum(dy * xhat)
        st_ref[2:3, :] += _colsum(dy)
        st_ref[3:4, :] += _colsum(dr2 * hv)

    return _rows_call(body, "loss_head", [x1, h, target], [ada8, lnp8], [(D, F32), (D, BF16)], [(8, D)], ts)


def _mid_bwd(du2, x1, dr2, mix, x, ada8, lnp8, ts):
    D = x.shape[1]

    def body(du2_ref, x1_ref, dr2_ref, mix_ref, x_ref, v_ref, p_ref, dr1_ref, dmix_ref, st_ref):
        @pl.when(pl.program_id(0) == 0)
        def _():
            st_ref[...] = jnp.zeros_like(st_ref)

        du2v = du2_ref[...]
        xh1, rstd1 = _ln_stats(x1_ref[...])
        dx1 = ALPHA * dr2_ref[...] + _ln_bwd(du2v * (1.0 + v_ref[4:5, :]), xh1, rstd1)
        mixv = mix_ref[...]
        g1 = v_ref[2:3, :]
        r1 = ALPHA * x_ref[...] + g1 * mixv
        xhr, rstdr = _ln_stats(r1)
        dr1 = _ln_bwd(dx1 * p_ref[0:1, :], xhr, rstdr)
        dr1_ref[...] = dr1
        dmix_ref[...] = (dr1 * g1).astype(BF16)
        st_ref[0:1, :] += _colsum(du2v * xh1)
        st_ref[1:2, :] += _colsum(du2v)
        st_ref[2:3, :] += _colsum(dx1 * xhr)
        st_ref[3:4, :] += _colsum(dx1)
        st_ref[4:5, :] += _colsum(dr1 * mixv)

    return _rows_call(body, "mid_bwd", [du2, x1, dr2, mix, x], [ada8, lnp8], [(D, F32), (D, BF16)], [(8, D)], ts)


def _first_bwd(du1, x, dr1, ada8, ts):
    D = x.shape[1]

    def body(du1_ref, x_ref, dr1_ref, v_ref, gx_ref, st_ref):
        @pl.when(pl.program_id(0) == 0)
        def _():
            st_ref[...] = jnp.zeros_like(st_ref)

        du1v = du1_ref[...]
        xh0, rstd0 = _ln_stats(x_ref[...])
        gx_ref[...] = ALPHA * dr1_ref[...] + _ln_bwd(du1v * (1.0 + v_ref[1:2, :]), xh0, rstd0)
        st_ref[0:1, :] += _colsum(du1v * xh0)
        st_ref[1:2, :] += _colsum(du1v)

    return _rows_call(body, "first_bwd", [du1, x, dr1], [ada8], [(D, F32)], [(8, D)], ts)


def _split3(v):
    hi = v.astype(BF16)
    r = v - hi.astype(F32)
    mid = r.astype(BF16)
    lo = (r - mid.astype(F32)).astype(BF16)
    return hi, mid, lo


def _fgate_fwd(f, bf_pad, tb):
    S = f.shape[0]

    def body(f_ref, b_ref, fc_ref, carry):
        @pl.when(pl.program_id(0) == 0)
        def _():
            carry[...] = jnp.zeros_like(carry)

        z = f_ref[...] + b_ref[...]
        ls = jnp.minimum(z, 0.0) - jnp.log(1.0 + jnp.exp(-jnp.abs(z)))
        r = lax.broadcasted_iota(jnp.int32, (tb, tb), 0)
        c = lax.broadcasted_iota(jnp.int32, (tb, tb), 1)
        tri = (c <= r).astype(BF16)
        hi, mid, lo = _split3(ls)
        cs = _dot(tri, hi) + _dot(tri, mid) + _dot(tri, lo) + carry[...]
        fc_ref[...] = cs
        carry[...] = cs[tb - 1:tb, :]

    return pl.pallas_call(
        body, name="fgate_fwd", grid=(S // tb,),
        in_specs=[pl.BlockSpec((tb, 128), lambda i: (i, 0)), pl.BlockSpec((1, 128), lambda i: (0, 0))],
        out_specs=pl.BlockSpec((tb, 128), lambda i: (i, 0)),
        out_shape=jax.ShapeDtypeStruct((S, 128), F32),
        scratch_shapes=[pltpu.VMEM((1, 128), F32)],
        compiler_params=_params(("arbitrary",)),
    )(f, bf_pad)


def _fgate_bwd(dfc, f, bf_pad, tb):
    S = f.shape[0]
    nb = S // tb

    def body(d_ref, f_ref, b_ref, df_ref, gb_ref, carry):
        @pl.when(pl.program_id(0) == 0)
        def _():
            carry[...] = jnp.zeros_like(carry)
            gb_ref[...] = jnp.zeros_like(gb_ref)

        r = lax.broadcasted_iota(jnp.int32, (tb, tb), 0)
        c = lax.broadcasted_iota(jnp.int32, (tb, tb), 1)
        tri = (c >= r).astype(BF16)
        hi, mid, lo = _split3(d_ref[...])
        rs = _dot(tri, hi) + _dot(tri, mid) + _dot(tri, lo) + carry[...]
        carry[...] = rs[0:1, :]
        z = f_ref[...] + b_ref[...]
        df = rs * (1.0 / (1.0 + jnp.exp(z)))
        df_ref[...] = df
        gb_ref[0:1, :] += _colsum(df)

    return pl.pallas_call(
        body, name="fgate_bwd", grid=(nb,),
        in_specs=[pl.BlockSpec((tb, 128), lambda i: (nb - 1 - i, 0)),
                  pl.BlockSpec((tb, 128), lambda i: (nb - 1 - i, 0)),
                  pl.BlockSpec((1, 128), lambda i: (0, 0))],
        out_specs=[pl.BlockSpec((tb, 128), lambda i: (nb - 1 - i, 0)), pl.BlockSpec((8, 128), lambda i: (0, 0))],
        out_shape=[jax.ShapeDtypeStruct((S, 128), F32), jax.ShapeDtypeStruct((8, 128), F32)],
        scratch_shapes=[pltpu.VMEM((1, 128), F32)],
        compiler_params=_params(("arbitrary",)),
    )(dfc, f, bf_pad)


def _split2(v):
    hi = v.astype(BF16)
    lo = (v - hi.astype(F32)).astype(BF16)
    return hi, lo


def _head_masks():
    lane = lax.broadcasted_iota(jnp.int32, (1, PAIR), 1)
    m0 = lane < HEAD_DIM
    return m0, jnp.logical_not(m0)


def _sel(mask, v):
    return jnp.where(mask, v, jnp.zeros_like(v))


def _softplus(z):
    return jnp.maximum(z, 0.0) + jnp.log(1.0 + jnp.exp(-jnp.abs(z)))


def _qkv_specs(S, tq, n_pairs, base):
    return [pl.BlockSpec((tq, PAIR), lambda p, i: (i, base + p)),
            pl.BlockSpec((S, PAIR), lambda p, i: (0, base + n_pairs + p)),
            pl.BlockSpec((S, PAIR), lambda p, i: (0, base + 2 * n_pairs + p))]


def _sb_fwd(qkv, n_pairs, base, tq):
    S = qkv.shape[0]
    tk = tq
    scale = HEAD_DIM ** -0.5

    def body(q_ref, k_ref, v_ref, o_ref, t_ref, acc_ref):
        i = pl.program_id(1)
        masks = _head_masks()
        q2 = q_ref[...]
        qm = [_sel(m, q2) for m in masks]
        rowpos = i * tq + lax.broadcasted_iota(jnp.int32, (tq, tk), 0)
        colin = lax.broadcasted_iota(jnp.int32, (tq, tk), 1)
        upper = (lax.broadcasted_iota(jnp.int32, (tk, tk), 0) > lax.broadcasted_iota(jnp.int32, (tk, tk), 1)).astype(BF16)
        acc_ref[...] = jnp.zeros_like(acc_ref)

        def step(jj, carry):
            j = i - jj
            off = pl.multiple_of(j * tk, tk)
            k2 = k_ref[pl.ds(off, tk), :]
            v2 = v_ref[pl.ds(off, tk), :]
            mask = (j * tk + colin) < rowpos
            out = None
            new = []
            for h in heads:
                z = _dot(qm[h], k2, 1, 1) * scale
                sp = _softplus(z)
                lg = jnp.where(mask, -sp, 0.0)
                hi, lo = _split2(lg)
                suf = _dot(hi, upper) + _dot(lo, upper)
                a = jnp.where(mask, jnp.exp(z - sp + suf + carry[h]), 0.0)
                d = _dot(a.astype(BF16), _sel(masks[h], v2))
                out = d if out is None else out + d
                new.append(carry[h] + jnp.sum(lg, axis=-1, keepdims=True))
            acc_ref[...] += out
            return tuple(new)

        zero = jnp.zeros((tq, 1), F32)
        r0, r1 = lax.fori_loop(0, i + 1, step, (zero, zero))
        o_ref[...] = acc_ref[...].astype(BF16)
        t_ref[...] = jnp.where(masks[0], r0, r1)

    W = n_pairs * PAIR
    return pl.pallas_call(
        body, name="sb_fwd", grid=(n_pairs, S // tq), in_specs=_qkv_specs(S, tq, n_pairs, base),
        out_specs=[pl.BlockSpec((tq, PAIR), lambda p, i: (i, p)), pl.BlockSpec((tq, PAIR), lambda p, i: (i, p))],
        out_shape=[jax.ShapeDtypeStruct((S, W), BF16), jax.ShapeDtypeStruct((S, W), F32)],
        scratch_shapes=[pltpu.VMEM((tq, PAIR), F32)],
        compiler_params=_params(("parallel", "arbitrary")),
    )(qkv, qkv, qkv)


def _sb_bwd(qkv, tot, do, n_pairs, base, tq):
    S = qkv.shape[0]
    tk = tq
    nq = S // tq
    scale = HEAD_DIM ** -0.5

    def body(q_ref, k_ref, v_ref, t_ref, do_ref, dq_ref, dk_ref, dv_ref, dq_acc, dk_acc, dv_acc):
        i = pl.program_id(1)
        masks = _head_masks()

        @pl.when(i == 0)
        def _():
            dk_acc[...] = jnp.zeros_like(dk_acc)
            dv_acc[...] = jnp.zeros_like(dv_acc)

        q2 = q_ref[...]
        do2 = do_ref[...]
        qm = [_sel(m, q2) for m in masks]
        dom = [_sel(m, do2) for m in masks]
        t2 = t_ref[...]
        tot_h = [t2[:, 0:1], t2[:, HEAD_DIM:HEAD_DIM + 1]]
        rowpos = i * tq + lax.broadcasted_iota(jnp.int32, (tq, tk), 0)
        colin = lax.broadcasted_iota(jnp.int32, (tq, tk), 1)
        r_i = lax.broadcasted_iota(jnp.int32, (tk, tk), 0)
        c_i = lax.broadcasted_iota(jnp.int32, (tk, tk), 1)
        upper = (r_i > c_i).astype(BF16)
        lower = (r_i < c_i).astype(BF16)
        dq_acc[...] = jnp.zeros_like(dq_acc)

        def step(j, carry):
            off = pl.multiple_of(j * tk, tk)
            k2 = k_ref[pl.ds(off, tk), :]
            v2 = v_ref[pl.ds(off, tk), :]
            mask = (j * tk + colin) < rowpos
            dq = None
            dk = None
            dv = None
            new = []
            for h in heads:
                cum_l, cum_g = carry[2 * h], carry[2 * h + 1]
                z = _dot(qm[h], k2, 1, 1) * scale
                sp = _softplus(z)
                lg = jnp.where(mask, -sp, 0.0)
                hi, lo = _split2(lg)
                suf = _dot(hi, upper) + _dot(lo, upper)
                row_l = jnp.sum(lg, axis=-1, keepdims=True)
                later = tot_h[h] - cum_l - row_l
                a = jnp.where(mask, jnp.exp(z - sp + suf + later), 0.0)
                da = _dot(dom[h], v2, 1, 1)
                g = da * a
                ghi, glo = _split2(g)
                pre = _dot(ghi, lower) + _dot(glo, lower) + cum_g
                one_m_beta = jnp.exp(-sp)
                dz = jnp.where(mask, g * one_m_beta - (1.0 - one_m_beta) * pre, 0.0)
                dzb = (dz * scale).astype(BF16)
                d1 = _dot(dzb, _sel(masks[h], k2))
                d2 = _dot(dzb, qm[h], 0, 0)
                d3 = _dot(a.astype(BF16), dom[h], 0, 0)
                dq = d1 if dq is None else dq + d1
                dk = d2 if dk is None else dk + d2
                dv = d3 if dv is None else dv + d3
                new += [cum_l + row_l, cum_g + jnp.sum(g, axis=-1, keepdims=True)]
            dq_acc[...] += dq
            dk_acc[pl.ds(off, tk), :] += dk
            dv_acc[pl.ds(off, tk), :] += dv
            return tuple(new)

        zero = jnp.zeros((tq, 1), F32)
        lax.fori_loop(0, i + 1, step, (zero, zero, zero, zero))
        dq_ref[...] = dq_acc[...].astype(BF16)

        @pl.when(i == nq - 1)
        def _():
            dk_ref[...] = dk_acc[...].astype(BF16)
            dv_ref[...] = dv_acc[...].astype(BF16)

    W = n_pairs * PAIR
    in_specs = _qkv_specs(S, tq, n_pairs, base) + [
        pl.BlockSpec((tq, PAIR), lambda p, i: (i, p)),
        pl.BlockSpec((tq, PAIR), lambda p, i: (i, p))]
    out_specs = [pl.BlockSpec((tq, PAIR), lambda p, i: (i, p)),
                 pl.BlockSpec((S, PAIR), lambda p, i: (0, p)),
                 pl.BlockSpec((S, PAIR), lambda p, i: (0, p))]
    dq, dk, dv = pl.pallas_call(
        body, name="sb_bwd", grid=(n_pairs, nq), in_specs=in_specs, out_specs=out_specs,
        out_shape=[jax.ShapeDtypeStruct((S, W), BF16)] * 3,
        scratch_shapes=[pltpu.VMEM((tq, PAIR), F32), pltpu.VMEM((S, PAIR), F32), pltpu.VMEM((S, PAIR), F32)],
        compiler_params=_params(("parallel", "arbitrary")),
    )(qkv, qkv, qkv, tot, do)
    return dq, dk, dv


NEG = -1e30


def _fox_specs(S, tq, n_pairs, base):
    return _qkv_specs(S, tq, n_pairs, base) + [
        pl.BlockSpec((tq, PAIR), lambda p, i: (i, p)),
        pl.BlockSpec((1, S // tq, 8, tq), lambda p, i: (p, 0, 0, 0))]


def _fox_fwd(qkv, fcx, fcr, n_pairs, base, tq):
    S = qkv.shape[0]
    tk = tq
    scale = HEAD_DIM ** -0.5

    def body(q_ref, k_ref, v_ref, fq_ref, fk_ref, o_ref, lse_ref, acc_ref):
        i = pl.program_id(1)
        masks = _head_masks()
        q2 = q_ref[...]
        qm = [_sel(m, q2) for m in masks]
        fq2 = fq_ref[...]
        fq = [fq2[:, 0:1], fq2[:, HEAD_DIM:HEAD_DIM + 1]]
        rowpos = i * tq + lax.broadcasted_iota(jnp.int32, (tq, tk), 0)
        colin = lax.broadcasted_iota(jnp.int32, (tq, tk), 1)
        acc_ref[...] = jnp.zeros_like(acc_ref)

        def step(j, carry):
            off = pl.multiple_of(j * tk, tk)
            k2 = k_ref[pl.ds(off, tk), :]
            v2 = v_ref[pl.ds(off, tk), :]
            fk2 = fk_ref[0, j]
            mask = (j * tk + colin) <= rowpos
            out = None
            new = []
            alphas = []
            for h in heads:
                m_old, l_old = carry[2 * h], carry[2 * h + 1]
                s = _dot(qm[h], k2, 1, 1) * scale + fq[h] - fk2[h:h + 1, :]
                s = jnp.where(mask, s, NEG)
                m_new = jnp.maximum(m_old, jnp.max(s, axis=-1, keepdims=True))
                p = jnp.exp(s - m_new)
                alpha = jnp.exp(m_old - m_new)
                alphas.append(alpha)
                d = _dot(p.astype(BF16), _sel(masks[h], v2))
                out = d if out is None else out + d
                new += [m_new, alpha * l_old + jnp.sum(p, axis=-1, keepdims=True)]
            acc_ref[...] = acc_ref[...] * jnp.where(masks[0], alphas[0], alphas[1]) + out
            return tuple(new)

        zero = jnp.zeros((tq, 1), F32)
        neg = jnp.full((tq, 1), NEG, F32)
        m0, l0, m1, l1 = lax.fori_loop(0, i + 1, step, (neg, zero, neg, zero))
        o_ref[...] = (acc_ref[...] / jnp.where(masks[0], l0, l1)).astype(BF16)
        lse_ref[...] = jnp.where(masks[0], m0 + jnp.log(l0), m1 + jnp.log(l1))

    W = n_pairs * PAIR
    return pl.pallas_call(
        body, name="fox_fwd", grid=(n_pairs, S // tq), in_specs=_fox_specs(S, tq, n_pairs, base),
        out_specs=[pl.BlockSpec((tq, PAIR), lambda p, i: (i, p)), pl.BlockSpec((tq, PAIR), lambda p, i: (i, p))],
        out_shape=[jax.ShapeDtypeStruct((S, W), BF16), jax.ShapeDtypeStruct((S, W), F32)],
        scratch_shapes=[pltpu.VMEM((tq, PAIR), F32)],
        compiler_params=_params(("parallel", "arbitrary")),
    )(qkv, qkv, qkv, fcx, fcr)


def _fox_bwd(qkv, fcx, fcr, o, lse, do, n_pairs, base, tq):
    S = qkv.shape[0]
    tk = tq
    nq = S // tq
    scale = HEAD_DIM ** -0.5

    def body(q_ref, k_ref, v_ref, fq_ref, fk_ref, o_ref, lse_ref, do_ref,
             dq_ref, dk_ref, dv_ref, dfq_ref, dfk_ref, dq_acc, dk_acc, dv_acc):
        i = pl.program_id(1)
        masks = _head_masks()

        @pl.when(i == 0)
        def _():
            dk_acc[...] = jnp.zeros_like(dk_acc)
            dv_acc[...] = jnp.zeros_like(dv_acc)
            dfk_ref[...] = jnp.zeros_like(dfk_ref)

        q2 = q_ref[...]
        do2 = do_ref[...]
        qm = [_sel(m, q2) for m in masks]
        dom = [_sel(m, do2) for m in masks]
        fq2 = fq_ref[...]
        fq = [fq2[:, 0:1], fq2[:, HEAD_DIM:HEAD_DIM + 1]]
        l2 = lse_ref[...]
        lse_h = [l2[:, 0:1], l2[:, HEAD_DIM:HEAD_DIM + 1]]
        prod = do2.astype(F32) * o_ref[...].astype(F32)
        delta = [jnp.sum(jnp.where(m, prod, 0.0), axis=-1, keepdims=True) for m in masks]
        rowpos = i * tq + lax.broadcasted_iota(jnp.int32, (tq, tk), 0)
        colin = lax.broadcasted_iota(jnp.int32, (tq, tk), 1)
        dq_acc[...] = jnp.zeros_like(dq_acc)

        def step(j, carry):
            off = pl.multiple_of(j * tk, tk)
            k2 = k_ref[pl.ds(off, tk), :]
            v2 = v_ref[pl.ds(off, tk), :]
            fk2 = fk_ref[0, j]
            mask = (j * tk + colin) <= rowpos
            dq = None
            dk = None
            dv = None
            new = []
            dfk_rows = []
            for h in heads:
                s = _dot(qm[h], k2, 1, 1) * scale + fq[h] - fk2[h:h + 1, :]
                p = jnp.where(mask, jnp.exp(s - lse_h[h]), 0.0)
                dp = _dot(dom[h], v2, 1, 1)
                ds = p * (dp - delta[h])
                dsb = (ds * scale).astype(BF16)
                d1 = _dot(dsb, _sel(masks[h], k2))
                d2 = _dot(dsb, qm[h], 0, 0)
                d3 = _dot(p.astype(BF16), dom[h], 0, 0)
                dq = d1 if dq is None else dq + d1
                dk = d2 if dk is None else dk + d2
                dv = d3 if dv is None else dv + d3
                new.append(carry[h] + jnp.sum(ds, axis=-1, keepdims=True))
                dfk_rows.append(jnp.sum(ds, axis=0, keepdims=True))
            dq_acc[...] += dq
            dk_acc[pl.ds(off, tk), :] += dk
            dv_acc[pl.ds(off, tk), :] += dv
            dfk_ref[0, j, 0:1, :] += dfk_rows[0]
            dfk_ref[0, j, 1:2, :] += dfk_rows[1]
            return tuple(new)

        zero = jnp.zeros((tq, 1), F32)
        r0, r1 = lax.fori_loop(0, i + 1, step, (zero, zero))
        dq_ref[...] = dq_acc[...].astype(BF16)
        dfq_ref[...] = jnp.where(masks[0], r0, r1)

        @pl.when(i == nq - 1)
        def _():
            dk_ref[...] = dk_acc[...].astype(BF16)
            dv_ref[...] = dv_acc[...].astype(BF16)

    W = n_pairs * PAIR
    in_specs = _fox_specs(S, tq, n_pairs, base) + [
        pl.BlockSpec((tq, PAIR), lambda p, i: (i, p)),
        pl.BlockSpec((tq, PAIR), lambda p, i: (i, p)),
        pl.BlockSpec((tq, PAIR), lambda p, i: (i, p))]
    out_specs = [pl.BlockSpec((tq, PAIR), lambda p, i: (i, p)),
                 pl.BlockSpec((S, PAIR), lambda p, i: (0, p)),
                 pl.BlockSpec((S, PAIR), lambda p, i: (0, p)),
                 pl.BlockSpec((tq, PAIR), lambda p, i: (i, p)),
                 pl.BlockSpec((1, nq, 8, tk), lambda p, i: (p, 0, 0, 0))]
    return pl.pallas_call(
        body, name="fox_bwd", grid=(n_pairs, nq), in_specs=in_specs, out_specs=out_specs,
        out_shape=[jax.ShapeDtypeStruct((S, W), BF16)] * 3
        + [jax.ShapeDtypeStruct((S, W), F32), jax.ShapeDtypeStruct((n_pairs, nq, 8, tk), F32)],
        scratch_shapes=[pltpu.VMEM((tq, PAIR), F32), pltpu.VMEM((S, PAIR), F32), pltpu.VMEM((S, PAIR), F32)],
        compiler_params=_params(("parallel", "arbitrary")),
    )(qkv, qkv, qkv, fcx, fcr, o, lse, do)


RC = 32
VANISH = -104.0


def _chunks(n_rows, fn):
    for ci in range(n_rows // RC):
        fn(ci * RC)


def _wide(v, tk):
    return v if tk == 128 else jnp.tile(v, (1, tk // 128))


def _rep(col):
    return jnp.broadcast_to(col, (col.shape[0], 128))


def _per_head(blk, masks):
    sw = pltpu.roll(blk, HEAD_DIM, axis=1)
    return jnp.where(masks[0], blk, sw), jnp.where(masks[0], sw, blk)


def _fill_masked(dst_ref, src_ref, masks, mul=None, ones_lane=None):
    v = src_ref[...]
    if mul is not None:
        v = v * mul
    lane = lax.broadcasted_iota(jnp.int32, (1, PAIR), 1)
    for h in range(2):
        m = _sel(masks[h], v)
        if ones_lane is not None:
            m = jnp.where(lane == ones_lane[h], jnp.ones_like(m), m)
        dst_ref[h] = m


def _head_norms(v, masks):
    sq = v.astype(F32)
    sq = sq * sq
    return [jnp.sqrt(jnp.sum(jnp.where(m, sq, 0.0), axis=-1, keepdims=True)) for m in masks]


def _largest_key_norm(kmax_ref, k_ref, masks):
    for h, n in enumerate(_head_norms(k_ref[...], masks)):
        kmax_ref[h] = jnp.broadcast_to(jnp.max(n, axis=0, keepdims=True), (8, 128))


def _score_bound(q_scaled, kmax_ref, masks):
    return [_rep(n) * kmax_ref[h][0:1, :] for h, n in enumerate(_head_norms(q_scaled, masks))]


def _tri(tk, cmp):
    r = lax.broadcasted_iota(jnp.int32, (tk, tk), 0)
    c = lax.broadcasted_iota(jnp.int32, (tk, tk), 1)
    return cmp(r, c).astype(BF16)


def _diag_mask(r0, tk, strict):
    row = r0 + lax.broadcasted_iota(jnp.int32, (RC, tk), 0)
    col = lax.broadcasted_iota(jnp.int32, (RC, tk), 1)
    return (col < row) if strict else (col <= row)


def _peer_copies(src_ref, land_ref, send_sems, recv_sems, scatter, receive_side):
    x, y, c = lax.axis_index("x"), lax.axis_index("y"), lax.axis_index("c")
    me = 4 * x + 2 * y + c
    copies = []
    for k in range(1, N_DEV):
        px, py, pc = (1 - x if k & 4 else x), (1 - y if k & 2 else y), (1 - c if k & 1 else c)
        slot = 4 * px + 2 * py + pc
        copies.append(pltpu.make_async_remote_copy(
            src_ref=src_ref.at[slot] if scatter else src_ref,
            dst_ref=land_ref.at[slot] if receive_side else land_ref.at[me],
            send_sem=send_sems.at[k - 1], recv_sem=recv_sems.at[k - 1], device_id=(px, py, pc), device_id_type=MESH))
    return copies


def _call_carrying(body, exchange, name, grid, in_specs, out_specs, out_shape, scratch_shapes, args):
    if exchange is None:
        return pl.pallas_call(body, name=name, grid=grid, in_specs=in_specs, out_specs=out_specs, out_shape=out_shape,
                              scratch_shapes=scratch_shapes, compiler_params=_params(("parallel", "arbitrary")))(*args)
    src, scatter = exchange
    n_in, n_out = len(in_specs), len(out_specs)

    def carrying(*refs):
        src_ref, land_ref = refs[n_in], refs[n_in + 1 + n_out]
        send_sems, recv_sems = refs[-2], refs[-1]
        first = jnp.logical_and(pl.program_id(0) == 0, pl.program_id(1) == 0)
        last = jnp.logical_and(pl.program_id(0) == grid[0] - 1, pl.program_id(1) == grid[1] - 1)

        @pl.when(first)
        def _():
            for cp in _peer_copies(src_ref, land_ref, send_sems, recv_sems, scatter, False):
                cp.start()

        body(*refs[:n_in], *refs[n_in + 1:n_in + 1 + n_out], *refs[n_in + 2 + n_out:-2])

        @pl.when(last)
        def _():
            for cp in _peer_copies(src_ref, land_ref, send_sems, recv_sems, scatter, True):
                cp.wait_send()
                cp.wait_recv()

    any_space = pl.BlockSpec(memory_space=pl.ANY)
    land = jax.ShapeDtypeStruct((N_DEV,) + src.shape[-2:], src.dtype)
    return pl.pallas_call(
        carrying, name=name, grid=grid, in_specs=list(in_specs) + [any_space],
        out_specs=list(out_specs) + [any_space], out_shape=list(out_shape) + [land],
        scratch_shapes=list(scratch_shapes) + [pltpu.SemaphoreType.DMA((N_DEV - 1,)), pltpu.SemaphoreType.DMA((N_DEV - 1,))],
        compiler_params=_params(("arbitrary", "arbitrary")))(*args, src)


def _staggered(bodies):
    active, waiting = [], list(bodies)
    while waiting or active:
        if waiting:
            active.append(waiting.pop(0))
        for g in list(active):
            try:
                next(g)
            except StopIteration:
                active.remove(g)


def _streams(tile, j, diag, slot):
    return [tile(j, diag, slot, (0, 1))]


def _tiles(i, tile):
    def step(jj, carry):
        _staggered(_streams(tile, 2 * jj, False, 0) + _streams(tile, 2 * jj + 1, False, 1))
        return carry
    lax.fori_loop(0, i // 2, step, 0)

    @pl.when(i % 2 == 1)
    def _():
        _staggered(_streams(tile, i - 1, False, 0))
    _staggered(_streams(tile, i, True, 1))


def _tiles_reversed(i, tile, keep_going):
    _staggered(_streams(tile, i, True, 1))

    def cond(carry):
        jj, go = carry
        return jnp.logical_and(jj < i // 2, go)

    def step(carry):
        jj, _ = carry
        _staggered(_streams(tile, i - 1 - 2 * jj, False, 0) + _streams(tile, i - 2 - 2 * jj, False, 1))
        return jj + 1, keep_going(jnp.maximum(i - 3 - 2 * jj, 0))

    jj, go = lax.while_loop(cond, step, (jnp.int32(0), keep_going(jnp.maximum(i - 1, 0))))

    @pl.when(jnp.logical_and(jnp.logical_and(i % 2 == 1, jj == i // 2), go))
    def _():
        _staggered(_streams(tile, 0, False, 0))


def _sb_fwd2(qkv, n_pairs, base, tq):
    S = qkv.shape[0]
    tk = tq
    scale = HEAD_DIM ** -0.5

    def body(q_ref, k_ref, v_ref, o_ref, t_ref, z_ref, hi_ref, suf_ref, p_ref, r_ref, acc_ref, vm_ref):
        i = pl.program_id(1)
        masks = _head_masks()

        @pl.when(i == 0)
        def _():
            _fill_masked(vm_ref, v_ref, masks)

        q2 = q_ref[...] * scale
        qm = [_sel(m, q2) for m in masks]
        incl = _tri(tk, lambda r, c: r >= c)
        r_ref[...] = jnp.zeros_like(r_ref)
        acc_ref[...] = jnp.zeros_like(acc_ref)

        def tile(j, diag, slot, heads):
            off = pl.multiple_of(j * tk, tk)
            k2 = k_ref[pl.ds(off, tk), :]
            v2 = v_ref[pl.ds(off, tk), :]
            for h in heads:
                z_ref[2 * slot + h] = _dot(qm[h], k2, 1, 1)
            yield
            for h in heads:
                def split(r0, h=h):
                    rows = pl.ds(r0, RC)
                    lg = -_softplus(z_ref[2 * slot + h, rows, :])
                    if diag:
                        lg = jnp.where(_diag_mask(r0, tk, True), lg, 0.0)
                    hi_ref[2 * slot + h, rows, :] = lg.astype(BF16)
                _chunks(tq, split)
            yield
            for h in heads:
                suf_ref[2 * slot + h] = _dot(hi_ref[2 * slot + h], incl)
            yield
            for h in heads:
                def weights(r0, h=h):
                    rows = pl.ds(r0, RC)
                    a = jnp.exp(z_ref[2 * slot + h, rows, :] + suf_ref[2 * slot + h, rows, :] + _wide(r_ref[h, rows, :], tk))
                    if diag:
                        a = jnp.where(_diag_mask(r0, tk, True), a, 0.0)
                    p_ref[2 * slot + h, rows, :] = a.astype(BF16)
                _chunks(tq, weights)
            yield
            keys = pl.ds(off, tk)
            for h in heads:
                acc_ref[...] += _dot(p_ref[2 * slot + h], vm_ref[h, keys, :])
            for h in heads:
                r_ref[h] += _rep(suf_ref[2 * slot + h, :, 0:1])

        _tiles_reversed(i, tile, lambda nearest: jnp.max(r_ref[...]) >= VANISH)
        o_ref[...] = acc_ref[...].astype(BF16)
        t_ref[...] = acc_ref[...]

    W = n_pairs * PAIR
    return pl.pallas_call(
        body, name="sb_fwd", grid=(n_pairs, S // tq), in_specs=_qkv_specs(S, tq, n_pairs, base),
        out_specs=[pl.BlockSpec((tq, PAIR), lambda p, i: (i, p)), pl.BlockSpec((tq, PAIR), lambda p, i: (i, p))],
        out_shape=[jax.ShapeDtypeStruct((S, W), BF16), jax.ShapeDtypeStruct((S, W), F32)],
        scratch_shapes=[pltpu.VMEM((4, tq, tk), F32), pltpu.VMEM((4, tq, tk), BF16),
                        pltpu.VMEM((4, tq, tk), F32), pltpu.VMEM((4, tq, tk), BF16), pltpu.VMEM((2, tq, 128), F32),
                        pltpu.VMEM((tq, PAIR), F32), pltpu.VMEM((2, S, PAIR), BF16)],
        compiler_params=_params(("parallel", "arbitrary")),
    )(qkv, qkv, qkv)


def _sb_bwd2(qkv, o32, do, n_pairs, base, tq):
    S = qkv.shape[0]
    tk = tq
    nq = S // tq
    scale = HEAD_DIM ** -0.5

    def body(q_ref, k_ref, v_ref, o_ref, do_ref, dq_ref, dk_ref, dv_ref,
             z_ref, g_ref, omb_ref, cum_ref, hi_ref, lo_ref, a_ref, dz_ref,
             r_ref, cg_ref, dl_ref, dq_acc, dk_acc, dv_acc, ks_ref):
        i = pl.program_id(1)
        masks = _head_masks()

        @pl.when(i == 0)
        def _():
            dk_acc[...] = jnp.zeros_like(dk_acc)
            dv_acc[...] = jnp.zeros_like(dv_acc)
            _fill_masked(ks_ref, k_ref, masks, mul=scale)

        q2 = q_ref[...] * scale
        do2 = do_ref[...]
        qm = [_sel(m, q2) for m in masks]
        dom = [_sel(m, do2) for m in masks]
        prod = do2.astype(F32) * o_ref[...]
        for h in range(2):
            dl_ref[h] = _rep(jnp.sum(jnp.where(masks[h], prod, 0.0), axis=-1, keepdims=True))
        suffix = _tri(tk, lambda r, c: r >= c)
        r_ref[...] = jnp.zeros_like(r_ref)
        cg_ref[...] = jnp.zeros_like(cg_ref)
        dq_acc[...] = jnp.zeros_like(dq_acc)

        def tile(j, diag, slot, heads):
            off = pl.multiple_of(j * tk, tk)
            k2 = k_ref[pl.ds(off, tk), :]
            v2 = v_ref[pl.ds(off, tk), :]
            for h in heads:
                z_ref[2 * slot + h] = _dot(qm[h], k2, 1, 1)
                g_ref[2 * slot + h] = _dot(dom[h], v2, 1, 1)
            yield
            for h in heads:
                def split(r0, h=h):
                    rows = pl.ds(r0, RC)
                    sp = _softplus(z_ref[2 * slot + h, rows, :])
                    omb_ref[2 * slot + h, rows, :] = jnp.exp(-sp)
                    lg = -sp
                    if diag:
                        lg = jnp.where(_diag_mask(r0, tk, True), lg, 0.0)
                    hi_ref[2 * slot + h, rows, :] = lg.astype(BF16)
                _chunks(tq, split)
            yield
            for h in heads:
                cum_ref[2 * slot + h] = _dot(hi_ref[2 * slot + h], suffix)
            yield
            for h in heads:
                def weights(r0, h=h):
                    rows = pl.ds(r0, RC)
                    a = jnp.exp(z_ref[2 * slot + h, rows, :] + cum_ref[2 * slot + h, rows, :] + _wide(r_ref[h, rows, :], tk))
                    if diag:
                        a = jnp.where(_diag_mask(r0, tk, True), a, 0.0)
                    ab = a.astype(BF16)
                    g = g_ref[2 * slot + h, rows, :] * ab.astype(F32)
                    g_ref[2 * slot + h, rows, :] = g
                    a_ref[2 * slot + h, rows, :] = ab
                    hi, lo = _split2(g)
                    hi_ref[2 * slot + h, rows, :] = hi
                    lo_ref[2 * slot + h, rows, :] = lo
                _chunks(tq, weights)
            for h in heads:
                r_ref[h] += _rep(cum_ref[2 * slot + h, :, 0:1])
            yield
            for h in heads:
                cum_ref[2 * slot + h] = _dot(hi_ref[2 * slot + h], suffix) + _dot(lo_ref[2 * slot + h], suffix)
            yield
            for h in heads:
                def dscore(r0, h=h):
                    rows = pl.ds(r0, RC)
                    g = g_ref[2 * slot + h, rows, :]
                    from_here = cum_ref[2 * slot + h, rows, :] + _wide(cg_ref[h, rows, :], tk)
                    before = _wide(dl_ref[h, rows, :], tk) - from_here
                    omb = omb_ref[2 * slot + h, rows, :]
                    dz = g * omb - (1.0 - omb) * before
                    if diag:
                        dz = jnp.where(_diag_mask(r0, tk, True), dz, 0.0)
                    dz_ref[2 * slot + h, rows, :] = dz.astype(BF16)
                _chunks(tq, dscore)
            for h in heads:
                cg_ref[h] += _rep(cum_ref[2 * slot + h, :, 0:1])
            yield
            keys = pl.ds(off, tk)
            for h in heads:
                dq_acc[...] += _dot(dz_ref[2 * slot + h], ks_ref[h, keys, :])
                dk_acc[keys, :] += _dot(dz_ref[2 * slot + h], qm[h], 0, 0)
                dv_acc[keys, :] += _dot(a_ref[2 * slot + h], dom[h], 0, 0)

        _tiles_reversed(i, tile, lambda nearest: jnp.max(r_ref[...]) >= VANISH)
        dq_ref[...] = dq_acc[...].astype(BF16)

        @pl.when(i == nq - 1)
        def _():
            dk_ref[...] = dk_acc[...].astype(BF16)
            dv_ref[...] = dv_acc[...].astype(BF16)

    W = n_pairs * PAIR
    in_specs = _qkv_specs(S, tq, n_pairs, base) + [
        pl.BlockSpec((tq, PAIR), lambda p, i: (i, p)),
        pl.BlockSpec((tq, PAIR), lambda p, i: (i, p))]
    out_specs = [pl.BlockSpec((tq, PAIR), lambda p, i: (i, p)),
                 pl.BlockSpec((S, PAIR), lambda p, i: (0, p)),
                 pl.BlockSpec((S, PAIR), lambda p, i: (0, p))]
    big, stat = (4, tq, tk), (2, tq, 128)
    return pl.pallas_call(
        body, name="sb_bwd", grid=(n_pairs, nq), in_specs=in_specs, out_specs=out_specs,
        out_shape=[jax.ShapeDtypeStruct((S, W), BF16)] * 3,
        scratch_shapes=[pltpu.VMEM(big, F32)] * 4 + [pltpu.VMEM(big, BF16)] * 4 + [pltpu.VMEM(stat, F32)] * 3
        + [pltpu.VMEM((tq, PAIR), F32), pltpu.VMEM((S, PAIR), F32), pltpu.VMEM((S, PAIR), F32),
           pltpu.VMEM((2, S, PAIR), BF16)],
        compiler_params=_params(("parallel", "arbitrary")),
    )(qkv, qkv, qkv, o32, do)


def _fox_fwd2(qkv, fcx, fcr, n_pairs, base, tq, exchange=None):
    S = qkv.shape[0]
    tk = tq
    scale = HEAD_DIM ** -0.5
    spare = (HEAD_DIM, 0)

    def body(q_ref, k_ref, v_ref, fq_ref, fk_ref, o_ref, lse_ref, s_ref, p_ref, m_ref, al_ref, fqr_ref, acc_ref, vm_ref):
        i = pl.program_id(1)
        masks = _head_masks()

        @pl.when(i == 0)
        def _():
            _fill_masked(vm_ref, v_ref, masks, ones_lane=spare)

        q2 = q_ref[...] * scale
        qm = [_sel(m, q2) for m in masks]
        f0, f1 = _per_head(fq_ref[...], masks)
        fqr_ref[0] = f0
        fqr_ref[1] = f1
        m_ref[...] = jnp.full(m_ref.shape, NEG, F32)
        acc_ref[...] = jnp.zeros_like(acc_ref)

        def tile(j, diag, slot, heads):
            off = pl.multiple_of(j * tk, tk)
            k2 = k_ref[pl.ds(off, tk), :]
            v2 = v_ref[pl.ds(off, tk), :]
            fk2 = fk_ref[0, j]
            for h in heads:
                s_ref[2 * slot + h] = _dot(qm[h], k2, 1, 1)
            yield
            for h in heads:
                fk_row = fk2[h:h + 1, :]

                def probs(r0, h=h, fk_row=fk_row):
                    rows = pl.ds(r0, RC)
                    sv = s_ref[2 * slot + h, rows, :] - fk_row
                    if diag:
                        sv = jnp.where(_diag_mask(r0, tk, False), sv, NEG)
                    fq = fqr_ref[h, rows, :]
                    m_prev = m_ref[h, rows, :]
                    m_new = jnp.maximum(m_prev, jnp.max(sv, axis=-1, keepdims=True) + fq)
                    p_ref[2 * slot + h, rows, :] = jnp.exp(sv + _wide(fq - m_new, tk)).astype(BF16)
                    al_ref[2 * slot + h, rows, :] = jnp.exp(m_prev - m_new)
                    m_ref[h, rows, :] = m_new
                _chunks(tq, probs)
            yield
            for h in heads:
                acc_ref[h] = acc_ref[h] * al_ref[2 * slot + h] + _dot(p_ref[2 * slot + h], vm_ref[h, pl.ds(off, tk), :])

        _tiles(i, tile)
        a0, a1 = acc_ref[0], acc_ref[1]
        l0 = _rep(a0[:, spare[0]:spare[0] + 1])
        l1 = _rep(a1[:, spare[1]:spare[1] + 1])
        o_ref[...] = jnp.where(masks[0], a0 / l0, a1 / l1).astype(BF16)
        lse_ref[...] = jnp.where(masks[0], m_ref[0] + jnp.log(l0), m_ref[1] + jnp.log(l1))

    W = n_pairs * PAIR
    return _call_carrying(
        body, exchange, "fox_fwd", (n_pairs, S // tq), _fox_specs(S, tq, n_pairs, base),
        [pl.BlockSpec((tq, PAIR), lambda p, i: (i, p)), pl.BlockSpec((tq, PAIR), lambda p, i: (i, p))],
        [jax.ShapeDtypeStruct((S, W), BF16), jax.ShapeDtypeStruct((S, W), F32)],
        [pltpu.VMEM((4, tq, tk), F32), pltpu.VMEM((4, tq, tk), BF16), pltpu.VMEM((2, tq, 128), F32),
         pltpu.VMEM((4, tq, 128), F32), pltpu.VMEM((2, tq, 128), F32), pltpu.VMEM((2, tq, 128), F32),
         pltpu.VMEM((2, S, PAIR), BF16)],
        (qkv, qkv, qkv, fcx, fcr))


def _fox_bwd2(qkv, fcx, fcr, o, lse, do, n_pairs, base, tq, exchange=None):
    S = qkv.shape[0]
    tk = tq
    nq = S // tq
    scale = HEAD_DIM ** -0.5

    def body(q_ref, k_ref, v_ref, fq_ref, fk_ref, o_ref, lse_ref, do_ref,
             dq_ref, dk_ref, dv_ref, dfq_ref, dfk_ref,
             s_ref, dp_ref, p_ref, ds_ref, row_ref, dl_ref, dfq_acc, col_ref, dq_acc, dk_acc, dv_acc, ks_ref):
        i = pl.program_id(1)
        masks = _head_masks()

        @pl.when(i == 0)
        def _():
            dk_acc[...] = jnp.zeros_like(dk_acc)
            dv_acc[...] = jnp.zeros_like(dv_acc)
            dfk_ref[...] = jnp.zeros_like(dfk_ref)
            _fill_masked(ks_ref, k_ref, masks, mul=scale)

        q2 = q_ref[...] * scale
        do2 = do_ref[...]
        qm = [_sel(m, q2) for m in masks]
        dom = [_sel(m, do2) for m in masks]
        f0, f1 = _per_head(fq_ref[...], masks)
        l0, l1 = _per_head(lse_ref[...], masks)
        row_ref[0] = f0 - l0
        row_ref[1] = f1 - l1
        prod = do2.astype(F32) * o_ref[...].astype(F32)
        for h in range(2):
            dl_ref[h] = _rep(jnp.sum(jnp.where(masks[h], prod, 0.0), axis=-1, keepdims=True))
        dfq_acc[...] = jnp.zeros_like(dfq_acc)
        dq_acc[...] = jnp.zeros_like(dq_acc)

        def tile(j, diag, slot, heads):
            off = pl.multiple_of(j * tk, tk)
            k2 = k_ref[pl.ds(off, tk), :]
            v2 = v_ref[pl.ds(off, tk), :]
            fk2 = fk_ref[0, j]
            for h in heads:
                s_ref[2 * slot + h] = _dot(qm[h], k2, 1, 1)
                dp_ref[2 * slot + h] = _dot(dom[h], v2, 1, 1)
            yield
            for h in heads:
                col_ref[2 * slot + h] = jnp.zeros((8, tk), F32)
                fk_row = fk2[h:h + 1, :]

                def dscore(r0, h=h, fk_row=fk_row):
                    rows = pl.ds(r0, RC)
                    p = jnp.exp(s_ref[2 * slot + h, rows, :] - fk_row + _wide(row_ref[h, rows, :], tk))
                    if diag:
                        p = jnp.where(_diag_mask(r0, tk, False), p, 0.0)
                    ds = p * (dp_ref[2 * slot + h, rows, :] - _wide(dl_ref[h, rows, :], tk))
                    p_ref[2 * slot + h, rows, :] = p.astype(BF16)
                    ds_ref[2 * slot + h, rows, :] = ds.astype(BF16)
                    dfq_acc[h, rows, :] += _rep(jnp.sum(ds, axis=-1, keepdims=True))
                    col_ref[2 * slot + h] += jnp.sum(ds.reshape(RC // 8, 8, tk), axis=0)
                _chunks(tq, dscore)
            yield
            keys = pl.ds(off, tk)
            for h in heads:
                dq_acc[...] += _dot(ds_ref[2 * slot + h], ks_ref[h, keys, :])
                dk_acc[keys, :] += _dot(ds_ref[2 * slot + h], qm[h], 0, 0)
                dv_acc[keys, :] += _dot(p_ref[2 * slot + h], dom[h], 0, 0)
            for h in heads:
                dfk_ref[0, j, h:h + 1, :] += jnp.sum(col_ref[2 * slot + h], axis=0, keepdims=True)

        _tiles(i, tile)
        dq_ref[...] = dq_acc[...].astype(BF16)
        dfq_ref[...] = jnp.where(masks[0], dfq_acc[0], dfq_acc[1])

        @pl.when(i == nq - 1)
        def _():
            dk_ref[...] = dk_acc[...].astype(BF16)
            dv_ref[...] = dv_acc[...].astype(BF16)

    W = n_pairs * PAIR
    in_specs = _fox_specs(S, tq, n_pairs, base) + [
        pl.BlockSpec((tq, PAIR), lambda p, i: (i, p)),
        pl.BlockSpec((tq, PAIR), lambda p, i: (i, p)),
        pl.BlockSpec((tq, PAIR), lambda p, i: (i, p))]
    out_specs = [pl.BlockSpec((tq, PAIR), lambda p, i: (i, p)),
                 pl.BlockSpec((S, PAIR), lambda p, i: (0, p)),
                 pl.BlockSpec((S, PAIR), lambda p, i: (0, p)),
                 pl.BlockSpec((tq, PAIR), lambda p, i: (i, p)),
                 pl.BlockSpec((1, nq, 8, tk), lambda p, i: (p, 0, 0, 0))]
    return _call_carrying(
        body, exchange, "fox_bwd", (n_pairs, nq), in_specs, out_specs,
        [jax.ShapeDtypeStruct((S, W), BF16)] * 3
        + [jax.ShapeDtypeStruct((S, W), F32), jax.ShapeDtypeStruct((n_pairs, nq, 8, tk), F32)],
        [pltpu.VMEM((4, tq, tk), F32)] * 2 + [pltpu.VMEM((4, tq, tk), BF16)] * 2
        + [pltpu.VMEM((2, tq, 128), F32)] * 3 + [pltpu.VMEM((4, 8, tk), F32)]
        + [pltpu.VMEM((tq, PAIR), F32), pltpu.VMEM((S, PAIR), F32), pltpu.VMEM((S, PAIR), F32),
           pltpu.VMEM((2, S, PAIR), BF16)],
        (qkv, qkv, qkv, fcx, fcr, o, lse, do))


def _swiglu_fwd(u2, wg, wu):
    S, D = u2.shape
    FF = wg.shape[1]
    tm, tn = _pick(S, (512, 256, 128)), _divisors(FF, 1536)[0]

    def body(u_ref, g_ref, w_ref, a_ref, b_ref, h_ref):
        u = u_ref[...]
        a = _dot(u, g_ref[...])
        b = _dot(u, w_ref[...])
        a_ref[...] = a.astype(BF16)
        b_ref[...] = b.astype(BF16)
        h_ref[...] = (a / (1.0 + jnp.exp(-a)) * b).astype(BF16)

    spec_o = pl.BlockSpec((tm, tn), lambda i, j: (i, j))
    return pl.pallas_call(
        body, name="swiglu_fwd", grid=(S // tm, FF // tn),
        in_specs=[pl.BlockSpec((tm, D), lambda i, j: (i, 0)),
                  pl.BlockSpec((D, tn), lambda i, j: (0, j)),
                  pl.BlockSpec((D, tn), lambda i, j: (0, j))],
        out_specs=[spec_o] * 3, out_shape=[jax.ShapeDtypeStruct((S, FF), BF16)] * 3,
        compiler_params=_params(("parallel", "parallel")),
    )(u2, wg, wu)


def _swiglu_bwd(dh, wd, a, b):
    S, D = dh.shape
    FF = wd.shape[0]
    tm, tn = _pick(S, (512, 256, 128)), _divisors(FF, 1536)[0]

    def body(dh_ref, w_ref, a_ref, b_ref, da_ref, db_ref):
        dhin = _dot(dh_ref[...], w_ref[...], 1, 1)
        av = a_ref[...].astype(F32)
        bv = b_ref[...].astype(F32)
        sig = 1.0 / (1.0 + jnp.exp(-av))
        da_ref[...] = (dhin * bv * (sig * (1.0 + av * (1.0 - sig)))).astype(BF16)
        db_ref[...] = (dhin * (av * sig)).astype(BF16)

    spec_o = pl.BlockSpec((tm, tn), lambda i, j: (i, j))
    return pl.pallas_call(
        body, name="swiglu_bwd", grid=(S // tm, FF // tn),
        in_specs=[pl.BlockSpec((tm, D), lambda i, j: (i, 0)),
                  pl.BlockSpec((tn, D), lambda i, j: (j, 0)), spec_o, spec_o],
        out_specs=[spec_o] * 2, out_shape=[jax.ShapeDtypeStruct((S, FF), BF16)] * 2,
        compiler_params=_params(("parallel", "parallel")),
    )(dh, wd, a, b)


def _gate_bwd(dmix, wo, g_sb, g_fx, y_sb, y_fx):
    S, D = dmix.shape
    tm, tn = _pick(S, (512, 256, 128)), _divisors(D, 1024)[0]

    def body(dm_ref, w_ref, gs_ref, gf_ref, ys_ref, yf_ref, dys_ref, dyf_ref, dls_ref, dlf_ref, bs_ref, bf_ref):
        @pl.when(pl.program_id(1) == 0)
        def _():
            bs_ref[...] = jnp.zeros_like(bs_ref)
            bf_ref[...] = jnp.zeros_like(bf_ref)

        dmi = _dot(dm_ref[...], w_ref[...], 1, 1)
        gs, gf = gs_ref[...], gf_ref[...]
        dys_ref[...] = (dmi * gs).astype(BF16)
        dyf_ref[...] = (dmi * gf).astype(BF16)
        dls = dmi * ys_ref[...] * gs * (1.0 - gs)
        dlf = dmi * yf_ref[...] * gf * (1.0 - gf)
        dls_ref[...] = dls.astype(BF16)
        dlf_ref[...] = dlf.astype(BF16)
        bs_ref[0:1, :] += _colsum(dls)
        bf_ref[0:1, :] += _colsum(dlf)

    t = pl.BlockSpec((tm, tn), lambda j, i: (i, j))
    accs = pl.BlockSpec((8, tn), lambda j, i: (0, j))
    return pl.pallas_call(
        body, name="gate_bwd", grid=(D // tn, S // tm),
        in_specs=[pl.BlockSpec((tm, D), lambda j, i: (i, 0)),
                  pl.BlockSpec((tn, D), lambda j, i: (j, 0)), t, t, t, t],
        out_specs=[t, t, t, t, accs, accs],
        out_shape=[jax.ShapeDtypeStruct((S, D), BF16)] * 4 + [jax.ShapeDtypeStruct((8, D), F32)] * 2,
        compiler_params=_params(("parallel", "arbitrary")),
    )(dmix, wo, g_sb, g_fx, y_sb, y_fx)


def _local_step(x, target, ada8, lnp8, bg_sb, bg_fx, bf_pad, wqkv, wf, wgs, wgf, gather, later_weights, pack_early,
                pack_last):
    S, D = x.shape
    W = wqkv.shape[1] // 6
    n_pairs = W // PAIR
    n_heads = W // HEAD_DIM
    ts = _pick(S, (256, 128))
    tq = _pick(S, (256, 128))

    u1 = _ln_mod(x, ada8, ts)
    qkv = _mm([(u1, wqkv)], 'nn', BF16, "in_qkv")
    f = _mm([(u1, wf)], 'nn', F32, "in_f")
    g_sb = _mm([(u1, wgs)], 'nn', F32, "in_gsb", bias=bg_sb, act='sigmoid')
    g_fx = _mm([(u1, wgf)], 'nn', F32, "in_gfx", bias=bg_fx, act='sigmoid')
    fc = _fgate_fwd(f, bf_pad, _pick(S, (512, 256, 128)))
    fch = fc[:, :n_heads]
    fcx = jnp.repeat(fch, HEAD_DIM, axis=1)
    nq = S // tq
    fcr = jnp.pad(fch.T.reshape(n_pairs, 2, nq, tq).transpose(0, 2, 1, 3),
                  ((0, 0), (0, 0), (0, 6), (0, 0)))
    o_sb, o_sb32 = _sb_fwd2(qkv, n_pairs, 0, tq)
    o_fx, lse, *zone = _fox_fwd2(qkv, fcx, fcr, n_pairs, 3 * n_pairs, tq, gather)
    wsb, wfx, wo, wfg, wfu, wfd = later_weights(zone[0] if zone else None)
    y_sb =_mm([(o_sb, wsb)], 'nn', F32, "out_sb")
    y_fx = _mm([(o_fx, wfx)], 'nn', F32, "out_fx")
    mix_in = _gate_mix(g_sb, g_fx, y_sb, y_fx, ts)
    mix = _mm([(mix_in, wo)], 'nn', F32, "out_o")
    x1, u2 = _post_attn(x, mix, ada8, lnp8, ts)
    a, b, hin = _swiglu_fwd(u2, wfg, wfu)
    h = _mm([(hin, wfd)], 'nn', F32, "ffn_down")
    dr2, dh, st_loss = _loss_head(x1, h, target, ada8, lnp8, ts)

    da, db = _swiglu_bwd(dh, wfd, a, b)
    g_wfd = _mm([(hin, dh)], 'tn', F32, "g_ffn_down")
    du2 = _mm([(da, wfg), (db, wfu)], 'nt', F32, "d_u2")
    g_wfg = _mm([(u2, da)], 'tn', F32, "g_ffn_gate")
    g_wfu = _mm([(u2, db)], 'tn', F32, "g_ffn_up")
    dr1, dmix, st_mid = _mid_bwd(du2, x1, dr2, mix, x, ada8, lnp8, ts)
    dys, dyf, dls, dlf, gb_sb, gb_fx = _gate_bwd(dmix, wo, g_sb, g_fx, y_sb, y_fx)
    g_wo = _mm([(mix_in, dmix)], 'tn', F32, "g_w_o")
    do_sb = _mm([(dys, wsb)], 'nt', BF16, "d_o_sb")
    do_fx = _mm([(dyf, wfx)], 'nt', BF16, "d_o_fx")
    g_wsb = _mm([(o_sb, dys)], 'tn', F32, "g_sb_out")
    g_wfx = _mm([(o_fx, dyf)], 'tn', F32, "g_fox_out")
    scatter = pack_early(dict(sb=g_wsb, fx=g_wfx, o=g_wo, fg=g_wfg, fu=g_wfu, fd=g_wfd))
    dq_s, dk_s, dv_s = _sb_bwd2(qkv, o_sb32, do_sb, n_pairs, 0, tq)
    dq_f, dk_f, dv_f, dfq, dfk, *zone = _fox_bwd2(qkv, fcx, fcr, o_fx, lse, do_fx, n_pairs, 3 * n_pairs, tq, scatter)
    dfc = dfq[:, ::HEAD_DIM] - dfk[:, :, :2, :].transpose(0, 2, 1, 3).reshape(n_heads, S).T
    dfc = jnp.pad(dfc, ((0, 0), (0, 128 - n_heads)))
    df, gb_f = _fgate_bwd(dfc, f, bf_pad, _pick(S, (512, 256, 128)))
    grads = [dq_s, dk_s, dv_s, dq_f, dk_f, dv_f]
    g_wqkv = [_mm([(u1, g)], 'tn', F32, "g_in_%d" % n) for n, g in enumerate(grads)]
    g_wf = _mm([(u1, df)], 'tn', F32, "g_in_f")
    g_wgs = _mm([(u1, dls)], 'tn', F32, "g_in_gsb")
    g_wgf = _mm([(u1, dlf)], 'tn', F32, "g_in_gfx")
    wgrads = dict(qkv=g_wqkv, f=g_wf, gs=g_wgs, gf=g_wgf)
    last = pack_last(wgrads)
    du1 = _mm([(g, wqkv, W, 0, n) for n, g in enumerate(grads)] + [(df, wf), (dls, wgs), (dlf, wgf)],
              'nt', F32, "d_u1", exchange=last)
    du1, last_zone = du1 if last is not None else (du1, None)
    gx, st_first = _first_bwd(du1, x, dr1, ada8, ts)

    stats = dict(loss=st_loss, mid=st_mid, first=st_first, gb_sb=gb_sb, gb_fx=gb_fx, gb_f=gb_f)
    return gx, wgrads, stats, (scatter, zone[0] if zone else None), (last, last_zone)


def _position():
    x, y, c = lax.axis_index("x"), lax.axis_index("y"), lax.axis_index("c")
    return x, y, c, 4 * x + 2 * y + c


def _flip(x, y, c, k):
    px = 1 - x if k & 4 else x
    py = 1 - y if k & 2 else y
    pc = 1 - c if k & 1 else c
    return (px, py, pc), 4 * px + 2 * py + pc


def _all_gather_small(v, name):
    r, n = v.shape

    def body(x_ref, out_ref, send_sems, recv_sems, local_sem):
        x, y, c, me = _position()
        mine = pltpu.make_async_copy(x_ref, out_ref.at[me], local_sem)
        mine.start()
        sends = []
        for k in range(1, N_DEV):
            peer, _ = _flip(x, y, c, k)
            cp = pltpu.make_async_remote_copy(
                src_ref=x_ref, dst_ref=out_ref.at[me], send_sem=send_sems.at[k - 1], recv_sem=recv_sems.at[k - 1],
                device_id=peer, device_id_type=MESH)
            cp.start()
            sends.append(cp)
        for k in range(1, N_DEV):
            peer, slot = _flip(x, y, c, k)
            pltpu.make_async_remote_copy(
                src_ref=x_ref, dst_ref=out_ref.at[slot], send_sem=send_sems.at[k - 1], recv_sem=recv_sems.at[k - 1],
                device_id=peer, device_id_type=MESH).wait_recv()
        for cp in sends:
            cp.wait_send()
        mine.wait()

    return pl.pallas_call(
        body, name=name, out_shape=jax.ShapeDtypeStruct((N_DEV, r, n), v.dtype),
        in_specs=[pl.BlockSpec(memory_space=pltpu.VMEM)], out_specs=pl.BlockSpec(memory_space=pltpu.VMEM),
        scratch_shapes=[pltpu.SemaphoreType.DMA((N_DEV - 1,)), pltpu.SemaphoreType.DMA((N_DEV - 1,)),
                        pltpu.SemaphoreType.DMA],
    )(v)


def _all_gather_weights(packed):
    R, C = packed.shape

    def body(x_ref, out_ref, send_sems, recv_sems, local_sem):
        x, y, c, me = _position()
        sibling, sib_slot = _flip(x, y, c, 1)
        mine = pltpu.make_async_copy(x_ref, out_ref.at[me], local_sem)
        mine.start()

        def copy(k, slot, to, src=None):
            return pltpu.make_async_remote_copy(
                src_ref=out_ref.at[slot] if src is None else src, dst_ref=out_ref.at[slot],
                send_sem=send_sems.at[k], recv_sem=recv_sems.at[k], device_id=to, device_id_type=MESH)

        first = [copy(0, me, sibling, src=x_ref)]
        chips = (4, 2, 6)
        for n, k in enumerate(chips):
            peer, _ = _flip(x, y, c, k)
            first.append(copy(1 + n, me, peer, src=x_ref))
        for cp in first:
            cp.start()
        passed = []
        for n, k in enumerate(chips):
            peer, slot = _flip(x, y, c, k)
            copy(1 + n, slot, peer).wait_recv()
            cp = copy(4 + n, slot, sibling)
            cp.start()
            passed.append(cp)
        copy(0, sib_slot, sibling).wait_recv()
        for n, k in enumerate(chips):
            _, slot = _flip(x, y, c, k | 1)
            copy(4 + n, slot, sibling).wait_recv()
        for cp in first + passed:
            cp.wait_send()
        mine.wait()

    return pl.pallas_call(
        body, name="all_gather_weights", out_shape=jax.ShapeDtypeStruct((N_DEV, R, C), packed.dtype),
        in_specs=[pl.BlockSpec(memory_space=pl.ANY)], out_specs=pl.BlockSpec(memory_space=pl.ANY),
        scratch_shapes=[pltpu.SemaphoreType.DMA((7,)), pltpu.SemaphoreType.DMA((7,)), pltpu.SemaphoreType.DMA],
    )(packed)


def _exchange_grads(gpack):
    _, R, C = gpack.shape

    def body(g_ref, out_ref, send_sems, recv_sems, local_sem):
        x, y, c, me = _position()
        mine = pltpu.make_async_copy(g_ref.at[me], out_ref.at[me], local_sem)
        mine.start()
        sends = []
        for k in range(1, N_DEV):
            peer, slot = _flip(x, y, c, k)
            cp = pltpu.make_async_remote_copy(
                src_ref=g_ref.at[slot], dst_ref=out_ref.at[me], send_sem=send_sems.at[k - 1],
                recv_sem=recv_sems.at[k - 1], device_id=peer, device_id_type=MESH)
            cp.start()
            sends.append(cp)
        for k in range(1, N_DEV):
            peer, slot = _flip(x, y, c, k)
            pltpu.make_async_remote_copy(
                src_ref=g_ref.at[slot], dst_ref=out_ref.at[slot], send_sem=send_sems.at[k - 1],
                recv_sem=recv_sems.at[k - 1], device_id=peer, device_id_type=MESH).wait_recv()
        for cp in sends:
            cp.wait_send()
        mine.wait()

    return pl.pallas_call(
        body, name="exchange_grads", out_shape=jax.ShapeDtypeStruct((N_DEV, R, C), gpack.dtype),
        in_specs=[pl.BlockSpec(memory_space=pl.ANY)], out_specs=pl.BlockSpec(memory_space=pl.ANY),
        scratch_shapes=[pltpu.SemaphoreType.DMA((N_DEV - 1,)), pltpu.SemaphoreType.DMA((N_DEV - 1,)),
                        pltpu.SemaphoreType.DMA],
    )(gpack)


def _own_slot(land, own):
    me = 4 * lax.axis_index("x") + 2 * lax.axis_index("y") + lax.axis_index("c")
    return lax.dynamic_update_slice(land, own[None], (me, 0, 0))


def _sum_slots(recv, name, tr):
    n, R, C = recv.shape

    def body(r_ref, o_ref):
        acc = r_ref[0].astype(F32)
        for s in range(1, n):
            acc = acc + r_ref[s].astype(F32)
        o_ref[...] = acc

    return pl.pallas_call(
        body, name=name, grid=(R // tr,), in_specs=[pl.BlockSpec((n, tr, C), lambda i: (0, i, 0))],
        out_specs=pl.BlockSpec((tr, C), lambda i: (i, 0)), out_shape=jax.ShapeDtypeStruct((R, C), F32),
        compiler_params=_params(("parallel",)),
    )(recv)


def _sum_stats(st_all, loss_row):
    n, r, D = st_all.shape

    def body(s_ref, o_ref, l_ref):
        acc = s_ref[0]
        for d in range(1, n):
            acc = acc + s_ref[d]
        o_ref[...] = acc
        l_ref[...] = jnp.zeros((8, 128), F32) + jnp.sum(acc[loss_row:loss_row + 1, :], axis=-1, keepdims=True)

    return pl.pallas_call(
        body, name="sum_stats", out_shape=[jax.ShapeDtypeStruct((r, D), F32), jax.ShapeDtypeStruct((8, 128), F32)],
    )(st_all)


def _adamw(w, g, m, v, name):
    R, C = w.shape
    tr = _pick(R, (256, 176, 128, 64, 32, 16, 8))
    c1 = 1.0 / (1.0 - ADAM_B1 ** ADAM_STEP)
    c2 = 1.0 / (1.0 - ADAM_B2 ** ADAM_STEP)

    def body(w_ref, g_ref, m_ref, v_ref, d_ref, nm_ref, nv_ref):
        gv = g_ref[...]
        nm = ADAM_B1 * m_ref[...] + (1.0 - ADAM_B1) * gv
        nv = ADAM_B2 * v_ref[...] + (1.0 - ADAM_B2) * (gv * gv)
        nm_ref[...] = nm
        nv_ref[...] = nv
        d_ref[...] = -ADAM_LR * ((nm * c1) / (jnp.sqrt(nv * c2) + ADAM_EPS) + ADAM_WD * w_ref[...])

    spec = pl.BlockSpec((tr, C), lambda i: (i, 0))
    return pl.pallas_call(
        body, name=name, grid=(R // tr,), in_specs=[spec] * 4, out_specs=[spec] * 3,
        out_shape=[jax.ShapeDtypeStruct((R, C), F32)] * 3, compiler_params=_params(("parallel",)),
    )(w, g, m, v)


def _round16(n):
    return -(-n // 16) * 16


def _pack_layout(D, in_cols, ff, W):
    parts = [("in", D * (in_cols // N_DEV) // D), ("fg", ff // N_DEV), ("fu", ff // N_DEV),
             ("sb", W * (D // N_DEV) // D), ("fx", W * (D // N_DEV) // D), ("o", D // N_DEV), ("fd", ff // N_DEV)]
    layout, off = {}, 0
    for nm, rows in parts:
        layout[nm] = (off, rows)
        off += _round16(rows)
    return layout, off


def _rows_of(a, D, rows):
    a = a.reshape(rows, D)
    return jnp.pad(a, ((0, _round16(rows) - rows), (0, 0)))


def _cols_to_dest(g, D):
    K, N = g.shape
    n = N // N_DEV
    return g.reshape(K, N_DEV, n).transpose(1, 0, 2).reshape(N_DEV, K * n // D, D)


def _cols_from_src(blocks, K, n):
    return blocks.reshape(N_DEV, K, n).transpose(1, 0, 2).reshape(K, N_DEV * n)


def _pad_rows16(a):
    rows = a.shape[1]
    return jnp.pad(a, ((0, 0), (0, _round16(rows) - rows), (0, 0)))


def kernel(x, c, w_ada, b_ada, w_in, b_gate, b_forget, w_sb_out, w_fox_out, w_o, ln1_g, ln1_b, w_ffn_gate, w_ffn_up, w_ffn_down, ln2_g, ln2_b, loss_target, m_w_ada, m_b_ada, m_w_in, m_b_gate, m_b_forget, m_w_sb_out, m_w_fox_out, m_w_o, m_ln1_g, m_ln1_b, m_w_ffn_gate, m_w_ffn_up, m_w_ffn_down, m_ln2_g, m_ln2_b, v_w_ada, v_b_ada, v_w_in, v_b_gate, v_b_forget, v_w_sb_out, v_w_fox_out, v_w_o, v_ln1_g, v_ln1_b, v_w_ffn_gate, v_w_ffn_up, v_w_ffn_down, v_ln2_g, v_ln2_b):
    S, D = x.shape[1], x.shape[2]
    W = w_sb_out.shape[1]
    n_heads = b_forget.shape[1]
    ff = w_ffn_down.shape[1] * N_DEV
    in_loc = w_in.shape[2]
    in_cols = in_loc * N_DEV
    ada_loc = w_ada.shape[2]
    n_cond = ada_loc * N_DEV // D
    assert w_ada.shape[0] == 1 and n_cond == 6 and in_cols == 6 * W + n_heads + 2 * D and n_heads <= 128
    me = 4 * lax.axis_index("x") + 2 * lax.axis_index("y") + lax.axis_index("c")

    c_all = _all_gather_small(c, "gather_c").reshape(N_DEV, D)
    c16 = jnp.pad(c_all, ((0, 16 - N_DEV), (0, 0)))
    b_cols = lax.dynamic_slice(b_ada, (0, me * ada_loc), (1, ada_loc))
    ada_cols = _mm([(c16, w_ada[0])], 'nn', F32, "ada_fwd", bias=b_cols, silu_a=True)[:N_DEV]
    ada_all = _all_gather_small(ada_cols, "gather_ada")
    ada_me = lax.dynamic_index_in_dim(ada_all, me, axis=1, keepdims=False)
    ada8 = jnp.pad(ada_me.reshape(n_cond, D), ((0, 8 - n_cond), (0, 0)))
    lnp8 = jnp.concatenate([ln1_g, ln1_b, ln2_g, ln2_b, jnp.zeros((4, D), F32)], axis=0)

    layout, R = _pack_layout(D, in_cols, ff, W)
    shards = dict(**{"in": w_in[0]}, fg=w_ffn_gate[0], fu=w_ffn_up[0], sb=w_sb_out[0], fx=w_fox_out[0], o=w_o[0],
                  fd=w_ffn_down[0])
    later = [nm for nm in layout if nm != "in"]
    r_in = _round16(layout["in"][1])
    first = _rows_of(shards["in"].astype(BF16), D, layout["in"][1])
    rest = jnp.concatenate([_rows_of(shards[nm].astype(BF16), D, layout[nm][1]) for nm in later], axis=0)
    gathered_in = _all_gather_weights(first)

    w_in_full = _cols_from_src(gathered_in[:, :layout["in"][1], :], D, in_loc)
    wqkv = w_in_full[:, :6 * W]
    wf = jnp.pad(w_in_full[:, 6 * W:6 * W + n_heads], ((0, 0), (0, 128 - n_heads)))
    wgs = w_in_full[:, 6 * W + n_heads:6 * W + n_heads + D]
    wgf = w_in_full[:, 6 * W + n_heads + D:]
    bf_pad = jnp.pad(b_forget, ((0, 0), (0, 128 - n_heads)))

    def later_weights(zone):
        gathered = _own_slot(zone, rest)

        def part(nm):
            off, rows = layout[nm]
            return gathered[:, off - r_in:off - r_in + rows, :]

        return (_cols_from_src(part("sb"), W, D // N_DEV), _cols_from_src(part("fx"), W, D // N_DEV),
                part("o").reshape(D, D), _cols_from_src(part("fg"), D, ff // N_DEV),
                _cols_from_src(part("fu"), D, ff // N_DEV), part("fd").reshape(ff, D))

    def pack_early(g):
        dest = {"fg": _cols_to_dest(g["fg"], D), "fu": _cols_to_dest(g["fu"], D), "sb": _cols_to_dest(g["sb"], D),
                "fx": _cols_to_dest(g["fx"], D), "o": g["o"].reshape(N_DEV, D // N_DEV, D),
                "fd": g["fd"].reshape(N_DEV, ff // N_DEV, D)}
        return jnp.concatenate([_pad_rows16(dest[nm].astype(BF16)) for nm in later], axis=1), True

    def pack_last(g):
        g_in = jnp.concatenate(g["qkv"] + [g["f"][:, :n_heads], g["gs"], g["gf"]], axis=1)
        return _pad_rows16(_cols_to_dest(g_in, D).astype(BF16)), True

    gx, wg, st, ((pack, _), land), ((pack_in, _), land_in) = _local_step(
        x[0], loss_target[0], ada8, lnp8, b_gate[:, :D], b_gate[:, D:], bf_pad, wqkv, wf, wgs, wgf,
        (rest, False), later_weights, pack_early, pack_last)

    def summed(zone, sent, name):
        own = lax.dynamic_index_in_dim(sent, me, axis=0, keepdims=False)
        return _sum_slots(_own_slot(zone, own), name, _pick(zone.shape[1], (512, 656, 256, 128, 64, 16)))

    gsum_rest = summed(land, pack, "sum_grads_rest")
    gsum_in = summed(land_in, pack_in, "sum_grads_in")

    def gshard(nm, shape):
        off, rows = layout[nm]
        if nm == "in":
            return gsum_in[:rows].reshape(shape)
        return gsum_rest[off - r_in:off - r_in + rows].reshape(shape)

    zrow = jnp.zeros((1, D), F32)
    gb_f_row = jnp.pad(st["gb_f"][0:1], ((0, 0), (0, D - 128)))
    stats16 = jnp.concatenate([
        st["first"][1:2], st["first"][0:1], st["mid"][4:5], st["mid"][1:2], st["mid"][0:1], st["loss"][3:4],
        st["mid"][2:3], st["mid"][3:4], st["loss"][1:2], st["loss"][2:3], st["gb_sb"][0:1], st["gb_fx"][0:1],
        st["loss"][0:1], gb_f_row, zrow, zrow], axis=0)
    st_all = _all_gather_small(stats16, "gather_stats")
    st_sum, loss_blk = _sum_stats(st_all, 12)
    loss = loss_blk[0, 0]

    d_ada_all = st_all[:, :n_cond, :].reshape(N_DEV, n_cond * D)
    d_cols = lax.dynamic_slice(d_ada_all, (0, me * ada_loc), (N_DEV, ada_loc))
    d16 = jnp.pad(d_cols, ((0, 16 - N_DEV), (0, 0)))
    g_w_ada = _mm([(c16, d16)], 'tn', F32, "ada_wgrad", silu_a=True)

    small_w = jnp.concatenate([b_ada.reshape(n_cond, D), ln1_g, ln1_b, ln2_g, ln2_b, b_gate.reshape(2, D), zrow,
                               jnp.pad(b_forget, ((0, 0), (0, D - n_heads))), zrow, zrow], axis=0)
    small_m = jnp.concatenate([m_b_ada.reshape(n_cond, D), m_ln1_g, m_ln1_b, m_ln2_g, m_ln2_b, m_b_gate.reshape(2, D),
                               zrow, jnp.pad(m_b_forget, ((0, 0), (0, D - n_heads))), zrow, zrow], axis=0)
    small_v = jnp.concatenate([v_b_ada.reshape(n_cond, D), v_ln1_g, v_ln1_b, v_ln2_g, v_ln2_b, v_b_gate.reshape(2, D),
                               zrow, jnp.pad(v_b_forget, ((0, 0), (0, D - n_heads))), zrow, zrow], axis=0)
    sm = _adamw(small_w, st_sum, small_m, small_v, "adamw_small")

    def small(a, nm):
        if nm == "b_ada":
            return a[0:n_cond].reshape(1, n_cond * D)
        if nm == "b_gate":
            return a[10:12].reshape(1, 2 * D)
        if nm == "b_forget":
            return a[13:14, :n_heads]
        row = {"ln1_g": 6, "ln1_b": 7, "ln2_g": 8, "ln2_b": 9}[nm]
        return a[row:row + 1]

    big = {
        "w_ada": (w_ada[0], g_w_ada, m_w_ada[0], v_w_ada[0]),
        "w_in": (w_in[0], gshard("in", w_in.shape[1:]), m_w_in[0], v_w_in[0]),
        "w_sb_out": (w_sb_out[0], gshard("sb", w_sb_out.shape[1:]), m_w_sb_out[0], v_w_sb_out[0]),
        "w_fox_out": (w_fox_out[0], gshard("fx", w_fox_out.shape[1:]), m_w_fox_out[0], v_w_fox_out[0]),
        "w_o": (w_o[0], gshard("o", w_o.shape[1:]), m_w_o[0], v_w_o[0]),
        "w_ffn_gate": (w_ffn_gate[0], gshard("fg", w_ffn_gate.shape[1:]), m_w_ffn_gate[0], v_w_ffn_gate[0]),
        "w_ffn_up": (w_ffn_up[0], gshard("fu", w_ffn_up.shape[1:]), m_w_ffn_up[0], v_w_ffn_up[0]),
        "w_ffn_down": (w_ffn_down[0], gshard("fd", w_ffn_down.shape[1:]), m_w_ffn_down[0], v_w_ffn_down[0]),
    }
    order = ["w_ada", "b_ada", "w_in", "b_gate", "b_forget", "w_sb_out", "w_fox_out", "w_o", "ln1_g", "ln1_b",
             "w_ffn_gate", "w_ffn_up", "w_ffn_down", "ln2_g", "ln2_b"]
    grads, deltas, new_ms, new_vs = [], [], [], []
    for nm in order:
        if nm in big:
            w, g, m, v = big[nm]
            d, nm_, nv_ = _adamw(w, g, m, v, "adamw_" + nm)
            grads.append(g[None])
            deltas.append(d[None])
            new_ms.append(nm_[None])
            new_vs.append(nv_[None])
        else:
            grads.append(small(st_sum, nm))
            deltas.append(small(sm[0], nm))
            new_ms.append(small(sm[1], nm))
            new_vs.append(small(sm[2], nm))
    return (loss, gx[None], *grads, *deltas, *new_ms, *new_vs)
```

```python
import functools

import jax
import jax.numpy as jnp
import numpy as np
from jax import lax
from jax.experimental import pallas as pl
from jax.experimental.pallas import tpu as pltpu

F32 = jnp.float32
BF16 = jnp.bfloat16

HEAD_DIM = 64
PAIR = 2 * HEAD_DIM
LN_EPS = 1e-5
ALPHA = 2.0 ** 0.25
ADAM_LR, ADAM_B1, ADAM_B2, ADAM_EPS, ADAM_WD, ADAM_STEP = 0.001, 0.9, 0.999, 1e-08, 0.01, 10
N_DEV = 8
VMEM_LIMIT = 56 * 1024 * 1024
MESH = pl.DeviceIdType.MESH


def _dot(a, b, ca=1, cb=0):
    return lax.dot_general(a, b, (((ca,), (cb,)), ((), ())), preferred_element_type=F32)


def _pick(n, cands):
    for t in cands:
        if n % t == 0:
            return t
    return n


def _params(sem):
    return pltpu.CompilerParams(dimension_semantics=sem, vmem_limit_bytes=VMEM_LIMIT)


MM_BLOCK_BYTES = 40 * 1024 * 1024
LANES = 128


def _divisors(n, cap):
    ds = [d for d in range(LANES, min(n, cap) + 1, LANES) if n % d == 0]
    return sorted(ds, reverse=True) or [n]


def _mm_tiles(M, N, a_row_bytes, b_row_bytes, out_itemsize):
    best = None
    for tm in _divisors(M, 1024):
        for tn in _divisors(N, 2048):
            need = 2 * (tm * a_row_bytes + tn * b_row_bytes + tm * tn * out_itemsize) + tm * tn * 4
            if need <= MM_BLOCK_BYTES and (best is None or (tm * tn, tm) > (best[0] * best[1], best[0])):
                best = (tm, tn)
    assert best is not None, (M, N, a_row_bytes, b_row_bytes)
    return best


def _mm(pairs, mode, out_dtype, name, bias=None, act=None, silu_a=False, exchange=None):
    norm = []
    for p in pairs:
        a, b = p[0], p[1]
        kdim_a = a.shape[0] if mode == 'tn' else a.shape[1]
        K, ka, kb = (p[2], p[3], p[4]) if len(p) > 2 else (kdim_a, 0, 0)
        norm.append((a, b, K, ka, kb))
    a0, b0 = norm[0][0], norm[0][1]
    M = a0.shape[1] if mode == 'tn' else a0.shape[0]
    N = b0.shape[0] if mode == 'nt' else b0.shape[1]
    tm, tn = _mm_tiles(M, N, sum(K * a.dtype.itemsize for a, _, K, _, _ in norm),
                       sum(K * b.dtype.itemsize for _, b, K, _, _ in norm), jnp.dtype(out_dtype).itemsize)
    n_pairs = len(norm)

    in_specs, args = [], []
    for a, b, K, ka, kb in norm:
        if mode == 'tn':
            in_specs.append(pl.BlockSpec((K, tm), lambda i, j, ka=ka: (ka, i)))
        else:
            in_specs.append(pl.BlockSpec((tm, K), lambda i, j, ka=ka: (i, ka)))
        if mode == 'nt':
            in_specs.append(pl.BlockSpec((tn, K), lambda i, j, kb=kb: (j, kb)))
        else:
            in_specs.append(pl.BlockSpec((K, tn), lambda i, j, kb=kb: (kb, j)))
        args += [a, b]
    if bias is not None:
        in_specs.append(pl.BlockSpec((1, tn), lambda i, j: (0, j)))
        args.append(bias)

    ca = 0 if mode == 'tn' else 1
    cb = 1 if mode == 'nt' else 0

    def body(*refs):
        o_ref = refs[-1]
        acc = None
        for p in range(n_pairs):
            av = refs[2 * p][...]
            if silu_a:
                av = av / (1.0 + jnp.exp(-av))
            d = _dot(av.astype(BF16), refs[2 * p + 1][...].astype(BF16), ca, cb)
            acc = d if acc is None else acc + d
        if bias is not None:
            acc = acc + refs[2 * n_pairs][...]
        if act == 'sigmoid':
            acc = 1.0 / (1.0 + jnp.exp(-acc))
        o_ref[...] = acc.astype(out_dtype)

    out_spec = pl.BlockSpec((tm, tn), lambda i, j: (i, j))
    out_shape = jax.ShapeDtypeStruct((M, N), out_dtype)
    if exchange is not None:
        return _call_carrying(body, exchange, name, (M // tm, N // tn), in_specs, [out_spec], [out_shape], [], args)
    return pl.pallas_call(
        body, name=name, grid=(M // tm, N // tn), in_specs=in_specs, out_specs=out_spec, out_shape=out_shape,
        compiler_params=_params(("parallel", "parallel")),
    )(*args)


def _rows_call(body, name, row_ins, vec_ins, row_outs, acc_outs, ts):
    S = row_ins[0].shape[0]
    in_specs = [pl.BlockSpec((ts, a.shape[1]), lambda i: (i, 0)) for a in row_ins]
    in_specs += [pl.BlockSpec(a.shape, lambda i: (0, 0)) for a in vec_ins]
    out_specs = [pl.BlockSpec((ts, c), lambda i: (i, 0)) for c, _ in row_outs]
    out_specs += [pl.BlockSpec(s, lambda i: (0, 0)) for s in acc_outs]
    out_shape = [jax.ShapeDtypeStruct((S, c), dt) for c, dt in row_outs]
    out_shape += [jax.ShapeDtypeStruct(s, F32) for s in acc_outs]
    return pl.pallas_call(
        body, name=name, grid=(S // ts,), in_specs=in_specs, out_specs=out_specs, out_shape=out_shape,
        compiler_params=_params(("arbitrary",)),
    )(*row_ins, *vec_ins)


def _ln_stats(v):
    mu = jnp.mean(v, axis=-1, keepdims=True)
    d = v - mu
    var = jnp.mean(d * d, axis=-1, keepdims=True)
    rstd = lax.rsqrt(var + LN_EPS)
    return d * rstd, rstd


def _ln_bwd(dxhat, xhat, rstd):
    m1 = jnp.mean(dxhat, axis=-1, keepdims=True)
    m2 = jnp.mean(dxhat * xhat, axis=-1, keepdims=True)
    return rstd * (dxhat - m1 - xhat * m2)


def _colsum(v):
    return jnp.sum(v, axis=0, keepdims=True)


def _ln_mod(x, ada8, ts):
    D = x.shape[1]

    def body(x_ref, v_ref, u_ref):
        xhat, _ = _ln_stats(x_ref[...])
        u_ref[...] = (xhat * (1.0 + v_ref[1:2, :]) + v_ref[0:1, :]).astype(BF16)

    return _rows_call(body, "ln_mod", [x], [ada8], [(D, BF16)], [], ts)[0]


def _gate_mix(g_sb, g_fx, y_sb, y_fx, ts):
    D = y_sb.shape[1]

    def body(gs, gf, ys, yf, o_ref):
        o_ref[...] = (gs[...] * ys[...] + gf[...] * yf[...]).astype(BF16)

    return _rows_call(body, "gate_mix", [g_sb, g_fx, y_sb, y_fx], [], [(D, BF16)], [], ts)[0]


def _post_attn(x, mix, ada8, lnp8, ts):
    D = x.shape[1]

    def body(x_ref, mix_ref, v_ref, p_ref, x1_ref, u2_ref):
        r1 = ALPHA * x_ref[...] + v_ref[2:3, :] * mix_ref[...]
        xhat, _ = _ln_stats(r1)
        x1 = xhat * p_ref[0:1, :] + p_ref[1:2, :]
        x1_ref[...] = x1
        xh1, _ = _ln_stats(x1)
        u2_ref[...] = (xh1 * (1.0 + v_ref[4:5, :]) + v_ref[3:4, :]).astype(BF16)

    return _rows_call(body, "post_attn", [x, mix], [ada8, lnp8], [(D, F32), (D, BF16)], [], ts)


def _loss_head(x1, h, target, ada8, lnp8, ts):
    D = x1.shape[1]

    def body(x1_ref, h_ref, t_ref, v_ref, p_ref, dr2_ref, dh_ref, st_ref):
        @pl.when(pl.program_id(0) == 0)
        def _():
            st_ref[...] = jnp.zeros_like(st_ref)

        hv = h_ref[...]
        g2 = v_ref[5:6, :]
        r2 = ALPHA * x1_ref[...] + g2 * hv
        xhat, rstd = _ln_stats(r2)
        y = xhat * p_ref[2:3, :] + p_ref[3:4, :]
        err = y - t_ref[...]
        dy = err * (1.0 / D)
        dr2 = _ln_bwd(dy * p_ref[2:3, :], xhat, rstd)
        dr2_ref[...] = dr2
        dh_ref[...] = (dr2 * g2).astype(BF16)
        st_ref[0:1, :] += _colsum(err * err) * (0.5 / D)
        st_ref[1:2, :] += _colsum(dy * xhat)
        st_ref[2:3, :] += _colsum(dy)
        st_ref[3:4, :] += _colsum(dr2 * hv)

    return _rows_call(body, "loss_head", [x1, h, target], [ada8, lnp8], [(D, F32), (D, BF16)], [(8, D)], ts)


def _mid_bwd(du2, x1, dr2, mix, x, ada8, lnp8, ts):
    D = x.shape[1]

    def body(du2_ref, x1_ref, dr2_ref, mix_ref, x_ref, v_ref, p_ref, dr1_ref, dmix_ref, st_ref):
        @pl.when(pl.program_id(0) == 0)
        def _():
            st_ref[...] = jnp.zeros_like(st_ref)

        du2v = du2_ref[...]
        xh1, rstd1 = _ln_stats(x1_ref[...])
        dx1 = ALPHA * dr2_ref[...] + _ln_bwd(du2v * (1.0 + v_ref[4:5, :]), xh1, rstd1)
        mixv = mix_ref[...]
        g1 = v_ref[2:3, :]
        r1 = ALPHA * x_ref[...] + g1 * mixv
        xhr, rstdr = _ln_stats(r1)
        dr1 = _ln_bwd(dx1 * p_ref[0:1, :], xhr, rstdr)
        dr1_ref[...] = dr1
        dmix_ref[...] = (dr1 * g1).astype(BF16)
        st_ref[0:1, :] += _colsum(du2v * xh1)
        st_ref[1:2, :] += _colsum(du2v)
        st_ref[2:3, :] += _colsum(dx1 * xhr)
        st_ref[3:4, :] += _colsum(dx1)
        st_ref[4:5, :] += _colsum(dr1 * mixv)

    return _rows_call(body, "mid_bwd", [du2, x1, dr2, mix, x], [ada8, lnp8], [(D, F32), (D, BF16)], [(8, D)], ts)


def _first_bwd(du1, x, dr1, ada8, ts):
    D = x.shape[1]

    def body(du1_ref, x_ref, dr1_ref, v_ref, gx_ref, st_ref):
        @pl.when(pl.program_id(0) == 0)
        def _():
            st_ref[...] = jnp.zeros_like(st_ref)

        du1v = du1_ref[...]
        xh0, rstd0 = _ln_stats(x_ref[...])
        gx_ref[...] = ALPHA * dr1_ref[...] + _ln_bwd(du1v * (1.0 + v_ref[1:2, :]), xh0, rstd0)
        st_ref[0:1, :] += _colsum(du1v * xh0)
        st_ref[1:2, :] += _colsum(du1v)

    return _rows_call(body, "first_bwd", [du1, x, dr1], [ada8], [(D, F32)], [(8, D)], ts)


def _split3(v):
    hi = v.astype(BF16)
    r = v - hi.astype(F32)
    mid = r.astype(BF16)
    lo = (r - mid.astype(F32)).astype(BF16)
    return hi, mid, lo


def _fgate_fwd(f, bf_pad, tb):
    S = f.shape[0]

    def body(f_ref, b_ref, fc_ref, carry):
        @pl.when(pl.program_id(0) == 0)
        def _():
            carry[...] = jnp.zeros_like(carry)

        z = f_ref[...] + b_ref[...]
        ls = jnp.minimum(z, 0.0) - jnp.log(1.0 + jnp.exp(-jnp.abs(z)))
        r = lax.broadcasted_iota(jnp.int32, (tb, tb), 0)
        c = lax.broadcasted_iota(jnp.int32, (tb, tb), 1)
        tri = (c <= r).astype(BF16)
        hi, mid, lo = _split3(ls)
        cs = _dot(tri, hi) + _dot(tri, mid) + _dot(tri, lo) + carry[...]
        fc_ref[...] = cs
        carry[...] = cs[tb - 1:tb, :]

    return pl.pallas_call(
        body, name="fgate_fwd", grid=(S // tb,),
        in_specs=[pl.BlockSpec((tb, 128), lambda i: (i, 0)), pl.BlockSpec((1, 128), lambda i: (0, 0))],
        out_specs=pl.BlockSpec((tb, 128), lambda i: (i, 0)),
        out_shape=jax.ShapeDtypeStruct((S, 128), F32),
        scratch_shapes=[pltpu.VMEM((1, 128), F32)],
        compiler_params=_params(("arbitrary",)),
    )(f, bf_pad)


def _fgate_bwd(dfc, f, bf_pad, tb):
    S = f.shape[0]
    nb = S // tb

    def body(d_ref, f_ref, b_ref, df_ref, gb_ref, carry):
        @pl.when(pl.program_id(0) == 0)
        def _():
            carry[...] = jnp.zeros_like(carry)
            gb_ref[...] = jnp.zeros_like(gb_ref)

        r = lax.broadcasted_iota(jnp.int32, (tb, tb), 0)
        c = lax.broadcasted_iota(jnp.int32, (tb, tb), 1)
        tri = (c >= r).astype(BF16)
        hi, mid, lo = _split3(d_ref[...])
        rs = _dot(tri, hi) + _dot(tri, mid) + _dot(tri, lo) + carry[...]
        carry[...] = rs[0:1, :]
        z = f_ref[...] + b_ref[...]
        df = rs * (1.0 / (1.0 + jnp.exp(z)))
        df_ref[...] = df
        gb_ref[0:1, :] += _colsum(df)

    return pl.pallas_call(
        body, name="fgate_bwd", grid=(nb,),
        in_specs=[pl.BlockSpec((tb, 128), lambda i: (nb - 1 - i, 0)),
                  pl.BlockSpec((tb, 128), lambda i: (nb - 1 - i, 0)),
                  pl.BlockSpec((1, 128), lambda i: (0, 0))],
        out_specs=[pl.BlockSpec((tb, 128), lambda i: (nb - 1 - i, 0)), pl.BlockSpec((8, 128), lambda i: (0, 0))],
        out_shape=[jax.ShapeDtypeStruct((S, 128), F32), jax.ShapeDtypeStruct((8, 128), F32)],
        scratch_shapes=[pltpu.VMEM((1, 128), F32)],
        compiler_params=_params(("arbitrary",)),
    )(dfc, f, bf_pad)


def _split2(v):
    hi = v.astype(BF16)
    lo = (v - hi.astype(F32)).astype(BF16)
    return hi, lo


def _head_masks():
    lane = lax.broadcasted_iota(jnp.int32, (1, PAIR), 1)
    m0 = lane < HEAD_DIM
    return m0, jnp.logical_not(m0)


def _sel(mask, v):
    return jnp.where(mask, v, jnp.zeros_like(v))


def _softplus(z):
    return jnp.maximum(z, 0.0) + jnp.log(1.0 + jnp.exp(-jnp.abs(z)))


def _qkv_specs(S, tq, n_pairs, base):
    return [pl.BlockSpec((tq, PAIR), lambda p, i: (i, base + p)),
            pl.BlockSpec((S, PAIR), lambda p, i: (0, base + n_pairs + p)),
            pl.BlockSpec((S, PAIR), lambda p, i: (0, base + 2 * n_pairs + p))]


def _sb_fwd(qkv, n_pairs, base, tq):
    S = qkv.shape[0]
    tk = tq
    scale = HEAD_DIM ** -0.5

    def body(q_ref, k_ref, v_ref, o_ref, t_ref, acc_ref):
        i = pl.program_id(1)
        masks = _head_masks()
        q2 = q_ref[...]
        qm = [_sel(m, q2) for m in masks]
        rowpos = i * tq + lax.broadcasted_iota(jnp.int32, (tq, tk), 0)
        colin = lax.broadcasted_iota(jnp.int32, (tq, tk), 1)
        upper = (lax.broadcasted_iota(jnp.int32, (tk, tk), 0) > lax.broadcasted_iota(jnp.int32, (tk, tk), 1)).astype(BF16)
        acc_ref[...] = jnp.zeros_like(acc_ref)

        def step(jj, carry):
            j = i - jj
            off = pl.multiple_of(j * tk, tk)
            k2 = k_ref[pl.ds(off, tk), :]
            v2 = v_ref[pl.ds(off, tk), :]
            mask = (j * tk + colin) < rowpos
            out = None
            new = []
            for h in heads:
                z = _dot(qm[h], k2, 1, 1) * scale
                sp = _softplus(z)
                lg = jnp.where(mask, -sp, 0.0)
                hi, lo = _split2(lg)
                suf = _dot(hi, upper) + _dot(lo, upper)
                a = jnp.where(mask, jnp.exp(z - sp + suf + carry[h]), 0.0)
                d = _dot(a.astype(BF16), _sel(masks[h], v2))
                out = d if out is None else out + d
                new.append(carry[h] + jnp.sum(lg, axis=-1, keepdims=True))
            acc_ref[...] += out
            return tuple(new)

        zero = jnp.zeros((tq, 1), F32)
        r0, r1 = lax.fori_loop(0, i + 1, step, (zero, zero))
        o_ref[...] = acc_ref[...].astype(BF16)
        t_ref[...] = jnp.where(masks[0], r0, r1)

    W = n_pairs * PAIR
    return pl.pallas_call(
        body, name="sb_fwd", grid=(n_pairs, S // tq), in_specs=_qkv_specs(S, tq, n_pairs, base),
        out_specs=[pl.BlockSpec((tq, PAIR), lambda p, i: (i, p)), pl.BlockSpec((tq, PAIR), lambda p, i: (i, p))],
        out_shape=[jax.ShapeDtypeStruct((S, W), BF16), jax.ShapeDtypeStruct((S, W), F32)],
        scratch_shapes=[pltpu.VMEM((tq, PAIR), F32)],
        compiler_params=_params(("parallel", "arbitrary")),
    )(qkv, qkv, qkv)


def _sb_bwd(qkv, tot, do, n_pairs, base, tq):
    S = qkv.shape[0]
    tk = tq
    nq = S // tq
    scale = HEAD_DIM ** -0.5

    def body(q_ref, k_ref, v_ref, t_ref, do_ref, dq_ref, dk_ref, dv_ref, dq_acc, dk_acc, dv_acc):
        i = pl.program_id(1)
        masks = _head_masks()

        @pl.when(i == 0)
        def _():
            dk_acc[...] = jnp.zeros_like(dk_acc)
            dv_acc[...] = jnp.zeros_like(dv_acc)

        q2 = q_ref[...]
        do2 = do_ref[...]
        qm = [_sel(m, q2) for m in masks]
        dom = [_sel(m, do2) for m in masks]
        t2 = t_ref[...]
        tot_h = [t2[:, 0:1], t2[:, HEAD_DIM:HEAD_DIM + 1]]
        rowpos = i * tq + lax.broadcasted_iota(jnp.int32, (tq, tk), 0)
        colin = lax.broadcasted_iota(jnp.int32, (tq, tk), 1)
        r_i = lax.broadcasted_iota(jnp.int32, (tk, tk), 0)
        c_i = lax.broadcasted_iota(jnp.int32, (tk, tk), 1)
        upper = (r_i > c_i).astype(BF16)
        lower = (r_i < c_i).astype(BF16)
        dq_acc[...] = jnp.zeros_like(dq_acc)

        def step(j, carry):
            off = pl.multiple_of(j * tk, tk)
            k2 = k_ref[pl.ds(off, tk), :]
            v2 = v_ref[pl.ds(off, tk), :]
            mask = (j * tk + colin) < rowpos
            dq = None
            dk = None
            dv = None
            new = []
            for h in heads:
                cum_l, cum_g = carry[2 * h], carry[2 * h + 1]
                z = _dot(qm[h], k2, 1, 1) * scale
                sp = _softplus(z)
                lg = jnp.where(mask, -sp, 0.0)
                hi, lo = _split2(lg)
                suf = _dot(hi, upper) + _dot(lo, upper)
                row_l = jnp.sum(lg, axis=-1, keepdims=True)
                later = tot_h[h] - cum_l - row_l
                a = jnp.where(mask, jnp.exp(z - sp + suf + later), 0.0)
                da = _dot(dom[h], v2, 1, 1)
                g = da * a
                ghi, glo = _split2(g)
                pre = _dot(ghi, lower) + _dot(glo, lower) + cum_g
                one_m_beta = jnp.exp(-sp)
                dz = jnp.where(mask, g * one_m_beta - (1.0 - one_m_beta) * pre, 0.0)
                dzb = (dz * scale).astype(BF16)
                d1 = _dot(dzb, _sel(masks[h], k2))
                d2 = _dot(dzb, qm[h], 0, 0)
                d3 = _dot(a.astype(BF16), dom[h], 0, 0)
                dq = d1 if dq is None else dq + d1
                dk = d2 if dk is None else dk + d2
                dv = d3 if dv is None else dv + d3
                new += [cum_l + row_l, cum_g + jnp.sum(g, axis=-1, keepdims=True)]
            dq_acc[...] += dq
            dk_acc[pl.ds(off, tk), :] += dk
            dv_acc[pl.ds(off, tk), :] += dv
            return tuple(new)

        zero = jnp.zeros((tq, 1), F32)
        lax.fori_loop(0, i + 1, step, (zero, zero, zero, zero))
        dq_ref[...] = dq_acc[...].astype(BF16)

        @pl.when(i == nq - 1)
        def _():
            dk_ref[...] = dk_acc[...].astype(BF16)
            dv_ref[...] = dv_acc[...].astype(BF16)

    W = n_pairs * PAIR
    in_specs = _qkv_specs(S, tq, n_pairs, base) + [
        pl.BlockSpec((tq, PAIR), lambda p, i: (i, p)),
        pl.BlockSpec((tq, PAIR), lambda p, i: (i, p))]
    out_specs = [pl.BlockSpec((tq, PAIR), lambda p, i: (i, p)),
                 pl.BlockSpec((S, PAIR), lambda p, i: (0, p)),
                 pl.BlockSpec((S, PAIR), lambda p, i: (0, p))]
    dq, dk, dv = pl.pallas_call(
        body, name="sb_bwd", grid=(n_pairs, nq), in_specs=in_specs, out_specs=out_specs,
        out_shape=[jax.ShapeDtypeStruct((S, W), BF16)] * 3,
        scratch_shapes=[pltpu.VMEM((tq, PAIR), F32), pltpu.VMEM((S, PAIR), F32), pltpu.VMEM((S, PAIR), F32)],
        compiler_params=_params(("parallel", "arbitrary")),
    )(qkv, qkv, qkv, tot, do)
    return dq, dk, dv


NEG = -1e30


def _fox_specs(S, tq, n_pairs, base):
    return _qkv_specs(S, tq, n_pairs, base) + [
        pl.BlockSpec((tq, PAIR), lambda p, i: (i, p)),
        pl.BlockSpec((1, S // tq, 8, tq), lambda p, i: (p, 0, 0, 0))]


def _fox_fwd(qkv, fcx, fcr, n_pairs, base, tq):
    S = qkv.shape[0]
    tk = tq
    scale = HEAD_DIM ** -0.5

    def body(q_ref, k_ref, v_ref, fq_ref, fk_ref, o_ref, lse_ref, acc_ref):
        i = pl.program_id(1)
        masks = _head_masks()
        q2 = q_ref[...]
        qm = [_sel(m, q2) for m in masks]
        fq2 = fq_ref[...]
        fq = [fq2[:, 0:1], fq2[:, HEAD_DIM:HEAD_DIM + 1]]
        rowpos = i * tq + lax.broadcasted_iota(jnp.int32, (tq, tk), 0)
        colin = lax.broadcasted_iota(jnp.int32, (tq, tk), 1)
        acc_ref[...] = jnp.zeros_like(acc_ref)

        def step(j, carry):
            off = pl.multiple_of(j * tk, tk)
            k2 = k_ref[pl.ds(off, tk), :]
            v2 = v_ref[pl.ds(off, tk), :]
            fk2 = fk_ref[0, j]
            mask = (j * tk + colin) <= rowpos
            out = None
            new = []
            alphas = []
            for h in heads:
                m_old, l_old = carry[2 * h], carry[2 * h + 1]
                s = _dot(qm[h], k2, 1, 1) * scale + fq[h] - fk2[h:h + 1, :]
                s = jnp.where(mask, s, NEG)
                m_new = jnp.maximum(m_old, jnp.max(s, axis=-1, keepdims=True))
                p = jnp.exp(s - m_new)
                alpha = jnp.exp(m_old - m_new)
                alphas.append(alpha)
                d = _dot(p.astype(BF16), _sel(masks[h], v2))
                out = d if out is None else out + d
                new += [m_new, alpha * l_old + jnp.sum(p, axis=-1, keepdims=True)]
            acc_ref[...] = acc_ref[...] * jnp.where(masks[0], alphas[0], alphas[1]) + out
            return tuple(new)

        zero = jnp.zeros((tq, 1), F32)
        neg = jnp.full((tq, 1), NEG, F32)
        m0, l0, m1, l1 = lax.fori_loop(0, i + 1, step, (neg, zero, neg, zero))
        o_ref[...] = (acc_ref[...] / jnp.where(masks[0], l0, l1)).astype(BF16)
        lse_ref[...] = jnp.where(masks[0], m0 + jnp.log(l0), m1 + jnp.log(l1))

    W = n_pairs * PAIR
    return pl.pallas_call(
        body, name="fox_fwd", grid=(n_pairs, S // tq), in_specs=_fox_specs(S, tq, n_pairs, base),
        out_specs=[pl.BlockSpec((tq, PAIR), lambda p, i: (i, p)), pl.BlockSpec((tq, PAIR), lambda p, i: (i, p))],
        out_shape=[jax.ShapeDtypeStruct((S, W), BF16), jax.ShapeDtypeStruct((S, W), F32)],
        scratch_shapes=[pltpu.VMEM((tq, PAIR), F32)],
        compiler_params=_params(("parallel", "arbitrary")),
    )(qkv, qkv, qkv, fcx, fcr)


def _fox_bwd(qkv, fcx, fcr, o, lse, do, n_pairs, base, tq):
    S = qkv.shape[0]
    tk = tq
    nq = S // tq
    scale = HEAD_DIM ** -0.5

    def body(q_ref, k_ref, v_ref, fq_ref, fk_ref, o_ref, lse_ref, do_ref,
             dq_ref, dk_ref, dv_ref, dfq_ref, dfk_ref, dq_acc, dk_acc, dv_acc):
        i = pl.program_id(1)
        masks = _head_masks()

        @pl.when(i == 0)
        def _():
            dk_acc[...] = jnp.zeros_like(dk_acc)
            dv_acc[...] = jnp.zeros_like(dv_acc)
            dfk_ref[...] = jnp.zeros_like(dfk_ref)

        q2 = q_ref[...]
        do2 = do_ref[...]
        qm = [_sel(m, q2) for m in masks]
        dom = [_sel(m, do2) for m in masks]
        fq2 = fq_ref[...]
        fq = [fq2[:, 0:1], fq2[:, HEAD_DIM:HEAD_DIM + 1]]
        l2 = lse_ref[...]
        lse_h = [l2[:, 0:1], l2[:, HEAD_DIM:HEAD_DIM + 1]]
        prod = do2.astype(F32) * o_ref[...].astype(F32)
        delta = [jnp.sum(jnp.where(m, prod, 0.0), axis=-1, keepdims=True) for m in masks]
        rowpos = i * tq + lax.broadcasted_iota(jnp.int32, (tq, tk), 0)
        colin = lax.broadcasted_iota(jnp.int32, (tq, tk), 1)
        dq_acc[...] = jnp.zeros_like(dq_acc)

        def step(j, carry):
            off = pl.multiple_of(j * tk, tk)
            k2 = k_ref[pl.ds(off, tk), :]
            v2 = v_ref[pl.ds(off, tk), :]
            fk2 = fk_ref[0, j]
            mask = (j * tk + colin) <= rowpos
            dq = None
            dk = None
            dv = None
            new = []
            dfk_rows = []
            for h in heads:
                s = _dot(qm[h], k2, 1, 1) * scale + fq[h] - fk2[h:h + 1, :]
                p = jnp.where(mask, jnp.exp(s - lse_h[h]), 0.0)
                dp = _dot(dom[h], v2, 1, 1)
                ds = p * (dp - delta[h])
                dsb = (ds * scale).astype(BF16)
                d1 = _dot(dsb, _sel(masks[h], k2))
                d2 = _dot(dsb, qm[h], 0, 0)
                d3 = _dot(p.astype(BF16), dom[h], 0, 0)
                dq = d1 if dq is None else dq + d1
                dk = d2 if dk is None else dk + d2
                dv = d3 if dv is None else dv + d3
                new.append(carry[h] + jnp.sum(ds, axis=-1, keepdims=True))
                dfk_rows.append(jnp.sum(ds, axis=0, keepdims=True))
            dq_acc[...] += dq
            dk_acc[pl.ds(off, tk), :] += dk
            dv_acc[pl.ds(off, tk), :] += dv
            dfk_ref[0, j, 0:1, :] += dfk_rows[0]
            dfk_ref[0, j, 1:2, :] += dfk_rows[1]
            return tuple(new)

        zero = jnp.zeros((tq, 1), F32)
        r0, r1 = lax.fori_loop(0, i + 1, step, (zero, zero))
        dq_ref[...] = dq_acc[...].astype(BF16)
        dfq_ref[...] = jnp.where(masks[0], r0, r1)

        @pl.when(i == nq - 1)
        def _():
            dk_ref[...] = dk_acc[...].astype(BF16)
            dv_ref[...] = dv_acc[...].astype(BF16)

    W = n_pairs * PAIR
    in_specs = _fox_specs(S, tq, n_pairs, base) + [
        pl.BlockSpec((tq, PAIR), lambda p, i: (i, p)),
        pl.BlockSpec((tq, PAIR), lambda p, i: (i, p)),
        pl.BlockSpec((tq, PAIR), lambda p, i: (i, p))]
    out_specs = [pl.BlockSpec((tq, PAIR), lambda p, i: (i, p)),
                 pl.BlockSpec((S, PAIR), lambda p, i: (0, p)),
                 pl.BlockSpec((S, PAIR), lambda p, i: (0, p)),
                 pl.BlockSpec((tq, PAIR), lambda p, i: (i, p)),
                 pl.BlockSpec((1, nq, 8, tk), lambda p, i: (p, 0, 0, 0))]
    return pl.pallas_call(
        body, name="fox_bwd", grid=(n_pairs, nq), in_specs=in_specs, out_specs=out_specs,
        out_shape=[jax.ShapeDtypeStruct((S, W), BF16)] * 3
        + [jax.ShapeDtypeStruct((S, W), F32), jax.ShapeDtypeStruct((n_pairs, nq, 8, tk), F32)],
        scratch_shapes=[pltpu.VMEM((tq, PAIR), F32), pltpu.VMEM((S, PAIR), F32), pltpu.VMEM((S, PAIR), F32)],
        compiler_params=_params(("parallel", "arbitrary")),
    )(qkv, qkv, qkv, fcx, fcr, o, lse, do)


RC = 32
VANISH = -104.0


def _chunks(n_rows, fn):
    for ci in range(n_rows // RC):
        fn(ci * RC)


def _wide(v, tk):
    return v if tk == 128 else jnp.tile(v, (1, tk // 128))


def _rep(col):
    return jnp.broadcast_to(col, (col.shape[0], 128))


def _per_head(blk, masks):
    sw = pltpu.roll(blk, HEAD_DIM, axis=1)
    return jnp.where(masks[0], blk, sw), jnp.where(masks[0], sw, blk)


def _fill_masked(dst_ref, src_ref, masks, mul=None, ones_lane=None):
    v = src_ref[...]
    if mul is not None:
        v = v * mul
    lane = lax.broadcasted_iota(jnp.int32, (1, PAIR), 1)
    for h in range(2):
        m = _sel(masks[h], v)
        if ones_lane is not None:
            m = jnp.where(lane == ones_lane[h], jnp.ones_like(m), m)
        dst_ref[h] = m


def _head_norms(v, masks):
    sq = v.astype(F32)
    sq = sq * sq
    return [jnp.sqrt(jnp.sum(jnp.where(m, sq, 0.0), axis=-1, keepdims=True)) for m in masks]


def _largest_key_norm(kmax_ref, k_ref, masks):
    for h, n in enumerate(_head_norms(k_ref[...], masks)):
        kmax_ref[h] = jnp.broadcast_to(jnp.max(n, axis=0, keepdims=True), (8, 128))


def _score_bound(q_scaled, kmax_ref, masks):
    return [_rep(n) * kmax_ref[h][0:1, :] for h, n in enumerate(_head_norms(q_scaled, masks))]


def _tri(tk, cmp):
    r = lax.broadcasted_iota(jnp.int32, (tk, tk), 0)
    c = lax.broadcasted_iota(jnp.int32, (tk, tk), 1)
    return cmp(r, c).astype(BF16)


def _diag_mask(r0, tk, strict):
    row = r0 + lax.broadcasted_iota(jnp.int32, (RC, tk), 0)
    col = lax.broadcasted_iota(jnp.int32, (RC, tk), 1)
    return (col < row) if strict else (col <= row)


def _peer_copies(src_ref, land_ref, send_sems, recv_sems, scatter, receive_side):
    x, y, c = lax.axis_index("x"), lax.axis_index("y"), lax.axis_index("c")
    me = 4 * x + 2 * y + c
    copies = []
    for k in range(1, N_DEV):
        px, py, pc = (1 - x if k & 4 else x), (1 - y if k & 2 else y), (1 - c if k & 1 else c)
        slot = 4 * px + 2 * py + pc
        copies.append(pltpu.make_async_remote_copy(
            src_ref=src_ref.at[slot] if scatter else src_ref,
            dst_ref=land_ref.at[slot] if receive_side else land_ref.at[me],
            send_sem=send_sems.at[k - 1], recv_sem=recv_sems.at[k - 1], device_id=(px, py, pc), device_id_type=MESH))
    return copies


def _call_carrying(body, exchange, name, grid, in_specs, out_specs, out_shape, scratch_shapes, args):
    if exchange is None:
        return pl.pallas_call(body, name=name, grid=grid, in_specs=in_specs, out_specs=out_specs, out_shape=out_shape,
                              scratch_shapes=scratch_shapes, compiler_params=_params(("parallel", "arbitrary")))(*args)
    src, scatter = exchange
    n_in, n_out = len(in_specs), len(out_specs)

    def carrying(*refs):
        src_ref, land_ref = refs[n_in], refs[n_in + 1 + n_out]
        send_sems, recv_sems = refs[-2], refs[-1]
        first = jnp.logical_and(pl.program_id(0) == 0, pl.program_id(1) == 0)
        last = jnp.logical_and(pl.program_id(0) == grid[0] - 1, pl.program_id(1) == grid[1] - 1)

        @pl.when(first)
        def _():
            for cp in _peer_copies(src_ref, land_ref, send_sems, recv_sems, scatter, False):
                cp.start()

        body(*refs[:n_in], *refs[n_in + 1:n_in + 1 + n_out], *refs[n_in + 2 + n_out:-2])

        @pl.when(last)
        def _():
            for cp in _peer_copies(src_ref, land_ref, send_sems, recv_sems, scatter, True):
                cp.wait_send()
                cp.wait_recv()

    any_space = pl.BlockSpec(memory_space=pl.ANY)
    land = jax.ShapeDtypeStruct((N_DEV,) + src.shape[-2:], src.dtype)
    return pl.pallas_call(
        carrying, name=name, grid=grid, in_specs=list(in_specs) + [any_space],
        out_specs=list(out_specs) + [any_space], out_shape=list(out_shape) + [land],
        scratch_shapes=list(scratch_shapes) + [pltpu.SemaphoreType.DMA((N_DEV - 1,)), pltpu.SemaphoreType.DMA((N_DEV - 1,))],
        compiler_params=_params(("arbitrary", "arbitrary")))(*args, src)


def _staggered(bodies):
    active, waiting = [], list(bodies)
    while waiting or active:
        if waiting:
            active.append(waiting.pop(0))
        for g in list(active):
            try:
                next(g)
            except StopIteration:
                active.remove(g)


def _streams(tile, j, diag, slot):
    return [tile(j, diag, slot, (0, 1))]


def _tiles(i, tile):
    def step(jj, carry):
        _staggered(_streams(tile, 2 * jj, False, 0) + _streams(tile, 2 * jj + 1, False, 1))
        return carry
    lax.fori_loop(0, (i - 1) // 2, step, 0)

    @pl.when(jnp.logical_and(i >= 1, (i - 1) % 2 == 1))
    def _():
        _staggered(_streams(tile, i - 2, False, 0))

    @pl.when(i >= 1)
    def _():
        _staggered(_streams(tile, i - 1, False, 0) + _streams(tile, i, True, 1))

    @pl.when(i == 0)
    def _():
        _staggered(_streams(tile, i, True, 1))


def _tiles_reversed(i, tile, keep_going):
    @pl.when(i == 0)
    def _():
        _staggered(_streams(tile, i, True, 0))

    @pl.when(i >= 1)
    def _():
        _staggered(_streams(tile, i, True, 0) + _streams(tile, i - 1, False, 1))

    pairs = (i - 1) // 2

    def cond(carry):
        jj, go = carry
        return jnp.logical_and(jj < pairs, go)

    def step(carry):
        jj, _ = carry
        _staggered(_streams(tile, i - 2 - 2 * jj, False, 0) + _streams(tile, i - 3 - 2 * jj, False, 1))
        return jj + 1, keep_going(jnp.maximum(i - 4 - 2 * jj, 0))

    jj, go = lax.while_loop(cond, step, (jnp.int32(0), keep_going(jnp.maximum(i - 2, 0))))

    @pl.when(jnp.logical_and(jnp.logical_and(i >= 1, (i - 1) % 2 == 1), jnp.logical_and(jj == pairs, go)))
    def _():
        _staggered(_streams(tile, 0, False, 0))


def _sb_fwd2(qkv, n_pairs, base, tq):
    S = qkv.shape[0]
    tk = tq
    scale = HEAD_DIM ** -0.5

    def body(q_ref, k_ref, v_ref, o_ref, t_ref, z_ref, hi_ref, suf_ref, p_ref, r_ref, acc_ref, vm_ref):
        i = pl.program_id(1)
        masks = _head_masks()

        @pl.when(i == 0)
        def _():
            _fill_masked(vm_ref, v_ref, masks)

        q2 = q_ref[...] * scale
        qm = [_sel(m, q2) for m in masks]
        incl = _tri(tk, lambda r, c: r >= c)
        r_ref[...] = jnp.zeros_like(r_ref)
        acc_ref[...] = jnp.zeros_like(acc_ref)

        def tile(j, diag, slot, heads):
            off = pl.multiple_of(j * tk, tk)
            k2 = k_ref[pl.ds(off, tk), :]
            v2 = v_ref[pl.ds(off, tk), :]
            for h in heads:
                z_ref[2 * slot + h] = _dot(qm[h], k2, 1, 1)
            yield
            for h in heads:
                def split(r0, h=h):
                    rows = pl.ds(r0, RC)
                    lg = -_softplus(z_ref[2 * slot + h, rows, :])
                    if diag:
                        lg = jnp.where(_diag_mask(r0, tk, True), lg, 0.0)
                    hi_ref[2 * slot + h, rows, :] = lg.astype(BF16)
                _chunks(tq, split)
            yield
            for h in heads:
                suf_ref[2 * slot + h] = _dot(hi_ref[2 * slot + h], incl)
            yield
            for h in heads:
                def weights(r0, h=h):
                    rows = pl.ds(r0, RC)
                    a = jnp.exp(z_ref[2 * slot + h, rows, :] + suf_ref[2 * slot + h, rows, :] + _wide(r_ref[h, rows, :], tk))
                    if diag:
                        a = jnp.where(_diag_mask(r0, tk, True), a, 0.0)
                    p_ref[2 * slot + h, rows, :] = a.astype(BF16)
                _chunks(tq, weights)
            yield
            keys = pl.ds(off, tk)
            for h in heads:
                acc_ref[...] += _dot(p_ref[2 * slot + h], vm_ref[h, keys, :])
            for h in heads:
                r_ref[h] += _rep(suf_ref[2 * slot + h, :, 0:1])

        _tiles_reversed(i, tile, lambda nearest: jnp.max(r_ref[...]) >= VANISH)
        o_ref[...] = acc_ref[...].astype(BF16)
        t_ref[...] = acc_ref[...]

    W = n_pairs * PAIR
    return pl.pallas_call(
        body, name="sb_fwd", grid=(n_pairs, S // tq), in_specs=_qkv_specs(S, tq, n_pairs, base),
        out_specs=[pl.BlockSpec((tq, PAIR), lambda p, i: (i, p)), pl.BlockSpec((tq, PAIR), lambda p, i: (i, p))],
        out_shape=[jax.ShapeDtypeStruct((S, W), BF16), jax.ShapeDtypeStruct((S, W), F32)],
        scratch_shapes=[pltpu.VMEM((4, tq, tk), F32), pltpu.VMEM((4, tq, tk), BF16),
                        pltpu.VMEM((4, tq, tk), F32), pltpu.VMEM((4, tq, tk), BF16), pltpu.VMEM((2, tq, 128), F32),
                        pltpu.VMEM((tq, PAIR), F32), pltpu.VMEM((2, S, PAIR), BF16)],
        compiler_params=_params(("parallel", "arbitrary")),
    )(qkv, qkv, qkv)


def _sb_bwd2(qkv, o32, do, n_pairs, base, tq):
    S = qkv.shape[0]
    tk = tq
    nq = S // tq
    scale = HEAD_DIM ** -0.5

    def body(q_ref, k_ref, v_ref, o_ref, do_ref, dq_ref, dk_ref, dv_ref,
             z_ref, g_ref, omb_ref, cum_ref, hi_ref, lo_ref, a_ref, dz_ref,
             r_ref, cg_ref, dl_ref, dq_acc, dk_acc, dv_acc, ks_ref):
        i = pl.program_id(1)
        masks = _head_masks()

        @pl.when(i == 0)
        def _():
            dk_acc[...] = jnp.zeros_like(dk_acc)
            dv_acc[...] = jnp.zeros_like(dv_acc)
            _fill_masked(ks_ref, k_ref, masks, mul=scale)

        q2 = q_ref[...] * scale
        do2 = do_ref[...]
        qm = [_sel(m, q2) for m in masks]
        dom = [_sel(m, do2) for m in masks]
        prod = do2.astype(F32) * o_ref[...]
        for h in range(2):
            dl_ref[h] = _rep(jnp.sum(jnp.where(masks[h], prod, 0.0), axis=-1, keepdims=True))
        suffix = _tri(tk, lambda r, c: r >= c)
        r_ref[...] = jnp.zeros_like(r_ref)
        cg_ref[...] = jnp.zeros_like(cg_ref)
        dq_acc[...] = jnp.zeros_like(dq_acc)

        def tile(j, diag, slot, heads):
            off = pl.multiple_of(j * tk, tk)
            k2 = k_ref[pl.ds(off, tk), :]
            v2 = v_ref[pl.ds(off, tk), :]
            for h in heads:
                z_ref[2 * slot + h] = _dot(qm[h], k2, 1, 1)
                g_ref[2 * slot + h] = _dot(dom[h], v2, 1, 1)
            yield
            for h in heads:
                def split(r0, h=h):
                    rows = pl.ds(r0, RC)
                    sp = _softplus(z_ref[2 * slot + h, rows, :])
                    omb_ref[2 * slot + h, rows, :] = jnp.exp(-sp)
                    lg = -sp
                    if diag:
                        lg = jnp.where(_diag_mask(r0, tk, True), lg, 0.0)
                    hi_ref[2 * slot + h, rows, :] = lg.astype(BF16)
                _chunks(tq, split)
            yield
            for h in heads:
                cum_ref[2 * slot + h] = _dot(hi_ref[2 * slot + h], suffix)
            yield
            for h in heads:
                def weights(r0, h=h):
                    rows = pl.ds(r0, RC)
                    a = jnp.exp(z_ref[2 * slot + h, rows, :] + cum_ref[2 * slot + h, rows, :] + _wide(r_ref[h, rows, :], tk))
                    if diag:
                        a = jnp.where(_diag_mask(r0, tk, True), a, 0.0)
                    ab = a.astype(BF16)
                    g = g_ref[2 * slot + h, rows, :] * ab.astype(F32)
                    g_ref[2 * slot + h, rows, :] = g
                    a_ref[2 * slot + h, rows, :] = ab
                    hi, lo = _split2(g)
                    hi_ref[2 * slot + h, rows, :] = hi
                    lo_ref[2 * slot + h, rows, :] = lo
                _chunks(tq, weights)
            for h in heads:
                r_ref[h] += _rep(cum_ref[2 * slot + h, :, 0:1])
            yield
            for h in heads:
                cum_ref[2 * slot + h] = _dot(hi_ref[2 * slot + h], suffix) + _dot(lo_ref[2 * slot + h], suffix)
            yield
            for h in heads:
                def dscore(r0, h=h):
                    rows = pl.ds(r0, RC)
                    g = g_ref[2 * slot + h, rows, :]
                    from_here = cum_ref[2 * slot + h, rows, :] + _wide(cg_ref[h, rows, :], tk)
                    before = _wide(dl_ref[h, rows, :], tk) - from_here
                    omb = omb_ref[2 * slot + h, rows, :]
                    dz = g * omb - (1.0 - omb) * before
                    if diag:
                        dz = jnp.where(_diag_mask(r0, tk, True), dz, 0.0)
                    dz_ref[2 * slot + h, rows, :] = dz.astype(BF16)
                _chunks(tq, dscore)
            for h in heads:
                cg_ref[h] += _rep(cum_ref[2 * slot + h, :, 0:1])
            yield
            keys = pl.ds(off, tk)
            for h in heads:
                dq_acc[...] += _dot(dz_ref[2 * slot + h], ks_ref[h, keys, :])
                dk_acc[keys, :] += _dot(dz_ref[2 * slot + h], qm[h], 0, 0)
                dv_acc[keys, :] += _dot(a_ref[2 * slot + h], dom[h], 0, 0)

        _tiles_reversed(i, tile, lambda nearest: jnp.max(r_ref[...]) >= VANISH)
        dq_ref[...] = dq_acc[...].astype(BF16)

        @pl.when(i == nq - 1)
        def _():
            dk_ref[...] = dk_acc[...].astype(BF16)
            dv_ref[...] = dv_acc[...].astype(BF16)

    W = n_pairs * PAIR
    in_specs = _qkv_specs(S, tq, n_pairs, base) + [
        pl.BlockSpec((tq, PAIR), lambda p, i: (i, p)),
        pl.BlockSpec((tq, PAIR), lambda p, i: (i, p))]
    out_specs = [pl.BlockSpec((tq, PAIR), lambda p, i: (i, p)),
                 pl.BlockSpec((S, PAIR), lambda p, i: (0, p)),
                 pl.BlockSpec((S, PAIR), lambda p, i: (0, p))]
    big, stat = (4, tq, tk), (2, tq, 128)
    return pl.pallas_call(
        body, name="sb_bwd", grid=(n_pairs, nq), in_specs=in_specs, out_specs=out_specs,
        out_shape=[jax.ShapeDtypeStruct((S, W), BF16)] * 3,
        scratch_shapes=[pltpu.VMEM(big, F32)] * 4 + [pltpu.VMEM(big, BF16)] * 4 + [pltpu.VMEM(stat, F32)] * 3
        + [pltpu.VMEM((tq, PAIR), F32), pltpu.VMEM((S, PAIR), F32), pltpu.VMEM((S, PAIR), F32),
           pltpu.VMEM((2, S, PAIR), BF16)],
        compiler_params=_params(("parallel", "arbitrary")),
    )(qkv, qkv, qkv, o32, do)


def _fox_fwd2(qkv, fcx, fcr, n_pairs, base, tq, exchange=None):
    S = qkv.shape[0]
    tk = tq
    scale = HEAD_DIM ** -0.5
    spare = (HEAD_DIM, 0)

    def body(q_ref, k_ref, v_ref, fq_ref, fk_ref, o_ref, lse_ref, s_ref, p_ref, m_ref, al_ref, fqr_ref, acc_ref, vm_ref):
        i = pl.program_id(1)
        masks = _head_masks()

        @pl.when(i == 0)
        def _():
            _fill_masked(vm_ref, v_ref, masks, ones_lane=spare)

        q2 = q_ref[...] * scale
        qm = [_sel(m, q2) for m in masks]
        f0, f1 = _per_head(fq_ref[...], masks)
        fqr_ref[0] = f0
        fqr_ref[1] = f1
        m_ref[...] = jnp.full(m_ref.shape, NEG, F32)
        acc_ref[...] = jnp.zeros_like(acc_ref)

        def tile(j, diag, slot, heads):
            off = pl.multiple_of(j * tk, tk)
            k2 = k_ref[pl.ds(off, tk), :]
            v2 = v_ref[pl.ds(off, tk), :]
            fk2 = fk_ref[0, j]
            for h in heads:
                s_ref[2 * slot + h] = _dot(qm[h], k2, 1, 1)
            yield
            for h in heads:
                fk_row = fk2[h:h + 1, :]

                def probs(r0, h=h, fk_row=fk_row):
                    rows = pl.ds(r0, RC)
                    sv = s_ref[2 * slot + h, rows, :] - fk_row
                    if diag:
                        sv = jnp.where(_diag_mask(r0, tk, False), sv, NEG)
                    fq = fqr_ref[h, rows, :]
                    m_prev = m_ref[h, rows, :]
                    m_new = jnp.maximum(m_prev, jnp.max(sv, axis=-1, keepdims=True) + fq)
                    p_ref[2 * slot + h, rows, :] = jnp.exp(sv + _wide(fq - m_new, tk)).astype(BF16)
                    al_ref[2 * slot + h, rows, :] = jnp.exp(m_prev - m_new)
                    m_ref[h, rows, :] = m_new
                _chunks(tq, probs)
            yield
            for h in heads:
                acc_ref[h] = acc_ref[h] * al_ref[2 * slot + h] + _dot(p_ref[2 * slot + h], vm_ref[h, pl.ds(off, tk), :])

        _tiles(i, tile)
        a0, a1 = acc_ref[0], acc_ref[1]
        l0 = _rep(a0[:, spare[0]:spare[0] + 1])
        l1 = _rep(a1[:, spare[1]:spare[1] + 1])
        o_ref[...] = jnp.where(masks[0], a0 / l0, a1 / l1).astype(BF16)
        lse_ref[...] = jnp.where(masks[0], m_ref[0] + jnp.log(l0), m_ref[1] + jnp.log(l1))

    W = n_pairs * PAIR
    return _call_carrying(
        body, exchange, "fox_fwd", (n_pairs, S // tq), _fox_specs(S, tq, n_pairs, base),
        [pl.BlockSpec((tq, PAIR), lambda p, i: (i, p)), pl.BlockSpec((tq, PAIR), lambda p, i: (i, p))],
        [jax.ShapeDtypeStruct((S, W), BF16), jax.ShapeDtypeStruct((S, W), F32)],
        [pltpu.VMEM((4, tq, tk), F32), pltpu.VMEM((4, tq, tk), BF16), pltpu.VMEM((2, tq, 128), F32),
         pltpu.VMEM((4, tq, 128), F32), pltpu.VMEM((2, tq, 128), F32), pltpu.VMEM((2, tq, 128), F32),
         pltpu.VMEM((2, S, PAIR), BF16)],
        (qkv, qkv, qkv, fcx, fcr))


def _fox_bwd2(qkv, fcx, fcr, o, lse, do, n_pairs, base, tq, exchange=None):
    S = qkv.shape[0]
    tk = tq
    nq = S // tq
    scale = HEAD_DIM ** -0.5

    def body(q_ref, k_ref, v_ref, fq_ref, fk_ref, o_ref, lse_ref, do_ref,
             dq_ref, dk_ref, dv_ref, dfq_ref, dfk_ref,
             s_ref, dp_ref, p_ref, ds_ref, row_ref, dl_ref, dfq_acc, col_ref, dq_acc, dk_acc, dv_acc, ks_ref):
        i = pl.program_id(1)
        masks = _head_masks()

        @pl.when(i == 0)
        def _():
            dk_acc[...] = jnp.zeros_like(dk_acc)
            dv_acc[...] = jnp.zeros_like(dv_acc)
            dfk_ref[...] = jnp.zeros_like(dfk_ref)
            _fill_masked(ks_ref, k_ref, masks, mul=scale)

        q2 = q_ref[...] * scale
        do2 = do_ref[...]
        qm = [_sel(m, q2) for m in masks]
        dom = [_sel(m, do2) for m in masks]
        f0, f1 = _per_head(fq_ref[...], masks)
        l0, l1 = _per_head(lse_ref[...], masks)
        row_ref[0] = f0 - l0
        row_ref[1] = f1 - l1
        prod = do2.astype(F32) * o_ref[...].astype(F32)
        for h in range(2):
            dl_ref[h] = _rep(jnp.sum(jnp.where(masks[h], prod, 0.0), axis=-1, keepdims=True))
        dfq_acc[...] = jnp.zeros_like(dfq_acc)
        dq_acc[...] = jnp.zeros_like(dq_acc)

        def tile(j, diag, slot, heads):
            off = pl.multiple_of(j * tk, tk)
            k2 = k_ref[pl.ds(off, tk), :]
            v2 = v_ref[pl.ds(off, tk), :]
            fk2 = fk_ref[0, j]
            for h in heads:
                s_ref[2 * slot + h] = _dot(qm[h], k2, 1, 1)
                dp_ref[2 * slot + h] = _dot(dom[h], v2, 1, 1)
            yield
            for h in heads:
                col_ref[2 * slot + h] = jnp.zeros((8, tk), F32)
                fk_row = fk2[h:h + 1, :]

                def dscore(r0, h=h, fk_row=fk_row):
                    rows = pl.ds(r0, RC)
                    p = jnp.exp(s_ref[2 * slot + h, rows, :] - fk_row + _wide(row_ref[h, rows, :], tk))
                    if diag:
                        p = jnp.where(_diag_mask(r0, tk, False), p, 0.0)
                    ds = p * (dp_ref[2 * slot + h, rows, :] - _wide(dl_ref[h, rows, :], tk))
                    p_ref[2 * slot + h, rows, :] = p.astype(BF16)
                    ds_ref[2 * slot + h, rows, :] = ds.astype(BF16)
                    dfq_acc[h, rows, :] += _rep(jnp.sum(ds, axis=-1, keepdims=True))
                    col_ref[2 * slot + h] += jnp.sum(ds.reshape(RC // 8, 8, tk), axis=0)
                _chunks(tq, dscore)
            yield
            keys = pl.ds(off, tk)
            for h in heads:
                dq_acc[...] += _dot(ds_ref[2 * slot + h], ks_ref[h, keys, :])
                dk_acc[keys, :] += _dot(ds_ref[2 * slot + h], qm[h], 0, 0)
                dv_acc[keys, :] += _dot(p_ref[2 * slot + h], dom[h], 0, 0)
            for h in heads:
                dfk_ref[0, j, h:h + 1, :] += jnp.sum(col_ref[2 * slot + h], axis=0, keepdims=True)

        _tiles(i, tile)
        dq_ref[...] = dq_acc[...].astype(BF16)
        dfq_ref[...] = jnp.where(masks[0], dfq_acc[0], dfq_acc[1])

        @pl.when(i == nq - 1)
        def _():
            dk_ref[...] = dk_acc[...].astype(BF16)
            dv_ref[...] = dv_acc[...].astype(BF16)

    W = n_pairs * PAIR
    in_specs = _fox_specs(S, tq, n_pairs, base) + [
        pl.BlockSpec((tq, PAIR), lambda p, i: (i, p)),
        pl.BlockSpec((tq, PAIR), lambda p, i: (i, p)),
        pl.BlockSpec((tq, PAIR), lambda p, i: (i, p))]
    out_specs = [pl.BlockSpec((tq, PAIR), lambda p, i: (i, p)),
                 pl.BlockSpec((S, PAIR), lambda p, i: (0, p)),
                 pl.BlockSpec((S, PAIR), lambda p, i: (0, p)),
                 pl.BlockSpec((tq, PAIR), lambda p, i: (i, p)),
                 pl.BlockSpec((1, nq, 8, tk), lambda p, i: (p, 0, 0, 0))]
    return _call_carrying(
        body, exchange, "fox_bwd", (n_pairs, nq), in_specs, out_specs,
        [jax.ShapeDtypeStruct((S, W), BF16)] * 3
        + [jax.ShapeDtypeStruct((S, W), F32), jax.ShapeDtypeStruct((n_pairs, nq, 8, tk), F32)],
        [pltpu.VMEM((4, tq, tk), F32)] * 2 + [pltpu.VMEM((4, tq, tk), BF16)] * 2
        + [pltpu.VMEM((2, tq, 128), F32)] * 3 + [pltpu.VMEM((4, 8, tk), F32)]
        + [pltpu.VMEM((tq, PAIR), F32), pltpu.VMEM((S, PAIR), F32), pltpu.VMEM((S, PAIR), F32),
           pltpu.VMEM((2, S, PAIR), BF16)],
        (qkv, qkv, qkv, fcx, fcr, o, lse, do))


def _swiglu_fwd(u2, wg, wu):
    S, D = u2.shape
    FF = wg.shape[1]
    tm, tn = _pick(S, (512, 256, 128)), _divisors(FF, 1536)[0]

    def body(u_ref, g_ref, w_ref, a_ref, b_ref, h_ref):
        u = u_ref[...]
        a = _dot(u, g_ref[...])
        b = _dot(u, w_ref[...])
        a_ref[...] = a.astype(BF16)
        b_ref[...] = b.astype(BF16)
        h_ref[...] = (a / (1.0 + jnp.exp(-a)) * b).astype(BF16)

    spec_o = pl.BlockSpec((tm, tn), lambda i, j: (i, j))
    return pl.pallas_call(
        body, name="swiglu_fwd", grid=(S // tm, FF // tn),
        in_specs=[pl.BlockSpec((tm, D), lambda i, j: (i, 0)),
                  pl.BlockSpec((D, tn), lambda i, j: (0, j)),
                  pl.BlockSpec((D, tn), lambda i, j: (0, j))],
        out_specs=[spec_o] * 3, out_shape=[jax.ShapeDtypeStruct((S, FF), BF16)] * 3,
        compiler_params=_params(("parallel", "parallel")),
    )(u2, wg, wu)


def _swiglu_bwd(dh, wd, a, b):
    S, D = dh.shape
    FF = wd.shape[0]
    tm, tn = _pick(S, (512, 256, 128)), _divisors(FF, 1536)[0]

    def body(dh_ref, w_ref, a_ref, b_ref, da_ref, db_ref):
        dhin = _dot(dh_ref[...], w_ref[...], 1, 1)
        av = a_ref[...].astype(F32)
        bv = b_ref[...].astype(F32)
        sig = 1.0 / (1.0 + jnp.exp(-av))
        da_ref[...] = (dhin * bv * (sig * (1.0 + av * (1.0 - sig)))).astype(BF16)
        db_ref[...] = (dhin * (av * sig)).astype(BF16)

    spec_o = pl.BlockSpec((tm, tn), lambda i, j: (i, j))
    return pl.pallas_call(
        body, name="swiglu_bwd", grid=(S // tm, FF // tn),
        in_specs=[pl.BlockSpec((tm, D), lambda i, j: (i, 0)),
                  pl.BlockSpec((tn, D), lambda i, j: (j, 0)), spec_o, spec_o],
        out_specs=[spec_o] * 2, out_shape=[jax.ShapeDtypeStruct((S, FF), BF16)] * 2,
        compiler_params=_params(("parallel", "parallel")),
    )(dh, wd, a, b)


def _gate_bwd(dmix, wo, g_sb, g_fx, y_sb, y_fx):
    S, D = dmix.shape
    tm, tn = _pick(S, (512, 256, 128)), _divisors(D, 1024)[0]

    def body(dm_ref, w_ref, gs_ref, gf_ref, ys_ref, yf_ref, dys_ref, dyf_ref, dls_ref, dlf_ref, bs_ref, bf_ref):
        @pl.when(pl.program_id(1) == 0)
        def _():
            bs_ref[...] = jnp.zeros_like(bs_ref)
            bf_ref[...] = jnp.zeros_like(bf_ref)

        dmi = _dot(dm_ref[...], w_ref[...], 1, 1)
        gs, gf = gs_ref[...], gf_ref[...]
        dys_ref[...] = (dmi * gs).astype(BF16)
        dyf_ref[...] = (dmi * gf).astype(BF16)
        dls = dmi * ys_ref[...] * gs * (1.0 - gs)
        dlf = dmi * yf_ref[...] * gf * (1.0 - gf)
        dls_ref[...] = dls.astype(BF16)
        dlf_ref[...] = dlf.astype(BF16)
        bs_ref[0:1, :] += _colsum(dls)
        bf_ref[0:1, :] += _colsum(dlf)

    t = pl.BlockSpec((tm, tn), lambda j, i: (i, j))
    accs = pl.BlockSpec((8, tn), lambda j, i: (0, j))
    return pl.pallas_call(
        body, name="gate_bwd", grid=(D // tn, S // tm),
        in_specs=[pl.BlockSpec((tm, D), lambda j, i: (i, 0)),
                  pl.BlockSpec((tn, D), lambda j, i: (j, 0)), t, t, t, t],
        out_specs=[t, t, t, t, accs, accs],
        out_shape=[jax.ShapeDtypeStruct((S, D), BF16)] * 4 + [jax.ShapeDtypeStruct((8, D), F32)] * 2,
        compiler_params=_params(("parallel", "arbitrary")),
    )(dmix, wo, g_sb, g_fx, y_sb, y_fx)


def _local_step(x, target, ada8, lnp8, bg_sb, bg_fx, bf_pad, wqkv, wf, wgs, wgf, gather, later_weights, pack_early,
                pack_last):
    S, D = x.shape
    W = wqkv.shape[1] // 6
    n_pairs = W // PAIR
    n_heads = W // HEAD_DIM
    ts = _pick(S, (512, 256, 128))
    tq = _pick(S, (256, 128))

    u1 = _ln_mod(x, ada8, ts)
    qkv = _mm([(u1, wqkv)], 'nn', BF16, "in_qkv")
    f = _mm([(u1, wf)], 'nn', F32, "in_f")
    g_sb = _mm([(u1, wgs)], 'nn', F32, "in_gsb", bias=bg_sb, act='sigmoid')
    g_fx = _mm([(u1, wgf)], 'nn', F32, "in_gfx", bias=bg_fx, act='sigmoid')
    fc = _fgate_fwd(f, bf_pad, _pick(S, (512, 256, 128)))
    fch = fc[:, :n_heads]
    fcx = jnp.repeat(fch, HEAD_DIM, axis=1)
    nq = S // tq
    fcr = jnp.pad(fch.T.reshape(n_pairs, 2, nq, tq).transpose(0, 2, 1, 3),
                  ((0, 0), (0, 0), (0, 6), (0, 0)))
    o_sb, o_sb32 = _sb_fwd2(qkv, n_pairs, 0, tq)
    o_fx, lse, *zone = _fox_fwd2(qkv, fcx, fcr, n_pairs, 3 * n_pairs, tq, gather)
    wsb, wfx, wo, wfg, wfu, wfd = later_weights(zone[0] if zone else None)
    y_sb =_mm([(o_sb, wsb)], 'nn', F32, "out_sb")
    y_fx = _mm([(o_fx, wfx)], 'nn', F32, "out_fx")
    mix_in = _gate_mix(g_sb, g_fx, y_sb, y_fx, ts)
    mix = _mm([(mix_in, wo)], 'nn', F32, "out_o")
    x1, u2 = _post_attn(x, mix, ada8, lnp8, ts)
    a, b, hin = _swiglu_fwd(u2, wfg, wfu)
    h = _mm([(hin, wfd)], 'nn', F32, "ffn_down")
    dr2, dh, st_loss = _loss_head(x1, h, target, ada8, lnp8, ts)

    da, db = _swiglu_bwd(dh, wfd, a, b)
    g_wfd = _mm([(hin, dh)], 'tn', F32, "g_ffn_down")
    du2 = _mm([(da, wfg), (db, wfu)], 'nt', F32, "d_u2")
    g_wfg = _mm([(u2, da)], 'tn', F32, "g_ffn_gate")
    g_wfu = _mm([(u2, db)], 'tn', F32, "g_ffn_up")
    dr1, dmix, st_mid = _mid_bwd(du2, x1, dr2, mix, x, ada8, lnp8, ts)
    dys, dyf, dls, dlf, gb_sb, gb_fx = _gate_bwd(dmix, wo, g_sb, g_fx, y_sb, y_fx)
    g_wo = _mm([(mix_in, dmix)], 'tn', F32, "g_w_o")
    do_sb = _mm([(dys, wsb)], 'nt', BF16, "d_o_sb")
    do_fx = _mm([(dyf, wfx)], 'nt', BF16, "d_o_fx")
    g_wsb = _mm([(o_sb, dys)], 'tn', F32, "g_sb_out")
    g_wfx = _mm([(o_fx, dyf)], 'tn', F32, "g_fox_out")
    scatter = pack_early(dict(sb=g_wsb, fx=g_wfx, o=g_wo, fg=g_wfg, fu=g_wfu, fd=g_wfd))
    dq_s, dk_s, dv_s = _sb_bwd2(qkv, o_sb32, do_sb, n_pairs, 0, tq)
    dq_f, dk_f, dv_f, dfq, dfk, *zone = _fox_bwd2(qkv, fcx, fcr, o_fx, lse, do_fx, n_pairs, 3 * n_pairs, tq, scatter)
    dfc = dfq[:, ::HEAD_DIM] - dfk[:, :, :2, :].transpose(0, 2, 1, 3).reshape(n_heads, S).T
    dfc = jnp.pad(dfc, ((0, 0), (0, 128 - n_heads)))
    df, gb_f = _fgate_bwd(dfc, f, bf_pad, _pick(S, (512, 256, 128)))
    grads = [dq_s, dk_s, dv_s, dq_f, dk_f, dv_f]
    g_wqkv = [_mm([(u1, g)], 'tn', F32, "g_in_%d" % n) for n, g in enumerate(grads)]
    g_wf = _mm([(u1, df)], 'tn', F32, "g_in_f")
    g_wgs = _mm([(u1, dls)], 'tn', F32, "g_in_gsb")
    g_wgf = _mm([(u1, dlf)], 'tn', F32, "g_in_gfx")
    wgrads = dict(qkv=g_wqkv, f=g_wf, gs=g_wgs, gf=g_wgf)
    last = pack_last(wgrads)
    du1 = _mm([(g, wqkv, W, 0, n) for n, g in enumerate(grads)] + [(df, wf), (dls, wgs), (dlf, wgf)],
              'nt', F32, "d_u1", exchange=last)
    du1, last_zone = du1 if last is not None else (du1, None)
    gx, st_first = _first_bwd(du1, x, dr1, ada8, ts)

    stats = dict(loss=st_loss, mid=st_mid, first=st_first, gb_sb=gb_sb, gb_fx=gb_fx, gb_f=gb_f)
    return gx, wgrads, stats, (scatter, zone[0] if zone else None), (last, last_zone)


def _position():
    x, y, c = lax.axis_index("x"), lax.axis_index("y"), lax.axis_index("c")
    return x, y, c, 4 * x + 2 * y + c


def _flip(x, y, c, k):
    px = 1 - x if k & 4 else x
    py = 1 - y if k & 2 else y
    pc = 1 - c if k & 1 else c
    return (px, py, pc), 4 * px + 2 * py + pc


def _all_gather_small(v, name):
    r, n = v.shape

    def body(x_ref, out_ref, send_sems, recv_sems, local_sem):
        x, y, c, me = _position()
        mine = pltpu.make_async_copy(x_ref, out_ref.at[me], local_sem)
        mine.start()
        sends = []
        for k in range(1, N_DEV):
            peer, _ = _flip(x, y, c, k)
            cp = pltpu.make_async_remote_copy(
                src_ref=x_ref, dst_ref=out_ref.at[me], send_sem=send_sems.at[k - 1], recv_sem=recv_sems.at[k - 1],
                device_id=peer, device_id_type=MESH)
            cp.start()
            sends.append(cp)
        for k in range(1, N_DEV):
            peer, slot = _flip(x, y, c, k)
            pltpu.make_async_remote_copy(
                src_ref=x_ref, dst_ref=out_ref.at[slot], send_sem=send_sems.at[k - 1], recv_sem=recv_sems.at[k - 1],
                device_id=peer, device_id_type=MESH).wait_recv()
        for cp in sends:
            cp.wait_send()
        mine.wait()

    return pl.pallas_call(
        body, name=name, out_shape=jax.ShapeDtypeStruct((N_DEV, r, n), v.dtype),
        in_specs=[pl.BlockSpec(memory_space=pltpu.VMEM)], out_specs=pl.BlockSpec(memory_space=pltpu.VMEM),
        scratch_shapes=[pltpu.SemaphoreType.DMA((N_DEV - 1,)), pltpu.SemaphoreType.DMA((N_DEV - 1,)),
                        pltpu.SemaphoreType.DMA],
    )(v)


def _all_gather_weights(packed):
    R, C = packed.shape

    def body(x_ref, out_ref, send_sems, recv_sems, local_sem):
        x, y, c, me = _position()
        sibling, sib_slot = _flip(x, y, c, 1)
        mine = pltpu.make_async_copy(x_ref, out_ref.at[me], local_sem)
        mine.start()

        def copy(k, slot, to, src=None):
            return pltpu.make_async_remote_copy(
                src_ref=out_ref.at[slot] if src is None else src, dst_ref=out_ref.at[slot],
                send_sem=send_sems.at[k], recv_sem=recv_sems.at[k], device_id=to, device_id_type=MESH)

        first = [copy(0, me, sibling, src=x_ref)]
        chips = (4, 2, 6)
        for n, k in enumerate(chips):
            peer, _ = _flip(x, y, c, k)
            first.append(copy(1 + n, me, peer, src=x_ref))
        for cp in first:
            cp.start()
        passed = []
        for n, k in enumerate(chips):
            peer, slot = _flip(x, y, c, k)
            copy(1 + n, slot, peer).wait_recv()
            cp = copy(4 + n, slot, sibling)
            cp.start()
            passed.append(cp)
        copy(0, sib_slot, sibling).wait_recv()
        for n, k in enumerate(chips):
            _, slot = _flip(x, y, c, k | 1)
            copy(4 + n, slot, sibling).wait_recv()
        for cp in first + passed:
            cp.wait_send()
        mine.wait()

    return pl.pallas_call(
        body, name="all_gather_weights", out_shape=jax.ShapeDtypeStruct((N_DEV, R, C), packed.dtype),
        in_specs=[pl.BlockSpec(memory_space=pl.ANY)], out_specs=pl.BlockSpec(memory_space=pl.ANY),
        scratch_shapes=[pltpu.SemaphoreType.DMA((7,)), pltpu.SemaphoreType.DMA((7,)), pltpu.SemaphoreType.DMA],
    )(packed)


def _exchange_grads(gpack):
    _, R, C = gpack.shape

    def body(g_ref, out_ref, send_sems, recv_sems, local_sem):
        x, y, c, me = _position()
        mine = pltpu.make_async_copy(g_ref.at[me], out_ref.at[me], local_sem)
        mine.start()
        sends = []
        for k in range(1, N_DEV):
            peer, slot = _flip(x, y, c, k)
            cp = pltpu.make_async_remote_copy(
                src_ref=g_ref.at[slot], dst_ref=out_ref.at[me], send_sem=send_sems.at[k - 1],
                recv_sem=recv_sems.at[k - 1], device_id=peer, device_id_type=MESH)
            cp.start()
            sends.append(cp)
        for k in range(1, N_DEV):
            peer, slot = _flip(x, y, c, k)
            pltpu.make_async_remote_copy(
                src_ref=g_ref.at[slot], dst_ref=out_ref.at[slot], send_sem=send_sems.at[k - 1],
                recv_sem=recv_sems.at[k - 1], device_id=peer, device_id_type=MESH).wait_recv()
        for cp in sends:
            cp.wait_send()
        mine.wait()

    return pl.pallas_call(
        body, name="exchange_grads", out_shape=jax.ShapeDtypeStruct((N_DEV, R, C), gpack.dtype),
        in_specs=[pl.BlockSpec(memory_space=pl.ANY)], out_specs=pl.BlockSpec(memory_space=pl.ANY),
        scratch_shapes=[pltpu.SemaphoreType.DMA((N_DEV - 1,)), pltpu.SemaphoreType.DMA((N_DEV - 1,)),
                        pltpu.SemaphoreType.DMA],
    )(gpack)


def _own_slot(land, own):
    me = 4 * lax.axis_index("x") + 2 * lax.axis_index("y") + lax.axis_index("c")
    return lax.dynamic_update_slice(land, own[None], (me, 0, 0))


def _sum_slots(recv, name, tr):
    n, R, C = recv.shape

    def body(r_ref, o_ref):
        acc = r_ref[0].astype(F32)
        for s in range(1, n):
            acc = acc + r_ref[s].astype(F32)
        o_ref[...] = acc

    return pl.pallas_call(
        body, name=name, grid=(R // tr,), in_specs=[pl.BlockSpec((n, tr, C), lambda i: (0, i, 0))],
        out_specs=pl.BlockSpec((tr, C), lambda i: (i, 0)), out_shape=jax.ShapeDtypeStruct((R, C), F32),
        compiler_params=_params(("parallel",)),
    )(recv)


def _sum_stats(st_all, loss_row):
    n, r, D = st_all.shape

    def body(s_ref, o_ref, l_ref):
        acc = s_ref[0]
        for d in range(1, n):
            acc = acc + s_ref[d]
        o_ref[...] = acc
        l_ref[...] = jnp.zeros((8, 128), F32) + jnp.sum(acc[loss_row:loss_row + 1, :], axis=-1, keepdims=True)

    return pl.pallas_call(
        body, name="sum_stats", out_shape=[jax.ShapeDtypeStruct((r, D), F32), jax.ShapeDtypeStruct((8, 128), F32)],
    )(st_all)


def _adamw(w, g, m, v, name):
    R, C = w.shape
    tr = _pick(R, (256, 176, 128, 64, 32, 16, 8))
    c1 = 1.0 / (1.0 - ADAM_B1 ** ADAM_STEP)
    c2 = 1.0 / (1.0 - ADAM_B2 ** ADAM_STEP)

    def body(w_ref, g_ref, m_ref, v_ref, d_ref, nm_ref, nv_ref):
        gv = g_ref[...]
        nm = ADAM_B1 * m_ref[...] + (1.0 - ADAM_B1) * gv
        nv = ADAM_B2 * v_ref[...] + (1.0 - ADAM_B2) * (gv * gv)
        nm_ref[...] = nm
        nv_ref[...] = nv
        d_ref[...] = -ADAM_LR * ((nm * c1) / (jnp.sqrt(nv * c2) + ADAM_EPS) + ADAM_WD * w_ref[...])

    spec = pl.BlockSpec((tr, C), lambda i: (i, 0))
    return pl.pallas_call(
        body, name=name, grid=(R // tr,), in_specs=[spec] * 4, out_specs=[spec] * 3,
        out_shape=[jax.ShapeDtypeStruct((R, C), F32)] * 3, compiler_params=_params(("parallel",)),
    )(w, g, m, v)


def _round16(n):
    return -(-n // 16) * 16


def _pack_layout(D, in_cols, ff, W):
    parts = [("in", D * (in_cols // N_DEV) // D), ("fg", ff // N_DEV), ("fu", ff // N_DEV),
             ("sb", W * (D // N_DEV) // D), ("fx", W * (D // N_DEV) // D), ("o", D // N_DEV), ("fd", ff // N_DEV)]
    layout, off = {}, 0
    for nm, rows in parts:
        layout[nm] = (off, rows)
        off += _round16(rows)
    return layout, off


def _rows_of(a, D, rows):
    a = a.reshape(rows, D)
    return jnp.pad(a, ((0, _round16(rows) - rows), (0, 0)))


def _cols_to_dest(g, D):
    K, N = g.shape
    n = N // N_DEV
    return g.reshape(K, N_DEV, n).transpose(1, 0, 2).reshape(N_DEV, K * n // D, D)


def _cols_from_src(blocks, K, n):
    return blocks.reshape(N_DEV, K, n).transpose(1, 0, 2).reshape(K, N_DEV * n)


def _pad_rows16(a):
    rows = a.shape[1]
    return jnp.pad(a, ((0, 0), (0, _round16(rows) - rows), (0, 0)))


def kernel(x, c, w_ada, b_ada, w_in, b_gate, b_forget, w_sb_out, w_fox_out, w_o, ln1_g, ln1_b, w_ffn_gate, w_ffn_up, w_ffn_down, ln2_g, ln2_b, loss_target, m_w_ada, m_b_ada, m_w_in, m_b_gate, m_b_forget, m_w_sb_out, m_w_fox_out, m_w_o, m_ln1_g, m_ln1_b, m_w_ffn_gate, m_w_ffn_up, m_w_ffn_down, m_ln2_g, m_ln2_b, v_w_ada, v_b_ada, v_w_in, v_b_gate, v_b_forget, v_w_sb_out, v_w_fox_out, v_w_o, v_ln1_g, v_ln1_b, v_w_ffn_gate, v_w_ffn_up, v_w_ffn_down, v_ln2_g, v_ln2_b):
    S, D = x.shape[1], x.shape[2]
    W = w_sb_out.shape[1]
    n_heads = b_forget.shape[1]
    ff = w_ffn_down.shape[1] * N_DEV
    in_loc = w_in.shape[2]
    in_cols = in_loc * N_DEV
    ada_loc = w_ada.shape[2]
    n_cond = ada_loc * N_DEV // D
    assert w_ada.shape[0] == 1 and n_cond == 6 and in_cols == 6 * W + n_heads + 2 * D and n_heads <= 128
    me = 4 * lax.axis_index("x") + 2 * lax.axis_index("y") + lax.axis_index("c")

    c_all = _all_gather_small(c, "gather_c").reshape(N_DEV, D)
    c16 = jnp.pad(c_all, ((0, 16 - N_DEV), (0, 0)))
    b_cols = lax.dynamic_slice(b_ada, (0, me * ada_loc), (1, ada_loc))
    ada_cols = _mm([(c16, w_ada[0])], 'nn', F32, "ada_fwd", bias=b_cols, silu_a=True)[:N_DEV]
    ada_all = _all_gather_small(ada_cols, "gather_ada")
    ada_me = lax.dynamic_index_in_dim(ada_all, me, axis=1, keepdims=False)
    ada8 = jnp.pad(ada_me.reshape(n_cond, D), ((0, 8 - n_cond), (0, 0)))
    lnp8 = jnp.concatenate([ln1_g, ln1_b, ln2_g, ln2_b, jnp.zeros((4, D), F32)], axis=0)

    layout, R = _pack_layout(D, in_cols, ff, W)
    shards = dict(**{"in": w_in[0]}, fg=w_ffn_gate[0], fu=w_ffn_up[0], sb=w_sb_out[0], fx=w_fox_out[0], o=w_o[0],
                  fd=w_ffn_down[0])
    later = [nm for nm in layout if nm != "in"]
    r_in = _round16(layout["in"][1])
    first = _rows_of(shards["in"].astype(BF16), D, layout["in"][1])
    rest = jnp.concatenate([_rows_of(shards[nm].astype(BF16), D, layout[nm][1]) for nm in later], axis=0)
    gathered_in = _all_gather_weights(first)

    w_in_full = _cols_from_src(gathered_in[:, :layout["in"][1], :], D, in_loc)
    wqkv = w_in_full[:, :6 * W]
    wf = jnp.pad(w_in_full[:, 6 * W:6 * W + n_heads], ((0, 0), (0, 128 - n_heads)))
    wgs = w_in_full[:, 6 * W + n_heads:6 * W + n_heads + D]
    wgf = w_in_full[:, 6 * W + n_heads + D:]
    bf_pad = jnp.pad(b_forget, ((0, 0), (0, 128 - n_heads)))

    def later_weights(zone):
        gathered = _own_slot(zone, rest)

        def part(nm):
            off, rows = layout[nm]
            return gathered[:, off - r_in:off - r_in + rows, :]

        return (_cols_from_src(part("sb"), W, D // N_DEV), _cols_from_src(part("fx"), W, D // N_DEV),
                part("o").reshape(D, D), _cols_from_src(part("fg"), D, ff // N_DEV),
                _cols_from_src(part("fu"), D, ff // N_DEV), part("fd").reshape(ff, D))

    def pack_early(g):
        dest = {"fg": _cols_to_dest(g["fg"], D), "fu": _cols_to_dest(g["fu"], D), "sb": _cols_to_dest(g["sb"], D),
                "fx": _cols_to_dest(g["fx"], D), "o": g["o"].reshape(N_DEV, D // N_DEV, D),
                "fd": g["fd"].reshape(N_DEV, ff // N_DEV, D)}
        return jnp.concatenate([_pad_rows16(dest[nm].astype(BF16)) for nm in later], axis=1), True

    def pack_last(g):
        g_in = jnp.concatenate(g["qkv"] + [g["f"][:, :n_heads], g["gs"], g["gf"]], axis=1)
        return _pad_rows16(_cols_to_dest(g_in, D).astype(BF16)), True

    gx, wg, st, ((pack, _), land), ((pack_in, _), land_in) = _local_step(
        x[0], loss_target[0], ada8, lnp8, b_gate[:, :D], b_gate[:, D:], bf_pad, wqkv, wf, wgs, wgf,
        (rest, False), later_weights, pack_early, pack_last)

    def summed(zone, sent, name):
        own = lax.dynamic_index_in_dim(sent, me, axis=0, keepdims=False)
        return _sum_slots(_own_slot(zone, own), name, _pick(zone.shape[1], (512, 656, 256, 128, 64, 16)))

    gsum_rest = summed(land, pack, "sum_grads_rest")
    gsum_in = summed(land_in, pack_in, "sum_grads_in")

    def gshard(nm, shape):
        off, rows = layout[nm]
        if nm == "in":
            return gsum_in[:rows].reshape(shape)
        return gsum_rest[off - r_in:off - r_in + rows].reshape(shape)

    zrow = jnp.zeros((1, D), F32)
    gb_f_row = jnp.pad(st["gb_f"][0:1], ((0, 0), (0, D - 128)))
    stats16 = jnp.concatenate([
        st["first"][1:2], st["first"][0:1], st["mid"][4:5], st["mid"][1:2], st["mid"][0:1], st["loss"][3:4],
        st["mid"][2:3], st["mid"][3:4], st["loss"][1:2], st["loss"][2:3], st["gb_sb"][0:1], st["gb_fx"][0:1],
        st["loss"][0:1], gb_f_row, zrow, zrow], axis=0)
    st_all = _all_gather_small(stats16, "gather_stats")
    st_sum, loss_blk = _sum_stats(st_all, 12)
    loss = loss_blk[0, 0]

    d_ada_all = st_all[:, :n_cond, :].reshape(N_DEV, n_cond * D)
    d_cols = lax.dynamic_slice(d_ada_all, (0, me * ada_loc), (N_DEV, ada_loc))
    d16 = jnp.pad(d_cols, ((0, 16 - N_DEV), (0, 0)))
    g_w_ada = _mm([(c16, d16)], 'tn', F32, "ada_wgrad", silu_a=True)

    small_w = jnp.concatenate([b_ada.reshape(n_cond, D), ln1_g, ln1_b, ln2_g, ln2_b, b_gate.reshape(2, D), zrow,
                               jnp.pad(b_forget, ((0, 0), (0, D - n_heads))), zrow, zrow], axis=0)
    small_m = jnp.concatenate([m_b_ada.reshape(n_cond, D), m_ln1_g, m_ln1_b, m_ln2_g, m_ln2_b, m_b_gate.reshape(2, D),
                               zrow, jnp.pad(m_b_forget, ((0, 0), (0, D - n_heads))), zrow, zrow], axis=0)
    small_v = jnp.concatenate([v_b_ada.reshape(n_cond, D), v_ln1_g, v_ln1_b, v_ln2_g, v_ln2_b, v_b_gate.reshape(2, D),
                               zrow, jnp.pad(v_b_forget, ((0, 0), (0, D - n_heads))), zrow, zrow], axis=0)
    sm = _adamw(small_w, st_sum, small_m, small_v, "adamw_small")

    def small(a, nm):
        if nm == "b_ada":
            return a[0:n_cond].reshape(1, n_cond * D)
        if nm == "b_gate":
            return a[10:12].reshape(1, 2 * D)
        if nm == "b_forget":
            return a[13:14, :n_heads]
        row = {"ln1_g": 6, "ln1_b": 7, "ln2_g": 8, "ln2_b": 9}[nm]
        return a[row:row + 1]

    big = {
        "w_ada": (w_ada[0], g_w_ada, m_w_ada[0], v_w_ada[0]),
        "w_in": (w_in[0], gshard("in", w_in.shape[1:]), m_w_in[0], v_w_in[0]),
        "w_sb_out": (w_sb_out[0], gshard("sb", w_sb_out.shape[1:]), m_w_sb_out[0], v_w_sb_out[0]),
        "w_fox_out": (w_fox_out[0], gshard("fx", w_fox_out.shape[1:]), m_w_fox_out[0], v_w_fox_out[0]),
        "w_o": (w_o[0], gshard("o", w_o.shape[1:]), m_w_o[0], v_w_o[0]),
        "w_ffn_gate": (w_ffn_gate[0], gshard("fg", w_ffn_gate.shape[1:]), m_w_ffn_gate[0], v_w_ffn_gate[0]),
        "w_ffn_up": (w_ffn_up[0], gshard("fu", w_ffn_up.shape[1:]), m_w_ffn_up[0], v_w_ffn_up[0]),
        "w_ffn_down": (w_ffn_down[0], gshard("fd", w_ffn_down.shape[1:]), m_w_ffn_down[0], v_w_ffn_down[0]),
    }
    order = ["w_ada", "b_ada", "w_in", "b_gate", "b_forget", "w_sb_out", "w_fox_out", "w_o", "ln1_g", "ln1_b",
             "w_ffn_gate", "w_ffn_up", "w_ffn_down", "ln2_g", "ln2_b"]
    grads, deltas, new_ms, new_vs = [], [], [], []
    for nm in order:
        if nm in big:
            w, g, m, v = big[nm]
            d, nm_, nv_ = _adamw(w, g, m, v, "adamw_" + nm)
            grads.append(g[None])
            deltas.append(d[None])
            new_ms.append(nm_[None])
            new_vs.append(nv_[None])
        else:
            grads.append(small(st_sum, nm))
            deltas.append(small(sm[0], nm))
            new_ms.append(small(sm[1], nm))
            new_vs.append(small(sm[2], nm))
    return (loss, gx[None], *grads, *deltas, *new_ms, *new_vs)
```

```python
import functools

import jax
import jax.numpy as jnp
import numpy as np
from jax import lax
from jax.experimental import pallas as pl
from jax.experimental.pallas import tpu as pltpu

F32 = jnp.float32
BF16 = jnp.bfloat16

HEAD_DIM = 64
PAIR = 2 * HEAD_DIM
LN_EPS = 1e-5
ALPHA = 2.0 ** 0.25
ADAM_LR, ADAM_B1, ADAM_B2, ADAM_EPS, ADAM_WD, ADAM_STEP = 0.001, 0.9, 0.999, 1e-08, 0.01, 10
N_DEV = 8
VMEM_LIMIT = 56 * 1024 * 1024
MESH = pl.DeviceIdType.MESH


def _dot(a, b, ca=1, cb=0):
    return lax.dot_general(a, b, (((ca,), (cb,)), ((), ())), preferred_element_type=F32)


def _pick(n, cands):
    for t in cands:
        if n % t == 0:
            return t
    return n


def _params(sem):
    return pltpu.CompilerParams(dimension_semantics=sem, vmem_limit_bytes=VMEM_LIMIT)


MM_BLOCK_BYTES = 40 * 1024 * 1024
LANES = 128


def _divisors(n, cap):
    ds = [d for d in range(LANES, min(n, cap) + 1, LANES) if n % d == 0]
    return sorted(ds, reverse=True) or [n]


def _mm_tiles(M, N, a_row_bytes, b_row_bytes, out_itemsize):
    best = None
    for tm in _divisors(M, 1024):
        for tn in _divisors(N, 2048):
            need = 2 * (tm * a_row_bytes + tn * b_row_bytes + tm * tn * out_itemsize) + tm * tn * 4
            if need <= MM_BLOCK_BYTES and (best is None or (tm * tn, tm) > (best[0] * best[1], best[0])):
                best = (tm, tn)
    assert best is not None, (M, N, a_row_bytes, b_row_bytes)
    return best


def _mm(pairs, mode, out_dtype, name, bias=None, act=None, silu_a=False, exchange=None):
    norm = []
    for p in pairs:
        a, b = p[0], p[1]
        kdim_a = a.shape[0] if mode == 'tn' else a.shape[1]
        K, ka, kb = (p[2], p[3], p[4]) if len(p) > 2 else (kdim_a, 0, 0)
        norm.append((a, b, K, ka, kb))
    a0, b0 = norm[0][0], norm[0][1]
    M = a0.shape[1] if mode == 'tn' else a0.shape[0]
    N = b0.shape[0] if mode == 'nt' else b0.shape[1]
    tm, tn = _mm_tiles(M, N, sum(K * a.dtype.itemsize for a, _, K, _, _ in norm),
                       sum(K * b.dtype.itemsize for _, b, K, _, _ in norm), jnp.dtype(out_dtype).itemsize)
    n_pairs = len(norm)

    in_specs, args = [], []
    for a, b, K, ka, kb in norm:
        if mode == 'tn':
            in_specs.append(pl.BlockSpec((K, tm), lambda i, j, ka=ka: (ka, i)))
        else:
            in_specs.append(pl.BlockSpec((tm, K), lambda i, j, ka=ka: (i, ka)))
        if mode == 'nt':
            in_specs.append(pl.BlockSpec((tn, K), lambda i, j, kb=kb: (j, kb)))
        else:
            in_specs.append(pl.BlockSpec((K, tn), lambda i, j, kb=kb: (kb, j)))
        args += [a, b]
    if bias is not None:
        in_specs.append(pl.BlockSpec((1, tn), lambda i, j: (0, j)))
        args.append(bias)

    ca = 0 if mode == 'tn' else 1
    cb = 1 if mode == 'nt' else 0

    def body(*refs):
        o_ref = refs[-1]
        acc = None
        for p in range(n_pairs):
            av = refs[2 * p][...]
            if silu_a:
                av = av / (1.0 + jnp.exp(-av))
            d = _dot(av.astype(BF16), refs[2 * p + 1][...].astype(BF16), ca, cb)
            acc = d if acc is None else acc + d
        if bias is not None:
            acc = acc + refs[2 * n_pairs][...]
        if act == 'sigmoid':
            acc = 1.0 / (1.0 + jnp.exp(-acc))
        o_ref[...] = acc.astype(out_dtype)

    out_spec = pl.BlockSpec((tm, tn), lambda i, j: (i, j))
    out_shape = jax.ShapeDtypeStruct((M, N), out_dtype)
    if exchange is not None:
        return _call_carrying(body, exchange, name, (M // tm, N // tn), in_specs, [out_spec], [out_shape], [], args)
    return pl.pallas_call(
        body, name=name, grid=(M // tm, N // tn), in_specs=in_specs, out_specs=out_spec, out_shape=out_shape,
        compiler_params=_params(("parallel", "parallel")),
    )(*args)


def _rows_call(body, name, row_ins, vec_ins, row_outs, acc_outs, ts):
    S = row_ins[0].shape[0]
    in_specs = [pl.BlockSpec((ts, a.shape[1]), lambda i: (i, 0)) for a in row_ins]
    in_specs += [pl.BlockSpec(a.shape, lambda i: (0, 0)) for a in vec_ins]
    out_specs = [pl.BlockSpec((ts, c), lambda i: (i, 0)) for c, _ in row_outs]
    out_specs += [pl.BlockSpec(s, lambda i: (0, 0)) for s in acc_outs]
    out_shape = [jax.ShapeDtypeStruct((S, c), dt) for c, dt in row_outs]
    out_shape += [jax.ShapeDtypeStruct(s, F32) for s in acc_outs]
    return pl.pallas_call(
        body, name=name, grid=(S // ts,), in_specs=in_specs, out_specs=out_specs, out_shape=out_shape,
        compiler_params=_params(("arbitrary",)),
    )(*row_ins, *vec_ins)


def _ln_stats(v):
    mu = jnp.mean(v, axis=-1, keepdims=True)
    d = v - mu
    var = jnp.mean(d * d, axis=-1, keepdims=True)
    rstd = lax.rsqrt(var + LN_EPS)
    return d * rstd, rstd


def _ln_bwd(dxhat, xhat, rstd):
    m1 = jnp.mean(dxhat, axis=-1, keepdims=True)
    m2 = jnp.mean(dxhat * xhat, axis=-1, keepdims=True)
    return rstd * (dxhat - m1 - xhat * m2)


def _colsum(v):
    return jnp.sum(v, axis=0, keepdims=True)


def _ln_mod(x, ada8, ts):
    D = x.shape[1]

    def body(x_ref, v_ref, u_ref):
        xhat, _ = _ln_stats(x_ref[...])
        u_ref[...] = (xhat * (1.0 + v_ref[1:2, :]) + v_ref[0:1, :]).astype(BF16)

    return _rows_call(body, "ln_mod", [x], [ada8], [(D, BF16)], [], ts)[0]


def _gate_mix(g_sb, g_fx, y_sb, y_fx, ts):
    D = y_sb.shape[1]

    def body(gs, gf, ys, yf, o_ref):
        o_ref[...] = (gs[...] * ys[...] + gf[...] * yf[...]).astype(BF16)

    return _rows_call(body, "gate_mix", [g_sb, g_fx, y_sb, y_fx], [], [(D, BF16)], [], ts)[0]


def _post_attn(x, mix, ada8, lnp8, ts):
    D = x.shape[1]

    def body(x_ref, mix_ref, v_ref, p_ref, x1_ref, u2_ref):
        r1 = ALPHA * x_ref[...] + v_ref[2:3, :] * mix_ref[...]
        xhat, _ = _ln_stats(r1)
        x1 = xhat * p_ref[0:1, :] + p_ref[1:2, :]
        x1_ref[...] = x1
        xh1, _ = _ln_stats(x1)
        u2_ref[...] = (xh1 * (1.0 + v_ref[4:5, :]) + v_ref[3:4, :]).astype(BF16)

    return _rows_call(body, "post_attn", [x, mix], [ada8, lnp8], [(D, F32), (D, BF16)], [], ts)


def _loss_head(x1, h, target, ada8, lnp8, ts):
    D = x1.shape[1]

    def body(x1_ref, h_ref, t_ref, v_ref, p_ref, dr2_ref, dh_ref, st_ref):
        @pl.when(pl.program_id(0) == 0)
        def _():
            st_ref[...] = jnp.zeros_like(st_ref)

        hv = h_ref[...]
        g2 = v_ref[5:6, :]
        r2 = ALPHA * x1_ref[...] + g2 * hv
        xhat, rstd = _ln_stats(r2)
        y = xhat * p_ref[2:3, :] + p_ref[3:4, :]
        err = y - t_ref[...]
        dy = err * (1.0 / D)
        dr2 = _ln_bwd(dy * p_ref[2:3, :], xhat, rstd)
        dr2_ref[...] = dr2
        dh_ref[...] = (dr2 * g2).astype(BF16)
        st_ref[0:1, :] += _colsum(err * err) * (0.5 / D)
        st_ref[1:2, :] += _colsum(dy * xhat)
        st_ref[2:3, :] += _colsum(dy)
        st_ref[3:4, :] += _colsum(dr2 * hv)

    return _rows_call(body, "loss_head", [x1, h, target], [ada8, lnp8], [(D, F32), (D, BF16)], [(8, D)], ts)


def _mid_bwd(du2, x1, dr2, mix, x, ada8, lnp8, ts):
    D = x.shape[1]

    def body(du2_ref, x1_ref, dr2_ref, mix_ref, x_ref, v_ref, p_ref, dr1_ref, dmix_ref, st_ref):
        @pl.when(pl.program_id(0) == 0)
        def _():
            st_ref[...] = jnp.zeros_like(st_ref)

        du2v = du2_ref[...]
        xh1, rstd1 = _ln_stats(x1_ref[...])
        dx1 = ALPHA * dr2_ref[...] + _ln_bwd(du2v * (1.0 + v_ref[4:5, :]), xh1, rstd1)
        mixv = mix_ref[...]
        g1 = v_ref[2:3, :]
        r1 = ALPHA * x_ref[...] + g1 * mixv
        xhr, rstdr = _ln_stats(r1)
        dr1 = _ln_bwd(dx1 * p_ref[0:1, :], xhr, rstdr)
        dr1_ref[...] = dr1
        dmix_ref[...] = (dr1 * g1).astype(BF16)
        st_ref[0:1, :] += _colsum(du2v * xh1)
        st_ref[1:2, :] += _colsum(du2v)
        st_ref[2:3, :] += _colsum(dx1 * xhr)
        st_ref[3:4, :] += _colsum(dx1)
        st_ref[4:5, :] += _colsum(dr1 * mixv)

    return _rows_call(body, "mid_bwd", [du2, x1, dr2, mix, x], [ada8, lnp8], [(D, F32), (D, BF16)], [(8, D)], ts)


def _first_bwd(du1, x, dr1, ada8, ts):
    D = x.shape[1]

    def body(du1_ref, x_ref, dr1_ref, v_ref, gx_ref, st_ref):
        @pl.when(pl.program_id(0) == 0)
        def _():
            st_ref[...] = jnp.zeros_like(st_ref)

        du1v = du1_ref[...]
        xh0, rstd0 = _ln_stats(x_ref[...])
        gx_ref[...] = ALPHA * dr1_ref[...] + _ln_bwd(du1v * (1.0 + v_ref[1:2, :]), xh0, rstd0)
        st_ref[0:1, :] += _colsum(du1v * xh0)
        st_ref[1:2, :] += _colsum(du1v)

    return _rows_call(body, "first_bwd", [du1, x, dr1], [ada8], [(D, F32)], [(8, D)], ts)


def _split3(v):
    hi = v.astype(BF16)
    r = v - hi.astype(F32)
    mid = r.astype(BF16)
    lo = (r - mid.astype(F32)).astype(BF16)
    return hi, mid, lo


def _fgate_fwd(f, bf_pad, tb):
    S = f.shape[0]

    def body(f_ref, b_ref, fc_ref, carry):
        @pl.when(pl.program_id(0) == 0)
        def _():
            carry[...] = jnp.zeros_like(carry)

        z = f_ref[...] + b_ref[...]
        ls = jnp.minimum(z, 0.0) - jnp.log(1.0 + jnp.exp(-jnp.abs(z)))
        r = lax.broadcasted_iota(jnp.int32, (tb, tb), 0)
        c = lax.broadcasted_iota(jnp.int32, (tb, tb), 1)
        tri = (c <= r).astype(BF16)
        hi, mid, lo = _split3(ls)
        cs = _dot(tri, hi) + _dot(tri, mid) + _dot(tri, lo) + carry[...]
        fc_ref[...] = cs
        carry[...] = cs[tb - 1:tb, :]

    return pl.pallas_call(
        body, name="fgate_fwd", grid=(S // tb,),
        in_specs=[pl.BlockSpec((tb, 128), lambda i: (i, 0)), pl.BlockSpec((1, 128), lambda i: (0, 0))],
        out_specs=pl.BlockSpec((tb, 128), lambda i: (i, 0)),
        out_shape=jax.ShapeDtypeStruct((S, 128), F32),
        scratch_shapes=[pltpu.VMEM((1, 128), F32)],
        compiler_params=_params(("arbitrary",)),
    )(f, bf_pad)


def _fgate_bwd(dfc, f, bf_pad, tb):
    S = f.shape[0]
    nb = S // tb

    def body(d_ref, f_ref, b_ref, df_ref, gb_ref, carry):
        @pl.when(pl.program_id(0) == 0)
        def _():
            carry[...] = jnp.zeros_like(carry)
            gb_ref[...] = jnp.zeros_like(gb_ref)

        r = lax.broadcasted_iota(jnp.int32, (tb, tb), 0)
        c = lax.broadcasted_iota(jnp.int32, (tb, tb), 1)
        tri = (c >= r).astype(BF16)
        hi, mid, lo = _split3(d_ref[...])
        rs = _dot(tri, hi) + _dot(tri, mid) + _dot(tri, lo) + carry[...]
        carry[...] = rs[0:1, :]
        z = f_ref[...] + b_ref[...]
        df = rs * (1.0 / (1.0 + jnp.exp(z)))
        df_ref[...] = df
        gb_ref[0:1, :] += _colsum(df)

    return pl.pallas_call(
        body, name="fgate_bwd", grid=(nb,),
        in_specs=[pl.BlockSpec((tb, 128), lambda i: (nb - 1 - i, 0)),
                  pl.BlockSpec((tb, 128), lambda i: (nb - 1 - i, 0)),
                  pl.BlockSpec((1, 128), lambda i: (0, 0))],
        out_specs=[pl.BlockSpec((tb, 128), lambda i: (nb - 1 - i, 0)), pl.BlockSpec((8, 128), lambda i: (0, 0))],
        out_shape=[jax.ShapeDtypeStruct((S, 128), F32), jax.ShapeDtypeStruct((8, 128), F32)],
        scratch_shapes=[pltpu.VMEM((1, 128), F32)],
        compiler_params=_params(("arbitrary",)),
    )(dfc, f, bf_pad)


def _split2(v):
    hi = v.astype(BF16)
    lo = (v - hi.astype(F32)).astype(BF16)
    return hi, lo


def _head_masks():
    lane = lax.broadcasted_iota(jnp.int32, (1, PAIR), 1)
    m0 = lane < HEAD_DIM
    return m0, jnp.logical_not(m0)


def _sel(mask, v):
    return jnp.where(mask, v, jnp.zeros_like(v))


def _softplus(z):
    return jnp.maximum(z, 0.0) + jnp.log(1.0 + jnp.exp(-jnp.abs(z)))


def _qkv_specs(S, tq, n_pairs, base):
    return [pl.BlockSpec((tq, PAIR), lambda p, i: (i, base + p)),
            pl.BlockSpec((S, PAIR), lambda p, i: (0, base + n_pairs + p)),
            pl.BlockSpec((S, PAIR), lambda p, i: (0, base + 2 * n_pairs + p))]


def _sb_fwd(qkv, n_pairs, base, tq):
    S = qkv.shape[0]
    tk = tq
    scale = HEAD_DIM ** -0.5

    def body(q_ref, k_ref, v_ref, o_ref, t_ref, acc_ref):
        i = pl.program_id(1)
        masks = _head_masks()
        q2 = q_ref[...]
        qm = [_sel(m, q2) for m in masks]
        rowpos = i * tq + lax.broadcasted_iota(jnp.int32, (tq, tk), 0)
        colin = lax.broadcasted_iota(jnp.int32, (tq, tk), 1)
        upper = (lax.broadcasted_iota(jnp.int32, (tk, tk), 0) > lax.broadcasted_iota(jnp.int32, (tk, tk), 1)).astype(BF16)
        acc_ref[...] = jnp.zeros_like(acc_ref)

        def step(jj, carry):
            j = i - jj
            off = pl.multiple_of(j * tk, tk)
            k2 = k_ref[pl.ds(off, tk), :]
            v2 = v_ref[pl.ds(off, tk), :]
            mask = (j * tk + colin) < rowpos
            out = None
            new = []
            for h in heads:
                z = _dot(qm[h], k2, 1, 1) * scale
                sp = _softplus(z)
                lg = jnp.where(mask, -sp, 0.0)
                hi, lo = _split2(lg)
                suf = _dot(hi, upper) + _dot(lo, upper)
                a = jnp.where(mask, jnp.exp(z - sp + suf + carry[h]), 0.0)
                d = _dot(a.astype(BF16), _sel(masks[h], v2))
                out = d if out is None else out + d
                new.append(carry[h] + jnp.sum(lg, axis=-1, keepdims=True))
            acc_ref[...] += out
            return tuple(new)

        zero = jnp.zeros((tq, 1), F32)
        r0, r1 = lax.fori_loop(0, i + 1, step, (zero, zero))
        o_ref[...] = acc_ref[...].astype(BF16)
        t_ref[...] = jnp.where(masks[0], r0, r1)

    W = n_pairs * PAIR
    return pl.pallas_call(
        body, name="sb_fwd", grid=(n_pairs, S // tq), in_specs=_qkv_specs(S, tq, n_pairs, base),
        out_specs=[pl.BlockSpec((tq, PAIR), lambda p, i: (i, p)), pl.BlockSpec((tq, PAIR), lambda p, i: (i, p))],
        out_shape=[jax.ShapeDtypeStruct((S, W), BF16), jax.ShapeDtypeStruct((S, W), F32)],
        scratch_shapes=[pltpu.VMEM((tq, PAIR), F32)],
        compiler_params=_params(("parallel", "arbitrary")),
    )(qkv, qkv, qkv)


def _sb_bwd(qkv, tot, do, n_pairs, base, tq):
    S = qkv.shape[0]
    tk = tq
    nq = S // tq
    scale = HEAD_DIM ** -0.5

    def body(q_ref, k_ref, v_ref, t_ref, do_ref, dq_ref, dk_ref, dv_ref, dq_acc, dk_acc, dv_acc):
        i = pl.program_id(1)
        masks = _head_masks()

        @pl.when(i == 0)
        def _():
            dk_acc[...] = jnp.zeros_like(dk_acc)
            dv_acc[...] = jnp.zeros_like(dv_acc)

        q2 = q_ref[...]
        do2 = do_ref[...]
        qm = [_sel(m, q2) for m in masks]
        dom = [_sel(m, do2) for m in masks]
        t2 = t_ref[...]
        tot_h = [t2[:, 0:1], t2[:, HEAD_DIM:HEAD_DIM + 1]]
        rowpos = i * tq + lax.broadcasted_iota(jnp.int32, (tq, tk), 0)
        colin = lax.broadcasted_iota(jnp.int32, (tq, tk), 1)
        r_i = lax.broadcasted_iota(jnp.int32, (tk, tk), 0)
        c_i = lax.broadcasted_iota(jnp.int32, (tk, tk), 1)
        upper = (r_i > c_i).astype(BF16)
        lower = (r_i < c_i).astype(BF16)
        dq_acc[...] = jnp.zeros_like(dq_acc)

        def step(j, carry):
            off = pl.multiple_of(j * tk, tk)
            k2 = k_ref[pl.ds(off, tk), :]
            v2 = v_ref[pl.ds(off, tk), :]
            mask = (j * tk + colin) < rowpos
            dq = None
            dk = None
            dv = None
            new = []
            for h in heads:
                cum_l, cum_g = carry[2 * h], carry[2 * h + 1]
                z = _dot(qm[h], k2, 1, 1) * scale
                sp = _softplus(z)
                lg = jnp.where(mask, -sp, 0.0)
                hi, lo = _split2(lg)
                suf = _dot(hi, upper) + _dot(lo, upper)
                row_l = jnp.sum(lg, axis=-1, keepdims=True)
                later = tot_h[h] - cum_l - row_l
                a = jnp.where(mask, jnp.exp(z - sp + suf + later), 0.0)
                da = _dot(dom[h], v2, 1, 1)
                g = da * a
                ghi, glo = _split2(g)
                pre = _dot(ghi, lower) + _dot(glo, lower) + cum_g
                one_m_beta = jnp.exp(-sp)
                dz = jnp.where(mask, g * one_m_beta - (1.0 - one_m_beta) * pre, 0.0)
                dzb = (dz * scale).astype(BF16)
                d1 = _dot(dzb, _sel(masks[h], k2))
                d2 = _dot(dzb, qm[h], 0, 0)
                d3 = _dot(a.astype(BF16), dom[h], 0, 0)
                dq = d1 if dq is None else dq + d1
                dk = d2 if dk is None else dk + d2
                dv = d3 if dv is None else dv + d3
                new += [cum_l + row_l, cum_g + jnp.sum(g, axis=-1, keepdims=True)]
            dq_acc[...] += dq
            dk_acc[pl.ds(off, tk), :] += dk
            dv_acc[pl.ds(off, tk), :] += dv
            return tuple(new)

        zero = jnp.zeros((tq, 1), F32)
        lax.fori_loop(0, i + 1, step, (zero, zero, zero, zero))
        dq_ref[...] = dq_acc[...].astype(BF16)

        @pl.when(i == nq - 1)
        def _():
            dk_ref[...] = dk_acc[...].astype(BF16)
            dv_ref[...] = dv_acc[...].astype(BF16)

    W = n_pairs * PAIR
    in_specs = _qkv_specs(S, tq, n_pairs, base) + [
        pl.BlockSpec((tq, PAIR), lambda p, i: (i, p)),
        pl.BlockSpec((tq, PAIR), lambda p, i: (i, p))]
    out_specs = [pl.BlockSpec((tq, PAIR), lambda p, i: (i, p)),
                 pl.BlockSpec((S, PAIR), lambda p, i: (0, p)),
                 pl.BlockSpec((S, PAIR), lambda p, i: (0, p))]
    dq, dk, dv = pl.pallas_call(
        body, name="sb_bwd", grid=(n_pairs, nq), in_specs=in_specs, out_specs=out_specs,
        out_shape=[jax.ShapeDtypeStruct((S, W), BF16)] * 3,
        scratch_shapes=[pltpu.VMEM((tq, PAIR), F32), pltpu.VMEM((S, PAIR), F32), pltpu.VMEM((S, PAIR), F32)],
        compiler_params=_params(("parallel", "arbitrary")),
    )(qkv, qkv, qkv, tot, do)
    return dq, dk, dv


NEG = -1e30


def _fox_specs(S, tq, n_pairs, base):
    return _qkv_specs(S, tq, n_pairs, base) + [
        pl.BlockSpec((tq, PAIR), lambda p, i: (i, p)),
        pl.BlockSpec((1, S // tq, 8, tq), lambda p, i: (p, 0, 0, 0))]


def _fox_fwd(qkv, fcx, fcr, n_pairs, base, tq):
    S = qkv.shape[0]
    tk = tq
    scale = HEAD_DIM ** -0.5

    def body(q_ref, k_ref, v_ref, fq_ref, fk_ref, o_ref, lse_ref, acc_ref):
        i = pl.program_id(1)
        masks = _head_masks()
        q2 = q_ref[...]
        qm = [_sel(m, q2) for m in masks]
        fq2 = fq_ref[...]
        fq = [fq2[:, 0:1], fq2[:, HEAD_DIM:HEAD_DIM + 1]]
        rowpos = i * tq + lax.broadcasted_iota(jnp.int32, (tq, tk), 0)
        colin = lax.broadcasted_iota(jnp.int32, (tq, tk), 1)
        acc_ref[...] = jnp.zeros_like(acc_ref)

        def step(j, carry):
            off = pl.multiple_of(j * tk, tk)
            k2 = k_ref[pl.ds(off, tk), :]
            v2 = v_ref[pl.ds(off, tk), :]
            fk2 = fk_ref[0, j]
            mask = (j * tk + colin) <= rowpos
            out = None
            new = []
            alphas = []
            for h in heads:
                m_old, l_old = carry[2 * h], carry[2 * h + 1]
                s = _dot(qm[h], k2, 1, 1) * scale + fq[h] - fk2[h:h + 1, :]
                s = jnp.where(mask, s, NEG)
                m_new = jnp.maximum(m_old, jnp.max(s, axis=-1, keepdims=True))
                p = jnp.exp(s - m_new)
                alpha = jnp.exp(m_old - m_new)
                alphas.append(alpha)
                d = _dot(p.astype(BF16), _sel(masks[h], v2))
                out = d if out is None else out + d
                new += [m_new, alpha * l_old + jnp.sum(p, axis=-1, keepdims=True)]
            acc_ref[...] = acc_ref[...] * jnp.where(masks[0], alphas[0], alphas[1]) + out
            return tuple(new)

        zero = jnp.zeros((tq, 1), F32)
        neg = jnp.full((tq, 1), NEG, F32)
        m0, l0, m1, l1 = lax.fori_loop(0, i + 1, step, (neg, zero, neg, zero))
        o_ref[...] = (acc_ref[...] / jnp.where(masks[0], l0, l1)).astype(BF16)
        lse_ref[...] = jnp.where(masks[0], m0 + jnp.log(l0), m1 + jnp.log(l1))

    W = n_pairs * PAIR
    return pl.pallas_call(
        body, name="fox_fwd", grid=(n_pairs, S // tq), in_specs=_fox_specs(S, tq, n_pairs, base),
        out_specs=[pl.BlockSpec((tq, PAIR), lambda p, i: (i, p)), pl.BlockSpec((tq, PAIR), lambda p, i: (i, p))],
        out_shape=[jax.ShapeDtypeStruct((S, W), BF16), jax.ShapeDtypeStruct((S, W), F32)],
        scratch_shapes=[pltpu.VMEM((tq, PAIR), F32)],
        compiler_params=_params(("parallel", "arbitrary")),
    )(qkv, qkv, qkv, fcx, fcr)


def _fox_bwd(qkv, fcx, fcr, o, lse, do, n_pairs, base, tq):
    S = qkv.shape[0]
    tk = tq
    nq = S // tq
    scale = HEAD_DIM ** -0.5

    def body(q_ref, k_ref, v_ref, fq_ref, fk_ref, o_ref, lse_ref, do_ref,
             dq_ref, dk_ref, dv_ref, dfq_ref, dfk_ref, dq_acc, dk_acc, dv_acc):
        i = pl.program_id(1)
        masks = _head_masks()

        @pl.when(i == 0)
        def _():
            dk_acc[...] = jnp.zeros_like(dk_acc)
            dv_acc[...] = jnp.zeros_like(dv_acc)
            dfk_ref[...] = jnp.zeros_like(dfk_ref)

        q2 = q_ref[...]
        do2 = do_ref[...]
        qm = [_sel(m, q2) for m in masks]
        dom = [_sel(m, do2) for m in masks]
        fq2 = fq_ref[...]
        fq = [fq2[:, 0:1], fq2[:, HEAD_DIM:HEAD_DIM + 1]]
        l2 = lse_ref[...]
        lse_h = [l2[:, 0:1], l2[:, HEAD_DIM:HEAD_DIM + 1]]
        prod = do2.astype(F32) * o_ref[...].astype(F32)
        delta = [jnp.sum(jnp.where(m, prod, 0.0), axis=-1, keepdims=True) for m in masks]
        rowpos = i * tq + lax.broadcasted_iota(jnp.int32, (tq, tk), 0)
        colin = lax.broadcasted_iota(jnp.int32, (tq, tk), 1)
        dq_acc[...] = jnp.zeros_like(dq_acc)

        def step(j, carry):
            off = pl.multiple_of(j * tk, tk)
            k2 = k_ref[pl.ds(off, tk), :]
            v2 = v_ref[pl.ds(off, tk), :]
            fk2 = fk_ref[0, j]
            mask = (j * tk + colin) <= rowpos
            dq = None
            dk = None
            dv = None
            new = []
            dfk_rows = []
            for h in heads:
                s = _dot(qm[h], k2, 1, 1) * scale + fq[h] - fk2[h:h + 1, :]
                p = jnp.where(mask, jnp.exp(s - lse_h[h]), 0.0)
                dp = _dot(dom[h], v2, 1, 1)
                ds = p * (dp - delta[h])
                dsb = (ds * scale).astype(BF16)
                d1 = _dot(dsb, _sel(masks[h], k2))
                d2 = _dot(dsb, qm[h], 0, 0)
                d3 = _dot(p.astype(BF16), dom[h], 0, 0)
                dq = d1 if dq is None else dq + d1
                dk = d2 if dk is None else dk + d2
                dv = d3 if dv is None else dv + d3
                new.append(carry[h] + jnp.sum(ds, axis=-1, keepdims=True))
                dfk_rows.append(jnp.sum(ds, axis=0, keepdims=True))
            dq_acc[...] += dq
            dk_acc[pl.ds(off, tk), :] += dk
            dv_acc[pl.ds(off, tk), :] += dv
            dfk_ref[0, j, 0:1, :] += dfk_rows[0]
            dfk_ref[0, j, 1:2, :] += dfk_rows[1]
            return tuple(new)

        zero = jnp.zeros((tq, 1), F32)
        r0, r1 = lax.fori_loop(0, i + 1, step, (zero, zero))
        dq_ref[...] = dq_acc[...].astype(BF16)
        dfq_ref[...] = jnp.where(masks[0], r0, r1)

        @pl.when(i == nq - 1)
        def _():
            dk_ref[...] = dk_acc[...].astype(BF16)
            dv_ref[...] = dv_acc[...].astype(BF16)

    W = n_pairs * PAIR
    in_specs = _fox_specs(S, tq, n_pairs, base) + [
        pl.BlockSpec((tq, PAIR), lambda p, i: (i, p)),
        pl.BlockSpec((tq, PAIR), lambda p, i: (i, p)),
        pl.BlockSpec((tq, PAIR), lambda p, i: (i, p))]
    out_specs = [pl.BlockSpec((tq, PAIR), lambda p, i: (i, p)),
                 pl.BlockSpec((S, PAIR), lambda p, i: (0, p)),
                 pl.BlockSpec((S, PAIR), lambda p, i: (0, p)),
                 pl.BlockSpec((tq, PAIR), lambda p, i: (i, p)),
                 pl.BlockSpec((1, nq, 8, tk), lambda p, i: (p, 0, 0, 0))]
    return pl.pallas_call(
        body, name="fox_bwd", grid=(n_pairs, nq), in_specs=in_specs, out_specs=out_specs,
        out_shape=[jax.ShapeDtypeStruct((S, W), BF16)] * 3
        + [jax.ShapeDtypeStruct((S, W), F32), jax.ShapeDtypeStruct((n_pairs, nq, 8, tk), F32)],
        scratch_shapes=[pltpu.VMEM((tq, PAIR), F32), pltpu.VMEM((S, PAIR), F32), pltpu.VMEM((S, PAIR), F32)],
        compiler_params=_params(("parallel", "arbitrary")),
    )(qkv, qkv, qkv, fcx, fcr, o, lse, do)


RC = 32
VANISH = -104.0


def _chunks(n_rows, fn):
    for ci in range(n_rows // RC):
        fn(ci * RC)


def _wide(v, tk):
    return v if tk == 128 else jnp.tile(v, (1, tk // 128))


def _rep(col):
    return jnp.broadcast_to(col, (col.shape[0], 128))


def _per_head(blk, masks):
    sw = pltpu.roll(blk, HEAD_DIM, axis=1)
    return jnp.where(masks[0], blk, sw), jnp.where(masks[0], sw, blk)


def _fill_masked(dst_ref, src_ref, masks, mul=None, ones_lane=None):
    v = src_ref[...]
    if mul is not None:
        v = v * mul
    lane = lax.broadcasted_iota(jnp.int32, (1, PAIR), 1)
    for h in range(2):
        m = _sel(masks[h], v)
        if ones_lane is not None:
            m = jnp.where(lane == ones_lane[h], jnp.ones_like(m), m)
        dst_ref[h] = m


def _head_norms(v, masks):
    sq = v.astype(F32)
    sq = sq * sq
    return [jnp.sqrt(jnp.sum(jnp.where(m, sq, 0.0), axis=-1, keepdims=True)) for m in masks]


def _largest_key_norm(kmax_ref, k_ref, masks):
    for h, n in enumerate(_head_norms(k_ref[...], masks)):
        kmax_ref[h] = jnp.broadcast_to(jnp.max(n, axis=0, keepdims=True), (8, 128))


def _score_bound(q_scaled, kmax_ref, masks):
    return [_rep(n) * kmax_ref[h][0:1, :] for h, n in enumerate(_head_norms(q_scaled, masks))]


def _tri(tk, cmp):
    r = lax.broadcasted_iota(jnp.int32, (tk, tk), 0)
    c = lax.broadcasted_iota(jnp.int32, (tk, tk), 1)
    return cmp(r, c).astype(BF16)


def _diag_mask(r0, tk, strict):
    row = r0 + lax.broadcasted_iota(jnp.int32, (RC, tk), 0)
    col = lax.broadcasted_iota(jnp.int32, (RC, tk), 1)
    return (col < row) if strict else (col <= row)


def _peer_copies(src_ref, land_ref, send_sems, recv_sems, scatter, receive_side):
    x, y, c = lax.axis_index("x"), lax.axis_index("y"), lax.axis_index("c")
    me = 4 * x + 2 * y + c
    copies = []
    for k in range(1, N_DEV):
        px, py, pc = (1 - x if k & 4 else x), (1 - y if k & 2 else y), (1 - c if k & 1 else c)
        slot = 4 * px + 2 * py + pc
        copies.append(pltpu.make_async_remote_copy(
            src_ref=src_ref.at[slot] if scatter else src_ref,
            dst_ref=land_ref.at[slot] if receive_side else land_ref.at[me],
            send_sem=send_sems.at[k - 1], recv_sem=recv_sems.at[k - 1], device_id=(px, py, pc), device_id_type=MESH))
    return copies


def _call_carrying(body, exchange, name, grid, in_specs, out_specs, out_shape, scratch_shapes, args):
    if exchange is None:
        return pl.pallas_call(body, name=name, grid=grid, in_specs=in_specs, out_specs=out_specs, out_shape=out_shape,
                              scratch_shapes=scratch_shapes, compiler_params=_params(("parallel", "arbitrary")))(*args)
    src, scatter = exchange
    n_in, n_out = len(in_specs), len(out_specs)

    def carrying(*refs):
        src_ref, land_ref = refs[n_in], refs[n_in + 1 + n_out]
        send_sems, recv_sems = refs[-2], refs[-1]
        first = jnp.logical_and(pl.program_id(0) == 0, pl.program_id(1) == 0)
        last = jnp.logical_and(pl.program_id(0) == grid[0] - 1, pl.program_id(1) == grid[1] - 1)

        @pl.when(first)
        def _():
            for cp in _peer_copies(src_ref, land_ref, send_sems, recv_sems, scatter, False):
                cp.start()

        body(*refs[:n_in], *refs[n_in + 1:n_in + 1 + n_out], *refs[n_in + 2 + n_out:-2])

        @pl.when(last)
        def _():
            for cp in _peer_copies(src_ref, land_ref, send_sems, recv_sems, scatter, True):
                cp.wait_send()
                cp.wait_recv()

    any_space = pl.BlockSpec(memory_space=pl.ANY)
    land = jax.ShapeDtypeStruct((N_DEV,) + src.shape[-2:], src.dtype)
    return pl.pallas_call(
        carrying, name=name, grid=grid, in_specs=list(in_specs) + [any_space],
        out_specs=list(out_specs) + [any_space], out_shape=list(out_shape) + [land],
        scratch_shapes=list(scratch_shapes) + [pltpu.SemaphoreType.DMA((N_DEV - 1,)), pltpu.SemaphoreType.DMA((N_DEV - 1,))],
        compiler_params=_params(("arbitrary", "arbitrary")))(*args, src)


def _staggered(bodies):
    active, waiting = [], list(bodies)
    while waiting or active:
        if waiting:
            active.append(waiting.pop(0))
        for g in list(active):
            try:
                next(g)
            except StopIteration:
                active.remove(g)


def _streams(tile, j, diag, slot):
    return [tile(j, diag, slot, (0, 1))]


def _tiles(i, tile):
    def step(jj, carry):
        _staggered(_streams(tile, 2 * jj, False, 0) + _streams(tile, 2 * jj + 1, False, 1))
        return carry
    lax.fori_loop(0, (i - 1) // 2, step, 0)

    @pl.when(jnp.logical_and(i >= 1, (i - 1) % 2 == 1))
    def _():
        _staggered(_streams(tile, i - 2, False, 0))

    @pl.when(i >= 1)
    def _():
        _staggered(_streams(tile, i - 1, False, 0) + _streams(tile, i, True, 1))

    @pl.when(i == 0)
    def _():
        _staggered(_streams(tile, i, True, 1))


def _tiles_reversed(i, tile, keep_going):
    @pl.when(i == 0)
    def _():
        _staggered(_streams(tile, i, True, 0))

    @pl.when(i >= 1)
    def _():
        _staggered(_streams(tile, i, True, 0) + _streams(tile, i - 1, False, 1))

    pairs = (i - 1) // 2

    def cond(carry):
        jj, go = carry
        return jnp.logical_and(jj < pairs, go)

    def step(carry):
        jj, _ = carry
        _staggered(_streams(tile, i - 2 - 2 * jj, False, 0) + _streams(tile, i - 3 - 2 * jj, False, 1))
        return jj + 1, keep_going(jnp.maximum(i - 4 - 2 * jj, 0))

    jj, go = lax.while_loop(cond, step, (jnp.int32(0), keep_going(jnp.maximum(i - 2, 0))))

    @pl.when(jnp.logical_and(jnp.logical_and(i >= 1, (i - 1) % 2 == 1), jnp.logical_and(jj == pairs, go)))
    def _():
        _staggered(_streams(tile, 0, False, 0))


def _sb_fwd2(qkv, n_pairs, base, tq, exchange=None):
    S = qkv.shape[0]
    tk = tq
    scale = HEAD_DIM ** -0.5

    def body(q_ref, k_ref, v_ref, o_ref, t_ref, z_ref, hi_ref, suf_ref, p_ref, r_ref, acc_ref, vm_ref):
        i = pl.program_id(1)
        masks = _head_masks()

        @pl.when(i == 0)
        def _():
            _fill_masked(vm_ref, v_ref, masks)

        q2 = q_ref[...] * scale
        qm = [_sel(m, q2) for m in masks]
        incl = _tri(tk, lambda r, c: r >= c)
        r_ref[...] = jnp.zeros_like(r_ref)
        acc_ref[...] = jnp.zeros_like(acc_ref)

        def tile(j, diag, slot, heads):
            off = pl.multiple_of(j * tk, tk)
            k2 = k_ref[pl.ds(off, tk), :]
            v2 = v_ref[pl.ds(off, tk), :]
            for h in heads:
                z_ref[2 * slot + h] = _dot(qm[h], k2, 1, 1)
            yield
            for h in heads:
                def split(r0, h=h):
                    rows = pl.ds(r0, RC)
                    lg = -_softplus(z_ref[2 * slot + h, rows, :])
                    if diag:
                        lg = jnp.where(_diag_mask(r0, tk, True), lg, 0.0)
                    hi_ref[2 * slot + h, rows, :] = lg.astype(BF16)
                _chunks(tq, split)
            yield
            for h in heads:
                suf_ref[2 * slot + h] = _dot(hi_ref[2 * slot + h], incl)
            yield
            for h in heads:
                def weights(r0, h=h):
                    rows = pl.ds(r0, RC)
                    a = jnp.exp(z_ref[2 * slot + h, rows, :] + suf_ref[2 * slot + h, rows, :] + _wide(r_ref[h, rows, :], tk))
                    if diag:
                        a = jnp.where(_diag_mask(r0, tk, True), a, 0.0)
                    p_ref[2 * slot + h, rows, :] = a.astype(BF16)
                _chunks(tq, weights)
            yield
            keys = pl.ds(off, tk)
            for h in heads:
                acc_ref[...] += _dot(p_ref[2 * slot + h], vm_ref[h, keys, :])
            for h in heads:
                r_ref[h] += _rep(suf_ref[2 * slot + h, :, 0:1])

        _tiles_reversed(i, tile, lambda nearest: jnp.max(r_ref[...]) >= VANISH)
        o_ref[...] = acc_ref[...].astype(BF16)
        t_ref[...] = acc_ref[...]

    W = n_pairs * PAIR
    return _call_carrying(
        body, exchange, "sb_fwd", (n_pairs, S // tq), _qkv_specs(S, tq, n_pairs, base),
        [pl.BlockSpec((tq, PAIR), lambda p, i: (i, p)), pl.BlockSpec((tq, PAIR), lambda p, i: (i, p))],
        [jax.ShapeDtypeStruct((S, W), BF16), jax.ShapeDtypeStruct((S, W), F32)],
        [pltpu.VMEM((4, tq, tk), F32), pltpu.VMEM((4, tq, tk), BF16),
         pltpu.VMEM((4, tq, tk), F32), pltpu.VMEM((4, tq, tk), BF16), pltpu.VMEM((2, tq, 128), F32),
         pltpu.VMEM((tq, PAIR), F32), pltpu.VMEM((2, S, PAIR), BF16)],
        (qkv, qkv, qkv))


def _sb_bwd2(qkv, o32, do, n_pairs, base, tq, exchange=None):
    S = qkv.shape[0]
    tk = tq
    nq = S // tq
    scale = HEAD_DIM ** -0.5

    def body(q_ref, k_ref, v_ref, o_ref, do_ref, dq_ref, dk_ref, dv_ref,
             z_ref, g_ref, omb_ref, cum_ref, hi_ref, lo_ref, a_ref, dz_ref,
             r_ref, cg_ref, dl_ref, dq_acc, dk_acc, dv_acc, ks_ref):
        i = pl.program_id(1)
        masks = _head_masks()

        @pl.when(i == 0)
        def _():
            dk_acc[...] = jnp.zeros_like(dk_acc)
            dv_acc[...] = jnp.zeros_like(dv_acc)
            _fill_masked(ks_ref, k_ref, masks, mul=scale)

        q2 = q_ref[...] * scale
        do2 = do_ref[...]
        qm = [_sel(m, q2) for m in masks]
        dom = [_sel(m, do2) for m in masks]
        prod = do2.astype(F32) * o_ref[...]
        for h in range(2):
            dl_ref[h] = _rep(jnp.sum(jnp.where(masks[h], prod, 0.0), axis=-1, keepdims=True))
        suffix = _tri(tk, lambda r, c: r >= c)
        r_ref[...] = jnp.zeros_like(r_ref)
        cg_ref[...] = jnp.zeros_like(cg_ref)
        dq_acc[...] = jnp.zeros_like(dq_acc)

        def tile(j, diag, slot, heads):
            off = pl.multiple_of(j * tk, tk)
            k2 = k_ref[pl.ds(off, tk), :]
            v2 = v_ref[pl.ds(off, tk), :]
            for h in heads:
                z_ref[2 * slot + h] = _dot(qm[h], k2, 1, 1)
                g_ref[2 * slot + h] = _dot(dom[h], v2, 1, 1)
            yield
            for h in heads:
                def split(r0, h=h):
                    rows = pl.ds(r0, RC)
                    sp = _softplus(z_ref[2 * slot + h, rows, :])
                    omb_ref[2 * slot + h, rows, :] = jnp.exp(-sp)
                    lg = -sp
                    if diag:
                        lg = jnp.where(_diag_mask(r0, tk, True), lg, 0.0)
                    hi_ref[2 * slot + h, rows, :] = lg.astype(BF16)
                _chunks(tq, split)
            yield
            for h in heads:
                cum_ref[2 * slot + h] = _dot(hi_ref[2 * slot + h], suffix)
            yield
            for h in heads:
                def weights(r0, h=h):
                    rows = pl.ds(r0, RC)
                    a = jnp.exp(z_ref[2 * slot + h, rows, :] + cum_ref[2 * slot + h, rows, :] + _wide(r_ref[h, rows, :], tk))
                    if diag:
                        a = jnp.where(_diag_mask(r0, tk, True), a, 0.0)
                    ab = a.astype(BF16)
                    g = g_ref[2 * slot + h, rows, :] * ab.astype(F32)
                    g_ref[2 * slot + h, rows, :] = g
                    a_ref[2 * slot + h, rows, :] = ab
                    hi, lo = _split2(g)
                    hi_ref[2 * slot + h, rows, :] = hi
                    lo_ref[2 * slot + h, rows, :] = lo
                _chunks(tq, weights)
            for h in heads:
                r_ref[h] += _rep(cum_ref[2 * slot + h, :, 0:1])
            yield
            for h in heads:
                cum_ref[2 * slot + h] = _dot(hi_ref[2 * slot + h], suffix) + _dot(lo_ref[2 * slot + h], suffix)
            yield
            for h in heads:
                def dscore(r0, h=h):
                    rows = pl.ds(r0, RC)
                    g = g_ref[2 * slot + h, rows, :]
                    from_here = cum_ref[2 * slot + h, rows, :] + _wide(cg_ref[h, rows, :], tk)
                    before = _wide(dl_ref[h, rows, :], tk) - from_here
                    omb = omb_ref[2 * slot + h, rows, :]
                    dz = g * omb - (1.0 - omb) * before
                    if diag:
                        dz = jnp.where(_diag_mask(r0, tk, True), dz, 0.0)
                    dz_ref[2 * slot + h, rows, :] = dz.astype(BF16)
                _chunks(tq, dscore)
            for h in heads:
                cg_ref[h] += _rep(cum_ref[2 * slot + h, :, 0:1])
            yield
            keys = pl.ds(off, tk)
            for h in heads:
                dq_acc[...] += _dot(dz_ref[2 * slot + h], ks_ref[h, keys, :])
                dk_acc[keys, :] += _dot(dz_ref[2 * slot + h], qm[h], 0, 0)
                dv_acc[keys, :] += _dot(a_ref[2 * slot + h], dom[h], 0, 0)

        _tiles_reversed(i, tile, lambda nearest: jnp.max(r_ref[...]) >= VANISH)
        dq_ref[...] = dq_acc[...].astype(BF16)

        @pl.when(i == nq - 1)
        def _():
            dk_ref[...] = dk_acc[...].astype(BF16)
            dv_ref[...] = dv_acc[...].astype(BF16)

    W = n_pairs * PAIR
    in_specs = _qkv_specs(S, tq, n_pairs, base) + [
        pl.BlockSpec((tq, PAIR), lambda p, i: (i, p)),
        pl.BlockSpec((tq, PAIR), lambda p, i: (i, p))]
    out_specs = [pl.BlockSpec((tq, PAIR), lambda p, i: (i, p)),
                 pl.BlockSpec((S, PAIR), lambda p, i: (0, p)),
                 pl.BlockSpec((S, PAIR), lambda p, i: (0, p))]
    big, stat = (4, tq, tk), (2, tq, 128)
    return _call_carrying(
        body, exchange, "sb_bwd", (n_pairs, nq), in_specs, out_specs, [jax.ShapeDtypeStruct((S, W), BF16)] * 3,
        [pltpu.VMEM(big, F32)] * 4 + [pltpu.VMEM(big, BF16)] * 4 + [pltpu.VMEM(stat, F32)] * 3
        + [pltpu.VMEM((tq, PAIR), F32), pltpu.VMEM((S, PAIR), F32), pltpu.VMEM((S, PAIR), F32),
           pltpu.VMEM((2, S, PAIR), BF16)],
        (qkv, qkv, qkv, o32, do))


def _fox_fwd2(qkv, fcx, fcr, n_pairs, base, tq, exchange=None):
    S = qkv.shape[0]
    tk = tq
    scale = HEAD_DIM ** -0.5
    spare = (HEAD_DIM, 0)

    def body(q_ref, k_ref, v_ref, fq_ref, fk_ref, o_ref, lse_ref, s_ref, p_ref, m_ref, al_ref, fqr_ref, acc_ref, vm_ref):
        i = pl.program_id(1)
        masks = _head_masks()

        @pl.when(i == 0)
        def _():
            _fill_masked(vm_ref, v_ref, masks, ones_lane=spare)

        q2 = q_ref[...] * scale
        qm = [_sel(m, q2) for m in masks]
        f0, f1 = _per_head(fq_ref[...], masks)
        fqr_ref[0] = f0
        fqr_ref[1] = f1
        m_ref[...] = jnp.full(m_ref.shape, NEG, F32)
        acc_ref[...] = jnp.zeros_like(acc_ref)

        def tile(j, diag, slot, heads):
            off = pl.multiple_of(j * tk, tk)
            k2 = k_ref[pl.ds(off, tk), :]
            v2 = v_ref[pl.ds(off, tk), :]
            fk2 = fk_ref[0, j]
            for h in heads:
                s_ref[2 * slot + h] = _dot(qm[h], k2, 1, 1)
            yield
            for h in heads:
                fk_row = fk2[h:h + 1, :]

                def probs(r0, h=h, fk_row=fk_row):
                    rows = pl.ds(r0, RC)
                    sv = s_ref[2 * slot + h, rows, :] - fk_row
                    if diag:
                        sv = jnp.where(_diag_mask(r0, tk, False), sv, NEG)
                    fq = fqr_ref[h, rows, :]
                    m_prev = m_ref[h, rows, :]
                    m_new = jnp.maximum(m_prev, jnp.max(sv, axis=-1, keepdims=True) + fq)
                    p_ref[2 * slot + h, rows, :] = jnp.exp(sv + _wide(fq - m_new, tk)).astype(BF16)
                    al_ref[2 * slot + h, rows, :] = jnp.exp(m_prev - m_new)
                    m_ref[h, rows, :] = m_new
                _chunks(tq, probs)
            yield
            for h in heads:
                acc_ref[h] = acc_ref[h] * al_ref[2 * slot + h] + _dot(p_ref[2 * slot + h], vm_ref[h, pl.ds(off, tk), :])

        _tiles(i, tile)
        a0, a1 = acc_ref[0], acc_ref[1]
        l0 = _rep(a0[:, spare[0]:spare[0] + 1])
        l1 = _rep(a1[:, spare[1]:spare[1] + 1])
        o_ref[...] = jnp.where(masks[0], a0 / l0, a1 / l1).astype(BF16)
        lse_ref[...] = jnp.where(masks[0], m_ref[0] + jnp.log(l0), m_ref[1] + jnp.log(l1))

    W = n_pairs * PAIR
    return _call_carrying(
        body, exchange, "fox_fwd", (n_pairs, S // tq), _fox_specs(S, tq, n_pairs, base),
        [pl.BlockSpec((tq, PAIR), lambda p, i: (i, p)), pl.BlockSpec((tq, PAIR), lambda p, i: (i, p))],
        [jax.ShapeDtypeStruct((S, W), BF16), jax.ShapeDtypeStruct((S, W), F32)],
        [pltpu.VMEM((4, tq, tk), F32), pltpu.VMEM((4, tq, tk), BF16), pltpu.VMEM((2, tq, 128), F32),
         pltpu.VMEM((4, tq, 128), F32), pltpu.VMEM((2, tq, 128), F32), pltpu.VMEM((2, tq, 128), F32),
         pltpu.VMEM((2, S, PAIR), BF16)],
        (qkv, qkv, qkv, fcx, fcr))


def _fox_bwd2(qkv, fcx, fcr, o, lse, do, n_pairs, base, tq, exchange=None):
    S = qkv.shape[0]
    tk = tq
    nq = S // tq
    scale = HEAD_DIM ** -0.5

    def body(q_ref, k_ref, v_ref, fq_ref, fk_ref, o_ref, lse_ref, do_ref,
             dq_ref, dk_ref, dv_ref, dfq_ref, dfk_ref,
             s_ref, dp_ref, p_ref, ds_ref, row_ref, dl_ref, dfq_acc, col_ref, dq_acc, dk_acc, dv_acc, ks_ref):
        i = pl.program_id(1)
        masks = _head_masks()

        @pl.when(i == 0)
        def _():
            dk_acc[...] = jnp.zeros_like(dk_acc)
            dv_acc[...] = jnp.zeros_like(dv_acc)
            dfk_ref[...] = jnp.zeros_like(dfk_ref)
            _fill_masked(ks_ref, k_ref, masks, mul=scale)

        q2 = q_ref[...] * scale
        do2 = do_ref[...]
        qm = [_sel(m, q2) for m in masks]
        dom = [_sel(m, do2) for m in masks]
        f0, f1 = _per_head(fq_ref[...], masks)
        l0, l1 = _per_head(lse_ref[...], masks)
        row_ref[0] = f0 - l0
        row_ref[1] = f1 - l1
        prod = do2.astype(F32) * o_ref[...].astype(F32)
        for h in range(2):
            dl_ref[h] = _rep(jnp.sum(jnp.where(masks[h], prod, 0.0), axis=-1, keepdims=True))
        dfq_acc[...] = jnp.zeros_like(dfq_acc)
        dq_acc[...] = jnp.zeros_like(dq_acc)

        def tile(j, diag, slot, heads):
            off = pl.multiple_of(j * tk, tk)
            k2 = k_ref[pl.ds(off, tk), :]
            v2 = v_ref[pl.ds(off, tk), :]
            fk2 = fk_ref[0, j]
            for h in heads:
                s_ref[2 * slot + h] = _dot(qm[h], k2, 1, 1)
                dp_ref[2 * slot + h] = _dot(dom[h], v2, 1, 1)
            yield
            for h in heads:
                col_ref[2 * slot + h] = jnp.zeros((8, tk), F32)
                fk_row = fk2[h:h + 1, :]

                def dscore(r0, h=h, fk_row=fk_row):
                    rows = pl.ds(r0, RC)
                    p = jnp.exp(s_ref[2 * slot + h, rows, :] - fk_row + _wide(row_ref[h, rows, :], tk))
                    if diag:
                        p = jnp.where(_diag_mask(r0, tk, False), p, 0.0)
                    ds = p * (dp_ref[2 * slot + h, rows, :] - _wide(dl_ref[h, rows, :], tk))
                    p_ref[2 * slot + h, rows, :] = p.astype(BF16)
                    ds_ref[2 * slot + h, rows, :] = ds.astype(BF16)
                    dfq_acc[h, rows, :] += _rep(jnp.sum(ds, axis=-1, keepdims=True))
                    col_ref[2 * slot + h] += jnp.sum(ds.reshape(RC // 8, 8, tk), axis=0)
                _chunks(tq, dscore)
            yield
            keys = pl.ds(off, tk)
            for h in heads:
                dq_acc[...] += _dot(ds_ref[2 * slot + h], ks_ref[h, keys, :])
                dk_acc[keys, :] += _dot(ds_ref[2 * slot + h], qm[h], 0, 0)
                dv_acc[keys, :] += _dot(p_ref[2 * slot + h], dom[h], 0, 0)
            for h in heads:
                dfk_ref[0, j, h:h + 1, :] += jnp.sum(col_ref[2 * slot + h], axis=0, keepdims=True)

        _tiles(i, tile)
        dq_ref[...] = dq_acc[...].astype(BF16)
        dfq_ref[...] = jnp.where(masks[0], dfq_acc[0], dfq_acc[1])

        @pl.when(i == nq - 1)
        def _():
            dk_ref[...] = dk_acc[...].astype(BF16)
            dv_ref[...] = dv_acc[...].astype(BF16)

    W = n_pairs * PAIR
    in_specs = _fox_specs(S, tq, n_pairs, base) + [
        pl.BlockSpec((tq, PAIR), lambda p, i: (i, p)),
        pl.BlockSpec((tq, PAIR), lambda p, i: (i, p)),
        pl.BlockSpec((tq, PAIR), lambda p, i: (i, p))]
    out_specs = [pl.BlockSpec((tq, PAIR), lambda p, i: (i, p)),
                 pl.BlockSpec((S, PAIR), lambda p, i: (0, p)),
                 pl.BlockSpec((S, PAIR), lambda p, i: (0, p)),
                 pl.BlockSpec((tq, PAIR), lambda p, i: (i, p)),
                 pl.BlockSpec((1, nq, 8, tk), lambda p, i: (p, 0, 0, 0))]
    return _call_carrying(
        body, exchange, "fox_bwd", (n_pairs, nq), in_specs, out_specs,
        [jax.ShapeDtypeStruct((S, W), BF16)] * 3
        + [jax.ShapeDtypeStruct((S, W), F32), jax.ShapeDtypeStruct((n_pairs, nq, 8, tk), F32)],
        [pltpu.VMEM((4, tq, tk), F32)] * 2 + [pltpu.VMEM((4, tq, tk), BF16)] * 2
        + [pltpu.VMEM((2, tq, 128), F32)] * 3 + [pltpu.VMEM((4, 8, tk), F32)]
        + [pltpu.VMEM((tq, PAIR), F32), pltpu.VMEM((S, PAIR), F32), pltpu.VMEM((S, PAIR), F32),
           pltpu.VMEM((2, S, PAIR), BF16)],
        (qkv, qkv, qkv, fcx, fcr, o, lse, do))


def _swiglu_fwd(u2, wg, wu):
    S, D = u2.shape
    FF = wg.shape[1]
    tm, tn = _pick(S, (512, 256, 128)), _divisors(FF, 1536)[0]

    def body(u_ref, g_ref, w_ref, a_ref, b_ref, h_ref):
        u = u_ref[...]
        a = _dot(u, g_ref[...])
        b = _dot(u, w_ref[...])
        a_ref[...] = a.astype(BF16)
        b_ref[...] = b.astype(BF16)
        h_ref[...] = (a / (1.0 + jnp.exp(-a)) * b).astype(BF16)

    spec_o = pl.BlockSpec((tm, tn), lambda i, j: (i, j))
    return pl.pallas_call(
        body, name="swiglu_fwd", grid=(S // tm, FF // tn),
        in_specs=[pl.BlockSpec((tm, D), lambda i, j: (i, 0)),
                  pl.BlockSpec((D, tn), lambda i, j: (0, j)),
                  pl.BlockSpec((D, tn), lambda i, j: (0, j))],
        out_specs=[spec_o] * 3, out_shape=[jax.ShapeDtypeStruct((S, FF), BF16)] * 3,
        compiler_params=_params(("parallel", "parallel")),
    )(u2, wg, wu)


def _swiglu_bwd(dh, wd, a, b):
    S, D = dh.shape
    FF = wd.shape[0]
    tm, tn = _pick(S, (512, 256, 128)), _divisors(FF, 1536)[0]

    def body(dh_ref, w_ref, a_ref, b_ref, da_ref, db_ref):
        dhin = _dot(dh_ref[...], w_ref[...], 1, 1)
        av = a_ref[...].astype(F32)
        bv = b_ref[...].astype(F32)
        sig = 1.0 / (1.0 + jnp.exp(-av))
        da_ref[...] = (dhin * bv * (sig * (1.0 + av * (1.0 - sig)))).astype(BF16)
        db_ref[...] = (dhin * (av * sig)).astype(BF16)

    spec_o = pl.BlockSpec((tm, tn), lambda i, j: (i, j))
    return pl.pallas_call(
        body, name="swiglu_bwd", grid=(S // tm, FF // tn),
        in_specs=[pl.BlockSpec((tm, D), lambda i, j: (i, 0)),
                  pl.BlockSpec((tn, D), lambda i, j: (j, 0)), spec_o, spec_o],
        out_specs=[spec_o] * 2, out_shape=[jax.ShapeDtypeStruct((S, FF), BF16)] * 2,
        compiler_params=_params(("parallel", "parallel")),
    )(dh, wd, a, b)


def _gate_bwd(dmix, wo, g_sb, g_fx, y_sb, y_fx):
    S, D = dmix.shape
    tm, tn = _pick(S, (512, 256, 128)), _divisors(D, 1024)[0]

    def body(dm_ref, w_ref, gs_ref, gf_ref, ys_ref, yf_ref, dys_ref, dyf_ref, dls_ref, dlf_ref, bs_ref, bf_ref):
        @pl.when(pl.program_id(1) == 0)
        def _():
            bs_ref[...] = jnp.zeros_like(bs_ref)
            bf_ref[...] = jnp.zeros_like(bf_ref)

        dmi = _dot(dm_ref[...], w_ref[...], 1, 1)
        gs, gf = gs_ref[...], gf_ref[...]
        dys_ref[...] = (dmi * gs).astype(BF16)
        dyf_ref[...] = (dmi * gf).astype(BF16)
        dls = dmi * ys_ref[...] * gs * (1.0 - gs)
        dlf = dmi * yf_ref[...] * gf * (1.0 - gf)
        dls_ref[...] = dls.astype(BF16)
        dlf_ref[...] = dlf.astype(BF16)
        bs_ref[0:1, :] += _colsum(dls)
        bf_ref[0:1, :] += _colsum(dlf)

    t = pl.BlockSpec((tm, tn), lambda j, i: (i, j))
    accs = pl.BlockSpec((8, tn), lambda j, i: (0, j))
    return pl.pallas_call(
        body, name="gate_bwd", grid=(D // tn, S // tm),
        in_specs=[pl.BlockSpec((tm, D), lambda j, i: (i, 0)),
                  pl.BlockSpec((tn, D), lambda j, i: (j, 0)), t, t, t, t],
        out_specs=[t, t, t, t, accs, accs],
        out_shape=[jax.ShapeDtypeStruct((S, D), BF16)] * 4 + [jax.ShapeDtypeStruct((8, D), F32)] * 2,
        compiler_params=_params(("parallel", "arbitrary")),
    )(dmix, wo, g_sb, g_fx, y_sb, y_fx)


def _local_step(x, target, ada8, lnp8, bg_sb, bg_fx, bf_pad, wqkv, wf, wgs, wgf, gather, later_weights, pack_early,
                pack_last):
    S, D = x.shape
    W = wqkv.shape[1] // 6
    n_pairs = W // PAIR
    n_heads = W // HEAD_DIM
    ts = _pick(S, (512, 256, 128))
    tq = _pick(S, (256, 128))

    u1 = _ln_mod(x, ada8, ts)
    qkv = _mm([(u1, wqkv)], 'nn', BF16, "in_qkv")
    f = _mm([(u1, wf)], 'nn', F32, "in_f")
    g_sb = _mm([(u1, wgs)], 'nn', F32, "in_gsb", bias=bg_sb, act='sigmoid')
    g_fx = _mm([(u1, wgf)], 'nn', F32, "in_gfx", bias=bg_fx, act='sigmoid')
    fc = _fgate_fwd(f, bf_pad, _pick(S, (512, 256, 128)))
    fch = fc[:, :n_heads]
    fcx = jnp.repeat(fch, HEAD_DIM, axis=1)
    nq = S // tq
    fcr = jnp.pad(fch.T.reshape(n_pairs, 2, nq, tq).transpose(0, 2, 1, 3),
                  ((0, 0), (0, 0), (0, 6), (0, 0)))
    o_sb, o_sb32, *zone_a = _sb_fwd2(qkv, n_pairs, 0, tq, gather[0])
    o_fx, lse, *zone_b = _fox_fwd2(qkv, fcx, fcr, n_pairs, 3 * n_pairs, tq, gather[1])
    wsb, wfx, wo, wfg, wfu, wfd = later_weights(zone_a[0] if zone_a else None, zone_b[0] if zone_b else None)
    y_sb =_mm([(o_sb, wsb)], 'nn', F32, "out_sb")
    y_fx = _mm([(o_fx, wfx)], 'nn', F32, "out_fx")
    mix_in = _gate_mix(g_sb, g_fx, y_sb, y_fx, ts)
    mix = _mm([(mix_in, wo)], 'nn', F32, "out_o")
    x1, u2 = _post_attn(x, mix, ada8, lnp8, ts)
    a, b, hin = _swiglu_fwd(u2, wfg, wfu)
    h = _mm([(hin, wfd)], 'nn', F32, "ffn_down")
    dr2, dh, st_loss = _loss_head(x1, h, target, ada8, lnp8, ts)

    da, db = _swiglu_bwd(dh, wfd, a, b)
    g_wfd = _mm([(hin, dh)], 'tn', F32, "g_ffn_down")
    du2 = _mm([(da, wfg), (db, wfu)], 'nt', F32, "d_u2")
    g_wfg = _mm([(u2, da)], 'tn', F32, "g_ffn_gate")
    g_wfu = _mm([(u2, db)], 'tn', F32, "g_ffn_up")
    dr1, dmix, st_mid = _mid_bwd(du2, x1, dr2, mix, x, ada8, lnp8, ts)
    dys, dyf, dls, dlf, gb_sb, gb_fx = _gate_bwd(dmix, wo, g_sb, g_fx, y_sb, y_fx)
    g_wo = _mm([(mix_in, dmix)], 'tn', F32, "g_w_o")
    do_sb = _mm([(dys, wsb)], 'nt', BF16, "d_o_sb")
    do_fx = _mm([(dyf, wfx)], 'nt', BF16, "d_o_fx")
    g_wsb = _mm([(o_sb, dys)], 'tn', F32, "g_sb_out")
    g_wfx = _mm([(o_fx, dyf)], 'tn', F32, "g_fox_out")
    scatter = pack_early(dict(sb=g_wsb, fx=g_wfx, o=g_wo, fg=g_wfg, fu=g_wfu, fd=g_wfd))
    dq_s, dk_s, dv_s, *zone_a = _sb_bwd2(qkv, o_sb32, do_sb, n_pairs, 0, tq, scatter[0])
    dq_f, dk_f, dv_f, dfq, dfk, *zone_b = _fox_bwd2(qkv, fcx, fcr, o_fx, lse, do_fx, n_pairs, 3 * n_pairs, tq,
                                                    scatter[1])
    early = [(scatter[0], zone_a[0] if zone_a else None), (scatter[1], zone_b[0] if zone_b else None)]
    dfc = dfq[:, ::HEAD_DIM] - dfk[:, :, :2, :].transpose(0, 2, 1, 3).reshape(n_heads, S).T
    dfc = jnp.pad(dfc, ((0, 0), (0, 128 - n_heads)))
    df, gb_f = _fgate_bwd(dfc, f, bf_pad, _pick(S, (512, 256, 128)))
    grads = [dq_s, dk_s, dv_s, dq_f, dk_f, dv_f]
    g_wqkv = [_mm([(u1, g)], 'tn', F32, "g_in_%d" % n) for n, g in enumerate(grads)]
    g_wf = _mm([(u1, df)], 'tn', F32, "g_in_f")
    g_wgs = _mm([(u1, dls)], 'tn', F32, "g_in_gsb")
    g_wgf = _mm([(u1, dlf)], 'tn', F32, "g_in_gfx")
    wgrads = dict(qkv=g_wqkv, f=g_wf, gs=g_wgs, gf=g_wgf)
    last = pack_last(wgrads)
    du1 = _mm([(g, wqkv, W, 0, n) for n, g in enumerate(grads)] + [(df, wf), (dls, wgs), (dlf, wgf)],
              'nt', F32, "d_u1", exchange=last)
    du1, last_zone = du1 if last is not None else (du1, None)
    gx, st_first = _first_bwd(du1, x, dr1, ada8, ts)

    stats = dict(loss=st_loss, mid=st_mid, first=st_first, gb_sb=gb_sb, gb_fx=gb_fx, gb_f=gb_f)
    return gx, wgrads, stats, early, (last, last_zone)


def _position():
    x, y, c = lax.axis_index("x"), lax.axis_index("y"), lax.axis_index("c")
    return x, y, c, 4 * x + 2 * y + c


def _flip(x, y, c, k):
    px = 1 - x if k & 4 else x
    py = 1 - y if k & 2 else y
    pc = 1 - c if k & 1 else c
    return (px, py, pc), 4 * px + 2 * py + pc


def _all_gather_small(v, name):
    r, n = v.shape

    def body(x_ref, out_ref, send_sems, recv_sems, local_sem):
        x, y, c, me = _position()
        mine = pltpu.make_async_copy(x_ref, out_ref.at[me], local_sem)
        mine.start()
        sends = []
        for k in range(1, N_DEV):
            peer, _ = _flip(x, y, c, k)
            cp = pltpu.make_async_remote_copy(
                src_ref=x_ref, dst_ref=out_ref.at[me], send_sem=send_sems.at[k - 1], recv_sem=recv_sems.at[k - 1],
                device_id=peer, device_id_type=MESH)
            cp.start()
            sends.append(cp)
        for k in range(1, N_DEV):
            peer, slot = _flip(x, y, c, k)
            pltpu.make_async_remote_copy(
                src_ref=x_ref, dst_ref=out_ref.at[slot], send_sem=send_sems.at[k - 1], recv_sem=recv_sems.at[k - 1],
                device_id=peer, device_id_type=MESH).wait_recv()
        for cp in sends:
            cp.wait_send()
        mine.wait()

    return pl.pallas_call(
        body, name=name, out_shape=jax.ShapeDtypeStruct((N_DEV, r, n), v.dtype),
        in_specs=[pl.BlockSpec(memory_space=pltpu.VMEM)], out_specs=pl.BlockSpec(memory_space=pltpu.VMEM),
        scratch_shapes=[pltpu.SemaphoreType.DMA((N_DEV - 1,)), pltpu.SemaphoreType.DMA((N_DEV - 1,)),
                        pltpu.SemaphoreType.DMA],
    )(v)


def _all_gather_weights(packed):
    R, C = packed.shape

    def body(x_ref, out_ref, send_sems, recv_sems, local_sem):
        x, y, c, me = _position()
        sibling, sib_slot = _flip(x, y, c, 1)
        mine = pltpu.make_async_copy(x_ref, out_ref.at[me], local_sem)
        mine.start()

        def copy(k, slot, to, src=None):
            return pltpu.make_async_remote_copy(
                src_ref=out_ref.at[slot] if src is None else src, dst_ref=out_ref.at[slot],
                send_sem=send_sems.at[k], recv_sem=recv_sems.at[k], device_id=to, device_id_type=MESH)

        first = [copy(0, me, sibling, src=x_ref)]
        chips = (4, 2, 6)
        for n, k in enumerate(chips):
            peer, _ = _flip(x, y, c, k)
            first.append(copy(1 + n, me, peer, src=x_ref))
        for cp in first:
            cp.start()
        passed = []
        for n, k in enumerate(chips):
            peer, slot = _flip(x, y, c, k)
            copy(1 + n, slot, peer).wait_recv()
            cp = copy(4 + n, slot, sibling)
            cp.start()
            passed.append(cp)
        copy(0, sib_slot, sibling).wait_recv()
        for n, k in enumerate(chips):
            _, slot = _flip(x, y, c, k | 1)
            copy(4 + n, slot, sibling).wait_recv()
        for cp in first + passed:
            cp.wait_send()
        mine.wait()

    return pl.pallas_call(
        body, name="all_gather_weights", out_shape=jax.ShapeDtypeStruct((N_DEV, R, C), packed.dtype),
        in_specs=[pl.BlockSpec(memory_space=pl.ANY)], out_specs=pl.BlockSpec(memory_space=pl.ANY),
        scratch_shapes=[pltpu.SemaphoreType.DMA((7,)), pltpu.SemaphoreType.DMA((7,)), pltpu.SemaphoreType.DMA],
    )(packed)


def _exchange_grads(gpack):
    _, R, C = gpack.shape

    def body(g_ref, out_ref, send_sems, recv_sems, local_sem):
        x, y, c, me = _position()
        mine = pltpu.make_async_copy(g_ref.at[me], out_ref.at[me], local_sem)
        mine.start()
        sends = []
        for k in range(1, N_DEV):
            peer, slot = _flip(x, y, c, k)
            cp = pltpu.make_async_remote_copy(
                src_ref=g_ref.at[slot], dst_ref=out_ref.at[me], send_sem=send_sems.at[k - 1],
                recv_sem=recv_sems.at[k - 1], device_id=peer, device_id_type=MESH)
            cp.start()
            sends.append(cp)
        for k in range(1, N_DEV):
            peer, slot = _flip(x, y, c, k)
            pltpu.make_async_remote_copy(
                src_ref=g_ref.at[slot], dst_ref=out_ref.at[slot], send_sem=send_sems.at[k - 1],
                recv_sem=recv_sems.at[k - 1], device_id=peer, device_id_type=MESH).wait_recv()
        for cp in sends:
            cp.wait_send()
        mine.wait()

    return pl.pallas_call(
        body, name="exchange_grads", out_shape=jax.ShapeDtypeStruct((N_DEV, R, C), gpack.dtype),
        in_specs=[pl.BlockSpec(memory_space=pl.ANY)], out_specs=pl.BlockSpec(memory_space=pl.ANY),
        scratch_shapes=[pltpu.SemaphoreType.DMA((N_DEV - 1,)), pltpu.SemaphoreType.DMA((N_DEV - 1,)),
                        pltpu.SemaphoreType.DMA],
    )(gpack)


def _own_slot(land, own):
    me = 4 * lax.axis_index("x") + 2 * lax.axis_index("y") + lax.axis_index("c")
    return lax.dynamic_update_slice(land, own[None], (me, 0, 0))


def _sum_slots(recv, name, tr):
    n, R, C = recv.shape

    def body(r_ref, o_ref):
        acc = r_ref[0].astype(F32)
        for s in range(1, n):
            acc = acc + r_ref[s].astype(F32)
        o_ref[...] = acc

    return pl.pallas_call(
        body, name=name, grid=(R // tr,), in_specs=[pl.BlockSpec((n, tr, C), lambda i: (0, i, 0))],
        out_specs=pl.BlockSpec((tr, C), lambda i: (i, 0)), out_shape=jax.ShapeDtypeStruct((R, C), F32),
        compiler_params=_params(("parallel",)),
    )(recv)


def _sum_stats(st_all, loss_row):
    n, r, D = st_all.shape

    def body(s_ref, o_ref, l_ref):
        acc = s_ref[0]
        for d in range(1, n):
            acc = acc + s_ref[d]
        o_ref[...] = acc
        l_ref[...] = jnp.zeros((8, 128), F32) + jnp.sum(acc[loss_row:loss_row + 1, :], axis=-1, keepdims=True)

    return pl.pallas_call(
        body, name="sum_stats", out_shape=[jax.ShapeDtypeStruct((r, D), F32), jax.ShapeDtypeStruct((8, 128), F32)],
    )(st_all)


def _adamw(w, g, m, v, name):
    R, C = w.shape
    tr = _pick(R, (256, 176, 128, 64, 32, 16, 8))
    c1 = 1.0 / (1.0 - ADAM_B1 ** ADAM_STEP)
    c2 = 1.0 / (1.0 - ADAM_B2 ** ADAM_STEP)

    def body(w_ref, g_ref, m_ref, v_ref, d_ref, nm_ref, nv_ref):
        gv = g_ref[...]
        nm = ADAM_B1 * m_ref[...] + (1.0 - ADAM_B1) * gv
        nv = ADAM_B2 * v_ref[...] + (1.0 - ADAM_B2) * (gv * gv)
        nm_ref[...] = nm
        nv_ref[...] = nv
        d_ref[...] = -ADAM_LR * ((nm * c1) / (jnp.sqrt(nv * c2) + ADAM_EPS) + ADAM_WD * w_ref[...])

    spec = pl.BlockSpec((tr, C), lambda i: (i, 0))
    return pl.pallas_call(
        body, name=name, grid=(R // tr,), in_specs=[spec] * 4, out_specs=[spec] * 3,
        out_shape=[jax.ShapeDtypeStruct((R, C), F32)] * 3, compiler_params=_params(("parallel",)),
    )(w, g, m, v)


def _round16(n):
    return -(-n // 16) * 16


def _pack_layout(D, in_cols, ff, W):
    parts = [("in", D * (in_cols // N_DEV) // D), ("fg", ff // N_DEV), ("fu", ff // N_DEV),
             ("sb", W * (D // N_DEV) // D), ("fx", W * (D // N_DEV) // D), ("o", D // N_DEV), ("fd", ff // N_DEV)]
    layout, off = {}, 0
    for nm, rows in parts:
        layout[nm] = (off, rows)
        off += _round16(rows)
    return layout, off


def _rows_of(a, D, rows):
    a = a.reshape(rows, D)
    return jnp.pad(a, ((0, _round16(rows) - rows), (0, 0)))


def _cols_to_dest(g, D):
    K, N = g.shape
    n = N // N_DEV
    return g.reshape(K, N_DEV, n).transpose(1, 0, 2).reshape(N_DEV, K * n // D, D)


def _cols_from_src(blocks, K, n):
    return blocks.reshape(N_DEV, K, n).transpose(1, 0, 2).reshape(K, N_DEV * n)


def _pad_rows16(a):
    rows = a.shape[1]
    return jnp.pad(a, ((0, 0), (0, _round16(rows) - rows), (0, 0)))


def kernel(x, c, w_ada, b_ada, w_in, b_gate, b_forget, w_sb_out, w_fox_out, w_o, ln1_g, ln1_b, w_ffn_gate, w_ffn_up, w_ffn_down, ln2_g, ln2_b, loss_target, m_w_ada, m_b_ada, m_w_in, m_b_gate, m_b_forget, m_w_sb_out, m_w_fox_out, m_w_o, m_ln1_g, m_ln1_b, m_w_ffn_gate, m_w_ffn_up, m_w_ffn_down, m_ln2_g, m_ln2_b, v_w_ada, v_b_ada, v_w_in, v_b_gate, v_b_forget, v_w_sb_out, v_w_fox_out, v_w_o, v_ln1_g, v_ln1_b, v_w_ffn_gate, v_w_ffn_up, v_w_ffn_down, v_ln2_g, v_ln2_b):
    S, D = x.shape[1], x.shape[2]
    W = w_sb_out.shape[1]
    n_heads = b_forget.shape[1]
    ff = w_ffn_down.shape[1] * N_DEV
    in_loc = w_in.shape[2]
    in_cols = in_loc * N_DEV
    ada_loc = w_ada.shape[2]
    n_cond = ada_loc * N_DEV // D
    assert w_ada.shape[0] == 1 and n_cond == 6 and in_cols == 6 * W + n_heads + 2 * D and n_heads <= 128
    me = 4 * lax.axis_index("x") + 2 * lax.axis_index("y") + lax.axis_index("c")

    c_all = _all_gather_small(c, "gather_c").reshape(N_DEV, D)
    c16 = jnp.pad(c_all, ((0, 16 - N_DEV), (0, 0)))
    b_cols = lax.dynamic_slice(b_ada, (0, me * ada_loc), (1, ada_loc))
    ada_cols = _mm([(c16, w_ada[0])], 'nn', F32, "ada_fwd", bias=b_cols, silu_a=True)[:N_DEV]
    ada_all = _all_gather_small(ada_cols, "gather_ada")
    ada_me = lax.dynamic_index_in_dim(ada_all, me, axis=1, keepdims=False)
    ada8 = jnp.pad(ada_me.reshape(n_cond, D), ((0, 8 - n_cond), (0, 0)))
    lnp8 = jnp.concatenate([ln1_g, ln1_b, ln2_g, ln2_b, jnp.zeros((4, D), F32)], axis=0)

    layout, R = _pack_layout(D, in_cols, ff, W)
    shards = dict(**{"in": w_in[0]}, fg=w_ffn_gate[0], fu=w_ffn_up[0], sb=w_sb_out[0], fx=w_fox_out[0], o=w_o[0],
                  fd=w_ffn_down[0])
    fwd_groups = (("sb", "fx", "o"), ("fg", "fu", "fd"))
    bwd_groups = (("fd", "o", "sb", "fx"), ("fg", "fu"))
    rows_of = {nm: layout[nm][1] for nm in layout}

    def offsets(names):
        offs, off = {}, 0
        for nm in names:
            offs[nm] = off
            off += _round16(rows_of[nm])
        return offs

    first = _rows_of(shards["in"].astype(BF16), D, rows_of["in"])
    gather_src = [jnp.concatenate([_rows_of(shards[nm].astype(BF16), D, rows_of[nm]) for nm in names], axis=0)
                  for names in fwd_groups]
    gathered_in = _all_gather_weights(first)

    w_in_full = _cols_from_src(gathered_in[:, :layout["in"][1], :], D, in_loc)
    wqkv = w_in_full[:, :6 * W]
    wf = jnp.pad(w_in_full[:, 6 * W:6 * W + n_heads], ((0, 0), (0, 128 - n_heads)))
    wgs = w_in_full[:, 6 * W + n_heads:6 * W + n_heads + D]
    wgf = w_in_full[:, 6 * W + n_heads + D:]
    bf_pad = jnp.pad(b_forget, ((0, 0), (0, 128 - n_heads)))

    def later_weights(*zones):
        part = {}
        for names, zone, src in zip(fwd_groups, zones, gather_src):
            gathered, offs = _own_slot(zone, src), offsets(names)
            for nm in names:
                part[nm] = gathered[:, offs[nm]:offs[nm] + rows_of[nm], :]
        return (_cols_from_src(part["sb"], W, D // N_DEV), _cols_from_src(part["fx"], W, D // N_DEV),
                part["o"].reshape(D, D), _cols_from_src(part["fg"], D, ff // N_DEV),
                _cols_from_src(part["fu"], D, ff // N_DEV), part["fd"].reshape(ff, D))

    def pack_early(g):
        dest = {"fg": _cols_to_dest(g["fg"], D), "fu": _cols_to_dest(g["fu"], D), "sb": _cols_to_dest(g["sb"], D),
                "fx": _cols_to_dest(g["fx"], D), "o": g["o"].reshape(N_DEV, D // N_DEV, D),
                "fd": g["fd"].reshape(N_DEV, ff // N_DEV, D)}
        return [(jnp.concatenate([_pad_rows16(dest[nm].astype(BF16)) for nm in names], axis=1), True)
                for names in bwd_groups]

    def pack_last(g):
        g_in = jnp.concatenate(g["qkv"] + [g["f"][:, :n_heads], g["gs"], g["gf"]], axis=1)
        return _pad_rows16(_cols_to_dest(g_in, D).astype(BF16)), True

    gx, wg, st, early, ((pack_in, _), land_in) = _local_step(
        x[0], loss_target[0], ada8, lnp8, b_gate[:, :D], b_gate[:, D:], bf_pad, wqkv, wf, wgs, wgf,
        [(src, False) for src in gather_src], later_weights, pack_early, pack_last)

    def summed(zone, sent, name):
        own = lax.dynamic_index_in_dim(sent, me, axis=0, keepdims=False)
        rows = zone.shape[1]
        block = max(t for t in range(16, 705, 16) if rows % t == 0)
        return _sum_slots(_own_slot(zone, own), name, block)

    gsum = {"in": summed(land_in, pack_in, "sum_grads_in")[:rows_of["in"]]}
    for n, (names, ((sent, _), zone)) in enumerate(zip(bwd_groups, early)):
        total, offs = summed(zone, sent, "sum_grads_%d" % n), offsets(names)
        for nm in names:
            gsum[nm] = total[offs[nm]:offs[nm] + rows_of[nm]]

    def gshard(nm, shape):
        return gsum[nm].reshape(shape)

    zrow = jnp.zeros((1, D), F32)
    gb_f_row = jnp.pad(st["gb_f"][0:1], ((0, 0), (0, D - 128)))
    stats16 = jnp.concatenate([
        st["first"][1:2], st["first"][0:1], st["mid"][4:5], st["mid"][1:2], st["mid"][0:1], st["loss"][3:4],
        st["mid"][2:3], st["mid"][3:4], st["loss"][1:2], st["loss"][2:3], st["gb_sb"][0:1], st["gb_fx"][0:1],
        st["loss"][0:1], gb_f_row, zrow, zrow], axis=0)
    st_all = _all_gather_small(stats16, "gather_stats")
    st_sum, loss_blk = _sum_stats(st_all, 12)
    loss = loss_blk[0, 0]

    d_ada_all = st_all[:, :n_cond, :].reshape(N_DEV, n_cond * D)
    d_cols = lax.dynamic_slice(d_ada_all, (0, me * ada_loc), (N_DEV, ada_loc))
    d16 = jnp.pad(d_cols, ((0, 16 - N_DEV), (0, 0)))
    g_w_ada = _mm([(c16, d16)], 'tn', F32, "ada_wgrad", silu_a=True)

    small_w = jnp.concatenate([b_ada.reshape(n_cond, D), ln1_g, ln1_b, ln2_g, ln2_b, b_gate.reshape(2, D), zrow,
                               jnp.pad(b_forget, ((0, 0), (0, D - n_heads))), zrow, zrow], axis=0)
    small_m = jnp.concatenate([m_b_ada.reshape(n_cond, D), m_ln1_g, m_ln1_b, m_ln2_g, m_ln2_b, m_b_gate.reshape(2, D),
                               zrow, jnp.pad(m_b_forget, ((0, 0), (0, D - n_heads))), zrow, zrow], axis=0)
    small_v = jnp.concatenate([v_b_ada.reshape(n_cond, D), v_ln1_g, v_ln1_b, v_ln2_g, v_ln2_b, v_b_gate.reshape(2, D),
                               zrow, jnp.pad(v_b_forget, ((0, 0), (0, D - n_heads))), zrow, zrow], axis=0)
    sm = _adamw(small_w, st_sum, small_m, small_v, "adamw_small")

    def small(a, nm):
        if nm == "b_ada":
            return a[0:n_cond].reshape(1, n_cond * D)
        if nm == "b_gate":
            return a[10:12].reshape(1, 2 * D)
        if nm == "b_forget":
            return a[13:14, :n_heads]
        row = {"ln1_g": 6, "ln1_b": 7, "ln2_g": 8, "ln2_b": 9}[nm]
        return a[row:row + 1]

    big = {
        "w_ada": (w_ada[0], g_w_ada, m_w_ada[0], v_w_ada[0]),
        "w_in": (w_in[0], gshard("in", w_in.shape[1:]), m_w_in[0], v_w_in[0]),
        "w_sb_out": (w_sb_out[0], gshard("sb", w_sb_out.shape[1:]), m_w_sb_out[0], v_w_sb_out[0]),
        "w_fox_out": (w_fox_out[0], gshard("fx", w_fox_out.shape[1:]), m_w_fox_out[0], v_w_fox_out[0]),
        "w_o": (w_o[0], gshard("o", w_o.shape[1:]), m_w_o[0], v_w_o[0]),
        "w_ffn_gate": (w_ffn_gate[0], gshard("fg", w_ffn_gate.shape[1:]), m_w_ffn_gate[0], v_w_ffn_gate[0]),
        "w_ffn_up": (w_ffn_up[0], gshard("fu", w_ffn_up.shape[1:]), m_w_ffn_up[0], v_w_ffn_up[0]),
        "w_ffn_down": (w_ffn_down[0], gshard("fd", w_ffn_down.shape[1:]), m_w_ffn_down[0], v_w_ffn_down[0]),
    }
    order = ["w_ada", "b_ada", "w_in", "b_gate", "b_forget", "w_sb_out", "w_fox_out", "w_o", "ln1_g", "ln1_b",
             "w_ffn_gate", "w_ffn_up", "w_ffn_down", "ln2_g", "ln2_b"]
    grads, deltas, new_ms, new_vs = [], [], [], []
    for nm in order:
        if nm in big:
            w, g, m, v = big[nm]
            d, nm_, nv_ = _adamw(w, g, m, v, "adamw_" + nm)
            grads.append(g[None])
            deltas.append(d[None])
            new_ms.append(nm_[None])
            new_vs.append(nv_[None])
        else:
            grads.append(small(st_sum, nm))
            deltas.append(small(sm[0], nm))
            new_ms.append(small(sm[1], nm))
            new_vs.append(small(sm[2], nm))
    return (loss, gx[None], *grads, *deltas, *new_ms, *new_vs)
```

```python
import functools

import jax
import jax.numpy as jnp
import numpy as np
from jax import lax
from jax.experimental import pallas as pl
from jax.experimental.pallas import tpu as pltpu

F32 = jnp.float32
BF16 = jnp.bfloat16

HEAD_DIM = 64
PAIR = 2 * HEAD_DIM
LN_EPS = 1e-5
ALPHA = 2.0 ** 0.25
ADAM_LR, ADAM_B1, ADAM_B2, ADAM_EPS, ADAM_WD, ADAM_STEP = 0.001, 0.9, 0.999, 1e-08, 0.01, 10
N_DEV = 8
VMEM_LIMIT = 56 * 1024 * 1024
MESH = pl.DeviceIdType.MESH


def _dot(a, b, ca=1, cb=0):
    return lax.dot_general(a, b, (((ca,), (cb,)), ((), ())), preferred_element_type=F32)


def _pick(n, cands):
    for t in cands:
        if n % t == 0:
            return t
    return n


def _params(sem):
    return pltpu.CompilerParams(dimension_semantics=sem, vmem_limit_bytes=VMEM_LIMIT)


MM_BLOCK_BYTES = 40 * 1024 * 1024
LANES = 128


def _divisors(n, cap):
    ds = [d for d in range(LANES, min(n, cap) + 1, LANES) if n % d == 0]
    return sorted(ds, reverse=True) or [n]


def _mm_tiles(M, N, a_row_bytes, b_row_bytes, out_itemsize):
    best = None
    for tm in _divisors(M, 1024):
        for tn in _divisors(N, 2048):
            need = 2 * (tm * a_row_bytes + tn * b_row_bytes + tm * tn * out_itemsize) + tm * tn * 4
            if need <= MM_BLOCK_BYTES and (best is None or (tm * tn, tm) > (best[0] * best[1], best[0])):
                best = (tm, tn)
    assert best is not None, (M, N, a_row_bytes, b_row_bytes)
    return best


def _mm(pairs, mode, out_dtype, name, bias=None, act=None, silu_a=False, exchange=None):
    norm = []
    for p in pairs:
        a, b = p[0], p[1]
        kdim_a = a.shape[0] if mode == 'tn' else a.shape[1]
        K, ka, kb = (p[2], p[3], p[4]) if len(p) > 2 else (kdim_a, 0, 0)
        norm.append((a, b, K, ka, kb))
    a0, b0 = norm[0][0], norm[0][1]
    M = a0.shape[1] if mode == 'tn' else a0.shape[0]
    N = b0.shape[0] if mode == 'nt' else b0.shape[1]
    tm, tn = _mm_tiles(M, N, sum(K * a.dtype.itemsize for a, _, K, _, _ in norm),
                       sum(K * b.dtype.itemsize for _, b, K, _, _ in norm), jnp.dtype(out_dtype).itemsize)
    n_pairs = len(norm)

    in_specs, args = [], []
    for a, b, K, ka, kb in norm:
        if mode == 'tn':
            in_specs.append(pl.BlockSpec((K, tm), lambda i, j, ka=ka: (ka, i)))
        else:
            in_specs.append(pl.BlockSpec((tm, K), lambda i, j, ka=ka: (i, ka)))
        if mode == 'nt':
            in_specs.append(pl.BlockSpec((tn, K), lambda i, j, kb=kb: (j, kb)))
        else:
            in_specs.append(pl.BlockSpec((K, tn), lambda i, j, kb=kb: (kb, j)))
        args += [a, b]
    if bias is not None:
        in_specs.append(pl.BlockSpec((1, tn), lambda i, j: (0, j)))
        args.append(bias)

    ca = 0 if mode == 'tn' else 1
    cb = 1 if mode == 'nt' else 0

    def body(*refs):
        o_ref = refs[-1]
        acc = None
        for p in range(n_pairs):
            av = refs[2 * p][...]
            if silu_a:
                av = av / (1.0 + jnp.exp(-av))
            d = _dot(av.astype(BF16), refs[2 * p + 1][...].astype(BF16), ca, cb)
            acc = d if acc is None else acc + d
        if bias is not None:
            acc = acc + refs[2 * n_pairs][...]
        if act == 'sigmoid':
            acc = 1.0 / (1.0 + jnp.exp(-acc))
        o_ref[...] = acc.astype(out_dtype)

    out_spec = pl.BlockSpec((tm, tn), lambda i, j: (i, j))
    out_shape = jax.ShapeDtypeStruct((M, N), out_dtype)
    if exchange is not None:
        return _call_carrying(body, exchange, name, (M // tm, N // tn), in_specs, [out_spec], [out_shape], [], args)
    return pl.pallas_call(
        body, name=name, grid=(M // tm, N // tn), in_specs=in_specs, out_specs=out_spec, out_shape=out_shape,
        compiler_params=_params(("parallel", "parallel")),
    )(*args)


def _rows_call(body, name, row_ins, vec_ins, row_outs, acc_outs, ts):
    S = row_ins[0].shape[0]
    in_specs = [pl.BlockSpec((ts, a.shape[1]), lambda i: (i, 0)) for a in row_ins]
    in_specs += [pl.BlockSpec(a.shape, lambda i: (0, 0)) for a in vec_ins]
    out_specs = [pl.BlockSpec((ts, c), lambda i: (i, 0)) for c, _ in row_outs]
    out_specs += [pl.BlockSpec(s, lambda i: (0, 0)) for s in acc_outs]
    out_shape = [jax.ShapeDtypeStruct((S, c), dt) for c, dt in row_outs]
    out_shape += [jax.ShapeDtypeStruct(s, F32) for s in acc_outs]
    return pl.pallas_call(
        body, name=name, grid=(S // ts,), in_specs=in_specs, out_specs=out_specs, out_shape=out_shape,
        compiler_params=_params(("arbitrary",)),
    )(*row_ins, *vec_ins)


def _ln_stats(v):
    mu = jnp.mean(v, axis=-1, keepdims=True)
    d = v - mu
    var = jnp.mean(d * d, axis=-1, keepdims=True)
    rstd = lax.rsqrt(var + LN_EPS)
    return d * rstd, rstd


def _ln_bwd(dxhat, xhat, rstd):
    m1 = jnp.mean(dxhat, axis=-1, keepdims=True)
    m2 = jnp.mean(dxhat * xhat, axis=-1, keepdims=True)
    return rstd * (dxhat - m1 - xhat * m2)


def _colsum(v):
    return jnp.sum(v, axis=0, keepdims=True)


def _ln_mod(x, ada8, ts):
    D = x.shape[1]

    def body(x_ref, v_ref, u_ref):
        xhat, _ = _ln_stats(x_ref[...])
        u_ref[...] = (xhat * (1.0 + v_ref[1:2, :]) + v_ref[0:1, :]).astype(BF16)

    return _rows_call(body, "ln_mod", [x], [ada8], [(D, BF16)], [], ts)[0]


def _gate_mix(g_sb, g_fx, y_sb, y_fx, ts):
    D = y_sb.shape[1]

    def body(gs, gf, ys, yf, o_ref):
        o_ref[...] = (gs[...] * ys[...] + gf[...] * yf[...]).astype(BF16)

    return _rows_call(body, "gate_mix", [g_sb, g_fx, y_sb, y_fx], [], [(D, BF16)], [], ts)[0]


def _post_attn(x, mix, ada8, lnp8, ts):
    D = x.shape[1]

    def body(x_ref, mix_ref, v_ref, p_ref, x1_ref, u2_ref):
        r1 = ALPHA * x_ref[...] + v_ref[2:3, :] * mix_ref[...]
        xhat, _ = _ln_stats(r1)
        x1 = xhat * p_ref[0:1, :] + p_ref[1:2, :]
        x1_ref[...] = x1
        xh1, _ = _ln_stats(x1)
        u2_ref[...] = (xh1 * (1.0 + v_ref[4:5, :]) + v_ref[3:4, :]).astype(BF16)

    return _rows_call(body, "post_attn", [x, mix], [ada8, lnp8], [(D, F32), (D, BF16)], [], ts)


def _loss_head(x1, h, target, ada8, lnp8, ts):
    D = x1.shape[1]

    def body(x1_ref, h_ref, t_ref, v_ref, p_ref, dr2_ref, dh_ref, st_ref):
        @pl.when(pl.program_id(0) == 0)
        def _():
            st_ref[...] = jnp.zeros_like(st_ref)

        hv = h_ref[...]
        g2 = v_ref[5:6, :]
        r2 = ALPHA * x1_ref[...] + g2 * hv
        xhat, rstd = _ln_stats(r2)
        y = xhat * p_ref[2:3, :] + p_ref[3:4, :]
        err = y - t_ref[...]
        dy = err * (1.0 / D)
        dr2 = _ln_bwd(dy * p_ref[2:3, :], xhat, rstd)
        dr2_ref[...] = dr2
        dh_ref[...] = (dr2 * g2).astype(BF16)
        st_ref[0:1, :] += _colsum(err * err) * (0.5 / D)
        st_ref[1:2, :] += _colsum(dy * xhat)
        st_ref[2:3, :] += _colsum(dy)
        st_ref[3:4, :] += _colsum(dr2 * hv)

    return _rows_call(body, "loss_head", [x1, h, target], [ada8, lnp8], [(D, F32), (D, BF16)], [(8, D)], ts)


def _mid_bwd(du2, x1, dr2, mix, x, ada8, lnp8, ts):
    D = x.shape[1]

    def body(du2_ref, x1_ref, dr2_ref, mix_ref, x_ref, v_ref, p_ref, dr1_ref, dmix_ref, st_ref):
        @pl.when(pl.program_id(0) == 0)
        def _():
            st_ref[...] = jnp.zeros_like(st_ref)

        du2v = du2_ref[...]
        xh1, rstd1 = _ln_stats(x1_ref[...])
        dx1 = ALPHA * dr2_ref[...] + _ln_bwd(du2v * (1.0 + v_ref[4:5, :]), xh1, rstd1)
        mixv = mix_ref[...]
        g1 = v_ref[2:3, :]
        r1 = ALPHA * x_ref[...] + g1 * mixv
        xhr, rstdr = _ln_stats(r1)
        dr1 = _ln_bwd(dx1 * p_ref[0:1, :], xhr, rstdr)
        dr1_ref[...] = dr1
        dmix_ref[...] = (dr1 * g1).astype(BF16)
        st_ref[0:1, :] += _colsum(du2v * xh1)
        st_ref[1:2, :] += _colsum(du2v)
        st_ref[2:3, :] += _colsum(dx1 * xhr)
        st_ref[3:4, :] += _colsum(dx1)
        st_ref[4:5, :] += _colsum(dr1 * mixv)

    return _rows_call(body, "mid_bwd", [du2, x1, dr2, mix, x], [ada8, lnp8], [(D, F32), (D, BF16)], [(8, D)], ts)


def _first_bwd(du1, x, dr1, ada8, ts):
    D = x.shape[1]

    def body(du1_ref, x_ref, dr1_ref, v_ref, gx_ref, st_ref):
        @pl.when(pl.program_id(0) == 0)
        def _():
            st_ref[...] = jnp.zeros_like(st_ref)

        du1v = du1_ref[...]
        xh0, rstd0 = _ln_stats(x_ref[...])
        gx_ref[...] = ALPHA * dr1_ref[...] + _ln_bwd(du1v * (1.0 + v_ref[1:2, :]), xh0, rstd0)
        st_ref[0:1, :] += _colsum(du1v * xh0)
        st_ref[1:2, :] += _colsum(du1v)

    return _rows_call(body, "first_bwd", [du1, x, dr1], [ada8], [(D, F32)], [(8, D)], ts)


def _split3(v):
    hi = v.astype(BF16)
    r = v - hi.astype(F32)
    mid = r.astype(BF16)
    lo = (r - mid.astype(F32)).astype(BF16)
    return hi, mid, lo


def _fgate_fwd(f, bf_pad, tb):
    S = f.shape[0]

    def body(f_ref, b_ref, fc_ref, carry):
        @pl.when(pl.program_id(0) == 0)
        def _():
            carry[...] = jnp.zeros_like(carry)

        z = f_ref[...] + b_ref[...]
        ls = jnp.minimum(z, 0.0) - jnp.log(1.0 + jnp.exp(-jnp.abs(z)))
        r = lax.broadcasted_iota(jnp.int32, (tb, tb), 0)
        c = lax.broadcasted_iota(jnp.int32, (tb, tb), 1)
        tri = (c <= r).astype(BF16)
        hi, mid, lo = _split3(ls)
        cs = _dot(tri, hi) + _dot(tri, mid) + _dot(tri, lo) + carry[...]
        fc_ref[...] = cs
        carry[...] = cs[tb - 1:tb, :]

    return pl.pallas_call(
        body, name="fgate_fwd", grid=(S // tb,),
        in_specs=[pl.BlockSpec((tb, 128), lambda i: (i, 0)), pl.BlockSpec((1, 128), lambda i: (0, 0))],
        out_specs=pl.BlockSpec((tb, 128), lambda i: (i, 0)),
        out_shape=jax.ShapeDtypeStruct((S, 128), F32),
        scratch_shapes=[pltpu.VMEM((1, 128), F32)],
        compiler_params=_params(("arbitrary",)),
    )(f, bf_pad)


def _fgate_bwd(dfc, f, bf_pad, tb):
    S = f.shape[0]
    nb = S // tb

    def body(d_ref, f_ref, b_ref, df_ref, gb_ref, carry):
        @pl.when(pl.program_id(0) == 0)
        def _():
            carry[...] = jnp.zeros_like(carry)
            gb_ref[...] = jnp.zeros_like(gb_ref)

        r = lax.broadcasted_iota(jnp.int32, (tb, tb), 0)
        c = lax.broadcasted_iota(jnp.int32, (tb, tb), 1)
        tri = (c >= r).astype(BF16)
        hi, mid, lo = _split3(d_ref[...])
        rs = _dot(tri, hi) + _dot(tri, mid) + _dot(tri, lo) + carry[...]
        carry[...] = rs[0:1, :]
        z = f_ref[...] + b_ref[...]
        df = rs * (1.0 / (1.0 + jnp.exp(z)))
        df_ref[...] = df
        gb_ref[0:1, :] += _colsum(df)

    return pl.pallas_call(
        body, name="fgate_bwd", grid=(nb,),
        in_specs=[pl.BlockSpec((tb, 128), lambda i: (nb - 1 - i, 0)),
                  pl.BlockSpec((tb, 128), lambda i: (nb - 1 - i, 0)),
                  pl.BlockSpec((1, 128), lambda i: (0, 0))],
        out_specs=[pl.BlockSpec((tb, 128), lambda i: (nb - 1 - i, 0)), pl.BlockSpec((8, 128), lambda i: (0, 0))],
        out_shape=[jax.ShapeDtypeStruct((S, 128), F32), jax.ShapeDtypeStruct((8, 128), F32)],
        scratch_shapes=[pltpu.VMEM((1, 128), F32)],
        compiler_params=_params(("arbitrary",)),
    )(dfc, f, bf_pad)


def _split2(v):
    hi = v.astype(BF16)
    lo = (v - hi.astype(F32)).astype(BF16)
    return hi, lo


def _head_masks():
    lane = lax.broadcasted_iota(jnp.int32, (1, PAIR), 1)
    m0 = lane < HEAD_DIM
    return m0, jnp.logical_not(m0)


def _sel(mask, v):
    return jnp.where(mask, v, jnp.zeros_like(v))


def _softplus(z):
    return jnp.maximum(z, 0.0) + jnp.log(1.0 + jnp.exp(-jnp.abs(z)))


def _qkv_specs(S, tq, n_pairs, base):
    return [pl.BlockSpec((tq, PAIR), lambda p, i: (i, base + p)),
            pl.BlockSpec((S, PAIR), lambda p, i: (0, base + n_pairs + p)),
            pl.BlockSpec((S, PAIR), lambda p, i: (0, base + 2 * n_pairs + p))]


def _sb_fwd(qkv, n_pairs, base, tq):
    S = qkv.shape[0]
    tk = tq
    scale = HEAD_DIM ** -0.5

    def body(q_ref, k_ref, v_ref, o_ref, t_ref, acc_ref):
        i = pl.program_id(1)
        masks = _head_masks()
        q2 = q_ref[...]
        qm = [_sel(m, q2) for m in masks]
        rowpos = i * tq + lax.broadcasted_iota(jnp.int32, (tq, tk), 0)
        colin = lax.broadcasted_iota(jnp.int32, (tq, tk), 1)
        upper = (lax.broadcasted_iota(jnp.int32, (tk, tk), 0) > lax.broadcasted_iota(jnp.int32, (tk, tk), 1)).astype(BF16)
        acc_ref[...] = jnp.zeros_like(acc_ref)

        def step(jj, carry):
            j = i - jj
            off = pl.multiple_of(j * tk, tk)
            k2 = k_ref[pl.ds(off, tk), :]
            v2 = v_ref[pl.ds(off, tk), :]
            mask = (j * tk + colin) < rowpos
            out = None
            new = []
            for h in heads:
                z = _dot(qm[h], k2, 1, 1) * scale
                sp = _softplus(z)
                lg = jnp.where(mask, -sp, 0.0)
                hi, lo = _split2(lg)
                suf = _dot(hi, upper) + _dot(lo, upper)
                a = jnp.where(mask, jnp.exp(z - sp + suf + carry[h]), 0.0)
                d = _dot(a.astype(BF16), _sel(masks[h], v2))
                out = d if out is None else out + d
                new.append(carry[h] + jnp.sum(lg, axis=-1, keepdims=True))
            acc_ref[...] += out
            return tuple(new)

        zero = jnp.zeros((tq, 1), F32)
        r0, r1 = lax.fori_loop(0, i + 1, step, (zero, zero))
        o_ref[...] = acc_ref[...].astype(BF16)
        t_ref[...] = jnp.where(masks[0], r0, r1)

    W = n_pairs * PAIR
    return pl.pallas_call(
        body, name="sb_fwd", grid=(n_pairs, S // tq), in_specs=_qkv_specs(S, tq, n_pairs, base),
        out_specs=[pl.BlockSpec((tq, PAIR), lambda p, i: (i, p)), pl.BlockSpec((tq, PAIR), lambda p, i: (i, p))],
        out_shape=[jax.ShapeDtypeStruct((S, W), BF16), jax.ShapeDtypeStruct((S, W), F32)],
        scratch_shapes=[pltpu.VMEM((tq, PAIR), F32)],
        compiler_params=_params(("parallel", "arbitrary")),
    )(qkv, qkv, qkv)


def _sb_bwd(qkv, tot, do, n_pairs, base, tq):
    S = qkv.shape[0]
    tk = tq
    nq = S // tq
    scale = HEAD_DIM ** -0.5

    def body(q_ref, k_ref, v_ref, t_ref, do_ref, dq_ref, dk_ref, dv_ref, dq_acc, dk_acc, dv_acc):
        i = pl.program_id(1)
        masks = _head_masks()

        @pl.when(i == 0)
        def _():
            dk_acc[...] = jnp.zeros_like(dk_acc)
            dv_acc[...] = jnp.zeros_like(dv_acc)

        q2 = q_ref[...]
        do2 = do_ref[...]
        qm = [_sel(m, q2) for m in masks]
        dom = [_sel(m, do2) for m in masks]
        t2 = t_ref[...]
        tot_h = [t2[:, 0:1], t2[:, HEAD_DIM:HEAD_DIM + 1]]
        rowpos = i * tq + lax.broadcasted_iota(jnp.int32, (tq, tk), 0)
        colin = lax.broadcasted_iota(jnp.int32, (tq, tk), 1)
        r_i = lax.broadcasted_iota(jnp.int32, (tk, tk), 0)
        c_i = lax.broadcasted_iota(jnp.int32, (tk, tk), 1)
        upper = (r_i > c_i).astype(BF16)
        lower = (r_i < c_i).astype(BF16)
        dq_acc[...] = jnp.zeros_like(dq_acc)

        def step(j, carry):
            off = pl.multiple_of(j * tk, tk)
            k2 = k_ref[pl.ds(off, tk), :]
            v2 = v_ref[pl.ds(off, tk), :]
            mask = (j * tk + colin) < rowpos
            dq = None
            dk = None
            dv = None
            new = []
            for h in heads:
                cum_l, cum_g = carry[2 * h], carry[2 * h + 1]
                z = _dot(qm[h], k2, 1, 1) * scale
                sp = _softplus(z)
                lg = jnp.where(mask, -sp, 0.0)
                hi, lo = _split2(lg)
                suf = _dot(hi, upper) + _dot(lo, upper)
                row_l = jnp.sum(lg, axis=-1, keepdims=True)
                later = tot_h[h] - cum_l - row_l
                a = jnp.where(mask, jnp.exp(z - sp + suf + later), 0.0)
                da = _dot(dom[h], v2, 1, 1)
                g = da * a
                ghi, glo = _split2(g)
                pre = _dot(ghi, lower) + _dot(glo, lower) + cum_g
                one_m_beta = jnp.exp(-sp)
                dz = jnp.where(mask, g * one_m_beta - (1.0 - one_m_beta) * pre, 0.0)
                dzb = (dz * scale).astype(BF16)
                d1 = _dot(dzb, _sel(masks[h], k2))
                d2 = _dot(dzb, qm[h], 0, 0)
                d3 = _dot(a.astype(BF16), dom[h], 0, 0)
                dq = d1 if dq is None else dq + d1
                dk = d2 if dk is None else dk + d2
                dv = d3 if dv is None else dv + d3
                new += [cum_l + row_l, cum_g + jnp.sum(g, axis=-1, keepdims=True)]
            dq_acc[...] += dq
            dk_acc[pl.ds(off, tk), :] += dk
            dv_acc[pl.ds(off, tk), :] += dv
            return tuple(new)

        zero = jnp.zeros((tq, 1), F32)
        lax.fori_loop(0, i + 1, step, (zero, zero, zero, zero))
        dq_ref[...] = dq_acc[...].astype(BF16)

        @pl.when(i == nq - 1)
        def _():
            dk_ref[...] = dk_acc[...].astype(BF16)
            dv_ref[...] = dv_acc[...].astype(BF16)

    W = n_pairs * PAIR
    in_specs = _qkv_specs(S, tq, n_pairs, base) + [
        pl.BlockSpec((tq, PAIR), lambda p, i: (i, p)),
        pl.BlockSpec((tq, PAIR), lambda p, i: (i, p))]
    out_specs = [pl.BlockSpec((tq, PAIR), lambda p, i: (i, p)),
                 pl.BlockSpec((S, PAIR), lambda p, i: (0, p)),
                 pl.BlockSpec((S, PAIR), lambda p, i: (0, p))]
    dq, dk, dv = pl.pallas_call(
        body, name="sb_bwd", grid=(n_pairs, nq), in_specs=in_specs, out_specs=out_specs,
        out_shape=[jax.ShapeDtypeStruct((S, W), BF16)] * 3,
        scratch_shapes=[pltpu.VMEM((tq, PAIR), F32), pltpu.VMEM((S, PAIR), F32), pltpu.VMEM((S, PAIR), F32)],
        compiler_params=_params(("parallel", "arbitrary")),
    )(qkv, qkv, qkv, tot, do)
    return dq, dk, dv


NEG = -1e30


def _fox_specs(S, tq, n_pairs, base):
    return _qkv_specs(S, tq, n_pairs, base) + [
        pl.BlockSpec((tq, PAIR), lambda p, i: (i, p)),
        pl.BlockSpec((1, S // tq, 8, tq), lambda p, i: (p, 0, 0, 0))]


def _fox_fwd(qkv, fcx, fcr, n_pairs, base, tq):
    S = qkv.shape[0]
    tk = tq
    scale = HEAD_DIM ** -0.5

    def body(q_ref, k_ref, v_ref, fq_ref, fk_ref, o_ref, lse_ref, acc_ref):
        i = pl.program_id(1)
        masks = _head_masks()
        q2 = q_ref[...]
        qm = [_sel(m, q2) for m in masks]
        fq2 = fq_ref[...]
        fq = [fq2[:, 0:1], fq2[:, HEAD_DIM:HEAD_DIM + 1]]
        rowpos = i * tq + lax.broadcasted_iota(jnp.int32, (tq, tk), 0)
        colin = lax.broadcasted_iota(jnp.int32, (tq, tk), 1)
        acc_ref[...] = jnp.zeros_like(acc_ref)

        def step(j, carry):
            off = pl.multiple_of(j * tk, tk)
            k2 = k_ref[pl.ds(off, tk), :]
            v2 = v_ref[pl.ds(off, tk), :]
            fk2 = fk_ref[0, j]
            mask = (j * tk + colin) <= rowpos
            out = None
            new = []
            alphas = []
            for h in heads:
                m_old, l_old = carry[2 * h], carry[2 * h + 1]
                s = _dot(qm[h], k2, 1, 1) * scale + fq[h] - fk2[h:h + 1, :]
                s = jnp.where(mask, s, NEG)
                m_new = jnp.maximum(m_old, jnp.max(s, axis=-1, keepdims=True))
                p = jnp.exp(s - m_new)
                alpha = jnp.exp(m_old - m_new)
                alphas.append(alpha)
                d = _dot(p.astype(BF16), _sel(masks[h], v2))
                out = d if out is None else out + d
                new += [m_new, alpha * l_old + jnp.sum(p, axis=-1, keepdims=True)]
            acc_ref[...] = acc_ref[...] * jnp.where(masks[0], alphas[0], alphas[1]) + out
            return tuple(new)

        zero = jnp.zeros((tq, 1), F32)
        neg = jnp.full((tq, 1), NEG, F32)
        m0, l0, m1, l1 = lax.fori_loop(0, i + 1, step, (neg, zero, neg, zero))
        o_ref[...] = (acc_ref[...] / jnp.where(masks[0], l0, l1)).astype(BF16)
        lse_ref[...] = jnp.where(masks[0], m0 + jnp.log(l0), m1 + jnp.log(l1))

    W = n_pairs * PAIR
    return pl.pallas_call(
        body, name="fox_fwd", grid=(n_pairs, S // tq), in_specs=_fox_specs(S, tq, n_pairs, base),
        out_specs=[pl.BlockSpec((tq, PAIR), lambda p, i: (i, p)), pl.BlockSpec((tq, PAIR), lambda p, i: (i, p))],
        out_shape=[jax.ShapeDtypeStruct((S, W), BF16), jax.ShapeDtypeStruct((S, W), F32)],
        scratch_shapes=[pltpu.VMEM((tq, PAIR), F32)],
        compiler_params=_params(("parallel", "arbitrary")),
    )(qkv, qkv, qkv, fcx, fcr)


def _fox_bwd(qkv, fcx, fcr, o, lse, do, n_pairs, base, tq):
    S = qkv.shape[0]
    tk = tq
    nq = S // tq
    scale = HEAD_DIM ** -0.5

    def body(q_ref, k_ref, v_ref, fq_ref, fk_ref, o_ref, lse_ref, do_ref,
             dq_ref, dk_ref, dv_ref, dfq_ref, dfk_ref, dq_acc, dk_acc, dv_acc):
        i = pl.program_id(1)
        masks = _head_masks()

        @pl.when(i == 0)
        def _():
            dk_acc[...] = jnp.zeros_like(dk_acc)
            dv_acc[...] = jnp.zeros_like(dv_acc)
            dfk_ref[...] = jnp.zeros_like(dfk_ref)

        q2 = q_ref[...]
        do2 = do_ref[...]
        qm = [_sel(m, q2) for m in masks]
        dom = [_sel(m, do2) for m in masks]
        fq2 = fq_ref[...]
        fq = [fq2[:, 0:1], fq2[:, HEAD_DIM:HEAD_DIM + 1]]
        l2 = lse_ref[...]
        lse_h = [l2[:, 0:1], l2[:, HEAD_DIM:HEAD_DIM + 1]]
        prod = do2.astype(F32) * o_ref[...].astype(F32)
        delta = [jnp.sum(jnp.where(m, prod, 0.0), axis=-1, keepdims=True) for m in masks]
        rowpos = i * tq + lax.broadcasted_iota(jnp.int32, (tq, tk), 0)
        colin = lax.broadcasted_iota(jnp.int32, (tq, tk), 1)
        dq_acc[...] = jnp.zeros_like(dq_acc)

        def step(j, carry):
            off = pl.multiple_of(j * tk, tk)
            k2 = k_ref[pl.ds(off, tk), :]
            v2 = v_ref[pl.ds(off, tk), :]
            fk2 = fk_ref[0, j]
            mask = (j * tk + colin) <= rowpos
            dq = None
            dk = None
            dv = None
            new = []
            dfk_rows = []
            for h in heads:
                s = _dot(qm[h], k2, 1, 1) * scale + fq[h] - fk2[h:h + 1, :]
                p = jnp.where(mask, jnp.exp(s - lse_h[h]), 0.0)
                dp = _dot(dom[h], v2, 1, 1)
                ds = p * (dp - delta[h])
                dsb = (ds * scale).astype(BF16)
                d1 = _dot(dsb, _sel(masks[h], k2))
                d2 = _dot(dsb, qm[h], 0, 0)
                d3 = _dot(p.astype(BF16), dom[h], 0, 0)
                dq = d1 if dq is None else dq + d1
                dk = d2 if dk is None else dk + d2
                dv = d3 if dv is None else dv + d3
                new.append(carry[h] + jnp.sum(ds, axis=-1, keepdims=True))
                dfk_rows.append(jnp.sum(ds, axis=0, keepdims=True))
            dq_acc[...] += dq
            dk_acc[pl.ds(off, tk), :] += dk
            dv_acc[pl.ds(off, tk), :] += dv
            dfk_ref[0, j, 0:1, :] += dfk_rows[0]
            dfk_ref[0, j, 1:2, :] += dfk_rows[1]
            return tuple(new)

        zero = jnp.zeros((tq, 1), F32)
        r0, r1 = lax.fori_loop(0, i + 1, step, (zero, zero))
        dq_ref[...] = dq_acc[...].astype(BF16)
        dfq_ref[...] = jnp.where(masks[0], r0, r1)

        @pl.when(i == nq - 1)
        def _():
            dk_ref[...] = dk_acc[...].astype(BF16)
            dv_ref[...] = dv_acc[...].astype(BF16)

    W = n_pairs * PAIR
    in_specs = _fox_specs(S, tq, n_pairs, base) + [
        pl.BlockSpec((tq, PAIR), lambda p, i: (i, p)),
        pl.BlockSpec((tq, PAIR), lambda p, i: (i, p)),
        pl.BlockSpec((tq, PAIR), lambda p, i: (i, p))]
    out_specs = [pl.BlockSpec((tq, PAIR), lambda p, i: (i, p)),
                 pl.BlockSpec((S, PAIR), lambda p, i: (0, p)),
                 pl.BlockSpec((S, PAIR), lambda p, i: (0, p)),
                 pl.BlockSpec((tq, PAIR), lambda p, i: (i, p)),
                 pl.BlockSpec((1, nq, 8, tk), lambda p, i: (p, 0, 0, 0))]
    return pl.pallas_call(
        body, name="fox_bwd", grid=(n_pairs, nq), in_specs=in_specs, out_specs=out_specs,
        out_shape=[jax.ShapeDtypeStruct((S, W), BF16)] * 3
        + [jax.ShapeDtypeStruct((S, W), F32), jax.ShapeDtypeStruct((n_pairs, nq, 8, tk), F32)],
        scratch_shapes=[pltpu.VMEM((tq, PAIR), F32), pltpu.VMEM((S, PAIR), F32), pltpu.VMEM((S, PAIR), F32)],
        compiler_params=_params(("parallel", "arbitrary")),
    )(qkv, qkv, qkv, fcx, fcr, o, lse, do)


RC = 32
VANISH = -104.0


def _chunks(n_rows, fn):
    for ci in range(n_rows // RC):
        fn(ci * RC)


def _wide(v, tk):
    return v if tk == 128 else jnp.tile(v, (1, tk // 128))


def _rep(col):
    return jnp.broadcast_to(col, (col.shape[0], 128))


def _per_head(blk, masks):
    sw = pltpu.roll(blk, HEAD_DIM, axis=1)
    return jnp.where(masks[0], blk, sw), jnp.where(masks[0], sw, blk)


def _fill_masked(dst_ref, src_ref, masks, mul=None, ones_lane=None):
    v = src_ref[...]
    if mul is not None:
        v = v * mul
    lane = lax.broadcasted_iota(jnp.int32, (1, PAIR), 1)
    for h in range(2):
        m = _sel(masks[h], v)
        if ones_lane is not None:
            m = jnp.where(lane == ones_lane[h], jnp.ones_like(m), m)
        dst_ref[h] = m


def _head_norms(v, masks):
    sq = v.astype(F32)
    sq = sq * sq
    return [jnp.sqrt(jnp.sum(jnp.where(m, sq, 0.0), axis=-1, keepdims=True)) for m in masks]


def _largest_key_norm(kmax_ref, k_ref, masks):
    for h, n in enumerate(_head_norms(k_ref[...], masks)):
        kmax_ref[h] = jnp.broadcast_to(jnp.max(n, axis=0, keepdims=True), (8, 128))


def _score_bound(q_scaled, kmax_ref, masks):
    return [_rep(n) * kmax_ref[h][0:1, :] for h, n in enumerate(_head_norms(q_scaled, masks))]


def _tri(tk, cmp):
    r = lax.broadcasted_iota(jnp.int32, (tk, tk), 0)
    c = lax.broadcasted_iota(jnp.int32, (tk, tk), 1)
    return cmp(r, c).astype(BF16)


def _diag_mask(r0, tk, strict):
    row = r0 + lax.broadcasted_iota(jnp.int32, (RC, tk), 0)
    col = lax.broadcasted_iota(jnp.int32, (RC, tk), 1)
    return (col < row) if strict else (col <= row)


def _peer_copies(src_ref, land_ref, send_sems, recv_sems, scatter, receive_side):
    x, y, c = lax.axis_index("x"), lax.axis_index("y"), lax.axis_index("c")
    me = 4 * x + 2 * y + c
    copies = []
    for k in range(1, N_DEV):
        px, py, pc = (1 - x if k & 4 else x), (1 - y if k & 2 else y), (1 - c if k & 1 else c)
        slot = 4 * px + 2 * py + pc
        copies.append(pltpu.make_async_remote_copy(
            src_ref=src_ref.at[slot] if scatter else src_ref,
            dst_ref=land_ref.at[slot] if receive_side else land_ref.at[me],
            send_sem=send_sems.at[k - 1], recv_sem=recv_sems.at[k - 1], device_id=(px, py, pc), device_id_type=MESH))
    return copies


def _call_carrying(body, exchange, name, grid, in_specs, out_specs, out_shape, scratch_shapes, args):
    if exchange is None:
        return pl.pallas_call(body, name=name, grid=grid, in_specs=in_specs, out_specs=out_specs, out_shape=out_shape,
                              scratch_shapes=scratch_shapes, compiler_params=_params(("parallel", "arbitrary")))(*args)
    exchanges = [exchange] if isinstance(exchange, tuple) else list(exchange)
    n_in, n_out, n_ex = len(in_specs), len(out_specs), len(exchanges)

    def carrying(*refs):
        srcs = refs[n_in:n_in + n_ex]
        lands = refs[n_in + n_ex + n_out:n_in + 2 * n_ex + n_out]
        sems = refs[len(refs) - 2 * n_ex:]

        def copies(receive_side):
            return [cp for e, (_, scatter) in enumerate(exchanges)
                    for cp in _peer_copies(srcs[e], lands[e], sems[2 * e], sems[2 * e + 1], scatter, receive_side)]

        first = jnp.logical_and(pl.program_id(0) == 0, pl.program_id(1) == 0)
        last = jnp.logical_and(pl.program_id(0) == grid[0] - 1, pl.program_id(1) == grid[1] - 1)

        @pl.when(first)
        def _():
            for cp in copies(False):
                cp.start()

        body(*refs[:n_in], *refs[n_in + n_ex:n_in + n_ex + n_out], *refs[n_in + 2 * n_ex + n_out:len(refs) - 2 * n_ex])

        @pl.when(last)
        def _():
            for cp in copies(True):
                cp.wait_send()
                cp.wait_recv()

    any_space = pl.BlockSpec(memory_space=pl.ANY)
    lands = [jax.ShapeDtypeStruct((N_DEV,) + src.shape[-2:], src.dtype) for src, _ in exchanges]
    return pl.pallas_call(
        carrying, name=name, grid=grid, in_specs=list(in_specs) + [any_space] * n_ex,
        out_specs=list(out_specs) + [any_space] * n_ex, out_shape=list(out_shape) + lands,
        scratch_shapes=list(scratch_shapes) + [pltpu.SemaphoreType.DMA((N_DEV - 1,))] * (2 * n_ex),
        compiler_params=_params(("arbitrary", "arbitrary")))(*args, *[src for src, _ in exchanges])


def _staggered(bodies):
    active, waiting = [], list(bodies)
    while waiting or active:
        if waiting:
            active.append(waiting.pop(0))
        for g in list(active):
            try:
                next(g)
            except StopIteration:
                active.remove(g)


def _streams(tile, j, diag, slot):
    return [tile(j, diag, slot, (0, 1))]


def _tiles(i, tile):
    def step(jj, carry):
        _staggered(_streams(tile, 2 * jj, False, 0) + _streams(tile, 2 * jj + 1, False, 1))
        return carry
    lax.fori_loop(0, (i - 1) // 2, step, 0)

    @pl.when(jnp.logical_and(i >= 1, (i - 1) % 2 == 1))
    def _():
        _staggered(_streams(tile, i - 2, False, 0))

    @pl.when(i >= 1)
    def _():
        _staggered(_streams(tile, i - 1, False, 0) + _streams(tile, i, True, 1))

    @pl.when(i == 0)
    def _():
        _staggered(_streams(tile, i, True, 1))


def _tiles_reversed(i, tile, keep_going):
    @pl.when(i == 0)
    def _():
        _staggered(_streams(tile, i, True, 0))

    @pl.when(i >= 1)
    def _():
        _staggered(_streams(tile, i, True, 0) + _streams(tile, i - 1, False, 1))

    pairs = (i - 1) // 2

    def cond(carry):
        jj, go = carry
        return jnp.logical_and(jj < pairs, go)

    def step(carry):
        jj, _ = carry
        _staggered(_streams(tile, i - 2 - 2 * jj, False, 0) + _streams(tile, i - 3 - 2 * jj, False, 1))
        return jj + 1, keep_going(jnp.maximum(i - 4 - 2 * jj, 0))

    jj, go = lax.while_loop(cond, step, (jnp.int32(0), keep_going(jnp.maximum(i - 2, 0))))

    @pl.when(jnp.logical_and(jnp.logical_and(i >= 1, (i - 1) % 2 == 1), jnp.logical_and(jj == pairs, go)))
    def _():
        _staggered(_streams(tile, 0, False, 0))


def _sb_fwd2(qkv, n_pairs, base, tq, exchange=None):
    S = qkv.shape[0]
    tk = tq
    scale = HEAD_DIM ** -0.5

    def body(q_ref, k_ref, v_ref, o_ref, t_ref, z_ref, hi_ref, suf_ref, p_ref, r_ref, acc_ref, vm_ref):
        i = pl.program_id(1)
        masks = _head_masks()

        @pl.when(i == 0)
        def _():
            _fill_masked(vm_ref, v_ref, masks)

        q2 = q_ref[...] * scale
        qm = [_sel(m, q2) for m in masks]
        incl = _tri(tk, lambda r, c: r >= c)
        r_ref[...] = jnp.zeros_like(r_ref)
        acc_ref[...] = jnp.zeros_like(acc_ref)

        def tile(j, diag, slot, heads):
            off = pl.multiple_of(j * tk, tk)
            k2 = k_ref[pl.ds(off, tk), :]
            v2 = v_ref[pl.ds(off, tk), :]
            for h in heads:
                z_ref[2 * slot + h] = _dot(qm[h], k2, 1, 1)
            yield
            for h in heads:
                def split(r0, h=h):
                    rows = pl.ds(r0, RC)
                    lg = -_softplus(z_ref[2 * slot + h, rows, :])
                    if diag:
                        lg = jnp.where(_diag_mask(r0, tk, True), lg, 0.0)
                    hi_ref[2 * slot + h, rows, :] = lg.astype(BF16)
                _chunks(tq, split)
            yield
            for h in heads:
                suf_ref[2 * slot + h] = _dot(hi_ref[2 * slot + h], incl)
            yield
            for h in heads:
                def weights(r0, h=h):
                    rows = pl.ds(r0, RC)
                    a = jnp.exp(z_ref[2 * slot + h, rows, :] + suf_ref[2 * slot + h, rows, :] + _wide(r_ref[h, rows, :], tk))
                    if diag:
                        a = jnp.where(_diag_mask(r0, tk, True), a, 0.0)
                    p_ref[2 * slot + h, rows, :] = a.astype(BF16)
                _chunks(tq, weights)
            yield
            keys = pl.ds(off, tk)
            for h in heads:
                acc_ref[...] += _dot(p_ref[2 * slot + h], vm_ref[h, keys, :])
            for h in heads:
                r_ref[h] += _rep(suf_ref[2 * slot + h, :, 0:1])

        _tiles_reversed(i, tile, lambda nearest: jnp.max(r_ref[...]) >= VANISH)
        o_ref[...] = acc_ref[...].astype(BF16)
        t_ref[...] = acc_ref[...]

    W = n_pairs * PAIR
    return _call_carrying(
        body, exchange, "sb_fwd", (n_pairs, S // tq), _qkv_specs(S, tq, n_pairs, base),
        [pl.BlockSpec((tq, PAIR), lambda p, i: (i, p)), pl.BlockSpec((tq, PAIR), lambda p, i: (i, p))],
        [jax.ShapeDtypeStruct((S, W), BF16), jax.ShapeDtypeStruct((S, W), F32)],
        [pltpu.VMEM((4, tq, tk), F32), pltpu.VMEM((4, tq, tk), BF16),
         pltpu.VMEM((4, tq, tk), F32), pltpu.VMEM((4, tq, tk), BF16), pltpu.VMEM((2, tq, 128), F32),
         pltpu.VMEM((tq, PAIR), F32), pltpu.VMEM((2, S, PAIR), BF16)],
        (qkv, qkv, qkv))


def _sb_bwd2(qkv, o32, do, n_pairs, base, tq, exchange=None):
    S = qkv.shape[0]
    tk = tq
    nq = S // tq
    scale = HEAD_DIM ** -0.5

    def body(q_ref, k_ref, v_ref, o_ref, do_ref, dq_ref, dk_ref, dv_ref,
             z_ref, g_ref, omb_ref, cum_ref, hi_ref, lo_ref, a_ref, dz_ref,
             r_ref, cg_ref, dl_ref, dq_acc, dk_acc, dv_acc, ks_ref):
        i = pl.program_id(1)
        masks = _head_masks()

        @pl.when(i == 0)
        def _():
            dk_acc[...] = jnp.zeros_like(dk_acc)
            dv_acc[...] = jnp.zeros_like(dv_acc)
            _fill_masked(ks_ref, k_ref, masks, mul=scale)

        q2 = q_ref[...] * scale
        do2 = do_ref[...]
        qm = [_sel(m, q2) for m in masks]
        dom = [_sel(m, do2) for m in masks]
        prod = do2.astype(F32) * o_ref[...]
        for h in range(2):
            dl_ref[h] = _rep(jnp.sum(jnp.where(masks[h], prod, 0.0), axis=-1, keepdims=True))
        suffix = _tri(tk, lambda r, c: r >= c)
        r_ref[...] = jnp.zeros_like(r_ref)
        cg_ref[...] = jnp.zeros_like(cg_ref)
        dq_acc[...] = jnp.zeros_like(dq_acc)

        def tile(j, diag, slot, heads):
            off = pl.multiple_of(j * tk, tk)
            k2 = k_ref[pl.ds(off, tk), :]
            v2 = v_ref[pl.ds(off, tk), :]
            for h in heads:
                z_ref[2 * slot + h] = _dot(qm[h], k2, 1, 1)
                g_ref[2 * slot + h] = _dot(dom[h], v2, 1, 1)
            yield
            for h in heads:
                def split(r0, h=h):
                    rows = pl.ds(r0, RC)
                    sp = _softplus(z_ref[2 * slot + h, rows, :])
                    omb_ref[2 * slot + h, rows, :] = jnp.exp(-sp)
                    lg = -sp
                    if diag:
                        lg = jnp.where(_diag_mask(r0, tk, True), lg, 0.0)
                    hi_ref[2 * slot + h, rows, :] = lg.astype(BF16)
                _chunks(tq, split)
            yield
            for h in heads:
                cum_ref[2 * slot + h] = _dot(hi_ref[2 * slot + h], suffix)
            yield
            for h in heads:
                def weights(r0, h=h):
                    rows = pl.ds(r0, RC)
                    a = jnp.exp(z_ref[2 * slot + h, rows, :] + cum_ref[2 * slot + h, rows, :] + _wide(r_ref[h, rows, :], tk))
                    if diag:
                        a = jnp.where(_diag_mask(r0, tk, True), a, 0.0)
                    ab = a.astype(BF16)
                    g = g_ref[2 * slot + h, rows, :] * ab.astype(F32)
                    g_ref[2 * slot + h, rows, :] = g
                    a_ref[2 * slot + h, rows, :] = ab
                    hi, lo = _split2(g)
                    hi_ref[2 * slot + h, rows, :] = hi
                    lo_ref[2 * slot + h, rows, :] = lo
                _chunks(tq, weights)
            for h in heads:
                r_ref[h] += _rep(cum_ref[2 * slot + h, :, 0:1])
            yield
            for h in heads:
                cum_ref[2 * slot + h] = _dot(hi_ref[2 * slot + h], suffix) + _dot(lo_ref[2 * slot + h], suffix)
            yield
            for h in heads:
                def dscore(r0, h=h):
                    rows = pl.ds(r0, RC)
                    g = g_ref[2 * slot + h, rows, :]
                    from_here = cum_ref[2 * slot + h, rows, :] + _wide(cg_ref[h, rows, :], tk)
                    before = _wide(dl_ref[h, rows, :], tk) - from_here
                    omb = omb_ref[2 * slot + h, rows, :]
                    dz = g * omb - (1.0 - omb) * before
                    if diag:
                        dz = jnp.where(_diag_mask(r0, tk, True), dz, 0.0)
                    dz_ref[2 * slot + h, rows, :] = dz.astype(BF16)
                _chunks(tq, dscore)
            for h in heads:
                cg_ref[h] += _rep(cum_ref[2 * slot + h, :, 0:1])
            yield
            keys = pl.ds(off, tk)
            for h in heads:
                dq_acc[...] += _dot(dz_ref[2 * slot + h], ks_ref[h, keys, :])
                dk_acc[keys, :] += _dot(dz_ref[2 * slot + h], qm[h], 0, 0)
                dv_acc[keys, :] += _dot(a_ref[2 * slot + h], dom[h], 0, 0)

        _tiles_reversed(i, tile, lambda nearest: jnp.max(r_ref[...]) >= VANISH)
        dq_ref[...] = dq_acc[...].astype(BF16)

        @pl.when(i == nq - 1)
        def _():
            dk_ref[...] = dk_acc[...].astype(BF16)
            dv_ref[...] = dv_acc[...].astype(BF16)

    W = n_pairs * PAIR
    in_specs = _qkv_specs(S, tq, n_pairs, base) + [
        pl.BlockSpec((tq, PAIR), lambda p, i: (i, p)),
        pl.BlockSpec((tq, PAIR), lambda p, i: (i, p))]
    out_specs = [pl.BlockSpec((tq, PAIR), lambda p, i: (i, p)),
                 pl.BlockSpec((S, PAIR), lambda p, i: (0, p)),
                 pl.BlockSpec((S, PAIR), lambda p, i: (0, p))]
    big, stat = (4, tq, tk), (2, tq, 128)
    return _call_carrying(
        body, exchange, "sb_bwd", (n_pairs, nq), in_specs, out_specs, [jax.ShapeDtypeStruct((S, W), BF16)] * 3,
        [pltpu.VMEM(big, F32)] * 4 + [pltpu.VMEM(big, BF16)] * 4 + [pltpu.VMEM(stat, F32)] * 3
        + [pltpu.VMEM((tq, PAIR), F32), pltpu.VMEM((S, PAIR), F32), pltpu.VMEM((S, PAIR), F32),
           pltpu.VMEM((2, S, PAIR), BF16)],
        (qkv, qkv, qkv, o32, do))


def _fox_fwd2(qkv, fcx, fcr, n_pairs, base, tq, exchange=None):
    S = qkv.shape[0]
    tk = tq
    scale = HEAD_DIM ** -0.5
    spare = (HEAD_DIM, 0)

    def body(q_ref, k_ref, v_ref, fq_ref, fk_ref, o_ref, lse_ref, s_ref, p_ref, m_ref, al_ref, fqr_ref, acc_ref, vm_ref):
        i = pl.program_id(1)
        masks = _head_masks()

        @pl.when(i == 0)
        def _():
            _fill_masked(vm_ref, v_ref, masks, ones_lane=spare)

        q2 = q_ref[...] * scale
        qm = [_sel(m, q2) for m in masks]
        f0, f1 = _per_head(fq_ref[...], masks)
        fqr_ref[0] = f0
        fqr_ref[1] = f1
        m_ref[...] = jnp.full(m_ref.shape, NEG, F32)
        acc_ref[...] = jnp.zeros_like(acc_ref)

        def tile(j, diag, slot, heads):
            off = pl.multiple_of(j * tk, tk)
            k2 = k_ref[pl.ds(off, tk), :]
            v2 = v_ref[pl.ds(off, tk), :]
            fk2 = fk_ref[0, j]
            for h in heads:
                s_ref[2 * slot + h] = _dot(qm[h], k2, 1, 1)
            yield
            for h in heads:
                fk_row = fk2[h:h + 1, :]

                def probs(r0, h=h, fk_row=fk_row):
                    rows = pl.ds(r0, RC)
                    sv = s_ref[2 * slot + h, rows, :] - fk_row
                    if diag:
                        sv = jnp.where(_diag_mask(r0, tk, False), sv, NEG)
                    fq = fqr_ref[h, rows, :]
                    m_prev = m_ref[h, rows, :]
                    m_new = jnp.maximum(m_prev, jnp.max(sv, axis=-1, keepdims=True) + fq)
                    p_ref[2 * slot + h, rows, :] = jnp.exp(sv + _wide(fq - m_new, tk)).astype(BF16)
                    al_ref[2 * slot + h, rows, :] = jnp.exp(m_prev - m_new)
                    m_ref[h, rows, :] = m_new
                _chunks(tq, probs)
            yield
            for h in heads:
                acc_ref[h] = acc_ref[h] * al_ref[2 * slot + h] + _dot(p_ref[2 * slot + h], vm_ref[h, pl.ds(off, tk), :])

        _tiles(i, tile)
        a0, a1 = acc_ref[0], acc_ref[1]
        l0 = _rep(a0[:, spare[0]:spare[0] + 1])
        l1 = _rep(a1[:, spare[1]:spare[1] + 1])
        o_ref[...] = jnp.where(masks[0], a0 / l0, a1 / l1).astype(BF16)
        lse_ref[...] = jnp.where(masks[0], m_ref[0] + jnp.log(l0), m_ref[1] + jnp.log(l1))

    W = n_pairs * PAIR
    return _call_carrying(
        body, exchange, "fox_fwd", (n_pairs, S // tq), _fox_specs(S, tq, n_pairs, base),
        [pl.BlockSpec((tq, PAIR), lambda p, i: (i, p)), pl.BlockSpec((tq, PAIR), lambda p, i: (i, p))],
        [jax.ShapeDtypeStruct((S, W), BF16), jax.ShapeDtypeStruct((S, W), F32)],
        [pltpu.VMEM((4, tq, tk), F32), pltpu.VMEM((4, tq, tk), BF16), pltpu.VMEM((2, tq, 128), F32),
         pltpu.VMEM((4, tq, 128), F32), pltpu.VMEM((2, tq, 128), F32), pltpu.VMEM((2, tq, 128), F32),
         pltpu.VMEM((2, S, PAIR), BF16)],
        (qkv, qkv, qkv, fcx, fcr))


def _fox_bwd2(qkv, fcx, fcr, o, lse, do, n_pairs, base, tq, exchange=None):
    S = qkv.shape[0]
    tk = tq
    nq = S // tq
    scale = HEAD_DIM ** -0.5

    def body(q_ref, k_ref, v_ref, fq_ref, fk_ref, o_ref, lse_ref, do_ref,
             dq_ref, dk_ref, dv_ref, dfq_ref, dfk_ref,
             s_ref, dp_ref, p_ref, ds_ref, row_ref, dl_ref, dfq_acc, col_ref, dq_acc, dk_acc, dv_acc, ks_ref):
        i = pl.program_id(1)
        masks = _head_masks()

        @pl.when(i == 0)
        def _():
            dk_acc[...] = jnp.zeros_like(dk_acc)
            dv_acc[...] = jnp.zeros_like(dv_acc)
            dfk_ref[...] = jnp.zeros_like(dfk_ref)
            _fill_masked(ks_ref, k_ref, masks, mul=scale)

        q2 = q_ref[...] * scale
        do2 = do_ref[...]
        qm = [_sel(m, q2) for m in masks]
        dom = [_sel(m, do2) for m in masks]
        f0, f1 = _per_head(fq_ref[...], masks)
        l0, l1 = _per_head(lse_ref[...], masks)
        row_ref[0] = f0 - l0
        row_ref[1] = f1 - l1
        prod = do2.astype(F32) * o_ref[...].astype(F32)
        for h in range(2):
            dl_ref[h] = _rep(jnp.sum(jnp.where(masks[h], prod, 0.0), axis=-1, keepdims=True))
        dfq_acc[...] = jnp.zeros_like(dfq_acc)
        dq_acc[...] = jnp.zeros_like(dq_acc)

        def tile(j, diag, slot, heads):
            off = pl.multiple_of(j * tk, tk)
            k2 = k_ref[pl.ds(off, tk), :]
            v2 = v_ref[pl.ds(off, tk), :]
            fk2 = fk_ref[0, j]
            for h in heads:
                s_ref[2 * slot + h] = _dot(qm[h], k2, 1, 1)
                dp_ref[2 * slot + h] = _dot(dom[h], v2, 1, 1)
            yield
            for h in heads:
                col_ref[2 * slot + h] = jnp.zeros((8, tk), F32)
                fk_row = fk2[h:h + 1, :]

                def dscore(r0, h=h, fk_row=fk_row):
                    rows = pl.ds(r0, RC)
                    p = jnp.exp(s_ref[2 * slot + h, rows, :] - fk_row + _wide(row_ref[h, rows, :], tk))
                    if diag:
                        p = jnp.where(_diag_mask(r0, tk, False), p, 0.0)
                    ds = p * (dp_ref[2 * slot + h, rows, :] - _wide(dl_ref[h, rows, :], tk))
                    p_ref[2 * slot + h, rows, :] = p.astype(BF16)
                    ds_ref[2 * slot + h, rows, :] = ds.astype(BF16)
                    dfq_acc[h, rows, :] += _rep(jnp.sum(ds, axis=-1, keepdims=True))
                    col_ref[2 * slot + h] += jnp.sum(ds.reshape(RC // 8, 8, tk), axis=0)
                _chunks(tq, dscore)
            yield
            keys = pl.ds(off, tk)
            for h in heads:
                dq_acc[...] += _dot(ds_ref[2 * slot + h], ks_ref[h, keys, :])
                dk_acc[keys, :] += _dot(ds_ref[2 * slot + h], qm[h], 0, 0)
                dv_acc[keys, :] += _dot(p_ref[2 * slot + h], dom[h], 0, 0)
            for h in heads:
                dfk_ref[0, j, h:h + 1, :] += jnp.sum(col_ref[2 * slot + h], axis=0, keepdims=True)

        _tiles(i, tile)
        dq_ref[...] = dq_acc[...].astype(BF16)
        dfq_ref[...] = jnp.where(masks[0], dfq_acc[0], dfq_acc[1])

        @pl.when(i == nq - 1)
        def _():
            dk_ref[...] = dk_acc[...].astype(BF16)
            dv_ref[...] = dv_acc[...].astype(BF16)

    W = n_pairs * PAIR
    in_specs = _fox_specs(S, tq, n_pairs, base) + [
        pl.BlockSpec((tq, PAIR), lambda p, i: (i, p)),
        pl.BlockSpec((tq, PAIR), lambda p, i: (i, p)),
        pl.BlockSpec((tq, PAIR), lambda p, i: (i, p))]
    out_specs = [pl.BlockSpec((tq, PAIR), lambda p, i: (i, p)),
                 pl.BlockSpec((S, PAIR), lambda p, i: (0, p)),
                 pl.BlockSpec((S, PAIR), lambda p, i: (0, p)),
                 pl.BlockSpec((tq, PAIR), lambda p, i: (i, p)),
                 pl.BlockSpec((1, nq, 8, tk), lambda p, i: (p, 0, 0, 0))]
    return _call_carrying(
        body, exchange, "fox_bwd", (n_pairs, nq), in_specs, out_specs,
        [jax.ShapeDtypeStruct((S, W), BF16)] * 3
        + [jax.ShapeDtypeStruct((S, W), F32), jax.ShapeDtypeStruct((n_pairs, nq, 8, tk), F32)],
        [pltpu.VMEM((4, tq, tk), F32)] * 2 + [pltpu.VMEM((4, tq, tk), BF16)] * 2
        + [pltpu.VMEM((2, tq, 128), F32)] * 3 + [pltpu.VMEM((4, 8, tk), F32)]
        + [pltpu.VMEM((tq, PAIR), F32), pltpu.VMEM((S, PAIR), F32), pltpu.VMEM((S, PAIR), F32),
           pltpu.VMEM((2, S, PAIR), BF16)],
        (qkv, qkv, qkv, fcx, fcr, o, lse, do))


def _swiglu_fwd(u2, wg, wu):
    S, D = u2.shape
    FF = wg.shape[1]
    tm, tn = _pick(S, (512, 256, 128)), _divisors(FF, 1536)[0]

    def body(u_ref, g_ref, w_ref, a_ref, b_ref, h_ref):
        u = u_ref[...]
        a = _dot(u, g_ref[...])
        b = _dot(u, w_ref[...])
        a_ref[...] = a.astype(BF16)
        b_ref[...] = b.astype(BF16)
        h_ref[...] = (a / (1.0 + jnp.exp(-a)) * b).astype(BF16)

    spec_o = pl.BlockSpec((tm, tn), lambda i, j: (i, j))
    return pl.pallas_call(
        body, name="swiglu_fwd", grid=(S // tm, FF // tn),
        in_specs=[pl.BlockSpec((tm, D), lambda i, j: (i, 0)),
                  pl.BlockSpec((D, tn), lambda i, j: (0, j)),
                  pl.BlockSpec((D, tn), lambda i, j: (0, j))],
        out_specs=[spec_o] * 3, out_shape=[jax.ShapeDtypeStruct((S, FF), BF16)] * 3,
        compiler_params=_params(("parallel", "parallel")),
    )(u2, wg, wu)


def _swiglu_bwd(dh, wd, a, b):
    S, D = dh.shape
    FF = wd.shape[0]
    tm, tn = _pick(S, (512, 256, 128)), _divisors(FF, 1536)[0]

    def body(dh_ref, w_ref, a_ref, b_ref, da_ref, db_ref):
        dhin = _dot(dh_ref[...], w_ref[...], 1, 1)
        av = a_ref[...].astype(F32)
        bv = b_ref[...].astype(F32)
        sig = 1.0 / (1.0 + jnp.exp(-av))
        da_ref[...] = (dhin * bv * (sig * (1.0 + av * (1.0 - sig)))).astype(BF16)
        db_ref[...] = (dhin * (av * sig)).astype(BF16)

    spec_o = pl.BlockSpec((tm, tn), lambda i, j: (i, j))
    return pl.pallas_call(
        body, name="swiglu_bwd", grid=(S // tm, FF // tn),
        in_specs=[pl.BlockSpec((tm, D), lambda i, j: (i, 0)),
                  pl.BlockSpec((tn, D), lambda i, j: (j, 0)), spec_o, spec_o],
        out_specs=[spec_o] * 2, out_shape=[jax.ShapeDtypeStruct((S, FF), BF16)] * 2,
        compiler_params=_params(("parallel", "parallel")),
    )(dh, wd, a, b)


def _gate_bwd(dmix, wo, g_sb, g_fx, y_sb, y_fx):
    S, D = dmix.shape
    tm, tn = _pick(S, (512, 256, 128)), _divisors(D, 1024)[0]

    def body(dm_ref, w_ref, gs_ref, gf_ref, ys_ref, yf_ref, dys_ref, dyf_ref, dls_ref, dlf_ref, bs_ref, bf_ref):
        @pl.when(pl.program_id(1) == 0)
        def _():
            bs_ref[...] = jnp.zeros_like(bs_ref)
            bf_ref[...] = jnp.zeros_like(bf_ref)

        dmi = _dot(dm_ref[...], w_ref[...], 1, 1)
        gs, gf = gs_ref[...], gf_ref[...]
        dys_ref[...] = (dmi * gs).astype(BF16)
        dyf_ref[...] = (dmi * gf).astype(BF16)
        dls = dmi * ys_ref[...] * gs * (1.0 - gs)
        dlf = dmi * yf_ref[...] * gf * (1.0 - gf)
        dls_ref[...] = dls.astype(BF16)
        dlf_ref[...] = dlf.astype(BF16)
        bs_ref[0:1, :] += _colsum(dls)
        bf_ref[0:1, :] += _colsum(dlf)

    t = pl.BlockSpec((tm, tn), lambda j, i: (i, j))
    accs = pl.BlockSpec((8, tn), lambda j, i: (0, j))
    return pl.pallas_call(
        body, name="gate_bwd", grid=(D // tn, S // tm),
        in_specs=[pl.BlockSpec((tm, D), lambda j, i: (i, 0)),
                  pl.BlockSpec((tn, D), lambda j, i: (j, 0)), t, t, t, t],
        out_specs=[t, t, t, t, accs, accs],
        out_shape=[jax.ShapeDtypeStruct((S, D), BF16)] * 4 + [jax.ShapeDtypeStruct((8, D), F32)] * 2,
        compiler_params=_params(("parallel", "arbitrary")),
    )(dmix, wo, g_sb, g_fx, y_sb, y_fx)


def _local_step(x, target, ada8, lnp8, bg_sb, bg_fx, bf_pad, wqkv, wf, wgs, wgf, gather, later_weights, pack_early,
                pack_last):
    S, D = x.shape
    W = wqkv.shape[1] // 6
    n_pairs = W // PAIR
    n_heads = W // HEAD_DIM
    ts = _pick(S, (512, 256, 128))
    tq = _pick(S, (256, 128))

    u1 = _ln_mod(x, ada8, ts)
    qkv = _mm([(u1, wqkv)], 'nn', BF16, "in_qkv")
    f = _mm([(u1, wf)], 'nn', F32, "in_f")
    g_sb = _mm([(u1, wgs)], 'nn', F32, "in_gsb", bias=bg_sb, act='sigmoid')
    g_fx = _mm([(u1, wgf)], 'nn', F32, "in_gfx", bias=bg_fx, act='sigmoid')
    fc = _fgate_fwd(f, bf_pad, _pick(S, (512, 256, 128)))
    fch = fc[:, :n_heads]
    fcx = jnp.repeat(fch, HEAD_DIM, axis=1)
    nq = S // tq
    fcr = jnp.pad(fch.T.reshape(n_pairs, 2, nq, tq).transpose(0, 2, 1, 3),
                  ((0, 0), (0, 0), (0, 6), (0, 0)))
    o_sb, o_sb32, *zone_a = _sb_fwd2(qkv, n_pairs, 0, tq, gather[0])
    o_fx, lse, *zone_b = _fox_fwd2(qkv, fcx, fcr, n_pairs, 3 * n_pairs, tq, gather[1])
    wsb, wfx, wo, wfg, wfu, wfd = later_weights(*zone_a, *zone_b)
    y_sb =_mm([(o_sb, wsb)], 'nn', F32, "out_sb")
    y_fx = _mm([(o_fx, wfx)], 'nn', F32, "out_fx")
    mix_in = _gate_mix(g_sb, g_fx, y_sb, y_fx, ts)
    mix = _mm([(mix_in, wo)], 'nn', F32, "out_o")
    x1, u2 = _post_attn(x, mix, ada8, lnp8, ts)
    a, b, hin = _swiglu_fwd(u2, wfg, wfu)
    h = _mm([(hin, wfd)], 'nn', F32, "ffn_down")
    dr2, dh, st_loss = _loss_head(x1, h, target, ada8, lnp8, ts)

    da, db = _swiglu_bwd(dh, wfd, a, b)
    g_wfd = _mm([(hin, dh)], 'tn', F32, "g_ffn_down")
    du2 = _mm([(da, wfg), (db, wfu)], 'nt', F32, "d_u2")
    g_wfg = _mm([(u2, da)], 'tn', F32, "g_ffn_gate")
    g_wfu = _mm([(u2, db)], 'tn', F32, "g_ffn_up")
    dr1, dmix, st_mid = _mid_bwd(du2, x1, dr2, mix, x, ada8, lnp8, ts)
    dys, dyf, dls, dlf, gb_sb, gb_fx = _gate_bwd(dmix, wo, g_sb, g_fx, y_sb, y_fx)
    g_wo = _mm([(mix_in, dmix)], 'tn', F32, "g_w_o")
    do_sb = _mm([(dys, wsb)], 'nt', BF16, "d_o_sb")
    do_fx = _mm([(dyf, wfx)], 'nt', BF16, "d_o_fx")
    g_wsb = _mm([(o_sb, dys)], 'tn', F32, "g_sb_out")
    g_wfx = _mm([(o_fx, dyf)], 'tn', F32, "g_fox_out")
    scatter = pack_early(dict(sb=g_wsb, fx=g_wfx, o=g_wo, fg=g_wfg, fu=g_wfu, fd=g_wfd))
    dq_s, dk_s, dv_s, *zone_a = _sb_bwd2(qkv, o_sb32, do_sb, n_pairs, 0, tq, scatter[0])
    dq_f, dk_f, dv_f, dfq, dfk, *zone_b = _fox_bwd2(qkv, fcx, fcr, o_fx, lse, do_fx, n_pairs, 3 * n_pairs, tq,
                                                    scatter[1])
    early = [(scatter[0], zone_a[0] if zone_a else None), (scatter[1], zone_b[0] if zone_b else None)]
    dfc = dfq[:, ::HEAD_DIM] - dfk[:, :, :2, :].transpose(0, 2, 1, 3).reshape(n_heads, S).T
    dfc = jnp.pad(dfc, ((0, 0), (0, 128 - n_heads)))
    df, gb_f = _fgate_bwd(dfc, f, bf_pad, _pick(S, (512, 256, 128)))
    grads = [dq_s, dk_s, dv_s, dq_f, dk_f, dv_f]
    g_wqkv = [_mm([(u1, g)], 'tn', F32, "g_in_%d" % n) for n, g in enumerate(grads)]
    g_wf = _mm([(u1, df)], 'tn', F32, "g_in_f")
    g_wgs = _mm([(u1, dls)], 'tn', F32, "g_in_gsb")
    g_wgf = _mm([(u1, dlf)], 'tn', F32, "g_in_gfx")
    wgrads = dict(qkv=g_wqkv, f=g_wf, gs=g_wgs, gf=g_wgf)
    last = pack_last(wgrads)
    du1 = _mm([(g, wqkv, W, 0, n) for n, g in enumerate(grads)] + [(df, wf), (dls, wgs), (dlf, wgf)],
              'nt', F32, "d_u1", exchange=last)
    du1, last_zone = du1 if last is not None else (du1, None)
    gx, st_first = _first_bwd(du1, x, dr1, ada8, ts)

    stats = dict(loss=st_loss, mid=st_mid, first=st_first, gb_sb=gb_sb, gb_fx=gb_fx, gb_f=gb_f)
    return gx, wgrads, stats, early, (last, last_zone)


def _position():
    x, y, c = lax.axis_index("x"), lax.axis_index("y"), lax.axis_index("c")
    return x, y, c, 4 * x + 2 * y + c


def _flip(x, y, c, k):
    px = 1 - x if k & 4 else x
    py = 1 - y if k & 2 else y
    pc = 1 - c if k & 1 else c
    return (px, py, pc), 4 * px + 2 * py + pc


def _all_gather_small(v, name):
    r, n = v.shape

    def body(x_ref, out_ref, send_sems, recv_sems, local_sem):
        x, y, c, me = _position()
        mine = pltpu.make_async_copy(x_ref, out_ref.at[me], local_sem)
        mine.start()
        sends = []
        for k in range(1, N_DEV):
            peer, _ = _flip(x, y, c, k)
            cp = pltpu.make_async_remote_copy(
                src_ref=x_ref, dst_ref=out_ref.at[me], send_sem=send_sems.at[k - 1], recv_sem=recv_sems.at[k - 1],
                device_id=peer, device_id_type=MESH)
            cp.start()
            sends.append(cp)
        for k in range(1, N_DEV):
            peer, slot = _flip(x, y, c, k)
            pltpu.make_async_remote_copy(
                src_ref=x_ref, dst_ref=out_ref.at[slot], send_sem=send_sems.at[k - 1], recv_sem=recv_sems.at[k - 1],
                device_id=peer, device_id_type=MESH).wait_recv()
        for cp in sends:
            cp.wait_send()
        mine.wait()

    return pl.pallas_call(
        body, name=name, out_shape=jax.ShapeDtypeStruct((N_DEV, r, n), v.dtype),
        in_specs=[pl.BlockSpec(memory_space=pltpu.VMEM)], out_specs=pl.BlockSpec(memory_space=pltpu.VMEM),
        scratch_shapes=[pltpu.SemaphoreType.DMA((N_DEV - 1,)), pltpu.SemaphoreType.DMA((N_DEV - 1,)),
                        pltpu.SemaphoreType.DMA],
    )(v)


def _all_gather_weights(packed):
    R, C = packed.shape

    def body(x_ref, out_ref, send_sems, recv_sems, local_sem):
        x, y, c, me = _position()
        sibling, sib_slot = _flip(x, y, c, 1)
        mine = pltpu.make_async_copy(x_ref, out_ref.at[me], local_sem)
        mine.start()

        def copy(k, slot, to, src=None):
            return pltpu.make_async_remote_copy(
                src_ref=out_ref.at[slot] if src is None else src, dst_ref=out_ref.at[slot],
                send_sem=send_sems.at[k], recv_sem=recv_sems.at[k], device_id=to, device_id_type=MESH)

        first = [copy(0, me, sibling, src=x_ref)]
        chips = (4, 2, 6)
        for n, k in enumerate(chips):
            peer, _ = _flip(x, y, c, k)
            first.append(copy(1 + n, me, peer, src=x_ref))
        for cp in first:
            cp.start()
        passed = []
        for n, k in enumerate(chips):
            peer, slot = _flip(x, y, c, k)
            copy(1 + n, slot, peer).wait_recv()
            cp = copy(4 + n, slot, sibling)
            cp.start()
            passed.append(cp)
        copy(0, sib_slot, sibling).wait_recv()
        for n, k in enumerate(chips):
            _, slot = _flip(x, y, c, k | 1)
            copy(4 + n, slot, sibling).wait_recv()
        for cp in first + passed:
            cp.wait_send()
        mine.wait()

    return pl.pallas_call(
        body, name="all_gather_weights", out_shape=jax.ShapeDtypeStruct((N_DEV, R, C), packed.dtype),
        in_specs=[pl.BlockSpec(memory_space=pl.ANY)], out_specs=pl.BlockSpec(memory_space=pl.ANY),
        scratch_shapes=[pltpu.SemaphoreType.DMA((7,)), pltpu.SemaphoreType.DMA((7,)), pltpu.SemaphoreType.DMA],
    )(packed)


def _exchange_grads(gpack):
    _, R, C = gpack.shape

    def body(g_ref, out_ref, send_sems, recv_sems, local_sem):
        x, y, c, me = _position()
        mine = pltpu.make_async_copy(g_ref.at[me], out_ref.at[me], local_sem)
        mine.start()
        sends = []
        for k in range(1, N_DEV):
            peer, slot = _flip(x, y, c, k)
            cp = pltpu.make_async_remote_copy(
                src_ref=g_ref.at[slot], dst_ref=out_ref.at[me], send_sem=send_sems.at[k - 1],
                recv_sem=recv_sems.at[k - 1], device_id=peer, device_id_type=MESH)
            cp.start()
            sends.append(cp)
        for k in range(1, N_DEV):
            peer, slot = _flip(x, y, c, k)
            pltpu.make_async_remote_copy(
                src_ref=g_ref.at[slot], dst_ref=out_ref.at[slot], send_sem=send_sems.at[k - 1],
                recv_sem=recv_sems.at[k - 1], device_id=peer, device_id_type=MESH).wait_recv()
        for cp in sends:
            cp.wait_send()
        mine.wait()

    return pl.pallas_call(
        body, name="exchange_grads", out_shape=jax.ShapeDtypeStruct((N_DEV, R, C), gpack.dtype),
        in_specs=[pl.BlockSpec(memory_space=pl.ANY)], out_specs=pl.BlockSpec(memory_space=pl.ANY),
        scratch_shapes=[pltpu.SemaphoreType.DMA((N_DEV - 1,)), pltpu.SemaphoreType.DMA((N_DEV - 1,)),
                        pltpu.SemaphoreType.DMA],
    )(gpack)


def _own_slot(land, own):
    me = 4 * lax.axis_index("x") + 2 * lax.axis_index("y") + lax.axis_index("c")
    return lax.dynamic_update_slice(land, own[None], (me, 0, 0))


def _sum_slots(recv, name, tr):
    n, R, C = recv.shape

    def body(r_ref, o_ref):
        acc = r_ref[0].astype(F32)
        for s in range(1, n):
            acc = acc + r_ref[s].astype(F32)
        o_ref[...] = acc

    return pl.pallas_call(
        body, name=name, grid=(R // tr,), in_specs=[pl.BlockSpec((n, tr, C), lambda i: (0, i, 0))],
        out_specs=pl.BlockSpec((tr, C), lambda i: (i, 0)), out_shape=jax.ShapeDtypeStruct((R, C), F32),
        compiler_params=_params(("parallel",)),
    )(recv)


def _sum_stats(st_all, loss_row):
    n, r, D = st_all.shape

    def body(s_ref, o_ref, l_ref):
        acc = s_ref[0]
        for d in range(1, n):
            acc = acc + s_ref[d]
        o_ref[...] = acc
        l_ref[...] = jnp.zeros((8, 128), F32) + jnp.sum(acc[loss_row:loss_row + 1, :], axis=-1, keepdims=True)

    return pl.pallas_call(
        body, name="sum_stats", out_shape=[jax.ShapeDtypeStruct((r, D), F32), jax.ShapeDtypeStruct((8, 128), F32)],
    )(st_all)


def _adamw(w, g, m, v, name):
    R, C = w.shape
    tr = _pick(R, (256, 176, 128, 64, 32, 16, 8))
    c1 = 1.0 / (1.0 - ADAM_B1 ** ADAM_STEP)
    c2 = 1.0 / (1.0 - ADAM_B2 ** ADAM_STEP)

    def body(w_ref, g_ref, m_ref, v_ref, d_ref, nm_ref, nv_ref):
        gv = g_ref[...]
        nm = ADAM_B1 * m_ref[...] + (1.0 - ADAM_B1) * gv
        nv = ADAM_B2 * v_ref[...] + (1.0 - ADAM_B2) * (gv * gv)
        nm_ref[...] = nm
        nv_ref[...] = nv
        d_ref[...] = -ADAM_LR * ((nm * c1) / (jnp.sqrt(nv * c2) + ADAM_EPS) + ADAM_WD * w_ref[...])

    spec = pl.BlockSpec((tr, C), lambda i: (i, 0))
    return pl.pallas_call(
        body, name=name, grid=(R // tr,), in_specs=[spec] * 4, out_specs=[spec] * 3,
        out_shape=[jax.ShapeDtypeStruct((R, C), F32)] * 3, compiler_params=_params(("parallel",)),
    )(w, g, m, v)


def _round16(n):
    return -(-n // 16) * 16


def _pack_layout(D, in_cols, ff, W):
    parts = [("in", D * (in_cols // N_DEV) // D), ("fg", ff // N_DEV), ("fu", ff // N_DEV),
             ("sb", W * (D // N_DEV) // D), ("fx", W * (D // N_DEV) // D), ("o", D // N_DEV), ("fd", ff // N_DEV)]
    layout, off = {}, 0
    for nm, rows in parts:
        layout[nm] = (off, rows)
        off += _round16(rows)
    return layout, off


def _rows_of(a, D, rows):
    a = a.reshape(rows, D)
    return jnp.pad(a, ((0, _round16(rows) - rows), (0, 0)))


def _cols_to_dest(g, D):
    K, N = g.shape
    n = N // N_DEV
    return g.reshape(K, N_DEV, n).transpose(1, 0, 2).reshape(N_DEV, K * n // D, D)


def _cols_from_src(blocks, K, n):
    return blocks.reshape(N_DEV, K, n).transpose(1, 0, 2).reshape(K, N_DEV * n)


def _pad_rows16(a):
    rows = a.shape[1]
    return jnp.pad(a, ((0, 0), (0, _round16(rows) - rows), (0, 0)))


def kernel(x, c, w_ada, b_ada, w_in, b_gate, b_forget, w_sb_out, w_fox_out, w_o, ln1_g, ln1_b, w_ffn_gate, w_ffn_up, w_ffn_down, ln2_g, ln2_b, loss_target, m_w_ada, m_b_ada, m_w_in, m_b_gate, m_b_forget, m_w_sb_out, m_w_fox_out, m_w_o, m_ln1_g, m_ln1_b, m_w_ffn_gate, m_w_ffn_up, m_w_ffn_down, m_ln2_g, m_ln2_b, v_w_ada, v_b_ada, v_w_in, v_b_gate, v_b_forget, v_w_sb_out, v_w_fox_out, v_w_o, v_ln1_g, v_ln1_b, v_w_ffn_gate, v_w_ffn_up, v_w_ffn_down, v_ln2_g, v_ln2_b):
    S, D = x.shape[1], x.shape[2]
    W = w_sb_out.shape[1]
    n_heads = b_forget.shape[1]
    ff = w_ffn_down.shape[1] * N_DEV
    in_loc = w_in.shape[2]
    in_cols = in_loc * N_DEV
    ada_loc = w_ada.shape[2]
    n_cond = ada_loc * N_DEV // D
    assert w_ada.shape[0] == 1 and n_cond == 6 and in_cols == 6 * W + n_heads + 2 * D and n_heads <= 128
    me = 4 * lax.axis_index("x") + 2 * lax.axis_index("y") + lax.axis_index("c")

    c_all = _all_gather_small(c, "gather_c").reshape(N_DEV, D)
    c16 = jnp.pad(c_all, ((0, 16 - N_DEV), (0, 0)))
    b_cols = lax.dynamic_slice(b_ada, (0, me * ada_loc), (1, ada_loc))
    ada_cols = _mm([(c16, w_ada[0])], 'nn', F32, "ada_fwd", bias=b_cols, silu_a=True)[:N_DEV]
    ada_all = _all_gather_small(ada_cols, "gather_ada")
    ada_me = lax.dynamic_index_in_dim(ada_all, me, axis=1, keepdims=False)
    ada8 = jnp.pad(ada_me.reshape(n_cond, D), ((0, 8 - n_cond), (0, 0)))
    lnp8 = jnp.concatenate([ln1_g, ln1_b, ln2_g, ln2_b, jnp.zeros((4, D), F32)], axis=0)

    layout, R = _pack_layout(D, in_cols, ff, W)
    shards = dict(**{"in": w_in[0]}, fg=w_ffn_gate[0], fu=w_ffn_up[0], sb=w_sb_out[0], fx=w_fox_out[0], o=w_o[0],
                  fd=w_ffn_down[0])
    rows_fwd, rows_bwd = ("sb", "fx", "o"), ("fd", "o", "sb", "fx")
    rows_of = {nm: layout[nm][1] for nm in layout}

    def offsets(names):
        offs, off = {}, 0
        for nm in names:
            offs[nm] = off
            off += _round16(rows_of[nm])
        return offs

    def as_rows(names):
        return jnp.concatenate([_rows_of(shards[nm].astype(BF16), D, rows_of[nm]) for nm in names], axis=0)

    def whole_from(blocks):
        return blocks.transpose(1, 0, 2).reshape(blocks.shape[1], N_DEV * blocks.shape[2])

    def blocks_of(g):
        return g.reshape(g.shape[0], N_DEV, g.shape[1] // N_DEV).transpose(1, 0, 2)

    gather_src = [as_rows(rows_fwd),
                  jnp.concatenate([shards["fg"].astype(BF16), shards["fu"].astype(BF16)], axis=0), as_rows(("fd",))]
    gathered_in = _all_gather_weights(shards["in"].astype(BF16))

    w_in_full = whole_from(gathered_in)
    wqkv = w_in_full[:, :6 * W]
    wf = jnp.pad(w_in_full[:, 6 * W:6 * W + n_heads], ((0, 0), (0, 128 - n_heads)))
    wgs = w_in_full[:, 6 * W + n_heads:6 * W + n_heads + D]
    wgf = w_in_full[:, 6 * W + n_heads + D:]
    bf_pad = jnp.pad(b_forget, ((0, 0), (0, 128 - n_heads)))

    def later_weights(zone_rows, zone_gate_up, zone_down):
        rows, offs = _own_slot(zone_rows, gather_src[0]), offsets(rows_fwd)
        part = {nm: rows[:, offs[nm]:offs[nm] + rows_of[nm], :] for nm in rows_fwd}
        gate_up = _own_slot(zone_gate_up, gather_src[1])
        down = _own_slot(zone_down, gather_src[2])[:, :rows_of["fd"], :]
        return (_cols_from_src(part["sb"], W, D // N_DEV), _cols_from_src(part["fx"], W, D // N_DEV),
                part["o"].reshape(D, D), whole_from(gate_up[:, :D, :]), whole_from(gate_up[:, D:, :]),
                down.reshape(ff, D))

    def pack_early(g):
        dest = {"sb": _cols_to_dest(g["sb"], D), "fx": _cols_to_dest(g["fx"], D),
                "o": g["o"].reshape(N_DEV, D // N_DEV, D), "fd": g["fd"].reshape(N_DEV, ff // N_DEV, D)}
        rows = jnp.concatenate([_pad_rows16(dest[nm].astype(BF16)) for nm in rows_bwd], axis=1)
        gate_up = jnp.concatenate([blocks_of(g["fg"].astype(BF16)), blocks_of(g["fu"].astype(BF16))], axis=1)
        return [(rows, True), (gate_up, True)]

    def pack_last(g):
        g_in = jnp.concatenate(g["qkv"] + [g["f"][:, :n_heads], g["gs"], g["gf"]], axis=1)
        return blocks_of(g_in.astype(BF16)), True

    gx, wg, st, early, ((pack_in, _), land_in) = _local_step(
        x[0], loss_target[0], ada8, lnp8, b_gate[:, :D], b_gate[:, D:], bf_pad, wqkv, wf, wgs, wgf,
        [(gather_src[0], False), [(gather_src[1], False), (gather_src[2], False)]],
        later_weights, pack_early, pack_last)

    def summed(zone, sent, name):
        own = lax.dynamic_index_in_dim(sent, me, axis=0, keepdims=False)
        rows = zone.shape[1]
        block = max(t for t in range(16, 705, 16) if rows % t == 0)
        return _sum_slots(_own_slot(zone, own), name, block)

    ((sent_rows, _), zone_rows), ((sent_gate_up, _), zone_gate_up) = early
    gate_up = summed(zone_gate_up, sent_gate_up, "sum_grads_gate_up")
    gsum = {"in": summed(land_in, pack_in, "sum_grads_in"), "fg": gate_up[:D], "fu": gate_up[D:]}
    total, offs = summed(zone_rows, sent_rows, "sum_grads_rows"), offsets(rows_bwd)
    for nm in rows_bwd:
        gsum[nm] = total[offs[nm]:offs[nm] + rows_of[nm]]

    def gshard(nm, shape):
        return gsum[nm].reshape(shape)

    zrow = jnp.zeros((1, D), F32)
    gb_f_row = jnp.pad(st["gb_f"][0:1], ((0, 0), (0, D - 128)))
    stats16 = jnp.concatenate([
        st["first"][1:2], st["first"][0:1], st["mid"][4:5], st["mid"][1:2], st["mid"][0:1], st["loss"][3:4],
        st["mid"][2:3], st["mid"][3:4], st["loss"][1:2], st["loss"][2:3], st["gb_sb"][0:1], st["gb_fx"][0:1],
        st["loss"][0:1], gb_f_row, zrow, zrow], axis=0)
    st_all = _all_gather_small(stats16, "gather_stats")
    st_sum, loss_blk = _sum_stats(st_all, 12)
    loss = loss_blk[0, 0]

    d_ada_all = st_all[:, :n_cond, :].reshape(N_DEV, n_cond * D)
    d_cols = lax.dynamic_slice(d_ada_all, (0, me * ada_loc), (N_DEV, ada_loc))
    d16 = jnp.pad(d_cols, ((0, 16 - N_DEV), (0, 0)))
    g_w_ada = _mm([(c16, d16)], 'tn', F32, "ada_wgrad", silu_a=True)

    small_w = jnp.concatenate([b_ada.reshape(n_cond, D), ln1_g, ln1_b, ln2_g, ln2_b, b_gate.reshape(2, D), zrow,
                               jnp.pad(b_forget, ((0, 0), (0, D - n_heads))), zrow, zrow], axis=0)
    small_m = jnp.concatenate([m_b_ada.reshape(n_cond, D), m_ln1_g, m_ln1_b, m_ln2_g, m_ln2_b, m_b_gate.reshape(2, D),
                               zrow, jnp.pad(m_b_forget, ((0, 0), (0, D - n_heads))), zrow, zrow], axis=0)
    small_v = jnp.concatenate([v_b_ada.reshape(n_cond, D), v_ln1_g, v_ln1_b, v_ln2_g, v_ln2_b, v_b_gate.reshape(2, D),
                               zrow, jnp.pad(v_b_forget, ((0, 0), (0, D - n_heads))), zrow, zrow], axis=0)
    sm = _adamw(small_w, st_sum, small_m, small_v, "adamw_small")

    def small(a, nm):
        if nm == "b_ada":
            return a[0:n_cond].reshape(1, n_cond * D)
        if nm == "b_gate":
            return a[10:12].reshape(1, 2 * D)
        if nm == "b_forget":
            return a[13:14, :n_heads]
        row = {"ln1_g": 6, "ln1_b": 7, "ln2_g": 8, "ln2_b": 9}[nm]
        return a[row:row + 1]

    big = {
        "w_ada": (w_ada[0], g_w_ada, m_w_ada[0], v_w_ada[0]),
        "w_in": (w_in[0], gshard("in", w_in.shape[1:]), m_w_in[0], v_w_in[0]),
        "w_sb_out": (w_sb_out[0], gshard("sb", w_sb_out.shape[1:]), m_w_sb_out[0], v_w_sb_out[0]),
        "w_fox_out": (w_fox_out[0], gshard("fx", w_fox_out.shape[1:]), m_w_fox_out[0], v_w_fox_out[0]),
        "w_o": (w_o[0], gshard("o", w_o.shape[1:]), m_w_o[0], v_w_o[0]),
        "w_ffn_gate": (w_ffn_gate[0], gshard("fg", w_ffn_gate.shape[1:]), m_w_ffn_gate[0], v_w_ffn_gate[0]),
        "w_ffn_up": (w_ffn_up[0], gshard("fu", w_ffn_up.shape[1:]), m_w_ffn_up[0], v_w_ffn_up[0]),
        "w_ffn_down": (w_ffn_down[0], gshard("fd", w_ffn_down.shape[1:]), m_w_ffn_down[0], v_w_ffn_down[0]),
    }
    order = ["w_ada", "b_ada", "w_in", "b_gate", "b_forget", "w_sb_out", "w_fox_out", "w_o", "ln1_g", "ln1_b",
             "w_ffn_gate", "w_ffn_up", "w_ffn_down", "ln2_g", "ln2_b"]
    grads, deltas, new_ms, new_vs = [], [], [], []
    for nm in order:
        if nm in big:
            w, g, m, v = big[nm]
            d, nm_, nv_ = _adamw(w, g, m, v, "adamw_" + nm)
            grads.append(g[None])
            deltas.append(d[None])
            new_ms.append(nm_[None])
            new_vs.append(nv_[None])
        else:
            grads.append(small(st_sum, nm))
            deltas.append(small(sm[0], nm))
            new_ms.append(small(sm[1], nm))
            new_vs.append(small(sm[2], nm))
    return (loss, gx[None], *grads, *deltas, *new_ms, *new_vs)
```

```python
import functools

import jax
import jax.numpy as jnp
import numpy as np
from jax import lax
from jax.experimental import pallas as pl
from jax.experimental.pallas import tpu as pltpu

F32 = jnp.float32
BF16 = jnp.bfloat16

HEAD_DIM = 64
PAIR = 2 * HEAD_DIM
LN_EPS = 1e-5
ALPHA = 2.0 ** 0.25
ADAM_LR, ADAM_B1, ADAM_B2, ADAM_EPS, ADAM_WD, ADAM_STEP = 0.001, 0.9, 0.999, 1e-08, 0.01, 10
N_DEV = 8
VMEM_LIMIT = 56 * 1024 * 1024
MESH = pl.DeviceIdType.MESH


def _dot(a, b, ca=1, cb=0):
    return lax.dot_general(a, b, (((ca,), (cb,)), ((), ())), preferred_element_type=F32)


def _pick(n, cands):
    for t in cands:
        if n % t == 0:
            return t
    return n


def _params(sem):
    return pltpu.CompilerParams(dimension_semantics=sem, vmem_limit_bytes=VMEM_LIMIT)


MM_BLOCK_BYTES = 40 * 1024 * 1024
LANES = 128


def _divisors(n, cap):
    ds = [d for d in range(LANES, min(n, cap) + 1, LANES) if n % d == 0]
    return sorted(ds, reverse=True) or [n]


def _mm_tiles(M, N, a_row_bytes, b_row_bytes, out_itemsize):
    best = None
    for tm in _divisors(M, 1024):
        for tn in _divisors(N, 2048):
            need = 2 * (tm * a_row_bytes + tn * b_row_bytes + tm * tn * out_itemsize) + tm * tn * 4
            if need <= MM_BLOCK_BYTES and (best is None or (tm * tn, tm) > (best[0] * best[1], best[0])):
                best = (tm, tn)
    assert best is not None, (M, N, a_row_bytes, b_row_bytes)
    return best


def _mm(pairs, mode, out_dtype, name, bias=None, act=None, silu_a=False, exchange=None):
    norm = []
    for p in pairs:
        a, b = p[0], p[1]
        kdim_a = a.shape[0] if mode == 'tn' else a.shape[1]
        K, ka, kb = (p[2], p[3], p[4]) if len(p) > 2 else (kdim_a, 0, 0)
        norm.append((a, b, K, ka, kb))
    a0, b0 = norm[0][0], norm[0][1]
    M = a0.shape[1] if mode == 'tn' else a0.shape[0]
    N = b0.shape[0] if mode == 'nt' else b0.shape[1]
    tm, tn = _mm_tiles(M, N, sum(K * a.dtype.itemsize for a, _, K, _, _ in norm),
                       sum(K * b.dtype.itemsize for _, b, K, _, _ in norm), jnp.dtype(out_dtype).itemsize)
    n_pairs = len(norm)

    in_specs, args = [], []
    for a, b, K, ka, kb in norm:
        if mode == 'tn':
            in_specs.append(pl.BlockSpec((K, tm), lambda i, j, ka=ka: (ka, i)))
        else:
            in_specs.append(pl.BlockSpec((tm, K), lambda i, j, ka=ka: (i, ka)))
        if mode == 'nt':
            in_specs.append(pl.BlockSpec((tn, K), lambda i, j, kb=kb: (j, kb)))
        else:
            in_specs.append(pl.BlockSpec((K, tn), lambda i, j, kb=kb: (kb, j)))
        args += [a, b]
    if bias is not None:
        in_specs.append(pl.BlockSpec((1, tn), lambda i, j: (0, j)))
        args.append(bias)

    ca = 0 if mode == 'tn' else 1
    cb = 1 if mode == 'nt' else 0

    def body(*refs):
        o_ref = refs[-1]
        acc = None
        for p in range(n_pairs):
            av = refs[2 * p][...]
            if silu_a:
                av = av / (1.0 + jnp.exp(-av))
            d = _dot(av.astype(BF16), refs[2 * p + 1][...].astype(BF16), ca, cb)
            acc = d if acc is None else acc + d
        if bias is not None:
            acc = acc + refs[2 * n_pairs][...]
        if act == 'sigmoid':
            acc = 1.0 / (1.0 + jnp.exp(-acc))
        o_ref[...] = acc.astype(out_dtype)

    out_spec = pl.BlockSpec((tm, tn), lambda i, j: (i, j))
    out_shape = jax.ShapeDtypeStruct((M, N), out_dtype)
    if exchange is not None:
        return _call_carrying(body, exchange, name, (M // tm, N // tn), in_specs, [out_spec], [out_shape], [], args)
    return pl.pallas_call(
        body, name=name, grid=(M // tm, N // tn), in_specs=in_specs, out_specs=out_spec, out_shape=out_shape,
        compiler_params=_params(("parallel", "parallel")),
    )(*args)


def _rows_call(body, name, row_ins, vec_ins, row_outs, acc_outs, ts):
    S = row_ins[0].shape[0]
    in_specs = [pl.BlockSpec((ts, a.shape[1]), lambda i: (i, 0)) for a in row_ins]
    in_specs += [pl.BlockSpec(a.shape, lambda i: (0, 0)) for a in vec_ins]
    out_specs = [pl.BlockSpec((ts, c), lambda i: (i, 0)) for c, _ in row_outs]
    out_specs += [pl.BlockSpec(s, lambda i: (0, 0)) for s in acc_outs]
    out_shape = [jax.ShapeDtypeStruct((S, c), dt) for c, dt in row_outs]
    out_shape += [jax.ShapeDtypeStruct(s, F32) for s in acc_outs]
    return pl.pallas_call(
        body, name=name, grid=(S // ts,), in_specs=in_specs, out_specs=out_specs, out_shape=out_shape,
        compiler_params=_params(("arbitrary",)),
    )(*row_ins, *vec_ins)


def _ln_stats(v):
    mu = jnp.mean(v, axis=-1, keepdims=True)
    d = v - mu
    var = jnp.mean(d * d, axis=-1, keepdims=True)
    rstd = lax.rsqrt(var + LN_EPS)
    return d * rstd, rstd


def _ln_bwd(dxhat, xhat, rstd):
    m1 = jnp.mean(dxhat, axis=-1, keepdims=True)
    m2 = jnp.mean(dxhat * xhat, axis=-1, keepdims=True)
    return rstd * (dxhat - m1 - xhat * m2)


def _colsum(v):
    return jnp.sum(v, axis=0, keepdims=True)


def _ln_mod(x, ada8, ts):
    D = x.shape[1]

    def body(x_ref, v_ref, u_ref):
        xhat, _ = _ln_stats(x_ref[...])
        u_ref[...] = (xhat * (1.0 + v_ref[1:2, :]) + v_ref[0:1, :]).astype(BF16)

    return _rows_call(body, "ln_mod", [x], [ada8], [(D, BF16)], [], ts)[0]


def _gate_mix_out(g_sb, g_fx, y_sb, y_fx, wo, ts):
    D = y_sb.shape[1]

    def body(gs, gf, ys, yf, w_ref, mi_ref, mix_ref):
        mi = (gs[...] * ys[...] + gf[...] * yf[...]).astype(BF16)
        mi_ref[...] = mi
        mix_ref[...] = _dot(mi, w_ref[...])

    return _rows_call(body, "gate_mix_out", [g_sb, g_fx, y_sb, y_fx], [wo], [(D, BF16), (D, F32)], [], ts)


def _post_attn(x, mix, ada8, lnp8, ts):
    D = x.shape[1]

    def body(x_ref, mix_ref, v_ref, p_ref, x1_ref, u2_ref):
        r1 = ALPHA * x_ref[...] + v_ref[2:3, :] * mix_ref[...]
        xhat, _ = _ln_stats(r1)
        x1 = xhat * p_ref[0:1, :] + p_ref[1:2, :]
        x1_ref[...] = x1
        xh1, _ = _ln_stats(x1)
        u2_ref[...] = (xh1 * (1.0 + v_ref[4:5, :]) + v_ref[3:4, :]).astype(BF16)

    return _rows_call(body, "post_attn", [x, mix], [ada8, lnp8], [(D, F32), (D, BF16)], [], ts)


def _loss_head(x1, hin, wfd, target, ada8, lnp8, ts):
    D = x1.shape[1]

    def body(x1_ref, hin_ref, t_ref, v_ref, p_ref, w_ref, dr2_ref, dh_ref, st_ref):
        @pl.when(pl.program_id(0) == 0)
        def _():
            st_ref[...] = jnp.zeros_like(st_ref)

        hv = _dot(hin_ref[...], w_ref[...])
        g2 = v_ref[5:6, :]
        r2 = ALPHA * x1_ref[...] + g2 * hv
        xhat, rstd = _ln_stats(r2)
        y = xhat * p_ref[2:3, :] + p_ref[3:4, :]
        err = y - t_ref[...]
        dy = err * (1.0 / D)
        dr2 = _ln_bwd(dy * p_ref[2:3, :], xhat, rstd)
        dr2_ref[...] = dr2
        dh_ref[...] = (dr2 * g2).astype(BF16)
        st_ref[0:1, :] += _colsum(err * err) * (0.5 / D)
        st_ref[1:2, :] += _colsum(dy * xhat)
        st_ref[2:3, :] += _colsum(dy)
        st_ref[3:4, :] += _colsum(dr2 * hv)

    return _rows_call(body, "loss_head", [x1, hin, target], [ada8, lnp8, wfd], [(D, F32), (D, BF16)], [(8, D)], ts)


def _mid_bwd(du2, x1, dr2, mix, x, ada8, lnp8, ts):
    D = x.shape[1]

    def body(du2_ref, x1_ref, dr2_ref, mix_ref, x_ref, v_ref, p_ref, dr1_ref, dmix_ref, st_ref):
        @pl.when(pl.program_id(0) == 0)
        def _():
            st_ref[...] = jnp.zeros_like(st_ref)

        du2v = du2_ref[...]
        xh1, rstd1 = _ln_stats(x1_ref[...])
        dx1 = ALPHA * dr2_ref[...] + _ln_bwd(du2v * (1.0 + v_ref[4:5, :]), xh1, rstd1)
        mixv = mix_ref[...]
        g1 = v_ref[2:3, :]
        r1 = ALPHA * x_ref[...] + g1 * mixv
        xhr, rstdr = _ln_stats(r1)
        dr1 = _ln_bwd(dx1 * p_ref[0:1, :], xhr, rstdr)
        dr1_ref[...] = dr1
        dmix_ref[...] = (dr1 * g1).astype(BF16)
        st_ref[0:1, :] += _colsum(du2v * xh1)
        st_ref[1:2, :] += _colsum(du2v)
        st_ref[2:3, :] += _colsum(dx1 * xhr)
        st_ref[3:4, :] += _colsum(dx1)
        st_ref[4:5, :] += _colsum(dr1 * mixv)

    return _rows_call(body, "mid_bwd", [du2, x1, dr2, mix, x], [ada8, lnp8], [(D, F32), (D, BF16)], [(8, D)], ts)


def _first_bwd(du1, x, dr1, ada8, ts):
    D = x.shape[1]

    def body(du1_ref, x_ref, dr1_ref, v_ref, gx_ref, st_ref):
        @pl.when(pl.program_id(0) == 0)
        def _():
            st_ref[...] = jnp.zeros_like(st_ref)

        du1v = du1_ref[...]
        xh0, rstd0 = _ln_stats(x_ref[...])
        gx_ref[...] = ALPHA * dr1_ref[...] + _ln_bwd(du1v * (1.0 + v_ref[1:2, :]), xh0, rstd0)
        st_ref[0:1, :] += _colsum(du1v * xh0)
        st_ref[1:2, :] += _colsum(du1v)

    return _rows_call(body, "first_bwd", [du1, x, dr1], [ada8], [(D, F32)], [(8, D)], ts)


def _split3(v):
    hi = v.astype(BF16)
    r = v - hi.astype(F32)
    mid = r.astype(BF16)
    lo = (r - mid.astype(F32)).astype(BF16)
    return hi, mid, lo


def _fgate_fwd(f, bf_pad, tb):
    S = f.shape[0]

    def body(f_ref, b_ref, fc_ref, carry):
        @pl.when(pl.program_id(0) == 0)
        def _():
            carry[...] = jnp.zeros_like(carry)

        z = f_ref[...] + b_ref[...]
        ls = jnp.minimum(z, 0.0) - jnp.log(1.0 + jnp.exp(-jnp.abs(z)))
        r = lax.broadcasted_iota(jnp.int32, (tb, tb), 0)
        c = lax.broadcasted_iota(jnp.int32, (tb, tb), 1)
        tri = (c <= r).astype(BF16)
        hi, mid, lo = _split3(ls)
        cs = _dot(tri, hi) + _dot(tri, mid) + _dot(tri, lo) + carry[...]
        fc_ref[...] = cs
        carry[...] = cs[tb - 1:tb, :]

    return pl.pallas_call(
        body, name="fgate_fwd", grid=(S // tb,),
        in_specs=[pl.BlockSpec((tb, 128), lambda i: (i, 0)), pl.BlockSpec((1, 128), lambda i: (0, 0))],
        out_specs=pl.BlockSpec((tb, 128), lambda i: (i, 0)),
        out_shape=jax.ShapeDtypeStruct((S, 128), F32),
        scratch_shapes=[pltpu.VMEM((1, 128), F32)],
        compiler_params=_params(("arbitrary",)),
    )(f, bf_pad)


def _fgate_bwd(dfc, f, bf_pad, tb):
    S = f.shape[0]
    nb = S // tb

    def body(d_ref, f_ref, b_ref, df_ref, gb_ref, carry):
        @pl.when(pl.program_id(0) == 0)
        def _():
            carry[...] = jnp.zeros_like(carry)
            gb_ref[...] = jnp.zeros_like(gb_ref)

        r = lax.broadcasted_iota(jnp.int32, (tb, tb), 0)
        c = lax.broadcasted_iota(jnp.int32, (tb, tb), 1)
        tri = (c >= r).astype(BF16)
        hi, mid, lo = _split3(d_ref[...])
        rs = _dot(tri, hi) + _dot(tri, mid) + _dot(tri, lo) + carry[...]
        carry[...] = rs[0:1, :]
        z = f_ref[...] + b_ref[...]
        df = rs * (1.0 / (1.0 + jnp.exp(z)))
        df_ref[...] = df
        gb_ref[0:1, :] += _colsum(df)

    return pl.pallas_call(
        body, name="fgate_bwd", grid=(nb,),
        in_specs=[pl.BlockSpec((tb, 128), lambda i: (nb - 1 - i, 0)),
                  pl.BlockSpec((tb, 128), lambda i: (nb - 1 - i, 0)),
                  pl.BlockSpec((1, 128), lambda i: (0, 0))],
        out_specs=[pl.BlockSpec((tb, 128), lambda i: (nb - 1 - i, 0)), pl.BlockSpec((8, 128), lambda i: (0, 0))],
        out_shape=[jax.ShapeDtypeStruct((S, 128), F32), jax.ShapeDtypeStruct((8, 128), F32)],
        scratch_shapes=[pltpu.VMEM((1, 128), F32)],
        compiler_params=_params(("arbitrary",)),
    )(dfc, f, bf_pad)


def _split2(v):
    hi = v.astype(BF16)
    lo = (v - hi.astype(F32)).astype(BF16)
    return hi, lo


def _head_masks():
    lane = lax.broadcasted_iota(jnp.int32, (1, PAIR), 1)
    m0 = lane < HEAD_DIM
    return m0, jnp.logical_not(m0)


def _sel(mask, v):
    return jnp.where(mask, v, jnp.zeros_like(v))


def _softplus(z):
    return jnp.maximum(z, 0.0) + jnp.log(1.0 + jnp.exp(-jnp.abs(z)))


def _qkv_specs(S, tq, n_pairs, base):
    return [pl.BlockSpec((tq, PAIR), lambda p, i: (i, base + p)),
            pl.BlockSpec((S, PAIR), lambda p, i: (0, base + n_pairs + p)),
            pl.BlockSpec((S, PAIR), lambda p, i: (0, base + 2 * n_pairs + p))]


def _sb_fwd(qkv, n_pairs, base, tq):
    S = qkv.shape[0]
    tk = tq
    scale = HEAD_DIM ** -0.5

    def body(q_ref, k_ref, v_ref, o_ref, t_ref, acc_ref):
        i = pl.program_id(1)
        masks = _head_masks()
        q2 = q_ref[...]
        qm = [_sel(m, q2) for m in masks]
        rowpos = i * tq + lax.broadcasted_iota(jnp.int32, (tq, tk), 0)
        colin = lax.broadcasted_iota(jnp.int32, (tq, tk), 1)
        upper = (lax.broadcasted_iota(jnp.int32, (tk, tk), 0) > lax.broadcasted_iota(jnp.int32, (tk, tk), 1)).astype(BF16)
        acc_ref[...] = jnp.zeros_like(acc_ref)

        def step(jj, carry):
            j = i - jj
            off = pl.multiple_of(j * tk, tk)
            k2 = k_ref[pl.ds(off, tk), :]
            v2 = v_ref[pl.ds(off, tk), :]
            mask = (j * tk + colin) < rowpos
            out = None
            new = []
            for h in heads:
                z = _dot(qm[h], k2, 1, 1) * scale
                sp = _softplus(z)
                lg = jnp.where(mask, -sp, 0.0)
                hi, lo = _split2(lg)
                suf = _dot(hi, upper) + _dot(lo, upper)
                a = jnp.where(mask, jnp.exp(z - sp + suf + carry[h]), 0.0)
                d = _dot(a.astype(BF16), _sel(masks[h], v2))
                out = d if out is None else out + d
                new.append(carry[h] + jnp.sum(lg, axis=-1, keepdims=True))
            acc_ref[...] += out
            return tuple(new)

        zero = jnp.zeros((tq, 1), F32)
        r0, r1 = lax.fori_loop(0, i + 1, step, (zero, zero))
        o_ref[...] = acc_ref[...].astype(BF16)
        t_ref[...] = jnp.where(masks[0], r0, r1)

    W = n_pairs * PAIR
    return pl.pallas_call(
        body, name="sb_fwd", grid=(n_pairs, S // tq), in_specs=_qkv_specs(S, tq, n_pairs, base),
        out_specs=[pl.BlockSpec((tq, PAIR), lambda p, i: (i, p)), pl.BlockSpec((tq, PAIR), lambda p, i: (i, p))],
        out_shape=[jax.ShapeDtypeStruct((S, W), BF16), jax.ShapeDtypeStruct((S, W), F32)],
        scratch_shapes=[pltpu.VMEM((tq, PAIR), F32)],
        compiler_params=_params(("parallel", "arbitrary")),
    )(qkv, qkv, qkv)


def _sb_bwd(qkv, tot, do, n_pairs, base, tq):
    S = qkv.shape[0]
    tk = tq
    nq = S // tq
    scale = HEAD_DIM ** -0.5

    def body(q_ref, k_ref, v_ref, t_ref, do_ref, dq_ref, dk_ref, dv_ref, dq_acc, dk_acc, dv_acc):
        i = pl.program_id(1)
        masks = _head_masks()

        @pl.when(i == 0)
        def _():
            dk_acc[...] = jnp.zeros_like(dk_acc)
            dv_acc[...] = jnp.zeros_like(dv_acc)

        q2 = q_ref[...]
        do2 = do_ref[...]
        qm = [_sel(m, q2) for m in masks]
        dom = [_sel(m, do2) for m in masks]
        t2 = t_ref[...]
        tot_h = [t2[:, 0:1], t2[:, HEAD_DIM:HEAD_DIM + 1]]
        rowpos = i * tq + lax.broadcasted_iota(jnp.int32, (tq, tk), 0)
        colin = lax.broadcasted_iota(jnp.int32, (tq, tk), 1)
        r_i = lax.broadcasted_iota(jnp.int32, (tk, tk), 0)
        c_i = lax.broadcasted_iota(jnp.int32, (tk, tk), 1)
        upper = (r_i > c_i).astype(BF16)
        lower = (r_i < c_i).astype(BF16)
        dq_acc[...] = jnp.zeros_like(dq_acc)

        def step(j, carry):
            off = pl.multiple_of(j * tk, tk)
            k2 = k_ref[pl.ds(off, tk), :]
            v2 = v_ref[pl.ds(off, tk), :]
            mask = (j * tk + colin) < rowpos
            dq = None
            dk = None
            dv = None
            new = []
            for h in heads:
                cum_l, cum_g = carry[2 * h], carry[2 * h + 1]
                z = _dot(qm[h], k2, 1, 1) * scale
                sp = _softplus(z)
                lg = jnp.where(mask, -sp, 0.0)
                hi, lo = _split2(lg)
                suf = _dot(hi, upper) + _dot(lo, upper)
                row_l = jnp.sum(lg, axis=-1, keepdims=True)
                later = tot_h[h] - cum_l - row_l
                a = jnp.where(mask, jnp.exp(z - sp + suf + later), 0.0)
                da = _dot(dom[h], v2, 1, 1)
                g = da * a
                ghi, glo = _split2(g)
                pre = _dot(ghi, lower) + _dot(glo, lower) + cum_g
                one_m_beta = jnp.exp(-sp)
                dz = jnp.where(mask, g * one_m_beta - (1.0 - one_m_beta) * pre, 0.0)
                dzb = (dz * scale).astype(BF16)
                d1 = _dot(dzb, _sel(masks[h], k2))
                d2 = _dot(dzb, qm[h], 0, 0)
                d3 = _dot(a.astype(BF16), dom[h], 0, 0)
                dq = d1 if dq is None else dq + d1
                dk = d2 if dk is None else dk + d2
                dv = d3 if dv is None else dv + d3
                new += [cum_l + row_l, cum_g + jnp.sum(g, axis=-1, keepdims=True)]
            dq_acc[...] += dq
            dk_acc[pl.ds(off, tk), :] += dk
            dv_acc[pl.ds(off, tk), :] += dv
            return tuple(new)

        zero = jnp.zeros((tq, 1), F32)
        lax.fori_loop(0, i + 1, step, (zero, zero, zero, zero))
        dq_ref[...] = dq_acc[...].astype(BF16)

        @pl.when(i == nq - 1)
        def _():
            dk_ref[...] = dk_acc[...].astype(BF16)
            dv_ref[...] = dv_acc[...].astype(BF16)

    W = n_pairs * PAIR
    in_specs = _qkv_specs(S, tq, n_pairs, base) + [
        pl.BlockSpec((tq, PAIR), lambda p, i: (i, p)),
        pl.BlockSpec((tq, PAIR), lambda p, i: (i, p))]
    out_specs = [pl.BlockSpec((tq, PAIR), lambda p, i: (i, p)),
                 pl.BlockSpec((S, PAIR), lambda p, i: (0, p)),
                 pl.BlockSpec((S, PAIR), lambda p, i: (0, p))]
    dq, dk, dv = pl.pallas_call(
        body, name="sb_bwd", grid=(n_pairs, nq), in_specs=in_specs, out_specs=out_specs,
        out_shape=[jax.ShapeDtypeStruct((S, W), BF16)] * 3,
        scratch_shapes=[pltpu.VMEM((tq, PAIR), F32), pltpu.VMEM((S, PAIR), F32), pltpu.VMEM((S, PAIR), F32)],
        compiler_params=_params(("parallel", "arbitrary")),
    )(qkv, qkv, qkv, tot, do)
    return dq, dk, dv


NEG = -1e30


def _fox_specs(S, tq, n_pairs, base):
    return _qkv_specs(S, tq, n_pairs, base) + [
        pl.BlockSpec((tq, PAIR), lambda p, i: (i, p)),
        pl.BlockSpec((1, S // tq, 8, tq), lambda p, i: (p, 0, 0, 0))]


def _fox_fwd(qkv, fcx, fcr, n_pairs, base, tq):
    S = qkv.shape[0]
    tk = tq
    scale = HEAD_DIM ** -0.5

    def body(q_ref, k_ref, v_ref, fq_ref, fk_ref, o_ref, lse_ref, acc_ref):
        i = pl.program_id(1)
        masks = _head_masks()
        q2 = q_ref[...]
        qm = [_sel(m, q2) for m in masks]
        fq2 = fq_ref[...]
        fq = [fq2[:, 0:1], fq2[:, HEAD_DIM:HEAD_DIM + 1]]
        rowpos = i * tq + lax.broadcasted_iota(jnp.int32, (tq, tk), 0)
        colin = lax.broadcasted_iota(jnp.int32, (tq, tk), 1)
        acc_ref[...] = jnp.zeros_like(acc_ref)

        def step(j, carry):
            off = pl.multiple_of(j * tk, tk)
            k2 = k_ref[pl.ds(off, tk), :]
            v2 = v_ref[pl.ds(off, tk), :]
            fk2 = fk_ref[0, j]
            mask = (j * tk + colin) <= rowpos
            out = None
            new = []
            alphas = []
            for h in heads:
                m_old, l_old = carry[2 * h], carry[2 * h + 1]
                s = _dot(qm[h], k2, 1, 1) * scale + fq[h] - fk2[h:h + 1, :]
                s = jnp.where(mask, s, NEG)
                m_new = jnp.maximum(m_old, jnp.max(s, axis=-1, keepdims=True))
                p = jnp.exp(s - m_new)
                alpha = jnp.exp(m_old - m_new)
                alphas.append(alpha)
                d = _dot(p.astype(BF16), _sel(masks[h], v2))
                out = d if out is None else out + d
                new += [m_new, alpha * l_old + jnp.sum(p, axis=-1, keepdims=True)]
            acc_ref[...] = acc_ref[...] * jnp.where(masks[0], alphas[0], alphas[1]) + out
            return tuple(new)

        zero = jnp.zeros((tq, 1), F32)
        neg = jnp.full((tq, 1), NEG, F32)
        m0, l0, m1, l1 = lax.fori_loop(0, i + 1, step, (neg, zero, neg, zero))
        o_ref[...] = (acc_ref[...] / jnp.where(masks[0], l0, l1)).astype(BF16)
        lse_ref[...] = jnp.where(masks[0], m0 + jnp.log(l0), m1 + jnp.log(l1))

    W = n_pairs * PAIR
    return pl.pallas_call(
        body, name="fox_fwd", grid=(n_pairs, S // tq), in_specs=_fox_specs(S, tq, n_pairs, base),
        out_specs=[pl.BlockSpec((tq, PAIR), lambda p, i: (i, p)), pl.BlockSpec((tq, PAIR), lambda p, i: (i, p))],
        out_shape=[jax.ShapeDtypeStruct((S, W), BF16), jax.ShapeDtypeStruct((S, W), F32)],
        scratch_shapes=[pltpu.VMEM((tq, PAIR), F32)],
        compiler_params=_params(("parallel", "arbitrary")),
    )(qkv, qkv, qkv, fcx, fcr)


def _fox_bwd(qkv, fcx, fcr, o, lse, do, n_pairs, base, tq):
    S = qkv.shape[0]
    tk = tq
    nq = S // tq
    scale = HEAD_DIM ** -0.5

    def body(q_ref, k_ref, v_ref, fq_ref, fk_ref, o_ref, lse_ref, do_ref,
             dq_ref, dk_ref, dv_ref, dfq_ref, dfk_ref, dq_acc, dk_acc, dv_acc):
        i = pl.program_id(1)
        masks = _head_masks()

        @pl.when(i == 0)
        def _():
            dk_acc[...] = jnp.zeros_like(dk_acc)
            dv_acc[...] = jnp.zeros_like(dv_acc)
            dfk_ref[...] = jnp.zeros_like(dfk_ref)

        q2 = q_ref[...]
        do2 = do_ref[...]
        qm = [_sel(m, q2) for m in masks]
        dom = [_sel(m, do2) for m in masks]
        fq2 = fq_ref[...]
        fq = [fq2[:, 0:1], fq2[:, HEAD_DIM:HEAD_DIM + 1]]
        l2 = lse_ref[...]
        lse_h = [l2[:, 0:1], l2[:, HEAD_DIM:HEAD_DIM + 1]]
        prod = do2.astype(F32) * o_ref[...].astype(F32)
        delta = [jnp.sum(jnp.where(m, prod, 0.0), axis=-1, keepdims=True) for m in masks]
        rowpos = i * tq + lax.broadcasted_iota(jnp.int32, (tq, tk), 0)
        colin = lax.broadcasted_iota(jnp.int32, (tq, tk), 1)
        dq_acc[...] = jnp.zeros_like(dq_acc)

        def step(j, carry):
            off = pl.multiple_of(j * tk, tk)
            k2 = k_ref[pl.ds(off, tk), :]
            v2 = v_ref[pl.ds(off, tk), :]
            fk2 = fk_ref[0, j]
            mask = (j * tk + colin) <= rowpos
            dq = None
            dk = None
            dv = None
            new = []
            dfk_rows = []
            for h in heads:
                s = _dot(qm[h], k2, 1, 1) * scale + fq[h] - fk2[h:h + 1, :]
                p = jnp.where(mask, jnp.exp(s - lse_h[h]), 0.0)
                dp = _dot(dom[h], v2, 1, 1)
                ds = p * (dp - delta[h])
                dsb = (ds * scale).astype(BF16)
                d1 = _dot(dsb, _sel(masks[h], k2))
                d2 = _dot(dsb, qm[h], 0, 0)
                d3 = _dot(p.astype(BF16), dom[h], 0, 0)
                dq = d1 if dq is None else dq + d1
                dk = d2 if dk is None else dk + d2
                dv = d3 if dv is None else dv + d3
                new.append(carry[h] + jnp.sum(ds, axis=-1, keepdims=True))
                dfk_rows.append(jnp.sum(ds, axis=0, keepdims=True))
            dq_acc[...] += dq
            dk_acc[pl.ds(off, tk), :] += dk
            dv_acc[pl.ds(off, tk), :] += dv
            dfk_ref[0, j, 0:1, :] += dfk_rows[0]
            dfk_ref[0, j, 1:2, :] += dfk_rows[1]
            return tuple(new)

        zero = jnp.zeros((tq, 1), F32)
        r0, r1 = lax.fori_loop(0, i + 1, step, (zero, zero))
        dq_ref[...] = dq_acc[...].astype(BF16)
        dfq_ref[...] = jnp.where(masks[0], r0, r1)

        @pl.when(i == nq - 1)
        def _():
            dk_ref[...] = dk_acc[...].astype(BF16)
            dv_ref[...] = dv_acc[...].astype(BF16)

    W = n_pairs * PAIR
    in_specs = _fox_specs(S, tq, n_pairs, base) + [
        pl.BlockSpec((tq, PAIR), lambda p, i: (i, p)),
        pl.BlockSpec((tq, PAIR), lambda p, i: (i, p)),
        pl.BlockSpec((tq, PAIR), lambda p, i: (i, p))]
    out_specs = [pl.BlockSpec((tq, PAIR), lambda p, i: (i, p)),
                 pl.BlockSpec((S, PAIR), lambda p, i: (0, p)),
                 pl.BlockSpec((S, PAIR), lambda p, i: (0, p)),
                 pl.BlockSpec((tq, PAIR), lambda p, i: (i, p)),
                 pl.BlockSpec((1, nq, 8, tk), lambda p, i: (p, 0, 0, 0))]
    return pl.pallas_call(
        body, name="fox_bwd", grid=(n_pairs, nq), in_specs=in_specs, out_specs=out_specs,
        out_shape=[jax.ShapeDtypeStruct((S, W), BF16)] * 3
        + [jax.ShapeDtypeStruct((S, W), F32), jax.ShapeDtypeStruct((n_pairs, nq, 8, tk), F32)],
        scratch_shapes=[pltpu.VMEM((tq, PAIR), F32), pltpu.VMEM((S, PAIR), F32), pltpu.VMEM((S, PAIR), F32)],
        compiler_params=_params(("parallel", "arbitrary")),
    )(qkv, qkv, qkv, fcx, fcr, o, lse, do)


RC = 32
VANISH = -104.0


def _chunks(n_rows, fn):
    for ci in range(n_rows // RC):
        fn(ci * RC)


def _wide(v, tk):
    return v if tk == 128 else jnp.tile(v, (1, tk // 128))


def _rep(col):
    return jnp.broadcast_to(col, (col.shape[0], 128))


def _per_head(blk, masks):
    sw = pltpu.roll(blk, HEAD_DIM, axis=1)
    return jnp.where(masks[0], blk, sw), jnp.where(masks[0], sw, blk)


def _fill_masked(dst_ref, src_ref, masks, mul=None, ones_lane=None):
    v = src_ref[...]
    if mul is not None:
        v = v * mul
    lane = lax.broadcasted_iota(jnp.int32, (1, PAIR), 1)
    for h in range(2):
        m = _sel(masks[h], v)
        if ones_lane is not None:
            m = jnp.where(lane == ones_lane[h], jnp.ones_like(m), m)
        dst_ref[h] = m


def _head_norms(v, masks):
    sq = v.astype(F32)
    sq = sq * sq
    return [jnp.sqrt(jnp.sum(jnp.where(m, sq, 0.0), axis=-1, keepdims=True)) for m in masks]


def _largest_key_norm(kmax_ref, k_ref, masks):
    for h, n in enumerate(_head_norms(k_ref[...], masks)):
        kmax_ref[h] = jnp.broadcast_to(jnp.max(n, axis=0, keepdims=True), (8, 128))


def _score_bound(q_scaled, kmax_ref, masks):
    return [_rep(n) * kmax_ref[h][0:1, :] for h, n in enumerate(_head_norms(q_scaled, masks))]


def _tri(tk, cmp):
    r = lax.broadcasted_iota(jnp.int32, (tk, tk), 0)
    c = lax.broadcasted_iota(jnp.int32, (tk, tk), 1)
    return cmp(r, c).astype(BF16)


def _diag_mask(r0, tk, strict):
    row = r0 + lax.broadcasted_iota(jnp.int32, (RC, tk), 0)
    col = lax.broadcasted_iota(jnp.int32, (RC, tk), 1)
    return (col < row) if strict else (col <= row)


def _peer_copies(src_ref, land_ref, send_sems, recv_sems, scatter, receive_side):
    x, y, c = lax.axis_index("x"), lax.axis_index("y"), lax.axis_index("c")
    me = 4 * x + 2 * y + c
    copies = []
    for k in range(1, N_DEV):
        px, py, pc = (1 - x if k & 4 else x), (1 - y if k & 2 else y), (1 - c if k & 1 else c)
        slot = 4 * px + 2 * py + pc
        copies.append(pltpu.make_async_remote_copy(
            src_ref=src_ref.at[slot] if scatter else src_ref,
            dst_ref=land_ref.at[slot] if receive_side else land_ref.at[me],
            send_sem=send_sems.at[k - 1], recv_sem=recv_sems.at[k - 1], device_id=(px, py, pc), device_id_type=MESH))
    return copies


def _call_carrying(body, exchange, name, grid, in_specs, out_specs, out_shape, scratch_shapes, args):
    if exchange is None:
        return pl.pallas_call(body, name=name, grid=grid, in_specs=in_specs, out_specs=out_specs, out_shape=out_shape,
                              scratch_shapes=scratch_shapes, compiler_params=_params(("parallel", "arbitrary")))(*args)
    exchanges = [exchange] if isinstance(exchange, tuple) else list(exchange)
    n_in, n_out, n_ex = len(in_specs), len(out_specs), len(exchanges)

    def carrying(*refs):
        srcs = refs[n_in:n_in + n_ex]
        lands = refs[n_in + n_ex + n_out:n_in + 2 * n_ex + n_out]
        sems = refs[len(refs) - 2 * n_ex:]

        def copies(receive_side):
            return [cp for e, (_, scatter) in enumerate(exchanges)
                    for cp in _peer_copies(srcs[e], lands[e], sems[2 * e], sems[2 * e + 1], scatter, receive_side)]

        first = jnp.logical_and(pl.program_id(0) == 0, pl.program_id(1) == 0)
        last = jnp.logical_and(pl.program_id(0) == grid[0] - 1, pl.program_id(1) == grid[1] - 1)

        @pl.when(first)
        def _():
            for cp in copies(False):
                cp.start()

        body(*refs[:n_in], *refs[n_in + n_ex:n_in + n_ex + n_out], *refs[n_in + 2 * n_ex + n_out:len(refs) - 2 * n_ex])

        @pl.when(last)
        def _():
            for cp in copies(True):
                cp.wait_send()
                cp.wait_recv()

    any_space = pl.BlockSpec(memory_space=pl.ANY)
    lands = [jax.ShapeDtypeStruct((N_DEV,) + src.shape[-2:], src.dtype) for src, _ in exchanges]
    return pl.pallas_call(
        carrying, name=name, grid=grid, in_specs=list(in_specs) + [any_space] * n_ex,
        out_specs=list(out_specs) + [any_space] * n_ex, out_shape=list(out_shape) + lands,
        scratch_shapes=list(scratch_shapes) + [pltpu.SemaphoreType.DMA((N_DEV - 1,))] * (2 * n_ex),
        compiler_params=_params(("arbitrary", "arbitrary")))(*args, *[src for src, _ in exchanges])


def _staggered(bodies):
    active, waiting = [], list(bodies)
    while waiting or active:
        if waiting:
            active.append(waiting.pop(0))
        for g in list(active):
            try:
                next(g)
            except StopIteration:
                active.remove(g)


def _streams(tile, j, diag, slot):
    return [tile(j, diag, slot, (0, 1))]


def _tiles(i, tile):
    def step(jj, carry):
        _staggered(_streams(tile, 2 * jj, False, 0) + _streams(tile, 2 * jj + 1, False, 1))
        return carry
    lax.fori_loop(0, (i - 1) // 2, step, 0)

    @pl.when(jnp.logical_and(i >= 1, (i - 1) % 2 == 1))
    def _():
        _staggered(_streams(tile, i - 2, False, 0))

    @pl.when(i >= 1)
    def _():
        _staggered(_streams(tile, i - 1, False, 0) + _streams(tile, i, True, 1))

    @pl.when(i == 0)
    def _():
        _staggered(_streams(tile, i, True, 1))


def _tiles_reversed(i, tile, keep_going):
    @pl.when(i == 0)
    def _():
        _staggered(_streams(tile, i, True, 0))

    @pl.when(i >= 1)
    def _():
        _staggered(_streams(tile, i, True, 0) + _streams(tile, i - 1, False, 1))

    pairs = (i - 1) // 2

    def cond(carry):
        jj, go = carry
        return jnp.logical_and(jj < pairs, go)

    def step(carry):
        jj, _ = carry
        _staggered(_streams(tile, i - 2 - 2 * jj, False, 0) + _streams(tile, i - 3 - 2 * jj, False, 1))
        return jj + 1, keep_going(jnp.maximum(i - 4 - 2 * jj, 0))

    jj, go = lax.while_loop(cond, step, (jnp.int32(0), keep_going(jnp.maximum(i - 2, 0))))

    @pl.when(jnp.logical_and(jnp.logical_and(i >= 1, (i - 1) % 2 == 1), jnp.logical_and(jj == pairs, go)))
    def _():
        _staggered(_streams(tile, 0, False, 0))


def _sb_fwd2(qkv, n_pairs, base, tq, exchange=None):
    S = qkv.shape[0]
    tk = tq
    scale = HEAD_DIM ** -0.5

    def body(q_ref, k_ref, v_ref, o_ref, t_ref, z_ref, hi_ref, suf_ref, p_ref, r_ref, acc_ref, vm_ref):
        i = pl.program_id(1)
        masks = _head_masks()

        @pl.when(i == 0)
        def _():
            _fill_masked(vm_ref, v_ref, masks)

        q2 = q_ref[...] * scale
        qm = [_sel(m, q2) for m in masks]
        incl = _tri(tk, lambda r, c: r >= c)
        r_ref[...] = jnp.zeros_like(r_ref)
        acc_ref[...] = jnp.zeros_like(acc_ref)

        def tile(j, diag, slot, heads):
            off = pl.multiple_of(j * tk, tk)
            k2 = k_ref[pl.ds(off, tk), :]
            v2 = v_ref[pl.ds(off, tk), :]
            for h in heads:
                z_ref[2 * slot + h] = _dot(qm[h], k2, 1, 1)
            yield
            for h in heads:
                def split(r0, h=h):
                    rows = pl.ds(r0, RC)
                    lg = -_softplus(z_ref[2 * slot + h, rows, :])
                    if diag:
                        lg = jnp.where(_diag_mask(r0, tk, True), lg, 0.0)
                    hi_ref[2 * slot + h, rows, :] = lg.astype(BF16)
                _chunks(tq, split)
            yield
            for h in heads:
                suf_ref[2 * slot + h] = _dot(hi_ref[2 * slot + h], incl)
            yield
            for h in heads:
                def weights(r0, h=h):
                    rows = pl.ds(r0, RC)
                    a = jnp.exp(z_ref[2 * slot + h, rows, :] + suf_ref[2 * slot + h, rows, :] + _wide(r_ref[h, rows, :], tk))
                    if diag:
                        a = jnp.where(_diag_mask(r0, tk, True), a, 0.0)
                    p_ref[2 * slot + h, rows, :] = a.astype(BF16)
                _chunks(tq, weights)
            yield
            keys = pl.ds(off, tk)
            for h in heads:
                acc_ref[...] += _dot(p_ref[2 * slot + h], vm_ref[h, keys, :])
            for h in heads:
                r_ref[h] += _rep(suf_ref[2 * slot + h, :, 0:1])

        _tiles_reversed(i, tile, lambda nearest: jnp.max(r_ref[...]) >= VANISH)
        o_ref[...] = acc_ref[...].astype(BF16)
        t_ref[...] = acc_ref[...]

    W = n_pairs * PAIR
    return _call_carrying(
        body, exchange, "sb_fwd", (n_pairs, S // tq), _qkv_specs(S, tq, n_pairs, base),
        [pl.BlockSpec((tq, PAIR), lambda p, i: (i, p)), pl.BlockSpec((tq, PAIR), lambda p, i: (i, p))],
        [jax.ShapeDtypeStruct((S, W), BF16), jax.ShapeDtypeStruct((S, W), F32)],
        [pltpu.VMEM((4, tq, tk), F32), pltpu.VMEM((4, tq, tk), BF16),
         pltpu.VMEM((4, tq, tk), F32), pltpu.VMEM((4, tq, tk), BF16), pltpu.VMEM((2, tq, 128), F32),
         pltpu.VMEM((tq, PAIR), F32), pltpu.VMEM((2, S, PAIR), BF16)],
        (qkv, qkv, qkv))


def _sb_bwd2(qkv, o32, do, n_pairs, base, tq, exchange=None):
    S = qkv.shape[0]
    tk = tq
    nq = S // tq
    scale = HEAD_DIM ** -0.5

    def body(q_ref, k_ref, v_ref, o_ref, do_ref, dq_ref, dk_ref, dv_ref,
             z_ref, g_ref, omb_ref, cum_ref, hi_ref, lo_ref, a_ref, dz_ref,
             r_ref, cg_ref, dl_ref, dq_acc, dk_acc, dv_acc, ks_ref):
        i = pl.program_id(1)
        masks = _head_masks()

        @pl.when(i == 0)
        def _():
            dk_acc[...] = jnp.zeros_like(dk_acc)
            dv_acc[...] = jnp.zeros_like(dv_acc)
            _fill_masked(ks_ref, k_ref, masks, mul=scale)

        q2 = q_ref[...] * scale
        do2 = do_ref[...]
        qm = [_sel(m, q2) for m in masks]
        dom = [_sel(m, do2) for m in masks]
        prod = do2.astype(F32) * o_ref[...]
        for h in range(2):
            dl_ref[h] = _rep(jnp.sum(jnp.where(masks[h], prod, 0.0), axis=-1, keepdims=True))
        suffix = _tri(tk, lambda r, c: r >= c)
        r_ref[...] = jnp.zeros_like(r_ref)
        cg_ref[...] = jnp.zeros_like(cg_ref)
        dq_acc[...] = jnp.zeros_like(dq_acc)

        def tile(j, diag, slot, heads):
            off = pl.multiple_of(j * tk, tk)
            k2 = k_ref[pl.ds(off, tk), :]
            v2 = v_ref[pl.ds(off, tk), :]
            for h in heads:
                z_ref[2 * slot + h] = _dot(qm[h], k2, 1, 1)
                g_ref[2 * slot + h] = _dot(dom[h], v2, 1, 1)
            yield
            for h in heads:
                def split(r0, h=h):
                    rows = pl.ds(r0, RC)
                    sp = _softplus(z_ref[2 * slot + h, rows, :])
                    omb_ref[2 * slot + h, rows, :] = jnp.exp(-sp)
                    lg = -sp
                    if diag:
                        lg = jnp.where(_diag_mask(r0, tk, True), lg, 0.0)
                    hi_ref[2 * slot + h, rows, :] = lg.astype(BF16)
                _chunks(tq, split)
            yield
            for h in heads:
                cum_ref[2 * slot + h] = _dot(hi_ref[2 * slot + h], suffix)
            yield
            for h in heads:
                def weights(r0, h=h):
                    rows = pl.ds(r0, RC)
                    a = jnp.exp(z_ref[2 * slot + h, rows, :] + cum_ref[2 * slot + h, rows, :] + _wide(r_ref[h, rows, :], tk))
                    if diag:
                        a = jnp.where(_diag_mask(r0, tk, True), a, 0.0)
                    ab = a.astype(BF16)
                    g = g_ref[2 * slot + h, rows, :] * ab.astype(F32)
                    g_ref[2 * slot + h, rows, :] = g
                    a_ref[2 * slot + h, rows, :] = ab
                    hi, lo = _split2(g)
                    hi_ref[2 * slot + h, rows, :] = hi
                    lo_ref[2 * slot + h, rows, :] = lo
                _chunks(tq, weights)
            for h in heads:
                r_ref[h] += _rep(cum_ref[2 * slot + h, :, 0:1])
            yield
            for h in heads:
                cum_ref[2 * slot + h] = _dot(hi_ref[2 * slot + h], suffix) + _dot(lo_ref[2 * slot + h], suffix)
            yield
            for h in heads:
                def dscore(r0, h=h):
                    rows = pl.ds(r0, RC)
                    g = g_ref[2 * slot + h, rows, :]
                    from_here = cum_ref[2 * slot + h, rows, :] + _wide(cg_ref[h, rows, :], tk)
                    before = _wide(dl_ref[h, rows, :], tk) - from_here
                    omb = omb_ref[2 * slot + h, rows, :]
                    dz = g * omb - (1.0 - omb) * before
                    if diag:
                        dz = jnp.where(_diag_mask(r0, tk, True), dz, 0.0)
                    dz_ref[2 * slot + h, rows, :] = dz.astype(BF16)
                _chunks(tq, dscore)
            for h in heads:
                cg_ref[h] += _rep(cum_ref[2 * slot + h, :, 0:1])
            yield
            keys = pl.ds(off, tk)
            for h in heads:
                dq_acc[...] += _dot(dz_ref[2 * slot + h], ks_ref[h, keys, :])
                dk_acc[keys, :] += _dot(dz_ref[2 * slot + h], qm[h], 0, 0)
                dv_acc[keys, :] += _dot(a_ref[2 * slot + h], dom[h], 0, 0)

        _tiles_reversed(i, tile, lambda nearest: jnp.max(r_ref[...]) >= VANISH)
        dq_ref[...] = dq_acc[...].astype(BF16)

        @pl.when(i == nq - 1)
        def _():
            dk_ref[...] = dk_acc[...].astype(BF16)
            dv_ref[...] = dv_acc[...].astype(BF16)

    W = n_pairs * PAIR
    in_specs = _qkv_specs(S, tq, n_pairs, base) + [
        pl.BlockSpec((tq, PAIR), lambda p, i: (i, p)),
        pl.BlockSpec((tq, PAIR), lambda p, i: (i, p))]
    out_specs = [pl.BlockSpec((tq, PAIR), lambda p, i: (i, p)),
                 pl.BlockSpec((S, PAIR), lambda p, i: (0, p)),
                 pl.BlockSpec((S, PAIR), lambda p, i: (0, p))]
    big, stat = (4, tq, tk), (2, tq, 128)
    return _call_carrying(
        body, exchange, "sb_bwd", (n_pairs, nq), in_specs, out_specs, [jax.ShapeDtypeStruct((S, W), BF16)] * 3,
        [pltpu.VMEM(big, F32)] * 4 + [pltpu.VMEM(big, BF16)] * 4 + [pltpu.VMEM(stat, F32)] * 3
        + [pltpu.VMEM((tq, PAIR), F32), pltpu.VMEM((S, PAIR), F32), pltpu.VMEM((S, PAIR), F32),
           pltpu.VMEM((2, S, PAIR), BF16)],
        (qkv, qkv, qkv, o32, do))


def _fox_fwd2(qkv, fcx, fcr, n_pairs, base, tq, exchange=None):
    S = qkv.shape[0]
    tk = tq
    scale = HEAD_DIM ** -0.5
    spare = (HEAD_DIM, 0)

    def body(q_ref, k_ref, v_ref, fq_ref, fk_ref, o_ref, lse_ref, s_ref, p_ref, m_ref, al_ref, fqr_ref, acc_ref, vm_ref):
        i = pl.program_id(1)
        masks = _head_masks()

        @pl.when(i == 0)
        def _():
            _fill_masked(vm_ref, v_ref, masks, ones_lane=spare)

        q2 = q_ref[...] * scale
        qm = [_sel(m, q2) for m in masks]
        f0, f1 = _per_head(fq_ref[...], masks)
        fqr_ref[0] = f0
        fqr_ref[1] = f1
        m_ref[...] = jnp.full(m_ref.shape, NEG, F32)
        acc_ref[...] = jnp.zeros_like(acc_ref)

        def tile(j, diag, slot, heads):
            off = pl.multiple_of(j * tk, tk)
            k2 = k_ref[pl.ds(off, tk), :]
            v2 = v_ref[pl.ds(off, tk), :]
            fk2 = fk_ref[0, j]
            for h in heads:
                s_ref[2 * slot + h] = _dot(qm[h], k2, 1, 1)
            yield
            for h in heads:
                fk_row = fk2[h:h + 1, :]

                def probs(r0, h=h, fk_row=fk_row):
                    rows = pl.ds(r0, RC)
                    sv = s_ref[2 * slot + h, rows, :] - fk_row
                    if diag:
                        sv = jnp.where(_diag_mask(r0, tk, False), sv, NEG)
                    fq = fqr_ref[h, rows, :]
                    m_prev = m_ref[h, rows, :]
                    m_new = jnp.maximum(m_prev, jnp.max(sv, axis=-1, keepdims=True) + fq)
                    p_ref[2 * slot + h, rows, :] = jnp.exp(sv + _wide(fq - m_new, tk)).astype(BF16)
                    al_ref[2 * slot + h, rows, :] = jnp.exp(m_prev - m_new)
                    m_ref[h, rows, :] = m_new
                _chunks(tq, probs)
            yield
            for h in heads:
                acc_ref[h] = acc_ref[h] * al_ref[2 * slot + h] + _dot(p_ref[2 * slot + h], vm_ref[h, pl.ds(off, tk), :])

        _tiles(i, tile)
        a0, a1 = acc_ref[0], acc_ref[1]
        l0 = _rep(a0[:, spare[0]:spare[0] + 1])
        l1 = _rep(a1[:, spare[1]:spare[1] + 1])
        o_ref[...] = jnp.where(masks[0], a0 / l0, a1 / l1).astype(BF16)
        lse_ref[...] = jnp.where(masks[0], m_ref[0] + jnp.log(l0), m_ref[1] + jnp.log(l1))

    W = n_pairs * PAIR
    return _call_carrying(
        body, exchange, "fox_fwd", (n_pairs, S // tq), _fox_specs(S, tq, n_pairs, base),
        [pl.BlockSpec((tq, PAIR), lambda p, i: (i, p)), pl.BlockSpec((tq, PAIR), lambda p, i: (i, p))],
        [jax.ShapeDtypeStruct((S, W), BF16), jax.ShapeDtypeStruct((S, W), F32)],
        [pltpu.VMEM((4, tq, tk), F32), pltpu.VMEM((4, tq, tk), BF16), pltpu.VMEM((2, tq, 128), F32),
         pltpu.VMEM((4, tq, 128), F32), pltpu.VMEM((2, tq, 128), F32), pltpu.VMEM((2, tq, 128), F32),
         pltpu.VMEM((2, S, PAIR), BF16)],
        (qkv, qkv, qkv, fcx, fcr))


def _fox_bwd2(qkv, fcx, fcr, o, lse, do, n_pairs, base, tq, exchange=None):
    S = qkv.shape[0]
    tk = tq
    nq = S // tq
    scale = HEAD_DIM ** -0.5

    def body(q_ref, k_ref, v_ref, fq_ref, fk_ref, o_ref, lse_ref, do_ref,
             dq_ref, dk_ref, dv_ref, dfq_ref, dfk_ref,
             s_ref, dp_ref, p_ref, ds_ref, row_ref, dl_ref, dfq_acc, col_ref, dq_acc, dk_acc, dv_acc, ks_ref):
        i = pl.program_id(1)
        masks = _head_masks()

        @pl.when(i == 0)
        def _():
            dk_acc[...] = jnp.zeros_like(dk_acc)
            dv_acc[...] = jnp.zeros_like(dv_acc)
            dfk_ref[...] = jnp.zeros_like(dfk_ref)
            _fill_masked(ks_ref, k_ref, masks, mul=scale)

        q2 = q_ref[...] * scale
        do2 = do_ref[...]
        qm = [_sel(m, q2) for m in masks]
        dom = [_sel(m, do2) for m in masks]
        f0, f1 = _per_head(fq_ref[...], masks)
        l0, l1 = _per_head(lse_ref[...], masks)
        row_ref[0] = f0 - l0
        row_ref[1] = f1 - l1
        prod = do2.astype(F32) * o_ref[...].astype(F32)
        for h in range(2):
            dl_ref[h] = _rep(jnp.sum(jnp.where(masks[h], prod, 0.0), axis=-1, keepdims=True))
        dfq_acc[...] = jnp.zeros_like(dfq_acc)
        dq_acc[...] = jnp.zeros_like(dq_acc)

        def tile(j, diag, slot, heads):
            off = pl.multiple_of(j * tk, tk)
            k2 = k_ref[pl.ds(off, tk), :]
            v2 = v_ref[pl.ds(off, tk), :]
            fk2 = fk_ref[0, j]
            for h in heads:
                s_ref[2 * slot + h] = _dot(qm[h], k2, 1, 1)
                dp_ref[2 * slot + h] = _dot(dom[h], v2, 1, 1)
            yield
            for h in heads:
                col_ref[2 * slot + h] = jnp.zeros((8, tk), F32)
                fk_row = fk2[h:h + 1, :]

                def dscore(r0, h=h, fk_row=fk_row):
                    rows = pl.ds(r0, RC)
                    p = jnp.exp(s_ref[2 * slot + h, rows, :] - fk_row + _wide(row_ref[h, rows, :], tk))
                    if diag:
                        p = jnp.where(_diag_mask(r0, tk, False), p, 0.0)
                    ds = p * (dp_ref[2 * slot + h, rows, :] - _wide(dl_ref[h, rows, :], tk))
                    p_ref[2 * slot + h, rows, :] = p.astype(BF16)
                    ds_ref[2 * slot + h, rows, :] = ds.astype(BF16)
                    dfq_acc[h, rows, :] += _rep(jnp.sum(ds, axis=-1, keepdims=True))
                    col_ref[2 * slot + h] += jnp.sum(ds.reshape(RC // 8, 8, tk), axis=0)
                _chunks(tq, dscore)
            yield
            keys = pl.ds(off, tk)
            for h in heads:
                dq_acc[...] += _dot(ds_ref[2 * slot + h], ks_ref[h, keys, :])
                dk_acc[keys, :] += _dot(ds_ref[2 * slot + h], qm[h], 0, 0)
                dv_acc[keys, :] += _dot(p_ref[2 * slot + h], dom[h], 0, 0)
            for h in heads:
                dfk_ref[0, j, h:h + 1, :] += jnp.sum(col_ref[2 * slot + h], axis=0, keepdims=True)

        _tiles(i, tile)
        dq_ref[...] = dq_acc[...].astype(BF16)
        dfq_ref[...] = jnp.where(masks[0], dfq_acc[0], dfq_acc[1])

        @pl.when(i == nq - 1)
        def _():
            dk_ref[...] = dk_acc[...].astype(BF16)
            dv_ref[...] = dv_acc[...].astype(BF16)

    W = n_pairs * PAIR
    in_specs = _fox_specs(S, tq, n_pairs, base) + [
        pl.BlockSpec((tq, PAIR), lambda p, i: (i, p)),
        pl.BlockSpec((tq, PAIR), lambda p, i: (i, p)),
        pl.BlockSpec((tq, PAIR), lambda p, i: (i, p))]
    out_specs = [pl.BlockSpec((tq, PAIR), lambda p, i: (i, p)),
                 pl.BlockSpec((S, PAIR), lambda p, i: (0, p)),
                 pl.BlockSpec((S, PAIR), lambda p, i: (0, p)),
                 pl.BlockSpec((tq, PAIR), lambda p, i: (i, p)),
                 pl.BlockSpec((1, nq, 8, tk), lambda p, i: (p, 0, 0, 0))]
    return _call_carrying(
        body, exchange, "fox_bwd", (n_pairs, nq), in_specs, out_specs,
        [jax.ShapeDtypeStruct((S, W), BF16)] * 3
        + [jax.ShapeDtypeStruct((S, W), F32), jax.ShapeDtypeStruct((n_pairs, nq, 8, tk), F32)],
        [pltpu.VMEM((4, tq, tk), F32)] * 2 + [pltpu.VMEM((4, tq, tk), BF16)] * 2
        + [pltpu.VMEM((2, tq, 128), F32)] * 3 + [pltpu.VMEM((4, 8, tk), F32)]
        + [pltpu.VMEM((tq, PAIR), F32), pltpu.VMEM((S, PAIR), F32), pltpu.VMEM((S, PAIR), F32),
           pltpu.VMEM((2, S, PAIR), BF16)],
        (qkv, qkv, qkv, fcx, fcr, o, lse, do))


def _col_chunks(n, width=256):
    return [(c, min(width, n - c)) for c in range(0, n, width)]


def _swiglu_fwd(u2, wg, wu):
    S, D = u2.shape
    FF = wg.shape[1]
    tm, tn = _pick(S, (512, 256, 128)), _divisors(FF, 1536)[0]

    def body(u_ref, g_ref, w_ref, a_ref, b_ref, h_ref):
        u = u_ref[...]
        for c, w in _col_chunks(tn):
            cols = slice(c, c + w)
            a = _dot(u, g_ref[:, cols])
            b = _dot(u, w_ref[:, cols])
            a_ref[:, cols] = a.astype(BF16)
            b_ref[:, cols] = b.astype(BF16)
            h_ref[:, cols] = (a / (1.0 + jnp.exp(-a)) * b).astype(BF16)

    spec_o = pl.BlockSpec((tm, tn), lambda i, j: (i, j))
    return pl.pallas_call(
        body, name="swiglu_fwd", grid=(S // tm, FF // tn),
        in_specs=[pl.BlockSpec((tm, D), lambda i, j: (i, 0)),
                  pl.BlockSpec((D, tn), lambda i, j: (0, j)),
                  pl.BlockSpec((D, tn), lambda i, j: (0, j))],
        out_specs=[spec_o] * 3, out_shape=[jax.ShapeDtypeStruct((S, FF), BF16)] * 3,
        compiler_params=_params(("parallel", "parallel")),
    )(u2, wg, wu)


def _swiglu_bwd(dh, wd, a, b):
    S, D = dh.shape
    FF = wd.shape[0]
    tm, tn = _pick(S, (512, 256, 128)), _divisors(FF, 1536)[0]

    def body(dh_ref, w_ref, a_ref, b_ref, da_ref, db_ref):
        dh_blk = dh_ref[...]
        for c, w in _col_chunks(tn):
            cols = slice(c, c + w)
            dhin = _dot(dh_blk, w_ref[cols, :], 1, 1)
            av = a_ref[:, cols].astype(F32)
            bv = b_ref[:, cols].astype(F32)
            sig = 1.0 / (1.0 + jnp.exp(-av))
            da_ref[:, cols] = (dhin * bv * (sig * (1.0 + av * (1.0 - sig)))).astype(BF16)
            db_ref[:, cols] = (dhin * (av * sig)).astype(BF16)

    spec_o = pl.BlockSpec((tm, tn), lambda i, j: (i, j))
    return pl.pallas_call(
        body, name="swiglu_bwd", grid=(S // tm, FF // tn),
        in_specs=[pl.BlockSpec((tm, D), lambda i, j: (i, 0)),
                  pl.BlockSpec((tn, D), lambda i, j: (j, 0)), spec_o, spec_o],
        out_specs=[spec_o] * 2, out_shape=[jax.ShapeDtypeStruct((S, FF), BF16)] * 2,
        compiler_params=_params(("parallel", "parallel")),
    )(dh, wd, a, b)


def _gate_bwd(dmix, wo, g_sb, g_fx, y_sb, y_fx):
    S, D = dmix.shape
    tm, tn = _pick(S, (512, 256, 128)), _divisors(D, 1024)[0]

    def body(dm_ref, w_ref, gs_ref, gf_ref, ys_ref, yf_ref, dys_ref, dyf_ref, dls_ref, dlf_ref, bs_ref, bf_ref):
        @pl.when(pl.program_id(1) == 0)
        def _():
            bs_ref[...] = jnp.zeros_like(bs_ref)
            bf_ref[...] = jnp.zeros_like(bf_ref)

        dm_blk = dm_ref[...]
        for c, w in _col_chunks(tn):
            cols = slice(c, c + w)
            dmi = _dot(dm_blk, w_ref[cols, :], 1, 1)
            gs, gf = gs_ref[:, cols], gf_ref[:, cols]
            dys_ref[:, cols] = (dmi * gs).astype(BF16)
            dyf_ref[:, cols] = (dmi * gf).astype(BF16)
            dls = dmi * ys_ref[:, cols] * gs * (1.0 - gs)
            dlf = dmi * yf_ref[:, cols] * gf * (1.0 - gf)
            dls_ref[:, cols] = dls.astype(BF16)
            dlf_ref[:, cols] = dlf.astype(BF16)
            bs_ref[0:1, cols] += _colsum(dls)
            bf_ref[0:1, cols] += _colsum(dlf)

    t = pl.BlockSpec((tm, tn), lambda j, i: (i, j))
    accs = pl.BlockSpec((8, tn), lambda j, i: (0, j))
    return pl.pallas_call(
        body, name="gate_bwd", grid=(D // tn, S // tm),
        in_specs=[pl.BlockSpec((tm, D), lambda j, i: (i, 0)),
                  pl.BlockSpec((tn, D), lambda j, i: (j, 0)), t, t, t, t],
        out_specs=[t, t, t, t, accs, accs],
        out_shape=[jax.ShapeDtypeStruct((S, D), BF16)] * 4 + [jax.ShapeDtypeStruct((8, D), F32)] * 2,
        compiler_params=_params(("parallel", "arbitrary")),
    )(dmix, wo, g_sb, g_fx, y_sb, y_fx)


def _local_step(x, target, ada8, lnp8, bg_sb, bg_fx, bf_pad, wqkv, wf, wgs, wgf, gather, later_weights, pack_early,
                pack_last):
    S, D = x.shape
    W = wqkv.shape[1] // 6
    n_pairs = W // PAIR
    n_heads = W // HEAD_DIM
    ts = _pick(S, (512, 256, 128))
    tq = _pick(S, (256, 128))

    u1 = _ln_mod(x, ada8, ts)
    qkv = _mm([(u1, wqkv)], 'nn', BF16, "in_qkv")
    f = _mm([(u1, wf)], 'nn', F32, "in_f")
    g_sb = _mm([(u1, wgs)], 'nn', F32, "in_gsb", bias=bg_sb, act='sigmoid')
    g_fx = _mm([(u1, wgf)], 'nn', F32, "in_gfx", bias=bg_fx, act='sigmoid')
    fc = _fgate_fwd(f, bf_pad, _pick(S, (512, 256, 128)))
    fch = fc[:, :n_heads]
    fcx = jnp.repeat(fch, HEAD_DIM, axis=1)
    nq = S // tq
    fcr = jnp.pad(fch.T.reshape(n_pairs, 2, nq, tq).transpose(0, 2, 1, 3),
                  ((0, 0), (0, 0), (0, 6), (0, 0)))
    o_sb, o_sb32, *zone_a = _sb_fwd2(qkv, n_pairs, 0, tq, gather[0])
    o_fx, lse, *zone_b = _fox_fwd2(qkv, fcx, fcr, n_pairs, 3 * n_pairs, tq, gather[1])
    wsb, wfx, wo, wfg, wfu, wfd = later_weights(*zone_a, *zone_b)
    y_sb =_mm([(o_sb, wsb)], 'nn', F32, "out_sb")
    y_fx = _mm([(o_fx, wfx)], 'nn', F32, "out_fx")
    mix_in, mix = _gate_mix_out(g_sb, g_fx, y_sb, y_fx, wo, ts)
    x1, u2 = _post_attn(x, mix, ada8, lnp8, ts)
    a, b, hin = _swiglu_fwd(u2, wfg, wfu)
    dr2, dh, st_loss = _loss_head(x1, hin, wfd, target, ada8, lnp8, ts)

    da, db = _swiglu_bwd(dh, wfd, a, b)
    g_wfd = _mm([(hin, dh)], 'tn', F32, "g_ffn_down")
    du2 = _mm([(da, wfg), (db, wfu)], 'nt', F32, "d_u2")
    g_wfg = _mm([(u2, da)], 'tn', F32, "g_ffn_gate")
    g_wfu = _mm([(u2, db)], 'tn', F32, "g_ffn_up")
    dr1, dmix, st_mid = _mid_bwd(du2, x1, dr2, mix, x, ada8, lnp8, ts)
    dys, dyf, dls, dlf, gb_sb, gb_fx = _gate_bwd(dmix, wo, g_sb, g_fx, y_sb, y_fx)
    g_wo = _mm([(mix_in, dmix)], 'tn', F32, "g_w_o")
    do_sb = _mm([(dys, wsb)], 'nt', BF16, "d_o_sb")
    do_fx = _mm([(dyf, wfx)], 'nt', BF16, "d_o_fx")
    g_wsb = _mm([(o_sb, dys)], 'tn', F32, "g_sb_out")
    g_wfx = _mm([(o_fx, dyf)], 'tn', F32, "g_fox_out")
    scatter = pack_early(dict(sb=g_wsb, fx=g_wfx, o=g_wo, fg=g_wfg, fu=g_wfu, fd=g_wfd))
    dq_s, dk_s, dv_s, *zone_a = _sb_bwd2(qkv, o_sb32, do_sb, n_pairs, 0, tq, scatter[0])
    dq_f, dk_f, dv_f, dfq, dfk, *zone_b = _fox_bwd2(qkv, fcx, fcr, o_fx, lse, do_fx, n_pairs, 3 * n_pairs, tq,
                                                    scatter[1])
    early = [(scatter[0], zone_a[0] if zone_a else None), (scatter[1], zone_b[0] if zone_b else None)]
    dfc = dfq[:, ::HEAD_DIM] - dfk[:, :, :2, :].transpose(0, 2, 1, 3).reshape(n_heads, S).T
    dfc = jnp.pad(dfc, ((0, 0), (0, 128 - n_heads)))
    df, gb_f = _fgate_bwd(dfc, f, bf_pad, _pick(S, (512, 256, 128)))
    grads = [dq_s, dk_s, dv_s, dq_f, dk_f, dv_f]
    g_wqkv = [_mm([(u1, g)], 'tn', F32, "g_in_%d" % n) for n, g in enumerate(grads)]
    g_wf = _mm([(u1, df)], 'tn', F32, "g_in_f")
    g_wgs = _mm([(u1, dls)], 'tn', F32, "g_in_gsb")
    g_wgf = _mm([(u1, dlf)], 'tn', F32, "g_in_gfx")
    wgrads = dict(qkv=g_wqkv, f=g_wf, gs=g_wgs, gf=g_wgf)
    last = pack_last(wgrads)
    du1 = _mm([(g, wqkv, W, 0, n) for n, g in enumerate(grads)] + [(df, wf), (dls, wgs), (dlf, wgf)],
              'nt', F32, "d_u1", exchange=last)
    du1, last_zone = du1 if last is not None else (du1, None)
    gx, st_first = _first_bwd(du1, x, dr1, ada8, ts)

    stats = dict(loss=st_loss, mid=st_mid, first=st_first, gb_sb=gb_sb, gb_fx=gb_fx, gb_f=gb_f)
    return gx, wgrads, stats, early, (last, last_zone)


def _position():
    x, y, c = lax.axis_index("x"), lax.axis_index("y"), lax.axis_index("c")
    return x, y, c, 4 * x + 2 * y + c


def _flip(x, y, c, k):
    px = 1 - x if k & 4 else x
    py = 1 - y if k & 2 else y
    pc = 1 - c if k & 1 else c
    return (px, py, pc), 4 * px + 2 * py + pc


def _all_gather_small(v, name):
    r, n = v.shape

    def body(x_ref, out_ref, send_sems, recv_sems, local_sem):
        x, y, c, me = _position()
        mine = pltpu.make_async_copy(x_ref, out_ref.at[me], local_sem)
        mine.start()
        sends = []
        for k in range(1, N_DEV):
            peer, _ = _flip(x, y, c, k)
            cp = pltpu.make_async_remote_copy(
                src_ref=x_ref, dst_ref=out_ref.at[me], send_sem=send_sems.at[k - 1], recv_sem=recv_sems.at[k - 1],
                device_id=peer, device_id_type=MESH)
            cp.start()
            sends.append(cp)
        for k in range(1, N_DEV):
            peer, slot = _flip(x, y, c, k)
            pltpu.make_async_remote_copy(
                src_ref=x_ref, dst_ref=out_ref.at[slot], send_sem=send_sems.at[k - 1], recv_sem=recv_sems.at[k - 1],
                device_id=peer, device_id_type=MESH).wait_recv()
        for cp in sends:
            cp.wait_send()
        mine.wait()

    return pl.pallas_call(
        body, name=name, out_shape=jax.ShapeDtypeStruct((N_DEV, r, n), v.dtype),
        in_specs=[pl.BlockSpec(memory_space=pltpu.VMEM)], out_specs=pl.BlockSpec(memory_space=pltpu.VMEM),
        scratch_shapes=[pltpu.SemaphoreType.DMA((N_DEV - 1,)), pltpu.SemaphoreType.DMA((N_DEV - 1,)),
                        pltpu.SemaphoreType.DMA],
    )(v)


def _all_gather_weights(packed):
    R, C = packed.shape

    def body(x_ref, out_ref, send_sems, recv_sems, local_sem):
        x, y, c, me = _position()
        sibling, sib_slot = _flip(x, y, c, 1)
        mine = pltpu.make_async_copy(x_ref, out_ref.at[me], local_sem)
        mine.start()

        def copy(k, slot, to, src=None):
            return pltpu.make_async_remote_copy(
                src_ref=out_ref.at[slot] if src is None else src, dst_ref=out_ref.at[slot],
                send_sem=send_sems.at[k], recv_sem=recv_sems.at[k], device_id=to, device_id_type=MESH)

        first = [copy(0, me, sibling, src=x_ref)]
        chips = (4, 2, 6)
        for n, k in enumerate(chips):
            peer, _ = _flip(x, y, c, k)
            first.append(copy(1 + n, me, peer, src=x_ref))
        for cp in first:
            cp.start()
        passed = []
        for n, k in enumerate(chips):
            peer, slot = _flip(x, y, c, k)
            copy(1 + n, slot, peer).wait_recv()
            cp = copy(4 + n, slot, sibling)
            cp.start()
            passed.append(cp)
        copy(0, sib_slot, sibling).wait_recv()
        for n, k in enumerate(chips):
            _, slot = _flip(x, y, c, k | 1)
            copy(4 + n, slot, sibling).wait_recv()
        for cp in first + passed:
            cp.wait_send()
        mine.wait()

    return pl.pallas_call(
        body, name="all_gather_weights", out_shape=jax.ShapeDtypeStruct((N_DEV, R, C), packed.dtype),
        in_specs=[pl.BlockSpec(memory_space=pl.ANY)], out_specs=pl.BlockSpec(memory_space=pl.ANY),
        scratch_shapes=[pltpu.SemaphoreType.DMA((7,)), pltpu.SemaphoreType.DMA((7,)), pltpu.SemaphoreType.DMA],
    )(packed)


def _exchange_grads(gpack):
    _, R, C = gpack.shape

    def body(g_ref, out_ref, send_sems, recv_sems, local_sem):
        x, y, c, me = _position()
        mine = pltpu.make_async_copy(g_ref.at[me], out_ref.at[me], local_sem)
        mine.start()
        sends = []
        for k in range(1, N_DEV):
            peer, slot = _flip(x, y, c, k)
            cp = pltpu.make_async_remote_copy(
                src_ref=g_ref.at[slot], dst_ref=out_ref.at[me], send_sem=send_sems.at[k - 1],
                recv_sem=recv_sems.at[k - 1], device_id=peer, device_id_type=MESH)
            cp.start()
            sends.append(cp)
        for k in range(1, N_DEV):
            peer, slot = _flip(x, y, c, k)
            pltpu.make_async_remote_copy(
                src_ref=g_ref.at[slot], dst_ref=out_ref.at[slot], send_sem=send_sems.at[k - 1],
                recv_sem=recv_sems.at[k - 1], device_id=peer, device_id_type=MESH).wait_recv()
        for cp in sends:
            cp.wait_send()
        mine.wait()

    return pl.pallas_call(
        body, name="exchange_grads", out_shape=jax.ShapeDtypeStruct((N_DEV, R, C), gpack.dtype),
        in_specs=[pl.BlockSpec(memory_space=pl.ANY)], out_specs=pl.BlockSpec(memory_space=pl.ANY),
        scratch_shapes=[pltpu.SemaphoreType.DMA((N_DEV - 1,)), pltpu.SemaphoreType.DMA((N_DEV - 1,)),
                        pltpu.SemaphoreType.DMA],
    )(gpack)


def _own_slot(land, own):
    me = 4 * lax.axis_index("x") + 2 * lax.axis_index("y") + lax.axis_index("c")
    return lax.dynamic_update_slice(land, own[None], (me, 0, 0))


def _sum_slots(recv, name, tr):
    n, R, C = recv.shape

    def body(r_ref, o_ref):
        acc = r_ref[0].astype(F32)
        for s in range(1, n):
            acc = acc + r_ref[s].astype(F32)
        o_ref[...] = acc

    return pl.pallas_call(
        body, name=name, grid=(R // tr,), in_specs=[pl.BlockSpec((n, tr, C), lambda i: (0, i, 0))],
        out_specs=pl.BlockSpec((tr, C), lambda i: (i, 0)), out_shape=jax.ShapeDtypeStruct((R, C), F32),
        compiler_params=_params(("parallel",)),
    )(recv)


def _sum_stats(st_all, loss_row):
    n, r, D = st_all.shape

    def body(s_ref, o_ref, l_ref):
        acc = s_ref[0]
        for d in range(1, n):
            acc = acc + s_ref[d]
        o_ref[...] = acc
        l_ref[...] = jnp.zeros((8, 128), F32) + jnp.sum(acc[loss_row:loss_row + 1, :], axis=-1, keepdims=True)

    return pl.pallas_call(
        body, name="sum_stats", out_shape=[jax.ShapeDtypeStruct((r, D), F32), jax.ShapeDtypeStruct((8, 128), F32)],
    )(st_all)


def _adamw(w, g, m, v, name):
    R, C = w.shape
    tr = _pick(R, (256, 176, 128, 64, 32, 16, 8))
    c1 = 1.0 / (1.0 - ADAM_B1 ** ADAM_STEP)
    c2 = 1.0 / (1.0 - ADAM_B2 ** ADAM_STEP)

    def body(w_ref, g_ref, m_ref, v_ref, d_ref, nm_ref, nv_ref):
        gv = g_ref[...]
        nm = ADAM_B1 * m_ref[...] + (1.0 - ADAM_B1) * gv
        nv = ADAM_B2 * v_ref[...] + (1.0 - ADAM_B2) * (gv * gv)
        nm_ref[...] = nm
        nv_ref[...] = nv
        d_ref[...] = -ADAM_LR * ((nm * c1) / (jnp.sqrt(nv * c2) + ADAM_EPS) + ADAM_WD * w_ref[...])

    spec = pl.BlockSpec((tr, C), lambda i: (i, 0))
    return pl.pallas_call(
        body, name=name, grid=(R // tr,), in_specs=[spec] * 4, out_specs=[spec] * 3,
        out_shape=[jax.ShapeDtypeStruct((R, C), F32)] * 3, compiler_params=_params(("parallel",)),
    )(w, g, m, v)


def _round16(n):
    return -(-n // 16) * 16


def _pack_layout(D, in_cols, ff, W):
    parts = [("in", D * (in_cols // N_DEV) // D), ("fg", ff // N_DEV), ("fu", ff // N_DEV),
             ("sb", W * (D // N_DEV) // D), ("fx", W * (D // N_DEV) // D), ("o", D // N_DEV), ("fd", ff // N_DEV)]
    layout, off = {}, 0
    for nm, rows in parts:
        layout[nm] = (off, rows)
        off += _round16(rows)
    return layout, off


def _rows_of(a, D, rows):
    a = a.reshape(rows, D)
    return jnp.pad(a, ((0, _round16(rows) - rows), (0, 0)))


def _cols_to_dest(g, D):
    K, N = g.shape
    n = N // N_DEV
    return g.reshape(K, N_DEV, n).transpose(1, 0, 2).reshape(N_DEV, K * n // D, D)


def _cols_from_src(blocks, K, n):
    return blocks.reshape(N_DEV, K, n).transpose(1, 0, 2).reshape(K, N_DEV * n)


def _pad_rows16(a):
    rows = a.shape[1]
    return jnp.pad(a, ((0, 0), (0, _round16(rows) - rows), (0, 0)))


def kernel(x, c, w_ada, b_ada, w_in, b_gate, b_forget, w_sb_out, w_fox_out, w_o, ln1_g, ln1_b, w_ffn_gate, w_ffn_up, w_ffn_down, ln2_g, ln2_b, loss_target, m_w_ada, m_b_ada, m_w_in, m_b_gate, m_b_forget, m_w_sb_out, m_w_fox_out, m_w_o, m_ln1_g, m_ln1_b, m_w_ffn_gate, m_w_ffn_up, m_w_ffn_down, m_ln2_g, m_ln2_b, v_w_ada, v_b_ada, v_w_in, v_b_gate, v_b_forget, v_w_sb_out, v_w_fox_out, v_w_o, v_ln1_g, v_ln1_b, v_w_ffn_gate, v_w_ffn_up, v_w_ffn_down, v_ln2_g, v_ln2_b):
    S, D = x.shape[1], x.shape[2]
    W = w_sb_out.shape[1]
    n_heads = b_forget.shape[1]
    ff = w_ffn_down.shape[1] * N_DEV
    in_loc = w_in.shape[2]
    in_cols = in_loc * N_DEV
    ada_loc = w_ada.shape[2]
    n_cond = ada_loc * N_DEV // D
    assert w_ada.shape[0] == 1 and n_cond == 6 and in_cols == 6 * W + n_heads + 2 * D and n_heads <= 128
    me = 4 * lax.axis_index("x") + 2 * lax.axis_index("y") + lax.axis_index("c")

    c_all = _all_gather_small(c, "gather_c").reshape(N_DEV, D)
    c16 = jnp.pad(c_all, ((0, 16 - N_DEV), (0, 0)))
    b_cols = lax.dynamic_slice(b_ada, (0, me * ada_loc), (1, ada_loc))
    ada_cols = _mm([(c16, w_ada[0])], 'nn', F32, "ada_fwd", bias=b_cols, silu_a=True)[:N_DEV]
    ada_all = _all_gather_small(ada_cols, "gather_ada")
    ada_me = lax.dynamic_index_in_dim(ada_all, me, axis=1, keepdims=False)
    ada8 = jnp.pad(ada_me.reshape(n_cond, D), ((0, 8 - n_cond), (0, 0)))
    lnp8 = jnp.concatenate([ln1_g, ln1_b, ln2_g, ln2_b, jnp.zeros((4, D), F32)], axis=0)

    layout, R = _pack_layout(D, in_cols, ff, W)
    shards = dict(**{"in": w_in[0]}, fg=w_ffn_gate[0], fu=w_ffn_up[0], sb=w_sb_out[0], fx=w_fox_out[0], o=w_o[0],
                  fd=w_ffn_down[0])
    rows_fwd, rows_bwd = ("sb", "fx", "o"), ("fd", "o", "sb", "fx")
    rows_of = {nm: layout[nm][1] for nm in layout}

    def offsets(names):
        offs, off = {}, 0
        for nm in names:
            offs[nm] = off
            off += _round16(rows_of[nm])
        return offs

    def as_rows(names):
        return jnp.concatenate([_rows_of(shards[nm].astype(BF16), D, rows_of[nm]) for nm in names], axis=0)

    def whole_from(blocks):
        return blocks.transpose(1, 0, 2).reshape(blocks.shape[1], N_DEV * blocks.shape[2])

    def blocks_of(g):
        return g.reshape(g.shape[0], N_DEV, g.shape[1] // N_DEV).transpose(1, 0, 2)

    gather_src = [as_rows(rows_fwd),
                  jnp.concatenate([shards["fg"].astype(BF16), shards["fu"].astype(BF16)], axis=0), as_rows(("fd",))]
    gathered_in = _all_gather_weights(shards["in"].astype(BF16))

    w_in_full = whole_from(gathered_in)
    wqkv = w_in_full[:, :6 * W]
    wf = jnp.pad(w_in_full[:, 6 * W:6 * W + n_heads], ((0, 0), (0, 128 - n_heads)))
    wgs = w_in_full[:, 6 * W + n_heads:6 * W + n_heads + D]
    wgf = w_in_full[:, 6 * W + n_heads + D:]
    bf_pad = jnp.pad(b_forget, ((0, 0), (0, 128 - n_heads)))

    def later_weights(zone_rows, zone_gate_up, zone_down):
        rows, offs = _own_slot(zone_rows, gather_src[0]), offsets(rows_fwd)
        part = {nm: rows[:, offs[nm]:offs[nm] + rows_of[nm], :] for nm in rows_fwd}
        gate_up = _own_slot(zone_gate_up, gather_src[1])
        down = _own_slot(zone_down, gather_src[2])[:, :rows_of["fd"], :]
        return (_cols_from_src(part["sb"], W, D // N_DEV), _cols_from_src(part["fx"], W, D // N_DEV),
                part["o"].reshape(D, D), whole_from(gate_up[:, :D, :]), whole_from(gate_up[:, D:, :]),
                down.reshape(ff, D))

    def pack_early(g):
        dest = {"sb": _cols_to_dest(g["sb"], D), "fx": _cols_to_dest(g["fx"], D),
                "o": g["o"].reshape(N_DEV, D // N_DEV, D), "fd": g["fd"].reshape(N_DEV, ff // N_DEV, D)}
        rows = jnp.concatenate([_pad_rows16(dest[nm].astype(BF16)) for nm in rows_bwd], axis=1)
        gate_up = jnp.concatenate([blocks_of(g["fg"].astype(BF16)), blocks_of(g["fu"].astype(BF16))], axis=1)
        return [(rows, True), (gate_up, True)]

    def pack_last(g):
        g_in = jnp.concatenate(g["qkv"] + [g["f"][:, :n_heads], g["gs"], g["gf"]], axis=1)
        return blocks_of(g_in.astype(BF16)), True

    gx, wg, st, early, ((pack_in, _), land_in) = _local_step(
        x[0], loss_target[0], ada8, lnp8, b_gate[:, :D], b_gate[:, D:], bf_pad, wqkv, wf, wgs, wgf,
        [(gather_src[0], False), [(gather_src[1], False), (gather_src[2], False)]],
        later_weights, pack_early, pack_last)

    def summed(zone, sent, name):
        own = lax.dynamic_index_in_dim(sent, me, axis=0, keepdims=False)
        rows = zone.shape[1]
        block = max(t for t in range(16, 705, 16) if rows % t == 0)
        return _sum_slots(_own_slot(zone, own), name, block)

    ((sent_rows, _), zone_rows), ((sent_gate_up, _), zone_gate_up) = early
    gate_up = summed(zone_gate_up, sent_gate_up, "sum_grads_gate_up")
    gsum = {"in": summed(land_in, pack_in, "sum_grads_in"), "fg": gate_up[:D], "fu": gate_up[D:]}
    total, offs = summed(zone_rows, sent_rows, "sum_grads_rows"), offsets(rows_bwd)
    for nm in rows_bwd:
        gsum[nm] = total[offs[nm]:offs[nm] + rows_of[nm]]

    def gshard(nm, shape):
        return gsum[nm].reshape(shape)

    zrow = jnp.zeros((1, D), F32)
    gb_f_row = jnp.pad(st["gb_f"][0:1], ((0, 0), (0, D - 128)))
    stats16 = jnp.concatenate([
        st["first"][1:2], st["first"][0:1], st["mid"][4:5], st["mid"][1:2], st["mid"][0:1], st["loss"][3:4],
        st["mid"][2:3], st["mid"][3:4], st["loss"][1:2], st["loss"][2:3], st["gb_sb"][0:1], st["gb_fx"][0:1],
        st["loss"][0:1], gb_f_row, zrow, zrow], axis=0)
    st_all = _all_gather_small(stats16, "gather_stats")
    st_sum, loss_blk = _sum_stats(st_all, 12)
    loss = loss_blk[0, 0]

    d_ada_all = st_all[:, :n_cond, :].reshape(N_DEV, n_cond * D)
    d_cols = lax.dynamic_slice(d_ada_all, (0, me * ada_loc), (N_DEV, ada_loc))
    d16 = jnp.pad(d_cols, ((0, 16 - N_DEV), (0, 0)))
    g_w_ada = _mm([(c16, d16)], 'tn', F32, "ada_wgrad", silu_a=True)

    small_w = jnp.concatenate([b_ada.reshape(n_cond, D), ln1_g, ln1_b, ln2_g, ln2_b, b_gate.reshape(2, D), zrow,
                               jnp.pad(b_forget, ((0, 0), (0, D - n_heads))), zrow, zrow], axis=0)
    small_m = jnp.concatenate([m_b_ada.reshape(n_cond, D), m_ln1_g, m_ln1_b, m_ln2_g, m_ln2_b, m_b_gate.reshape(2, D),
                               zrow, jnp.pad(m_b_forget, ((0, 0), (0, D - n_heads))), zrow, zrow], axis=0)
    small_v = jnp.concatenate([v_b_ada.reshape(n_cond, D), v_ln1_g, v_ln1_b, v_ln2_g, v_ln2_b, v_b_gate.reshape(2, D),
                               zrow, jnp.pad(v_b_forget, ((0, 0), (0, D - n_heads))), zrow, zrow], axis=0)
    sm = _adamw(small_w, st_sum, small_m, small_v, "adamw_small")

    def small(a, nm):
        if nm == "b_ada":
            return a[0:n_cond].reshape(1, n_cond * D)
        if nm == "b_gate":
            return a[10:12].reshape(1, 2 * D)
        if nm == "b_forget":
            return a[13:14, :n_heads]
        row = {"ln1_g": 6, "ln1_b": 7, "ln2_g": 8, "ln2_b": 9}[nm]
        return a[row:row + 1]

    big = {
        "w_ada": (w_ada[0], g_w_ada, m_w_ada[0], v_w_ada[0]),
        "w_in": (w_in[0], gshard("in", w_in.shape[1:]), m_w_in[0], v_w_in[0]),
        "w_sb_out": (w_sb_out[0], gshard("sb", w_sb_out.shape[1:]), m_w_sb_out[0], v_w_sb_out[0]),
        "w_fox_out": (w_fox_out[0], gshard("fx", w_fox_out.shape[1:]), m_w_fox_out[0], v_w_fox_out[0]),
        "w_o": (w_o[0], gshard("o", w_o.shape[1:]), m_w_o[0], v_w_o[0]),
        "w_ffn_gate": (w_ffn_gate[0], gshard("fg", w_ffn_gate.shape[1:]), m_w_ffn_gate[0], v_w_ffn_gate[0]),
        "w_ffn_up": (w_ffn_up[0], gshard("fu", w_ffn_up.shape[1:]), m_w_ffn_up[0], v_w_ffn_up[0]),
        "w_ffn_down": (w_ffn_down[0], gshard("fd", w_ffn_down.shape[1:]), m_w_ffn_down[0], v_w_ffn_down[0]),
    }
    order = ["w_ada", "b_ada", "w_in", "b_gate", "b_forget", "w_sb_out", "w_fox_out", "w_o", "ln1_g", "ln1_b",
             "w_ffn_gate", "w_ffn_up", "w_ffn_down", "ln2_g", "ln2_b"]
    grads, deltas, new_ms, new_vs = [], [], [], []
    for nm in order:
        if nm in big:
            w, g, m, v = big[nm]
            d, nm_, nv_ = _adamw(w, g, m, v, "adamw_" + nm)
            grads.append(g[None])
            deltas.append(d[None])
            new_ms.append(nm_[None])
            new_vs.append(nv_[None])
        else:
            grads.append(small(st_sum, nm))
            deltas.append(small(sm[0], nm))
            new_ms.append(small(sm[1], nm))
            new_vs.append(small(sm[2], nm))
    return (loss, gx[None], *grads, *deltas, *new_ms, *new_vs)
```

```python
import jax
import jax.numpy as jnp
from jax import lax
from jax.experimental import pallas as pl
from jax.experimental.pallas import tpu as pltpu

F32 = jnp.float32
BF16 = jnp.bfloat16

HEAD_DIM = 64
PAIR = 2 * HEAD_DIM
LN_EPS = 1e-5
ALPHA = 2.0 ** 0.25
ADAM_LR, ADAM_B1, ADAM_B2, ADAM_EPS, ADAM_WD, ADAM_STEP = 0.001, 0.9, 0.999, 1e-08, 0.01, 10
N_DEV = 8
VMEM_LIMIT = 56 * 1024 * 1024
MESH = pl.DeviceIdType.MESH


def _dot(a, b, ca=1, cb=0):
    return lax.dot_general(a, b, (((ca,), (cb,)), ((), ())), preferred_element_type=F32)


def _pick(n, cands):
    for t in cands:
        if n % t == 0:
            return t
    return n


def _params(sem):
    return pltpu.CompilerParams(dimension_semantics=sem, vmem_limit_bytes=VMEM_LIMIT)


MM_BLOCK_BYTES = 40 * 1024 * 1024
LANES = 128


def _divisors(n, cap):
    ds = [d for d in range(LANES, min(n, cap) + 1, LANES) if n % d == 0]
    return sorted(ds, reverse=True) or [n]


def _mm_tiles(M, N, a_row_bytes, b_row_bytes, out_itemsize):
    best = None
    for tm in _divisors(M, 1024):
        for tn in _divisors(N, 2048):
            need = 2 * (tm * a_row_bytes + tn * b_row_bytes + tm * tn * out_itemsize) + tm * tn * 4
            if need <= MM_BLOCK_BYTES and (best is None or (tm * tn, tm) > (best[0] * best[1], best[0])):
                best = (tm, tn)
    assert best is not None, (M, N, a_row_bytes, b_row_bytes)
    return best


def _mm(pairs, mode, out_dtype, name, bias=None, act=None, silu_a=False, exchange=None):
    norm = []
    for p in pairs:
        a, b = p[0], p[1]
        kdim_a = a.shape[0] if mode == 'tn' else a.shape[1]
        K, ka, kb = (p[2], p[3], p[4]) if len(p) > 2 else (kdim_a, 0, 0)
        norm.append((a, b, K, ka, kb))
    a0, b0 = norm[0][0], norm[0][1]
    M = a0.shape[1] if mode == 'tn' else a0.shape[0]
    N = b0.shape[0] if mode == 'nt' else b0.shape[1]
    tm, tn = _mm_tiles(M, N, sum(K * a.dtype.itemsize for a, _, K, _, _ in norm),
                       sum(K * b.dtype.itemsize for _, b, K, _, _ in norm), jnp.dtype(out_dtype).itemsize)
    n_pairs = len(norm)

    in_specs, args = [], []
    for a, b, K, ka, kb in norm:
        if mode == 'tn':
            in_specs.append(pl.BlockSpec((K, tm), lambda i, j, ka=ka: (ka, i)))
        else:
            in_specs.append(pl.BlockSpec((tm, K), lambda i, j, ka=ka: (i, ka)))
        if mode == 'nt':
            in_specs.append(pl.BlockSpec((tn, K), lambda i, j, kb=kb: (j, kb)))
        else:
            in_specs.append(pl.BlockSpec((K, tn), lambda i, j, kb=kb: (kb, j)))
        args += [a, b]
    if bias is not None:
        in_specs.append(pl.BlockSpec((1, tn), lambda i, j: (0, j)))
        args.append(bias)

    ca = 0 if mode == 'tn' else 1
    cb = 1 if mode == 'nt' else 0

    def body(*refs):
        o_ref = refs[-1]
        acc = None
        for p in range(n_pairs):
            av = refs[2 * p][...]
            if silu_a:
                av = av / (1.0 + jnp.exp(-av))
            d = _dot(av.astype(BF16), refs[2 * p + 1][...].astype(BF16), ca, cb)
            acc = d if acc is None else acc + d
        if bias is not None:
            acc = acc + refs[2 * n_pairs][...]
        if act == 'sigmoid':
            acc = 1.0 / (1.0 + jnp.exp(-acc))
        o_ref[...] = acc.astype(out_dtype)

    out_spec = pl.BlockSpec((tm, tn), lambda i, j: (i, j))
    out_shape = jax.ShapeDtypeStruct((M, N), out_dtype)
    if exchange is not None:
        return _call_carrying(body, exchange, name, (M // tm, N // tn), in_specs, [out_spec], [out_shape], [], args)
    return pl.pallas_call(
        body, name=name, grid=(M // tm, N // tn), in_specs=in_specs, out_specs=out_spec, out_shape=out_shape,
        compiler_params=_params(("parallel", "parallel")),
    )(*args)


def _rows_call(body, name, row_ins, vec_ins, row_outs, acc_outs, ts):
    S = row_ins[0].shape[0]
    in_specs = [pl.BlockSpec((ts, a.shape[1]), lambda i: (i, 0)) for a in row_ins]
    in_specs += [pl.BlockSpec(a.shape, lambda i: (0, 0)) for a in vec_ins]
    out_specs = [pl.BlockSpec((ts, c), lambda i: (i, 0)) for c, _ in row_outs]
    out_specs += [pl.BlockSpec(s, lambda i: (0, 0)) for s in acc_outs]
    out_shape = [jax.ShapeDtypeStruct((S, c), dt) for c, dt in row_outs]
    out_shape += [jax.ShapeDtypeStruct(s, F32) for s in acc_outs]
    return pl.pallas_call(
        body, name=name, grid=(S // ts,), in_specs=in_specs, out_specs=out_specs, out_shape=out_shape,
        compiler_params=_params(("arbitrary",)),
    )(*row_ins, *vec_ins)


def _ln_stats(v):
    mu = jnp.mean(v, axis=-1, keepdims=True)
    d = v - mu
    var = jnp.mean(d * d, axis=-1, keepdims=True)
    rstd = lax.rsqrt(var + LN_EPS)
    return d * rstd, rstd


def _ln_bwd(dxhat, xhat, rstd):
    m1 = jnp.mean(dxhat, axis=-1, keepdims=True)
    m2 = jnp.mean(dxhat * xhat, axis=-1, keepdims=True)
    return rstd * (dxhat - m1 - xhat * m2)


def _colsum(v):
    return jnp.sum(v, axis=0, keepdims=True)


def _ln_mod(x, ada8, ts):
    D = x.shape[1]

    def body(x_ref, v_ref, u_ref):
        xhat, _ = _ln_stats(x_ref[...])
        u_ref[...] = (xhat * (1.0 + v_ref[1:2, :]) + v_ref[0:1, :]).astype(BF16)

    return _rows_call(body, "ln_mod", [x], [ada8], [(D, BF16)], [], ts)[0]


def _gate_mix_out(g_sb, g_fx, y_sb, y_fx, wo, ts):
    D = y_sb.shape[1]

    def body(gs, gf, ys, yf, w_ref, mi_ref, mix_ref):
        mi = (gs[...] * ys[...] + gf[...] * yf[...]).astype(BF16)
        mi_ref[...] = mi
        mix_ref[...] = _dot(mi, w_ref[...])

    return _rows_call(body, "gate_mix_out", [g_sb, g_fx, y_sb, y_fx], [wo], [(D, BF16), (D, F32)], [], ts)


def _post_attn(x, mix, ada8, lnp8, ts):
    D = x.shape[1]

    def body(x_ref, mix_ref, v_ref, p_ref, x1_ref, u2_ref):
        r1 = ALPHA * x_ref[...] + v_ref[2:3, :] * mix_ref[...]
        xhat, _ = _ln_stats(r1)
        x1 = xhat * p_ref[0:1, :] + p_ref[1:2, :]
        x1_ref[...] = x1
        xh1, _ = _ln_stats(x1)
        u2_ref[...] = (xh1 * (1.0 + v_ref[4:5, :]) + v_ref[3:4, :]).astype(BF16)

    return _rows_call(body, "post_attn", [x, mix], [ada8, lnp8], [(D, F32), (D, BF16)], [], ts)


def _loss_head(x1, hin, wfd, target, ada8, lnp8, ts):
    D = x1.shape[1]

    def body(x1_ref, hin_ref, t_ref, v_ref, p_ref, w_ref, dr2_ref, dh_ref, st_ref):
        @pl.when(pl.program_id(0) == 0)
        def _():
            st_ref[...] = jnp.zeros_like(st_ref)

        hv = _dot(hin_ref[...], w_ref[...])
        g2 = v_ref[5:6, :]
        r2 = ALPHA * x1_ref[...] + g2 * hv
        xhat, rstd = _ln_stats(r2)
        y = xhat * p_ref[2:3, :] + p_ref[3:4, :]
        err = y - t_ref[...]
        dy = err * (1.0 / D)
        dr2 = _ln_bwd(dy * p_ref[2:3, :], xhat, rstd)
        dr2_ref[...] = dr2
        dh_ref[...] = (dr2 * g2).astype(BF16)
        st_ref[0:1, :] += _colsum(err * err) * (0.5 / D)
        st_ref[1:2, :] += _colsum(dy * xhat)
        st_ref[2:3, :] += _colsum(dy)
        st_ref[3:4, :] += _colsum(dr2 * hv)

    return _rows_call(body, "loss_head", [x1, hin, target], [ada8, lnp8, wfd], [(D, F32), (D, BF16)], [(8, D)], ts)


def _mid_bwd(du2, x1, dr2, mix, x, ada8, lnp8, ts):
    D = x.shape[1]

    def body(du2_ref, x1_ref, dr2_ref, mix_ref, x_ref, v_ref, p_ref, dr1_ref, dmix_ref, st_ref):
        @pl.when(pl.program_id(0) == 0)
        def _():
            st_ref[...] = jnp.zeros_like(st_ref)

        du2v = du2_ref[...]
        xh1, rstd1 = _ln_stats(x1_ref[...])
        dx1 = ALPHA * dr2_ref[...] + _ln_bwd(du2v * (1.0 + v_ref[4:5, :]), xh1, rstd1)
        mixv = mix_ref[...]
        g1 = v_ref[2:3, :]
        r1 = ALPHA * x_ref[...] + g1 * mixv
        xhr, rstdr = _ln_stats(r1)
        dr1 = _ln_bwd(dx1 * p_ref[0:1, :], xhr, rstdr)
        dr1_ref[...] = dr1
        dmix_ref[...] = (dr1 * g1).astype(BF16)
        st_ref[0:1, :] += _colsum(du2v * xh1)
        st_ref[1:2, :] += _colsum(du2v)
        st_ref[2:3, :] += _colsum(dx1 * xhr)
        st_ref[3:4, :] += _colsum(dx1)
        st_ref[4:5, :] += _colsum(dr1 * mixv)

    return _rows_call(body, "mid_bwd", [du2, x1, dr2, mix, x], [ada8, lnp8], [(D, F32), (D, BF16)], [(8, D)], ts)


def _first_bwd(pairs, x, dr1, ada8, exchange):
    S, D = x.shape
    norm = [(p[0], p[1]) + ((p[2], p[3], p[4]) if len(p) > 2 else (p[0].shape[1], 0, 0)) for p in pairs]
    tm = _pick(S, (256, 128))
    in_specs, args = [], []
    for a, b, K, ka, kb in norm:
        in_specs += [pl.BlockSpec((tm, K), lambda i, j, ka=ka: (i, ka)), pl.BlockSpec((D, K), lambda i, j, kb=kb: (0, kb))]
        args += [a, b]
    rows = pl.BlockSpec((tm, D), lambda i, j: (i, 0))
    sums = pl.BlockSpec((8, D), lambda i, j: (0, 0))
    n_pairs = len(norm)

    def body(*refs):
        x_ref, dr1_ref, v_ref, gx_ref, st_ref = refs[2 * n_pairs:]

        @pl.when(pl.program_id(0) == 0)
        def _():
            st_ref[...] = jnp.zeros_like(st_ref)

        du1 = None
        for p in range(n_pairs):
            d = _dot(refs[2 * p][...].astype(BF16), refs[2 * p + 1][...].astype(BF16), 1, 1)
            du1 = d if du1 is None else du1 + d
        xh0, rstd0 = _ln_stats(x_ref[...])
        gx_ref[...] = ALPHA * dr1_ref[...] + _ln_bwd(du1 * (1.0 + v_ref[1:2, :]), xh0, rstd0)
        st_ref[0:1, :] += _colsum(du1 * xh0)
        st_ref[1:2, :] += _colsum(du1)

    return _call_carrying(
        body, exchange, "first_bwd", (S // tm, 1), in_specs + [rows, rows, sums], [rows, sums],
        [jax.ShapeDtypeStruct((S, D), F32), jax.ShapeDtypeStruct((8, D), F32)], [], (*args, x, dr1, ada8),
        semantics=("arbitrary", "arbitrary"))


def _split3(v):
    hi = v.astype(BF16)
    r = v - hi.astype(F32)
    mid = r.astype(BF16)
    lo = (r - mid.astype(F32)).astype(BF16)
    return hi, mid, lo


def _fgate_fwd(f, bf_pad, tb):
    S = f.shape[0]

    def body(f_ref, b_ref, fc_ref, carry):
        @pl.when(pl.program_id(0) == 0)
        def _():
            carry[...] = jnp.zeros_like(carry)

        z = f_ref[...] + b_ref[...]
        ls = jnp.minimum(z, 0.0) - jnp.log(1.0 + jnp.exp(-jnp.abs(z)))
        r = lax.broadcasted_iota(jnp.int32, (tb, tb), 0)
        c = lax.broadcasted_iota(jnp.int32, (tb, tb), 1)
        tri = (c <= r).astype(BF16)
        hi, mid, lo = _split3(ls)
        cs = _dot(tri, hi) + _dot(tri, mid) + _dot(tri, lo) + carry[...]
        fc_ref[...] = cs
        carry[...] = cs[tb - 1:tb, :]

    return pl.pallas_call(
        body, name="fgate_fwd", grid=(S // tb,),
        in_specs=[pl.BlockSpec((tb, 128), lambda i: (i, 0)), pl.BlockSpec((1, 128), lambda i: (0, 0))],
        out_specs=pl.BlockSpec((tb, 128), lambda i: (i, 0)),
        out_shape=jax.ShapeDtypeStruct((S, 128), F32),
        scratch_shapes=[pltpu.VMEM((1, 128), F32)],
        compiler_params=_params(("arbitrary",)),
    )(f, bf_pad)


def _fgate_bwd(dfc, f, bf_pad, tb):
    S = f.shape[0]
    nb = S // tb

    def body(d_ref, f_ref, b_ref, df_ref, gb_ref, carry):
        @pl.when(pl.program_id(0) == 0)
        def _():
            carry[...] = jnp.zeros_like(carry)
            gb_ref[...] = jnp.zeros_like(gb_ref)

        r = lax.broadcasted_iota(jnp.int32, (tb, tb), 0)
        c = lax.broadcasted_iota(jnp.int32, (tb, tb), 1)
        tri = (c >= r).astype(BF16)
        hi, mid, lo = _split3(d_ref[...])
        rs = _dot(tri, hi) + _dot(tri, mid) + _dot(tri, lo) + carry[...]
        carry[...] = rs[0:1, :]
        z = f_ref[...] + b_ref[...]
        df = rs * (1.0 / (1.0 + jnp.exp(z)))
        df_ref[...] = df
        gb_ref[0:1, :] += _colsum(df)

    return pl.pallas_call(
        body, name="fgate_bwd", grid=(nb,),
        in_specs=[pl.BlockSpec((tb, 128), lambda i: (nb - 1 - i, 0)),
                  pl.BlockSpec((tb, 128), lambda i: (nb - 1 - i, 0)),
                  pl.BlockSpec((1, 128), lambda i: (0, 0))],
        out_specs=[pl.BlockSpec((tb, 128), lambda i: (nb - 1 - i, 0)), pl.BlockSpec((8, 128), lambda i: (0, 0))],
        out_shape=[jax.ShapeDtypeStruct((S, 128), F32), jax.ShapeDtypeStruct((8, 128), F32)],
        scratch_shapes=[pltpu.VMEM((1, 128), F32)],
        compiler_params=_params(("arbitrary",)),
    )(dfc, f, bf_pad)


def _split2(v):
    hi = v.astype(BF16)
    lo = (v - hi.astype(F32)).astype(BF16)
    return hi, lo


def _head_masks():
    lane = lax.broadcasted_iota(jnp.int32, (1, PAIR), 1)
    m0 = lane < HEAD_DIM
    return m0, jnp.logical_not(m0)


def _sel(mask, v):
    return jnp.where(mask, v, jnp.zeros_like(v))


def _softplus(z):
    return jnp.maximum(z, 0.0) + jnp.log(1.0 + jnp.exp(-jnp.abs(z)))


def _qkv_specs(S, tq, n_pairs, base):
    return [pl.BlockSpec((tq, PAIR), lambda p, i: (i, base + p)),
            pl.BlockSpec((S, PAIR), lambda p, i: (0, base + n_pairs + p)),
            pl.BlockSpec((S, PAIR), lambda p, i: (0, base + 2 * n_pairs + p))]


NEG = -1e30


def _fox_specs(S, tq, n_pairs, base):
    return _qkv_specs(S, tq, n_pairs, base) + [
        pl.BlockSpec((tq, PAIR), lambda p, i: (i, p)),
        pl.BlockSpec((1, S // tq, 8, tq), lambda p, i: (p, 0, 0, 0))]


RC = 32
VANISH = -104.0


def _chunks(n_rows, fn):
    for ci in range(n_rows // RC):
        fn(ci * RC)


def _wide(v, tk):
    return v if tk == 128 else jnp.tile(v, (1, tk // 128))


def _rep(col):
    return jnp.broadcast_to(col, (col.shape[0], 128))


def _per_head(blk, masks):
    sw = pltpu.roll(blk, HEAD_DIM, axis=1)
    return jnp.where(masks[0], blk, sw), jnp.where(masks[0], sw, blk)


def _fill_masked(dst_ref, src_ref, masks, mul=None, ones_lane=None):
    v = src_ref[...]
    if mul is not None:
        v = v * mul
    lane = lax.broadcasted_iota(jnp.int32, (1, PAIR), 1)
    for h in range(2):
        m = _sel(masks[h], v)
        if ones_lane is not None:
            m = jnp.where(lane == ones_lane[h], jnp.ones_like(m), m)
        dst_ref[h] = m


def _tri(tk, cmp):
    r = lax.broadcasted_iota(jnp.int32, (tk, tk), 0)
    c = lax.broadcasted_iota(jnp.int32, (tk, tk), 1)
    return cmp(r, c).astype(BF16)


def _diag_mask(r0, tk, strict):
    row = r0 + lax.broadcasted_iota(jnp.int32, (RC, tk), 0)
    col = lax.broadcasted_iota(jnp.int32, (RC, tk), 1)
    return (col < row) if strict else (col <= row)


def _peer_copies(src_ref, land_ref, send_sems, recv_sems, scatter, receive_side):
    x, y, c = lax.axis_index("x"), lax.axis_index("y"), lax.axis_index("c")
    me = 4 * x + 2 * y + c
    copies = []
    for k in range(1, N_DEV):
        px, py, pc = (1 - x if k & 4 else x), (1 - y if k & 2 else y), (1 - c if k & 1 else c)
        slot = 4 * px + 2 * py + pc
        copies.append(pltpu.make_async_remote_copy(
            src_ref=src_ref.at[slot] if scatter else src_ref,
            dst_ref=land_ref.at[slot] if receive_side else land_ref.at[me],
            send_sem=send_sems.at[k - 1], recv_sem=recv_sems.at[k - 1], device_id=(px, py, pc), device_id_type=MESH))
    return copies


def _call_carrying(body, exchange, name, grid, in_specs, out_specs, out_shape, scratch_shapes, args,
                   semantics=("parallel", "arbitrary")):
    if exchange is None:
        return pl.pallas_call(body, name=name, grid=grid, in_specs=in_specs, out_specs=out_specs, out_shape=out_shape,
                              scratch_shapes=scratch_shapes, compiler_params=_params(semantics))(*args)
    exchanges = [exchange] if isinstance(exchange, tuple) else list(exchange)
    n_in, n_out, n_ex = len(in_specs), len(out_specs), len(exchanges)

    def carrying(*refs):
        srcs = refs[n_in:n_in + n_ex]
        lands = refs[n_in + n_ex + n_out:n_in + 2 * n_ex + n_out]
        sems = refs[len(refs) - 2 * n_ex:]

        def copies(receive_side):
            return [cp for e, (_, scatter) in enumerate(exchanges)
                    for cp in _peer_copies(srcs[e], lands[e], sems[2 * e], sems[2 * e + 1], scatter, receive_side)]

        first = jnp.logical_and(pl.program_id(0) == 0, pl.program_id(1) == 0)
        last = jnp.logical_and(pl.program_id(0) == grid[0] - 1, pl.program_id(1) == grid[1] - 1)

        @pl.when(first)
        def _():
            for cp in copies(False):
                cp.start()

        body(*refs[:n_in], *refs[n_in + n_ex:n_in + n_ex + n_out], *refs[n_in + 2 * n_ex + n_out:len(refs) - 2 * n_ex])

        @pl.when(last)
        def _():
            for cp in copies(True):
                cp.wait_send()
                cp.wait_recv()

    any_space = pl.BlockSpec(memory_space=pl.ANY)
    lands = [jax.ShapeDtypeStruct((N_DEV,) + src.shape[-2:], src.dtype) for src, _ in exchanges]
    return pl.pallas_call(
        carrying, name=name, grid=grid, in_specs=list(in_specs) + [any_space] * n_ex,
        out_specs=list(out_specs) + [any_space] * n_ex, out_shape=list(out_shape) + lands,
        scratch_shapes=list(scratch_shapes) + [pltpu.SemaphoreType.DMA((N_DEV - 1,))] * (2 * n_ex),
        compiler_params=_params(("arbitrary", "arbitrary")))(*args, *[src for src, _ in exchanges])


def _staggered(bodies):
    active, waiting = [], list(bodies)
    while waiting or active:
        if waiting:
            active.append(waiting.pop(0))
        for g in list(active):
            try:
                next(g)
            except StopIteration:
                active.remove(g)


def _streams(tile, j, diag, slot):
    return [tile(j, diag, slot, (0, 1))]


def _tiles(i, tile):
    def step(jj, carry):
        _staggered(_streams(tile, 2 * jj, False, 0) + _streams(tile, 2 * jj + 1, False, 1))
        return carry
    lax.fori_loop(0, (i - 1) // 2, step, 0)

    @pl.when(jnp.logical_and(i >= 1, (i - 1) % 2 == 1))
    def _():
        _staggered(_streams(tile, i - 2, False, 0))

    @pl.when(i >= 1)
    def _():
        _staggered(_streams(tile, i - 1, False, 0) + _streams(tile, i, True, 1))

    @pl.when(i == 0)
    def _():
        _staggered(_streams(tile, i, True, 1))


def _tiles_reversed(i, tile, keep_going):
    @pl.when(i == 0)
    def _():
        _staggered(_streams(tile, i, True, 0))

    @pl.when(i >= 1)
    def _():
        _staggered(_streams(tile, i, True, 0) + _streams(tile, i - 1, False, 1))

    pairs = (i - 1) // 2

    def cond(carry):
        jj, go = carry
        return jnp.logical_and(jj < pairs, go)

    def step(carry):
        jj, _ = carry
        _staggered(_streams(tile, i - 2 - 2 * jj, False, 0) + _streams(tile, i - 3 - 2 * jj, False, 1))
        return jj + 1, keep_going(jnp.maximum(i - 4 - 2 * jj, 0))

    jj, go = lax.while_loop(cond, step, (jnp.int32(0), keep_going(jnp.maximum(i - 2, 0))))

    @pl.when(jnp.logical_and(jnp.logical_and(i >= 1, (i - 1) % 2 == 1), jnp.logical_and(jj == pairs, go)))
    def _():
        _staggered(_streams(tile, 0, False, 0))


def _sb_fwd(qkv, n_pairs, base, tq, exchange=None):
    S = qkv.shape[0]
    tk = tq
    scale = HEAD_DIM ** -0.5

    def body(q_ref, k_ref, v_ref, o_ref, t_ref, z_ref, hi_ref, suf_ref, p_ref, r_ref, acc_ref, vm_ref):
        i = pl.program_id(1)
        masks = _head_masks()

        @pl.when(i == 0)
        def _():
            _fill_masked(vm_ref, v_ref, masks)

        q2 = q_ref[...] * scale
        qm = [_sel(m, q2) for m in masks]
        incl = _tri(tk, lambda r, c: r >= c)
        r_ref[...] = jnp.zeros_like(r_ref)
        acc_ref[...] = jnp.zeros_like(acc_ref)

        def tile(j, diag, slot, heads):
            off = pl.multiple_of(j * tk, tk)
            k2 = k_ref[pl.ds(off, tk), :]
            v2 = v_ref[pl.ds(off, tk), :]
            for h in heads:
                z_ref[2 * slot + h] = _dot(qm[h], k2, 1, 1)
            yield
            for h in heads:
                def split(r0, h=h):
                    rows = pl.ds(r0, RC)
                    lg = -_softplus(z_ref[2 * slot + h, rows, :])
                    if diag:
                        lg = jnp.where(_diag_mask(r0, tk, True), lg, 0.0)
                    hi_ref[2 * slot + h, rows, :] = lg.astype(BF16)
                _chunks(tq, split)
            yield
            for h in heads:
                suf_ref[2 * slot + h] = _dot(hi_ref[2 * slot + h], incl)
            yield
            for h in heads:
                def weights(r0, h=h):
                    rows = pl.ds(r0, RC)
                    a = jnp.exp(z_ref[2 * slot + h, rows, :] + suf_ref[2 * slot + h, rows, :] + _wide(r_ref[h, rows, :], tk))
                    if diag:
                        a = jnp.where(_diag_mask(r0, tk, True), a, 0.0)
                    p_ref[2 * slot + h, rows, :] = a.astype(BF16)
                _chunks(tq, weights)
            yield
            keys = pl.ds(off, tk)
            for h in heads:
                acc_ref[...] += _dot(p_ref[2 * slot + h], vm_ref[h, keys, :])
            for h in heads:
                r_ref[h] += _rep(suf_ref[2 * slot + h, :, 0:1])

        _tiles_reversed(i, tile, lambda nearest: jnp.max(r_ref[...]) >= VANISH)
        o_ref[...] = acc_ref[...].astype(BF16)
        t_ref[...] = acc_ref[...]

    W = n_pairs * PAIR
    return _call_carrying(
        body, exchange, "sb_fwd", (n_pairs, S // tq), _qkv_specs(S, tq, n_pairs, base),
        [pl.BlockSpec((tq, PAIR), lambda p, i: (i, p)), pl.BlockSpec((tq, PAIR), lambda p, i: (i, p))],
        [jax.ShapeDtypeStruct((S, W), BF16), jax.ShapeDtypeStruct((S, W), F32)],
        [pltpu.VMEM((4, tq, tk), F32), pltpu.VMEM((4, tq, tk), BF16),
         pltpu.VMEM((4, tq, tk), F32), pltpu.VMEM((4, tq, tk), BF16), pltpu.VMEM((2, tq, 128), F32),
         pltpu.VMEM((tq, PAIR), F32), pltpu.VMEM((2, S, PAIR), BF16)],
        (qkv, qkv, qkv))


def _sb_bwd(qkv, o32, do, n_pairs, base, tq, exchange=None):
    S = qkv.shape[0]
    tk = tq
    nq = S // tq
    scale = HEAD_DIM ** -0.5

    def body(q_ref, k_ref, v_ref, o_ref, do_ref, dq_ref, dk_ref, dv_ref,
             z_ref, g_ref, omb_ref, cum_ref, hi_ref, lo_ref, a_ref, dz_ref,
             r_ref, cg_ref, dl_ref, dq_acc, dk_acc, dv_acc, ks_ref):
        i = pl.program_id(1)
        masks = _head_masks()

        @pl.when(i == 0)
        def _():
            dk_acc[...] = jnp.zeros_like(dk_acc)
            dv_acc[...] = jnp.zeros_like(dv_acc)
            _fill_masked(ks_ref, k_ref, masks, mul=scale)

        q2 = q_ref[...] * scale
        do2 = do_ref[...]
        qm = [_sel(m, q2) for m in masks]
        dom = [_sel(m, do2) for m in masks]
        prod = do2.astype(F32) * o_ref[...]
        for h in range(2):
            dl_ref[h] = _rep(jnp.sum(jnp.where(masks[h], prod, 0.0), axis=-1, keepdims=True))
        suffix = _tri(tk, lambda r, c: r >= c)
        r_ref[...] = jnp.zeros_like(r_ref)
        cg_ref[...] = jnp.zeros_like(cg_ref)
        dq_acc[...] = jnp.zeros_like(dq_acc)

        def tile(j, diag, slot, heads):
            off = pl.multiple_of(j * tk, tk)
            k2 = k_ref[pl.ds(off, tk), :]
            v2 = v_ref[pl.ds(off, tk), :]
            for h in heads:
                z_ref[2 * slot + h] = _dot(qm[h], k2, 1, 1)
                g_ref[2 * slot + h] = _dot(dom[h], v2, 1, 1)
            yield
            for h in heads:
                def split(r0, h=h):
                    rows = pl.ds(r0, RC)
                    sp = _softplus(z_ref[2 * slot + h, rows, :])
                    omb_ref[2 * slot + h, rows, :] = jnp.exp(-sp)
                    lg = -sp
                    if diag:
                        lg = jnp.where(_diag_mask(r0, tk, True), lg, 0.0)
                    hi_ref[2 * slot + h, rows, :] = lg.astype(BF16)
                _chunks(tq, split)
            yield
            for h in heads:
                cum_ref[2 * slot + h] = _dot(hi_ref[2 * slot + h], suffix)
            yield
            for h in heads:
                def weights(r0, h=h):
                    rows = pl.ds(r0, RC)
                    a = jnp.exp(z_ref[2 * slot + h, rows, :] + cum_ref[2 * slot + h, rows, :] + _wide(r_ref[h, rows, :], tk))
                    if diag:
                        a = jnp.where(_diag_mask(r0, tk, True), a, 0.0)
                    ab = a.astype(BF16)
                    g = g_ref[2 * slot + h, rows, :] * ab.astype(F32)
                    g_ref[2 * slot + h, rows, :] = g
                    a_ref[2 * slot + h, rows, :] = ab
                    hi, lo = _split2(g)
                    hi_ref[2 * slot + h, rows, :] = hi
                    lo_ref[2 * slot + h, rows, :] = lo
                _chunks(tq, weights)
            for h in heads:
                r_ref[h] += _rep(cum_ref[2 * slot + h, :, 0:1])
            yield
            for h in heads:
                cum_ref[2 * slot + h] = _dot(hi_ref[2 * slot + h], suffix) + _dot(lo_ref[2 * slot + h], suffix)
            yield
            for h in heads:
                def dscore(r0, h=h):
                    rows = pl.ds(r0, RC)
                    g = g_ref[2 * slot + h, rows, :]
                    from_here = cum_ref[2 * slot + h, rows, :] + _wide(cg_ref[h, rows, :], tk)
                    before = _wide(dl_ref[h, rows, :], tk) - from_here
                    omb = omb_ref[2 * slot + h, rows, :]
                    dz = g * omb - (1.0 - omb) * before
                    if diag:
                        dz = jnp.where(_diag_mask(r0, tk, True), dz, 0.0)
                    dz_ref[2 * slot + h, rows, :] = dz.astype(BF16)
                _chunks(tq, dscore)
            for h in heads:
                cg_ref[h] += _rep(cum_ref[2 * slot + h, :, 0:1])
            yield
            keys = pl.ds(off, tk)
            for h in heads:
                dq_acc[...] += _dot(dz_ref[2 * slot + h], ks_ref[h, keys, :])
                dk_acc[keys, :] += _dot(dz_ref[2 * slot + h], qm[h], 0, 0)
                dv_acc[keys, :] += _dot(a_ref[2 * slot + h], dom[h], 0, 0)

        _tiles_reversed(i, tile, lambda nearest: jnp.max(r_ref[...]) >= VANISH)
        dq_ref[...] = dq_acc[...].astype(BF16)

        @pl.when(i == nq - 1)
        def _():
            dk_ref[...] = dk_acc[...].astype(BF16)
            dv_ref[...] = dv_acc[...].astype(BF16)

    W = n_pairs * PAIR
    in_specs = _qkv_specs(S, tq, n_pairs, base) + [
        pl.BlockSpec((tq, PAIR), lambda p, i: (i, p)),
        pl.BlockSpec((tq, PAIR), lambda p, i: (i, p))]
    out_specs = [pl.BlockSpec((tq, PAIR), lambda p, i: (i, p)),
                 pl.BlockSpec((S, PAIR), lambda p, i: (0, p)),
                 pl.BlockSpec((S, PAIR), lambda p, i: (0, p))]
    big, stat = (4, tq, tk), (2, tq, 128)
    return _call_carrying(
        body, exchange, "sb_bwd", (n_pairs, nq), in_specs, out_specs, [jax.ShapeDtypeStruct((S, W), BF16)] * 3,
        [pltpu.VMEM(big, F32)] * 4 + [pltpu.VMEM(big, BF16)] * 4 + [pltpu.VMEM(stat, F32)] * 3
        + [pltpu.VMEM((tq, PAIR), F32), pltpu.VMEM((S, PAIR), F32), pltpu.VMEM((S, PAIR), F32),
           pltpu.VMEM((2, S, PAIR), BF16)],
        (qkv, qkv, qkv, o32, do))


def _fox_fwd(qkv, fcx, fcr, n_pairs, base, tq, exchange=None):
    S = qkv.shape[0]
    tk = tq
    scale = HEAD_DIM ** -0.5
    spare = (HEAD_DIM, 0)

    def body(q_ref, k_ref, v_ref, fq_ref, fk_ref, o_ref, lse_ref, s_ref, p_ref, m_ref, al_ref, fqr_ref, acc_ref, vm_ref):
        i = pl.program_id(1)
        masks = _head_masks()

        @pl.when(i == 0)
        def _():
            _fill_masked(vm_ref, v_ref, masks, ones_lane=spare)

        q2 = q_ref[...] * scale
        qm = [_sel(m, q2) for m in masks]
        f0, f1 = _per_head(fq_ref[...], masks)
        fqr_ref[0] = f0
        fqr_ref[1] = f1
        m_ref[...] = jnp.full(m_ref.shape, NEG, F32)
        acc_ref[...] = jnp.zeros_like(acc_ref)

        def tile(j, diag, slot, heads):
            off = pl.multiple_of(j * tk, tk)
            k2 = k_ref[pl.ds(off, tk), :]
            v2 = v_ref[pl.ds(off, tk), :]
            fk2 = fk_ref[0, j]
            for h in heads:
                s_ref[2 * slot + h] = _dot(qm[h], k2, 1, 1)
            yield
            for h in heads:
                fk_row = fk2[h:h + 1, :]

                def probs(r0, h=h, fk_row=fk_row):
                    rows = pl.ds(r0, RC)
                    sv = s_ref[2 * slot + h, rows, :] - fk_row
                    if diag:
                        sv = jnp.where(_diag_mask(r0, tk, False), sv, NEG)
                    fq = fqr_ref[h, rows, :]
                    m_prev = m_ref[h, rows, :]
                    m_new = jnp.maximum(m_prev, jnp.max(sv, axis=-1, keepdims=True) + fq)
                    p_ref[2 * slot + h, rows, :] = jnp.exp(sv + _wide(fq - m_new, tk)).astype(BF16)
                    al_ref[2 * slot + h, rows, :] = jnp.exp(m_prev - m_new)
                    m_ref[h, rows, :] = m_new
                _chunks(tq, probs)
            yield
            for h in heads:
                acc_ref[h] = acc_ref[h] * al_ref[2 * slot + h] + _dot(p_ref[2 * slot + h], vm_ref[h, pl.ds(off, tk), :])

        _tiles(i, tile)
        a0, a1 = acc_ref[0], acc_ref[1]
        l0 = _rep(a0[:, spare[0]:spare[0] + 1])
        l1 = _rep(a1[:, spare[1]:spare[1] + 1])
        o_ref[...] = jnp.where(masks[0], a0 / l0, a1 / l1).astype(BF16)
        lse_ref[...] = jnp.where(masks[0], m_ref[0] + jnp.log(l0), m_ref[1] + jnp.log(l1))

    W = n_pairs * PAIR
    return _call_carrying(
        body, exchange, "fox_fwd", (n_pairs, S // tq), _fox_specs(S, tq, n_pairs, base),
        [pl.BlockSpec((tq, PAIR), lambda p, i: (i, p)), pl.BlockSpec((tq, PAIR), lambda p, i: (i, p))],
        [jax.ShapeDtypeStruct((S, W), BF16), jax.ShapeDtypeStruct((S, W), F32)],
        [pltpu.VMEM((4, tq, tk), F32), pltpu.VMEM((4, tq, tk), BF16), pltpu.VMEM((2, tq, 128), F32),
         pltpu.VMEM((4, tq, 128), F32), pltpu.VMEM((2, tq, 128), F32), pltpu.VMEM((2, tq, 128), F32),
         pltpu.VMEM((2, S, PAIR), BF16)],
        (qkv, qkv, qkv, fcx, fcr))


def _fox_bwd(qkv, fcx, fcr, o, lse, do, n_pairs, base, tq, exchange=None):
    S = qkv.shape[0]
    tk = tq
    nq = S // tq
    scale = HEAD_DIM ** -0.5

    def body(q_ref, k_ref, v_ref, fq_ref, fk_ref, o_ref, lse_ref, do_ref,
             dq_ref, dk_ref, dv_ref, dfq_ref, dfk_ref,
             s_ref, dp_ref, p_ref, ds_ref, row_ref, dl_ref, dfq_acc, col_ref, dq_acc, dk_acc, dv_acc, ks_ref):
        i = pl.program_id(1)
        masks = _head_masks()

        @pl.when(i == 0)
        def _():
            dk_acc[...] = jnp.zeros_like(dk_acc)
            dv_acc[...] = jnp.zeros_like(dv_acc)
            dfk_ref[...] = jnp.zeros_like(dfk_ref)
            _fill_masked(ks_ref, k_ref, masks, mul=scale)

        q2 = q_ref[...] * scale
        do2 = do_ref[...]
        qm = [_sel(m, q2) for m in masks]
        dom = [_sel(m, do2) for m in masks]
        f0, f1 = _per_head(fq_ref[...], masks)
        l0, l1 = _per_head(lse_ref[...], masks)
        row_ref[0] = f0 - l0
        row_ref[1] = f1 - l1
        prod = do2.astype(F32) * o_ref[...].astype(F32)
        for h in range(2):
            dl_ref[h] = _rep(jnp.sum(jnp.where(masks[h], prod, 0.0), axis=-1, keepdims=True))
        dfq_acc[...] = jnp.zeros_like(dfq_acc)
        dq_acc[...] = jnp.zeros_like(dq_acc)

        def tile(j, diag, slot, heads):
            off = pl.multiple_of(j * tk, tk)
            k2 = k_ref[pl.ds(off, tk), :]
            v2 = v_ref[pl.ds(off, tk), :]
            fk2 = fk_ref[0, j]
            for h in heads:
                s_ref[2 * slot + h] = _dot(qm[h], k2, 1, 1)
                dp_ref[2 * slot + h] = _dot(dom[h], v2, 1, 1)
            yield
            for h in heads:
                col_ref[2 * slot + h] = jnp.zeros((8, tk), F32)
                fk_row = fk2[h:h + 1, :]

                def dscore(r0, h=h, fk_row=fk_row):
                    rows = pl.ds(r0, RC)
                    p = jnp.exp(s_ref[2 * slot + h, rows, :] - fk_row + _wide(row_ref[h, rows, :], tk))
                    if diag:
                        p = jnp.where(_diag_mask(r0, tk, False), p, 0.0)
                    ds = p * (dp_ref[2 * slot + h, rows, :] - _wide(dl_ref[h, rows, :], tk))
                    p_ref[2 * slot + h, rows, :] = p.astype(BF16)
                    ds_ref[2 * slot + h, rows, :] = ds.astype(BF16)
                    dfq_acc[h, rows, :] += _rep(jnp.sum(ds, axis=-1, keepdims=True))
                    col_ref[2 * slot + h] += jnp.sum(ds.reshape(RC // 8, 8, tk), axis=0)
                _chunks(tq, dscore)
            yield
            keys = pl.ds(off, tk)
            for h in heads:
                dq_acc[...] += _dot(ds_ref[2 * slot + h], ks_ref[h, keys, :])
                dk_acc[keys, :] += _dot(ds_ref[2 * slot + h], qm[h], 0, 0)
                dv_acc[keys, :] += _dot(p_ref[2 * slot + h], dom[h], 0, 0)
            for h in heads:
                dfk_ref[0, j, h:h + 1, :] += jnp.sum(col_ref[2 * slot + h], axis=0, keepdims=True)

        _tiles(i, tile)
        dq_ref[...] = dq_acc[...].astype(BF16)
        dfq_ref[...] = jnp.where(masks[0], dfq_acc[0], dfq_acc[1])

        @pl.when(i == nq - 1)
        def _():
            dk_ref[...] = dk_acc[...].astype(BF16)
            dv_ref[...] = dv_acc[...].astype(BF16)

    W = n_pairs * PAIR
    in_specs = _fox_specs(S, tq, n_pairs, base) + [
        pl.BlockSpec((tq, PAIR), lambda p, i: (i, p)),
        pl.BlockSpec((tq, PAIR), lambda p, i: (i, p)),
        pl.BlockSpec((tq, PAIR), lambda p, i: (i, p))]
    out_specs = [pl.BlockSpec((tq, PAIR), lambda p, i: (i, p)),
                 pl.BlockSpec((S, PAIR), lambda p, i: (0, p)),
                 pl.BlockSpec((S, PAIR), lambda p, i: (0, p)),
                 pl.BlockSpec((tq, PAIR), lambda p, i: (i, p)),
                 pl.BlockSpec((1, nq, 8, tk), lambda p, i: (p, 0, 0, 0))]
    return _call_carrying(
        body, exchange, "fox_bwd", (n_pairs, nq), in_specs, out_specs,
        [jax.ShapeDtypeStruct((S, W), BF16)] * 3
        + [jax.ShapeDtypeStruct((S, W), F32), jax.ShapeDtypeStruct((n_pairs, nq, 8, tk), F32)],
        [pltpu.VMEM((4, tq, tk), F32)] * 2 + [pltpu.VMEM((4, tq, tk), BF16)] * 2
        + [pltpu.VMEM((2, tq, 128), F32)] * 3 + [pltpu.VMEM((4, 8, tk), F32)]
        + [pltpu.VMEM((tq, PAIR), F32), pltpu.VMEM((S, PAIR), F32), pltpu.VMEM((S, PAIR), F32),
           pltpu.VMEM((2, S, PAIR), BF16)],
        (qkv, qkv, qkv, fcx, fcr, o, lse, do))


def _col_chunks(n, width=256):
    return [(c, min(width, n - c)) for c in range(0, n, width)]


def _swiglu_fwd(u2, wg, wu):
    S, D = u2.shape
    FF = wg.shape[1]
    tm, tn = _pick(S, (512, 256, 128)), _divisors(FF, 1536)[0]

    def body(u_ref, g_ref, w_ref, a_ref, b_ref, h_ref):
        u = u_ref[...]
        for c, w in _col_chunks(tn):
            cols = slice(c, c + w)
            a = _dot(u, g_ref[:, cols])
            b = _dot(u, w_ref[:, cols])
            a_ref[:, cols] = a.astype(BF16)
            b_ref[:, cols] = b.astype(BF16)
            h_ref[:, cols] = (a / (1.0 + jnp.exp(-a)) * b).astype(BF16)

    spec_o = pl.BlockSpec((tm, tn), lambda i, j: (i, j))
    return pl.pallas_call(
        body, name="swiglu_fwd", grid=(S // tm, FF // tn),
        in_specs=[pl.BlockSpec((tm, D), lambda i, j: (i, 0)),
                  pl.BlockSpec((D, tn), lambda i, j: (0, j)),
                  pl.BlockSpec((D, tn), lambda i, j: (0, j))],
        out_specs=[spec_o] * 3, out_shape=[jax.ShapeDtypeStruct((S, FF), BF16)] * 3,
        compiler_params=_params(("parallel", "parallel")),
    )(u2, wg, wu)


def _swiglu_bwd(dh, wd, a, b):
    S, D = dh.shape
    FF = wd.shape[0]
    tm, tn = _pick(S, (512, 256, 128)), _divisors(FF, 1536)[0]

    def body(dh_ref, w_ref, a_ref, b_ref, da_ref, db_ref):
        dh_blk = dh_ref[...]
        for c, w in _col_chunks(tn):
            cols = slice(c, c + w)
            dhin = _dot(dh_blk, w_ref[cols, :], 1, 1)
            av = a_ref[:, cols].astype(F32)
            bv = b_ref[:, cols].astype(F32)
            sig = 1.0 / (1.0 + jnp.exp(-av))
            da_ref[:, cols] = (dhin * bv * (sig * (1.0 + av * (1.0 - sig)))).astype(BF16)
            db_ref[:, cols] = (dhin * (av * sig)).astype(BF16)

    spec_o = pl.BlockSpec((tm, tn), lambda i, j: (i, j))
    return pl.pallas_call(
        body, name="swiglu_bwd", grid=(S // tm, FF // tn),
        in_specs=[pl.BlockSpec((tm, D), lambda i, j: (i, 0)),
                  pl.BlockSpec((tn, D), lambda i, j: (j, 0)), spec_o, spec_o],
        out_specs=[spec_o] * 2, out_shape=[jax.ShapeDtypeStruct((S, FF), BF16)] * 2,
        compiler_params=_params(("parallel", "parallel")),
    )(dh, wd, a, b)


def _gate_bwd(dmix, wo, g_sb, g_fx, y_sb, y_fx):
    S, D = dmix.shape
    tm, tn = _pick(S, (512, 256, 128)), _divisors(D, 1024)[0]

    def body(dm_ref, w_ref, gs_ref, gf_ref, ys_ref, yf_ref, dys_ref, dyf_ref, dls_ref, dlf_ref, bs_ref, bf_ref):
        @pl.when(pl.program_id(1) == 0)
        def _():
            bs_ref[...] = jnp.zeros_like(bs_ref)
            bf_ref[...] = jnp.zeros_like(bf_ref)

        dm_blk = dm_ref[...]
        for c, w in _col_chunks(tn):
            cols = slice(c, c + w)
            dmi = _dot(dm_blk, w_ref[cols, :], 1, 1)
            gs, gf = gs_ref[:, cols], gf_ref[:, cols]
            dys_ref[:, cols] = (dmi * gs).astype(BF16)
            dyf_ref[:, cols] = (dmi * gf).astype(BF16)
            dls = dmi * ys_ref[:, cols] * gs * (1.0 - gs)
            dlf = dmi * yf_ref[:, cols] * gf * (1.0 - gf)
            dls_ref[:, cols] = dls.astype(BF16)
            dlf_ref[:, cols] = dlf.astype(BF16)
            bs_ref[0:1, cols] += _colsum(dls)
            bf_ref[0:1, cols] += _colsum(dlf)

    t = pl.BlockSpec((tm, tn), lambda j, i: (i, j))
    accs = pl.BlockSpec((8, tn), lambda j, i: (0, j))
    return pl.pallas_call(
        body, name="gate_bwd", grid=(D // tn, S // tm),
        in_specs=[pl.BlockSpec((tm, D), lambda j, i: (i, 0)),
                  pl.BlockSpec((tn, D), lambda j, i: (j, 0)), t, t, t, t],
        out_specs=[t, t, t, t, accs, accs],
        out_shape=[jax.ShapeDtypeStruct((S, D), BF16)] * 4 + [jax.ShapeDtypeStruct((8, D), F32)] * 2,
        compiler_params=_params(("parallel", "arbitrary")),
    )(dmix, wo, g_sb, g_fx, y_sb, y_fx)


def _local_step(x, target, ada8, lnp8, bg_sb, bg_fx, bf_pad, wqkv, wf, wgs, wgf, gather, later_weights, pack_early,
                pack_last):
    S, D = x.shape
    W = wqkv.shape[1] // 6
    n_pairs = W // PAIR
    n_heads = W // HEAD_DIM
    ts = _pick(S, (512, 256, 128))
    tq = _pick(S, (256, 128))

    u1 = _ln_mod(x, ada8, ts)
    qkv = _mm([(u1, wqkv)], 'nn', BF16, "in_qkv")
    f = _mm([(u1, wf)], 'nn', F32, "in_f")
    g_sb = _mm([(u1, wgs)], 'nn', F32, "in_gsb", bias=bg_sb, act='sigmoid')
    g_fx = _mm([(u1, wgf)], 'nn', F32, "in_gfx", bias=bg_fx, act='sigmoid')
    fc = _fgate_fwd(f, bf_pad, _pick(S, (512, 256, 128)))
    fch = fc[:, :n_heads]
    fcx = jnp.repeat(fch, HEAD_DIM, axis=1)
    nq = S // tq
    fcr = jnp.pad(fch.T.reshape(n_pairs, 2, nq, tq).transpose(0, 2, 1, 3),
                  ((0, 0), (0, 0), (0, 6), (0, 0)))
    o_sb, o_sb32, *zone_a = _sb_fwd(qkv, n_pairs, 0, tq, gather[0])
    o_fx, lse, *zone_b = _fox_fwd(qkv, fcx, fcr, n_pairs, 3 * n_pairs, tq, gather[1])
    wsb, wfx, wo, wfg, wfu, wfd = later_weights(*zone_a, *zone_b)
    y_sb =_mm([(o_sb, wsb)], 'nn', F32, "out_sb")
    y_fx = _mm([(o_fx, wfx)], 'nn', F32, "out_fx")
    mix_in, mix = _gate_mix_out(g_sb, g_fx, y_sb, y_fx, wo, ts)
    x1, u2 = _post_attn(x, mix, ada8, lnp8, ts)
    a, b, hin = _swiglu_fwd(u2, wfg, wfu)
    dr2, dh, st_loss = _loss_head(x1, hin, wfd, target, ada8, lnp8, ts)

    da, db = _swiglu_bwd(dh, wfd, a, b)
    g_wfd = _mm([(hin, dh)], 'tn', F32, "g_ffn_down")
    du2 = _mm([(da, wfg), (db, wfu)], 'nt', F32, "d_u2")
    g_wfg = _mm([(u2, da)], 'tn', F32, "g_ffn_gate")
    g_wfu = _mm([(u2, db)], 'tn', F32, "g_ffn_up")
    dr1, dmix, st_mid = _mid_bwd(du2, x1, dr2, mix, x, ada8, lnp8, ts)
    dys, dyf, dls, dlf, gb_sb, gb_fx = _gate_bwd(dmix, wo, g_sb, g_fx, y_sb, y_fx)
    g_wo = _mm([(mix_in, dmix)], 'tn', F32, "g_w_o")
    do_sb = _mm([(dys, wsb)], 'nt', BF16, "d_o_sb")
    do_fx = _mm([(dyf, wfx)], 'nt', BF16, "d_o_fx")
    g_wsb = _mm([(o_sb, dys)], 'tn', F32, "g_sb_out")
    g_wfx = _mm([(o_fx, dyf)], 'tn', F32, "g_fox_out")
    scatter = pack_early(dict(sb=g_wsb, fx=g_wfx, o=g_wo, fg=g_wfg, fu=g_wfu, fd=g_wfd))
    dq_s, dk_s, dv_s, *zone_a = _sb_bwd(qkv, o_sb32, do_sb, n_pairs, 0, tq, scatter[0])
    dq_f, dk_f, dv_f, dfq, dfk, *zone_b = _fox_bwd(qkv, fcx, fcr, o_fx, lse, do_fx, n_pairs, 3 * n_pairs, tq,
                                                    scatter[1])
    early = [(scatter[0], zone_a[0] if zone_a else None), (scatter[1], zone_b[0] if zone_b else None)]
    dfc = dfq[:, ::HEAD_DIM] - dfk[:, :, :2, :].transpose(0, 2, 1, 3).reshape(n_heads, S).T
    dfc = jnp.pad(dfc, ((0, 0), (0, 128 - n_heads)))
    df, gb_f = _fgate_bwd(dfc, f, bf_pad, _pick(S, (512, 256, 128)))
    grads = [dq_s, dk_s, dv_s, dq_f, dk_f, dv_f]
    g_wqkv = [_mm([(u1, g)], 'tn', F32, "g_in_%d" % n) for n, g in enumerate(grads)]
    g_wf = _mm([(u1, df)], 'tn', F32, "g_in_f")
    g_wgs = _mm([(u1, dls)], 'tn', F32, "g_in_gsb")
    g_wgf = _mm([(u1, dlf)], 'tn', F32, "g_in_gfx")
    wgrads = dict(qkv=g_wqkv, f=g_wf, gs=g_wgs, gf=g_wgf)
    last = pack_last(wgrads)
    gx, st_first, *last_zone = _first_bwd(
        [(g, wqkv, W, 0, n) for n, g in enumerate(grads)] + [(df, wf), (dls, wgs), (dlf, wgf)], x, dr1, ada8, last)
    last_zone = last_zone[0] if last_zone else None

    stats = dict(loss=st_loss, mid=st_mid, first=st_first, gb_sb=gb_sb, gb_fx=gb_fx, gb_f=gb_f)
    return gx, wgrads, stats, early, (last, last_zone)


def _position():
    x, y, c = lax.axis_index("x"), lax.axis_index("y"), lax.axis_index("c")
    return x, y, c, 4 * x + 2 * y + c


def _flip(x, y, c, k):
    px = 1 - x if k & 4 else x
    py = 1 - y if k & 2 else y
    pc = 1 - c if k & 1 else c
    return (px, py, pc), 4 * px + 2 * py + pc


def _all_gather_small(v, name):
    r, n = v.shape

    def body(x_ref, out_ref, send_sems, recv_sems, local_sem):
        x, y, c, me = _position()
        mine = pltpu.make_async_copy(x_ref, out_ref.at[me], local_sem)
        mine.start()
        sends = []
        for k in range(1, N_DEV):
            peer, _ = _flip(x, y, c, k)
            cp = pltpu.make_async_remote_copy(
                src_ref=x_ref, dst_ref=out_ref.at[me], send_sem=send_sems.at[k - 1], recv_sem=recv_sems.at[k - 1],
                device_id=peer, device_id_type=MESH)
            cp.start()
            sends.append(cp)
        for k in range(1, N_DEV):
            peer, slot = _flip(x, y, c, k)
            pltpu.make_async_remote_copy(
                src_ref=x_ref, dst_ref=out_ref.at[slot], send_sem=send_sems.at[k - 1], recv_sem=recv_sems.at[k - 1],
                device_id=peer, device_id_type=MESH).wait_recv()
        for cp in sends:
            cp.wait_send()
        mine.wait()

    return pl.pallas_call(
        body, name=name, out_shape=jax.ShapeDtypeStruct((N_DEV, r, n), v.dtype),
        in_specs=[pl.BlockSpec(memory_space=pltpu.VMEM)], out_specs=pl.BlockSpec(memory_space=pltpu.VMEM),
        scratch_shapes=[pltpu.SemaphoreType.DMA((N_DEV - 1,)), pltpu.SemaphoreType.DMA((N_DEV - 1,)),
                        pltpu.SemaphoreType.DMA],
    )(v)


def _all_gather_weights(packed):
    R, C = packed.shape

    def body(x_ref, out_ref, send_sems, recv_sems, local_sem):
        x, y, c, me = _position()
        sibling, sib_slot = _flip(x, y, c, 1)
        mine = pltpu.make_async_copy(x_ref, out_ref.at[me], local_sem)
        mine.start()

        def copy(k, slot, to, src=None):
            return pltpu.make_async_remote_copy(
                src_ref=out_ref.at[slot] if src is None else src, dst_ref=out_ref.at[slot],
                send_sem=send_sems.at[k], recv_sem=recv_sems.at[k], device_id=to, device_id_type=MESH)

        first = [copy(0, me, sibling, src=x_ref)]
        chips = (4, 2, 6)
        for n, k in enumerate(chips):
            peer, _ = _flip(x, y, c, k)
            first.append(copy(1 + n, me, peer, src=x_ref))
        for cp in first:
            cp.start()
        passed = []
        for n, k in enumerate(chips):
            peer, slot = _flip(x, y, c, k)
            copy(1 + n, slot, peer).wait_recv()
            cp = copy(4 + n, slot, sibling)
            cp.start()
            passed.append(cp)
        copy(0, sib_slot, sibling).wait_recv()
        for n, k in enumerate(chips):
            _, slot = _flip(x, y, c, k | 1)
            copy(4 + n, slot, sibling).wait_recv()
        for cp in first + passed:
            cp.wait_send()
        mine.wait()

    return pl.pallas_call(
        body, name="all_gather_weights", out_shape=jax.ShapeDtypeStruct((N_DEV, R, C), packed.dtype),
        in_specs=[pl.BlockSpec(memory_space=pl.ANY)], out_specs=pl.BlockSpec(memory_space=pl.ANY),
        scratch_shapes=[pltpu.SemaphoreType.DMA((7,)), pltpu.SemaphoreType.DMA((7,)), pltpu.SemaphoreType.DMA],
    )(packed)


def _own_slot(land, own):
    me = 4 * lax.axis_index("x") + 2 * lax.axis_index("y") + lax.axis_index("c")
    return lax.dynamic_update_slice(land, own[None], (me, 0, 0))


def _sum_slots(recv, name, tr):
    n, R, C = recv.shape

    def body(r_ref, o_ref):
        acc = r_ref[0].astype(F32)
        for s in range(1, n):
            acc = acc + r_ref[s].astype(F32)
        o_ref[...] = acc

    return pl.pallas_call(
        body, name=name, grid=(R // tr,), in_specs=[pl.BlockSpec((n, tr, C), lambda i: (0, i, 0))],
        out_specs=pl.BlockSpec((tr, C), lambda i: (i, 0)), out_shape=jax.ShapeDtypeStruct((R, C), F32),
        compiler_params=_params(("parallel",)),
    )(recv)


def _sum_stats(st_all, loss_row):
    n, r, D = st_all.shape

    def body(s_ref, o_ref, l_ref):
        acc = s_ref[0]
        for d in range(1, n):
            acc = acc + s_ref[d]
        o_ref[...] = acc
        l_ref[...] = jnp.zeros((8, 128), F32) + jnp.sum(acc[loss_row:loss_row + 1, :], axis=-1, keepdims=True)

    return pl.pallas_call(
        body, name="sum_stats", out_shape=[jax.ShapeDtypeStruct((r, D), F32), jax.ShapeDtypeStruct((8, 128), F32)],
    )(st_all)


def _adamw(w, g, m, v, name):
    R, C = w.shape
    tr = _pick(R, (256, 176, 128, 64, 32, 16, 8))
    c1 = 1.0 / (1.0 - ADAM_B1 ** ADAM_STEP)
    c2 = 1.0 / (1.0 - ADAM_B2 ** ADAM_STEP)

    def body(w_ref, g_ref, m_ref, v_ref, d_ref, nm_ref, nv_ref):
        gv = g_ref[...]
        nm = ADAM_B1 * m_ref[...] + (1.0 - ADAM_B1) * gv
        nv = ADAM_B2 * v_ref[...] + (1.0 - ADAM_B2) * (gv * gv)
        nm_ref[...] = nm
        nv_ref[...] = nv
        d_ref[...] = -ADAM_LR * ((nm * c1) / (jnp.sqrt(nv * c2) + ADAM_EPS) + ADAM_WD * w_ref[...])

    spec = pl.BlockSpec((tr, C), lambda i: (i, 0))
    return pl.pallas_call(
        body, name=name, grid=(R // tr,), in_specs=[spec] * 4, out_specs=[spec] * 3,
        out_shape=[jax.ShapeDtypeStruct((R, C), F32)] * 3, compiler_params=_params(("parallel",)),
    )(w, g, m, v)


def _round16(n):
    return -(-n // 16) * 16


def _pack_layout(D, in_cols, ff, W):
    parts = [("in", D * (in_cols // N_DEV) // D), ("fg", ff // N_DEV), ("fu", ff // N_DEV),
             ("sb", W * (D // N_DEV) // D), ("fx", W * (D // N_DEV) // D), ("o", D // N_DEV), ("fd", ff // N_DEV)]
    layout, off = {}, 0
    for nm, rows in parts:
        layout[nm] = (off, rows)
        off += _round16(rows)
    return layout, off


def _rows_of(a, D, rows):
    a = a.reshape(rows, D)
    return jnp.pad(a, ((0, _round16(rows) - rows), (0, 0)))


def _cols_to_dest(g, D):
    K, N = g.shape
    n = N // N_DEV
    return g.reshape(K, N_DEV, n).transpose(1, 0, 2).reshape(N_DEV, K * n // D, D)


def _cols_from_src(blocks, K, n):
    return blocks.reshape(N_DEV, K, n).transpose(1, 0, 2).reshape(K, N_DEV * n)


def _pad_rows16(a):
    rows = a.shape[1]
    return jnp.pad(a, ((0, 0), (0, _round16(rows) - rows), (0, 0)))


def kernel(x, c, w_ada, b_ada, w_in, b_gate, b_forget, w_sb_out, w_fox_out, w_o, ln1_g, ln1_b, w_ffn_gate, w_ffn_up, w_ffn_down, ln2_g, ln2_b, loss_target, m_w_ada, m_b_ada, m_w_in, m_b_gate, m_b_forget, m_w_sb_out, m_w_fox_out, m_w_o, m_ln1_g, m_ln1_b, m_w_ffn_gate, m_w_ffn_up, m_w_ffn_down, m_ln2_g, m_ln2_b, v_w_ada, v_b_ada, v_w_in, v_b_gate, v_b_forget, v_w_sb_out, v_w_fox_out, v_w_o, v_ln1_g, v_ln1_b, v_w_ffn_gate, v_w_ffn_up, v_w_ffn_down, v_ln2_g, v_ln2_b):
    S, D = x.shape[1], x.shape[2]
    W = w_sb_out.shape[1]
    n_heads = b_forget.shape[1]
    ff = w_ffn_down.shape[1] * N_DEV
    in_loc = w_in.shape[2]
    in_cols = in_loc * N_DEV
    ada_loc = w_ada.shape[2]
    n_cond = ada_loc * N_DEV // D
    assert w_ada.shape[0] == 1 and n_cond == 6 and in_cols == 6 * W + n_heads + 2 * D and n_heads <= 128
    me = 4 * lax.axis_index("x") + 2 * lax.axis_index("y") + lax.axis_index("c")

    c_all = _all_gather_small(c, "gather_c").reshape(N_DEV, D)
    c16 = jnp.pad(c_all, ((0, 16 - N_DEV), (0, 0)))
    b_cols = lax.dynamic_slice(b_ada, (0, me * ada_loc), (1, ada_loc))
    ada_cols = _mm([(c16, w_ada[0])], 'nn', F32, "ada_fwd", bias=b_cols, silu_a=True)[:N_DEV]
    ada_all = _all_gather_small(ada_cols, "gather_ada")
    ada_me = lax.dynamic_index_in_dim(ada_all, me, axis=1, keepdims=False)
    ada8 = jnp.pad(ada_me.reshape(n_cond, D), ((0, 8 - n_cond), (0, 0)))
    lnp8 = jnp.concatenate([ln1_g, ln1_b, ln2_g, ln2_b, jnp.zeros((4, D), F32)], axis=0)

    layout, R = _pack_layout(D, in_cols, ff, W)
    shards = dict(**{"in": w_in[0]}, fg=w_ffn_gate[0], fu=w_ffn_up[0], sb=w_sb_out[0], fx=w_fox_out[0], o=w_o[0],
                  fd=w_ffn_down[0])
    rows_fwd, rows_bwd = ("sb", "fx", "o"), ("fd", "o", "sb", "fx")
    rows_of = {nm: layout[nm][1] for nm in layout}

    def offsets(names):
        offs, off = {}, 0
        for nm in names:
            offs[nm] = off
            off += _round16(rows_of[nm])
        return offs

    def as_rows(names):
        return jnp.concatenate([_rows_of(shards[nm].astype(BF16), D, rows_of[nm]) for nm in names], axis=0)

    def whole_from(blocks):
        return blocks.transpose(1, 0, 2).reshape(blocks.shape[1], N_DEV * blocks.shape[2])

    def blocks_of(g):
        return g.reshape(g.shape[0], N_DEV, g.shape[1] // N_DEV).transpose(1, 0, 2)

    gather_src = [as_rows(rows_fwd),
                  jnp.concatenate([shards["fg"].astype(BF16), shards["fu"].astype(BF16)], axis=0), as_rows(("fd",))]
    gathered_in = _all_gather_weights(shards["in"].astype(BF16))

    w_in_full = whole_from(gathered_in)
    wqkv = w_in_full[:, :6 * W]
    wf = jnp.pad(w_in_full[:, 6 * W:6 * W + n_heads], ((0, 0), (0, 128 - n_heads)))
    wgs = w_in_full[:, 6 * W + n_heads:6 * W + n_heads + D]
    wgf = w_in_full[:, 6 * W + n_heads + D:]
    bf_pad = jnp.pad(b_forget, ((0, 0), (0, 128 - n_heads)))

    def later_weights(zone_rows, zone_gate_up, zone_down):
        rows, offs = _own_slot(zone_rows, gather_src[0]), offsets(rows_fwd)
        part = {nm: rows[:, offs[nm]:offs[nm] + rows_of[nm], :] for nm in rows_fwd}
        gate_up = _own_slot(zone_gate_up, gather_src[1])
        down = _own_slot(zone_down, gather_src[2])[:, :rows_of["fd"], :]
        return (_cols_from_src(part["sb"], W, D // N_DEV), _cols_from_src(part["fx"], W, D // N_DEV),
                part["o"].reshape(D, D), whole_from(gate_up[:, :D, :]), whole_from(gate_up[:, D:, :]),
                down.reshape(ff, D))

    def pack_early(g):
        dest = {"sb": _cols_to_dest(g["sb"], D), "fx": _cols_to_dest(g["fx"], D),
                "o": g["o"].reshape(N_DEV, D // N_DEV, D), "fd": g["fd"].reshape(N_DEV, ff // N_DEV, D)}
        rows = jnp.concatenate([_pad_rows16(dest[nm].astype(BF16)) for nm in rows_bwd], axis=1)
        gate_up = jnp.concatenate([blocks_of(g["fg"].astype(BF16)), blocks_of(g["fu"].astype(BF16))], axis=1)
        return [(rows, True), (gate_up, True)]

    def pack_last(g):
        g_in = jnp.concatenate(g["qkv"] + [g["f"][:, :n_heads], g["gs"], g["gf"]], axis=1)
        return blocks_of(g_in.astype(BF16)), True

    gx, wg, st, early, ((pack_in, _), land_in) = _local_step(
        x[0], loss_target[0], ada8, lnp8, b_gate[:, :D], b_gate[:, D:], bf_pad, wqkv, wf, wgs, wgf,
        [(gather_src[0], False), [(gather_src[1], False), (gather_src[2], False)]],
        later_weights, pack_early, pack_last)

    def summed(zone, sent, name):
        own = lax.dynamic_index_in_dim(sent, me, axis=0, keepdims=False)
        rows = zone.shape[1]
        block = max(t for t in range(16, 705, 16) if rows % t == 0)
        return _sum_slots(_own_slot(zone, own), name, block)

    ((sent_rows, _), zone_rows), ((sent_gate_up, _), zone_gate_up) = early
    gate_up = summed(zone_gate_up, sent_gate_up, "sum_grads_gate_up")
    gsum = {"in": summed(land_in, pack_in, "sum_grads_in"), "fg": gate_up[:D], "fu": gate_up[D:]}
    total, offs = summed(zone_rows, sent_rows, "sum_grads_rows"), offsets(rows_bwd)
    for nm in rows_bwd:
        gsum[nm] = total[offs[nm]:offs[nm] + rows_of[nm]]

    def gshard(nm, shape):
        return gsum[nm].reshape(shape)

    zrow = jnp.zeros((1, D), F32)
    gb_f_row = jnp.pad(st["gb_f"][0:1], ((0, 0), (0, D - 128)))
    stats16 = jnp.concatenate([
        st["first"][1:2], st["first"][0:1], st["mid"][4:5], st["mid"][1:2], st["mid"][0:1], st["loss"][3:4],
        st["mid"][2:3], st["mid"][3:4], st["loss"][1:2], st["loss"][2:3], st["gb_sb"][0:1], st["gb_fx"][0:1],
        st["loss"][0:1], gb_f_row, zrow, zrow], axis=0)
    st_all = _all_gather_small(stats16, "gather_stats")
    st_sum, loss_blk = _sum_stats(st_all, 12)
    loss = loss_blk[0, 0]

    d_ada_all = st_all[:, :n_cond, :].reshape(N_DEV, n_cond * D)
    d_cols = lax.dynamic_slice(d_ada_all, (0, me * ada_loc), (N_DEV, ada_loc))
    d16 = jnp.pad(d_cols, ((0, 16 - N_DEV), (0, 0)))
    g_w_ada = _mm([(c16, d16)], 'tn', F32, "ada_wgrad", silu_a=True)

    small_w = jnp.concatenate([b_ada.reshape(n_cond, D), ln1_g, ln1_b, ln2_g, ln2_b, b_gate.reshape(2, D), zrow,
                               jnp.pad(b_forget, ((0, 0), (0, D - n_heads))), zrow, zrow], axis=0)
    small_m = jnp.concatenate([m_b_ada.reshape(n_cond, D), m_ln1_g, m_ln1_b, m_ln2_g, m_ln2_b, m_b_gate.reshape(2, D),
                               zrow, jnp.pad(m_b_forget, ((0, 0), (0, D - n_heads))), zrow, zrow], axis=0)
    small_v = jnp.concatenate([v_b_ada.reshape(n_cond, D), v_ln1_g, v_ln1_b, v_ln2_g, v_ln2_b, v_b_gate.reshape(2, D),
                               zrow, jnp.pad(v_b_forget, ((0, 0), (0, D - n_heads))), zrow, zrow], axis=0)
    sm = _adamw(small_w, st_sum, small_m, small_v, "adamw_small")

    def small(a, nm):
        if nm == "b_ada":
            return a[0:n_cond].reshape(1, n_cond * D)
        if nm == "b_gate":
            return a[10:12].reshape(1, 2 * D)
        if nm == "b_forget":
            return a[13:14, :n_heads]
        row = {"ln1_g": 6, "ln1_b": 7, "ln2_g": 8, "ln2_b": 9}[nm]
        return a[row:row + 1]

    big = {
        "w_ada": (w_ada[0], g_w_ada, m_w_ada[0], v_w_ada[0]),
        "w_in": (w_in[0], gshard("in", w_in.shape[1:]), m_w_in[0], v_w_in[0]),
        "w_sb_out": (w_sb_out[0], gshard("sb", w_sb_out.shape[1:]), m_w_sb_out[0], v_w_sb_out[0]),
        "w_fox_out": (w_fox_out[0], gshard("fx", w_fox_out.shape[1:]), m_w_fox_out[0], v_w_fox_out[0]),
        "w_o": (w_o[0], gshard("o", w_o.shape[1:]), m_w_o[0], v_w_o[0]),
        "w_ffn_gate": (w_ffn_gate[0], gshard("fg", w_ffn_gate.shape[1:]), m_w_ffn_gate[0], v_w_ffn_gate[0]),
        "w_ffn_up": (w_ffn_up[0], gshard("fu", w_ffn_up.shape[1:]), m_w_ffn_up[0], v_w_ffn_up[0]),
        "w_ffn_down": (w_ffn_down[0], gshard("fd", w_ffn_down.shape[1:]), m_w_ffn_down[0], v_w_ffn_down[0]),
    }
    order = ["w_ada", "b_ada", "w_in", "b_gate", "b_forget", "w_sb_out", "w_fox_out", "w_o", "ln1_g", "ln1_b",
             "w_ffn_gate", "w_ffn_up", "w_ffn_down", "ln2_g", "ln2_b"]
    grads, deltas, new_ms, new_vs = [], [], [], []
    for nm in order:
        if nm in big:
            w, g, m, v = big[nm]
            d, nm_, nv_ = _adamw(w, g, m, v, "adamw_" + nm)
            grads.append(g[None])
            deltas.append(d[None])
            new_ms.append(nm_[None])
            new_vs.append(nv_[None])
        else:
            grads.append(small(st_sum, nm))
            deltas.append(small(sm[0], nm))
            new_ms.append(small(sm[1], nm))
            new_vs.append(small(sm[2], nm))
    return (loss, gx[None], *grads, *deltas, *new_ms, *new_vs)
```

```python
import jax
import jax.numpy as jnp
from jax import lax
from jax.experimental import pallas as pl
from jax.experimental.pallas import tpu as pltpu

F32 = jnp.float32
BF16 = jnp.bfloat16

HEAD_DIM = 64
PAIR = 2 * HEAD_DIM
LN_EPS = 1e-5
ALPHA = 2.0 ** 0.25
ADAM_LR, ADAM_B1, ADAM_B2, ADAM_EPS, ADAM_WD, ADAM_STEP = 0.001, 0.9, 0.999, 1e-08, 0.01, 10
N_DEV = 8
VMEM_LIMIT = 56 * 1024 * 1024
MESH = pl.DeviceIdType.MESH


def _dot(a, b, ca=1, cb=0):
    return lax.dot_general(a, b, (((ca,), (cb,)), ((), ())), preferred_element_type=F32)


def _pick(n, cands):
    for t in cands:
        if n % t == 0:
            return t
    return n


def _params(sem):
    return pltpu.CompilerParams(dimension_semantics=sem, vmem_limit_bytes=VMEM_LIMIT)


MM_BLOCK_BYTES = 40 * 1024 * 1024
LANES = 128


def _divisors(n, cap):
    ds = [d for d in range(LANES, min(n, cap) + 1, LANES) if n % d == 0]
    return sorted(ds, reverse=True) or [n]


def _mm_tiles(M, N, a_row_bytes, b_row_bytes, out_itemsize):
    best = None
    for tm in _divisors(M, 1024):
        for tn in _divisors(N, 2048):
            need = 2 * (tm * a_row_bytes + tn * b_row_bytes + tm * tn * out_itemsize) + tm * tn * 4
            if need <= MM_BLOCK_BYTES and (best is None or (tm * tn, tm) > (best[0] * best[1], best[0])):
                best = (tm, tn)
    assert best is not None, (M, N, a_row_bytes, b_row_bytes)
    return best


def _mm(pairs, mode, out_dtype, name, bias=None, act=None, silu_a=False, exchange=None):
    norm = []
    for p in pairs:
        a, b = p[0], p[1]
        kdim_a = a.shape[0] if mode == 'tn' else a.shape[1]
        K, ka, kb = (p[2], p[3], p[4]) if len(p) > 2 else (kdim_a, 0, 0)
        norm.append((a, b, K, ka, kb))
    a0, b0 = norm[0][0], norm[0][1]
    M = a0.shape[1] if mode == 'tn' else a0.shape[0]
    N = b0.shape[0] if mode == 'nt' else b0.shape[1]
    tm, tn = _mm_tiles(M, N, sum(K * a.dtype.itemsize for a, _, K, _, _ in norm),
                       sum(K * b.dtype.itemsize for _, b, K, _, _ in norm), jnp.dtype(out_dtype).itemsize)
    n_pairs = len(norm)

    in_specs, args = [], []
    for a, b, K, ka, kb in norm:
        if mode == 'tn':
            in_specs.append(pl.BlockSpec((K, tm), lambda i, j, ka=ka: (ka, i)))
        else:
            in_specs.append(pl.BlockSpec((tm, K), lambda i, j, ka=ka: (i, ka)))
        if mode == 'nt':
            in_specs.append(pl.BlockSpec((tn, K), lambda i, j, kb=kb: (j, kb)))
        else:
            in_specs.append(pl.BlockSpec((K, tn), lambda i, j, kb=kb: (kb, j)))
        args += [a, b]
    if bias is not None:
        in_specs.append(pl.BlockSpec((1, tn), lambda i, j: (0, j)))
        args.append(bias)

    ca = 0 if mode == 'tn' else 1
    cb = 1 if mode == 'nt' else 0

    def body(*refs):
        o_ref = refs[-1]
        acc = None
        for p in range(n_pairs):
            av = refs[2 * p][...]
            if silu_a:
                av = av / (1.0 + jnp.exp(-av))
            d = _dot(av.astype(BF16), refs[2 * p + 1][...].astype(BF16), ca, cb)
            acc = d if acc is None else acc + d
        if bias is not None:
            acc = acc + refs[2 * n_pairs][...]
        if act == 'sigmoid':
            acc = 1.0 / (1.0 + jnp.exp(-acc))
        o_ref[...] = acc.astype(out_dtype)

    out_spec = pl.BlockSpec((tm, tn), lambda i, j: (i, j))
    out_shape = jax.ShapeDtypeStruct((M, N), out_dtype)
    if exchange is not None:
        return _call_carrying(body, exchange, name, (M // tm, N // tn), in_specs, [out_spec], [out_shape], [], args)
    return pl.pallas_call(
        body, name=name, grid=(M // tm, N // tn), in_specs=in_specs, out_specs=out_spec, out_shape=out_shape,
        compiler_params=_params(("parallel", "parallel")),
    )(*args)


def _rows_call(body, name, row_ins, vec_ins, row_outs, acc_outs, ts):
    S = row_ins[0].shape[0]
    in_specs = [pl.BlockSpec((ts, a.shape[1]), lambda i: (i, 0)) for a in row_ins]
    in_specs += [pl.BlockSpec(a.shape, lambda i: (0, 0)) for a in vec_ins]
    out_specs = [pl.BlockSpec((ts, c), lambda i: (i, 0)) for c, _ in row_outs]
    out_specs += [pl.BlockSpec(s, lambda i: (0, 0)) for s in acc_outs]
    out_shape = [jax.ShapeDtypeStruct((S, c), dt) for c, dt in row_outs]
    out_shape += [jax.ShapeDtypeStruct(s, F32) for s in acc_outs]
    return pl.pallas_call(
        body, name=name, grid=(S // ts,), in_specs=in_specs, out_specs=out_specs, out_shape=out_shape,
        compiler_params=_params(("arbitrary",)),
    )(*row_ins, *vec_ins)


def _ln_stats(v):
    mu = jnp.mean(v, axis=-1, keepdims=True)
    d = v - mu
    var = jnp.mean(d * d, axis=-1, keepdims=True)
    rstd = lax.rsqrt(var + LN_EPS)
    return d * rstd, rstd


def _ln_bwd(dxhat, xhat, rstd):
    m1 = jnp.mean(dxhat, axis=-1, keepdims=True)
    m2 = jnp.mean(dxhat * xhat, axis=-1, keepdims=True)
    return rstd * (dxhat - m1 - xhat * m2)


def _colsum(v):
    return jnp.sum(v, axis=0, keepdims=True)


def _ln_mod(x, ada8, ts):
    D = x.shape[1]

    def body(x_ref, v_ref, u_ref):
        xhat, _ = _ln_stats(x_ref[...])
        u_ref[...] = (xhat * (1.0 + v_ref[1:2, :]) + v_ref[0:1, :]).astype(BF16)

    return _rows_call(body, "ln_mod", [x], [ada8], [(D, BF16)], [], ts)[0]


def _gate_mix_out(g_sb, g_fx, y_sb, y_fx, wo, ts):
    D = y_sb.shape[1]

    def body(gs, gf, ys, yf, w_ref, mi_ref, mix_ref):
        mi = (gs[...] * ys[...] + gf[...] * yf[...]).astype(BF16)
        mi_ref[...] = mi
        mix_ref[...] = _dot(mi, w_ref[...])

    return _rows_call(body, "gate_mix_out", [g_sb, g_fx, y_sb, y_fx], [wo], [(D, BF16), (D, F32)], [], ts)


def _post_attn(x, mix, ada8, lnp8, ts):
    D = x.shape[1]

    def body(x_ref, mix_ref, v_ref, p_ref, x1_ref, u2_ref):
        r1 = ALPHA * x_ref[...] + v_ref[2:3, :] * mix_ref[...]
        xhat, _ = _ln_stats(r1)
        x1 = xhat * p_ref[0:1, :] + p_ref[1:2, :]
        x1_ref[...] = x1
        xh1, _ = _ln_stats(x1)
        u2_ref[...] = (xh1 * (1.0 + v_ref[4:5, :]) + v_ref[3:4, :]).astype(BF16)

    return _rows_call(body, "post_attn", [x, mix], [ada8, lnp8], [(D, F32), (D, BF16)], [], ts)


def _loss_head(x1, hin, wfd, target, ada8, lnp8, ts):
    D = x1.shape[1]

    def body(x1_ref, hin_ref, t_ref, v_ref, p_ref, w_ref, dr2_ref, dh_ref, st_ref):
        @pl.when(pl.program_id(0) == 0)
        def _():
            st_ref[...] = jnp.zeros_like(st_ref)

        hv = _dot(hin_ref[...], w_ref[...])
        g2 = v_ref[5:6, :]
        r2 = ALPHA * x1_ref[...] + g2 * hv
        xhat, rstd = _ln_stats(r2)
        y = xhat * p_ref[2:3, :] + p_ref[3:4, :]
        err = y - t_ref[...]
        dy = err * (1.0 / D)
        dr2 = _ln_bwd(dy * p_ref[2:3, :], xhat, rstd)
        dr2_ref[...] = dr2
        dh_ref[...] = (dr2 * g2).astype(BF16)
        st_ref[0:1, :] += _colsum(err * err) * (0.5 / D)
        st_ref[1:2, :] += _colsum(dy * xhat)
        st_ref[2:3, :] += _colsum(dy)
        st_ref[3:4, :] += _colsum(dr2 * hv)

    return _rows_call(body, "loss_head", [x1, hin, target], [ada8, lnp8, wfd], [(D, F32), (D, BF16)], [(8, D)], ts)


def _mid_bwd(du2, x1, dr2, mix, x, ada8, lnp8, ts):
    D = x.shape[1]

    def body(du2_ref, x1_ref, dr2_ref, mix_ref, x_ref, v_ref, p_ref, dr1_ref, dmix_ref, st_ref):
        @pl.when(pl.program_id(0) == 0)
        def _():
            st_ref[...] = jnp.zeros_like(st_ref)

        du2v = du2_ref[...]
        xh1, rstd1 = _ln_stats(x1_ref[...])
        dx1 = ALPHA * dr2_ref[...] + _ln_bwd(du2v * (1.0 + v_ref[4:5, :]), xh1, rstd1)
        mixv = mix_ref[...]
        g1 = v_ref[2:3, :]
        r1 = ALPHA * x_ref[...] + g1 * mixv
        xhr, rstdr = _ln_stats(r1)
        dr1 = _ln_bwd(dx1 * p_ref[0:1, :], xhr, rstdr)
        dr1_ref[...] = dr1
        dmix_ref[...] = (dr1 * g1).astype(BF16)
        st_ref[0:1, :] += _colsum(du2v * xh1)
        st_ref[1:2, :] += _colsum(du2v)
        st_ref[2:3, :] += _colsum(dx1 * xhr)
        st_ref[3:4, :] += _colsum(dx1)
        st_ref[4:5, :] += _colsum(dr1 * mixv)

    return _rows_call(body, "mid_bwd", [du2, x1, dr2, mix, x], [ada8, lnp8], [(D, F32), (D, BF16)], [(8, D)], ts)


def _first_bwd(pairs, x, dr1, ada8, exchange):
    S, D = x.shape
    norm = [(p[0], p[1]) + ((p[2], p[3], p[4]) if len(p) > 2 else (p[0].shape[1], 0, 0)) for p in pairs]
    tm = _pick(S, (256, 128))
    in_specs, args = [], []
    for a, b, K, ka, kb in norm:
        in_specs += [pl.BlockSpec((tm, K), lambda i, j, ka=ka: (i, ka)), pl.BlockSpec((D, K), lambda i, j, kb=kb: (0, kb))]
        args += [a, b]
    rows = pl.BlockSpec((tm, D), lambda i, j: (i, 0))
    sums = pl.BlockSpec((8, D), lambda i, j: (0, 0))
    n_pairs = len(norm)

    def body(*refs):
        x_ref, dr1_ref, v_ref, gx_ref, st_ref = refs[2 * n_pairs:]

        @pl.when(pl.program_id(0) == 0)
        def _():
            st_ref[...] = jnp.zeros_like(st_ref)

        du1 = None
        for p in range(n_pairs):
            d = _dot(refs[2 * p][...].astype(BF16), refs[2 * p + 1][...].astype(BF16), 1, 1)
            du1 = d if du1 is None else du1 + d
        xh0, rstd0 = _ln_stats(x_ref[...])
        gx_ref[...] = ALPHA * dr1_ref[...] + _ln_bwd(du1 * (1.0 + v_ref[1:2, :]), xh0, rstd0)
        st_ref[0:1, :] += _colsum(du1 * xh0)
        st_ref[1:2, :] += _colsum(du1)

    return _call_carrying(
        body, exchange, "first_bwd", (S // tm, 1), in_specs + [rows, rows, sums], [rows, sums],
        [jax.ShapeDtypeStruct((S, D), F32), jax.ShapeDtypeStruct((8, D), F32)], [], (*args, x, dr1, ada8),
        semantics=("arbitrary", "arbitrary"))


def _split3(v):
    hi = v.astype(BF16)
    r = v - hi.astype(F32)
    mid = r.astype(BF16)
    lo = (r - mid.astype(F32)).astype(BF16)
    return hi, mid, lo


def _fgate_fwd(f, bf_pad, tb):
    S = f.shape[0]

    def body(f_ref, b_ref, fc_ref, carry):
        @pl.when(pl.program_id(0) == 0)
        def _():
            carry[...] = jnp.zeros_like(carry)

        z = f_ref[...] + b_ref[...]
        ls = jnp.minimum(z, 0.0) - jnp.log(1.0 + jnp.exp(-jnp.abs(z)))
        r = lax.broadcasted_iota(jnp.int32, (tb, tb), 0)
        c = lax.broadcasted_iota(jnp.int32, (tb, tb), 1)
        tri = (c <= r).astype(BF16)
        hi, mid, lo = _split3(ls)
        cs = _dot(tri, hi) + _dot(tri, mid) + _dot(tri, lo) + carry[...]
        fc_ref[...] = cs
        carry[...] = cs[tb - 1:tb, :]

    return pl.pallas_call(
        body, name="fgate_fwd", grid=(S // tb,),
        in_specs=[pl.BlockSpec((tb, 128), lambda i: (i, 0)), pl.BlockSpec((1, 128), lambda i: (0, 0))],
        out_specs=pl.BlockSpec((tb, 128), lambda i: (i, 0)),
        out_shape=jax.ShapeDtypeStruct((S, 128), F32),
        scratch_shapes=[pltpu.VMEM((1, 128), F32)],
        compiler_params=_params(("arbitrary",)),
    )(f, bf_pad)


def _fgate_bwd(dfc, f, bf_pad, tb):
    S = f.shape[0]
    nb = S // tb

    def body(d_ref, f_ref, b_ref, df_ref, gb_ref, carry):
        @pl.when(pl.program_id(0) == 0)
        def _():
            carry[...] = jnp.zeros_like(carry)
            gb_ref[...] = jnp.zeros_like(gb_ref)

        r = lax.broadcasted_iota(jnp.int32, (tb, tb), 0)
        c = lax.broadcasted_iota(jnp.int32, (tb, tb), 1)
        tri = (c >= r).astype(BF16)
        hi, mid, lo = _split3(d_ref[...])
        rs = _dot(tri, hi) + _dot(tri, mid) + _dot(tri, lo) + carry[...]
        carry[...] = rs[0:1, :]
        z = f_ref[...] + b_ref[...]
        df = rs * (1.0 / (1.0 + jnp.exp(z)))
        df_ref[...] = df
        gb_ref[0:1, :] += _colsum(df)

    return pl.pallas_call(
        body, name="fgate_bwd", grid=(nb,),
        in_specs=[pl.BlockSpec((tb, 128), lambda i: (nb - 1 - i, 0)),
                  pl.BlockSpec((tb, 128), lambda i: (nb - 1 - i, 0)),
                  pl.BlockSpec((1, 128), lambda i: (0, 0))],
        out_specs=[pl.BlockSpec((tb, 128), lambda i: (nb - 1 - i, 0)), pl.BlockSpec((8, 128), lambda i: (0, 0))],
        out_shape=[jax.ShapeDtypeStruct((S, 128), F32), jax.ShapeDtypeStruct((8, 128), F32)],
        scratch_shapes=[pltpu.VMEM((1, 128), F32)],
        compiler_params=_params(("arbitrary",)),
    )(dfc, f, bf_pad)


def _split2(v):
    hi = v.astype(BF16)
    lo = (v - hi.astype(F32)).astype(BF16)
    return hi, lo


def _head_masks():
    lane = lax.broadcasted_iota(jnp.int32, (1, PAIR), 1)
    m0 = lane < HEAD_DIM
    return m0, jnp.logical_not(m0)


def _sel(mask, v):
    return jnp.where(mask, v, jnp.zeros_like(v))


def _softplus(z):
    return jnp.maximum(z, 0.0) + jnp.log(1.0 + jnp.exp(-jnp.abs(z)))


def _qkv_specs(S, tq, n_pairs, base):
    return [pl.BlockSpec((tq, PAIR), lambda p, i: (i, base + p)),
            pl.BlockSpec((S, PAIR), lambda p, i: (0, base + n_pairs + p)),
            pl.BlockSpec((S, PAIR), lambda p, i: (0, base + 2 * n_pairs + p))]


NEG = -1e30


def _fox_specs(S, tq, n_pairs, base):
    return _qkv_specs(S, tq, n_pairs, base) + [
        pl.BlockSpec((tq, PAIR), lambda p, i: (i, p)),
        pl.BlockSpec((1, S // tq, 8, tq), lambda p, i: (p, 0, 0, 0))]


RC = 32
VANISH = -104.0


def _chunks(n_rows, fn):
    for ci in range(n_rows // RC):
        fn(ci * RC)


def _wide(v, tk):
    return v if tk == 128 else jnp.tile(v, (1, tk // 128))


def _rep(col):
    return jnp.broadcast_to(col, (col.shape[0], 128))


def _per_head(blk, masks):
    sw = pltpu.roll(blk, HEAD_DIM, axis=1)
    return jnp.where(masks[0], blk, sw), jnp.where(masks[0], sw, blk)


def _fill_masked(dst_ref, src_ref, masks, mul=None, ones_lane=None):
    v = src_ref[...]
    if mul is not None:
        v = v * mul
    lane = lax.broadcasted_iota(jnp.int32, (1, PAIR), 1)
    for h in range(2):
        m = _sel(masks[h], v)
        if ones_lane is not None:
            m = jnp.where(lane == ones_lane[h], jnp.ones_like(m), m)
        dst_ref[h] = m


def _tri(tk, cmp):
    r = lax.broadcasted_iota(jnp.int32, (tk, tk), 0)
    c = lax.broadcasted_iota(jnp.int32, (tk, tk), 1)
    return cmp(r, c).astype(BF16)


def _diag_mask(r0, tk, strict):
    row = r0 + lax.broadcasted_iota(jnp.int32, (RC, tk), 0)
    col = lax.broadcasted_iota(jnp.int32, (RC, tk), 1)
    return (col < row) if strict else (col <= row)


def _peer_copies(src_ref, land_ref, send_sems, recv_sems, scatter, receive_side, dests=None):
    x, y, c = lax.axis_index("x"), lax.axis_index("y"), lax.axis_index("c")
    me = 4 * x + 2 * y + c
    lo, hi = dests if dests is not None else (0, N_DEV)
    copies = []
    for k in range(1, N_DEV):
        px, py, pc = (1 - x if k & 4 else x), (1 - y if k & 2 else y), (1 - c if k & 1 else c)
        slot = 4 * px + 2 * py + pc
        block = jnp.clip(slot - lo, 0, hi - lo - 1)
        cp = pltpu.make_async_remote_copy(
            src_ref=src_ref.at[block] if scatter else src_ref,
            dst_ref=land_ref.at[slot] if receive_side else land_ref.at[me],
            send_sem=send_sems.at[k - 1], recv_sem=recv_sems.at[k - 1], device_id=(px, py, pc), device_id_type=MESH)
        if dests is None:
            copies.append((cp, None, None))
        else:
            copies.append((cp, jnp.logical_and(slot >= lo, slot < hi), jnp.logical_and(me >= lo, me < hi)))
    return copies


def _when(cond, fn):
    if cond is None:
        fn()
    else:
        pl.when(cond)(fn)


def _call_carrying(body, exchange, name, grid, in_specs, out_specs, out_shape, scratch_shapes, args,
                   semantics=("parallel", "arbitrary")):
    if exchange is None:
        return pl.pallas_call(body, name=name, grid=grid, in_specs=in_specs, out_specs=out_specs, out_shape=out_shape,
                              scratch_shapes=scratch_shapes, compiler_params=_params(semantics))(*args)
    exchanges = [exchange] if isinstance(exchange, tuple) else list(exchange)
    n_in, n_out, n_ex = len(in_specs), len(out_specs), len(exchanges)

    def carrying(*refs):
        srcs = refs[n_in:n_in + n_ex]
        lands = refs[n_in + n_ex + n_out:n_in + 2 * n_ex + n_out]
        sems = refs[len(refs) - 2 * n_ex:]

        def copies(receive_side):
            return [triple for e, ex in enumerate(exchanges)
                    for triple in _peer_copies(srcs[e], lands[e], sems[2 * e], sems[2 * e + 1], ex[1], receive_side,
                                               ex[2] if len(ex) > 2 else None)]

        first = jnp.logical_and(pl.program_id(0) == 0, pl.program_id(1) == 0)
        last = jnp.logical_and(pl.program_id(0) == grid[0] - 1, pl.program_id(1) == grid[1] - 1)

        @pl.when(first)
        def _():
            for cp, sent, _ in copies(False):
                _when(sent, cp.start)

        body(*refs[:n_in], *refs[n_in + n_ex:n_in + n_ex + n_out], *refs[n_in + 2 * n_ex + n_out:len(refs) - 2 * n_ex])

        @pl.when(last)
        def _():
            for cp, sent, received in copies(True):
                _when(sent, cp.wait_send)
                _when(received, cp.wait_recv)

    any_space = pl.BlockSpec(memory_space=pl.ANY)
    lands = [jax.ShapeDtypeStruct((N_DEV,) + ex[0].shape[-2:], ex[0].dtype) for ex in exchanges]
    return pl.pallas_call(
        carrying, name=name, grid=grid, in_specs=list(in_specs) + [any_space] * n_ex,
        out_specs=list(out_specs) + [any_space] * n_ex, out_shape=list(out_shape) + lands,
        scratch_shapes=list(scratch_shapes) + [pltpu.SemaphoreType.DMA((N_DEV - 1,))] * (2 * n_ex),
        compiler_params=_params(("arbitrary", "arbitrary")))(*args, *[ex[0] for ex in exchanges])


def _staggered(bodies):
    active, waiting = [], list(bodies)
    while waiting or active:
        if waiting:
            active.append(waiting.pop(0))
        for g in list(active):
            try:
                next(g)
            except StopIteration:
                active.remove(g)


def _streams(tile, j, diag, slot):
    return [tile(j, diag, slot, (0, 1))]


def _tiles(i, tile):
    def step(jj, carry):
        _staggered(_streams(tile, 2 * jj, False, 0) + _streams(tile, 2 * jj + 1, False, 1))
        return carry
    lax.fori_loop(0, (i - 1) // 2, step, 0)

    @pl.when(jnp.logical_and(i >= 1, (i - 1) % 2 == 1))
    def _():
        _staggered(_streams(tile, i - 2, False, 0))

    @pl.when(i >= 1)
    def _():
        _staggered(_streams(tile, i - 1, False, 0) + _streams(tile, i, True, 1))

    @pl.when(i == 0)
    def _():
        _staggered(_streams(tile, i, True, 1))


def _tiles_reversed(i, tile, keep_going):
    @pl.when(i == 0)
    def _():
        _staggered(_streams(tile, i, True, 0))

    @pl.when(i >= 1)
    def _():
        _staggered(_streams(tile, i, True, 0) + _streams(tile, i - 1, False, 1))

    pairs = (i - 1) // 2

    def cond(carry):
        jj, go = carry
        return jnp.logical_and(jj < pairs, go)

    def step(carry):
        jj, _ = carry
        _staggered(_streams(tile, i - 2 - 2 * jj, False, 0) + _streams(tile, i - 3 - 2 * jj, False, 1))
        return jj + 1, keep_going(jnp.maximum(i - 4 - 2 * jj, 0))

    jj, go = lax.while_loop(cond, step, (jnp.int32(0), keep_going(jnp.maximum(i - 2, 0))))

    @pl.when(jnp.logical_and(jnp.logical_and(i >= 1, (i - 1) % 2 == 1), jnp.logical_and(jj == pairs, go)))
    def _():
        _staggered(_streams(tile, 0, False, 0))


def _sb_fwd(qkv, n_pairs, base, tq, exchange=None):
    S = qkv.shape[0]
    tk = tq
    scale = HEAD_DIM ** -0.5

    def body(q_ref, k_ref, v_ref, o_ref, t_ref, z_ref, hi_ref, suf_ref, p_ref, r_ref, acc_ref, vm_ref):
        i = pl.program_id(1)
        masks = _head_masks()

        @pl.when(i == 0)
        def _():
            _fill_masked(vm_ref, v_ref, masks)

        q2 = q_ref[...] * scale
        qm = [_sel(m, q2) for m in masks]
        incl = _tri(tk, lambda r, c: r >= c)
        r_ref[...] = jnp.zeros_like(r_ref)
        acc_ref[...] = jnp.zeros_like(acc_ref)

        def tile(j, diag, slot, heads):
            off = pl.multiple_of(j * tk, tk)
            k2 = k_ref[pl.ds(off, tk), :]
            v2 = v_ref[pl.ds(off, tk), :]
            for h in heads:
                z_ref[2 * slot + h] = _dot(qm[h], k2, 1, 1)
            yield
            for h in heads:
                def split(r0, h=h):
                    rows = pl.ds(r0, RC)
                    lg = -_softplus(z_ref[2 * slot + h, rows, :])
                    if diag:
                        lg = jnp.where(_diag_mask(r0, tk, True), lg, 0.0)
                    hi_ref[2 * slot + h, rows, :] = lg.astype(BF16)
                _chunks(tq, split)
            yield
            for h in heads:
                suf_ref[2 * slot + h] = _dot(hi_ref[2 * slot + h], incl)
            yield
            for h in heads:
                def weights(r0, h=h):
                    rows = pl.ds(r0, RC)
                    a = jnp.exp(z_ref[2 * slot + h, rows, :] + suf_ref[2 * slot + h, rows, :] + _wide(r_ref[h, rows, :], tk))
                    if diag:
                        a = jnp.where(_diag_mask(r0, tk, True), a, 0.0)
                    p_ref[2 * slot + h, rows, :] = a.astype(BF16)
                _chunks(tq, weights)
            yield
            keys = pl.ds(off, tk)
            for h in heads:
                acc_ref[...] += _dot(p_ref[2 * slot + h], vm_ref[h, keys, :])
            for h in heads:
                r_ref[h] += _rep(suf_ref[2 * slot + h, :, 0:1])

        _tiles_reversed(i, tile, lambda nearest: jnp.max(r_ref[...]) >= VANISH)
        o_ref[...] = acc_ref[...].astype(BF16)
        t_ref[...] = acc_ref[...]

    W = n_pairs * PAIR
    return _call_carrying(
        body, exchange, "sb_fwd", (n_pairs, S // tq), _qkv_specs(S, tq, n_pairs, base),
        [pl.BlockSpec((tq, PAIR), lambda p, i: (i, p)), pl.BlockSpec((tq, PAIR), lambda p, i: (i, p))],
        [jax.ShapeDtypeStruct((S, W), BF16), jax.ShapeDtypeStruct((S, W), F32)],
        [pltpu.VMEM((4, tq, tk), F32), pltpu.VMEM((4, tq, tk), BF16),
         pltpu.VMEM((4, tq, tk), F32), pltpu.VMEM((4, tq, tk), BF16), pltpu.VMEM((2, tq, 128), F32),
         pltpu.VMEM((tq, PAIR), F32), pltpu.VMEM((2, S, PAIR), BF16)],
        (qkv, qkv, qkv))


def _sb_bwd(qkv, o32, do, n_pairs, base, tq, exchange=None):
    S = qkv.shape[0]
    tk = tq
    nq = S // tq
    scale = HEAD_DIM ** -0.5

    def body(q_ref, k_ref, v_ref, o_ref, do_ref, dq_ref, dk_ref, dv_ref,
             z_ref, g_ref, omb_ref, cum_ref, hi_ref, lo_ref, a_ref, dz_ref,
             r_ref, cg_ref, dl_ref, dq_acc, dk_acc, dv_acc, ks_ref):
        i = pl.program_id(1)
        masks = _head_masks()

        @pl.when(i == 0)
        def _():
            dk_acc[...] = jnp.zeros_like(dk_acc)
            dv_acc[...] = jnp.zeros_like(dv_acc)
            _fill_masked(ks_ref, k_ref, masks, mul=scale)

        q2 = q_ref[...] * scale
        do2 = do_ref[...]
        qm = [_sel(m, q2) for m in masks]
        dom = [_sel(m, do2) for m in masks]
        prod = do2.astype(F32) * o_ref[...]
        for h in range(2):
            dl_ref[h] = _rep(jnp.sum(jnp.where(masks[h], prod, 0.0), axis=-1, keepdims=True))
        suffix = _tri(tk, lambda r, c: r >= c)
        r_ref[...] = jnp.zeros_like(r_ref)
        cg_ref[...] = jnp.zeros_like(cg_ref)
        dq_acc[...] = jnp.zeros_like(dq_acc)

        def tile(j, diag, slot, heads):
            off = pl.multiple_of(j * tk, tk)
            k2 = k_ref[pl.ds(off, tk), :]
            v2 = v_ref[pl.ds(off, tk), :]
            for h in heads:
                z_ref[2 * slot + h] = _dot(qm[h], k2, 1, 1)
                g_ref[2 * slot + h] = _dot(dom[h], v2, 1, 1)
            yield
            for h in heads:
                def split(r0, h=h):
                    rows = pl.ds(r0, RC)
                    sp = _softplus(z_ref[2 * slot + h, rows, :])
                    omb_ref[2 * slot + h, rows, :] = jnp.exp(-sp)
                    lg = -sp
                    if diag:
                        lg = jnp.where(_diag_mask(r0, tk, True), lg, 0.0)
                    hi_ref[2 * slot + h, rows, :] = lg.astype(BF16)
                _chunks(tq, split)
            yield
            for h in heads:
                cum_ref[2 * slot + h] = _dot(hi_ref[2 * slot + h], suffix)
            yield
            for h in heads:
                def weights(r0, h=h):
                    rows = pl.ds(r0, RC)
                    a = jnp.exp(z_ref[2 * slot + h, rows, :] + cum_ref[2 * slot + h, rows, :] + _wide(r_ref[h, rows, :], tk))
                    if diag:
                        a = jnp.where(_diag_mask(r0, tk, True), a, 0.0)
                    ab = a.astype(BF16)
                    g = g_ref[2 * slot + h, rows, :] * ab.astype(F32)
                    g_ref[2 * slot + h, rows, :] = g
                    a_ref[2 * slot + h, rows, :] = ab
                    hi, lo = _split2(g)
                    hi_ref[2 * slot + h, rows, :] = hi
                    lo_ref[2 * slot + h, rows, :] = lo
                _chunks(tq, weights)
            for h in heads:
                r_ref[h] += _rep(cum_ref[2 * slot + h, :, 0:1])
            yield
            for h in heads:
                cum_ref[2 * slot + h] = _dot(hi_ref[2 * slot + h], suffix) + _dot(lo_ref[2 * slot + h], suffix)
            yield
            for h in heads:
                def dscore(r0, h=h):
                    rows = pl.ds(r0, RC)
                    g = g_ref[2 * slot + h, rows, :]
                    from_here = cum_ref[2 * slot + h, rows, :] + _wide(cg_ref[h, rows, :], tk)
                    before = _wide(dl_ref[h, rows, :], tk) - from_here
                    omb = omb_ref[2 * slot + h, rows, :]
                    dz = g * omb - (1.0 - omb) * before
                    if diag:
                        dz = jnp.where(_diag_mask(r0, tk, True), dz, 0.0)
                    dz_ref[2 * slot + h, rows, :] = dz.astype(BF16)
                _chunks(tq, dscore)
            for h in heads:
                cg_ref[h] += _rep(cum_ref[2 * slot + h, :, 0:1])
            yield
            keys = pl.ds(off, tk)
            for h in heads:
                dq_acc[...] += _dot(dz_ref[2 * slot + h], ks_ref[h, keys, :])
                dk_acc[keys, :] += _dot(dz_ref[2 * slot + h], qm[h], 0, 0)
                dv_acc[keys, :] += _dot(a_ref[2 * slot + h], dom[h], 0, 0)

        _tiles_reversed(i, tile, lambda nearest: jnp.max(r_ref[...]) >= VANISH)
        dq_ref[...] = dq_acc[...].astype(BF16)

        @pl.when(i == nq - 1)
        def _():
            dk_ref[...] = dk_acc[...].astype(BF16)
            dv_ref[...] = dv_acc[...].astype(BF16)

    W = n_pairs * PAIR
    in_specs = _qkv_specs(S, tq, n_pairs, base) + [
        pl.BlockSpec((tq, PAIR), lambda p, i: (i, p)),
        pl.BlockSpec((tq, PAIR), lambda p, i: (i, p))]
    out_specs = [pl.BlockSpec((tq, PAIR), lambda p, i: (i, p)),
                 pl.BlockSpec((S, PAIR), lambda p, i: (0, p)),
                 pl.BlockSpec((S, PAIR), lambda p, i: (0, p))]
    big, stat = (4, tq, tk), (2, tq, 128)
    return _call_carrying(
        body, exchange, "sb_bwd", (n_pairs, nq), in_specs, out_specs, [jax.ShapeDtypeStruct((S, W), BF16)] * 3,
        [pltpu.VMEM(big, F32)] * 4 + [pltpu.VMEM(big, BF16)] * 4 + [pltpu.VMEM(stat, F32)] * 3
        + [pltpu.VMEM((tq, PAIR), F32), pltpu.VMEM((S, PAIR), F32), pltpu.VMEM((S, PAIR), F32),
           pltpu.VMEM((2, S, PAIR), BF16)],
        (qkv, qkv, qkv, o32, do))


def _fox_fwd(qkv, fcx, fcr, n_pairs, base, tq, exchange=None):
    S = qkv.shape[0]
    tk = tq
    scale = HEAD_DIM ** -0.5
    spare = (HEAD_DIM, 0)

    def body(q_ref, k_ref, v_ref, fq_ref, fk_ref, o_ref, lse_ref, s_ref, p_ref, m_ref, al_ref, fqr_ref, acc_ref, vm_ref):
        i = pl.program_id(1)
        masks = _head_masks()

        @pl.when(i == 0)
        def _():
            _fill_masked(vm_ref, v_ref, masks, ones_lane=spare)

        q2 = q_ref[...] * scale
        qm = [_sel(m, q2) for m in masks]
        f0, f1 = _per_head(fq_ref[...], masks)
        fqr_ref[0] = f0
        fqr_ref[1] = f1
        m_ref[...] = jnp.full(m_ref.shape, NEG, F32)
        acc_ref[...] = jnp.zeros_like(acc_ref)

        def tile(j, diag, slot, heads):
            off = pl.multiple_of(j * tk, tk)
            k2 = k_ref[pl.ds(off, tk), :]
            v2 = v_ref[pl.ds(off, tk), :]
            fk2 = fk_ref[0, j]
            for h in heads:
                s_ref[2 * slot + h] = _dot(qm[h], k2, 1, 1)
            yield
            for h in heads:
                fk_row = fk2[h:h + 1, :]

                def probs(r0, h=h, fk_row=fk_row):
                    rows = pl.ds(r0, RC)
                    sv = s_ref[2 * slot + h, rows, :] - fk_row
                    if diag:
                        sv = jnp.where(_diag_mask(r0, tk, False), sv, NEG)
                    fq = fqr_ref[h, rows, :]
                    m_prev = m_ref[h, rows, :]
                    m_new = jnp.maximum(m_prev, jnp.max(sv, axis=-1, keepdims=True) + fq)
                    p_ref[2 * slot + h, rows, :] = jnp.exp(sv + _wide(fq - m_new, tk)).astype(BF16)
                    al_ref[2 * slot + h, rows, :] = jnp.exp(m_prev - m_new)
                    m_ref[h, rows, :] = m_new
                _chunks(tq, probs)
            yield
            for h in heads:
                acc_ref[h] = acc_ref[h] * al_ref[2 * slot + h] + _dot(p_ref[2 * slot + h], vm_ref[h, pl.ds(off, tk), :])

        _tiles(i, tile)
        a0, a1 = acc_ref[0], acc_ref[1]
        l0 = _rep(a0[:, spare[0]:spare[0] + 1])
        l1 = _rep(a1[:, spare[1]:spare[1] + 1])
        o_ref[...] = jnp.where(masks[0], a0 / l0, a1 / l1).astype(BF16)
        lse_ref[...] = jnp.where(masks[0], m_ref[0] + jnp.log(l0), m_ref[1] + jnp.log(l1))

    W = n_pairs * PAIR
    return _call_carrying(
        body, exchange, "fox_fwd", (n_pairs, S // tq), _fox_specs(S, tq, n_pairs, base),
        [pl.BlockSpec((tq, PAIR), lambda p, i: (i, p)), pl.BlockSpec((tq, PAIR), lambda p, i: (i, p))],
        [jax.ShapeDtypeStruct((S, W), BF16), jax.ShapeDtypeStruct((S, W), F32)],
        [pltpu.VMEM((4, tq, tk), F32), pltpu.VMEM((4, tq, tk), BF16), pltpu.VMEM((2, tq, 128), F32),
         pltpu.VMEM((4, tq, 128), F32), pltpu.VMEM((2, tq, 128), F32), pltpu.VMEM((2, tq, 128), F32),
         pltpu.VMEM((2, S, PAIR), BF16)],
        (qkv, qkv, qkv, fcx, fcr))


def _fox_bwd(qkv, fcx, fcr, o, lse, do, n_pairs, base, tq, exchange=None):
    S = qkv.shape[0]
    tk = tq
    nq = S // tq
    scale = HEAD_DIM ** -0.5

    def body(q_ref, k_ref, v_ref, fq_ref, fk_ref, o_ref, lse_ref, do_ref,
             dq_ref, dk_ref, dv_ref, dfq_ref, dfk_ref,
             s_ref, dp_ref, p_ref, ds_ref, row_ref, dl_ref, dfq_acc, col_ref, dq_acc, dk_acc, dv_acc, ks_ref):
        i = pl.program_id(1)
        masks = _head_masks()

        @pl.when(i == 0)
        def _():
            dk_acc[...] = jnp.zeros_like(dk_acc)
            dv_acc[...] = jnp.zeros_like(dv_acc)
            dfk_ref[...] = jnp.zeros_like(dfk_ref)
            _fill_masked(ks_ref, k_ref, masks, mul=scale)

        q2 = q_ref[...] * scale
        do2 = do_ref[...]
        qm = [_sel(m, q2) for m in masks]
        dom = [_sel(m, do2) for m in masks]
        f0, f1 = _per_head(fq_ref[...], masks)
        l0, l1 = _per_head(lse_ref[...], masks)
        row_ref[0] = f0 - l0
        row_ref[1] = f1 - l1
        prod = do2.astype(F32) * o_ref[...].astype(F32)
        for h in range(2):
            dl_ref[h] = _rep(jnp.sum(jnp.where(masks[h], prod, 0.0), axis=-1, keepdims=True))
        dfq_acc[...] = jnp.zeros_like(dfq_acc)
        dq_acc[...] = jnp.zeros_like(dq_acc)

        def tile(j, diag, slot, heads):
            off = pl.multiple_of(j * tk, tk)
            k2 = k_ref[pl.ds(off, tk), :]
            v2 = v_ref[pl.ds(off, tk), :]
            fk2 = fk_ref[0, j]
            for h in heads:
                s_ref[2 * slot + h] = _dot(qm[h], k2, 1, 1)
                dp_ref[2 * slot + h] = _dot(dom[h], v2, 1, 1)
            yield
            for h in heads:
                col_ref[2 * slot + h] = jnp.zeros((8, tk), F32)
                fk_row = fk2[h:h + 1, :]

                def dscore(r0, h=h, fk_row=fk_row):
                    rows = pl.ds(r0, RC)
                    p = jnp.exp(s_ref[2 * slot + h, rows, :] - fk_row + _wide(row_ref[h, rows, :], tk))
                    if diag:
                        p = jnp.where(_diag_mask(r0, tk, False), p, 0.0)
                    ds = p * (dp_ref[2 * slot + h, rows, :] - _wide(dl_ref[h, rows, :], tk))
                    p_ref[2 * slot + h, rows, :] = p.astype(BF16)
                    ds_ref[2 * slot + h, rows, :] = ds.astype(BF16)
                    dfq_acc[h, rows, :] += _rep(jnp.sum(ds, axis=-1, keepdims=True))
                    col_ref[2 * slot + h] += jnp.sum(ds.reshape(RC // 8, 8, tk), axis=0)
                _chunks(tq, dscore)
            yield
            keys = pl.ds(off, tk)
            for h in heads:
                dq_acc[...] += _dot(ds_ref[2 * slot + h], ks_ref[h, keys, :])
                dk_acc[keys, :] += _dot(ds_ref[2 * slot + h], qm[h], 0, 0)
                dv_acc[keys, :] += _dot(p_ref[2 * slot + h], dom[h], 0, 0)
            for h in heads:
                dfk_ref[0, j, h:h + 1, :] += jnp.sum(col_ref[2 * slot + h], axis=0, keepdims=True)

        _tiles(i, tile)
        dq_ref[...] = dq_acc[...].astype(BF16)
        dfq_ref[...] = jnp.where(masks[0], dfq_acc[0], dfq_acc[1])

        @pl.when(i == nq - 1)
        def _():
            dk_ref[...] = dk_acc[...].astype(BF16)
            dv_ref[...] = dv_acc[...].astype(BF16)

    W = n_pairs * PAIR
    in_specs = _fox_specs(S, tq, n_pairs, base) + [
        pl.BlockSpec((tq, PAIR), lambda p, i: (i, p)),
        pl.BlockSpec((tq, PAIR), lambda p, i: (i, p)),
        pl.BlockSpec((tq, PAIR), lambda p, i: (i, p))]
    out_specs = [pl.BlockSpec((tq, PAIR), lambda p, i: (i, p)),
                 pl.BlockSpec((S, PAIR), lambda p, i: (0, p)),
                 pl.BlockSpec((S, PAIR), lambda p, i: (0, p)),
                 pl.BlockSpec((tq, PAIR), lambda p, i: (i, p)),
                 pl.BlockSpec((1, nq, 8, tk), lambda p, i: (p, 0, 0, 0))]
    return _call_carrying(
        body, exchange, "fox_bwd", (n_pairs, nq), in_specs, out_specs,
        [jax.ShapeDtypeStruct((S, W), BF16)] * 3
        + [jax.ShapeDtypeStruct((S, W), F32), jax.ShapeDtypeStruct((n_pairs, nq, 8, tk), F32)],
        [pltpu.VMEM((4, tq, tk), F32)] * 2 + [pltpu.VMEM((4, tq, tk), BF16)] * 2
        + [pltpu.VMEM((2, tq, 128), F32)] * 3 + [pltpu.VMEM((4, 8, tk), F32)]
        + [pltpu.VMEM((tq, PAIR), F32), pltpu.VMEM((S, PAIR), F32), pltpu.VMEM((S, PAIR), F32),
           pltpu.VMEM((2, S, PAIR), BF16)],
        (qkv, qkv, qkv, fcx, fcr, o, lse, do))


def _col_chunks(n, width=256):
    return [(c, min(width, n - c)) for c in range(0, n, width)]


def _swiglu_fwd(u2, wg, wu):
    S, D = u2.shape
    FF = wg.shape[1]
    tm, tn = _pick(S, (512, 256, 128)), _divisors(FF, 1536)[0]

    def body(u_ref, g_ref, w_ref, a_ref, b_ref, h_ref):
        u = u_ref[...]
        for c, w in _col_chunks(tn):
            cols = slice(c, c + w)
            a = _dot(u, g_ref[:, cols])
            b = _dot(u, w_ref[:, cols])
            a_ref[:, cols] = a.astype(BF16)
            b_ref[:, cols] = b.astype(BF16)
            h_ref[:, cols] = (a / (1.0 + jnp.exp(-a)) * b).astype(BF16)

    spec_o = pl.BlockSpec((tm, tn), lambda i, j: (i, j))
    return pl.pallas_call(
        body, name="swiglu_fwd", grid=(S // tm, FF // tn),
        in_specs=[pl.BlockSpec((tm, D), lambda i, j: (i, 0)),
                  pl.BlockSpec((D, tn), lambda i, j: (0, j)),
                  pl.BlockSpec((D, tn), lambda i, j: (0, j))],
        out_specs=[spec_o] * 3, out_shape=[jax.ShapeDtypeStruct((S, FF), BF16)] * 3,
        compiler_params=_params(("parallel", "parallel")),
    )(u2, wg, wu)


def _swiglu_bwd(dh, wd, a, b):
    S, D = dh.shape
    FF = wd.shape[0]
    tm, tn = _pick(S, (512, 256, 128)), _divisors(FF, 1536)[0]

    def body(dh_ref, w_ref, a_ref, b_ref, da_ref, db_ref):
        dh_blk = dh_ref[...]
        for c, w in _col_chunks(tn):
            cols = slice(c, c + w)
            dhin = _dot(dh_blk, w_ref[cols, :], 1, 1)
            av = a_ref[:, cols].astype(F32)
            bv = b_ref[:, cols].astype(F32)
            sig = 1.0 / (1.0 + jnp.exp(-av))
            da_ref[:, cols] = (dhin * bv * (sig * (1.0 + av * (1.0 - sig)))).astype(BF16)
            db_ref[:, cols] = (dhin * (av * sig)).astype(BF16)

    spec_o = pl.BlockSpec((tm, tn), lambda i, j: (i, j))
    return pl.pallas_call(
        body, name="swiglu_bwd", grid=(S // tm, FF // tn),
        in_specs=[pl.BlockSpec((tm, D), lambda i, j: (i, 0)),
                  pl.BlockSpec((tn, D), lambda i, j: (j, 0)), spec_o, spec_o],
        out_specs=[spec_o] * 2, out_shape=[jax.ShapeDtypeStruct((S, FF), BF16)] * 2,
        compiler_params=_params(("parallel", "parallel")),
    )(dh, wd, a, b)


def _gate_bwd(dmix, wo, g_sb, g_fx, y_sb, y_fx):
    S, D = dmix.shape
    tm, tn = _pick(S, (512, 256, 128)), _divisors(D, 1024)[0]

    def body(dm_ref, w_ref, gs_ref, gf_ref, ys_ref, yf_ref, dys_ref, dyf_ref, dls_ref, dlf_ref, bs_ref, bf_ref):
        @pl.when(pl.program_id(1) == 0)
        def _():
            bs_ref[...] = jnp.zeros_like(bs_ref)
            bf_ref[...] = jnp.zeros_like(bf_ref)

        dm_blk = dm_ref[...]
        for c, w in _col_chunks(tn):
            cols = slice(c, c + w)
            dmi = _dot(dm_blk, w_ref[cols, :], 1, 1)
            gs, gf = gs_ref[:, cols], gf_ref[:, cols]
            dys_ref[:, cols] = (dmi * gs).astype(BF16)
            dyf_ref[:, cols] = (dmi * gf).astype(BF16)
            dls = dmi * ys_ref[:, cols] * gs * (1.0 - gs)
            dlf = dmi * yf_ref[:, cols] * gf * (1.0 - gf)
            dls_ref[:, cols] = dls.astype(BF16)
            dlf_ref[:, cols] = dlf.astype(BF16)
            bs_ref[0:1, cols] += _colsum(dls)
            bf_ref[0:1, cols] += _colsum(dlf)

    t = pl.BlockSpec((tm, tn), lambda j, i: (i, j))
    accs = pl.BlockSpec((8, tn), lambda j, i: (0, j))
    return pl.pallas_call(
        body, name="gate_bwd", grid=(D // tn, S // tm),
        in_specs=[pl.BlockSpec((tm, D), lambda j, i: (i, 0)),
                  pl.BlockSpec((tn, D), lambda j, i: (j, 0)), t, t, t, t],
        out_specs=[t, t, t, t, accs, accs],
        out_shape=[jax.ShapeDtypeStruct((S, D), BF16)] * 4 + [jax.ShapeDtypeStruct((8, D), F32)] * 2,
        compiler_params=_params(("parallel", "arbitrary")),
    )(dmix, wo, g_sb, g_fx, y_sb, y_fx)


def _local_step(x, target, ada8, lnp8, bg_sb, bg_fx, bf_pad, wqkv, wf, wgs, wgf, gather, later_weights, pack_early,
                pack_last):
    S, D = x.shape
    W = wqkv.shape[1] // 6
    n_pairs = W // PAIR
    n_heads = W // HEAD_DIM
    ts = _pick(S, (512, 256, 128))
    tq = _pick(S, (256, 128))

    u1 = _ln_mod(x, ada8, ts)
    qkv = _mm([(u1, wqkv)], 'nn', BF16, "in_qkv")
    f = _mm([(u1, wf)], 'nn', F32, "in_f")
    g_sb = _mm([(u1, wgs)], 'nn', F32, "in_gsb", bias=bg_sb, act='sigmoid')
    g_fx = _mm([(u1, wgf)], 'nn', F32, "in_gfx", bias=bg_fx, act='sigmoid')
    fc = _fgate_fwd(f, bf_pad, _pick(S, (512, 256, 128)))
    fch = fc[:, :n_heads]
    fcx = jnp.repeat(fch, HEAD_DIM, axis=1)
    nq = S // tq
    fcr = jnp.pad(fch.T.reshape(n_pairs, 2, nq, tq).transpose(0, 2, 1, 3),
                  ((0, 0), (0, 0), (0, 6), (0, 0)))
    o_sb, o_sb32, *zone_a = _sb_fwd(qkv, n_pairs, 0, tq, gather[0])
    o_fx, lse, *zone_b = _fox_fwd(qkv, fcx, fcr, n_pairs, 3 * n_pairs, tq, gather[1])
    wsb, wfx, wo, wfg, wfu, wfd = later_weights(*zone_a, *zone_b)
    y_sb =_mm([(o_sb, wsb)], 'nn', F32, "out_sb")
    y_fx = _mm([(o_fx, wfx)], 'nn', F32, "out_fx")
    mix_in, mix = _gate_mix_out(g_sb, g_fx, y_sb, y_fx, wo, ts)
    x1, u2 = _post_attn(x, mix, ada8, lnp8, ts)
    a, b, hin = _swiglu_fwd(u2, wfg, wfu)
    dr2, dh, st_loss = _loss_head(x1, hin, wfd, target, ada8, lnp8, ts)

    da, db = _swiglu_bwd(dh, wfd, a, b)
    g_wfd = _mm([(hin, dh)], 'tn', F32, "g_ffn_down")
    du2 = _mm([(da, wfg), (db, wfu)], 'nt', F32, "d_u2")
    g_wfg = _mm([(u2, da)], 'tn', F32, "g_ffn_gate")
    g_wfu = _mm([(u2, db)], 'tn', F32, "g_ffn_up")
    dr1, dmix, st_mid = _mid_bwd(du2, x1, dr2, mix, x, ada8, lnp8, ts)
    dys, dyf, dls, dlf, gb_sb, gb_fx = _gate_bwd(dmix, wo, g_sb, g_fx, y_sb, y_fx)
    g_wo = _mm([(mix_in, dmix)], 'tn', F32, "g_w_o")
    do_sb = _mm([(dys, wsb)], 'nt', BF16, "d_o_sb")
    do_fx = _mm([(dyf, wfx)], 'nt', BF16, "d_o_fx")
    g_wsb = _mm([(o_sb, dys)], 'tn', F32, "g_sb_out")
    g_wfx = _mm([(o_fx, dyf)], 'tn', F32, "g_fox_out")
    g_wgs = _mm([(u1, dls)], 'tn', F32, "g_in_gsb")
    g_wgf = _mm([(u1, dlf)], 'tn', F32, "g_in_gfx")
    scatter = pack_early(dict(sb=g_wsb, fx=g_wfx, o=g_wo, fg=g_wfg, fu=g_wfu, fd=g_wfd, gs=g_wgs, gf=g_wgf))
    dq_s, dk_s, dv_s, *zone_a = _sb_bwd(qkv, o_sb32, do_sb, n_pairs, 0, tq, scatter[0])
    dq_f, dk_f, dv_f, dfq, dfk, *zone_b = _fox_bwd(qkv, fcx, fcr, o_fx, lse, do_fx, n_pairs, 3 * n_pairs, tq,
                                                    scatter[1])
    early = ([(scatter[0], zone_a[0])] + list(zip(scatter[1], zone_b))) if zone_a else []
    dfc = dfq[:, ::HEAD_DIM] - dfk[:, :, :2, :].transpose(0, 2, 1, 3).reshape(n_heads, S).T
    dfc = jnp.pad(dfc, ((0, 0), (0, 128 - n_heads)))
    df, gb_f = _fgate_bwd(dfc, f, bf_pad, _pick(S, (512, 256, 128)))
    grads = [dq_s, dk_s, dv_s, dq_f, dk_f, dv_f]
    g_wqkv = [_mm([(u1, g)], 'tn', F32, "g_in_%d" % n) for n, g in enumerate(grads)]
    g_wf = _mm([(u1, df)], 'tn', F32, "g_in_f")
    wgrads = dict(qkv=g_wqkv, f=g_wf, gs=g_wgs, gf=g_wgf)
    last = pack_last(wgrads)
    gx, st_first, *last_zone = _first_bwd(
        [(g, wqkv, W, 0, n) for n, g in enumerate(grads)] + [(df, wf), (dls, wgs), (dlf, wgf)], x, dr1, ada8, last)
    last_zone = last_zone[0] if last_zone else None

    stats = dict(loss=st_loss, mid=st_mid, first=st_first, gb_sb=gb_sb, gb_fx=gb_fx, gb_f=gb_f)
    return gx, wgrads, stats, early, (last, last_zone)


def _position():
    x, y, c = lax.axis_index("x"), lax.axis_index("y"), lax.axis_index("c")
    return x, y, c, 4 * x + 2 * y + c


def _flip(x, y, c, k):
    px = 1 - x if k & 4 else x
    py = 1 - y if k & 2 else y
    pc = 1 - c if k & 1 else c
    return (px, py, pc), 4 * px + 2 * py + pc


def _all_gather_small(v, name):
    r, n = v.shape

    def body(x_ref, out_ref, send_sems, recv_sems, local_sem):
        x, y, c, me = _position()
        mine = pltpu.make_async_copy(x_ref, out_ref.at[me], local_sem)
        mine.start()
        sends = []
        for k in range(1, N_DEV):
            peer, _ = _flip(x, y, c, k)
            cp = pltpu.make_async_remote_copy(
                src_ref=x_ref, dst_ref=out_ref.at[me], send_sem=send_sems.at[k - 1], recv_sem=recv_sems.at[k - 1],
                device_id=peer, device_id_type=MESH)
            cp.start()
            sends.append(cp)
        for k in range(1, N_DEV):
            peer, slot = _flip(x, y, c, k)
            pltpu.make_async_remote_copy(
                src_ref=x_ref, dst_ref=out_ref.at[slot], send_sem=send_sems.at[k - 1], recv_sem=recv_sems.at[k - 1],
                device_id=peer, device_id_type=MESH).wait_recv()
        for cp in sends:
            cp.wait_send()
        mine.wait()

    return pl.pallas_call(
        body, name=name, out_shape=jax.ShapeDtypeStruct((N_DEV, r, n), v.dtype),
        in_specs=[pl.BlockSpec(memory_space=pltpu.VMEM)], out_specs=pl.BlockSpec(memory_space=pltpu.VMEM),
        scratch_shapes=[pltpu.SemaphoreType.DMA((N_DEV - 1,)), pltpu.SemaphoreType.DMA((N_DEV - 1,)),
                        pltpu.SemaphoreType.DMA],
    )(v)


def _all_gather_weights(packed):
    R, C = packed.shape

    def body(x_ref, out_ref, send_sems, recv_sems, local_sem):
        x, y, c, me = _position()
        sibling, sib_slot = _flip(x, y, c, 1)
        mine = pltpu.make_async_copy(x_ref, out_ref.at[me], local_sem)
        mine.start()

        def copy(k, slot, to, src=None):
            return pltpu.make_async_remote_copy(
                src_ref=out_ref.at[slot] if src is None else src, dst_ref=out_ref.at[slot],
                send_sem=send_sems.at[k], recv_sem=recv_sems.at[k], device_id=to, device_id_type=MESH)

        first = [copy(0, me, sibling, src=x_ref)]
        chips = (4, 2, 6)
        for n, k in enumerate(chips):
            peer, _ = _flip(x, y, c, k)
            first.append(copy(1 + n, me, peer, src=x_ref))
        for cp in first:
            cp.start()
        passed = []
        for n, k in enumerate(chips):
            peer, slot = _flip(x, y, c, k)
            copy(1 + n, slot, peer).wait_recv()
            cp = copy(4 + n, slot, sibling)
            cp.start()
            passed.append(cp)
        copy(0, sib_slot, sibling).wait_recv()
        for n, k in enumerate(chips):
            _, slot = _flip(x, y, c, k | 1)
            copy(4 + n, slot, sibling).wait_recv()
        for cp in first + passed:
            cp.wait_send()
        mine.wait()

    return pl.pallas_call(
        body, name="all_gather_weights", out_shape=jax.ShapeDtypeStruct((N_DEV, R, C), packed.dtype),
        in_specs=[pl.BlockSpec(memory_space=pl.ANY)], out_specs=pl.BlockSpec(memory_space=pl.ANY),
        scratch_shapes=[pltpu.SemaphoreType.DMA((7,)), pltpu.SemaphoreType.DMA((7,)), pltpu.SemaphoreType.DMA],
    )(packed)


def _own_slot(land, own):
    me = 4 * lax.axis_index("x") + 2 * lax.axis_index("y") + lax.axis_index("c")
    return lax.dynamic_update_slice(land, own[None], (me, 0, 0))


def _sum_slots(recv, name, tr):
    n, R, C = recv.shape

    def body(r_ref, o_ref):
        acc = r_ref[0].astype(F32)
        for s in range(1, n):
            acc = acc + r_ref[s].astype(F32)
        o_ref[...] = acc

    return pl.pallas_call(
        body, name=name, grid=(R // tr,), in_specs=[pl.BlockSpec((n, tr, C), lambda i: (0, i, 0))],
        out_specs=pl.BlockSpec((tr, C), lambda i: (i, 0)), out_shape=jax.ShapeDtypeStruct((R, C), F32),
        compiler_params=_params(("parallel",)),
    )(recv)


def _sum_stats(st_all, loss_row):
    n, r, D = st_all.shape

    def body(s_ref, o_ref, l_ref):
        acc = s_ref[0]
        for d in range(1, n):
            acc = acc + s_ref[d]
        o_ref[...] = acc
        l_ref[...] = jnp.zeros((8, 128), F32) + jnp.sum(acc[loss_row:loss_row + 1, :], axis=-1, keepdims=True)

    return pl.pallas_call(
        body, name="sum_stats", out_shape=[jax.ShapeDtypeStruct((r, D), F32), jax.ShapeDtypeStruct((8, 128), F32)],
    )(st_all)


def _adamw(w, g, m, v, name):
    R, C = w.shape
    tr = _pick(R, (256, 176, 128, 64, 32, 16, 8))
    c1 = 1.0 / (1.0 - ADAM_B1 ** ADAM_STEP)
    c2 = 1.0 / (1.0 - ADAM_B2 ** ADAM_STEP)

    def body(w_ref, g_ref, m_ref, v_ref, d_ref, nm_ref, nv_ref):
        gv = g_ref[...]
        nm = ADAM_B1 * m_ref[...] + (1.0 - ADAM_B1) * gv
        nv = ADAM_B2 * v_ref[...] + (1.0 - ADAM_B2) * (gv * gv)
        nm_ref[...] = nm
        nv_ref[...] = nv
        d_ref[...] = -ADAM_LR * ((nm * c1) / (jnp.sqrt(nv * c2) + ADAM_EPS) + ADAM_WD * w_ref[...])

    spec = pl.BlockSpec((tr, C), lambda i: (i, 0))
    return pl.pallas_call(
        body, name=name, grid=(R // tr,), in_specs=[spec] * 4, out_specs=[spec] * 3,
        out_shape=[jax.ShapeDtypeStruct((R, C), F32)] * 3, compiler_params=_params(("parallel",)),
    )(w, g, m, v)


def _round16(n):
    return -(-n // 16) * 16


def _pack_layout(D, in_cols, ff, W):
    parts = [("in", D * (in_cols // N_DEV) // D), ("fg", ff // N_DEV), ("fu", ff // N_DEV),
             ("sb", W * (D // N_DEV) // D), ("fx", W * (D // N_DEV) // D), ("o", D // N_DEV), ("fd", ff // N_DEV)]
    layout, off = {}, 0
    for nm, rows in parts:
        layout[nm] = (off, rows)
        off += _round16(rows)
    return layout, off


def _rows_of(a, D, rows):
    a = a.reshape(rows, D)
    return jnp.pad(a, ((0, _round16(rows) - rows), (0, 0)))


def _cols_to_dest(g, D):
    K, N = g.shape
    n = N // N_DEV
    return g.reshape(K, N_DEV, n).transpose(1, 0, 2).reshape(N_DEV, K * n // D, D)


def _cols_from_src(blocks, K, n):
    return blocks.reshape(N_DEV, K, n).transpose(1, 0, 2).reshape(K, N_DEV * n)


def _pad_rows16(a):
    rows = a.shape[1]
    return jnp.pad(a, ((0, 0), (0, _round16(rows) - rows), (0, 0)))


def kernel(x, c, w_ada, b_ada, w_in, b_gate, b_forget, w_sb_out, w_fox_out, w_o, ln1_g, ln1_b, w_ffn_gate, w_ffn_up, w_ffn_down, ln2_g, ln2_b, loss_target, m_w_ada, m_b_ada, m_w_in, m_b_gate, m_b_forget, m_w_sb_out, m_w_fox_out, m_w_o, m_ln1_g, m_ln1_b, m_w_ffn_gate, m_w_ffn_up, m_w_ffn_down, m_ln2_g, m_ln2_b, v_w_ada, v_b_ada, v_w_in, v_b_gate, v_b_forget, v_w_sb_out, v_w_fox_out, v_w_o, v_ln1_g, v_ln1_b, v_w_ffn_gate, v_w_ffn_up, v_w_ffn_down, v_ln2_g, v_ln2_b):
    S, D = x.shape[1], x.shape[2]
    W = w_sb_out.shape[1]
    n_heads = b_forget.shape[1]
    ff = w_ffn_down.shape[1] * N_DEV
    in_loc = w_in.shape[2]
    in_cols = in_loc * N_DEV
    ada_loc = w_ada.shape[2]
    n_cond = ada_loc * N_DEV // D
    assert w_ada.shape[0] == 1 and n_cond == 6 and in_cols == 6 * W + n_heads + 2 * D and n_heads <= 128
    me = 4 * lax.axis_index("x") + 2 * lax.axis_index("y") + lax.axis_index("c")

    c_all = _all_gather_small(c, "gather_c").reshape(N_DEV, D)
    c16 = jnp.pad(c_all, ((0, 16 - N_DEV), (0, 0)))
    b_cols = lax.dynamic_slice(b_ada, (0, me * ada_loc), (1, ada_loc))
    ada_cols = _mm([(c16, w_ada[0])], 'nn', F32, "ada_fwd", bias=b_cols, silu_a=True)[:N_DEV]
    ada_all = _all_gather_small(ada_cols, "gather_ada")
    ada_me = lax.dynamic_index_in_dim(ada_all, me, axis=1, keepdims=False)
    ada8 = jnp.pad(ada_me.reshape(n_cond, D), ((0, 8 - n_cond), (0, 0)))
    lnp8 = jnp.concatenate([ln1_g, ln1_b, ln2_g, ln2_b, jnp.zeros((4, D), F32)], axis=0)

    layout, R = _pack_layout(D, in_cols, ff, W)
    shards = dict(**{"in": w_in[0]}, fg=w_ffn_gate[0], fu=w_ffn_up[0], sb=w_sb_out[0], fx=w_fox_out[0], o=w_o[0],
                  fd=w_ffn_down[0])
    rows_fwd, rows_bwd = ("sb", "fx", "o"), ("fd", "o", "sb", "fx")
    rows_of = {nm: layout[nm][1] for nm in layout}

    def offsets(names):
        offs, off = {}, 0
        for nm in names:
            offs[nm] = off
            off += _round16(rows_of[nm])
        return offs

    def as_rows(names):
        return jnp.concatenate([_rows_of(shards[nm].astype(BF16), D, rows_of[nm]) for nm in names], axis=0)

    def whole_from(blocks):
        return blocks.transpose(1, 0, 2).reshape(blocks.shape[1], N_DEV * blocks.shape[2])

    def blocks_of(g):
        return g.reshape(g.shape[0], N_DEV, g.shape[1] // N_DEV).transpose(1, 0, 2)

    gather_src = [as_rows(rows_fwd),
                  jnp.concatenate([shards["fg"].astype(BF16), shards["fu"].astype(BF16)], axis=0), as_rows(("fd",))]
    gathered_in = _all_gather_weights(shards["in"].astype(BF16))

    w_in_full = whole_from(gathered_in)
    wqkv = w_in_full[:, :6 * W]
    wf = jnp.pad(w_in_full[:, 6 * W:6 * W + n_heads], ((0, 0), (0, 128 - n_heads)))
    wgs = w_in_full[:, 6 * W + n_heads:6 * W + n_heads + D]
    wgf = w_in_full[:, 6 * W + n_heads + D:]
    bf_pad = jnp.pad(b_forget, ((0, 0), (0, 128 - n_heads)))

    def later_weights(zone_rows, zone_gate_up, zone_down):
        rows, offs = _own_slot(zone_rows, gather_src[0]), offsets(rows_fwd)
        part = {nm: rows[:, offs[nm]:offs[nm] + rows_of[nm], :] for nm in rows_fwd}
        gate_up = _own_slot(zone_gate_up, gather_src[1])
        down = _own_slot(zone_down, gather_src[2])[:, :rows_of["fd"], :]
        return (_cols_from_src(part["sb"], W, D // N_DEV), _cols_from_src(part["fx"], W, D // N_DEV),
                part["o"].reshape(D, D), whole_from(gate_up[:, :D, :]), whole_from(gate_up[:, D:, :]),
                down.reshape(ff, D))

    def pack_early(g):
        dest = {"sb": _cols_to_dest(g["sb"], D), "fx": _cols_to_dest(g["fx"], D),
                "o": g["o"].reshape(N_DEV, D // N_DEV, D), "fd": g["fd"].reshape(N_DEV, ff // N_DEV, D)}
        rows = jnp.concatenate([_pad_rows16(dest[nm].astype(BF16)) for nm in rows_bwd], axis=1)
        gate_up = jnp.concatenate([blocks_of(g["fg"].astype(BF16)), blocks_of(g["fu"].astype(BF16))], axis=1)
        gates = jnp.concatenate([g["gs"], g["gf"]], axis=1)[:, gate_lo * in_loc - (6 * W + n_heads):]
        gates = gates.astype(BF16).reshape(D, N_DEV - gate_lo, in_loc).transpose(1, 0, 2)
        return [(rows, True), [(gate_up, True), (gates, True, (gate_lo, N_DEV))]]

    def pack_last(g):
        g_in = jnp.concatenate(g["qkv"] + [g["f"][:, :n_heads], g["gs"], g["gf"]], axis=1)[:, :gate_lo * in_loc]
        return g_in.astype(BF16).reshape(D, gate_lo, in_loc).transpose(1, 0, 2), True, (0, gate_lo)

    gate_lo = -(-(6 * W + n_heads) // in_loc)
    gx, wg, st, early, (last_exchange, land_in) = _local_step(
        x[0], loss_target[0], ada8, lnp8, b_gate[:, :D], b_gate[:, D:], bf_pad, wqkv, wf, wgs, wgf,
        [(gather_src[0], False), [(gather_src[1], False), (gather_src[2], False)]],
        later_weights, pack_early, pack_last)

    def summed(zone, sent, name, lo=0):
        own = lax.dynamic_index_in_dim(sent, jnp.clip(me - lo, 0, sent.shape[0] - 1), axis=0, keepdims=False)
        rows = zone.shape[1]
        block = max(t for t in range(16, 705, 16) if rows % t == 0)
        return _sum_slots(_own_slot(zone, own), name, block)

    (ex_rows, zone_rows), (ex_gate_up, zone_gate_up), (ex_gates, zone_gates) = early
    gate_up = summed(zone_gate_up, ex_gate_up[0], "sum_grads_gate_up")
    in_early = summed(zone_gates, ex_gates[0], "sum_grads_in_gates", gate_lo)
    in_late = summed(land_in, last_exchange[0], "sum_grads_in")
    gsum = {"in": jnp.where(me >= gate_lo, in_early, in_late), "fg": gate_up[:D], "fu": gate_up[D:]}
    total, offs = summed(zone_rows, ex_rows[0], "sum_grads_rows"), offsets(rows_bwd)
    for nm in rows_bwd:
        gsum[nm] = total[offs[nm]:offs[nm] + rows_of[nm]]

    def gshard(nm, shape):
        return gsum[nm].reshape(shape)

    zrow = jnp.zeros((1, D), F32)
    gb_f_row = jnp.pad(st["gb_f"][0:1], ((0, 0), (0, D - 128)))
    stats16 = jnp.concatenate([
        st["first"][1:2], st["first"][0:1], st["mid"][4:5], st["mid"][1:2], st["mid"][0:1], st["loss"][3:4],
        st["mid"][2:3], st["mid"][3:4], st["loss"][1:2], st["loss"][2:3], st["gb_sb"][0:1], st["gb_fx"][0:1],
        st["loss"][0:1], gb_f_row, zrow, zrow], axis=0)
    st_all = _all_gather_small(stats16, "gather_stats")
    st_sum, loss_blk = _sum_stats(st_all, 12)
    loss = loss_blk[0, 0]

    d_ada_all = st_all[:, :n_cond, :].reshape(N_DEV, n_cond * D)
    d_cols = lax.dynamic_slice(d_ada_all, (0, me * ada_loc), (N_DEV, ada_loc))
    d16 = jnp.pad(d_cols, ((0, 16 - N_DEV), (0, 0)))
    g_w_ada = _mm([(c16, d16)], 'tn', F32, "ada_wgrad", silu_a=True)

    small_w = jnp.concatenate([b_ada.reshape(n_cond, D), ln1_g, ln1_b, ln2_g, ln2_b, b_gate.reshape(2, D), zrow,
                               jnp.pad(b_forget, ((0, 0), (0, D - n_heads))), zrow, zrow], axis=0)
    small_m = jnp.concatenate([m_b_ada.reshape(n_cond, D), m_ln1_g, m_ln1_b, m_ln2_g, m_ln2_b, m_b_gate.reshape(2, D),
                               zrow, jnp.pad(m_b_forget, ((0, 0), (0, D - n_heads))), zrow, zrow], axis=0)
    small_v = jnp.concatenate([v_b_ada.reshape(n_cond, D), v_ln1_g, v_ln1_b, v_ln2_g, v_ln2_b, v_b_gate.reshape(2, D),
                               zrow, jnp.pad(v_b_forget, ((0, 0), (0, D - n_heads))), zrow, zrow], axis=0)
    sm = _adamw(small_w, st_sum, small_m, small_v, "adamw_small")

    def small(a, nm):
        if nm == "b_ada":
            return a[0:n_cond].reshape(1, n_cond * D)
        if nm == "b_gate":
            return a[10:12].reshape(1, 2 * D)
        if nm == "b_forget":
            return a[13:14, :n_heads]
        row = {"ln1_g": 6, "ln1_b": 7, "ln2_g": 8, "ln2_b": 9}[nm]
        return a[row:row + 1]

    big = {
        "w_ada": (w_ada[0], g_w_ada, m_w_ada[0], v_w_ada[0]),
        "w_in": (w_in[0], gshard("in", w_in.shape[1:]), m_w_in[0], v_w_in[0]),
        "w_sb_out": (w_sb_out[0], gshard("sb", w_sb_out.shape[1:]), m_w_sb_out[0], v_w_sb_out[0]),
        "w_fox_out": (w_fox_out[0], gshard("fx", w_fox_out.shape[1:]), m_w_fox_out[0], v_w_fox_out[0]),
        "w_o": (w_o[0], gshard("o", w_o.shape[1:]), m_w_o[0], v_w_o[0]),
        "w_ffn_gate": (w_ffn_gate[0], gshard("fg", w_ffn_gate.shape[1:]), m_w_ffn_gate[0], v_w_ffn_gate[0]),
        "w_ffn_up": (w_ffn_up[0], gshard("fu", w_ffn_up.shape[1:]), m_w_ffn_up[0], v_w_ffn_up[0]),
        "w_ffn_down": (w_ffn_down[0], gshard("fd", w_ffn_down.shape[1:]), m_w_ffn_down[0], v_w_ffn_down[0]),
    }
    order = ["w_ada", "b_ada", "w_in", "b_gate", "b_forget", "w_sb_out", "w_fox_out", "w_o", "ln1_g", "ln1_b",
             "w_ffn_gate", "w_ffn_up", "w_ffn_down", "ln2_g", "ln2_b"]
    grads, deltas, new_ms, new_vs = [], [], [], []
    for nm in order:
        if nm in big:
            w, g, m, v = big[nm]
            d, nm_, nv_ = _adamw(w, g, m, v, "adamw_" + nm)
            grads.append(g[None])
            deltas.append(d[None])
            new_ms.append(nm_[None])
            new_vs.append(nv_[None])
        else:
            grads.append(small(st_sum, nm))
            deltas.append(small(sm[0], nm))
            new_ms.append(small(sm[1], nm))
            new_vs.append(small(sm[2], nm))
    return (loss, gx[None], *grads, *deltas, *new_ms, *new_vs)
```

```python
import jax
import jax.numpy as jnp
from jax import lax
from jax.experimental import pallas as pl
from jax.experimental.pallas import tpu as pltpu

F32 = jnp.float32
BF16 = jnp.bfloat16

HEAD_DIM = 64
PAIR = 2 * HEAD_DIM
LN_EPS = 1e-5
ALPHA = 2.0 ** 0.25
ADAM_LR, ADAM_B1, ADAM_B2, ADAM_EPS, ADAM_WD, ADAM_STEP = 0.001, 0.9, 0.999, 1e-08, 0.01, 10
N_DEV = 8
VMEM_LIMIT = 56 * 1024 * 1024
MESH = pl.DeviceIdType.MESH


def _dot(a, b, ca=1, cb=0):
    return lax.dot_general(a, b, (((ca,), (cb,)), ((), ())), preferred_element_type=F32)


def _pick(n, cands):
    for t in cands:
        if n % t == 0:
            return t
    return n


def _params(sem):
    return pltpu.CompilerParams(dimension_semantics=sem, vmem_limit_bytes=VMEM_LIMIT)


MM_BLOCK_BYTES = 40 * 1024 * 1024
LANES = 128


def _divisors(n, cap):
    ds = [d for d in range(LANES, min(n, cap) + 1, LANES) if n % d == 0]
    return sorted(ds, reverse=True) or [n]


def _mm_tiles(M, N, a_row_bytes, b_row_bytes, out_itemsize):
    best = None
    for tm in _divisors(M, 1024):
        for tn in _divisors(N, 2048):
            need = 2 * (tm * a_row_bytes + tn * b_row_bytes + tm * tn * out_itemsize) + tm * tn * 4
            if need <= MM_BLOCK_BYTES and (best is None or (tm * tn, tm) > (best[0] * best[1], best[0])):
                best = (tm, tn)
    assert best is not None, (M, N, a_row_bytes, b_row_bytes)
    return best


def _mm(pairs, mode, out_dtype, name, bias=None, act=None, silu_a=False, exchange=None):
    norm = []
    for p in pairs:
        a, b = p[0], p[1]
        kdim_a = a.shape[0] if mode == 'tn' else a.shape[1]
        K, ka, kb = (p[2], p[3], p[4]) if len(p) > 2 else (kdim_a, 0, 0)
        norm.append((a, b, K, ka, kb))
    a0, b0 = norm[0][0], norm[0][1]
    M = a0.shape[1] if mode == 'tn' else a0.shape[0]
    N = b0.shape[0] if mode == 'nt' else b0.shape[1]
    tm, tn = _mm_tiles(M, N, sum(K * a.dtype.itemsize for a, _, K, _, _ in norm),
                       sum(K * b.dtype.itemsize for _, b, K, _, _ in norm), jnp.dtype(out_dtype).itemsize)
    n_pairs = len(norm)

    in_specs, args = [], []
    for a, b, K, ka, kb in norm:
        if mode == 'tn':
            in_specs.append(pl.BlockSpec((K, tm), lambda i, j, ka=ka: (ka, i)))
        else:
            in_specs.append(pl.BlockSpec((tm, K), lambda i, j, ka=ka: (i, ka)))
        if mode == 'nt':
            in_specs.append(pl.BlockSpec((tn, K), lambda i, j, kb=kb: (j, kb)))
        else:
            in_specs.append(pl.BlockSpec((K, tn), lambda i, j, kb=kb: (kb, j)))
        args += [a, b]
    if bias is not None:
        in_specs.append(pl.BlockSpec((1, tn), lambda i, j: (0, j)))
        args.append(bias)

    ca = 0 if mode == 'tn' else 1
    cb = 1 if mode == 'nt' else 0

    def body(*refs):
        o_ref = refs[-1]
        acc = None
        for p in range(n_pairs):
            av = refs[2 * p][...]
            if silu_a:
                av = av / (1.0 + jnp.exp(-av))
            d = _dot(av.astype(BF16), refs[2 * p + 1][...].astype(BF16), ca, cb)
            acc = d if acc is None else acc + d
        if bias is not None:
            acc = acc + refs[2 * n_pairs][...]
        if act == 'sigmoid':
            acc = 1.0 / (1.0 + jnp.exp(-acc))
        o_ref[...] = acc.astype(out_dtype)

    out_spec = pl.BlockSpec((tm, tn), lambda i, j: (i, j))
    out_shape = jax.ShapeDtypeStruct((M, N), out_dtype)
    if exchange is not None:
        return _call_carrying(body, exchange, name, (M // tm, N // tn), in_specs, [out_spec], [out_shape], [], args)
    return pl.pallas_call(
        body, name=name, grid=(M // tm, N // tn), in_specs=in_specs, out_specs=out_spec, out_shape=out_shape,
        compiler_params=_params(("parallel", "parallel")),
    )(*args)


def _rows_call(body, name, row_ins, vec_ins, row_outs, acc_outs, ts):
    S = row_ins[0].shape[0]
    in_specs = [pl.BlockSpec((ts, a.shape[1]), lambda i: (i, 0)) for a in row_ins]
    in_specs += [pl.BlockSpec(a.shape, lambda i: (0, 0)) for a in vec_ins]
    out_specs = [pl.BlockSpec((ts, c), lambda i: (i, 0)) for c, _ in row_outs]
    out_specs += [pl.BlockSpec(s, lambda i: (0, 0)) for s in acc_outs]
    out_shape = [jax.ShapeDtypeStruct((S, c), dt) for c, dt in row_outs]
    out_shape += [jax.ShapeDtypeStruct(s, F32) for s in acc_outs]
    return pl.pallas_call(
        body, name=name, grid=(S // ts,), in_specs=in_specs, out_specs=out_specs, out_shape=out_shape,
        compiler_params=_params(("arbitrary",)),
    )(*row_ins, *vec_ins)


def _ln_stats(v):
    mu = jnp.mean(v, axis=-1, keepdims=True)
    d = v - mu
    var = jnp.mean(d * d, axis=-1, keepdims=True)
    rstd = lax.rsqrt(var + LN_EPS)
    return d * rstd, rstd


def _ln_bwd(dxhat, xhat, rstd):
    m1 = jnp.mean(dxhat, axis=-1, keepdims=True)
    m2 = jnp.mean(dxhat * xhat, axis=-1, keepdims=True)
    return rstd * (dxhat - m1 - xhat * m2)


def _colsum(v):
    return jnp.sum(v, axis=0, keepdims=True)


def _ln_mod(x, ada8, ts):
    D = x.shape[1]

    def body(x_ref, v_ref, u_ref):
        xhat, _ = _ln_stats(x_ref[...])
        u_ref[...] = (xhat * (1.0 + v_ref[1:2, :]) + v_ref[0:1, :]).astype(BF16)

    return _rows_call(body, "ln_mod", [x], [ada8], [(D, BF16)], [], ts)[0]


def _gate_mix_out(g_sb, g_fx, y_sb, y_fx, wo, ts):
    D = y_sb.shape[1]

    def body(gs, gf, ys, yf, w_ref, mi_ref, mix_ref):
        mi = (gs[...] * ys[...] + gf[...] * yf[...]).astype(BF16)
        mi_ref[...] = mi
        mix_ref[...] = _dot(mi, w_ref[...])

    return _rows_call(body, "gate_mix_out", [g_sb, g_fx, y_sb, y_fx], [wo], [(D, BF16), (D, F32)], [], ts)


def _post_attn(x, mix, ada8, lnp8, ts):
    D = x.shape[1]

    def body(x_ref, mix_ref, v_ref, p_ref, x1_ref, u2_ref):
        r1 = ALPHA * x_ref[...] + v_ref[2:3, :] * mix_ref[...]
        xhat, _ = _ln_stats(r1)
        x1 = xhat * p_ref[0:1, :] + p_ref[1:2, :]
        x1_ref[...] = x1
        xh1, _ = _ln_stats(x1)
        u2_ref[...] = (xh1 * (1.0 + v_ref[4:5, :]) + v_ref[3:4, :]).astype(BF16)

    return _rows_call(body, "post_attn", [x, mix], [ada8, lnp8], [(D, F32), (D, BF16)], [], ts)


def _loss_head(x1, hin, wfd, target, ada8, lnp8, ts):
    D = x1.shape[1]

    def body(x1_ref, hin_ref, t_ref, v_ref, p_ref, w_ref, dr2_ref, dh_ref, st_ref):
        @pl.when(pl.program_id(0) == 0)
        def _():
            st_ref[...] = jnp.zeros_like(st_ref)

        hv = _dot(hin_ref[...], w_ref[...])
        g2 = v_ref[5:6, :]
        r2 = ALPHA * x1_ref[...] + g2 * hv
        xhat, rstd = _ln_stats(r2)
        y = xhat * p_ref[2:3, :] + p_ref[3:4, :]
        err = y - t_ref[...]
        dy = err * (1.0 / D)
        dr2 = _ln_bwd(dy * p_ref[2:3, :], xhat, rstd)
        dr2_ref[...] = dr2
        dh_ref[...] = (dr2 * g2).astype(BF16)
        st_ref[0:1, :] += _colsum(err * err) * (0.5 / D)
        st_ref[1:2, :] += _colsum(dy * xhat)
        st_ref[2:3, :] += _colsum(dy)
        st_ref[3:4, :] += _colsum(dr2 * hv)

    return _rows_call(body, "loss_head", [x1, hin, target], [ada8, lnp8, wfd], [(D, F32), (D, BF16)], [(8, D)], ts)


def _mid_bwd(du2, x1, dr2, mix, x, ada8, lnp8, ts):
    D = x.shape[1]

    def body(du2_ref, x1_ref, dr2_ref, mix_ref, x_ref, v_ref, p_ref, dr1_ref, dmix_ref, st_ref):
        @pl.when(pl.program_id(0) == 0)
        def _():
            st_ref[...] = jnp.zeros_like(st_ref)

        du2v = du2_ref[...]
        xh1, rstd1 = _ln_stats(x1_ref[...])
        dx1 = ALPHA * dr2_ref[...] + _ln_bwd(du2v * (1.0 + v_ref[4:5, :]), xh1, rstd1)
        mixv = mix_ref[...]
        g1 = v_ref[2:3, :]
        r1 = ALPHA * x_ref[...] + g1 * mixv
        xhr, rstdr = _ln_stats(r1)
        dr1 = _ln_bwd(dx1 * p_ref[0:1, :], xhr, rstdr)
        dr1_ref[...] = dr1
        dmix_ref[...] = (dr1 * g1).astype(BF16)
        st_ref[0:1, :] += _colsum(du2v * xh1)
        st_ref[1:2, :] += _colsum(du2v)
        st_ref[2:3, :] += _colsum(dx1 * xhr)
        st_ref[3:4, :] += _colsum(dx1)
        st_ref[4:5, :] += _colsum(dr1 * mixv)

    return _rows_call(body, "mid_bwd", [du2, x1, dr2, mix, x], [ada8, lnp8], [(D, F32), (D, BF16)], [(8, D)], ts)


def _first_bwd(pairs, x, dr1, ada8, exchange):
    S, D = x.shape
    norm = [(p[0], p[1]) + ((p[2], p[3], p[4]) if len(p) > 2 else (p[0].shape[1], 0, 0)) for p in pairs]
    tm = _pick(S, (256, 128))
    in_specs, args = [], []
    for a, b, K, ka, kb in norm:
        in_specs += [pl.BlockSpec((tm, K), lambda i, j, ka=ka: (i, ka)), pl.BlockSpec((D, K), lambda i, j, kb=kb: (0, kb))]
        args += [a, b]
    rows = pl.BlockSpec((tm, D), lambda i, j: (i, 0))
    sums = pl.BlockSpec((8, D), lambda i, j: (0, 0))
    n_pairs = len(norm)

    def body(*refs):
        x_ref, dr1_ref, v_ref, gx_ref, st_ref = refs[2 * n_pairs:]

        @pl.when(pl.program_id(0) == 0)
        def _():
            st_ref[...] = jnp.zeros_like(st_ref)

        du1 = None
        for p in range(n_pairs):
            d = _dot(refs[2 * p][...].astype(BF16), refs[2 * p + 1][...].astype(BF16), 1, 1)
            du1 = d if du1 is None else du1 + d
        xh0, rstd0 = _ln_stats(x_ref[...])
        gx_ref[...] = ALPHA * dr1_ref[...] + _ln_bwd(du1 * (1.0 + v_ref[1:2, :]), xh0, rstd0)
        st_ref[0:1, :] += _colsum(du1 * xh0)
        st_ref[1:2, :] += _colsum(du1)

    return _call_carrying(
        body, exchange, "first_bwd", (S // tm, 1), in_specs + [rows, rows, sums], [rows, sums],
        [jax.ShapeDtypeStruct((S, D), F32), jax.ShapeDtypeStruct((8, D), F32)], [], (*args, x, dr1, ada8),
        semantics=("arbitrary", "arbitrary"))


def _split3(v):
    hi = v.astype(BF16)
    r = v - hi.astype(F32)
    mid = r.astype(BF16)
    lo = (r - mid.astype(F32)).astype(BF16)
    return hi, mid, lo


def _fgate_fwd(f, bf_pad, tb):
    S = f.shape[0]

    def body(f_ref, b_ref, fc_ref, carry):
        @pl.when(pl.program_id(0) == 0)
        def _():
            carry[...] = jnp.zeros_like(carry)

        z = f_ref[...] + b_ref[...]
        ls = jnp.minimum(z, 0.0) - jnp.log(1.0 + jnp.exp(-jnp.abs(z)))
        r = lax.broadcasted_iota(jnp.int32, (tb, tb), 0)
        c = lax.broadcasted_iota(jnp.int32, (tb, tb), 1)
        tri = (c <= r).astype(BF16)
        hi, mid, lo = _split3(ls)
        cs = _dot(tri, hi) + _dot(tri, mid) + _dot(tri, lo) + carry[...]
        fc_ref[...] = cs
        carry[...] = cs[tb - 1:tb, :]

    return pl.pallas_call(
        body, name="fgate_fwd", grid=(S // tb,),
        in_specs=[pl.BlockSpec((tb, 128), lambda i: (i, 0)), pl.BlockSpec((1, 128), lambda i: (0, 0))],
        out_specs=pl.BlockSpec((tb, 128), lambda i: (i, 0)),
        out_shape=jax.ShapeDtypeStruct((S, 128), F32),
        scratch_shapes=[pltpu.VMEM((1, 128), F32)],
        compiler_params=_params(("arbitrary",)),
    )(f, bf_pad)


def _fgate_bwd(dfc, f, bf_pad, tb):
    S = f.shape[0]
    nb = S // tb

    def body(d_ref, f_ref, b_ref, df_ref, gb_ref, carry):
        @pl.when(pl.program_id(0) == 0)
        def _():
            carry[...] = jnp.zeros_like(carry)
            gb_ref[...] = jnp.zeros_like(gb_ref)

        r = lax.broadcasted_iota(jnp.int32, (tb, tb), 0)
        c = lax.broadcasted_iota(jnp.int32, (tb, tb), 1)
        tri = (c >= r).astype(BF16)
        hi, mid, lo = _split3(d_ref[...])
        rs = _dot(tri, hi) + _dot(tri, mid) + _dot(tri, lo) + carry[...]
        carry[...] = rs[0:1, :]
        z = f_ref[...] + b_ref[...]
        df = rs * (1.0 / (1.0 + jnp.exp(z)))
        df_ref[...] = df
        gb_ref[0:1, :] += _colsum(df)

    return pl.pallas_call(
        body, name="fgate_bwd", grid=(nb,),
        in_specs=[pl.BlockSpec((tb, 128), lambda i: (nb - 1 - i, 0)),
                  pl.BlockSpec((tb, 128), lambda i: (nb - 1 - i, 0)),
                  pl.BlockSpec((1, 128), lambda i: (0, 0))],
        out_specs=[pl.BlockSpec((tb, 128), lambda i: (nb - 1 - i, 0)), pl.BlockSpec((8, 128), lambda i: (0, 0))],
        out_shape=[jax.ShapeDtypeStruct((S, 128), F32), jax.ShapeDtypeStruct((8, 128), F32)],
        scratch_shapes=[pltpu.VMEM((1, 128), F32)],
        compiler_params=_params(("arbitrary",)),
    )(dfc, f, bf_pad)


def _split2(v):
    hi = v.astype(BF16)
    lo = (v - hi.astype(F32)).astype(BF16)
    return hi, lo


def _head_masks():
    lane = lax.broadcasted_iota(jnp.int32, (1, PAIR), 1)
    m0 = lane < HEAD_DIM
    return m0, jnp.logical_not(m0)


def _sel(mask, v):
    return jnp.where(mask, v, jnp.zeros_like(v))


def _softplus(z):
    return jnp.maximum(z, 0.0) + jnp.log(1.0 + jnp.exp(-jnp.abs(z)))


def _qkv_specs(S, tq, n_pairs, base):
    return [pl.BlockSpec((tq, PAIR), lambda p, i: (i, base + p)),
            pl.BlockSpec((S, PAIR), lambda p, i: (0, base + n_pairs + p)),
            pl.BlockSpec((S, PAIR), lambda p, i: (0, base + 2 * n_pairs + p))]


NEG = -1e30


def _fox_specs(S, tq, n_pairs, base):
    return _qkv_specs(S, tq, n_pairs, base) + [
        pl.BlockSpec((tq, PAIR), lambda p, i: (i, p)),
        pl.BlockSpec((1, S // tq, 8, tq), lambda p, i: (p, 0, 0, 0))]


RC = 32
VANISH = -104.0


def _chunks(n_rows, fn):
    for ci in range(n_rows // RC):
        fn(ci * RC)


def _wide(v, tk):
    return v if tk == 128 else jnp.tile(v, (1, tk // 128))


def _rep(col):
    return jnp.broadcast_to(col, (col.shape[0], 128))


def _per_head(blk, masks):
    sw = pltpu.roll(blk, HEAD_DIM, axis=1)
    return jnp.where(masks[0], blk, sw), jnp.where(masks[0], sw, blk)


def _fill_masked(dst_ref, src_ref, masks, mul=None, ones_lane=None):
    v = src_ref[...]
    if mul is not None:
        v = v * mul
    lane = lax.broadcasted_iota(jnp.int32, (1, PAIR), 1)
    for h in range(2):
        m = _sel(masks[h], v)
        if ones_lane is not None:
            m = jnp.where(lane == ones_lane[h], jnp.ones_like(m), m)
        dst_ref[h] = m


def _tri(tk, cmp):
    r = lax.broadcasted_iota(jnp.int32, (tk, tk), 0)
    c = lax.broadcasted_iota(jnp.int32, (tk, tk), 1)
    return cmp(r, c).astype(BF16)


def _diag_mask(r0, tk, strict):
    row = r0 + lax.broadcasted_iota(jnp.int32, (RC, tk), 0)
    col = lax.broadcasted_iota(jnp.int32, (RC, tk), 1)
    return (col < row) if strict else (col <= row)


def _peer_copies(src_ref, land_ref, send_sems, recv_sems, scatter, receive_side):
    x, y, c = lax.axis_index("x"), lax.axis_index("y"), lax.axis_index("c")
    me = 4 * x + 2 * y + c
    copies = []
    for k in range(1, N_DEV):
        px, py, pc = (1 - x if k & 4 else x), (1 - y if k & 2 else y), (1 - c if k & 1 else c)
        slot = 4 * px + 2 * py + pc
        copies.append(pltpu.make_async_remote_copy(
            src_ref=src_ref.at[slot] if scatter else src_ref,
            dst_ref=land_ref.at[slot] if receive_side else land_ref.at[me],
            send_sem=send_sems.at[k - 1], recv_sem=recv_sems.at[k - 1], device_id=(px, py, pc), device_id_type=MESH))
    return copies


def _call_carrying(body, exchange, name, grid, in_specs, out_specs, out_shape, scratch_shapes, args,
                   semantics=("parallel", "arbitrary")):
    if exchange is None:
        return pl.pallas_call(body, name=name, grid=grid, in_specs=in_specs, out_specs=out_specs, out_shape=out_shape,
                              scratch_shapes=scratch_shapes, compiler_params=_params(semantics))(*args)
    exchanges = [exchange] if isinstance(exchange, tuple) else list(exchange)
    n_in, n_out, n_ex = len(in_specs), len(out_specs), len(exchanges)

    def carrying(*refs):
        srcs = refs[n_in:n_in + n_ex]
        lands = refs[n_in + n_ex + n_out:n_in + 2 * n_ex + n_out]
        sems = refs[len(refs) - 2 * n_ex:]

        def copies(receive_side):
            return [cp for e, (_, scatter) in enumerate(exchanges)
                    for cp in _peer_copies(srcs[e], lands[e], sems[2 * e], sems[2 * e + 1], scatter, receive_side)]

        first = jnp.logical_and(pl.program_id(0) == 0, pl.program_id(1) == 0)
        last = jnp.logical_and(pl.program_id(0) == grid[0] - 1, pl.program_id(1) == grid[1] - 1)

        @pl.when(first)
        def _():
            for cp in copies(False):
                cp.start()

        body(*refs[:n_in], *refs[n_in + n_ex:n_in + n_ex + n_out], *refs[n_in + 2 * n_ex + n_out:len(refs) - 2 * n_ex])

        @pl.when(last)
        def _():
            for cp in copies(True):
                cp.wait_send()
                cp.wait_recv()

    any_space = pl.BlockSpec(memory_space=pl.ANY)
    lands = [jax.ShapeDtypeStruct((N_DEV,) + src.shape[-2:], src.dtype) for src, _ in exchanges]
    return pl.pallas_call(
        carrying, name=name, grid=grid, in_specs=list(in_specs) + [any_space] * n_ex,
        out_specs=list(out_specs) + [any_space] * n_ex, out_shape=list(out_shape) + lands,
        scratch_shapes=list(scratch_shapes) + [pltpu.SemaphoreType.DMA((N_DEV - 1,))] * (2 * n_ex),
        compiler_params=_params(("arbitrary", "arbitrary")))(*args, *[src for src, _ in exchanges])


def _staggered(bodies):
    active, waiting = [], list(bodies)
    while waiting or active:
        if waiting:
            active.append(waiting.pop(0))
        for g in list(active):
            try:
                next(g)
            except StopIteration:
                active.remove(g)


def _streams(tile, j, diag, slot):
    return [tile(j, diag, slot, (0, 1))]


def _tiles(i, tile):
    def step(jj, carry):
        _staggered(_streams(tile, 2 * jj, False, 0) + _streams(tile, 2 * jj + 1, False, 1))
        return carry
    lax.fori_loop(0, (i - 1) // 2, step, 0)

    @pl.when(jnp.logical_and(i >= 1, (i - 1) % 2 == 1))
    def _():
        _staggered(_streams(tile, i - 2, False, 0))

    @pl.when(i >= 1)
    def _():
        _staggered(_streams(tile, i - 1, False, 0) + _streams(tile, i, True, 1))

    @pl.when(i == 0)
    def _():
        _staggered(_streams(tile, i, True, 1))


def _tiles_reversed(i, tile, keep_going):
    @pl.when(i == 0)
    def _():
        _staggered(_streams(tile, i, True, 0))

    @pl.when(i >= 1)
    def _():
        _staggered(_streams(tile, i, True, 0) + _streams(tile, i - 1, False, 1))

    pairs = (i - 1) // 2

    def cond(carry):
        jj, go = carry
        return jnp.logical_and(jj < pairs, go)

    def step(carry):
        jj, _ = carry
        _staggered(_streams(tile, i - 2 - 2 * jj, False, 0) + _streams(tile, i - 3 - 2 * jj, False, 1))
        return jj + 1, keep_going(jnp.maximum(i - 4 - 2 * jj, 0))

    jj, go = lax.while_loop(cond, step, (jnp.int32(0), keep_going(jnp.maximum(i - 2, 0))))

    @pl.when(jnp.logical_and(jnp.logical_and(i >= 1, (i - 1) % 2 == 1), jnp.logical_and(jj == pairs, go)))
    def _():
        _staggered(_streams(tile, 0, False, 0))


def _sb_fwd(qkv, n_pairs, base, tq, exchange=None):
    S = qkv.shape[0]
    tk = tq
    scale = HEAD_DIM ** -0.5

    def body(q_ref, k_ref, v_ref, o_ref, t_ref, z_ref, hi_ref, suf_ref, p_ref, r_ref, acc_ref, vm_ref):
        i = pl.program_id(1)
        masks = _head_masks()

        @pl.when(i == 0)
        def _():
            _fill_masked(vm_ref, v_ref, masks)

        q2 = q_ref[...] * scale
        qm = [_sel(m, q2) for m in masks]
        incl = _tri(tk, lambda r, c: r >= c)
        r_ref[...] = jnp.zeros_like(r_ref)
        acc_ref[...] = jnp.zeros_like(acc_ref)

        def tile(j, diag, slot, heads):
            off = pl.multiple_of(j * tk, tk)
            k2 = k_ref[pl.ds(off, tk), :]
            v2 = v_ref[pl.ds(off, tk), :]
            for h in heads:
                z_ref[2 * slot + h] = _dot(qm[h], k2, 1, 1)
            yield
            for h in heads:
                def split(r0, h=h):
                    rows = pl.ds(r0, RC)
                    lg = -_softplus(z_ref[2 * slot + h, rows, :])
                    if diag:
                        lg = jnp.where(_diag_mask(r0, tk, True), lg, 0.0)
                    hi_ref[2 * slot + h, rows, :] = lg.astype(BF16)
                _chunks(tq, split)
            yield
            for h in heads:
                suf_ref[2 * slot + h] = _dot(hi_ref[2 * slot + h], incl)
            yield
            for h in heads:
                def weights(r0, h=h):
                    rows = pl.ds(r0, RC)
                    a = jnp.exp(z_ref[2 * slot + h, rows, :] + suf_ref[2 * slot + h, rows, :] + _wide(r_ref[h, rows, :], tk))
                    if diag:
                        a = jnp.where(_diag_mask(r0, tk, True), a, 0.0)
                    p_ref[2 * slot + h, rows, :] = a.astype(BF16)
                _chunks(tq, weights)
            yield
            keys = pl.ds(off, tk)
            for h in heads:
                acc_ref[...] += _dot(p_ref[2 * slot + h], vm_ref[h, keys, :])
            for h in heads:
                r_ref[h] += _rep(suf_ref[2 * slot + h, :, 0:1])

        _tiles_reversed(i, tile, lambda nearest: jnp.max(r_ref[...]) >= VANISH)
        o_ref[...] = acc_ref[...].astype(BF16)
        t_ref[...] = acc_ref[...]

    W = n_pairs * PAIR
    return _call_carrying(
        body, exchange, "sb_fwd", (n_pairs, S // tq), _qkv_specs(S, tq, n_pairs, base),
        [pl.BlockSpec((tq, PAIR), lambda p, i: (i, p)), pl.BlockSpec((tq, PAIR), lambda p, i: (i, p))],
        [jax.ShapeDtypeStruct((S, W), BF16), jax.ShapeDtypeStruct((S, W), F32)],
        [pltpu.VMEM((4, tq, tk), F32), pltpu.VMEM((4, tq, tk), BF16),
         pltpu.VMEM((4, tq, tk), F32), pltpu.VMEM((4, tq, tk), BF16), pltpu.VMEM((2, tq, 128), F32),
         pltpu.VMEM((tq, PAIR), F32), pltpu.VMEM((2, S, PAIR), BF16)],
        (qkv, qkv, qkv))


def _sb_bwd(qkv, o32, do, n_pairs, base, tq, exchange=None):
    S = qkv.shape[0]
    tk = tq
    nq = S // tq
    scale = HEAD_DIM ** -0.5

    def body(q_ref, k_ref, v_ref, o_ref, do_ref, dq_ref, dk_ref, dv_ref,
             z_ref, g_ref, omb_ref, cum_ref, hi_ref, lo_ref, a_ref, dz_ref,
             r_ref, cg_ref, dl_ref, dq_acc, dk_acc, dv_acc, ks_ref):
        i = pl.program_id(1)
        masks = _head_masks()

        @pl.when(i == 0)
        def _():
            dk_acc[...] = jnp.zeros_like(dk_acc)
            dv_acc[...] = jnp.zeros_like(dv_acc)
            _fill_masked(ks_ref, k_ref, masks, mul=scale)

        q2 = q_ref[...] * scale
        do2 = do_ref[...]
        qm = [_sel(m, q2) for m in masks]
        dom = [_sel(m, do2) for m in masks]
        prod = do2.astype(F32) * o_ref[...]
        for h in range(2):
            dl_ref[h] = _rep(jnp.sum(jnp.where(masks[h], prod, 0.0), axis=-1, keepdims=True))
        suffix = _tri(tk, lambda r, c: r >= c)
        r_ref[...] = jnp.zeros_like(r_ref)
        cg_ref[...] = jnp.zeros_like(cg_ref)
        dq_acc[...] = jnp.zeros_like(dq_acc)

        def tile(j, diag, slot, heads):
            off = pl.multiple_of(j * tk, tk)
            k2 = k_ref[pl.ds(off, tk), :]
            v2 = v_ref[pl.ds(off, tk), :]
            for h in heads:
                z_ref[2 * slot + h] = _dot(qm[h], k2, 1, 1)
                g_ref[2 * slot + h] = _dot(dom[h], v2, 1, 1)
            yield
            for h in heads:
                def split(r0, h=h):
                    rows = pl.ds(r0, RC)
                    sp = _softplus(z_ref[2 * slot + h, rows, :])
                    omb_ref[2 * slot + h, rows, :] = jnp.exp(-sp)
                    lg = -sp
                    if diag:
                        lg = jnp.where(_diag_mask(r0, tk, True), lg, 0.0)
                    hi_ref[2 * slot + h, rows, :] = lg.astype(BF16)
                _chunks(tq, split)
            yield
            for h in heads:
                cum_ref[2 * slot + h] = _dot(hi_ref[2 * slot + h], suffix)
            yield
            for h in heads:
                def weights(r0, h=h):
                    rows = pl.ds(r0, RC)
                    a = jnp.exp(z_ref[2 * slot + h, rows, :] + cum_ref[2 * slot + h, rows, :] + _wide(r_ref[h, rows, :], tk))
                    if diag:
                        a = jnp.where(_diag_mask(r0, tk, True), a, 0.0)
                    ab = a.astype(BF16)
                    g = g_ref[2 * slot + h, rows, :] * ab.astype(F32)
                    g_ref[2 * slot + h, rows, :] = g
                    a_ref[2 * slot + h, rows, :] = ab
                    hi, lo = _split2(g)
                    hi_ref[2 * slot + h, rows, :] = hi
                    lo_ref[2 * slot + h, rows, :] = lo
                _chunks(tq, weights)
            for h in heads:
                r_ref[h] += _rep(cum_ref[2 * slot + h, :, 0:1])
            yield
            for h in heads:
                cum_ref[2 * slot + h] = _dot(hi_ref[2 * slot + h], suffix) + _dot(lo_ref[2 * slot + h], suffix)
            yield
            for h in heads:
                def dscore(r0, h=h):
                    rows = pl.ds(r0, RC)
                    g = g_ref[2 * slot + h, rows, :]
                    from_here = cum_ref[2 * slot + h, rows, :] + _wide(cg_ref[h, rows, :], tk)
                    before = _wide(dl_ref[h, rows, :], tk) - from_here
                    omb = omb_ref[2 * slot + h, rows, :]
                    dz = g * omb - (1.0 - omb) * before
                    if diag:
                        dz = jnp.where(_diag_mask(r0, tk, True), dz, 0.0)
                    dz_ref[2 * slot + h, rows, :] = dz.astype(BF16)
                _chunks(tq, dscore)
            for h in heads:
                cg_ref[h] += _rep(cum_ref[2 * slot + h, :, 0:1])
            yield
            keys = pl.ds(off, tk)
            for h in heads:
                dq_acc[...] += _dot(dz_ref[2 * slot + h], ks_ref[h, keys, :])
                dk_acc[keys, :] += _dot(dz_ref[2 * slot + h], qm[h], 0, 0)
                dv_acc[keys, :] += _dot(a_ref[2 * slot + h], dom[h], 0, 0)

        _tiles_reversed(i, tile, lambda nearest: jnp.max(r_ref[...]) >= VANISH)
        dq_ref[...] = dq_acc[...].astype(BF16)

        @pl.when(i == nq - 1)
        def _():
            dk_ref[...] = dk_acc[...].astype(BF16)
            dv_ref[...] = dv_acc[...].astype(BF16)

    W = n_pairs * PAIR
    in_specs = _qkv_specs(S, tq, n_pairs, base) + [
        pl.BlockSpec((tq, PAIR), lambda p, i: (i, p)),
        pl.BlockSpec((tq, PAIR), lambda p, i: (i, p))]
    out_specs = [pl.BlockSpec((tq, PAIR), lambda p, i: (i, p)),
                 pl.BlockSpec((S, PAIR), lambda p, i: (0, p)),
                 pl.BlockSpec((S, PAIR), lambda p, i: (0, p))]
    big, stat = (4, tq, tk), (2, tq, 128)
    return _call_carrying(
        body, exchange, "sb_bwd", (n_pairs, nq), in_specs, out_specs, [jax.ShapeDtypeStruct((S, W), BF16)] * 3,
        [pltpu.VMEM(big, F32)] * 4 + [pltpu.VMEM(big, BF16)] * 4 + [pltpu.VMEM(stat, F32)] * 3
        + [pltpu.VMEM((tq, PAIR), F32), pltpu.VMEM((S, PAIR), F32), pltpu.VMEM((S, PAIR), F32),
           pltpu.VMEM((2, S, PAIR), BF16)],
        (qkv, qkv, qkv, o32, do))


def _fox_fwd(qkv, fcx, fcr, n_pairs, base, tq, exchange=None):
    S = qkv.shape[0]
    tk = tq
    scale = HEAD_DIM ** -0.5
    spare = (HEAD_DIM, 0)

    def body(q_ref, k_ref, v_ref, fq_ref, fk_ref, o_ref, lse_ref, s_ref, p_ref, m_ref, al_ref, fqr_ref, acc_ref, vm_ref):
        i = pl.program_id(1)
        masks = _head_masks()

        @pl.when(i == 0)
        def _():
            _fill_masked(vm_ref, v_ref, masks, ones_lane=spare)

        q2 = q_ref[...] * scale
        qm = [_sel(m, q2) for m in masks]
        f0, f1 = _per_head(fq_ref[...], masks)
        fqr_ref[0] = f0
        fqr_ref[1] = f1
        m_ref[...] = jnp.full(m_ref.shape, NEG, F32)
        acc_ref[...] = jnp.zeros_like(acc_ref)

        def tile(j, diag, slot, heads):
            off = pl.multiple_of(j * tk, tk)
            k2 = k_ref[pl.ds(off, tk), :]
            v2 = v_ref[pl.ds(off, tk), :]
            fk2 = fk_ref[0, j]
            for h in heads:
                s_ref[2 * slot + h] = _dot(qm[h], k2, 1, 1)
            yield
            for h in heads:
                fk_row = fk2[h:h + 1, :]

                def probs(r0, h=h, fk_row=fk_row):
                    rows = pl.ds(r0, RC)
                    sv = s_ref[2 * slot + h, rows, :] - fk_row
                    if diag:
                        sv = jnp.where(_diag_mask(r0, tk, False), sv, NEG)
                    fq = fqr_ref[h, rows, :]
                    m_prev = m_ref[h, rows, :]
                    m_new = jnp.maximum(m_prev, jnp.max(sv, axis=-1, keepdims=True) + fq)
                    p_ref[2 * slot + h, rows, :] = jnp.exp(sv + _wide(fq - m_new, tk)).astype(BF16)
                    al_ref[2 * slot + h, rows, :] = jnp.exp(m_prev - m_new)
                    m_ref[h, rows, :] = m_new
                _chunks(tq, probs)
            yield
            for h in heads:
                acc_ref[h] = acc_ref[h] * al_ref[2 * slot + h] + _dot(p_ref[2 * slot + h], vm_ref[h, pl.ds(off, tk), :])

        _tiles(i, tile)
        a0, a1 = acc_ref[0], acc_ref[1]
        l0 = _rep(a0[:, spare[0]:spare[0] + 1])
        l1 = _rep(a1[:, spare[1]:spare[1] + 1])
        o_ref[...] = jnp.where(masks[0], a0 / l0, a1 / l1).astype(BF16)
        lse_ref[...] = jnp.where(masks[0], m_ref[0] + jnp.log(l0), m_ref[1] + jnp.log(l1))

    W = n_pairs * PAIR
    return _call_carrying(
        body, exchange, "fox_fwd", (n_pairs, S // tq), _fox_specs(S, tq, n_pairs, base),
        [pl.BlockSpec((tq, PAIR), lambda p, i: (i, p)), pl.BlockSpec((tq, PAIR), lambda p, i: (i, p))],
        [jax.ShapeDtypeStruct((S, W), BF16), jax.ShapeDtypeStruct((S, W), F32)],
        [pltpu.VMEM((4, tq, tk), F32), pltpu.VMEM((4, tq, tk), BF16), pltpu.VMEM((2, tq, 128), F32),
         pltpu.VMEM((4, tq, 128), F32), pltpu.VMEM((2, tq, 128), F32), pltpu.VMEM((2, tq, 128), F32),
         pltpu.VMEM((2, S, PAIR), BF16)],
        (qkv, qkv, qkv, fcx, fcr))


def _fox_bwd(qkv, fcx, fcr, o, lse, do, n_pairs, base, tq, exchange=None):
    S = qkv.shape[0]
    tk = tq
    nq = S // tq
    scale = HEAD_DIM ** -0.5

    def body(q_ref, k_ref, v_ref, fq_ref, fk_ref, o_ref, lse_ref, do_ref,
             dq_ref, dk_ref, dv_ref, dfq_ref, dfk_ref,
             s_ref, dp_ref, p_ref, ds_ref, row_ref, dl_ref, dfq_acc, col_ref, dq_acc, dk_acc, dv_acc, ks_ref):
        i = pl.program_id(1)
        masks = _head_masks()

        @pl.when(i == 0)
        def _():
            dk_acc[...] = jnp.zeros_like(dk_acc)
            dv_acc[...] = jnp.zeros_like(dv_acc)
            dfk_ref[...] = jnp.zeros_like(dfk_ref)
            _fill_masked(ks_ref, k_ref, masks, mul=scale)

        q2 = q_ref[...] * scale
        do2 = do_ref[...]
        qm = [_sel(m, q2) for m in masks]
        dom = [_sel(m, do2) for m in masks]
        f0, f1 = _per_head(fq_ref[...], masks)
        l0, l1 = _per_head(lse_ref[...], masks)
        row_ref[0] = f0 - l0
        row_ref[1] = f1 - l1
        prod = do2.astype(F32) * o_ref[...].astype(F32)
        for h in range(2):
            dl_ref[h] = _rep(jnp.sum(jnp.where(masks[h], prod, 0.0), axis=-1, keepdims=True))
        dfq_acc[...] = jnp.zeros_like(dfq_acc)
        dq_acc[...] = jnp.zeros_like(dq_acc)

        def tile(j, diag, slot, heads):
            off = pl.multiple_of(j * tk, tk)
            k2 = k_ref[pl.ds(off, tk), :]
            v2 = v_ref[pl.ds(off, tk), :]
            fk2 = fk_ref[0, j]
            for h in heads:
                s_ref[2 * slot + h] = _dot(qm[h], k2, 1, 1)
                dp_ref[2 * slot + h] = _dot(dom[h], v2, 1, 1)
            yield
            for h in heads:
                col_ref[2 * slot + h] = jnp.zeros((8, tk), F32)
                fk_row = fk2[h:h + 1, :]

                def dscore(r0, h=h, fk_row=fk_row):
                    rows = pl.ds(r0, RC)
                    p = jnp.exp(s_ref[2 * slot + h, rows, :] - fk_row + _wide(row_ref[h, rows, :], tk))
                    if diag:
                        p = jnp.where(_diag_mask(r0, tk, False), p, 0.0)
                    ds = p * (dp_ref[2 * slot + h, rows, :] - _wide(dl_ref[h, rows, :], tk))
                    p_ref[2 * slot + h, rows, :] = p.astype(BF16)
                    ds_ref[2 * slot + h, rows, :] = ds.astype(BF16)
                    dfq_acc[h, rows, :] += _rep(jnp.sum(ds, axis=-1, keepdims=True))
                    col_ref[2 * slot + h] += jnp.sum(ds.reshape(RC // 8, 8, tk), axis=0)
                _chunks(tq, dscore)
            yield
            keys = pl.ds(off, tk)
            for h in heads:
                dq_acc[...] += _dot(ds_ref[2 * slot + h], ks_ref[h, keys, :])
                dk_acc[keys, :] += _dot(ds_ref[2 * slot + h], qm[h], 0, 0)
                dv_acc[keys, :] += _dot(p_ref[2 * slot + h], dom[h], 0, 0)
            for h in heads:
                dfk_ref[0, j, h:h + 1, :] += jnp.sum(col_ref[2 * slot + h], axis=0, keepdims=True)

        _tiles(i, tile)
        dq_ref[...] = dq_acc[...].astype(BF16)
        dfq_ref[...] = jnp.where(masks[0], dfq_acc[0], dfq_acc[1])

        @pl.when(i == nq - 1)
        def _():
            dk_ref[...] = dk_acc[...].astype(BF16)
            dv_ref[...] = dv_acc[...].astype(BF16)

    W = n_pairs * PAIR
    in_specs = _fox_specs(S, tq, n_pairs, base) + [
        pl.BlockSpec((tq, PAIR), lambda p, i: (i, p)),
        pl.BlockSpec((tq, PAIR), lambda p, i: (i, p)),
        pl.BlockSpec((tq, PAIR), lambda p, i: (i, p))]
    out_specs = [pl.BlockSpec((tq, PAIR), lambda p, i: (i, p)),
                 pl.BlockSpec((S, PAIR), lambda p, i: (0, p)),
                 pl.BlockSpec((S, PAIR), lambda p, i: (0, p)),
                 pl.BlockSpec((tq, PAIR), lambda p, i: (i, p)),
                 pl.BlockSpec((1, nq, 8, tk), lambda p, i: (p, 0, 0, 0))]
    return _call_carrying(
        body, exchange, "fox_bwd", (n_pairs, nq), in_specs, out_specs,
        [jax.ShapeDtypeStruct((S, W), BF16)] * 3
        + [jax.ShapeDtypeStruct((S, W), F32), jax.ShapeDtypeStruct((n_pairs, nq, 8, tk), F32)],
        [pltpu.VMEM((4, tq, tk), F32)] * 2 + [pltpu.VMEM((4, tq, tk), BF16)] * 2
        + [pltpu.VMEM((2, tq, 128), F32)] * 3 + [pltpu.VMEM((4, 8, tk), F32)]
        + [pltpu.VMEM((tq, PAIR), F32), pltpu.VMEM((S, PAIR), F32), pltpu.VMEM((S, PAIR), F32),
           pltpu.VMEM((2, S, PAIR), BF16)],
        (qkv, qkv, qkv, fcx, fcr, o, lse, do))


def _col_chunks(n, width=256):
    return [(c, min(width, n - c)) for c in range(0, n, width)]


def _swiglu_fwd(u2, wg, wu):
    S, D = u2.shape
    FF = wg.shape[1]
    tm, tn = _pick(S, (512, 256, 128)), _divisors(FF, 1536)[0]

    def body(u_ref, g_ref, w_ref, a_ref, b_ref, h_ref):
        u = u_ref[...]
        for c, w in _col_chunks(tn):
            cols = slice(c, c + w)
            a = _dot(u, g_ref[:, cols])
            b = _dot(u, w_ref[:, cols])
            a_ref[:, cols] = a.astype(BF16)
            b_ref[:, cols] = b.astype(BF16)
            h_ref[:, cols] = (a / (1.0 + jnp.exp(-a)) * b).astype(BF16)

    spec_o = pl.BlockSpec((tm, tn), lambda i, j: (i, j))
    return pl.pallas_call(
        body, name="swiglu_fwd", grid=(S // tm, FF // tn),
        in_specs=[pl.BlockSpec((tm, D), lambda i, j: (i, 0)),
                  pl.BlockSpec((D, tn), lambda i, j: (0, j)),
                  pl.BlockSpec((D, tn), lambda i, j: (0, j))],
        out_specs=[spec_o] * 3, out_shape=[jax.ShapeDtypeStruct((S, FF), BF16)] * 3,
        compiler_params=_params(("parallel", "parallel")),
    )(u2, wg, wu)


def _swiglu_bwd(dh, wd, a, b):
    S, D = dh.shape
    FF = wd.shape[0]
    tm, tn = _pick(S, (512, 256, 128)), _divisors(FF, 1536)[0]

    def body(dh_ref, w_ref, a_ref, b_ref, da_ref, db_ref):
        dh_blk = dh_ref[...]
        for c, w in _col_chunks(tn):
            cols = slice(c, c + w)
            dhin = _dot(dh_blk, w_ref[cols, :], 1, 1)
            av = a_ref[:, cols].astype(F32)
            bv = b_ref[:, cols].astype(F32)
            sig = 1.0 / (1.0 + jnp.exp(-av))
            da_ref[:, cols] = (dhin * bv * (sig * (1.0 + av * (1.0 - sig)))).astype(BF16)
            db_ref[:, cols] = (dhin * (av * sig)).astype(BF16)

    spec_o = pl.BlockSpec((tm, tn), lambda i, j: (i, j))
    return pl.pallas_call(
        body, name="swiglu_bwd", grid=(S // tm, FF // tn),
        in_specs=[pl.BlockSpec((tm, D), lambda i, j: (i, 0)),
                  pl.BlockSpec((tn, D), lambda i, j: (j, 0)), spec_o, spec_o],
        out_specs=[spec_o] * 2, out_shape=[jax.ShapeDtypeStruct((S, FF), BF16)] * 2,
        compiler_params=_params(("parallel", "parallel")),
    )(dh, wd, a, b)


def _gate_bwd(dmix, wo, g_sb, g_fx, y_sb, y_fx):
    S, D = dmix.shape
    tm, tn = _pick(S, (512, 256, 128)), _divisors(D, 1024)[0]

    def body(dm_ref, w_ref, gs_ref, gf_ref, ys_ref, yf_ref, dys_ref, dyf_ref, dls_ref, dlf_ref, bs_ref, bf_ref):
        @pl.when(pl.program_id(1) == 0)
        def _():
            bs_ref[...] = jnp.zeros_like(bs_ref)
            bf_ref[...] = jnp.zeros_like(bf_ref)

        dm_blk = dm_ref[...]
        for c, w in _col_chunks(tn):
            cols = slice(c, c + w)
            dmi = _dot(dm_blk, w_ref[cols, :], 1, 1)
            gs, gf = gs_ref[:, cols], gf_ref[:, cols]
            dys_ref[:, cols] = (dmi * gs).astype(BF16)
            dyf_ref[:, cols] = (dmi * gf).astype(BF16)
            dls = dmi * ys_ref[:, cols] * gs * (1.0 - gs)
            dlf = dmi * yf_ref[:, cols] * gf * (1.0 - gf)
            dls_ref[:, cols] = dls.astype(BF16)
            dlf_ref[:, cols] = dlf.astype(BF16)
            bs_ref[0:1, cols] += _colsum(dls)
            bf_ref[0:1, cols] += _colsum(dlf)

    t = pl.BlockSpec((tm, tn), lambda j, i: (i, j))
    accs = pl.BlockSpec((8, tn), lambda j, i: (0, j))
    return pl.pallas_call(
        body, name="gate_bwd", grid=(D // tn, S // tm),
        in_specs=[pl.BlockSpec((tm, D), lambda j, i: (i, 0)),
                  pl.BlockSpec((tn, D), lambda j, i: (j, 0)), t, t, t, t],
        out_specs=[t, t, t, t, accs, accs],
        out_shape=[jax.ShapeDtypeStruct((S, D), BF16)] * 4 + [jax.ShapeDtypeStruct((8, D), F32)] * 2,
        compiler_params=_params(("parallel", "arbitrary")),
    )(dmix, wo, g_sb, g_fx, y_sb, y_fx)


def _local_step(x, target, ada8, lnp8, bg_sb, bg_fx, bf_pad, wqkv, wf, wgs, wgf, gather, later_weights, pack_early,
                pack_last):
    S, D = x.shape
    W = wqkv.shape[1] // 6
    n_pairs = W // PAIR
    n_heads = W // HEAD_DIM
    ts = _pick(S, (512, 256, 128))
    tq = _pick(S, (256, 128))

    u1 = _ln_mod(x, ada8, ts)
    qkv = _mm([(u1, wqkv)], 'nn', BF16, "in_qkv")
    f = _mm([(u1, wf)], 'nn', F32, "in_f")
    g_sb = _mm([(u1, wgs)], 'nn', F32, "in_gsb", bias=bg_sb, act='sigmoid')
    g_fx = _mm([(u1, wgf)], 'nn', F32, "in_gfx", bias=bg_fx, act='sigmoid')
    fc = _fgate_fwd(f, bf_pad, _pick(S, (512, 256, 128)))
    fch = fc[:, :n_heads]
    fcx = jnp.repeat(fch, HEAD_DIM, axis=1)
    nq = S // tq
    fcr = jnp.pad(fch.T.reshape(n_pairs, 2, nq, tq).transpose(0, 2, 1, 3),
                  ((0, 0), (0, 0), (0, 6), (0, 0)))
    o_sb, o_sb32, *zone_a = _sb_fwd(qkv, n_pairs, 0, tq, gather[0])
    o_fx, lse, *zone_b = _fox_fwd(qkv, fcx, fcr, n_pairs, 3 * n_pairs, tq, gather[1])
    wsb, wfx, wo, wfg, wfu, wfd = later_weights(*zone_a, *zone_b)
    y_sb =_mm([(o_sb, wsb)], 'nn', F32, "out_sb")
    y_fx = _mm([(o_fx, wfx)], 'nn', F32, "out_fx")
    mix_in, mix = _gate_mix_out(g_sb, g_fx, y_sb, y_fx, wo, ts)
    x1, u2 = _post_attn(x, mix, ada8, lnp8, ts)
    a, b, hin = _swiglu_fwd(u2, wfg, wfu)
    dr2, dh, st_loss = _loss_head(x1, hin, wfd, target, ada8, lnp8, ts)

    da, db = _swiglu_bwd(dh, wfd, a, b)
    g_wfd = _mm([(hin, dh)], 'tn', F32, "g_ffn_down")
    du2 = _mm([(da, wfg), (db, wfu)], 'nt', F32, "d_u2")
    g_wfg = _mm([(u2, da)], 'tn', F32, "g_ffn_gate")
    g_wfu = _mm([(u2, db)], 'tn', F32, "g_ffn_up")
    dr1, dmix, st_mid = _mid_bwd(du2, x1, dr2, mix, x, ada8, lnp8, ts)
    dys, dyf, dls, dlf, gb_sb, gb_fx = _gate_bwd(dmix, wo, g_sb, g_fx, y_sb, y_fx)
    g_wo = _mm([(mix_in, dmix)], 'tn', F32, "g_w_o")
    do_sb = _mm([(dys, wsb)], 'nt', BF16, "d_o_sb")
    do_fx = _mm([(dyf, wfx)], 'nt', BF16, "d_o_fx")
    g_wsb = _mm([(o_sb, dys)], 'tn', F32, "g_sb_out")
    g_wfx = _mm([(o_fx, dyf)], 'tn', F32, "g_fox_out")
    scatter = pack_early(dict(sb=g_wsb, fx=g_wfx, o=g_wo, fg=g_wfg, fu=g_wfu, fd=g_wfd))
    dq_s, dk_s, dv_s, *zone_a = _sb_bwd(qkv, o_sb32, do_sb, n_pairs, 0, tq, scatter[0])
    dq_f, dk_f, dv_f, dfq, dfk, *zone_b = _fox_bwd(qkv, fcx, fcr, o_fx, lse, do_fx, n_pairs, 3 * n_pairs, tq,
                                                    scatter[1])
    early = [(scatter[0], zone_a[0] if zone_a else None), (scatter[1], zone_b[0] if zone_b else None)]
    dfc = dfq[:, ::HEAD_DIM] - dfk[:, :, :2, :].transpose(0, 2, 1, 3).reshape(n_heads, S).T
    dfc = jnp.pad(dfc, ((0, 0), (0, 128 - n_heads)))
    df, gb_f = _fgate_bwd(dfc, f, bf_pad, _pick(S, (512, 256, 128)))
    dqkv = jnp.concatenate([dq_s, dk_s, dv_s, dq_f, dk_f, dv_f], axis=1)
    g_wqkv = [_mm([(u1, dqkv)], 'tn', F32, "g_in_qkv")]
    g_wf = _mm([(u1, df)], 'tn', F32, "g_in_f")
    g_wgs = _mm([(u1, dls)], 'tn', F32, "g_in_gsb")
    g_wgf = _mm([(u1, dlf)], 'tn', F32, "g_in_gfx")
    wgrads = dict(qkv=g_wqkv, f=g_wf, gs=g_wgs, gf=g_wgf)
    last = pack_last(wgrads)
    gx, st_first, *last_zone = _first_bwd(
        [(dqkv, wqkv), (df, wf), (dls, wgs), (dlf, wgf)], x, dr1, ada8, last)
    last_zone = last_zone[0] if last_zone else None

    stats = dict(loss=st_loss, mid=st_mid, first=st_first, gb_sb=gb_sb, gb_fx=gb_fx, gb_f=gb_f)
    return gx, wgrads, stats, early, (last, last_zone)


def _position():
    x, y, c = lax.axis_index("x"), lax.axis_index("y"), lax.axis_index("c")
    return x, y, c, 4 * x + 2 * y + c


def _flip(x, y, c, k):
    px = 1 - x if k & 4 else x
    py = 1 - y if k & 2 else y
    pc = 1 - c if k & 1 else c
    return (px, py, pc), 4 * px + 2 * py + pc


def _all_gather_small(v, name):
    r, n = v.shape

    def body(x_ref, out_ref, send_sems, recv_sems, local_sem):
        x, y, c, me = _position()
        mine = pltpu.make_async_copy(x_ref, out_ref.at[me], local_sem)
        mine.start()
        sends = []
        for k in range(1, N_DEV):
            peer, _ = _flip(x, y, c, k)
            cp = pltpu.make_async_remote_copy(
                src_ref=x_ref, dst_ref=out_ref.at[me], send_sem=send_sems.at[k - 1], recv_sem=recv_sems.at[k - 1],
                device_id=peer, device_id_type=MESH)
            cp.start()
            sends.append(cp)
        for k in range(1, N_DEV):
            peer, slot = _flip(x, y, c, k)
            pltpu.make_async_remote_copy(
                src_ref=x_ref, dst_ref=out_ref.at[slot], send_sem=send_sems.at[k - 1], recv_sem=recv_sems.at[k - 1],
                device_id=peer, device_id_type=MESH).wait_recv()
        for cp in sends:
            cp.wait_send()
        mine.wait()

    return pl.pallas_call(
        body, name=name, out_shape=jax.ShapeDtypeStruct((N_DEV, r, n), v.dtype),
        in_specs=[pl.BlockSpec(memory_space=pltpu.VMEM)], out_specs=pl.BlockSpec(memory_space=pltpu.VMEM),
        scratch_shapes=[pltpu.SemaphoreType.DMA((N_DEV - 1,)), pltpu.SemaphoreType.DMA((N_DEV - 1,)),
                        pltpu.SemaphoreType.DMA],
    )(v)


def _all_gather_weights(packed):
    R, C = packed.shape

    def body(x_ref, out_ref, send_sems, recv_sems, local_sem):
        x, y, c, me = _position()
        sibling, sib_slot = _flip(x, y, c, 1)
        mine = pltpu.make_async_copy(x_ref, out_ref.at[me], local_sem)
        mine.start()

        def copy(k, slot, to, src=None):
            return pltpu.make_async_remote_copy(
                src_ref=out_ref.at[slot] if src is None else src, dst_ref=out_ref.at[slot],
                send_sem=send_sems.at[k], recv_sem=recv_sems.at[k], device_id=to, device_id_type=MESH)

        first = [copy(0, me, sibling, src=x_ref)]
        chips = (4, 2, 6)
        for n, k in enumerate(chips):
            peer, _ = _flip(x, y, c, k)
            first.append(copy(1 + n, me, peer, src=x_ref))
        for cp in first:
            cp.start()
        passed = []
        for n, k in enumerate(chips):
            peer, slot = _flip(x, y, c, k)
            copy(1 + n, slot, peer).wait_recv()
            cp = copy(4 + n, slot, sibling)
            cp.start()
            passed.append(cp)
        copy(0, sib_slot, sibling).wait_recv()
        for n, k in enumerate(chips):
            _, slot = _flip(x, y, c, k | 1)
            copy(4 + n, slot, sibling).wait_recv()
        for cp in first + passed:
            cp.wait_send()
        mine.wait()

    return pl.pallas_call(
        body, name="all_gather_weights", out_shape=jax.ShapeDtypeStruct((N_DEV, R, C), packed.dtype),
        in_specs=[pl.BlockSpec(memory_space=pl.ANY)], out_specs=pl.BlockSpec(memory_space=pl.ANY),
        scratch_shapes=[pltpu.SemaphoreType.DMA((7,)), pltpu.SemaphoreType.DMA((7,)), pltpu.SemaphoreType.DMA],
    )(packed)


def _own_slot(land, own):
    me = 4 * lax.axis_index("x") + 2 * lax.axis_index("y") + lax.axis_index("c")
    return lax.dynamic_update_slice(land, own[None], (me, 0, 0))


def _sum_slots(recv, name, tr):
    n, R, C = recv.shape

    def body(r_ref, o_ref):
        acc = r_ref[0].astype(F32)
        for s in range(1, n):
            acc = acc + r_ref[s].astype(F32)
        o_ref[...] = acc

    return pl.pallas_call(
        body, name=name, grid=(R // tr,), in_specs=[pl.BlockSpec((n, tr, C), lambda i: (0, i, 0))],
        out_specs=pl.BlockSpec((tr, C), lambda i: (i, 0)), out_shape=jax.ShapeDtypeStruct((R, C), F32),
        compiler_params=_params(("parallel",)),
    )(recv)


def _sum_stats(st_all, loss_row):
    n, r, D = st_all.shape

    def body(s_ref, o_ref, l_ref):
        acc = s_ref[0]
        for d in range(1, n):
            acc = acc + s_ref[d]
        o_ref[...] = acc
        l_ref[...] = jnp.zeros((8, 128), F32) + jnp.sum(acc[loss_row:loss_row + 1, :], axis=-1, keepdims=True)

    return pl.pallas_call(
        body, name="sum_stats", out_shape=[jax.ShapeDtypeStruct((r, D), F32), jax.ShapeDtypeStruct((8, 128), F32)],
    )(st_all)


def _adamw(w, g, m, v, name):
    R, C = w.shape
    tr = _pick(R, (256, 176, 128, 64, 32, 16, 8))
    c1 = 1.0 / (1.0 - ADAM_B1 ** ADAM_STEP)
    c2 = 1.0 / (1.0 - ADAM_B2 ** ADAM_STEP)

    def body(w_ref, g_ref, m_ref, v_ref, d_ref, nm_ref, nv_ref):
        gv = g_ref[...]
        nm = ADAM_B1 * m_ref[...] + (1.0 - ADAM_B1) * gv
        nv = ADAM_B2 * v_ref[...] + (1.0 - ADAM_B2) * (gv * gv)
        nm_ref[...] = nm
        nv_ref[...] = nv
        d_ref[...] = -ADAM_LR * ((nm * c1) / (jnp.sqrt(nv * c2) + ADAM_EPS) + ADAM_WD * w_ref[...])

    spec = pl.BlockSpec((tr, C), lambda i: (i, 0))
    return pl.pallas_call(
        body, name=name, grid=(R // tr,), in_specs=[spec] * 4, out_specs=[spec] * 3,
        out_shape=[jax.ShapeDtypeStruct((R, C), F32)] * 3, compiler_params=_params(("parallel",)),
    )(w, g, m, v)


def _round16(n):
    return -(-n // 16) * 16


def _pack_layout(D, in_cols, ff, W):
    parts = [("in", D * (in_cols // N_DEV) // D), ("fg", ff // N_DEV), ("fu", ff // N_DEV),
             ("sb", W * (D // N_DEV) // D), ("fx", W * (D // N_DEV) // D), ("o", D // N_DEV), ("fd", ff // N_DEV)]
    layout, off = {}, 0
    for nm, rows in parts:
        layout[nm] = (off, rows)
        off += _round16(rows)
    return layout, off


def _rows_of(a, D, rows):
    a = a.reshape(rows, D)
    return jnp.pad(a, ((0, _round16(rows) - rows), (0, 0)))


def _cols_to_dest(g, D):
    K, N = g.shape
    n = N // N_DEV
    return g.reshape(K, N_DEV, n).transpose(1, 0, 2).reshape(N_DEV, K * n // D, D)


def _cols_from_src(blocks, K, n):
    return blocks.reshape(N_DEV, K, n).transpose(1, 0, 2).reshape(K, N_DEV * n)


def _pad_rows16(a):
    rows = a.shape[1]
    return jnp.pad(a, ((0, 0), (0, _round16(rows) - rows), (0, 0)))


def kernel(x, c, w_ada, b_ada, w_in, b_gate, b_forget, w_sb_out, w_fox_out, w_o, ln1_g, ln1_b, w_ffn_gate, w_ffn_up, w_ffn_down, ln2_g, ln2_b, loss_target, m_w_ada, m_b_ada, m_w_in, m_b_gate, m_b_forget, m_w_sb_out, m_w_fox_out, m_w_o, m_ln1_g, m_ln1_b, m_w_ffn_gate, m_w_ffn_up, m_w_ffn_down, m_ln2_g, m_ln2_b, v_w_ada, v_b_ada, v_w_in, v_b_gate, v_b_forget, v_w_sb_out, v_w_fox_out, v_w_o, v_ln1_g, v_ln1_b, v_w_ffn_gate, v_w_ffn_up, v_w_ffn_down, v_ln2_g, v_ln2_b):
    S, D = x.shape[1], x.shape[2]
    W = w_sb_out.shape[1]
    n_heads = b_forget.shape[1]
    ff = w_ffn_down.shape[1] * N_DEV
    in_loc = w_in.shape[2]
    in_cols = in_loc * N_DEV
    ada_loc = w_ada.shape[2]
    n_cond = ada_loc * N_DEV // D
    assert w_ada.shape[0] == 1 and n_cond == 6 and in_cols == 6 * W + n_heads + 2 * D and n_heads <= 128
    me = 4 * lax.axis_index("x") + 2 * lax.axis_index("y") + lax.axis_index("c")

    c_all = _all_gather_small(c, "gather_c").reshape(N_DEV, D)
    c16 = jnp.pad(c_all, ((0, 16 - N_DEV), (0, 0)))
    b_cols = lax.dynamic_slice(b_ada, (0, me * ada_loc), (1, ada_loc))
    ada_cols = _mm([(c16, w_ada[0])], 'nn', F32, "ada_fwd", bias=b_cols, silu_a=True)[:N_DEV]
    ada_all = _all_gather_small(ada_cols, "gather_ada")
    ada_me = lax.dynamic_index_in_dim(ada_all, me, axis=1, keepdims=False)
    ada8 = jnp.pad(ada_me.reshape(n_cond, D), ((0, 8 - n_cond), (0, 0)))
    lnp8 = jnp.concatenate([ln1_g, ln1_b, ln2_g, ln2_b, jnp.zeros((4, D), F32)], axis=0)

    layout, R = _pack_layout(D, in_cols, ff, W)
    shards = dict(**{"in": w_in[0]}, fg=w_ffn_gate[0], fu=w_ffn_up[0], sb=w_sb_out[0], fx=w_fox_out[0], o=w_o[0],
                  fd=w_ffn_down[0])
    rows_fwd, rows_bwd = ("sb", "fx", "o"), ("fd", "o", "sb", "fx")
    rows_of = {nm: layout[nm][1] for nm in layout}

    def offsets(names):
        offs, off = {}, 0
        for nm in names:
            offs[nm] = off
            off += _round16(rows_of[nm])
        return offs

    def as_rows(names):
        return jnp.concatenate([_rows_of(shards[nm].astype(BF16), D, rows_of[nm]) for nm in names], axis=0)

    def whole_from(blocks):
        return blocks.transpose(1, 0, 2).reshape(blocks.shape[1], N_DEV * blocks.shape[2])

    def blocks_of(g):
        return g.reshape(g.shape[0], N_DEV, g.shape[1] // N_DEV).transpose(1, 0, 2)

    gather_src = [as_rows(rows_fwd),
                  jnp.concatenate([shards["fg"].astype(BF16), shards["fu"].astype(BF16)], axis=0), as_rows(("fd",))]
    gathered_in = _all_gather_weights(shards["in"].astype(BF16))

    w_in_full = whole_from(gathered_in)
    wqkv = w_in_full[:, :6 * W]
    wf = jnp.pad(w_in_full[:, 6 * W:6 * W + n_heads], ((0, 0), (0, 128 - n_heads)))
    wgs = w_in_full[:, 6 * W + n_heads:6 * W + n_heads + D]
    wgf = w_in_full[:, 6 * W + n_heads + D:]
    bf_pad = jnp.pad(b_forget, ((0, 0), (0, 128 - n_heads)))

    def later_weights(zone_rows, zone_gate_up, zone_down):
        rows, offs = _own_slot(zone_rows, gather_src[0]), offsets(rows_fwd)
        part = {nm: rows[:, offs[nm]:offs[nm] + rows_of[nm], :] for nm in rows_fwd}
        gate_up = _own_slot(zone_gate_up, gather_src[1])
        down = _own_slot(zone_down, gather_src[2])[:, :rows_of["fd"], :]
        return (_cols_from_src(part["sb"], W, D // N_DEV), _cols_from_src(part["fx"], W, D // N_DEV),
                part["o"].reshape(D, D), whole_from(gate_up[:, :D, :]), whole_from(gate_up[:, D:, :]),
                down.reshape(ff, D))

    def pack_early(g):
        dest = {"sb": _cols_to_dest(g["sb"], D), "fx": _cols_to_dest(g["fx"], D),
                "o": g["o"].reshape(N_DEV, D // N_DEV, D), "fd": g["fd"].reshape(N_DEV, ff // N_DEV, D)}
        rows = jnp.concatenate([_pad_rows16(dest[nm].astype(BF16)) for nm in rows_bwd], axis=1)
        gate_up = jnp.concatenate([blocks_of(g["fg"].astype(BF16)), blocks_of(g["fu"].astype(BF16))], axis=1)
        return [(rows, True), (gate_up, True)]

    def pack_last(g):
        g_in = jnp.concatenate(g["qkv"] + [g["f"][:, :n_heads], g["gs"], g["gf"]], axis=1)
        return blocks_of(g_in.astype(BF16)), True

    gx, wg, st, early, ((pack_in, _), land_in) = _local_step(
        x[0], loss_target[0], ada8, lnp8, b_gate[:, :D], b_gate[:, D:], bf_pad, wqkv, wf, wgs, wgf,
        [(gather_src[0], False), [(gather_src[1], False), (gather_src[2], False)]],
        later_weights, pack_early, pack_last)

    def summed(zone, sent, name):
        own = lax.dynamic_index_in_dim(sent, me, axis=0, keepdims=False)
        rows = zone.shape[1]
        block = max(t for t in range(16, 705, 16) if rows % t == 0)
        return _sum_slots(_own_slot(zone, own), name, block)

    ((sent_rows, _), zone_rows), ((sent_gate_up, _), zone_gate_up) = early
    gate_up = summed(zone_gate_up, sent_gate_up, "sum_grads_gate_up")
    gsum = {"in": summed(land_in, pack_in, "sum_grads_in"), "fg": gate_up[:D], "fu": gate_up[D:]}
    total, offs = summed(zone_rows, sent_rows, "sum_grads_rows"), offsets(rows_bwd)
    for nm in rows_bwd:
        gsum[nm] = total[offs[nm]:offs[nm] + rows_of[nm]]

    def gshard(nm, shape):
        return gsum[nm].reshape(shape)

    zrow = jnp.zeros((1, D), F32)
    gb_f_row = jnp.pad(st["gb_f"][0:1], ((0, 0), (0, D - 128)))
    stats16 = jnp.concatenate([
        st["first"][1:2], st["first"][0:1], st["mid"][4:5], st["mid"][1:2], st["mid"][0:1], st["loss"][3:4],
        st["mid"][2:3], st["mid"][3:4], st["loss"][1:2], st["loss"][2:3], st["gb_sb"][0:1], st["gb_fx"][0:1],
        st["loss"][0:1], gb_f_row, zrow, zrow], axis=0)
    st_all = _all_gather_small(stats16, "gather_stats")
    st_sum, loss_blk = _sum_stats(st_all, 12)
    loss = loss_blk[0, 0]

    d_ada_all = st_all[:, :n_cond, :].reshape(N_DEV, n_cond * D)
    d_cols = lax.dynamic_slice(d_ada_all, (0, me * ada_loc), (N_DEV, ada_loc))
    d16 = jnp.pad(d_cols, ((0, 16 - N_DEV), (0, 0)))
    g_w_ada = _mm([(c16, d16)], 'tn', F32, "ada_wgrad", silu_a=True)

    small_w = jnp.concatenate([b_ada.reshape(n_cond, D), ln1_g, ln1_b, ln2_g, ln2_b, b_gate.reshape(2, D), zrow,
                               jnp.pad(b_forget, ((0, 0), (0, D - n_heads))), zrow, zrow], axis=0)
    small_m = jnp.concatenate([m_b_ada.reshape(n_cond, D), m_ln1_g, m_ln1_b, m_ln2_g, m_ln2_b, m_b_gate.reshape(2, D),
                               zrow, jnp.pad(m_b_forget, ((0, 0), (0, D - n_heads))), zrow, zrow], axis=0)
    small_v = jnp.concatenate([v_b_ada.reshape(n_cond, D), v_ln1_g, v_ln1_b, v_ln2_g, v_ln2_b, v_b_gate.reshape(2, D),
                               zrow, jnp.pad(v_b_forget, ((0, 0), (0, D - n_heads))), zrow, zrow], axis=0)
    sm = _adamw(small_w, st_sum, small_m, small_v, "adamw_small")

    def small(a, nm):
        if nm == "b_ada":
            return a[0:n_cond].reshape(1, n_cond * D)
        if nm == "b_gate":
            return a[10:12].reshape(1, 2 * D)
        if nm == "b_forget":
            return a[13:14, :n_heads]
        row = {"ln1_g": 6, "ln1_b": 7, "ln2_g": 8, "ln2_b": 9}[nm]
        return a[row:row + 1]

    big = {
        "w_ada": (w_ada[0], g_w_ada, m_w_ada[0], v_w_ada[0]),
        "w_in": (w_in[0], gshard("in", w_in.shape[1:]), m_w_in[0], v_w_in[0]),
        "w_sb_out": (w_sb_out[0], gshard("sb", w_sb_out.shape[1:]), m_w_sb_out[0], v_w_sb_out[0]),
        "w_fox_out": (w_fox_out[0], gshard("fx", w_fox_out.shape[1:]), m_w_fox_out[0], v_w_fox_out[0]),
        "w_o": (w_o[0], gshard("o", w_o.shape[1:]), m_w_o[0], v_w_o[0]),
        "w_ffn_gate": (w_ffn_gate[0], gshard("fg", w_ffn_gate.shape[1:]), m_w_ffn_gate[0], v_w_ffn_gate[0]),
        "w_ffn_up": (w_ffn_up[0], gshard("fu", w_ffn_up.shape[1:]), m_w_ffn_up[0], v_w_ffn_up[0]),
        "w_ffn_down": (w_ffn_down[0], gshard("fd", w_ffn_down.shape[1:]), m_w_ffn_down[0], v_w_ffn_down[0]),
    }
    order = ["w_ada", "b_ada", "w_in", "b_gate", "b_forget", "w_sb_out", "w_fox_out", "w_o", "ln1_g", "ln1_b",
             "w_ffn_gate", "w_ffn_up", "w_ffn_down", "ln2_g", "ln2_b"]
    grads, deltas, new_ms, new_vs = [], [], [], []
    for nm in order:
        if nm in big:
            w, g, m, v = big[nm]
            d, nm_, nv_ = _adamw(w, g, m, v, "adamw_" + nm)
            grads.append(g[None])
            deltas.append(d[None])
            new_ms.append(nm_[None])
            new_vs.append(nv_[None])
        else:
            grads.append(small(st_sum, nm))
            deltas.append(small(sm[0], nm))
            new_ms.append(small(sm[1], nm))
            new_vs.append(small(sm[2], nm))
    return (loss, gx[None], *grads, *deltas, *new_ms, *new_vs)
```

```python
import jax
import jax.numpy as jnp
from jax import lax
from jax.experimental import pallas as pl
from jax.experimental.pallas import tpu as pltpu

F32 = jnp.float32
BF16 = jnp.bfloat16

HEAD_DIM = 64
PAIR = 2 * HEAD_DIM
LN_EPS = 1e-5
ALPHA = 2.0 ** 0.25
ADAM_LR, ADAM_B1, ADAM_B2, ADAM_EPS, ADAM_WD, ADAM_STEP = 0.001, 0.9, 0.999, 1e-08, 0.01, 10
N_DEV = 8
VMEM_LIMIT = 56 * 1024 * 1024
MESH = pl.DeviceIdType.MESH


def _dot(a, b, ca=1, cb=0):
    return lax.dot_general(a, b, (((ca,), (cb,)), ((), ())), preferred_element_type=F32)


def _pick(n, cands):
    for t in cands:
        if n % t == 0:
            return t
    return n


def _params(sem):
    return pltpu.CompilerParams(dimension_semantics=sem, vmem_limit_bytes=VMEM_LIMIT)


MM_BLOCK_BYTES = 40 * 1024 * 1024
LANES = 128


def _divisors(n, cap):
    ds = [d for d in range(LANES, min(n, cap) + 1, LANES) if n % d == 0]
    return sorted(ds, reverse=True) or [n]


def _mm_tiles(M, N, a_row_bytes, b_row_bytes, out_itemsize):
    best = None
    for tm in _divisors(M, 1024):
        for tn in _divisors(N, 2048):
            need = 2 * (tm * a_row_bytes + tn * b_row_bytes + tm * tn * out_itemsize) + tm * tn * 4
            if need <= MM_BLOCK_BYTES and (best is None or (tm * tn, tm) > (best[0] * best[1], best[0])):
                best = (tm, tn)
    assert best is not None, (M, N, a_row_bytes, b_row_bytes)
    return best


def _mm(pairs, mode, out_dtype, name, bias=None, act=None, silu_a=False, exchange=None):
    norm = []
    for p in pairs:
        a, b = p[0], p[1]
        kdim_a = a.shape[0] if mode == 'tn' else a.shape[1]
        K, ka, kb = (p[2], p[3], p[4]) if len(p) > 2 else (kdim_a, 0, 0)
        norm.append((a, b, K, ka, kb))
    a0, b0 = norm[0][0], norm[0][1]
    M = a0.shape[1] if mode == 'tn' else a0.shape[0]
    N = b0.shape[0] if mode == 'nt' else b0.shape[1]
    tm, tn = _mm_tiles(M, N, sum(K * a.dtype.itemsize for a, _, K, _, _ in norm),
                       sum(K * b.dtype.itemsize for _, b, K, _, _ in norm), jnp.dtype(out_dtype).itemsize)
    n_pairs = len(norm)

    in_specs, args = [], []
    for a, b, K, ka, kb in norm:
        if mode == 'tn':
            in_specs.append(pl.BlockSpec((K, tm), lambda i, j, ka=ka: (ka, i)))
        else:
            in_specs.append(pl.BlockSpec((tm, K), lambda i, j, ka=ka: (i, ka)))
        if mode == 'nt':
            in_specs.append(pl.BlockSpec((tn, K), lambda i, j, kb=kb: (j, kb)))
        else:
            in_specs.append(pl.BlockSpec((K, tn), lambda i, j, kb=kb: (kb, j)))
        args += [a, b]
    if bias is not None:
        in_specs.append(pl.BlockSpec((1, tn), lambda i, j: (0, j)))
        args.append(bias)

    ca = 0 if mode == 'tn' else 1
    cb = 1 if mode == 'nt' else 0

    def body(*refs):
        o_ref = refs[-1]
        acc = None
        for p in range(n_pairs):
            av = refs[2 * p][...]
            if silu_a:
                av = av / (1.0 + jnp.exp(-av))
            d = _dot(av.astype(BF16), refs[2 * p + 1][...].astype(BF16), ca, cb)
            acc = d if acc is None else acc + d
        if bias is not None:
            acc = acc + refs[2 * n_pairs][...]
        if act == 'sigmoid':
            acc = 1.0 / (1.0 + jnp.exp(-acc))
        o_ref[...] = acc.astype(out_dtype)

    out_spec = pl.BlockSpec((tm, tn), lambda i, j: (i, j))
    out_shape = jax.ShapeDtypeStruct((M, N), out_dtype)
    if exchange is not None:
        return _call_carrying(body, exchange, name, (M // tm, N // tn), in_specs, [out_spec], [out_shape], [], args)
    return pl.pallas_call(
        body, name=name, grid=(M // tm, N // tn), in_specs=in_specs, out_specs=out_spec, out_shape=out_shape,
        compiler_params=_params(("parallel", "parallel")),
    )(*args)


def _rows_call(body, name, row_ins, vec_ins, row_outs, acc_outs, ts):
    S = row_ins[0].shape[0]
    in_specs = [pl.BlockSpec((ts, a.shape[1]), lambda i: (i, 0)) for a in row_ins]
    in_specs += [pl.BlockSpec(a.shape, lambda i: (0, 0)) for a in vec_ins]
    out_specs = [pl.BlockSpec((ts, c), lambda i: (i, 0)) for c, _ in row_outs]
    out_specs += [pl.BlockSpec(s, lambda i: (0, 0)) for s in acc_outs]
    out_shape = [jax.ShapeDtypeStruct((S, c), dt) for c, dt in row_outs]
    out_shape += [jax.ShapeDtypeStruct(s, F32) for s in acc_outs]
    return pl.pallas_call(
        body, name=name, grid=(S // ts,), in_specs=in_specs, out_specs=out_specs, out_shape=out_shape,
        compiler_params=_params(("arbitrary",)),
    )(*row_ins, *vec_ins)


def _ln_stats(v):
    mu = jnp.mean(v, axis=-1, keepdims=True)
    d = v - mu
    var = jnp.mean(d * d, axis=-1, keepdims=True)
    rstd = lax.rsqrt(var + LN_EPS)
    return d * rstd, rstd


def _ln_bwd(dxhat, xhat, rstd):
    m1 = jnp.mean(dxhat, axis=-1, keepdims=True)
    m2 = jnp.mean(dxhat * xhat, axis=-1, keepdims=True)
    return rstd * (dxhat - m1 - xhat * m2)


def _colsum(v):
    return jnp.sum(v, axis=0, keepdims=True)


def _ln_mod(x, ada8, ts):
    D = x.shape[1]

    def body(x_ref, v_ref, u_ref):
        xhat, _ = _ln_stats(x_ref[...])
        u_ref[...] = (xhat * (1.0 + v_ref[1:2, :]) + v_ref[0:1, :]).astype(BF16)

    return _rows_call(body, "ln_mod", [x], [ada8], [(D, BF16)], [], ts)[0]


def _gate_mix_out(g_sb, g_fx, y_sb, y_fx, wo, ts):
    D = y_sb.shape[1]

    def body(gs, gf, ys, yf, w_ref, mi_ref, mix_ref):
        mi = (gs[...] * ys[...] + gf[...] * yf[...]).astype(BF16)
        mi_ref[...] = mi
        mix_ref[...] = _dot(mi, w_ref[...])

    return _rows_call(body, "gate_mix_out", [g_sb, g_fx, y_sb, y_fx], [wo], [(D, BF16), (D, F32)], [], ts)


def _post_attn(x, mix, ada8, lnp8, ts):
    D = x.shape[1]

    def body(x_ref, mix_ref, v_ref, p_ref, x1_ref, u2_ref):
        r1 = ALPHA * x_ref[...] + v_ref[2:3, :] * mix_ref[...]
        xhat, _ = _ln_stats(r1)
        x1 = xhat * p_ref[0:1, :] + p_ref[1:2, :]
        x1_ref[...] = x1
        xh1, _ = _ln_stats(x1)
        u2_ref[...] = (xh1 * (1.0 + v_ref[4:5, :]) + v_ref[3:4, :]).astype(BF16)

    return _rows_call(body, "post_attn", [x, mix], [ada8, lnp8], [(D, F32), (D, BF16)], [], ts)


def _loss_head(x1, hin, wfd, target, ada8, lnp8, ts):
    D = x1.shape[1]

    def body(x1_ref, hin_ref, t_ref, v_ref, p_ref, w_ref, dr2_ref, dh_ref, st_ref):
        @pl.when(pl.program_id(0) == 0)
        def _():
            st_ref[...] = jnp.zeros_like(st_ref)

        hv = _dot(hin_ref[...], w_ref[...])
        g2 = v_ref[5:6, :]
        r2 = ALPHA * x1_ref[...] + g2 * hv
        xhat, rstd = _ln_stats(r2)
        y = xhat * p_ref[2:3, :] + p_ref[3:4, :]
        err = y - t_ref[...]
        dy = err * (1.0 / D)
        dr2 = _ln_bwd(dy * p_ref[2:3, :], xhat, rstd)
        dr2_ref[...] = dr2
        dh_ref[...] = (dr2 * g2).astype(BF16)
        st_ref[0:1, :] += _colsum(err * err) * (0.5 / D)
        st_ref[1:2, :] += _colsum(dy * xhat)
        st_ref[2:3, :] += _colsum(dy)
        st_ref[3:4, :] += _colsum(dr2 * hv)

    return _rows_call(body, "loss_head", [x1, hin, target], [ada8, lnp8, wfd], [(D, F32), (D, BF16)], [(8, D)], ts)


def _mid_bwd(da, db, wfg, wfu, x1, dr2, mix, x, ada8, lnp8, ts):
    D = x.shape[1]

    def body(da_ref, db_ref, x1_ref, dr2_ref, mix_ref, x_ref, v_ref, p_ref, wg_ref, wu_ref, dr1_ref, dmix_ref, st_ref):
        @pl.when(pl.program_id(0) == 0)
        def _():
            st_ref[...] = jnp.zeros_like(st_ref)

        du2v = _dot(da_ref[...], wg_ref[...], 1, 1) + _dot(db_ref[...], wu_ref[...], 1, 1)
        xh1, rstd1 = _ln_stats(x1_ref[...])
        dx1 = ALPHA * dr2_ref[...] + _ln_bwd(du2v * (1.0 + v_ref[4:5, :]), xh1, rstd1)
        mixv = mix_ref[...]
        g1 = v_ref[2:3, :]
        r1 = ALPHA * x_ref[...] + g1 * mixv
        xhr, rstdr = _ln_stats(r1)
        dr1 = _ln_bwd(dx1 * p_ref[0:1, :], xhr, rstdr)
        dr1_ref[...] = dr1
        dmix_ref[...] = (dr1 * g1).astype(BF16)
        st_ref[0:1, :] += _colsum(du2v * xh1)
        st_ref[1:2, :] += _colsum(du2v)
        st_ref[2:3, :] += _colsum(dx1 * xhr)
        st_ref[3:4, :] += _colsum(dx1)
        st_ref[4:5, :] += _colsum(dr1 * mixv)

    return _rows_call(body, "mid_bwd", [da, db, x1, dr2, mix, x], [ada8, lnp8, wfg, wfu], [(D, F32), (D, BF16)],
                      [(8, D)], ts)


def _first_bwd(pairs, x, dr1, ada8, exchange):
    S, D = x.shape
    norm = [(p[0], p[1]) + ((p[2], p[3], p[4]) if len(p) > 2 else (p[0].shape[1], 0, 0)) for p in pairs]
    tm = _pick(S, (256, 128))
    in_specs, args = [], []
    for a, b, K, ka, kb in norm:
        in_specs += [pl.BlockSpec((tm, K), lambda i, j, ka=ka: (i, ka)), pl.BlockSpec((D, K), lambda i, j, kb=kb: (0, kb))]
        args += [a, b]
    rows = pl.BlockSpec((tm, D), lambda i, j: (i, 0))
    sums = pl.BlockSpec((8, D), lambda i, j: (0, 0))
    n_pairs = len(norm)

    def body(*refs):
        x_ref, dr1_ref, v_ref, gx_ref, st_ref = refs[2 * n_pairs:]

        @pl.when(pl.program_id(0) == 0)
        def _():
            st_ref[...] = jnp.zeros_like(st_ref)

        du1 = None
        for p in range(n_pairs):
            d = _dot(refs[2 * p][...].astype(BF16), refs[2 * p + 1][...].astype(BF16), 1, 1)
            du1 = d if du1 is None else du1 + d
        xh0, rstd0 = _ln_stats(x_ref[...])
        gx_ref[...] = ALPHA * dr1_ref[...] + _ln_bwd(du1 * (1.0 + v_ref[1:2, :]), xh0, rstd0)
        st_ref[0:1, :] += _colsum(du1 * xh0)
        st_ref[1:2, :] += _colsum(du1)

    return _call_carrying(
        body, exchange, "first_bwd", (S // tm, 1), in_specs + [rows, rows, sums], [rows, sums],
        [jax.ShapeDtypeStruct((S, D), F32), jax.ShapeDtypeStruct((8, D), F32)], [], (*args, x, dr1, ada8),
        semantics=("arbitrary", "arbitrary"))


def _split3(v):
    hi = v.astype(BF16)
    r = v - hi.astype(F32)
    mid = r.astype(BF16)
    lo = (r - mid.astype(F32)).astype(BF16)
    return hi, mid, lo


def _fgate_fwd(f, bf_pad, tb):
    S = f.shape[0]

    def body(f_ref, b_ref, fc_ref, carry):
        @pl.when(pl.program_id(0) == 0)
        def _():
            carry[...] = jnp.zeros_like(carry)

        z = f_ref[...] + b_ref[...]
        ls = jnp.minimum(z, 0.0) - jnp.log(1.0 + jnp.exp(-jnp.abs(z)))
        r = lax.broadcasted_iota(jnp.int32, (tb, tb), 0)
        c = lax.broadcasted_iota(jnp.int32, (tb, tb), 1)
        tri = (c <= r).astype(BF16)
        hi, mid, lo = _split3(ls)
        cs = _dot(tri, hi) + _dot(tri, mid) + _dot(tri, lo) + carry[...]
        fc_ref[...] = cs
        carry[...] = cs[tb - 1:tb, :]

    return pl.pallas_call(
        body, name="fgate_fwd", grid=(S // tb,),
        in_specs=[pl.BlockSpec((tb, 128), lambda i: (i, 0)), pl.BlockSpec((1, 128), lambda i: (0, 0))],
        out_specs=pl.BlockSpec((tb, 128), lambda i: (i, 0)),
        out_shape=jax.ShapeDtypeStruct((S, 128), F32),
        scratch_shapes=[pltpu.VMEM((1, 128), F32)],
        compiler_params=_params(("arbitrary",)),
    )(f, bf_pad)


def _fgate_bwd(dfc, f, bf_pad, tb):
    S = f.shape[0]
    nb = S // tb

    def body(d_ref, f_ref, b_ref, df_ref, gb_ref, carry):
        @pl.when(pl.program_id(0) == 0)
        def _():
            carry[...] = jnp.zeros_like(carry)
            gb_ref[...] = jnp.zeros_like(gb_ref)

        r = lax.broadcasted_iota(jnp.int32, (tb, tb), 0)
        c = lax.broadcasted_iota(jnp.int32, (tb, tb), 1)
        tri = (c >= r).astype(BF16)
        hi, mid, lo = _split3(d_ref[...])
        rs = _dot(tri, hi) + _dot(tri, mid) + _dot(tri, lo) + carry[...]
        carry[...] = rs[0:1, :]
        z = f_ref[...] + b_ref[...]
        df = rs * (1.0 / (1.0 + jnp.exp(z)))
        df_ref[...] = df
        gb_ref[0:1, :] += _colsum(df)

    return pl.pallas_call(
        body, name="fgate_bwd", grid=(nb,),
        in_specs=[pl.BlockSpec((tb, 128), lambda i: (nb - 1 - i, 0)),
                  pl.BlockSpec((tb, 128), lambda i: (nb - 1 - i, 0)),
                  pl.BlockSpec((1, 128), lambda i: (0, 0))],
        out_specs=[pl.BlockSpec((tb, 128), lambda i: (nb - 1 - i, 0)), pl.BlockSpec((8, 128), lambda i: (0, 0))],
        out_shape=[jax.ShapeDtypeStruct((S, 128), F32), jax.ShapeDtypeStruct((8, 128), F32)],
        scratch_shapes=[pltpu.VMEM((1, 128), F32)],
        compiler_params=_params(("arbitrary",)),
    )(dfc, f, bf_pad)


def _split2(v):
    hi = v.astype(BF16)
    lo = (v - hi.astype(F32)).astype(BF16)
    return hi, lo


def _head_masks():
    lane = lax.broadcasted_iota(jnp.int32, (1, PAIR), 1)
    m0 = lane < HEAD_DIM
    return m0, jnp.logical_not(m0)


def _sel(mask, v):
    return jnp.where(mask, v, jnp.zeros_like(v))


def _softplus(z):
    return jnp.maximum(z, 0.0) + jnp.log(1.0 + jnp.exp(-jnp.abs(z)))


def _qkv_specs(S, tq, n_pairs, base):
    return [pl.BlockSpec((tq, PAIR), lambda p, i: (i, base + p)),
            pl.BlockSpec((S, PAIR), lambda p, i: (0, base + n_pairs + p)),
            pl.BlockSpec((S, PAIR), lambda p, i: (0, base + 2 * n_pairs + p))]


NEG = -1e30


def _fox_specs(S, tq, n_pairs, base):
    return _qkv_specs(S, tq, n_pairs, base) + [
        pl.BlockSpec((tq, PAIR), lambda p, i: (i, p)),
        pl.BlockSpec((1, S // tq, 8, tq), lambda p, i: (p, 0, 0, 0))]


RC = 32
VANISH = -104.0


def _chunks(n_rows, fn):
    for ci in range(n_rows // RC):
        fn(ci * RC)


def _wide(v, tk):
    return v if tk == 128 else jnp.tile(v, (1, tk // 128))


def _rep(col):
    return jnp.broadcast_to(col, (col.shape[0], 128))


def _per_head(blk, masks):
    sw = pltpu.roll(blk, HEAD_DIM, axis=1)
    return jnp.where(masks[0], blk, sw), jnp.where(masks[0], sw, blk)


def _fill_masked(dst_ref, src_ref, masks, mul=None, ones_lane=None):
    v = src_ref[...]
    if mul is not None:
        v = v * mul
    lane = lax.broadcasted_iota(jnp.int32, (1, PAIR), 1)
    for h in range(2):
        m = _sel(masks[h], v)
        if ones_lane is not None:
            m = jnp.where(lane == ones_lane[h], jnp.ones_like(m), m)
        dst_ref[h] = m


def _tri(tk, cmp):
    r = lax.broadcasted_iota(jnp.int32, (tk, tk), 0)
    c = lax.broadcasted_iota(jnp.int32, (tk, tk), 1)
    return cmp(r, c).astype(BF16)


def _diag_mask(r0, tk, strict):
    row = r0 + lax.broadcasted_iota(jnp.int32, (RC, tk), 0)
    col = lax.broadcasted_iota(jnp.int32, (RC, tk), 1)
    return (col < row) if strict else (col <= row)


def _peer_copies(src_ref, land_ref, send_sems, recv_sems, scatter, receive_side):
    x, y, c = lax.axis_index("x"), lax.axis_index("y"), lax.axis_index("c")
    me = 4 * x + 2 * y + c
    copies = []
    for k in range(1, N_DEV):
        px, py, pc = (1 - x if k & 4 else x), (1 - y if k & 2 else y), (1 - c if k & 1 else c)
        slot = 4 * px + 2 * py + pc
        copies.append(pltpu.make_async_remote_copy(
            src_ref=src_ref.at[slot] if scatter else src_ref,
            dst_ref=land_ref.at[slot] if receive_side else land_ref.at[me],
            send_sem=send_sems.at[k - 1], recv_sem=recv_sems.at[k - 1], device_id=(px, py, pc), device_id_type=MESH))
    return copies


def _call_carrying(body, exchange, name, grid, in_specs, out_specs, out_shape, scratch_shapes, args,
                   semantics=("parallel", "arbitrary")):
    if exchange is None:
        return pl.pallas_call(body, name=name, grid=grid, in_specs=in_specs, out_specs=out_specs, out_shape=out_shape,
                              scratch_shapes=scratch_shapes, compiler_params=_params(semantics))(*args)
    exchanges = [exchange] if isinstance(exchange, tuple) else list(exchange)
    n_in, n_out, n_ex = len(in_specs), len(out_specs), len(exchanges)

    def carrying(*refs):
        srcs = refs[n_in:n_in + n_ex]
        lands = refs[n_in + n_ex + n_out:n_in + 2 * n_ex + n_out]
        sems = refs[len(refs) - 2 * n_ex:]

        def copies(receive_side):
            return [cp for e, (_, scatter) in enumerate(exchanges)
                    for cp in _peer_copies(srcs[e], lands[e], sems[2 * e], sems[2 * e + 1], scatter, receive_side)]

        first = jnp.logical_and(pl.program_id(0) == 0, pl.program_id(1) == 0)
        last = jnp.logical_and(pl.program_id(0) == grid[0] - 1, pl.program_id(1) == grid[1] - 1)

        @pl.when(first)
        def _():
            for cp in copies(False):
                cp.start()

        body(*refs[:n_in], *refs[n_in + n_ex:n_in + n_ex + n_out], *refs[n_in + 2 * n_ex + n_out:len(refs) - 2 * n_ex])

        @pl.when(last)
        def _():
            for cp in copies(True):
                cp.wait_send()
                cp.wait_recv()

    any_space = pl.BlockSpec(memory_space=pl.ANY)
    lands = [jax.ShapeDtypeStruct((N_DEV,) + src.shape[-2:], src.dtype) for src, _ in exchanges]
    return pl.pallas_call(
        carrying, name=name, grid=grid, in_specs=list(in_specs) + [any_space] * n_ex,
        out_specs=list(out_specs) + [any_space] * n_ex, out_shape=list(out_shape) + lands,
        scratch_shapes=list(scratch_shapes) + [pltpu.SemaphoreType.DMA((N_DEV - 1,))] * (2 * n_ex),
        compiler_params=_params(("arbitrary", "arbitrary")))(*args, *[src for src, _ in exchanges])


def _staggered(bodies):
    active, waiting = [], list(bodies)
    while waiting or active:
        if waiting:
            active.append(waiting.pop(0))
        for g in list(active):
            try:
                next(g)
            except StopIteration:
                active.remove(g)


def _streams(tile, j, diag, slot):
    return [tile(j, diag, slot, (0, 1))]


def _tiles(i, tile):
    def step(jj, carry):
        _staggered(_streams(tile, 2 * jj, False, 0) + _streams(tile, 2 * jj + 1, False, 1))
        return carry
    lax.fori_loop(0, (i - 1) // 2, step, 0)

    @pl.when(jnp.logical_and(i >= 1, (i - 1) % 2 == 1))
    def _():
        _staggered(_streams(tile, i - 2, False, 0))

    @pl.when(i >= 1)
    def _():
        _staggered(_streams(tile, i - 1, False, 0) + _streams(tile, i, True, 1))

    @pl.when(i == 0)
    def _():
        _staggered(_streams(tile, i, True, 1))


def _tiles_reversed(i, tile, keep_going):
    @pl.when(i == 0)
    def _():
        _staggered(_streams(tile, i, True, 0))

    @pl.when(i >= 1)
    def _():
        _staggered(_streams(tile, i, True, 0) + _streams(tile, i - 1, False, 1))

    pairs = (i - 1) // 2

    def cond(carry):
        jj, go = carry
        return jnp.logical_and(jj < pairs, go)

    def step(carry):
        jj, _ = carry
        _staggered(_streams(tile, i - 2 - 2 * jj, False, 0) + _streams(tile, i - 3 - 2 * jj, False, 1))
        return jj + 1, keep_going(jnp.maximum(i - 4 - 2 * jj, 0))

    jj, go = lax.while_loop(cond, step, (jnp.int32(0), keep_going(jnp.maximum(i - 2, 0))))

    @pl.when(jnp.logical_and(jnp.logical_and(i >= 1, (i - 1) % 2 == 1), jnp.logical_and(jj == pairs, go)))
    def _():
        _staggered(_streams(tile, 0, False, 0))


def _sb_fwd(qkv, n_pairs, base, tq, exchange=None):
    S = qkv.shape[0]
    tk = tq
    scale = HEAD_DIM ** -0.5

    def body(q_ref, k_ref, v_ref, o_ref, t_ref, z_ref, hi_ref, suf_ref, p_ref, r_ref, acc_ref, vm_ref):
        i = pl.program_id(1)
        masks = _head_masks()

        @pl.when(i == 0)
        def _():
            _fill_masked(vm_ref, v_ref, masks)

        q2 = q_ref[...] * scale
        qm = [_sel(m, q2) for m in masks]
        incl = _tri(tk, lambda r, c: r >= c)
        r_ref[...] = jnp.zeros_like(r_ref)
        acc_ref[...] = jnp.zeros_like(acc_ref)

        def tile(j, diag, slot, heads):
            off = pl.multiple_of(j * tk, tk)
            k2 = k_ref[pl.ds(off, tk), :]
            v2 = v_ref[pl.ds(off, tk), :]
            for h in heads:
                z_ref[2 * slot + h] = _dot(qm[h], k2, 1, 1)
            yield
            for h in heads:
                def split(r0, h=h):
                    rows = pl.ds(r0, RC)
                    lg = -_softplus(z_ref[2 * slot + h, rows, :])
                    if diag:
                        lg = jnp.where(_diag_mask(r0, tk, True), lg, 0.0)
                    hi_ref[2 * slot + h, rows, :] = lg.astype(BF16)
                _chunks(tq, split)
            yield
            for h in heads:
                suf_ref[2 * slot + h] = _dot(hi_ref[2 * slot + h], incl)
            yield
            for h in heads:
                def weights(r0, h=h):
                    rows = pl.ds(r0, RC)
                    a = jnp.exp(z_ref[2 * slot + h, rows, :] + suf_ref[2 * slot + h, rows, :] + _wide(r_ref[h, rows, :], tk))
                    if diag:
                        a = jnp.where(_diag_mask(r0, tk, True), a, 0.0)
                    p_ref[2 * slot + h, rows, :] = a.astype(BF16)
                _chunks(tq, weights)
            yield
            keys = pl.ds(off, tk)
            for h in heads:
                acc_ref[...] += _dot(p_ref[2 * slot + h], vm_ref[h, keys, :])
            for h in heads:
                r_ref[h] += _rep(suf_ref[2 * slot + h, :, 0:1])

        _tiles_reversed(i, tile, lambda nearest: jnp.max(r_ref[...]) >= VANISH)
        o_ref[...] = acc_ref[...].astype(BF16)
        t_ref[...] = acc_ref[...]

    W = n_pairs * PAIR
    return _call_carrying(
        body, exchange, "sb_fwd", (n_pairs, S // tq), _qkv_specs(S, tq, n_pairs, base),
        [pl.BlockSpec((tq, PAIR), lambda p, i: (i, p)), pl.BlockSpec((tq, PAIR), lambda p, i: (i, p))],
        [jax.ShapeDtypeStruct((S, W), BF16), jax.ShapeDtypeStruct((S, W), F32)],
        [pltpu.VMEM((4, tq, tk), F32), pltpu.VMEM((4, tq, tk), BF16),
         pltpu.VMEM((4, tq, tk), F32), pltpu.VMEM((4, tq, tk), BF16), pltpu.VMEM((2, tq, 128), F32),
         pltpu.VMEM((tq, PAIR), F32), pltpu.VMEM((2, S, PAIR), BF16)],
        (qkv, qkv, qkv))


def _sb_bwd(qkv, o32, do, n_pairs, base, tq, exchange=None):
    S = qkv.shape[0]
    tk = tq
    nq = S // tq
    scale = HEAD_DIM ** -0.5

    def body(q_ref, k_ref, v_ref, o_ref, do_ref, dq_ref, dk_ref, dv_ref,
             z_ref, g_ref, omb_ref, cum_ref, hi_ref, lo_ref, a_ref, dz_ref,
             r_ref, cg_ref, dl_ref, dq_acc, dk_acc, dv_acc, ks_ref):
        i = pl.program_id(1)
        masks = _head_masks()

        @pl.when(i == 0)
        def _():
            dk_acc[...] = jnp.zeros_like(dk_acc)
            dv_acc[...] = jnp.zeros_like(dv_acc)
            _fill_masked(ks_ref, k_ref, masks, mul=scale)

        q2 = q_ref[...] * scale
        do2 = do_ref[...]
        qm = [_sel(m, q2) for m in masks]
        dom = [_sel(m, do2) for m in masks]
        prod = do2.astype(F32) * o_ref[...]
        for h in range(2):
            dl_ref[h] = _rep(jnp.sum(jnp.where(masks[h], prod, 0.0), axis=-1, keepdims=True))
        suffix = _tri(tk, lambda r, c: r >= c)
        r_ref[...] = jnp.zeros_like(r_ref)
        cg_ref[...] = jnp.zeros_like(cg_ref)
        dq_acc[...] = jnp.zeros_like(dq_acc)

        def tile(j, diag, slot, heads):
            off = pl.multiple_of(j * tk, tk)
            k2 = k_ref[pl.ds(off, tk), :]
            v2 = v_ref[pl.ds(off, tk), :]
            for h in heads:
                z_ref[2 * slot + h] = _dot(qm[h], k2, 1, 1)
                g_ref[2 * slot + h] = _dot(dom[h], v2, 1, 1)
            yield
            for h in heads:
                def split(r0, h=h):
                    rows = pl.ds(r0, RC)
                    sp = _softplus(z_ref[2 * slot + h, rows, :])
                    omb_ref[2 * slot + h, rows, :] = jnp.exp(-sp)
                    lg = -sp
                    if diag:
                        lg = jnp.where(_diag_mask(r0, tk, True), lg, 0.0)
                    hi_ref[2 * slot + h, rows, :] = lg.astype(BF16)
                _chunks(tq, split)
            yield
            for h in heads:
                cum_ref[2 * slot + h] = _dot(hi_ref[2 * slot + h], suffix)
            yield
            for h in heads:
                def weights(r0, h=h):
                    rows = pl.ds(r0, RC)
                    a = jnp.exp(z_ref[2 * slot + h, rows, :] + cum_ref[2 * slot + h, rows, :] + _wide(r_ref[h, rows, :], tk))
                    if diag:
                        a = jnp.where(_diag_mask(r0, tk, True), a, 0.0)
                    ab = a.astype(BF16)
                    g = g_ref[2 * slot + h, rows, :] * ab.astype(F32)
                    g_ref[2 * slot + h, rows, :] = g
                    a_ref[2 * slot + h, rows, :] = ab
                    hi, lo = _split2(g)
                    hi_ref[2 * slot + h, rows, :] = hi
                    lo_ref[2 * slot + h, rows, :] = lo
                _chunks(tq, weights)
            for h in heads:
                r_ref[h] += _rep(cum_ref[2 * slot + h, :, 0:1])
            yield
            for h in heads:
                cum_ref[2 * slot + h] = _dot(hi_ref[2 * slot + h], suffix) + _dot(lo_ref[2 * slot + h], suffix)
            yield
            for h in heads:
                def dscore(r0, h=h):
                    rows = pl.ds(r0, RC)
                    g = g_ref[2 * slot + h, rows, :]
                    from_here = cum_ref[2 * slot + h, rows, :] + _wide(cg_ref[h, rows, :], tk)
                    before = _wide(dl_ref[h, rows, :], tk) - from_here
                    omb = omb_ref[2 * slot + h, rows, :]
                    dz = g * omb - (1.0 - omb) * before
                    if diag:
                        dz = jnp.where(_diag_mask(r0, tk, True), dz, 0.0)
                    dz_ref[2 * slot + h, rows, :] = dz.astype(BF16)
                _chunks(tq, dscore)
            for h in heads:
                cg_ref[h] += _rep(cum_ref[2 * slot + h, :, 0:1])
            yield
            keys = pl.ds(off, tk)
            for h in heads:
                dq_acc[...] += _dot(dz_ref[2 * slot + h], ks_ref[h, keys, :])
                dk_acc[keys, :] += _dot(dz_ref[2 * slot + h], qm[h], 0, 0)
                dv_acc[keys, :] += _dot(a_ref[2 * slot + h], dom[h], 0, 0)

        _tiles_reversed(i, tile, lambda nearest: jnp.max(r_ref[...]) >= VANISH)
        dq_ref[...] = dq_acc[...].astype(BF16)

        @pl.when(i == nq - 1)
        def _():
            dk_ref[...] = dk_acc[...].astype(BF16)
            dv_ref[...] = dv_acc[...].astype(BF16)

    W = n_pairs * PAIR
    in_specs = _qkv_specs(S, tq, n_pairs, base) + [
        pl.BlockSpec((tq, PAIR), lambda p, i: (i, p)),
        pl.BlockSpec((tq, PAIR), lambda p, i: (i, p))]
    out_specs = [pl.BlockSpec((tq, PAIR), lambda p, i: (i, p)),
                 pl.BlockSpec((S, PAIR), lambda p, i: (0, p)),
                 pl.BlockSpec((S, PAIR), lambda p, i: (0, p))]
    big, stat = (4, tq, tk), (2, tq, 128)
    return _call_carrying(
        body, exchange, "sb_bwd", (n_pairs, nq), in_specs, out_specs, [jax.ShapeDtypeStruct((S, W), BF16)] * 3,
        [pltpu.VMEM(big, F32)] * 4 + [pltpu.VMEM(big, BF16)] * 4 + [pltpu.VMEM(stat, F32)] * 3
        + [pltpu.VMEM((tq, PAIR), F32), pltpu.VMEM((S, PAIR), F32), pltpu.VMEM((S, PAIR), F32),
           pltpu.VMEM((2, S, PAIR), BF16)],
        (qkv, qkv, qkv, o32, do))


def _fox_fwd(qkv, fcx, fcr, n_pairs, base, tq, exchange=None):
    S = qkv.shape[0]
    tk = tq
    scale = HEAD_DIM ** -0.5
    spare = (HEAD_DIM, 0)

    def body(q_ref, k_ref, v_ref, fq_ref, fk_ref, o_ref, lse_ref, s_ref, p_ref, m_ref, al_ref, fqr_ref, acc_ref, vm_ref):
        i = pl.program_id(1)
        masks = _head_masks()

        @pl.when(i == 0)
        def _():
            _fill_masked(vm_ref, v_ref, masks, ones_lane=spare)

        q2 = q_ref[...] * scale
        qm = [_sel(m, q2) for m in masks]
        f0, f1 = _per_head(fq_ref[...], masks)
        fqr_ref[0] = f0
        fqr_ref[1] = f1
        m_ref[...] = jnp.full(m_ref.shape, NEG, F32)
        acc_ref[...] = jnp.zeros_like(acc_ref)

        def tile(j, diag, slot, heads):
            off = pl.multiple_of(j * tk, tk)
            k2 = k_ref[pl.ds(off, tk), :]
            v2 = v_ref[pl.ds(off, tk), :]
            fk2 = fk_ref[0, j]
            for h in heads:
                s_ref[2 * slot + h] = _dot(qm[h], k2, 1, 1)
            yield
            for h in heads:
                fk_row = fk2[h:h + 1, :]

                def probs(r0, h=h, fk_row=fk_row):
                    rows = pl.ds(r0, RC)
                    sv = s_ref[2 * slot + h, rows, :] - fk_row
                    if diag:
                        sv = jnp.where(_diag_mask(r0, tk, False), sv, NEG)
                    fq = fqr_ref[h, rows, :]
                    m_prev = m_ref[h, rows, :]
                    m_new = jnp.maximum(m_prev, jnp.max(sv, axis=-1, keepdims=True) + fq)
                    p_ref[2 * slot + h, rows, :] = jnp.exp(sv + _wide(fq - m_new, tk)).astype(BF16)
                    al_ref[2 * slot + h, rows, :] = jnp.exp(m_prev - m_new)
                    m_ref[h, rows, :] = m_new
                _chunks(tq, probs)
            yield
            for h in heads:
                acc_ref[h] = acc_ref[h] * al_ref[2 * slot + h] + _dot(p_ref[2 * slot + h], vm_ref[h, pl.ds(off, tk), :])

        _tiles(i, tile)
        a0, a1 = acc_ref[0], acc_ref[1]
        l0 = _rep(a0[:, spare[0]:spare[0] + 1])
        l1 = _rep(a1[:, spare[1]:spare[1] + 1])
        o_ref[...] = jnp.where(masks[0], a0 / l0, a1 / l1).astype(BF16)
        lse_ref[...] = jnp.where(masks[0], m_ref[0] + jnp.log(l0), m_ref[1] + jnp.log(l1))

    W = n_pairs * PAIR
    return _call_carrying(
        body, exchange, "fox_fwd", (n_pairs, S // tq), _fox_specs(S, tq, n_pairs, base),
        [pl.BlockSpec((tq, PAIR), lambda p, i: (i, p)), pl.BlockSpec((tq, PAIR), lambda p, i: (i, p))],
        [jax.ShapeDtypeStruct((S, W), BF16), jax.ShapeDtypeStruct((S, W), F32)],
        [pltpu.VMEM((4, tq, tk), F32), pltpu.VMEM((4, tq, tk), BF16), pltpu.VMEM((2, tq, 128), F32),
         pltpu.VMEM((4, tq, 128), F32), pltpu.VMEM((2, tq, 128), F32), pltpu.VMEM((2, tq, 128), F32),
         pltpu.VMEM((2, S, PAIR), BF16)],
        (qkv, qkv, qkv, fcx, fcr))


def _fox_bwd(qkv, fcx, fcr, o, lse, do, n_pairs, base, tq, exchange=None):
    S = qkv.shape[0]
    tk = tq
    nq = S // tq
    scale = HEAD_DIM ** -0.5

    def body(q_ref, k_ref, v_ref, fq_ref, fk_ref, o_ref, lse_ref, do_ref,
             dq_ref, dk_ref, dv_ref, dfq_ref, dfk_ref,
             s_ref, dp_ref, p_ref, ds_ref, row_ref, dl_ref, dfq_acc, col_ref, dq_acc, dk_acc, dv_acc, ks_ref):
        i = pl.program_id(1)
        masks = _head_masks()

        @pl.when(i == 0)
        def _():
            dk_acc[...] = jnp.zeros_like(dk_acc)
            dv_acc[...] = jnp.zeros_like(dv_acc)
            dfk_ref[...] = jnp.zeros_like(dfk_ref)
            _fill_masked(ks_ref, k_ref, masks, mul=scale)

        q2 = q_ref[...] * scale
        do2 = do_ref[...]
        qm = [_sel(m, q2) for m in masks]
        dom = [_sel(m, do2) for m in masks]
        f0, f1 = _per_head(fq_ref[...], masks)
        l0, l1 = _per_head(lse_ref[...], masks)
        row_ref[0] = f0 - l0
        row_ref[1] = f1 - l1
        prod = do2.astype(F32) * o_ref[...].astype(F32)
        for h in range(2):
            dl_ref[h] = _rep(jnp.sum(jnp.where(masks[h], prod, 0.0), axis=-1, keepdims=True))
        dfq_acc[...] = jnp.zeros_like(dfq_acc)
        dq_acc[...] = jnp.zeros_like(dq_acc)

        def tile(j, diag, slot, heads):
            off = pl.multiple_of(j * tk, tk)
            k2 = k_ref[pl.ds(off, tk), :]
            v2 = v_ref[pl.ds(off, tk), :]
            fk2 = fk_ref[0, j]
            for h in heads:
                s_ref[2 * slot + h] = _dot(qm[h], k2, 1, 1)
                dp_ref[2 * slot + h] = _dot(dom[h], v2, 1, 1)
            yield
            for h in heads:
                col_ref[2 * slot + h] = jnp.zeros((8, tk), F32)
                fk_row = fk2[h:h + 1, :]

                def dscore(r0, h=h, fk_row=fk_row):
                    rows = pl.ds(r0, RC)
                    p = jnp.exp(s_ref[2 * slot + h, rows, :] - fk_row + _wide(row_ref[h, rows, :], tk))
                    if diag:
                        p = jnp.where(_diag_mask(r0, tk, False), p, 0.0)
                    ds = p * (dp_ref[2 * slot + h, rows, :] - _wide(dl_ref[h, rows, :], tk))
                    p_ref[2 * slot + h, rows, :] = p.astype(BF16)
                    ds_ref[2 * slot + h, rows, :] = ds.astype(BF16)
                    dfq_acc[h, rows, :] += _rep(jnp.sum(ds, axis=-1, keepdims=True))
                    col_ref[2 * slot + h] += jnp.sum(ds.reshape(RC // 8, 8, tk), axis=0)
                _chunks(tq, dscore)
            yield
            keys = pl.ds(off, tk)
            for h in heads:
                dq_acc[...] += _dot(ds_ref[2 * slot + h], ks_ref[h, keys, :])
                dk_acc[keys, :] += _dot(ds_ref[2 * slot + h], qm[h], 0, 0)
                dv_acc[keys, :] += _dot(p_ref[2 * slot + h], dom[h], 0, 0)
            for h in heads:
                dfk_ref[0, j, h:h + 1, :] += jnp.sum(col_ref[2 * slot + h], axis=0, keepdims=True)

        _tiles(i, tile)
        dq_ref[...] = dq_acc[...].astype(BF16)
        dfq_ref[...] = jnp.where(masks[0], dfq_acc[0], dfq_acc[1])

        @pl.when(i == nq - 1)
        def _():
            dk_ref[...] = dk_acc[...].astype(BF16)
            dv_ref[...] = dv_acc[...].astype(BF16)

    W = n_pairs * PAIR
    in_specs = _fox_specs(S, tq, n_pairs, base) + [
        pl.BlockSpec((tq, PAIR), lambda p, i: (i, p)),
        pl.BlockSpec((tq, PAIR), lambda p, i: (i, p)),
        pl.BlockSpec((tq, PAIR), lambda p, i: (i, p))]
    out_specs = [pl.BlockSpec((tq, PAIR), lambda p, i: (i, p)),
                 pl.BlockSpec((S, PAIR), lambda p, i: (0, p)),
                 pl.BlockSpec((S, PAIR), lambda p, i: (0, p)),
                 pl.BlockSpec((tq, PAIR), lambda p, i: (i, p)),
                 pl.BlockSpec((1, nq, 8, tk), lambda p, i: (p, 0, 0, 0))]
    return _call_carrying(
        body, exchange, "fox_bwd", (n_pairs, nq), in_specs, out_specs,
        [jax.ShapeDtypeStruct((S, W), BF16)] * 3
        + [jax.ShapeDtypeStruct((S, W), F32), jax.ShapeDtypeStruct((n_pairs, nq, 8, tk), F32)],
        [pltpu.VMEM((4, tq, tk), F32)] * 2 + [pltpu.VMEM((4, tq, tk), BF16)] * 2
        + [pltpu.VMEM((2, tq, 128), F32)] * 3 + [pltpu.VMEM((4, 8, tk), F32)]
        + [pltpu.VMEM((tq, PAIR), F32), pltpu.VMEM((S, PAIR), F32), pltpu.VMEM((S, PAIR), F32),
           pltpu.VMEM((2, S, PAIR), BF16)],
        (qkv, qkv, qkv, fcx, fcr, o, lse, do))


def _col_chunks(n, width=256):
    return [(c, min(width, n - c)) for c in range(0, n, width)]


def _swiglu_fwd(u2, wg, wu):
    S, D = u2.shape
    FF = wg.shape[1]
    tm, tn = _pick(S, (512, 256, 128)), _divisors(FF, 1536)[0]

    def body(u_ref, g_ref, w_ref, a_ref, b_ref, h_ref):
        u = u_ref[...]
        for c, w in _col_chunks(tn):
            cols = slice(c, c + w)
            a = _dot(u, g_ref[:, cols])
            b = _dot(u, w_ref[:, cols])
            a_ref[:, cols] = a.astype(BF16)
            b_ref[:, cols] = b.astype(BF16)
            h_ref[:, cols] = (a / (1.0 + jnp.exp(-a)) * b).astype(BF16)

    spec_o = pl.BlockSpec((tm, tn), lambda i, j: (i, j))
    return pl.pallas_call(
        body, name="swiglu_fwd", grid=(S // tm, FF // tn),
        in_specs=[pl.BlockSpec((tm, D), lambda i, j: (i, 0)),
                  pl.BlockSpec((D, tn), lambda i, j: (0, j)),
                  pl.BlockSpec((D, tn), lambda i, j: (0, j))],
        out_specs=[spec_o] * 3, out_shape=[jax.ShapeDtypeStruct((S, FF), BF16)] * 3,
        compiler_params=_params(("parallel", "parallel")),
    )(u2, wg, wu)


def _swiglu_bwd(dh, wd, a, b):
    S, D = dh.shape
    FF = wd.shape[0]
    tm, tn = _pick(S, (512, 256, 128)), _divisors(FF, 1536)[0]

    def body(dh_ref, w_ref, a_ref, b_ref, da_ref, db_ref):
        dh_blk = dh_ref[...]
        for c, w in _col_chunks(tn):
            cols = slice(c, c + w)
            dhin = _dot(dh_blk, w_ref[cols, :], 1, 1)
            av = a_ref[:, cols].astype(F32)
            bv = b_ref[:, cols].astype(F32)
            sig = 1.0 / (1.0 + jnp.exp(-av))
            da_ref[:, cols] = (dhin * bv * (sig * (1.0 + av * (1.0 - sig)))).astype(BF16)
            db_ref[:, cols] = (dhin * (av * sig)).astype(BF16)

    spec_o = pl.BlockSpec((tm, tn), lambda i, j: (i, j))
    return pl.pallas_call(
        body, name="swiglu_bwd", grid=(S // tm, FF // tn),
        in_specs=[pl.BlockSpec((tm, D), lambda i, j: (i, 0)),
                  pl.BlockSpec((tn, D), lambda i, j: (j, 0)), spec_o, spec_o],
        out_specs=[spec_o] * 2, out_shape=[jax.ShapeDtypeStruct((S, FF), BF16)] * 2,
        compiler_params=_params(("parallel", "parallel")),
    )(dh, wd, a, b)


def _gate_bwd(dmix, wo, g_sb, g_fx, y_sb, y_fx):
    S, D = dmix.shape
    tm, tn = _pick(S, (512, 256, 128)), _divisors(D, 1024)[0]

    def body(dm_ref, w_ref, gs_ref, gf_ref, ys_ref, yf_ref, dys_ref, dyf_ref, dls_ref, dlf_ref, bs_ref, bf_ref):
        @pl.when(pl.program_id(1) == 0)
        def _():
            bs_ref[...] = jnp.zeros_like(bs_ref)
            bf_ref[...] = jnp.zeros_like(bf_ref)

        dm_blk = dm_ref[...]
        for c, w in _col_chunks(tn):
            cols = slice(c, c + w)
            dmi = _dot(dm_blk, w_ref[cols, :], 1, 1)
            gs, gf = gs_ref[:, cols], gf_ref[:, cols]
            dys_ref[:, cols] = (dmi * gs).astype(BF16)
            dyf_ref[:, cols] = (dmi * gf).astype(BF16)
            dls = dmi * ys_ref[:, cols] * gs * (1.0 - gs)
            dlf = dmi * yf_ref[:, cols] * gf * (1.0 - gf)
            dls_ref[:, cols] = dls.astype(BF16)
            dlf_ref[:, cols] = dlf.astype(BF16)
            bs_ref[0:1, cols] += _colsum(dls)
            bf_ref[0:1, cols] += _colsum(dlf)

    t = pl.BlockSpec((tm, tn), lambda j, i: (i, j))
    accs = pl.BlockSpec((8, tn), lambda j, i: (0, j))
    return pl.pallas_call(
        body, name="gate_bwd", grid=(D // tn, S // tm),
        in_specs=[pl.BlockSpec((tm, D), lambda j, i: (i, 0)),
                  pl.BlockSpec((tn, D), lambda j, i: (j, 0)), t, t, t, t],
        out_specs=[t, t, t, t, accs, accs],
        out_shape=[jax.ShapeDtypeStruct((S, D), BF16)] * 4 + [jax.ShapeDtypeStruct((8, D), F32)] * 2,
        compiler_params=_params(("parallel", "arbitrary")),
    )(dmix, wo, g_sb, g_fx, y_sb, y_fx)


def _local_step(x, target, ada8, lnp8, bg_sb, bg_fx, bf_pad, wqkv, wf, wgs, wgf, gather, later_weights, pack_early,
                pack_last):
    S, D = x.shape
    W = wqkv.shape[1] // 6
    n_pairs = W // PAIR
    n_heads = W // HEAD_DIM
    ts = _pick(S, (512, 256, 128))
    tq = _pick(S, (256, 128))

    u1 = _ln_mod(x, ada8, ts)
    qkv = _mm([(u1, wqkv)], 'nn', BF16, "in_qkv")
    f = _mm([(u1, wf)], 'nn', F32, "in_f")
    g_sb = _mm([(u1, wgs)], 'nn', F32, "in_gsb", bias=bg_sb, act='sigmoid')
    g_fx = _mm([(u1, wgf)], 'nn', F32, "in_gfx", bias=bg_fx, act='sigmoid')
    fc = _fgate_fwd(f, bf_pad, _pick(S, (512, 256, 128)))
    fch = fc[:, :n_heads]
    fcx = jnp.repeat(fch, HEAD_DIM, axis=1)
    nq = S // tq
    fcr = jnp.pad(fch.T.reshape(n_pairs, 2, nq, tq).transpose(0, 2, 1, 3),
                  ((0, 0), (0, 0), (0, 6), (0, 0)))
    o_sb, o_sb32, *zone_a = _sb_fwd(qkv, n_pairs, 0, tq, gather[0])
    o_fx, lse, *zone_b = _fox_fwd(qkv, fcx, fcr, n_pairs, 3 * n_pairs, tq, gather[1])
    wsb, wfx, wo, wfg, wfu, wfd = later_weights(*zone_a, *zone_b)
    y_sb =_mm([(o_sb, wsb)], 'nn', F32, "out_sb")
    y_fx = _mm([(o_fx, wfx)], 'nn', F32, "out_fx")
    mix_in, mix = _gate_mix_out(g_sb, g_fx, y_sb, y_fx, wo, ts)
    x1, u2 = _post_attn(x, mix, ada8, lnp8, ts)
    a, b, hin = _swiglu_fwd(u2, wfg, wfu)
    dr2, dh, st_loss = _loss_head(x1, hin, wfd, target, ada8, lnp8, ts)

    da, db = _swiglu_bwd(dh, wfd, a, b)
    g_wfd = _mm([(hin, dh)], 'tn', F32, "g_ffn_down")
    g_wfg = _mm([(u2, da)], 'tn', F32, "g_ffn_gate")
    g_wfu = _mm([(u2, db)], 'tn', F32, "g_ffn_up")
    dr1, dmix, st_mid = _mid_bwd(da, db, wfg, wfu, x1, dr2, mix, x, ada8, lnp8, _pick(S, (256, 128)))
    dys, dyf, dls, dlf, gb_sb, gb_fx = _gate_bwd(dmix, wo, g_sb, g_fx, y_sb, y_fx)
    g_wo = _mm([(mix_in, dmix)], 'tn', F32, "g_w_o")
    do_sb = _mm([(dys, wsb)], 'nt', BF16, "d_o_sb")
    do_fx = _mm([(dyf, wfx)], 'nt', BF16, "d_o_fx")
    g_wsb = _mm([(o_sb, dys)], 'tn', F32, "g_sb_out")
    g_wfx = _mm([(o_fx, dyf)], 'tn', F32, "g_fox_out")
    scatter = pack_early(dict(sb=g_wsb, fx=g_wfx, o=g_wo, fg=g_wfg, fu=g_wfu, fd=g_wfd))
    dq_s, dk_s, dv_s, *zone_a = _sb_bwd(qkv, o_sb32, do_sb, n_pairs, 0, tq, scatter[0])
    dq_f, dk_f, dv_f, dfq, dfk, *zone_b = _fox_bwd(qkv, fcx, fcr, o_fx, lse, do_fx, n_pairs, 3 * n_pairs, tq,
                                                    scatter[1])
    early = [(scatter[0], zone_a[0] if zone_a else None), (scatter[1], zone_b[0] if zone_b else None)]
    dfc = dfq[:, ::HEAD_DIM] - dfk[:, :, :2, :].transpose(0, 2, 1, 3).reshape(n_heads, S).T
    dfc = jnp.pad(dfc, ((0, 0), (0, 128 - n_heads)))
    df, gb_f = _fgate_bwd(dfc, f, bf_pad, _pick(S, (512, 256, 128)))
    dqkv = jnp.concatenate([dq_s, dk_s, dv_s, dq_f, dk_f, dv_f], axis=1)
    g_wqkv = [_mm([(u1, dqkv)], 'tn', F32, "g_in_qkv")]
    g_wf = _mm([(u1, df)], 'tn', F32, "g_in_f")
    g_wgs = _mm([(u1, dls)], 'tn', F32, "g_in_gsb")
    g_wgf = _mm([(u1, dlf)], 'tn', F32, "g_in_gfx")
    wgrads = dict(qkv=g_wqkv, f=g_wf, gs=g_wgs, gf=g_wgf)
    last = pack_last(wgrads)
    gx, st_first, *last_zone = _first_bwd(
        [(dqkv, wqkv), (df, wf), (dls, wgs), (dlf, wgf)], x, dr1, ada8, last)
    last_zone = last_zone[0] if last_zone else None

    stats = dict(loss=st_loss, mid=st_mid, first=st_first, gb_sb=gb_sb, gb_fx=gb_fx, gb_f=gb_f)
    return gx, wgrads, stats, early, (last, last_zone)


def _position():
    x, y, c = lax.axis_index("x"), lax.axis_index("y"), lax.axis_index("c")
    return x, y, c, 4 * x + 2 * y + c


def _flip(x, y, c, k):
    px = 1 - x if k & 4 else x
    py = 1 - y if k & 2 else y
    pc = 1 - c if k & 1 else c
    return (px, py, pc), 4 * px + 2 * py + pc


def _all_gather_small(v, name):
    r, n = v.shape

    def body(x_ref, out_ref, send_sems, recv_sems, local_sem):
        x, y, c, me = _position()
        mine = pltpu.make_async_copy(x_ref, out_ref.at[me], local_sem)
        mine.start()
        sends = []
        for k in range(1, N_DEV):
            peer, _ = _flip(x, y, c, k)
            cp = pltpu.make_async_remote_copy(
                src_ref=x_ref, dst_ref=out_ref.at[me], send_sem=send_sems.at[k - 1], recv_sem=recv_sems.at[k - 1],
                device_id=peer, device_id_type=MESH)
            cp.start()
            sends.append(cp)
        for k in range(1, N_DEV):
            peer, slot = _flip(x, y, c, k)
            pltpu.make_async_remote_copy(
                src_ref=x_ref, dst_ref=out_ref.at[slot], send_sem=send_sems.at[k - 1], recv_sem=recv_sems.at[k - 1],
                device_id=peer, device_id_type=MESH).wait_recv()
        for cp in sends:
            cp.wait_send()
        mine.wait()

    return pl.pallas_call(
        body, name=name, out_shape=jax.ShapeDtypeStruct((N_DEV, r, n), v.dtype),
        in_specs=[pl.BlockSpec(memory_space=pltpu.VMEM)], out_specs=pl.BlockSpec(memory_space=pltpu.VMEM),
        scratch_shapes=[pltpu.SemaphoreType.DMA((N_DEV - 1,)), pltpu.SemaphoreType.DMA((N_DEV - 1,)),
                        pltpu.SemaphoreType.DMA],
    )(v)


def _all_gather_weights(packed):
    R, C = packed.shape

    def body(x_ref, out_ref, send_sems, recv_sems, local_sem):
        x, y, c, me = _position()
        sibling, sib_slot = _flip(x, y, c, 1)
        mine = pltpu.make_async_copy(x_ref, out_ref.at[me], local_sem)
        mine.start()

        def copy(k, slot, to, src=None):
            return pltpu.make_async_remote_copy(
                src_ref=out_ref.at[slot] if src is None else src, dst_ref=out_ref.at[slot],
                send_sem=send_sems.at[k], recv_sem=recv_sems.at[k], device_id=to, device_id_type=MESH)

        first = [copy(0, me, sibling, src=x_ref)]
        chips = (4, 2, 6)
        for n, k in enumerate(chips):
            peer, _ = _flip(x, y, c, k)
            first.append(copy(1 + n, me, peer, src=x_ref))
        for cp in first:
            cp.start()
        passed = []
        for n, k in enumerate(chips):
            peer, slot = _flip(x, y, c, k)
            copy(1 + n, slot, peer).wait_recv()
            cp = copy(4 + n, slot, sibling)
            cp.start()
            passed.append(cp)
        copy(0, sib_slot, sibling).wait_recv()
        for n, k in enumerate(chips):
            _, slot = _flip(x, y, c, k | 1)
            copy(4 + n, slot, sibling).wait_recv()
        for cp in first + passed:
            cp.wait_send()
        mine.wait()

    return pl.pallas_call(
        body, name="all_gather_weights", out_shape=jax.ShapeDtypeStruct((N_DEV, R, C), packed.dtype),
        in_specs=[pl.BlockSpec(memory_space=pl.ANY)], out_specs=pl.BlockSpec(memory_space=pl.ANY),
        scratch_shapes=[pltpu.SemaphoreType.DMA((7,)), pltpu.SemaphoreType.DMA((7,)), pltpu.SemaphoreType.DMA],
    )(packed)


def _own_slot(land, own):
    me = 4 * lax.axis_index("x") + 2 * lax.axis_index("y") + lax.axis_index("c")
    return lax.dynamic_update_slice(land, own[None], (me, 0, 0))


def _sum_slots(recv, name, tr):
    n, R, C = recv.shape

    def body(r_ref, o_ref):
        acc = r_ref[0].astype(F32)
        for s in range(1, n):
            acc = acc + r_ref[s].astype(F32)
        o_ref[...] = acc

    return pl.pallas_call(
        body, name=name, grid=(R // tr,), in_specs=[pl.BlockSpec((n, tr, C), lambda i: (0, i, 0))],
        out_specs=pl.BlockSpec((tr, C), lambda i: (i, 0)), out_shape=jax.ShapeDtypeStruct((R, C), F32),
        compiler_params=_params(("parallel",)),
    )(recv)


def _sum_stats(st_all, loss_row):
    n, r, D = st_all.shape

    def body(s_ref, o_ref, l_ref):
        acc = s_ref[0]
        for d in range(1, n):
            acc = acc + s_ref[d]
        o_ref[...] = acc
        l_ref[...] = jnp.zeros((8, 128), F32) + jnp.sum(acc[loss_row:loss_row + 1, :], axis=-1, keepdims=True)

    return pl.pallas_call(
        body, name="sum_stats", out_shape=[jax.ShapeDtypeStruct((r, D), F32), jax.ShapeDtypeStruct((8, 128), F32)],
    )(st_all)


def _adamw(w, g, m, v, name):
    R, C = w.shape
    tr = _pick(R, (256, 176, 128, 64, 32, 16, 8))
    c1 = 1.0 / (1.0 - ADAM_B1 ** ADAM_STEP)
    c2 = 1.0 / (1.0 - ADAM_B2 ** ADAM_STEP)

    def body(w_ref, g_ref, m_ref, v_ref, d_ref, nm_ref, nv_ref):
        gv = g_ref[...]
        nm = ADAM_B1 * m_ref[...] + (1.0 - ADAM_B1) * gv
        nv = ADAM_B2 * v_ref[...] + (1.0 - ADAM_B2) * (gv * gv)
        nm_ref[...] = nm
        nv_ref[...] = nv
        d_ref[...] = -ADAM_LR * ((nm * c1) / (jnp.sqrt(nv * c2) + ADAM_EPS) + ADAM_WD * w_ref[...])

    spec = pl.BlockSpec((tr, C), lambda i: (i, 0))
    return pl.pallas_call(
        body, name=name, grid=(R // tr,), in_specs=[spec] * 4, out_specs=[spec] * 3,
        out_shape=[jax.ShapeDtypeStruct((R, C), F32)] * 3, compiler_params=_params(("parallel",)),
    )(w, g, m, v)


def _round16(n):
    return -(-n // 16) * 16


def _pack_layout(D, in_cols, ff, W):
    parts = [("in", D * (in_cols // N_DEV) // D), ("fg", ff // N_DEV), ("fu", ff // N_DEV),
             ("sb", W * (D // N_DEV) // D), ("fx", W * (D // N_DEV) // D), ("o", D // N_DEV), ("fd", ff // N_DEV)]
    layout, off = {}, 0
    for nm, rows in parts:
        layout[nm] = (off, rows)
        off += _round16(rows)
    return layout, off


def _rows_of(a, D, rows):
    a = a.reshape(rows, D)
    return jnp.pad(a, ((0, _round16(rows) - rows), (0, 0)))


def _cols_to_dest(g, D):
    K, N = g.shape
    n = N // N_DEV
    return g.reshape(K, N_DEV, n).transpose(1, 0, 2).reshape(N_DEV, K * n // D, D)


def _cols_from_src(blocks, K, n):
    return blocks.reshape(N_DEV, K, n).transpose(1, 0, 2).reshape(K, N_DEV * n)


def _pad_rows16(a):
    rows = a.shape[1]
    return jnp.pad(a, ((0, 0), (0, _round16(rows) - rows), (0, 0)))


def kernel(x, c, w_ada, b_ada, w_in, b_gate, b_forget, w_sb_out, w_fox_out, w_o, ln1_g, ln1_b, w_ffn_gate, w_ffn_up, w_ffn_down, ln2_g, ln2_b, loss_target, m_w_ada, m_b_ada, m_w_in, m_b_gate, m_b_forget, m_w_sb_out, m_w_fox_out, m_w_o, m_ln1_g, m_ln1_b, m_w_ffn_gate, m_w_ffn_up, m_w_ffn_down, m_ln2_g, m_ln2_b, v_w_ada, v_b_ada, v_w_in, v_b_gate, v_b_forget, v_w_sb_out, v_w_fox_out, v_w_o, v_ln1_g, v_ln1_b, v_w_ffn_gate, v_w_ffn_up, v_w_ffn_down, v_ln2_g, v_ln2_b):
    S, D = x.shape[1], x.shape[2]
    W = w_sb_out.shape[1]
    n_heads = b_forget.shape[1]
    ff = w_ffn_down.shape[1] * N_DEV
    in_loc = w_in.shape[2]
    in_cols = in_loc * N_DEV
    ada_loc = w_ada.shape[2]
    n_cond = ada_loc * N_DEV // D
    assert w_ada.shape[0] == 1 and n_cond == 6 and in_cols == 6 * W + n_heads + 2 * D and n_heads <= 128
    me = 4 * lax.axis_index("x") + 2 * lax.axis_index("y") + lax.axis_index("c")

    c_all = _all_gather_small(c, "gather_c").reshape(N_DEV, D)
    c16 = jnp.pad(c_all, ((0, 16 - N_DEV), (0, 0)))
    b_cols = lax.dynamic_slice(b_ada, (0, me * ada_loc), (1, ada_loc))
    ada_cols = _mm([(c16, w_ada[0])], 'nn', F32, "ada_fwd", bias=b_cols, silu_a=True)[:N_DEV]
    ada_all = _all_gather_small(ada_cols, "gather_ada")
    ada_me = lax.dynamic_index_in_dim(ada_all, me, axis=1, keepdims=False)
    ada8 = jnp.pad(ada_me.reshape(n_cond, D), ((0, 8 - n_cond), (0, 0)))
    lnp8 = jnp.concatenate([ln1_g, ln1_b, ln2_g, ln2_b, jnp.zeros((4, D), F32)], axis=0)

    layout, R = _pack_layout(D, in_cols, ff, W)
    shards = dict(**{"in": w_in[0]}, fg=w_ffn_gate[0], fu=w_ffn_up[0], sb=w_sb_out[0], fx=w_fox_out[0], o=w_o[0],
                  fd=w_ffn_down[0])
    rows_fwd, rows_bwd = ("sb", "fx", "o"), ("fd", "o", "sb", "fx")
    rows_of = {nm: layout[nm][1] for nm in layout}

    def offsets(names):
        offs, off = {}, 0
        for nm in names:
            offs[nm] = off
            off += _round16(rows_of[nm])
        return offs

    def as_rows(names):
        return jnp.concatenate([_rows_of(shards[nm].astype(BF16), D, rows_of[nm]) for nm in names], axis=0)

    def whole_from(blocks):
        return blocks.transpose(1, 0, 2).reshape(blocks.shape[1], N_DEV * blocks.shape[2])

    def blocks_of(g):
        return g.reshape(g.shape[0], N_DEV, g.shape[1] // N_DEV).transpose(1, 0, 2)

    gather_src = [as_rows(rows_fwd),
                  jnp.concatenate([shards["fg"].astype(BF16), shards["fu"].astype(BF16)], axis=0), as_rows(("fd",))]
    gathered_in = _all_gather_weights(shards["in"].astype(BF16))

    w_in_full = whole_from(gathered_in)
    wqkv = w_in_full[:, :6 * W]
    wf = jnp.pad(w_in_full[:, 6 * W:6 * W + n_heads], ((0, 0), (0, 128 - n_heads)))
    wgs = w_in_full[:, 6 * W + n_heads:6 * W + n_heads + D]
    wgf = w_in_full[:, 6 * W + n_heads + D:]
    bf_pad = jnp.pad(b_forget, ((0, 0), (0, 128 - n_heads)))

    def later_weights(zone_rows, zone_gate_up, zone_down):
        rows, offs = _own_slot(zone_rows, gather_src[0]), offsets(rows_fwd)
        part = {nm: rows[:, offs[nm]:offs[nm] + rows_of[nm], :] for nm in rows_fwd}
        gate_up = _own_slot(zone_gate_up, gather_src[1])
        down = _own_slot(zone_down, gather_src[2])[:, :rows_of["fd"], :]
        return (_cols_from_src(part["sb"], W, D // N_DEV), _cols_from_src(part["fx"], W, D // N_DEV),
                part["o"].reshape(D, D), whole_from(gate_up[:, :D, :]), whole_from(gate_up[:, D:, :]),
                down.reshape(ff, D))

    def pack_early(g):
        dest = {"sb": _cols_to_dest(g["sb"], D), "fx": _cols_to_dest(g["fx"], D),
                "o": g["o"].reshape(N_DEV, D // N_DEV, D), "fd": g["fd"].reshape(N_DEV, ff // N_DEV, D)}
        rows = jnp.concatenate([_pad_rows16(dest[nm].astype(BF16)) for nm in rows_bwd], axis=1)
        gate_up = jnp.concatenate([blocks_of(g["fg"].astype(BF16)), blocks_of(g["fu"].astype(BF16))], axis=1)
        return [(rows, True), (gate_up, True)]

    def pack_last(g):
        g_in = jnp.concatenate(g["qkv"] + [g["f"][:, :n_heads], g["gs"], g["gf"]], axis=1)
        return blocks_of(g_in.astype(BF16)), True

    gx, wg, st, early, ((pack_in, _), land_in) = _local_step(
        x[0], loss_target[0], ada8, lnp8, b_gate[:, :D], b_gate[:, D:], bf_pad, wqkv, wf, wgs, wgf,
        [(gather_src[0], False), [(gather_src[1], False), (gather_src[2], False)]],
        later_weights, pack_early, pack_last)

    def summed(zone, sent, name):
        own = lax.dynamic_index_in_dim(sent, me, axis=0, keepdims=False)
        rows = zone.shape[1]
        block = max(t for t in range(16, 705, 16) if rows % t == 0)
        return _sum_slots(_own_slot(zone, own), name, block)

    ((sent_rows, _), zone_rows), ((sent_gate_up, _), zone_gate_up) = early
    gate_up = summed(zone_gate_up, sent_gate_up, "sum_grads_gate_up")
    gsum = {"in": summed(land_in, pack_in, "sum_grads_in"), "fg": gate_up[:D], "fu": gate_up[D:]}
    total, offs = summed(zone_rows, sent_rows, "sum_grads_rows"), offsets(rows_bwd)
    for nm in rows_bwd:
        gsum[nm] = total[offs[nm]:offs[nm] + rows_of[nm]]

    def gshard(nm, shape):
        return gsum[nm].reshape(shape)

    zrow = jnp.zeros((1, D), F32)
    gb_f_row = jnp.pad(st["gb_f"][0:1], ((0, 0), (0, D - 128)))
    stats16 = jnp.concatenate([
        st["first"][1:2], st["first"][0:1], st["mid"][4:5], st["mid"][1:2], st["mid"][0:1], st["loss"][3:4],
        st["mid"][2:3], st["mid"][3:4], st["loss"][1:2], st["loss"][2:3], st["gb_sb"][0:1], st["gb_fx"][0:1],
        st["loss"][0:1], gb_f_row, zrow, zrow], axis=0)
    st_all = _all_gather_small(stats16, "gather_stats")
    st_sum, loss_blk = _sum_stats(st_all, 12)
    loss = loss_blk[0, 0]

    d_ada_all = st_all[:, :n_cond, :].reshape(N_DEV, n_cond * D)
    d_cols = lax.dynamic_slice(d_ada_all, (0, me * ada_loc), (N_DEV, ada_loc))
    d16 = jnp.pad(d_cols, ((0, 16 - N_DEV), (0, 0)))
    g_w_ada = _mm([(c16, d16)], 'tn', F32, "ada_wgrad", silu_a=True)

    small_w = jnp.concatenate([b_ada.reshape(n_cond, D), ln1_g, ln1_b, ln2_g, ln2_b, b_gate.reshape(2, D), zrow,
                               jnp.pad(b_forget, ((0, 0), (0, D - n_heads))), zrow, zrow], axis=0)
    small_m = jnp.concatenate([m_b_ada.reshape(n_cond, D), m_ln1_g, m_ln1_b, m_ln2_g, m_ln2_b, m_b_gate.reshape(2, D),
                               zrow, jnp.pad(m_b_forget, ((0, 0), (0, D - n_heads))), zrow, zrow], axis=0)
    small_v = jnp.concatenate([v_b_ada.reshape(n_cond, D), v_ln1_g, v_ln1_b, v_ln2_g, v_ln2_b, v_b_gate.reshape(2, D),
                               zrow, jnp.pad(v_b_forget, ((0, 0), (0, D - n_heads))), zrow, zrow], axis=0)
    sm = _adamw(small_w, st_sum, small_m, small_v, "adamw_small")

    def small(a, nm):
        if nm == "b_ada":
            return a[0:n_cond].reshape(1, n_cond * D)
        if nm == "b_gate":
            return a[10:12].reshape(1, 2 * D)
        if nm == "b_forget":
            return a[13:14, :n_heads]
        row = {"ln1_g": 6, "ln1_b": 7, "ln2_g": 8, "ln2_b": 9}[nm]
        return a[row:row + 1]

    big = {
        "w_ada": (w_ada[0], g_w_ada, m_w_ada[0], v_w_ada[0]),
        "w_in": (w_in[0], gshard("in", w_in.shape[1:]), m_w_in[0], v_w_in[0]),
        "w_sb_out": (w_sb_out[0], gshard("sb", w_sb_out.shape[1:]), m_w_sb_out[0], v_w_sb_out[0]),
        "w_fox_out": (w_fox_out[0], gshard("fx", w_fox_out.shape[1:]), m_w_fox_out[0], v_w_fox_out[0]),
        "w_o": (w_o[0], gshard("o", w_o.shape[1:]), m_w_o[0], v_w_o[0]),
        "w_ffn_gate": (w_ffn_gate[0], gshard("fg", w_ffn_gate.shape[1:]), m_w_ffn_gate[0], v_w_ffn_gate[0]),
        "w_ffn_up": (w_ffn_up[0], gshard("fu", w_ffn_up.shape[1:]), m_w_ffn_up[0], v_w_ffn_up[0]),
        "w_ffn_down": (w_ffn_down[0], gshard("fd", w_ffn_down.shape[1:]), m_w_ffn_down[0], v_w_ffn_down[0]),
    }
    order = ["w_ada", "b_ada", "w_in", "b_gate", "b_forget", "w_sb_out", "w_fox_out", "w_o", "ln1_g", "ln1_b",
             "w_ffn_gate", "w_ffn_up", "w_ffn_down", "ln2_g", "ln2_b"]
    grads, deltas, new_ms, new_vs = [], [], [], []
    for nm in order:
        if nm in big:
            w, g, m, v = big[nm]
            d, nm_, nv_ = _adamw(w, g, m, v, "adamw_" + nm)
            grads.append(g[None])
            deltas.append(d[None])
            new_ms.append(nm_[None])
            new_vs.append(nv_[None])
        else:
            grads.append(small(st_sum, nm))
            deltas.append(small(sm[0], nm))
            new_ms.append(small(sm[1], nm))
            new_vs.append(small(sm[2], nm))
    return (loss, gx[None], *grads, *deltas, *new_ms, *new_vs)
```

```python
import jax
import jax.numpy as jnp
from jax import lax
from jax.experimental import pallas as pl
from jax.experimental.pallas import tpu as pltpu

F32 = jnp.float32
BF16 = jnp.bfloat16

HEAD_DIM = 64
PAIR = 2 * HEAD_DIM
LN_EPS = 1e-5
ALPHA = 2.0 ** 0.25
ADAM_LR, ADAM_B1, ADAM_B2, ADAM_EPS, ADAM_WD, ADAM_STEP = 0.001, 0.9, 0.999, 1e-08, 0.01, 10
N_DEV = 8
VMEM_LIMIT = 56 * 1024 * 1024
MESH = pl.DeviceIdType.MESH


def _dot(a, b, ca=1, cb=0):
    return lax.dot_general(a, b, (((ca,), (cb,)), ((), ())), preferred_element_type=F32)


def _pick(n, cands):
    for t in cands:
        if n % t == 0:
            return t
    return n


def _params(sem):
    return pltpu.CompilerParams(dimension_semantics=sem, vmem_limit_bytes=VMEM_LIMIT)


MM_BLOCK_BYTES = 40 * 1024 * 1024
LANES = 128


def _divisors(n, cap):
    ds = [d for d in range(LANES, min(n, cap) + 1, LANES) if n % d == 0]
    return sorted(ds, reverse=True) or [n]


def _mm_tiles(M, N, a_row_bytes, b_row_bytes, out_itemsize):
    best = None
    for tm in _divisors(M, 1024):
        for tn in _divisors(N, 2048):
            need = 2 * (tm * a_row_bytes + tn * b_row_bytes + tm * tn * out_itemsize) + tm * tn * 4
            if need <= MM_BLOCK_BYTES and (best is None or (tm * tn, tm) > (best[0] * best[1], best[0])):
                best = (tm, tn)
    assert best is not None, (M, N, a_row_bytes, b_row_bytes)
    return best


def _mm(pairs, mode, out_dtype, name, bias=None, act=None, silu_a=False, exchange=None):
    norm = []
    for p in pairs:
        a, b = p[0], p[1]
        kdim_a = a.shape[0] if mode == 'tn' else a.shape[1]
        K, ka, kb = (p[2], p[3], p[4]) if len(p) > 2 else (kdim_a, 0, 0)
        norm.append((a, b, K, ka, kb))
    a0, b0 = norm[0][0], norm[0][1]
    M = a0.shape[1] if mode == 'tn' else a0.shape[0]
    N = b0.shape[0] if mode == 'nt' else b0.shape[1]
    tm, tn = _mm_tiles(M, N, sum(K * a.dtype.itemsize for a, _, K, _, _ in norm),
                       sum(K * b.dtype.itemsize for _, b, K, _, _ in norm), jnp.dtype(out_dtype).itemsize)
    n_pairs = len(norm)

    in_specs, args = [], []
    for a, b, K, ka, kb in norm:
        if mode == 'tn':
            in_specs.append(pl.BlockSpec((K, tm), lambda i, j, ka=ka: (ka, i)))
        else:
            in_specs.append(pl.BlockSpec((tm, K), lambda i, j, ka=ka: (i, ka)))
        if mode == 'nt':
            in_specs.append(pl.BlockSpec((tn, K), lambda i, j, kb=kb: (j, kb)))
        else:
            in_specs.append(pl.BlockSpec((K, tn), lambda i, j, kb=kb: (kb, j)))
        args += [a, b]
    if bias is not None:
        in_specs.append(pl.BlockSpec((1, tn), lambda i, j: (0, j)))
        args.append(bias)

    ca = 0 if mode == 'tn' else 1
    cb = 1 if mode == 'nt' else 0

    def body(*refs):
        o_ref = refs[-1]
        acc = None
        for p in range(n_pairs):
            av = refs[2 * p][...]
            if silu_a:
                av = av / (1.0 + jnp.exp(-av))
            d = _dot(av.astype(BF16), refs[2 * p + 1][...].astype(BF16), ca, cb)
            acc = d if acc is None else acc + d
        if bias is not None:
            acc = acc + refs[2 * n_pairs][...]
        if act == 'sigmoid':
            acc = 1.0 / (1.0 + jnp.exp(-acc))
        o_ref[...] = acc.astype(out_dtype)

    out_spec = pl.BlockSpec((tm, tn), lambda i, j: (i, j))
    out_shape = jax.ShapeDtypeStruct((M, N), out_dtype)
    if exchange is not None:
        return _call_carrying(body, exchange, name, (M // tm, N // tn), in_specs, [out_spec], [out_shape], [], args)
    return pl.pallas_call(
        body, name=name, grid=(M // tm, N // tn), in_specs=in_specs, out_specs=out_spec, out_shape=out_shape,
        compiler_params=_params(("parallel", "parallel")),
    )(*args)


def _rows_call(body, name, row_ins, vec_ins, row_outs, acc_outs, ts):
    S = row_ins[0].shape[0]
    in_specs = [pl.BlockSpec((ts, a.shape[1]), lambda i: (i, 0)) for a in row_ins]
    in_specs += [pl.BlockSpec(a.shape, lambda i: (0, 0)) for a in vec_ins]
    out_specs = [pl.BlockSpec((ts, c), lambda i: (i, 0)) for c, _ in row_outs]
    out_specs += [pl.BlockSpec(s, lambda i: (0, 0)) for s in acc_outs]
    out_shape = [jax.ShapeDtypeStruct((S, c), dt) for c, dt in row_outs]
    out_shape += [jax.ShapeDtypeStruct(s, F32) for s in acc_outs]
    return pl.pallas_call(
        body, name=name, grid=(S // ts,), in_specs=in_specs, out_specs=out_specs, out_shape=out_shape,
        compiler_params=_params(("arbitrary",)),
    )(*row_ins, *vec_ins)


def _ln_stats(v):
    mu = jnp.mean(v, axis=-1, keepdims=True)
    d = v - mu
    var = jnp.mean(d * d, axis=-1, keepdims=True)
    rstd = lax.rsqrt(var + LN_EPS)
    return d * rstd, rstd


def _ln_bwd(dxhat, xhat, rstd):
    m1 = jnp.mean(dxhat, axis=-1, keepdims=True)
    m2 = jnp.mean(dxhat * xhat, axis=-1, keepdims=True)
    return rstd * (dxhat - m1 - xhat * m2)


def _colsum(v):
    return jnp.sum(v, axis=0, keepdims=True)


def _ln_mod(x, ada8, ts):
    D = x.shape[1]

    def body(x_ref, v_ref, u_ref):
        xhat, _ = _ln_stats(x_ref[...])
        u_ref[...] = (xhat * (1.0 + v_ref[1:2, :]) + v_ref[0:1, :]).astype(BF16)

    return _rows_call(body, "ln_mod", [x], [ada8], [(D, BF16)], [], ts)[0]


def _gate_mix_out(g_sb, g_fx, y_sb, y_fx, x, wo, ada8, lnp8, ts):
    D = y_sb.shape[1]

    def body(gs, gf, ys, yf, x_ref, v_ref, p_ref, w_ref, mi_ref, mix_ref, x1_ref, u2_ref):
        mi = (gs[...] * ys[...] + gf[...] * yf[...]).astype(BF16)
        mi_ref[...] = mi
        mix = _dot(mi, w_ref[...])
        mix_ref[...] = mix
        r1 = ALPHA * x_ref[...] + v_ref[2:3, :] * mix
        xhat, _ = _ln_stats(r1)
        x1 = xhat * p_ref[0:1, :] + p_ref[1:2, :]
        x1_ref[...] = x1
        xh1, _ = _ln_stats(x1)
        u2_ref[...] = (xh1 * (1.0 + v_ref[4:5, :]) + v_ref[3:4, :]).astype(BF16)

    return _rows_call(body, "gate_mix_out", [g_sb, g_fx, y_sb, y_fx, x], [ada8, lnp8, wo],
                      [(D, BF16), (D, F32), (D, F32), (D, BF16)], [], ts)


def _loss_head(x1, hin, wfd, target, ada8, lnp8, ts):
    D = x1.shape[1]

    def body(x1_ref, hin_ref, t_ref, v_ref, p_ref, w_ref, dr2_ref, dh_ref, st_ref):
        @pl.when(pl.program_id(0) == 0)
        def _():
            st_ref[...] = jnp.zeros_like(st_ref)

        hv = _dot(hin_ref[...], w_ref[...])
        g2 = v_ref[5:6, :]
        r2 = ALPHA * x1_ref[...] + g2 * hv
        xhat, rstd = _ln_stats(r2)
        y = xhat * p_ref[2:3, :] + p_ref[3:4, :]
        err = y - t_ref[...]
        dy = err * (1.0 / D)
        dr2 = _ln_bwd(dy * p_ref[2:3, :], xhat, rstd)
        dr2_ref[...] = dr2
        dh_ref[...] = (dr2 * g2).astype(BF16)
        st_ref[0:1, :] += _colsum(err * err) * (0.5 / D)
        st_ref[1:2, :] += _colsum(dy * xhat)
        st_ref[2:3, :] += _colsum(dy)
        st_ref[3:4, :] += _colsum(dr2 * hv)

    return _rows_call(body, "loss_head", [x1, hin, target], [ada8, lnp8, wfd], [(D, F32), (D, BF16)], [(8, D)], ts)


def _mid_bwd(da, db, wfg, wfu, x1, dr2, mix, x, ada8, lnp8, ts):
    D = x.shape[1]

    def body(da_ref, db_ref, x1_ref, dr2_ref, mix_ref, x_ref, v_ref, p_ref, wg_ref, wu_ref, dr1_ref, dmix_ref, st_ref):
        @pl.when(pl.program_id(0) == 0)
        def _():
            st_ref[...] = jnp.zeros_like(st_ref)

        du2v = _dot(da_ref[...], wg_ref[...], 1, 1) + _dot(db_ref[...], wu_ref[...], 1, 1)
        xh1, rstd1 = _ln_stats(x1_ref[...])
        dx1 = ALPHA * dr2_ref[...] + _ln_bwd(du2v * (1.0 + v_ref[4:5, :]), xh1, rstd1)
        mixv = mix_ref[...]
        g1 = v_ref[2:3, :]
        r1 = ALPHA * x_ref[...] + g1 * mixv
        xhr, rstdr = _ln_stats(r1)
        dr1 = _ln_bwd(dx1 * p_ref[0:1, :], xhr, rstdr)
        dr1_ref[...] = dr1
        dmix_ref[...] = (dr1 * g1).astype(BF16)
        st_ref[0:1, :] += _colsum(du2v * xh1)
        st_ref[1:2, :] += _colsum(du2v)
        st_ref[2:3, :] += _colsum(dx1 * xhr)
        st_ref[3:4, :] += _colsum(dx1)
        st_ref[4:5, :] += _colsum(dr1 * mixv)

    return _rows_call(body, "mid_bwd", [da, db, x1, dr2, mix, x], [ada8, lnp8, wfg, wfu], [(D, F32), (D, BF16)],
                      [(8, D)], ts)


def _first_bwd(pairs, x, dr1, ada8, exchange):
    S, D = x.shape
    norm = [(p[0], p[1]) + ((p[2], p[3], p[4]) if len(p) > 2 else (p[0].shape[1], 0, 0)) for p in pairs]
    tm = _pick(S, (256, 128))
    in_specs, args = [], []
    for a, b, K, ka, kb in norm:
        in_specs += [pl.BlockSpec((tm, K), lambda i, j, ka=ka: (i, ka)), pl.BlockSpec((D, K), lambda i, j, kb=kb: (0, kb))]
        args += [a, b]
    rows = pl.BlockSpec((tm, D), lambda i, j: (i, 0))
    sums = pl.BlockSpec((8, D), lambda i, j: (0, 0))
    n_pairs = len(norm)

    def body(*refs):
        x_ref, dr1_ref, v_ref, gx_ref, st_ref = refs[2 * n_pairs:]

        @pl.when(pl.program_id(0) == 0)
        def _():
            st_ref[...] = jnp.zeros_like(st_ref)

        du1 = None
        for p in range(n_pairs):
            d = _dot(refs[2 * p][...].astype(BF16), refs[2 * p + 1][...].astype(BF16), 1, 1)
            du1 = d if du1 is None else du1 + d
        xh0, rstd0 = _ln_stats(x_ref[...])
        gx_ref[...] = ALPHA * dr1_ref[...] + _ln_bwd(du1 * (1.0 + v_ref[1:2, :]), xh0, rstd0)
        st_ref[0:1, :] += _colsum(du1 * xh0)
        st_ref[1:2, :] += _colsum(du1)

    return _call_carrying(
        body, exchange, "first_bwd", (S // tm, 1), in_specs + [rows, rows, sums], [rows, sums],
        [jax.ShapeDtypeStruct((S, D), F32), jax.ShapeDtypeStruct((8, D), F32)], [], (*args, x, dr1, ada8),
        semantics=("arbitrary", "arbitrary"))


def _split3(v):
    hi = v.astype(BF16)
    r = v - hi.astype(F32)
    mid = r.astype(BF16)
    lo = (r - mid.astype(F32)).astype(BF16)
    return hi, mid, lo


def _fgate_fwd(f, bf_pad, tb):
    S = f.shape[0]

    def body(f_ref, b_ref, fc_ref, carry):
        @pl.when(pl.program_id(0) == 0)
        def _():
            carry[...] = jnp.zeros_like(carry)

        z = f_ref[...] + b_ref[...]
        ls = jnp.minimum(z, 0.0) - jnp.log(1.0 + jnp.exp(-jnp.abs(z)))
        r = lax.broadcasted_iota(jnp.int32, (tb, tb), 0)
        c = lax.broadcasted_iota(jnp.int32, (tb, tb), 1)
        tri = (c <= r).astype(BF16)
        hi, mid, lo = _split3(ls)
        cs = _dot(tri, hi) + _dot(tri, mid) + _dot(tri, lo) + carry[...]
        fc_ref[...] = cs
        carry[...] = cs[tb - 1:tb, :]

    return pl.pallas_call(
        body, name="fgate_fwd", grid=(S // tb,),
        in_specs=[pl.BlockSpec((tb, 128), lambda i: (i, 0)), pl.BlockSpec((1, 128), lambda i: (0, 0))],
        out_specs=pl.BlockSpec((tb, 128), lambda i: (i, 0)),
        out_shape=jax.ShapeDtypeStruct((S, 128), F32),
        scratch_shapes=[pltpu.VMEM((1, 128), F32)],
        compiler_params=_params(("arbitrary",)),
    )(f, bf_pad)


def _fgate_bwd(dfc, f, bf_pad, tb):
    S = f.shape[0]
    nb = S // tb

    def body(d_ref, f_ref, b_ref, df_ref, gb_ref, carry):
        @pl.when(pl.program_id(0) == 0)
        def _():
            carry[...] = jnp.zeros_like(carry)
            gb_ref[...] = jnp.zeros_like(gb_ref)

        r = lax.broadcasted_iota(jnp.int32, (tb, tb), 0)
        c = lax.broadcasted_iota(jnp.int32, (tb, tb), 1)
        tri = (c >= r).astype(BF16)
        hi, mid, lo = _split3(d_ref[...])
        rs = _dot(tri, hi) + _dot(tri, mid) + _dot(tri, lo) + carry[...]
        carry[...] = rs[0:1, :]
        z = f_ref[...] + b_ref[...]
        df = rs * (1.0 / (1.0 + jnp.exp(z)))
        df_ref[...] = df
        gb_ref[0:1, :] += _colsum(df)

    return pl.pallas_call(
        body, name="fgate_bwd", grid=(nb,),
        in_specs=[pl.BlockSpec((tb, 128), lambda i: (nb - 1 - i, 0)),
                  pl.BlockSpec((tb, 128), lambda i: (nb - 1 - i, 0)),
                  pl.BlockSpec((1, 128), lambda i: (0, 0))],
        out_specs=[pl.BlockSpec((tb, 128), lambda i: (nb - 1 - i, 0)), pl.BlockSpec((8, 128), lambda i: (0, 0))],
        out_shape=[jax.ShapeDtypeStruct((S, 128), F32), jax.ShapeDtypeStruct((8, 128), F32)],
        scratch_shapes=[pltpu.VMEM((1, 128), F32)],
        compiler_params=_params(("arbitrary",)),
    )(dfc, f, bf_pad)


def _split2(v):
    hi = v.astype(BF16)
    lo = (v - hi.astype(F32)).astype(BF16)
    return hi, lo


def _head_masks():
    lane = lax.broadcasted_iota(jnp.int32, (1, PAIR), 1)
    m0 = lane < HEAD_DIM
    return m0, jnp.logical_not(m0)


def _sel(mask, v):
    return jnp.where(mask, v, jnp.zeros_like(v))


def _softplus(z):
    return jnp.maximum(z, 0.0) + jnp.log(1.0 + jnp.exp(-jnp.abs(z)))


def _qkv_specs(S, tq, n_pairs, base):
    return [pl.BlockSpec((tq, PAIR), lambda p, i: (i, base + p)),
            pl.BlockSpec((S, PAIR), lambda p, i: (0, base + n_pairs + p)),
            pl.BlockSpec((S, PAIR), lambda p, i: (0, base + 2 * n_pairs + p))]


NEG = -1e30


def _fox_specs(S, tq, n_pairs, base):
    return _qkv_specs(S, tq, n_pairs, base) + [
        pl.BlockSpec((tq, PAIR), lambda p, i: (i, p)),
        pl.BlockSpec((1, S // tq, 8, tq), lambda p, i: (p, 0, 0, 0))]


RC = 32
VANISH = -104.0


def _chunks(n_rows, fn):
    for ci in range(n_rows // RC):
        fn(ci * RC)


def _wide(v, tk):
    return v if tk == 128 else jnp.tile(v, (1, tk // 128))


def _rep(col):
    return jnp.broadcast_to(col, (col.shape[0], 128))


def _per_head(blk, masks):
    sw = pltpu.roll(blk, HEAD_DIM, axis=1)
    return jnp.where(masks[0], blk, sw), jnp.where(masks[0], sw, blk)


def _fill_masked(dst_ref, src_ref, masks, mul=None, ones_lane=None):
    v = src_ref[...]
    if mul is not None:
        v = v * mul
    lane = lax.broadcasted_iota(jnp.int32, (1, PAIR), 1)
    for h in range(2):
        m = _sel(masks[h], v)
        if ones_lane is not None:
            m = jnp.where(lane == ones_lane[h], jnp.ones_like(m), m)
        dst_ref[h] = m


def _tri(tk, cmp):
    r = lax.broadcasted_iota(jnp.int32, (tk, tk), 0)
    c = lax.broadcasted_iota(jnp.int32, (tk, tk), 1)
    return cmp(r, c).astype(BF16)


def _diag_mask(r0, tk, strict):
    row = r0 + lax.broadcasted_iota(jnp.int32, (RC, tk), 0)
    col = lax.broadcasted_iota(jnp.int32, (RC, tk), 1)
    return (col < row) if strict else (col <= row)


def _peer_copies(src_ref, land_ref, send_sems, recv_sems, scatter, receive_side):
    x, y, c = lax.axis_index("x"), lax.axis_index("y"), lax.axis_index("c")
    me = 4 * x + 2 * y + c
    copies = []
    for k in range(1, N_DEV):
        px, py, pc = (1 - x if k & 4 else x), (1 - y if k & 2 else y), (1 - c if k & 1 else c)
        slot = 4 * px + 2 * py + pc
        copies.append(pltpu.make_async_remote_copy(
            src_ref=src_ref.at[slot] if scatter else src_ref,
            dst_ref=land_ref.at[slot] if receive_side else land_ref.at[me],
            send_sem=send_sems.at[k - 1], recv_sem=recv_sems.at[k - 1], device_id=(px, py, pc), device_id_type=MESH))
    return copies


def _call_carrying(body, exchange, name, grid, in_specs, out_specs, out_shape, scratch_shapes, args,
                   semantics=("parallel", "arbitrary")):
    if exchange is None:
        return pl.pallas_call(body, name=name, grid=grid, in_specs=in_specs, out_specs=out_specs, out_shape=out_shape,
                              scratch_shapes=scratch_shapes, compiler_params=_params(semantics))(*args)
    exchanges = [exchange] if isinstance(exchange, tuple) else list(exchange)
    n_in, n_out, n_ex = len(in_specs), len(out_specs), len(exchanges)

    def carrying(*refs):
        srcs = refs[n_in:n_in + n_ex]
        lands = refs[n_in + n_ex + n_out:n_in + 2 * n_ex + n_out]
        sems = refs[len(refs) - 2 * n_ex:]

        def copies(receive_side):
            return [cp for e, (_, scatter) in enumerate(exchanges)
                    for cp in _peer_copies(srcs[e], lands[e], sems[2 * e], sems[2 * e + 1], scatter, receive_side)]

        first = jnp.logical_and(pl.program_id(0) == 0, pl.program_id(1) == 0)
        last = jnp.logical_and(pl.program_id(0) == grid[0] - 1, pl.program_id(1) == grid[1] - 1)

        @pl.when(first)
        def _():
            for cp in copies(False):
                cp.start()

        body(*refs[:n_in], *refs[n_in + n_ex:n_in + n_ex + n_out], *refs[n_in + 2 * n_ex + n_out:len(refs) - 2 * n_ex])

        @pl.when(last)
        def _():
            for cp in copies(True):
                cp.wait_send()
                cp.wait_recv()

    any_space = pl.BlockSpec(memory_space=pl.ANY)
    lands = [jax.ShapeDtypeStruct((N_DEV,) + src.shape[-2:], src.dtype) for src, _ in exchanges]
    return pl.pallas_call(
        carrying, name=name, grid=grid, in_specs=list(in_specs) + [any_space] * n_ex,
        out_specs=list(out_specs) + [any_space] * n_ex, out_shape=list(out_shape) + lands,
        scratch_shapes=list(scratch_shapes) + [pltpu.SemaphoreType.DMA((N_DEV - 1,))] * (2 * n_ex),
        compiler_params=_params(("arbitrary", "arbitrary")))(*args, *[src for src, _ in exchanges])


def _staggered(bodies):
    active, waiting = [], list(bodies)
    while waiting or active:
        if waiting:
            active.append(waiting.pop(0))
        for g in list(active):
            try:
                next(g)
            except StopIteration:
                active.remove(g)


def _streams(tile, j, diag, slot):
    return [tile(j, diag, slot, (0, 1))]


def _tiles(i, tile):
    def step(jj, carry):
        _staggered(_streams(tile, 2 * jj, False, 0) + _streams(tile, 2 * jj + 1, False, 1))
        return carry
    lax.fori_loop(0, (i - 1) // 2, step, 0)

    @pl.when(jnp.logical_and(i >= 1, (i - 1) % 2 == 1))
    def _():
        _staggered(_streams(tile, i - 2, False, 0))

    @pl.when(i >= 1)
    def _():
        _staggered(_streams(tile, i - 1, False, 0) + _streams(tile, i, True, 1))

    @pl.when(i == 0)
    def _():
        _staggered(_streams(tile, i, True, 1))


def _tiles_reversed(i, tile, keep_going):
    @pl.when(i == 0)
    def _():
        _staggered(_streams(tile, i, True, 0))

    @pl.when(i >= 1)
    def _():
        _staggered(_streams(tile, i, True, 0) + _streams(tile, i - 1, False, 1))

    pairs = (i - 1) // 2

    def cond(carry):
        jj, go = carry
        return jnp.logical_and(jj < pairs, go)

    def step(carry):
        jj, _ = carry
        _staggered(_streams(tile, i - 2 - 2 * jj, False, 0) + _streams(tile, i - 3 - 2 * jj, False, 1))
        return jj + 1, keep_going(jnp.maximum(i - 4 - 2 * jj, 0))

    jj, go = lax.while_loop(cond, step, (jnp.int32(0), keep_going(jnp.maximum(i - 2, 0))))

    @pl.when(jnp.logical_and(jnp.logical_and(i >= 1, (i - 1) % 2 == 1), jnp.logical_and(jj == pairs, go)))
    def _():
        _staggered(_streams(tile, 0, False, 0))


def _sb_fwd(qkv, n_pairs, base, tq, exchange=None):
    S = qkv.shape[0]
    tk = tq
    scale = HEAD_DIM ** -0.5

    def body(q_ref, k_ref, v_ref, o_ref, t_ref, z_ref, hi_ref, suf_ref, p_ref, r_ref, acc_ref, vm_ref):
        i = pl.program_id(1)
        masks = _head_masks()

        @pl.when(i == 0)
        def _():
            _fill_masked(vm_ref, v_ref, masks)

        q2 = q_ref[...] * scale
        qm = [_sel(m, q2) for m in masks]
        incl = _tri(tk, lambda r, c: r >= c)
        r_ref[...] = jnp.zeros_like(r_ref)
        acc_ref[...] = jnp.zeros_like(acc_ref)

        def tile(j, diag, slot, heads):
            off = pl.multiple_of(j * tk, tk)
            k2 = k_ref[pl.ds(off, tk), :]
            v2 = v_ref[pl.ds(off, tk), :]
            for h in heads:
                z_ref[2 * slot + h] = _dot(qm[h], k2, 1, 1)
            yield
            for h in heads:
                def split(r0, h=h):
                    rows = pl.ds(r0, RC)
                    lg = -_softplus(z_ref[2 * slot + h, rows, :])
                    if diag:
                        lg = jnp.where(_diag_mask(r0, tk, True), lg, 0.0)
                    hi_ref[2 * slot + h, rows, :] = lg.astype(BF16)
                _chunks(tq, split)
            yield
            for h in heads:
                suf_ref[2 * slot + h] = _dot(hi_ref[2 * slot + h], incl)
            yield
            for h in heads:
                def weights(r0, h=h):
                    rows = pl.ds(r0, RC)
                    a = jnp.exp(z_ref[2 * slot + h, rows, :] + suf_ref[2 * slot + h, rows, :] + _wide(r_ref[h, rows, :], tk))
                    if diag:
                        a = jnp.where(_diag_mask(r0, tk, True), a, 0.0)
                    p_ref[2 * slot + h, rows, :] = a.astype(BF16)
                _chunks(tq, weights)
            yield
            keys = pl.ds(off, tk)
            for h in heads:
                acc_ref[...] += _dot(p_ref[2 * slot + h], vm_ref[h, keys, :])
            for h in heads:
                r_ref[h] += _rep(suf_ref[2 * slot + h, :, 0:1])

        _tiles_reversed(i, tile, lambda nearest: jnp.max(r_ref[...]) >= VANISH)
        o_ref[...] = acc_ref[...].astype(BF16)
        t_ref[...] = acc_ref[...]

    W = n_pairs * PAIR
    return _call_carrying(
        body, exchange, "sb_fwd", (n_pairs, S // tq), _qkv_specs(S, tq, n_pairs, base),
        [pl.BlockSpec((tq, PAIR), lambda p, i: (i, p)), pl.BlockSpec((tq, PAIR), lambda p, i: (i, p))],
        [jax.ShapeDtypeStruct((S, W), BF16), jax.ShapeDtypeStruct((S, W), F32)],
        [pltpu.VMEM((4, tq, tk), F32), pltpu.VMEM((4, tq, tk), BF16),
         pltpu.VMEM((4, tq, tk), F32), pltpu.VMEM((4, tq, tk), BF16), pltpu.VMEM((2, tq, 128), F32),
         pltpu.VMEM((tq, PAIR), F32), pltpu.VMEM((2, S, PAIR), BF16)],
        (qkv, qkv, qkv))


def _sb_bwd(qkv, o32, do, n_pairs, base, tq, exchange=None):
    S = qkv.shape[0]
    tk = tq
    nq = S // tq
    scale = HEAD_DIM ** -0.5

    def body(q_ref, k_ref, v_ref, o_ref, do_ref, dq_ref, dk_ref, dv_ref,
             z_ref, g_ref, omb_ref, cum_ref, hi_ref, lo_ref, a_ref, dz_ref,
             r_ref, cg_ref, dl_ref, dq_acc, dk_acc, dv_acc, ks_ref):
        i = pl.program_id(1)
        masks = _head_masks()

        @pl.when(i == 0)
        def _():
            dk_acc[...] = jnp.zeros_like(dk_acc)
            dv_acc[...] = jnp.zeros_like(dv_acc)
            _fill_masked(ks_ref, k_ref, masks, mul=scale)

        q2 = q_ref[...] * scale
        do2 = do_ref[...]
        qm = [_sel(m, q2) for m in masks]
        dom = [_sel(m, do2) for m in masks]
        prod = do2.astype(F32) * o_ref[...]
        for h in range(2):
            dl_ref[h] = _rep(jnp.sum(jnp.where(masks[h], prod, 0.0), axis=-1, keepdims=True))
        suffix = _tri(tk, lambda r, c: r >= c)
        r_ref[...] = jnp.zeros_like(r_ref)
        cg_ref[...] = jnp.zeros_like(cg_ref)
        dq_acc[...] = jnp.zeros_like(dq_acc)

        def tile(j, diag, slot, heads):
            off = pl.multiple_of(j * tk, tk)
            k2 = k_ref[pl.ds(off, tk), :]
            v2 = v_ref[pl.ds(off, tk), :]
            for h in heads:
                z_ref[2 * slot + h] = _dot(qm[h], k2, 1, 1)
                g_ref[2 * slot + h] = _dot(dom[h], v2, 1, 1)
            yield
            for h in heads:
                def split(r0, h=h):
                    rows = pl.ds(r0, RC)
                    sp = _softplus(z_ref[2 * slot + h, rows, :])
                    omb_ref[2 * slot + h, rows, :] = jnp.exp(-sp)
                    lg = -sp
                    if diag:
                        lg = jnp.where(_diag_mask(r0, tk, True), lg, 0.0)
                    hi_ref[2 * slot + h, rows, :] = lg.astype(BF16)
                _chunks(tq, split)
            yield
            for h in heads:
                cum_ref[2 * slot + h] = _dot(hi_ref[2 * slot + h], suffix)
            yield
            for h in heads:
                def weights(r0, h=h):
                    rows = pl.ds(r0, RC)
                    a = jnp.exp(z_ref[2 * slot + h, rows, :] + cum_ref[2 * slot + h, rows, :] + _wide(r_ref[h, rows, :], tk))
                    if diag:
                        a = jnp.where(_diag_mask(r0, tk, True), a, 0.0)
                    ab = a.astype(BF16)
                    g = g_ref[2 * slot + h, rows, :] * ab.astype(F32)
                    g_ref[2 * slot + h, rows, :] = g
                    a_ref[2 * slot + h, rows, :] = ab
                    hi, lo = _split2(g)
                    hi_ref[2 * slot + h, rows, :] = hi
                    lo_ref[2 * slot + h, rows, :] = lo
                _chunks(tq, weights)
            for h in heads:
                r_ref[h] += _rep(cum_ref[2 * slot + h, :, 0:1])
            yield
            for h in heads:
                cum_ref[2 * slot + h] = _dot(hi_ref[2 * slot + h], suffix) + _dot(lo_ref[2 * slot + h], suffix)
            yield
            for h in heads:
                def dscore(r0, h=h):
                    rows = pl.ds(r0, RC)
                    g = g_ref[2 * slot + h, rows, :]
                    from_here = cum_ref[2 * slot + h, rows, :] + _wide(cg_ref[h, rows, :], tk)
                    before = _wide(dl_ref[h, rows, :], tk) - from_here
                    omb = omb_ref[2 * slot + h, rows, :]
                    dz = g * omb - (1.0 - omb) * before
                    if diag:
                        dz = jnp.where(_diag_mask(r0, tk, True), dz, 0.0)
                    dz_ref[2 * slot + h, rows, :] = dz.astype(BF16)
                _chunks(tq, dscore)
            for h in heads:
                cg_ref[h] += _rep(cum_ref[2 * slot + h, :, 0:1])
            yield
            keys = pl.ds(off, tk)
            for h in heads:
                dq_acc[...] += _dot(dz_ref[2 * slot + h], ks_ref[h, keys, :])
                dk_acc[keys, :] += _dot(dz_ref[2 * slot + h], qm[h], 0, 0)
                dv_acc[keys, :] += _dot(a_ref[2 * slot + h], dom[h], 0, 0)

        _tiles_reversed(i, tile, lambda nearest: jnp.max(r_ref[...]) >= VANISH)
        dq_ref[...] = dq_acc[...].astype(BF16)

        @pl.when(i == nq - 1)
        def _():
            dk_ref[...] = dk_acc[...].astype(BF16)
            dv_ref[...] = dv_acc[...].astype(BF16)

    W = n_pairs * PAIR
    in_specs = _qkv_specs(S, tq, n_pairs, base) + [
        pl.BlockSpec((tq, PAIR), lambda p, i: (i, p)),
        pl.BlockSpec((tq, PAIR), lambda p, i: (i, p))]
    out_specs = [pl.BlockSpec((tq, PAIR), lambda p, i: (i, p)),
                 pl.BlockSpec((S, PAIR), lambda p, i: (0, p)),
                 pl.BlockSpec((S, PAIR), lambda p, i: (0, p))]
    big, stat = (4, tq, tk), (2, tq, 128)
    return _call_carrying(
        body, exchange, "sb_bwd", (n_pairs, nq), in_specs, out_specs, [jax.ShapeDtypeStruct((S, W), BF16)] * 3,
        [pltpu.VMEM(big, F32)] * 4 + [pltpu.VMEM(big, BF16)] * 4 + [pltpu.VMEM(stat, F32)] * 3
        + [pltpu.VMEM((tq, PAIR), F32), pltpu.VMEM((S, PAIR), F32), pltpu.VMEM((S, PAIR), F32),
           pltpu.VMEM((2, S, PAIR), BF16)],
        (qkv, qkv, qkv, o32, do))


def _fox_fwd(qkv, fcx, fcr, n_pairs, base, tq, exchange=None):
    S = qkv.shape[0]
    tk = tq
    scale = HEAD_DIM ** -0.5
    spare = (HEAD_DIM, 0)

    def body(q_ref, k_ref, v_ref, fq_ref, fk_ref, o_ref, lse_ref, s_ref, p_ref, m_ref, al_ref, fqr_ref, acc_ref, vm_ref):
        i = pl.program_id(1)
        masks = _head_masks()

        @pl.when(i == 0)
        def _():
            _fill_masked(vm_ref, v_ref, masks, ones_lane=spare)

        q2 = q_ref[...] * scale
        qm = [_sel(m, q2) for m in masks]
        f0, f1 = _per_head(fq_ref[...], masks)
        fqr_ref[0] = f0
        fqr_ref[1] = f1
        m_ref[...] = jnp.full(m_ref.shape, NEG, F32)
        acc_ref[...] = jnp.zeros_like(acc_ref)

        def tile(j, diag, slot, heads):
            off = pl.multiple_of(j * tk, tk)
            k2 = k_ref[pl.ds(off, tk), :]
            v2 = v_ref[pl.ds(off, tk), :]
            fk2 = fk_ref[0, j]
            for h in heads:
                s_ref[2 * slot + h] = _dot(qm[h], k2, 1, 1)
            yield
            for h in heads:
                fk_row = fk2[h:h + 1, :]

                def probs(r0, h=h, fk_row=fk_row):
                    rows = pl.ds(r0, RC)
                    sv = s_ref[2 * slot + h, rows, :] - fk_row
                    if diag:
                        sv = jnp.where(_diag_mask(r0, tk, False), sv, NEG)
                    fq = fqr_ref[h, rows, :]
                    m_prev = m_ref[h, rows, :]
                    m_new = jnp.maximum(m_prev, jnp.max(sv, axis=-1, keepdims=True) + fq)
                    p_ref[2 * slot + h, rows, :] = jnp.exp(sv + _wide(fq - m_new, tk)).astype(BF16)
                    al_ref[2 * slot + h, rows, :] = jnp.exp(m_prev - m_new)
                    m_ref[h, rows, :] = m_new
                _chunks(tq, probs)
            yield
            for h in heads:
                acc_ref[h] = acc_ref[h] * al_ref[2 * slot + h] + _dot(p_ref[2 * slot + h], vm_ref[h, pl.ds(off, tk), :])

        _tiles(i, tile)
        a0, a1 = acc_ref[0], acc_ref[1]
        l0 = _rep(a0[:, spare[0]:spare[0] + 1])
        l1 = _rep(a1[:, spare[1]:spare[1] + 1])
        o_ref[...] = jnp.where(masks[0], a0 / l0, a1 / l1).astype(BF16)
        lse_ref[...] = jnp.where(masks[0], m_ref[0] + jnp.log(l0), m_ref[1] + jnp.log(l1))

    W = n_pairs * PAIR
    return _call_carrying(
        body, exchange, "fox_fwd", (n_pairs, S // tq), _fox_specs(S, tq, n_pairs, base),
        [pl.BlockSpec((tq, PAIR), lambda p, i: (i, p)), pl.BlockSpec((tq, PAIR), lambda p, i: (i, p))],
        [jax.ShapeDtypeStruct((S, W), BF16), jax.ShapeDtypeStruct((S, W), F32)],
        [pltpu.VMEM((4, tq, tk), F32), pltpu.VMEM((4, tq, tk), BF16), pltpu.VMEM((2, tq, 128), F32),
         pltpu.VMEM((4, tq, 128), F32), pltpu.VMEM((2, tq, 128), F32), pltpu.VMEM((2, tq, 128), F32),
         pltpu.VMEM((2, S, PAIR), BF16)],
        (qkv, qkv, qkv, fcx, fcr))


def _fox_bwd(qkv, fcx, fcr, o, lse, do, n_pairs, base, tq, exchange=None):
    S = qkv.shape[0]
    tk = tq
    nq = S // tq
    scale = HEAD_DIM ** -0.5

    def body(q_ref, k_ref, v_ref, fq_ref, fk_ref, o_ref, lse_ref, do_ref,
             dq_ref, dk_ref, dv_ref, dfq_ref, dfk_ref,
             s_ref, dp_ref, p_ref, ds_ref, row_ref, dl_ref, dfq_acc, col_ref, dq_acc, dk_acc, dv_acc, ks_ref):
        i = pl.program_id(1)
        masks = _head_masks()

        @pl.when(i == 0)
        def _():
            dk_acc[...] = jnp.zeros_like(dk_acc)
            dv_acc[...] = jnp.zeros_like(dv_acc)
            dfk_ref[...] = jnp.zeros_like(dfk_ref)
            _fill_masked(ks_ref, k_ref, masks, mul=scale)

        q2 = q_ref[...] * scale
        do2 = do_ref[...]
        qm = [_sel(m, q2) for m in masks]
        dom = [_sel(m, do2) for m in masks]
        f0, f1 = _per_head(fq_ref[...], masks)
        l0, l1 = _per_head(lse_ref[...], masks)
        row_ref[0] = f0 - l0
        row_ref[1] = f1 - l1
        prod = do2.astype(F32) * o_ref[...].astype(F32)
        for h in range(2):
            dl_ref[h] = _rep(jnp.sum(jnp.where(masks[h], prod, 0.0), axis=-1, keepdims=True))
        dfq_acc[...] = jnp.zeros_like(dfq_acc)
        dq_acc[...] = jnp.zeros_like(dq_acc)

        def tile(j, diag, slot, heads):
            off = pl.multiple_of(j * tk, tk)
            k2 = k_ref[pl.ds(off, tk), :]
            v2 = v_ref[pl.ds(off, tk), :]
            fk2 = fk_ref[0, j]
            for h in heads:
                s_ref[2 * slot + h] = _dot(qm[h], k2, 1, 1)
                dp_ref[2 * slot + h] = _dot(dom[h], v2, 1, 1)
            yield
            for h in heads:
                col_ref[2 * slot + h] = jnp.zeros((8, tk), F32)
                fk_row = fk2[h:h + 1, :]

                def dscore(r0, h=h, fk_row=fk_row):
                    rows = pl.ds(r0, RC)
                    p = jnp.exp(s_ref[2 * slot + h, rows, :] - fk_row + _wide(row_ref[h, rows, :], tk))
                    if diag:
                        p = jnp.where(_diag_mask(r0, tk, False), p, 0.0)
                    ds = p * (dp_ref[2 * slot + h, rows, :] - _wide(dl_ref[h, rows, :], tk))
                    p_ref[2 * slot + h, rows, :] = p.astype(BF16)
                    ds_ref[2 * slot + h, rows, :] = ds.astype(BF16)
                    dfq_acc[h, rows, :] += _rep(jnp.sum(ds, axis=-1, keepdims=True))
                    col_ref[2 * slot + h] += jnp.sum(ds.reshape(RC // 8, 8, tk), axis=0)
                _chunks(tq, dscore)
            yield
            keys = pl.ds(off, tk)
            for h in heads:
                dq_acc[...] += _dot(ds_ref[2 * slot + h], ks_ref[h, keys, :])
                dk_acc[keys, :] += _dot(ds_ref[2 * slot + h], qm[h], 0, 0)
                dv_acc[keys, :] += _dot(p_ref[2 * slot + h], dom[h], 0, 0)
            for h in heads:
                dfk_ref[0, j, h:h + 1, :] += jnp.sum(col_ref[2 * slot + h], axis=0, keepdims=True)

        _tiles(i, tile)
        dq_ref[...] = dq_acc[...].astype(BF16)
        dfq_ref[...] = jnp.where(masks[0], dfq_acc[0], dfq_acc[1])

        @pl.when(i == nq - 1)
        def _():
            dk_ref[...] = dk_acc[...].astype(BF16)
            dv_ref[...] = dv_acc[...].astype(BF16)

    W = n_pairs * PAIR
    in_specs = _fox_specs(S, tq, n_pairs, base) + [
        pl.BlockSpec((tq, PAIR), lambda p, i: (i, p)),
        pl.BlockSpec((tq, PAIR), lambda p, i: (i, p)),
        pl.BlockSpec((tq, PAIR), lambda p, i: (i, p))]
    out_specs = [pl.BlockSpec((tq, PAIR), lambda p, i: (i, p)),
                 pl.BlockSpec((S, PAIR), lambda p, i: (0, p)),
                 pl.BlockSpec((S, PAIR), lambda p, i: (0, p)),
                 pl.BlockSpec((tq, PAIR), lambda p, i: (i, p)),
                 pl.BlockSpec((1, nq, 8, tk), lambda p, i: (p, 0, 0, 0))]
    return _call_carrying(
        body, exchange, "fox_bwd", (n_pairs, nq), in_specs, out_specs,
        [jax.ShapeDtypeStruct((S, W), BF16)] * 3
        + [jax.ShapeDtypeStruct((S, W), F32), jax.ShapeDtypeStruct((n_pairs, nq, 8, tk), F32)],
        [pltpu.VMEM((4, tq, tk), F32)] * 2 + [pltpu.VMEM((4, tq, tk), BF16)] * 2
        + [pltpu.VMEM((2, tq, 128), F32)] * 3 + [pltpu.VMEM((4, 8, tk), F32)]
        + [pltpu.VMEM((tq, PAIR), F32), pltpu.VMEM((S, PAIR), F32), pltpu.VMEM((S, PAIR), F32),
           pltpu.VMEM((2, S, PAIR), BF16)],
        (qkv, qkv, qkv, fcx, fcr, o, lse, do))


def _col_chunks(n, width=256):
    return [(c, min(width, n - c)) for c in range(0, n, width)]


def _swiglu_fwd(u2, wg, wu):
    S, D = u2.shape
    FF = wg.shape[1]
    tm, tn = _pick(S, (512, 256, 128)), _divisors(FF, 1536)[0]

    def body(u_ref, g_ref, w_ref, a_ref, b_ref, h_ref):
        u = u_ref[...]
        for c, w in _col_chunks(tn):
            cols = slice(c, c + w)
            a = _dot(u, g_ref[:, cols])
            b = _dot(u, w_ref[:, cols])
            a_ref[:, cols] = a.astype(BF16)
            b_ref[:, cols] = b.astype(BF16)
            h_ref[:, cols] = (a / (1.0 + jnp.exp(-a)) * b).astype(BF16)

    spec_o = pl.BlockSpec((tm, tn), lambda i, j: (i, j))
    return pl.pallas_call(
        body, name="swiglu_fwd", grid=(S // tm, FF // tn),
        in_specs=[pl.BlockSpec((tm, D), lambda i, j: (i, 0)),
                  pl.BlockSpec((D, tn), lambda i, j: (0, j)),
                  pl.BlockSpec((D, tn), lambda i, j: (0, j))],
        out_specs=[spec_o] * 3, out_shape=[jax.ShapeDtypeStruct((S, FF), BF16)] * 3,
        compiler_params=_params(("parallel", "parallel")),
    )(u2, wg, wu)


def _swiglu_bwd(dh, wd, a, b):
    S, D = dh.shape
    FF = wd.shape[0]
    tm, tn = _pick(S, (512, 256, 128)), _divisors(FF, 1536)[0]

    def body(dh_ref, w_ref, a_ref, b_ref, da_ref, db_ref):
        dh_blk = dh_ref[...]
        for c, w in _col_chunks(tn):
            cols = slice(c, c + w)
            dhin = _dot(dh_blk, w_ref[cols, :], 1, 1)
            av = a_ref[:, cols].astype(F32)
            bv = b_ref[:, cols].astype(F32)
            sig = 1.0 / (1.0 + jnp.exp(-av))
            da_ref[:, cols] = (dhin * bv * (sig * (1.0 + av * (1.0 - sig)))).astype(BF16)
            db_ref[:, cols] = (dhin * (av * sig)).astype(BF16)

    spec_o = pl.BlockSpec((tm, tn), lambda i, j: (i, j))
    return pl.pallas_call(
        body, name="swiglu_bwd", grid=(S // tm, FF // tn),
        in_specs=[pl.BlockSpec((tm, D), lambda i, j: (i, 0)),
                  pl.BlockSpec((tn, D), lambda i, j: (j, 0)), spec_o, spec_o],
        out_specs=[spec_o] * 2, out_shape=[jax.ShapeDtypeStruct((S, FF), BF16)] * 2,
        compiler_params=_params(("parallel", "parallel")),
    )(dh, wd, a, b)


def _gate_bwd(dmix, wo, g_sb, g_fx, y_sb, y_fx):
    S, D = dmix.shape
    tm, tn = _pick(S, (512, 256, 128)), _divisors(D, 1024)[0]

    def body(dm_ref, w_ref, gs_ref, gf_ref, ys_ref, yf_ref, dys_ref, dyf_ref, dls_ref, dlf_ref, bs_ref, bf_ref):
        @pl.when(pl.program_id(1) == 0)
        def _():
            bs_ref[...] = jnp.zeros_like(bs_ref)
            bf_ref[...] = jnp.zeros_like(bf_ref)

        dm_blk = dm_ref[...]
        for c, w in _col_chunks(tn):
            cols = slice(c, c + w)
            dmi = _dot(dm_blk, w_ref[cols, :], 1, 1)
            gs, gf = gs_ref[:, cols], gf_ref[:, cols]
            dys_ref[:, cols] = (dmi * gs).astype(BF16)
            dyf_ref[:, cols] = (dmi * gf).astype(BF16)
            dls = dmi * ys_ref[:, cols] * gs * (1.0 - gs)
            dlf = dmi * yf_ref[:, cols] * gf * (1.0 - gf)
            dls_ref[:, cols] = dls.astype(BF16)
            dlf_ref[:, cols] = dlf.astype(BF16)
            bs_ref[0:1, cols] += _colsum(dls)
            bf_ref[0:1, cols] += _colsum(dlf)

    t = pl.BlockSpec((tm, tn), lambda j, i: (i, j))
    accs = pl.BlockSpec((8, tn), lambda j, i: (0, j))
    return pl.pallas_call(
        body, name="gate_bwd", grid=(D // tn, S // tm),
        in_specs=[pl.BlockSpec((tm, D), lambda j, i: (i, 0)),
                  pl.BlockSpec((tn, D), lambda j, i: (j, 0)), t, t, t, t],
        out_specs=[t, t, t, t, accs, accs],
        out_shape=[jax.ShapeDtypeStruct((S, D), BF16)] * 4 + [jax.ShapeDtypeStruct((8, D), F32)] * 2,
        compiler_params=_params(("parallel", "arbitrary")),
    )(dmix, wo, g_sb, g_fx, y_sb, y_fx)


def _local_step(x, target, ada8, lnp8, bg_sb, bg_fx, bf_pad, wqkv, wf, wgs, wgf, gather, later_weights, pack_early,
                pack_last):
    S, D = x.shape
    W = wqkv.shape[1] // 6
    n_pairs = W // PAIR
    n_heads = W // HEAD_DIM
    ts = _pick(S, (512, 256, 128))
    tq = _pick(S, (256, 128))

    u1 = _ln_mod(x, ada8, ts)
    qkv = _mm([(u1, wqkv)], 'nn', BF16, "in_qkv")
    f = _mm([(u1, wf)], 'nn', F32, "in_f")
    g_sb = _mm([(u1, wgs)], 'nn', F32, "in_gsb", bias=bg_sb, act='sigmoid')
    g_fx = _mm([(u1, wgf)], 'nn', F32, "in_gfx", bias=bg_fx, act='sigmoid')
    fc = _fgate_fwd(f, bf_pad, _pick(S, (512, 256, 128)))
    fch = fc[:, :n_heads]
    fcx = jnp.repeat(fch, HEAD_DIM, axis=1)
    nq = S // tq
    fcr = jnp.pad(fch.T.reshape(n_pairs, 2, nq, tq).transpose(0, 2, 1, 3),
                  ((0, 0), (0, 0), (0, 6), (0, 0)))
    o_sb, o_sb32, *zone_a = _sb_fwd(qkv, n_pairs, 0, tq, gather[0])
    o_fx, lse, *zone_b = _fox_fwd(qkv, fcx, fcr, n_pairs, 3 * n_pairs, tq, gather[1])
    wsb, wfx, wo, wfg, wfu, wfd = later_weights(*zone_a, *zone_b)
    y_sb =_mm([(o_sb, wsb)], 'nn', F32, "out_sb")
    y_fx = _mm([(o_fx, wfx)], 'nn', F32, "out_fx")
    mix_in, mix, x1, u2 = _gate_mix_out(g_sb, g_fx, y_sb, y_fx, x, wo, ada8, lnp8, ts)
    a, b, hin = _swiglu_fwd(u2, wfg, wfu)
    dr2, dh, st_loss = _loss_head(x1, hin, wfd, target, ada8, lnp8, ts)

    da, db = _swiglu_bwd(dh, wfd, a, b)
    g_wfd = _mm([(hin, dh)], 'tn', F32, "g_ffn_down")
    g_wfg = _mm([(u2, da)], 'tn', F32, "g_ffn_gate")
    g_wfu = _mm([(u2, db)], 'tn', F32, "g_ffn_up")
    dr1, dmix, st_mid = _mid_bwd(da, db, wfg, wfu, x1, dr2, mix, x, ada8, lnp8, _pick(S, (256, 128)))
    dys, dyf, dls, dlf, gb_sb, gb_fx = _gate_bwd(dmix, wo, g_sb, g_fx, y_sb, y_fx)
    g_wo = _mm([(mix_in, dmix)], 'tn', F32, "g_w_o")
    do_sb = _mm([(dys, wsb)], 'nt', BF16, "d_o_sb")
    do_fx = _mm([(dyf, wfx)], 'nt', BF16, "d_o_fx")
    g_wsb = _mm([(o_sb, dys)], 'tn', F32, "g_sb_out")
    g_wfx = _mm([(o_fx, dyf)], 'tn', F32, "g_fox_out")
    scatter = pack_early(dict(sb=g_wsb, fx=g_wfx, o=g_wo, fg=g_wfg, fu=g_wfu, fd=g_wfd))
    dq_s, dk_s, dv_s, *zone_a = _sb_bwd(qkv, o_sb32, do_sb, n_pairs, 0, tq, scatter[0])
    dq_f, dk_f, dv_f, dfq, dfk, *zone_b = _fox_bwd(qkv, fcx, fcr, o_fx, lse, do_fx, n_pairs, 3 * n_pairs, tq,
                                                    scatter[1])
    early = [(scatter[0], zone_a[0] if zone_a else None), (scatter[1], zone_b[0] if zone_b else None)]
    dfc = dfq[:, ::HEAD_DIM] - dfk[:, :, :2, :].transpose(0, 2, 1, 3).reshape(n_heads, S).T
    dfc = jnp.pad(dfc, ((0, 0), (0, 128 - n_heads)))
    df, gb_f = _fgate_bwd(dfc, f, bf_pad, _pick(S, (512, 256, 128)))
    dqkv = jnp.concatenate([dq_s, dk_s, dv_s, dq_f, dk_f, dv_f], axis=1)
    g_wqkv = [_mm([(u1, dqkv)], 'tn', F32, "g_in_qkv")]
    g_wf = _mm([(u1, df)], 'tn', F32, "g_in_f")
    g_wgs = _mm([(u1, dls)], 'tn', F32, "g_in_gsb")
    g_wgf = _mm([(u1, dlf)], 'tn', F32, "g_in_gfx")
    wgrads = dict(qkv=g_wqkv, f=g_wf, gs=g_wgs, gf=g_wgf)
    last = pack_last(wgrads)
    gx, st_first, *last_zone = _first_bwd(
        [(dqkv, wqkv), (df, wf), (dls, wgs), (dlf, wgf)], x, dr1, ada8, last)
    last_zone = last_zone[0] if last_zone else None

    stats = dict(loss=st_loss, mid=st_mid, first=st_first, gb_sb=gb_sb, gb_fx=gb_fx, gb_f=gb_f)
    return gx, wgrads, stats, early, (last, last_zone)


def _position():
    x, y, c = lax.axis_index("x"), lax.axis_index("y"), lax.axis_index("c")
    return x, y, c, 4 * x + 2 * y + c


def _flip(x, y, c, k):
    px = 1 - x if k & 4 else x
    py = 1 - y if k & 2 else y
    pc = 1 - c if k & 1 else c
    return (px, py, pc), 4 * px + 2 * py + pc


def _all_gather_small(v, name):
    r, n = v.shape

    def body(x_ref, out_ref, send_sems, recv_sems, local_sem):
        x, y, c, me = _position()
        mine = pltpu.make_async_copy(x_ref, out_ref.at[me], local_sem)
        mine.start()
        sends = []
        for k in range(1, N_DEV):
            peer, _ = _flip(x, y, c, k)
            cp = pltpu.make_async_remote_copy(
                src_ref=x_ref, dst_ref=out_ref.at[me], send_sem=send_sems.at[k - 1], recv_sem=recv_sems.at[k - 1],
                device_id=peer, device_id_type=MESH)
            cp.start()
            sends.append(cp)
        for k in range(1, N_DEV):
            peer, slot = _flip(x, y, c, k)
            pltpu.make_async_remote_copy(
                src_ref=x_ref, dst_ref=out_ref.at[slot], send_sem=send_sems.at[k - 1], recv_sem=recv_sems.at[k - 1],
                device_id=peer, device_id_type=MESH).wait_recv()
        for cp in sends:
            cp.wait_send()
        mine.wait()

    return pl.pallas_call(
        body, name=name, out_shape=jax.ShapeDtypeStruct((N_DEV, r, n), v.dtype),
        in_specs=[pl.BlockSpec(memory_space=pltpu.VMEM)], out_specs=pl.BlockSpec(memory_space=pltpu.VMEM),
        scratch_shapes=[pltpu.SemaphoreType.DMA((N_DEV - 1,)), pltpu.SemaphoreType.DMA((N_DEV - 1,)),
                        pltpu.SemaphoreType.DMA],
    )(v)


def _all_gather_weights(packed):
    R, C = packed.shape

    def body(x_ref, out_ref, send_sems, recv_sems, local_sem):
        x, y, c, me = _position()
        sibling, sib_slot = _flip(x, y, c, 1)
        mine = pltpu.make_async_copy(x_ref, out_ref.at[me], local_sem)
        mine.start()

        def copy(k, slot, to, src=None):
            return pltpu.make_async_remote_copy(
                src_ref=out_ref.at[slot] if src is None else src, dst_ref=out_ref.at[slot],
                send_sem=send_sems.at[k], recv_sem=recv_sems.at[k], device_id=to, device_id_type=MESH)

        first = [copy(0, me, sibling, src=x_ref)]
        chips = (4, 2, 6)
        for n, k in enumerate(chips):
            peer, _ = _flip(x, y, c, k)
            first.append(copy(1 + n, me, peer, src=x_ref))
        for cp in first:
            cp.start()
        passed = []
        for n, k in enumerate(chips):
            peer, slot = _flip(x, y, c, k)
            copy(1 + n, slot, peer).wait_recv()
            cp = copy(4 + n, slot, sibling)
            cp.start()
            passed.append(cp)
        copy(0, sib_slot, sibling).wait_recv()
        for n, k in enumerate(chips):
            _, slot = _flip(x, y, c, k | 1)
            copy(4 + n, slot, sibling).wait_recv()
        for cp in first + passed:
            cp.wait_send()
        mine.wait()

    return pl.pallas_call(
        body, name="all_gather_weights", out_shape=jax.ShapeDtypeStruct((N_DEV, R, C), packed.dtype),
        in_specs=[pl.BlockSpec(memory_space=pl.ANY)], out_specs=pl.BlockSpec(memory_space=pl.ANY),
        scratch_shapes=[pltpu.SemaphoreType.DMA((7,)), pltpu.SemaphoreType.DMA((7,)), pltpu.SemaphoreType.DMA],
    )(packed)


def _own_slot(land, own):
    me = 4 * lax.axis_index("x") + 2 * lax.axis_index("y") + lax.axis_index("c")
    return lax.dynamic_update_slice(land, own[None], (me, 0, 0))


def _sum_slots(recv, name, tr):
    n, R, C = recv.shape

    def body(r_ref, o_ref):
        acc = r_ref[0].astype(F32)
        for s in range(1, n):
            acc = acc + r_ref[s].astype(F32)
        o_ref[...] = acc

    return pl.pallas_call(
        body, name=name, grid=(R // tr,), in_specs=[pl.BlockSpec((n, tr, C), lambda i: (0, i, 0))],
        out_specs=pl.BlockSpec((tr, C), lambda i: (i, 0)), out_shape=jax.ShapeDtypeStruct((R, C), F32),
        compiler_params=_params(("parallel",)),
    )(recv)


def _sum_stats(st_all, loss_row):
    n, r, D = st_all.shape

    def body(s_ref, o_ref, l_ref):
        acc = s_ref[0]
        for d in range(1, n):
            acc = acc + s_ref[d]
        o_ref[...] = acc
        l_ref[...] = jnp.zeros((8, 128), F32) + jnp.sum(acc[loss_row:loss_row + 1, :], axis=-1, keepdims=True)

    return pl.pallas_call(
        body, name="sum_stats", out_shape=[jax.ShapeDtypeStruct((r, D), F32), jax.ShapeDtypeStruct((8, 128), F32)],
    )(st_all)


def _adamw(w, g, m, v, name):
    R, C = w.shape
    tr = _pick(R, (256, 176, 128, 64, 32, 16, 8))
    c1 = 1.0 / (1.0 - ADAM_B1 ** ADAM_STEP)
    c2 = 1.0 / (1.0 - ADAM_B2 ** ADAM_STEP)

    def body(w_ref, g_ref, m_ref, v_ref, d_ref, nm_ref, nv_ref):
        gv = g_ref[...]
        nm = ADAM_B1 * m_ref[...] + (1.0 - ADAM_B1) * gv
        nv = ADAM_B2 * v_ref[...] + (1.0 - ADAM_B2) * (gv * gv)
        nm_ref[...] = nm
        nv_ref[...] = nv
        d_ref[...] = -ADAM_LR * ((nm * c1) / (jnp.sqrt(nv * c2) + ADAM_EPS) + ADAM_WD * w_ref[...])

    spec = pl.BlockSpec((tr, C), lambda i: (i, 0))
    return pl.pallas_call(
        body, name=name, grid=(R // tr,), in_specs=[spec] * 4, out_specs=[spec] * 3,
        out_shape=[jax.ShapeDtypeStruct((R, C), F32)] * 3, compiler_params=_params(("parallel",)),
    )(w, g, m, v)


def _round16(n):
    return -(-n // 16) * 16


def _pack_layout(D, in_cols, ff, W):
    parts = [("in", D * (in_cols // N_DEV) // D), ("fg", ff // N_DEV), ("fu", ff // N_DEV),
             ("sb", W * (D // N_DEV) // D), ("fx", W * (D // N_DEV) // D), ("o", D // N_DEV), ("fd", ff // N_DEV)]
    layout, off = {}, 0
    for nm, rows in parts:
        layout[nm] = (off, rows)
        off += _round16(rows)
    return layout, off


def _rows_of(a, D, rows):
    a = a.reshape(rows, D)
    return jnp.pad(a, ((0, _round16(rows) - rows), (0, 0)))


def _cols_to_dest(g, D):
    K, N = g.shape
    n = N // N_DEV
    return g.reshape(K, N_DEV, n).transpose(1, 0, 2).reshape(N_DEV, K * n // D, D)


def _cols_from_src(blocks, K, n):
    return blocks.reshape(N_DEV, K, n).transpose(1, 0, 2).reshape(K, N_DEV * n)


def _pad_rows16(a):
    rows = a.shape[1]
    return jnp.pad(a, ((0, 0), (0, _round16(rows) - rows), (0, 0)))


def kernel(x, c, w_ada, b_ada, w_in, b_gate, b_forget, w_sb_out, w_fox_out, w_o, ln1_g, ln1_b, w_ffn_gate, w_ffn_up, w_ffn_down, ln2_g, ln2_b, loss_target, m_w_ada, m_b_ada, m_w_in, m_b_gate, m_b_forget, m_w_sb_out, m_w_fox_out, m_w_o, m_ln1_g, m_ln1_b, m_w_ffn_gate, m_w_ffn_up, m_w_ffn_down, m_ln2_g, m_ln2_b, v_w_ada, v_b_ada, v_w_in, v_b_gate, v_b_forget, v_w_sb_out, v_w_fox_out, v_w_o, v_ln1_g, v_ln1_b, v_w_ffn_gate, v_w_ffn_up, v_w_ffn_down, v_ln2_g, v_ln2_b):
    S, D = x.shape[1], x.shape[2]
    W = w_sb_out.shape[1]
    n_heads = b_forget.shape[1]
    ff = w_ffn_down.shape[1] * N_DEV
    in_loc = w_in.shape[2]
    in_cols = in_loc * N_DEV
    ada_loc = w_ada.shape[2]
    n_cond = ada_loc * N_DEV // D
    assert w_ada.shape[0] == 1 and n_cond == 6 and in_cols == 6 * W + n_heads + 2 * D and n_heads <= 128
    me = 4 * lax.axis_index("x") + 2 * lax.axis_index("y") + lax.axis_index("c")

    c_all = _all_gather_small(c, "gather_c").reshape(N_DEV, D)
    c16 = jnp.pad(c_all, ((0, 16 - N_DEV), (0, 0)))
    b_cols = lax.dynamic_slice(b_ada, (0, me * ada_loc), (1, ada_loc))
    ada_cols = _mm([(c16, w_ada[0])], 'nn', F32, "ada_fwd", bias=b_cols, silu_a=True)[:N_DEV]
    ada_all = _all_gather_small(ada_cols, "gather_ada")
    ada_me = lax.dynamic_index_in_dim(ada_all, me, axis=1, keepdims=False)
    ada8 = jnp.pad(ada_me.reshape(n_cond, D), ((0, 8 - n_cond), (0, 0)))
    lnp8 = jnp.concatenate([ln1_g, ln1_b, ln2_g, ln2_b, jnp.zeros((4, D), F32)], axis=0)

    layout, R = _pack_layout(D, in_cols, ff, W)
    shards = dict(**{"in": w_in[0]}, fg=w_ffn_gate[0], fu=w_ffn_up[0], sb=w_sb_out[0], fx=w_fox_out[0], o=w_o[0],
                  fd=w_ffn_down[0])
    rows_fwd, rows_bwd = ("sb", "fx", "o"), ("fd", "o", "sb", "fx")
    rows_of = {nm: layout[nm][1] for nm in layout}

    def offsets(names):
        offs, off = {}, 0
        for nm in names:
            offs[nm] = off
            off += _round16(rows_of[nm])
        return offs

    def as_rows(names):
        return jnp.concatenate([_rows_of(shards[nm].astype(BF16), D, rows_of[nm]) for nm in names], axis=0)

    def whole_from(blocks):
        return blocks.transpose(1, 0, 2).reshape(blocks.shape[1], N_DEV * blocks.shape[2])

    def blocks_of(g):
        return g.reshape(g.shape[0], N_DEV, g.shape[1] // N_DEV).transpose(1, 0, 2)

    gather_src = [as_rows(rows_fwd),
                  jnp.concatenate([shards["fg"].astype(BF16), shards["fu"].astype(BF16)], axis=0), as_rows(("fd",))]
    gathered_in = _all_gather_weights(shards["in"].astype(BF16))

    w_in_full = whole_from(gathered_in)
    wqkv = w_in_full[:, :6 * W]
    wf = jnp.pad(w_in_full[:, 6 * W:6 * W + n_heads], ((0, 0), (0, 128 - n_heads)))
    wgs = w_in_full[:, 6 * W + n_heads:6 * W + n_heads + D]
    wgf = w_in_full[:, 6 * W + n_heads + D:]
    bf_pad = jnp.pad(b_forget, ((0, 0), (0, 128 - n_heads)))

    def later_weights(zone_rows, zone_gate_up, zone_down):
        rows, offs = _own_slot(zone_rows, gather_src[0]), offsets(rows_fwd)
        part = {nm: rows[:, offs[nm]:offs[nm] + rows_of[nm], :] for nm in rows_fwd}
        gate_up = _own_slot(zone_gate_up, gather_src[1])
        down = _own_slot(zone_down, gather_src[2])[:, :rows_of["fd"], :]
        return (_cols_from_src(part["sb"], W, D // N_DEV), _cols_from_src(part["fx"], W, D // N_DEV),
                part["o"].reshape(D, D), whole_from(gate_up[:, :D, :]), whole_from(gate_up[:, D:, :]),
                down.reshape(ff, D))

    def pack_early(g):
        dest = {"sb": _cols_to_dest(g["sb"], D), "fx": _cols_to_dest(g["fx"], D),
                "o": g["o"].reshape(N_DEV, D // N_DEV, D), "fd": g["fd"].reshape(N_DEV, ff // N_DEV, D)}
        rows = jnp.concatenate([_pad_rows16(dest[nm].astype(BF16)) for nm in rows_bwd], axis=1)
        gate_up = jnp.concatenate([blocks_of(g["fg"].astype(BF16)), blocks_of(g["fu"].astype(BF16))], axis=1)
        return [(rows, True), (gate_up, True)]

    def pack_last(g):
        g_in = jnp.concatenate(g["qkv"] + [g["f"][:, :n_heads], g["gs"], g["gf"]], axis=1)
        return blocks_of(g_in.astype(BF16)), True

    gx, wg, st, early, ((pack_in, _), land_in) = _local_step(
        x[0], loss_target[0], ada8, lnp8, b_gate[:, :D], b_gate[:, D:], bf_pad, wqkv, wf, wgs, wgf,
        [(gather_src[0], False), [(gather_src[1], False), (gather_src[2], False)]],
        later_weights, pack_early, pack_last)

    def summed(zone, sent, name):
        own = lax.dynamic_index_in_dim(sent, me, axis=0, keepdims=False)
        rows = zone.shape[1]
        block = max(t for t in range(16, 705, 16) if rows % t == 0)
        return _sum_slots(_own_slot(zone, own), name, block)

    ((sent_rows, _), zone_rows), ((sent_gate_up, _), zone_gate_up) = early
    gate_up = summed(zone_gate_up, sent_gate_up, "sum_grads_gate_up")
    gsum = {"in": summed(land_in, pack_in, "sum_grads_in"), "fg": gate_up[:D], "fu": gate_up[D:]}
    total, offs = summed(zone_rows, sent_rows, "sum_grads_rows"), offsets(rows_bwd)
    for nm in rows_bwd:
        gsum[nm] = total[offs[nm]:offs[nm] + rows_of[nm]]

    def gshard(nm, shape):
        return gsum[nm].reshape(shape)

    zrow = jnp.zeros((1, D), F32)
    gb_f_row = jnp.pad(st["gb_f"][0:1], ((0, 0), (0, D - 128)))
    stats16 = jnp.concatenate([
        st["first"][1:2], st["first"][0:1], st["mid"][4:5], st["mid"][1:2], st["mid"][0:1], st["loss"][3:4],
        st["mid"][2:3], st["mid"][3:4], st["loss"][1:2], st["loss"][2:3], st["gb_sb"][0:1], st["gb_fx"][0:1],
        st["loss"][0:1], gb_f_row, zrow, zrow], axis=0)
    st_all = _all_gather_small(stats16, "gather_stats")
    st_sum, loss_blk = _sum_stats(st_all, 12)
    loss = loss_blk[0, 0]

    d_ada_all = st_all[:, :n_cond, :].reshape(N_DEV, n_cond * D)
    d_cols = lax.dynamic_slice(d_ada_all, (0, me * ada_loc), (N_DEV, ada_loc))
    d16 = jnp.pad(d_cols, ((0, 16 - N_DEV), (0, 0)))
    g_w_ada = _mm([(c16, d16)], 'tn', F32, "ada_wgrad", silu_a=True)

    small_w = jnp.concatenate([b_ada.reshape(n_cond, D), ln1_g, ln1_b, ln2_g, ln2_b, b_gate.reshape(2, D), zrow,
                               jnp.pad(b_forget, ((0, 0), (0, D - n_heads))), zrow, zrow], axis=0)
    small_m = jnp.concatenate([m_b_ada.reshape(n_cond, D), m_ln1_g, m_ln1_b, m_ln2_g, m_ln2_b, m_b_gate.reshape(2, D),
                               zrow, jnp.pad(m_b_forget, ((0, 0), (0, D - n_heads))), zrow, zrow], axis=0)
    small_v = jnp.concatenate([v_b_ada.reshape(n_cond, D), v_ln1_g, v_ln1_b, v_ln2_g, v_ln2_b, v_b_gate.reshape(2, D),
                               zrow, jnp.pad(v_b_forget, ((0, 0), (0, D - n_heads))), zrow, zrow], axis=0)
    sm = _adamw(small_w, st_sum, small_m, small_v, "adamw_small")

    def small(a, nm):
        if nm == "b_ada":
            return a[0:n_cond].reshape(1, n_cond * D)
        if nm == "b_gate":
            return a[10:12].reshape(1, 2 * D)
        if nm == "b_forget":
            return a[13:14, :n_heads]
        row = {"ln1_g": 6, "ln1_b": 7, "ln2_g": 8, "ln2_b": 9}[nm]
        return a[row:row + 1]

    big = {
        "w_ada": (w_ada[0], g_w_ada, m_w_ada[0], v_w_ada[0]),
        "w_in": (w_in[0], gshard("in", w_in.shape[1:]), m_w_in[0], v_w_in[0]),
        "w_sb_out": (w_sb_out[0], gshard("sb", w_sb_out.shape[1:]), m_w_sb_out[0], v_w_sb_out[0]),
        "w_fox_out": (w_fox_out[0], gshard("fx", w_fox_out.shape[1:]), m_w_fox_out[0], v_w_fox_out[0]),
        "w_o": (w_o[0], gshard("o", w_o.shape[1:]), m_w_o[0], v_w_o[0]),
        "w_ffn_gate": (w_ffn_gate[0], gshard("fg", w_ffn_gate.shape[1:]), m_w_ffn_gate[0], v_w_ffn_gate[0]),
        "w_ffn_up": (w_ffn_up[0], gshard("fu", w_ffn_up.shape[1:]), m_w_ffn_up[0], v_w_ffn_up[0]),
        "w_ffn_down": (w_ffn_down[0], gshard("fd", w_ffn_down.shape[1:]), m_w_ffn_down[0], v_w_ffn_down[0]),
    }
    order = ["w_ada", "b_ada", "w_in", "b_gate", "b_forget", "w_sb_out", "w_fox_out", "w_o", "ln1_g", "ln1_b",
             "w_ffn_gate", "w_ffn_up", "w_ffn_down", "ln2_g", "ln2_b"]
    grads, deltas, new_ms, new_vs = [], [], [], []
    for nm in order:
        if nm in big:
            w, g, m, v = big[nm]
            d, nm_, nv_ = _adamw(w, g, m, v, "adamw_" + nm)
            grads.append(g[None])
            deltas.append(d[None])
            new_ms.append(nm_[None])
            new_vs.append(nv_[None])
        else:
            grads.append(small(st_sum, nm))
            deltas.append(small(sm[0], nm))
            new_ms.append(small(sm[1], nm))
            new_vs.append(small(sm[2], nm))
    return (loss, gx[None], *grads, *deltas, *new_ms, *new_vs)
```

```python
import jax
import jax.numpy as jnp
from jax import lax
from jax.experimental import pallas as pl
from jax.experimental.pallas import tpu as pltpu

F32 = jnp.float32
BF16 = jnp.bfloat16

HEAD_DIM = 64
PAIR = 2 * HEAD_DIM
LN_EPS = 1e-5
ALPHA = 2.0 ** 0.25
ADAM_LR, ADAM_B1, ADAM_B2, ADAM_EPS, ADAM_WD, ADAM_STEP = 0.001, 0.9, 0.999, 1e-08, 0.01, 10
N_DEV = 8
VMEM_LIMIT = 56 * 1024 * 1024
MESH = pl.DeviceIdType.MESH


def _dot(a, b, ca=1, cb=0):
    return lax.dot_general(a, b, (((ca,), (cb,)), ((), ())), preferred_element_type=F32)


def _pick(n, cands):
    for t in cands:
        if n % t == 0:
            return t
    return n


def _params(sem):
    return pltpu.CompilerParams(dimension_semantics=sem, vmem_limit_bytes=VMEM_LIMIT)


MM_BLOCK_BYTES = 40 * 1024 * 1024
LANES = 128


def _divisors(n, cap):
    ds = [d for d in range(LANES, min(n, cap) + 1, LANES) if n % d == 0]
    return sorted(ds, reverse=True) or [n]


def _mm_tiles(M, N, a_row_bytes, b_row_bytes, out_itemsize):
    best = None
    for tm in _divisors(M, 1024):
        for tn in _divisors(N, 2048):
            need = 2 * (tm * a_row_bytes + tn * b_row_bytes + tm * tn * out_itemsize) + tm * tn * 4
            if need <= MM_BLOCK_BYTES and (best is None or (tm * tn, tm) > (best[0] * best[1], best[0])):
                best = (tm, tn)
    assert best is not None, (M, N, a_row_bytes, b_row_bytes)
    return best


def _mm(pairs, mode, out_dtype, name, bias=None, act=None, silu_a=False, exchange=None):
    norm = []
    for p in pairs:
        a, b = p[0], p[1]
        kdim_a = a.shape[0] if mode == 'tn' else a.shape[1]
        K, ka, kb = (p[2], p[3], p[4]) if len(p) > 2 else (kdim_a, 0, 0)
        norm.append((a, b, K, ka, kb))
    a0, b0 = norm[0][0], norm[0][1]
    M = a0.shape[1] if mode == 'tn' else a0.shape[0]
    N = b0.shape[0] if mode == 'nt' else b0.shape[1]
    tm, tn = _mm_tiles(M, N, sum(K * a.dtype.itemsize for a, _, K, _, _ in norm),
                       sum(K * b.dtype.itemsize for _, b, K, _, _ in norm), jnp.dtype(out_dtype).itemsize)
    n_pairs = len(norm)

    in_specs, args = [], []
    for a, b, K, ka, kb in norm:
        if mode == 'tn':
            in_specs.append(pl.BlockSpec((K, tm), lambda i, j, ka=ka: (ka, i)))
        else:
            in_specs.append(pl.BlockSpec((tm, K), lambda i, j, ka=ka: (i, ka)))
        if mode == 'nt':
            in_specs.append(pl.BlockSpec((tn, K), lambda i, j, kb=kb: (j, kb)))
        else:
            in_specs.append(pl.BlockSpec((K, tn), lambda i, j, kb=kb: (kb, j)))
        args += [a, b]
    if bias is not None:
        in_specs.append(pl.BlockSpec((1, tn), lambda i, j: (0, j)))
        args.append(bias)

    ca = 0 if mode == 'tn' else 1
    cb = 1 if mode == 'nt' else 0

    def body(*refs):
        o_ref = refs[-1]
        acc = None
        for p in range(n_pairs):
            av = refs[2 * p][...]
            if silu_a:
                av = av / (1.0 + jnp.exp(-av))
            d = _dot(av.astype(BF16), refs[2 * p + 1][...].astype(BF16), ca, cb)
            acc = d if acc is None else acc + d
        if bias is not None:
            acc = acc + refs[2 * n_pairs][...]
        if act == 'sigmoid':
            acc = 1.0 / (1.0 + jnp.exp(-acc))
        o_ref[...] = acc.astype(out_dtype)

    out_spec = pl.BlockSpec((tm, tn), lambda i, j: (i, j))
    out_shape = jax.ShapeDtypeStruct((M, N), out_dtype)
    if exchange is not None:
        return _call_carrying(body, exchange, name, (M // tm, N // tn), in_specs, [out_spec], [out_shape], [], args)
    return pl.pallas_call(
        body, name=name, grid=(M // tm, N // tn), in_specs=in_specs, out_specs=out_spec, out_shape=out_shape,
        compiler_params=_params(("parallel", "parallel")),
    )(*args)


def _rows_call(body, name, row_ins, vec_ins, row_outs, acc_outs, ts):
    S = row_ins[0].shape[0]
    in_specs = [pl.BlockSpec((ts, a.shape[1]), lambda i: (i, 0)) for a in row_ins]
    in_specs += [pl.BlockSpec(a.shape, lambda i: (0, 0)) for a in vec_ins]
    out_specs = [pl.BlockSpec((ts, c), lambda i: (i, 0)) for c, _ in row_outs]
    out_specs += [pl.BlockSpec(s, lambda i: (0, 0)) for s in acc_outs]
    out_shape = [jax.ShapeDtypeStruct((S, c), dt) for c, dt in row_outs]
    out_shape += [jax.ShapeDtypeStruct(s, F32) for s in acc_outs]
    return pl.pallas_call(
        body, name=name, grid=(S // ts,), in_specs=in_specs, out_specs=out_specs, out_shape=out_shape,
        compiler_params=_params(("arbitrary",)),
    )(*row_ins, *vec_ins)


def _ln_stats(v):
    mu = jnp.mean(v, axis=-1, keepdims=True)
    d = v - mu
    var = jnp.mean(d * d, axis=-1, keepdims=True)
    rstd = lax.rsqrt(var + LN_EPS)
    return d * rstd, rstd


def _ln_bwd(dxhat, xhat, rstd):
    m1 = jnp.mean(dxhat, axis=-1, keepdims=True)
    m2 = jnp.mean(dxhat * xhat, axis=-1, keepdims=True)
    return rstd * (dxhat - m1 - xhat * m2)


def _colsum(v):
    return jnp.sum(v, axis=0, keepdims=True)


def _ln_mod(x, ada8, ts):
    D = x.shape[1]

    def body(x_ref, v_ref, u_ref):
        xhat, _ = _ln_stats(x_ref[...])
        u_ref[...] = (xhat * (1.0 + v_ref[1:2, :]) + v_ref[0:1, :]).astype(BF16)

    return _rows_call(body, "ln_mod", [x], [ada8], [(D, BF16)], [], ts)[0]


def _in_gates(u1, wf, wgs, wgf, bg_sb, bg_fx, ts):
    D = wgs.shape[1]

    def body(u_ref, wf_ref, ws_ref, wx_ref, bs_ref, bx_ref, f_ref, gs_ref, gx_ref):
        u = u_ref[...]
        f_ref[...] = _dot(u, wf_ref[...])
        gs_ref[...] = 1.0 / (1.0 + jnp.exp(-(_dot(u, ws_ref[...]) + bs_ref[...])))
        gx_ref[...] = 1.0 / (1.0 + jnp.exp(-(_dot(u, wx_ref[...]) + bx_ref[...])))

    return _rows_call(body, "in_gates", [u1], [wf, wgs, wgf, bg_sb, bg_fx],
                      [(wf.shape[1], F32), (D, F32), (D, F32)], [], ts)


def _gate_mix_out(g_sb, g_fx, y_sb, y_fx, x, wo, ada8, lnp8, ts):
    D = y_sb.shape[1]

    def body(gs, gf, ys, yf, x_ref, v_ref, p_ref, w_ref, mi_ref, mix_ref, x1_ref, u2_ref):
        mi = (gs[...] * ys[...] + gf[...] * yf[...]).astype(BF16)
        mi_ref[...] = mi
        mix = _dot(mi, w_ref[...])
        mix_ref[...] = mix
        r1 = ALPHA * x_ref[...] + v_ref[2:3, :] * mix
        xhat, _ = _ln_stats(r1)
        x1 = xhat * p_ref[0:1, :] + p_ref[1:2, :]
        x1_ref[...] = x1
        xh1, _ = _ln_stats(x1)
        u2_ref[...] = (xh1 * (1.0 + v_ref[4:5, :]) + v_ref[3:4, :]).astype(BF16)

    return _rows_call(body, "gate_mix_out", [g_sb, g_fx, y_sb, y_fx, x], [ada8, lnp8, wo],
                      [(D, BF16), (D, F32), (D, F32), (D, BF16)], [], ts)


def _loss_head(x1, hin, wfd, target, ada8, lnp8, ts):
    D = x1.shape[1]

    def body(x1_ref, hin_ref, t_ref, v_ref, p_ref, w_ref, dr2_ref, dh_ref, st_ref):
        @pl.when(pl.program_id(0) == 0)
        def _():
            st_ref[...] = jnp.zeros_like(st_ref)

        hv = _dot(hin_ref[...], w_ref[...])
        g2 = v_ref[5:6, :]
        r2 = ALPHA * x1_ref[...] + g2 * hv
        xhat, rstd = _ln_stats(r2)
        y = xhat * p_ref[2:3, :] + p_ref[3:4, :]
        err = y - t_ref[...]
        dy = err * (1.0 / D)
        dr2 = _ln_bwd(dy * p_ref[2:3, :], xhat, rstd)
        dr2_ref[...] = dr2
        dh_ref[...] = (dr2 * g2).astype(BF16)
        st_ref[0:1, :] += _colsum(err * err) * (0.5 / D)
        st_ref[1:2, :] += _colsum(dy * xhat)
        st_ref[2:3, :] += _colsum(dy)
        st_ref[3:4, :] += _colsum(dr2 * hv)

    return _rows_call(body, "loss_head", [x1, hin, target], [ada8, lnp8, wfd], [(D, F32), (D, BF16)], [(8, D)], ts)


def _mid_bwd(da, db, wfg, wfu, x1, dr2, mix, x, ada8, lnp8, ts):
    D = x.shape[1]

    def body(da_ref, db_ref, x1_ref, dr2_ref, mix_ref, x_ref, v_ref, p_ref, wg_ref, wu_ref, dr1_ref, dmix_ref, st_ref):
        @pl.when(pl.program_id(0) == 0)
        def _():
            st_ref[...] = jnp.zeros_like(st_ref)

        du2v = _dot(da_ref[...], wg_ref[...], 1, 1) + _dot(db_ref[...], wu_ref[...], 1, 1)
        xh1, rstd1 = _ln_stats(x1_ref[...])
        dx1 = ALPHA * dr2_ref[...] + _ln_bwd(du2v * (1.0 + v_ref[4:5, :]), xh1, rstd1)
        mixv = mix_ref[...]
        g1 = v_ref[2:3, :]
        r1 = ALPHA * x_ref[...] + g1 * mixv
        xhr, rstdr = _ln_stats(r1)
        dr1 = _ln_bwd(dx1 * p_ref[0:1, :], xhr, rstdr)
        dr1_ref[...] = dr1
        dmix_ref[...] = (dr1 * g1).astype(BF16)
        st_ref[0:1, :] += _colsum(du2v * xh1)
        st_ref[1:2, :] += _colsum(du2v)
        st_ref[2:3, :] += _colsum(dx1 * xhr)
        st_ref[3:4, :] += _colsum(dx1)
        st_ref[4:5, :] += _colsum(dr1 * mixv)

    return _rows_call(body, "mid_bwd", [da, db, x1, dr2, mix, x], [ada8, lnp8, wfg, wfu], [(D, F32), (D, BF16)],
                      [(8, D)], ts)


def _first_bwd(pairs, x, dr1, ada8, exchange):
    S, D = x.shape
    norm = [(p[0], p[1]) + ((p[2], p[3], p[4]) if len(p) > 2 else (p[0].shape[1], 0, 0)) for p in pairs]
    tm = _pick(S, (256, 128))
    in_specs, args = [], []
    for a, b, K, ka, kb in norm:
        in_specs += [pl.BlockSpec((tm, K), lambda i, j, ka=ka: (i, ka)), pl.BlockSpec((D, K), lambda i, j, kb=kb: (0, kb))]
        args += [a, b]
    rows = pl.BlockSpec((tm, D), lambda i, j: (i, 0))
    sums = pl.BlockSpec((8, D), lambda i, j: (0, 0))
    n_pairs = len(norm)

    def body(*refs):
        x_ref, dr1_ref, v_ref, gx_ref, st_ref = refs[2 * n_pairs:]

        @pl.when(pl.program_id(0) == 0)
        def _():
            st_ref[...] = jnp.zeros_like(st_ref)

        du1 = None
        for p in range(n_pairs):
            d = _dot(refs[2 * p][...].astype(BF16), refs[2 * p + 1][...].astype(BF16), 1, 1)
            du1 = d if du1 is None else du1 + d
        xh0, rstd0 = _ln_stats(x_ref[...])
        gx_ref[...] = ALPHA * dr1_ref[...] + _ln_bwd(du1 * (1.0 + v_ref[1:2, :]), xh0, rstd0)
        st_ref[0:1, :] += _colsum(du1 * xh0)
        st_ref[1:2, :] += _colsum(du1)

    return _call_carrying(
        body, exchange, "first_bwd", (S // tm, 1), in_specs + [rows, rows, sums], [rows, sums],
        [jax.ShapeDtypeStruct((S, D), F32), jax.ShapeDtypeStruct((8, D), F32)], [], (*args, x, dr1, ada8),
        semantics=("arbitrary", "arbitrary"))


def _split3(v):
    hi = v.astype(BF16)
    r = v - hi.astype(F32)
    mid = r.astype(BF16)
    lo = (r - mid.astype(F32)).astype(BF16)
    return hi, mid, lo


def _fgate_fwd(f, bf_pad, tb):
    S = f.shape[0]

    def body(f_ref, b_ref, fc_ref, carry):
        @pl.when(pl.program_id(0) == 0)
        def _():
            carry[...] = jnp.zeros_like(carry)

        z = f_ref[...] + b_ref[...]
        ls = jnp.minimum(z, 0.0) - jnp.log(1.0 + jnp.exp(-jnp.abs(z)))
        r = lax.broadcasted_iota(jnp.int32, (tb, tb), 0)
        c = lax.broadcasted_iota(jnp.int32, (tb, tb), 1)
        tri = (c <= r).astype(BF16)
        hi, mid, lo = _split3(ls)
        cs = _dot(tri, hi) + _dot(tri, mid) + _dot(tri, lo) + carry[...]
        fc_ref[...] = cs
        carry[...] = cs[tb - 1:tb, :]

    return pl.pallas_call(
        body, name="fgate_fwd", grid=(S // tb,),
        in_specs=[pl.BlockSpec((tb, 128), lambda i: (i, 0)), pl.BlockSpec((1, 128), lambda i: (0, 0))],
        out_specs=pl.BlockSpec((tb, 128), lambda i: (i, 0)),
        out_shape=jax.ShapeDtypeStruct((S, 128), F32),
        scratch_shapes=[pltpu.VMEM((1, 128), F32)],
        compiler_params=_params(("arbitrary",)),
    )(f, bf_pad)


def _fgate_bwd(dfc, f, bf_pad, tb):
    S = f.shape[0]
    nb = S // tb

    def body(d_ref, f_ref, b_ref, df_ref, gb_ref, carry):
        @pl.when(pl.program_id(0) == 0)
        def _():
            carry[...] = jnp.zeros_like(carry)
            gb_ref[...] = jnp.zeros_like(gb_ref)

        r = lax.broadcasted_iota(jnp.int32, (tb, tb), 0)
        c = lax.broadcasted_iota(jnp.int32, (tb, tb), 1)
        tri = (c >= r).astype(BF16)
        hi, mid, lo = _split3(d_ref[...])
        rs = _dot(tri, hi) + _dot(tri, mid) + _dot(tri, lo) + carry[...]
        carry[...] = rs[0:1, :]
        z = f_ref[...] + b_ref[...]
        df = rs * (1.0 / (1.0 + jnp.exp(z)))
        df_ref[...] = df
        gb_ref[0:1, :] += _colsum(df)

    return pl.pallas_call(
        body, name="fgate_bwd", grid=(nb,),
        in_specs=[pl.BlockSpec((tb, 128), lambda i: (nb - 1 - i, 0)),
                  pl.BlockSpec((tb, 128), lambda i: (nb - 1 - i, 0)),
                  pl.BlockSpec((1, 128), lambda i: (0, 0))],
        out_specs=[pl.BlockSpec((tb, 128), lambda i: (nb - 1 - i, 0)), pl.BlockSpec((8, 128), lambda i: (0, 0))],
        out_shape=[jax.ShapeDtypeStruct((S, 128), F32), jax.ShapeDtypeStruct((8, 128), F32)],
        scratch_shapes=[pltpu.VMEM((1, 128), F32)],
        compiler_params=_params(("arbitrary",)),
    )(dfc, f, bf_pad)


def _split2(v):
    hi = v.astype(BF16)
    lo = (v - hi.astype(F32)).astype(BF16)
    return hi, lo


def _head_masks():
    lane = lax.broadcasted_iota(jnp.int32, (1, PAIR), 1)
    m0 = lane < HEAD_DIM
    return m0, jnp.logical_not(m0)


def _sel(mask, v):
    return jnp.where(mask, v, jnp.zeros_like(v))


def _softplus(z):
    return jnp.maximum(z, 0.0) + jnp.log(1.0 + jnp.exp(-jnp.abs(z)))


def _qkv_specs(S, tq, n_pairs, base):
    return [pl.BlockSpec((tq, PAIR), lambda p, i: (i, base + p)),
            pl.BlockSpec((S, PAIR), lambda p, i: (0, base + n_pairs + p)),
            pl.BlockSpec((S, PAIR), lambda p, i: (0, base + 2 * n_pairs + p))]


NEG = -1e30


def _fox_specs(S, tq, n_pairs, base):
    return _qkv_specs(S, tq, n_pairs, base) + [
        pl.BlockSpec((tq, PAIR), lambda p, i: (i, p)),
        pl.BlockSpec((1, S // tq, 8, tq), lambda p, i: (p, 0, 0, 0))]


RC = 32
VANISH = -104.0


def _chunks(n_rows, fn):
    for ci in range(n_rows // RC):
        fn(ci * RC)


def _wide(v, tk):
    return v if tk == 128 else jnp.tile(v, (1, tk // 128))


def _rep(col):
    return jnp.broadcast_to(col, (col.shape[0], 128))


def _per_head(blk, masks):
    sw = pltpu.roll(blk, HEAD_DIM, axis=1)
    return jnp.where(masks[0], blk, sw), jnp.where(masks[0], sw, blk)


def _fill_masked(dst_ref, src_ref, masks, mul=None, ones_lane=None):
    v = src_ref[...]
    if mul is not None:
        v = v * mul
    lane = lax.broadcasted_iota(jnp.int32, (1, PAIR), 1)
    for h in range(2):
        m = _sel(masks[h], v)
        if ones_lane is not None:
            m = jnp.where(lane == ones_lane[h], jnp.ones_like(m), m)
        dst_ref[h] = m


def _tri(tk, cmp):
    r = lax.broadcasted_iota(jnp.int32, (tk, tk), 0)
    c = lax.broadcasted_iota(jnp.int32, (tk, tk), 1)
    return cmp(r, c).astype(BF16)


def _diag_mask(r0, tk, strict):
    row = r0 + lax.broadcasted_iota(jnp.int32, (RC, tk), 0)
    col = lax.broadcasted_iota(jnp.int32, (RC, tk), 1)
    return (col < row) if strict else (col <= row)


def _peer_copies(src_ref, land_ref, send_sems, recv_sems, scatter, receive_side):
    x, y, c = lax.axis_index("x"), lax.axis_index("y"), lax.axis_index("c")
    me = 4 * x + 2 * y + c
    copies = []
    for k in range(1, N_DEV):
        px, py, pc = (1 - x if k & 4 else x), (1 - y if k & 2 else y), (1 - c if k & 1 else c)
        slot = 4 * px + 2 * py + pc
        copies.append(pltpu.make_async_remote_copy(
            src_ref=src_ref.at[slot] if scatter else src_ref,
            dst_ref=land_ref.at[slot] if receive_side else land_ref.at[me],
            send_sem=send_sems.at[k - 1], recv_sem=recv_sems.at[k - 1], device_id=(px, py, pc), device_id_type=MESH))
    return copies


def _call_carrying(body, exchange, name, grid, in_specs, out_specs, out_shape, scratch_shapes, args,
                   semantics=("parallel", "arbitrary")):
    if exchange is None:
        return pl.pallas_call(body, name=name, grid=grid, in_specs=in_specs, out_specs=out_specs, out_shape=out_shape,
                              scratch_shapes=scratch_shapes, compiler_params=_params(semantics))(*args)
    exchanges = [exchange] if isinstance(exchange, tuple) else list(exchange)
    n_in, n_out, n_ex = len(in_specs), len(out_specs), len(exchanges)

    def carrying(*refs):
        srcs = refs[n_in:n_in + n_ex]
        lands = refs[n_in + n_ex + n_out:n_in + 2 * n_ex + n_out]
        sems = refs[len(refs) - 2 * n_ex:]

        def copies(receive_side):
            return [cp for e, (_, scatter) in enumerate(exchanges)
                    for cp in _peer_copies(srcs[e], lands[e], sems[2 * e], sems[2 * e + 1], scatter, receive_side)]

        first = jnp.logical_and(pl.program_id(0) == 0, pl.program_id(1) == 0)
        last = jnp.logical_and(pl.program_id(0) == grid[0] - 1, pl.program_id(1) == grid[1] - 1)

        @pl.when(first)
        def _():
            for cp in copies(False):
                cp.start()

        body(*refs[:n_in], *refs[n_in + n_ex:n_in + n_ex + n_out], *refs[n_in + 2 * n_ex + n_out:len(refs) - 2 * n_ex])

        @pl.when(last)
        def _():
            for cp in copies(True):
                cp.wait_send()
                cp.wait_recv()

    any_space = pl.BlockSpec(memory_space=pl.ANY)
    lands = [jax.ShapeDtypeStruct((N_DEV,) + src.shape[-2:], src.dtype) for src, _ in exchanges]
    return pl.pallas_call(
        carrying, name=name, grid=grid, in_specs=list(in_specs) + [any_space] * n_ex,
        out_specs=list(out_specs) + [any_space] * n_ex, out_shape=list(out_shape) + lands,
        scratch_shapes=list(scratch_shapes) + [pltpu.SemaphoreType.DMA((N_DEV - 1,))] * (2 * n_ex),
        compiler_params=_params(("arbitrary", "arbitrary")))(*args, *[src for src, _ in exchanges])


def _staggered(bodies):
    active, waiting = [], list(bodies)
    while waiting or active:
        if waiting:
            active.append(waiting.pop(0))
        for g in list(active):
            try:
                next(g)
            except StopIteration:
                active.remove(g)


def _streams(tile, j, diag, slot):
    return [tile(j, diag, slot, (0, 1))]


def _tiles(i, tile):
    def step(jj, carry):
        _staggered(_streams(tile, 2 * jj, False, 0) + _streams(tile, 2 * jj + 1, False, 1))
        return carry
    lax.fori_loop(0, (i - 1) // 2, step, 0)

    @pl.when(jnp.logical_and(i >= 1, (i - 1) % 2 == 1))
    def _():
        _staggered(_streams(tile, i - 2, False, 0))

    @pl.when(i >= 1)
    def _():
        _staggered(_streams(tile, i - 1, False, 0) + _streams(tile, i, True, 1))

    @pl.when(i == 0)
    def _():
        _staggered(_streams(tile, i, True, 1))


def _tiles_reversed(i, tile, keep_going):
    @pl.when(i == 0)
    def _():
        _staggered(_streams(tile, i, True, 0))

    @pl.when(i >= 1)
    def _():
        _staggered(_streams(tile, i, True, 0) + _streams(tile, i - 1, False, 1))

    pairs = (i - 1) // 2

    def cond(carry):
        jj, go = carry
        return jnp.logical_and(jj < pairs, go)

    def step(carry):
        jj, _ = carry
        _staggered(_streams(tile, i - 2 - 2 * jj, False, 0) + _streams(tile, i - 3 - 2 * jj, False, 1))
        return jj + 1, keep_going(jnp.maximum(i - 4 - 2 * jj, 0))

    jj, go = lax.while_loop(cond, step, (jnp.int32(0), keep_going(jnp.maximum(i - 2, 0))))

    @pl.when(jnp.logical_and(jnp.logical_and(i >= 1, (i - 1) % 2 == 1), jnp.logical_and(jj == pairs, go)))
    def _():
        _staggered(_streams(tile, 0, False, 0))


def _sb_fwd(qkv, n_pairs, base, tq, exchange=None):
    S = qkv.shape[0]
    tk = tq
    scale = HEAD_DIM ** -0.5

    def body(q_ref, k_ref, v_ref, o_ref, t_ref, z_ref, hi_ref, suf_ref, p_ref, r_ref, acc_ref, vm_ref):
        i = pl.program_id(1)
        masks = _head_masks()

        @pl.when(i == 0)
        def _():
            _fill_masked(vm_ref, v_ref, masks)

        q2 = q_ref[...] * scale
        qm = [_sel(m, q2) for m in masks]
        incl = _tri(tk, lambda r, c: r >= c)
        r_ref[...] = jnp.zeros_like(r_ref)
        acc_ref[...] = jnp.zeros_like(acc_ref)

        def tile(j, diag, slot, heads):
            off = pl.multiple_of(j * tk, tk)
            k2 = k_ref[pl.ds(off, tk), :]
            v2 = v_ref[pl.ds(off, tk), :]
            for h in heads:
                z_ref[2 * slot + h] = _dot(qm[h], k2, 1, 1)
            yield
            for h in heads:
                def split(r0, h=h):
                    rows = pl.ds(r0, RC)
                    lg = -_softplus(z_ref[2 * slot + h, rows, :])
                    if diag:
                        lg = jnp.where(_diag_mask(r0, tk, True), lg, 0.0)
                    hi_ref[2 * slot + h, rows, :] = lg.astype(BF16)
                _chunks(tq, split)
            yield
            for h in heads:
                suf_ref[2 * slot + h] = _dot(hi_ref[2 * slot + h], incl)
            yield
            for h in heads:
                def weights(r0, h=h):
                    rows = pl.ds(r0, RC)
                    a = jnp.exp(z_ref[2 * slot + h, rows, :] + suf_ref[2 * slot + h, rows, :] + _wide(r_ref[h, rows, :], tk))
                    if diag:
                        a = jnp.where(_diag_mask(r0, tk, True), a, 0.0)
                    p_ref[2 * slot + h, rows, :] = a.astype(BF16)
                _chunks(tq, weights)
            yield
            keys = pl.ds(off, tk)
            for h in heads:
                acc_ref[...] += _dot(p_ref[2 * slot + h], vm_ref[h, keys, :])
            for h in heads:
                r_ref[h] += _rep(suf_ref[2 * slot + h, :, 0:1])

        _tiles_reversed(i, tile, lambda nearest: jnp.max(r_ref[...]) >= VANISH)
        o_ref[...] = acc_ref[...].astype(BF16)
        t_ref[...] = acc_ref[...]

    W = n_pairs * PAIR
    return _call_carrying(
        body, exchange, "sb_fwd", (n_pairs, S // tq), _qkv_specs(S, tq, n_pairs, base),
        [pl.BlockSpec((tq, PAIR), lambda p, i: (i, p)), pl.BlockSpec((tq, PAIR), lambda p, i: (i, p))],
        [jax.ShapeDtypeStruct((S, W), BF16), jax.ShapeDtypeStruct((S, W), F32)],
        [pltpu.VMEM((4, tq, tk), F32), pltpu.VMEM((4, tq, tk), BF16),
         pltpu.VMEM((4, tq, tk), F32), pltpu.VMEM((4, tq, tk), BF16), pltpu.VMEM((2, tq, 128), F32),
         pltpu.VMEM((tq, PAIR), F32), pltpu.VMEM((2, S, PAIR), BF16)],
        (qkv, qkv, qkv))


def _sb_bwd(qkv, o32, do, n_pairs, base, tq, exchange=None):
    S = qkv.shape[0]
    tk = tq
    nq = S // tq
    scale = HEAD_DIM ** -0.5

    def body(q_ref, k_ref, v_ref, o_ref, do_ref, dq_ref, dk_ref, dv_ref,
             z_ref, g_ref, omb_ref, cum_ref, hi_ref, lo_ref, a_ref, dz_ref,
             r_ref, cg_ref, dl_ref, dq_acc, dk_acc, dv_acc, ks_ref):
        i = pl.program_id(1)
        masks = _head_masks()

        @pl.when(i == 0)
        def _():
            dk_acc[...] = jnp.zeros_like(dk_acc)
            dv_acc[...] = jnp.zeros_like(dv_acc)
            _fill_masked(ks_ref, k_ref, masks, mul=scale)

        q2 = q_ref[...] * scale
        do2 = do_ref[...]
        qm = [_sel(m, q2) for m in masks]
        dom = [_sel(m, do2) for m in masks]
        prod = do2.astype(F32) * o_ref[...]
        for h in range(2):
            dl_ref[h] = _rep(jnp.sum(jnp.where(masks[h], prod, 0.0), axis=-1, keepdims=True))
        suffix = _tri(tk, lambda r, c: r >= c)
        r_ref[...] = jnp.zeros_like(r_ref)
        cg_ref[...] = jnp.zeros_like(cg_ref)
        dq_acc[...] = jnp.zeros_like(dq_acc)

        def tile(j, diag, slot, heads):
            off = pl.multiple_of(j * tk, tk)
            k2 = k_ref[pl.ds(off, tk), :]
            v2 = v_ref[pl.ds(off, tk), :]
            for h in heads:
                z_ref[2 * slot + h] = _dot(qm[h], k2, 1, 1)
                g_ref[2 * slot + h] = _dot(dom[h], v2, 1, 1)
            yield
            for h in heads:
                def split(r0, h=h):
                    rows = pl.ds(r0, RC)
                    sp = _softplus(z_ref[2 * slot + h, rows, :])
                    omb_ref[2 * slot + h, rows, :] = jnp.exp(-sp)
                    lg = -sp
                    if diag:
                        lg = jnp.where(_diag_mask(r0, tk, True), lg, 0.0)
                    hi_ref[2 * slot + h, rows, :] = lg.astype(BF16)
                _chunks(tq, split)
            yield
            for h in heads:
                cum_ref[2 * slot + h] = _dot(hi_ref[2 * slot + h], suffix)
            yield
            for h in heads:
                def weights(r0, h=h):
                    rows = pl.ds(r0, RC)
                    a = jnp.exp(z_ref[2 * slot + h, rows, :] + cum_ref[2 * slot + h, rows, :] + _wide(r_ref[h, rows, :], tk))
                    if diag:
                        a = jnp.where(_diag_mask(r0, tk, True), a, 0.0)
                    ab = a.astype(BF16)
                    g = g_ref[2 * slot + h, rows, :] * ab.astype(F32)
                    g_ref[2 * slot + h, rows, :] = g
                    a_ref[2 * slot + h, rows, :] = ab
                    hi, lo = _split2(g)
                    hi_ref[2 * slot + h, rows, :] = hi
                    lo_ref[2 * slot + h, rows, :] = lo
                _chunks(tq, weights)
            for h in heads:
                r_ref[h] += _rep(cum_ref[2 * slot + h, :, 0:1])
            yield
            for h in heads:
                cum_ref[2 * slot + h] = _dot(hi_ref[2 * slot + h], suffix) + _dot(lo_ref[2 * slot + h], suffix)
            yield
            for h in heads:
                def dscore(r0, h=h):
                    rows = pl.ds(r0, RC)
                    g = g_ref[2 * slot + h, rows, :]
                    from_here = cum_ref[2 * slot + h, rows, :] + _wide(cg_ref[h, rows, :], tk)
                    before = _wide(dl_ref[h, rows, :], tk) - from_here
                    omb = omb_ref[2 * slot + h, rows, :]
                    dz = g * omb - (1.0 - omb) * before
                    if diag:
                        dz = jnp.where(_diag_mask(r0, tk, True), dz, 0.0)
                    dz_ref[2 * slot + h, rows, :] = dz.astype(BF16)
                _chunks(tq, dscore)
            for h in heads:
                cg_ref[h] += _rep(cum_ref[2 * slot + h, :, 0:1])
            yield
            keys = pl.ds(off, tk)
            for h in heads:
                dq_acc[...] += _dot(dz_ref[2 * slot + h], ks_ref[h, keys, :])
                dk_acc[keys, :] += _dot(dz_ref[2 * slot + h], qm[h], 0, 0)
                dv_acc[keys, :] += _dot(a_ref[2 * slot + h], dom[h], 0, 0)

        _tiles_reversed(i, tile, lambda nearest: jnp.max(r_ref[...]) >= VANISH)
        dq_ref[...] = dq_acc[...].astype(BF16)

        @pl.when(i == nq - 1)
        def _():
            dk_ref[...] = dk_acc[...].astype(BF16)
            dv_ref[...] = dv_acc[...].astype(BF16)

    W = n_pairs * PAIR
    in_specs = _qkv_specs(S, tq, n_pairs, base) + [
        pl.BlockSpec((tq, PAIR), lambda p, i: (i, p)),
        pl.BlockSpec((tq, PAIR), lambda p, i: (i, p))]
    out_specs = [pl.BlockSpec((tq, PAIR), lambda p, i: (i, p)),
                 pl.BlockSpec((S, PAIR), lambda p, i: (0, p)),
                 pl.BlockSpec((S, PAIR), lambda p, i: (0, p))]
    big, stat = (4, tq, tk), (2, tq, 128)
    return _call_carrying(
        body, exchange, "sb_bwd", (n_pairs, nq), in_specs, out_specs, [jax.ShapeDtypeStruct((S, W), BF16)] * 3,
        [pltpu.VMEM(big, F32)] * 4 + [pltpu.VMEM(big, BF16)] * 4 + [pltpu.VMEM(stat, F32)] * 3
        + [pltpu.VMEM((tq, PAIR), F32), pltpu.VMEM((S, PAIR), F32), pltpu.VMEM((S, PAIR), F32),
           pltpu.VMEM((2, S, PAIR), BF16)],
        (qkv, qkv, qkv, o32, do))


def _fox_fwd(qkv, fcx, fcr, n_pairs, base, tq, exchange=None):
    S = qkv.shape[0]
    tk = tq
    scale = HEAD_DIM ** -0.5
    spare = (HEAD_DIM, 0)

    def body(q_ref, k_ref, v_ref, fq_ref, fk_ref, o_ref, lse_ref, s_ref, p_ref, m_ref, al_ref, fqr_ref, acc_ref, vm_ref):
        i = pl.program_id(1)
        masks = _head_masks()

        @pl.when(i == 0)
        def _():
            _fill_masked(vm_ref, v_ref, masks, ones_lane=spare)

        q2 = q_ref[...] * scale
        qm = [_sel(m, q2) for m in masks]
        f0, f1 = _per_head(fq_ref[...], masks)
        fqr_ref[0] = f0
        fqr_ref[1] = f1
        m_ref[...] = jnp.full(m_ref.shape, NEG, F32)
        acc_ref[...] = jnp.zeros_like(acc_ref)

        def tile(j, diag, slot, heads):
            off = pl.multiple_of(j * tk, tk)
            k2 = k_ref[pl.ds(off, tk), :]
            v2 = v_ref[pl.ds(off, tk), :]
            fk2 = fk_ref[0, j]
            for h in heads:
                s_ref[2 * slot + h] = _dot(qm[h], k2, 1, 1)
            yield
            for h in heads:
                fk_row = fk2[h:h + 1, :]

                def probs(r0, h=h, fk_row=fk_row):
                    rows = pl.ds(r0, RC)
                    sv = s_ref[2 * slot + h, rows, :] - fk_row
                    if diag:
                        sv = jnp.where(_diag_mask(r0, tk, False), sv, NEG)
                    fq = fqr_ref[h, rows, :]
                    m_prev = m_ref[h, rows, :]
                    m_new = jnp.maximum(m_prev, jnp.max(sv, axis=-1, keepdims=True) + fq)
                    p_ref[2 * slot + h, rows, :] = jnp.exp(sv + _wide(fq - m_new, tk)).astype(BF16)
                    al_ref[2 * slot + h, rows, :] = jnp.exp(m_prev - m_new)
                    m_ref[h, rows, :] = m_new
                _chunks(tq, probs)
            yield
            for h in heads:
                acc_ref[h] = acc_ref[h] * al_ref[2 * slot + h] + _dot(p_ref[2 * slot + h], vm_ref[h, pl.ds(off, tk), :])

        _tiles(i, tile)
        a0, a1 = acc_ref[0], acc_ref[1]
        l0 = _rep(a0[:, spare[0]:spare[0] + 1])
        l1 = _rep(a1[:, spare[1]:spare[1] + 1])
        o_ref[...] = jnp.where(masks[0], a0 / l0, a1 / l1).astype(BF16)
        lse_ref[...] = jnp.where(masks[0], m_ref[0] + jnp.log(l0), m_ref[1] + jnp.log(l1))

    W = n_pairs * PAIR
    return _call_carrying(
        body, exchange, "fox_fwd", (n_pairs, S // tq), _fox_specs(S, tq, n_pairs, base),
        [pl.BlockSpec((tq, PAIR), lambda p, i: (i, p)), pl.BlockSpec((tq, PAIR), lambda p, i: (i, p))],
        [jax.ShapeDtypeStruct((S, W), BF16), jax.ShapeDtypeStruct((S, W), F32)],
        [pltpu.VMEM((4, tq, tk), F32), pltpu.VMEM((4, tq, tk), BF16), pltpu.VMEM((2, tq, 128), F32),
         pltpu.VMEM((4, tq, 128), F32), pltpu.VMEM((2, tq, 128), F32), pltpu.VMEM((2, tq, 128), F32),
         pltpu.VMEM((2, S, PAIR), BF16)],
        (qkv, qkv, qkv, fcx, fcr))


def _fox_bwd(qkv, fcx, fcr, o, lse, do, n_pairs, base, tq, exchange=None):
    S = qkv.shape[0]
    tk = tq
    nq = S // tq
    scale = HEAD_DIM ** -0.5

    def body(q_ref, k_ref, v_ref, fq_ref, fk_ref, o_ref, lse_ref, do_ref,
             dq_ref, dk_ref, dv_ref, dfq_ref, dfk_ref,
             s_ref, dp_ref, p_ref, ds_ref, row_ref, dl_ref, dfq_acc, col_ref, dq_acc, dk_acc, dv_acc, ks_ref):
        i = pl.program_id(1)
        masks = _head_masks()

        @pl.when(i == 0)
        def _():
            dk_acc[...] = jnp.zeros_like(dk_acc)
            dv_acc[...] = jnp.zeros_like(dv_acc)
            dfk_ref[...] = jnp.zeros_like(dfk_ref)
            _fill_masked(ks_ref, k_ref, masks, mul=scale)

        q2 = q_ref[...] * scale
        do2 = do_ref[...]
        qm = [_sel(m, q2) for m in masks]
        dom = [_sel(m, do2) for m in masks]
        f0, f1 = _per_head(fq_ref[...], masks)
        l0, l1 = _per_head(lse_ref[...], masks)
        row_ref[0] = f0 - l0
        row_ref[1] = f1 - l1
        prod = do2.astype(F32) * o_ref[...].astype(F32)
        for h in range(2):
            dl_ref[h] = _rep(jnp.sum(jnp.where(masks[h], prod, 0.0), axis=-1, keepdims=True))
        dfq_acc[...] = jnp.zeros_like(dfq_acc)
        dq_acc[...] = jnp.zeros_like(dq_acc)

        def tile(j, diag, slot, heads):
            off = pl.multiple_of(j * tk, tk)
            k2 = k_ref[pl.ds(off, tk), :]
            v2 = v_ref[pl.ds(off, tk), :]
            fk2 = fk_ref[0, j]
            for h in heads:
                s_ref[2 * slot + h] = _dot(qm[h], k2, 1, 1)
                dp_ref[2 * slot + h] = _dot(dom[h], v2, 1, 1)
            yield
            for h in heads:
                col_ref[2 * slot + h] = jnp.zeros((8, tk), F32)
                fk_row = fk2[h:h + 1, :]

                def dscore(r0, h=h, fk_row=fk_row):
                    rows = pl.ds(r0, RC)
                    p = jnp.exp(s_ref[2 * slot + h, rows, :] - fk_row + _wide(row_ref[h, rows, :], tk))
                    if diag:
                        p = jnp.where(_diag_mask(r0, tk, False), p, 0.0)
                    ds = p * (dp_ref[2 * slot + h, rows, :] - _wide(dl_ref[h, rows, :], tk))
                    p_ref[2 * slot + h, rows, :] = p.astype(BF16)
                    ds_ref[2 * slot + h, rows, :] = ds.astype(BF16)
                    dfq_acc[h, rows, :] += _rep(jnp.sum(ds, axis=-1, keepdims=True))
                    col_ref[2 * slot + h] += jnp.sum(ds.reshape(RC // 8, 8, tk), axis=0)
                _chunks(tq, dscore)
            yield
            keys = pl.ds(off, tk)
            for h in heads:
                dq_acc[...] += _dot(ds_ref[2 * slot + h], ks_ref[h, keys, :])
                dk_acc[keys, :] += _dot(ds_ref[2 * slot + h], qm[h], 0, 0)
                dv_acc[keys, :] += _dot(p_ref[2 * slot + h], dom[h], 0, 0)
            for h in heads:
                dfk_ref[0, j, h:h + 1, :] += jnp.sum(col_ref[2 * slot + h], axis=0, keepdims=True)

        _tiles(i, tile)
        dq_ref[...] = dq_acc[...].astype(BF16)
        dfq_ref[...] = jnp.where(masks[0], dfq_acc[0], dfq_acc[1])

        @pl.when(i == nq - 1)
        def _():
            dk_ref[...] = dk_acc[...].astype(BF16)
            dv_ref[...] = dv_acc[...].astype(BF16)

    W = n_pairs * PAIR
    in_specs = _fox_specs(S, tq, n_pairs, base) + [
        pl.BlockSpec((tq, PAIR), lambda p, i: (i, p)),
        pl.BlockSpec((tq, PAIR), lambda p, i: (i, p)),
        pl.BlockSpec((tq, PAIR), lambda p, i: (i, p))]
    out_specs = [pl.BlockSpec((tq, PAIR), lambda p, i: (i, p)),
                 pl.BlockSpec((S, PAIR), lambda p, i: (0, p)),
                 pl.BlockSpec((S, PAIR), lambda p, i: (0, p)),
                 pl.BlockSpec((tq, PAIR), lambda p, i: (i, p)),
                 pl.BlockSpec((1, nq, 8, tk), lambda p, i: (p, 0, 0, 0))]
    return _call_carrying(
        body, exchange, "fox_bwd", (n_pairs, nq), in_specs, out_specs,
        [jax.ShapeDtypeStruct((S, W), BF16)] * 3
        + [jax.ShapeDtypeStruct((S, W), F32), jax.ShapeDtypeStruct((n_pairs, nq, 8, tk), F32)],
        [pltpu.VMEM((4, tq, tk), F32)] * 2 + [pltpu.VMEM((4, tq, tk), BF16)] * 2
        + [pltpu.VMEM((2, tq, 128), F32)] * 3 + [pltpu.VMEM((4, 8, tk), F32)]
        + [pltpu.VMEM((tq, PAIR), F32), pltpu.VMEM((S, PAIR), F32), pltpu.VMEM((S, PAIR), F32),
           pltpu.VMEM((2, S, PAIR), BF16)],
        (qkv, qkv, qkv, fcx, fcr, o, lse, do))


def _col_chunks(n, width=256):
    return [(c, min(width, n - c)) for c in range(0, n, width)]


def _swiglu_fwd(u2, wg, wu):
    S, D = u2.shape
    FF = wg.shape[1]
    tm, tn = _pick(S, (512, 256, 128)), _divisors(FF, 1536)[0]

    def body(u_ref, g_ref, w_ref, a_ref, b_ref, h_ref):
        u = u_ref[...]
        for c, w in _col_chunks(tn):
            cols = slice(c, c + w)
            a = _dot(u, g_ref[:, cols])
            b = _dot(u, w_ref[:, cols])
            a_ref[:, cols] = a.astype(BF16)
            b_ref[:, cols] = b.astype(BF16)
            h_ref[:, cols] = (a / (1.0 + jnp.exp(-a)) * b).astype(BF16)

    spec_o = pl.BlockSpec((tm, tn), lambda i, j: (i, j))
    return pl.pallas_call(
        body, name="swiglu_fwd", grid=(S // tm, FF // tn),
        in_specs=[pl.BlockSpec((tm, D), lambda i, j: (i, 0)),
                  pl.BlockSpec((D, tn), lambda i, j: (0, j)),
                  pl.BlockSpec((D, tn), lambda i, j: (0, j))],
        out_specs=[spec_o] * 3, out_shape=[jax.ShapeDtypeStruct((S, FF), BF16)] * 3,
        compiler_params=_params(("parallel", "parallel")),
    )(u2, wg, wu)


def _swiglu_bwd(dh, wd, a, b):
    S, D = dh.shape
    FF = wd.shape[0]
    tm, tn = _pick(S, (512, 256, 128)), _divisors(FF, 1536)[0]

    def body(dh_ref, w_ref, a_ref, b_ref, da_ref, db_ref):
        dh_blk = dh_ref[...]
        for c, w in _col_chunks(tn):
            cols = slice(c, c + w)
            dhin = _dot(dh_blk, w_ref[cols, :], 1, 1)
            av = a_ref[:, cols].astype(F32)
            bv = b_ref[:, cols].astype(F32)
            sig = 1.0 / (1.0 + jnp.exp(-av))
            da_ref[:, cols] = (dhin * bv * (sig * (1.0 + av * (1.0 - sig)))).astype(BF16)
            db_ref[:, cols] = (dhin * (av * sig)).astype(BF16)

    spec_o = pl.BlockSpec((tm, tn), lambda i, j: (i, j))
    return pl.pallas_call(
        body, name="swiglu_bwd", grid=(S // tm, FF // tn),
        in_specs=[pl.BlockSpec((tm, D), lambda i, j: (i, 0)),
                  pl.BlockSpec((tn, D), lambda i, j: (j, 0)), spec_o, spec_o],
        out_specs=[spec_o] * 2, out_shape=[jax.ShapeDtypeStruct((S, FF), BF16)] * 2,
        compiler_params=_params(("parallel", "parallel")),
    )(dh, wd, a, b)


def _gate_bwd(dmix, wo, g_sb, g_fx, y_sb, y_fx):
    S, D = dmix.shape
    tm, tn = _pick(S, (512, 256, 128)), _divisors(D, 1024)[0]

    def body(dm_ref, w_ref, gs_ref, gf_ref, ys_ref, yf_ref, dys_ref, dyf_ref, dls_ref, dlf_ref, bs_ref, bf_ref):
        @pl.when(pl.program_id(1) == 0)
        def _():
            bs_ref[...] = jnp.zeros_like(bs_ref)
            bf_ref[...] = jnp.zeros_like(bf_ref)

        dm_blk = dm_ref[...]
        for c, w in _col_chunks(tn):
            cols = slice(c, c + w)
            dmi = _dot(dm_blk, w_ref[cols, :], 1, 1)
            gs, gf = gs_ref[:, cols], gf_ref[:, cols]
            dys_ref[:, cols] = (dmi * gs).astype(BF16)
            dyf_ref[:, cols] = (dmi * gf).astype(BF16)
            dls = dmi * ys_ref[:, cols] * gs * (1.0 - gs)
            dlf = dmi * yf_ref[:, cols] * gf * (1.0 - gf)
            dls_ref[:, cols] = dls.astype(BF16)
            dlf_ref[:, cols] = dlf.astype(BF16)
            bs_ref[0:1, cols] += _colsum(dls)
            bf_ref[0:1, cols] += _colsum(dlf)

    t = pl.BlockSpec((tm, tn), lambda j, i: (i, j))
    accs = pl.BlockSpec((8, tn), lambda j, i: (0, j))
    return pl.pallas_call(
        body, name="gate_bwd", grid=(D // tn, S // tm),
        in_specs=[pl.BlockSpec((tm, D), lambda j, i: (i, 0)),
                  pl.BlockSpec((tn, D), lambda j, i: (j, 0)), t, t, t, t],
        out_specs=[t, t, t, t, accs, accs],
        out_shape=[jax.ShapeDtypeStruct((S, D), BF16)] * 4 + [jax.ShapeDtypeStruct((8, D), F32)] * 2,
        compiler_params=_params(("parallel", "arbitrary")),
    )(dmix, wo, g_sb, g_fx, y_sb, y_fx)


def _local_step(x, target, ada8, lnp8, bg_sb, bg_fx, bf_pad, wqkv, wf, wgs, wgf, gather, later_weights, pack_early,
                pack_last):
    S, D = x.shape
    W = wqkv.shape[1] // 6
    n_pairs = W // PAIR
    n_heads = W // HEAD_DIM
    ts = _pick(S, (512, 256, 128))
    tq = _pick(S, (256, 128))

    u1 = _ln_mod(x, ada8, ts)
    qkv = _mm([(u1, wqkv)], 'nn', BF16, "in_qkv")
    f, g_sb, g_fx = _in_gates(u1, wf, wgs, wgf, bg_sb, bg_fx, ts)
    fc = _fgate_fwd(f, bf_pad, _pick(S, (512, 256, 128)))
    fch = fc[:, :n_heads]
    fcx = jnp.repeat(fch, HEAD_DIM, axis=1)
    nq = S // tq
    fcr = jnp.pad(fch.T.reshape(n_pairs, 2, nq, tq).transpose(0, 2, 1, 3),
                  ((0, 0), (0, 0), (0, 6), (0, 0)))
    o_sb, o_sb32, *zone_a = _sb_fwd(qkv, n_pairs, 0, tq, gather[0])
    o_fx, lse, *zone_b = _fox_fwd(qkv, fcx, fcr, n_pairs, 3 * n_pairs, tq, gather[1])
    wsb, wfx, wo, wfg, wfu, wfd = later_weights(*zone_a, *zone_b)
    y_sb =_mm([(o_sb, wsb)], 'nn', F32, "out_sb")
    y_fx = _mm([(o_fx, wfx)], 'nn', F32, "out_fx")
    mix_in, mix, x1, u2 = _gate_mix_out(g_sb, g_fx, y_sb, y_fx, x, wo, ada8, lnp8, ts)
    a, b, hin = _swiglu_fwd(u2, wfg, wfu)
    dr2, dh, st_loss = _loss_head(x1, hin, wfd, target, ada8, lnp8, ts)

    da, db = _swiglu_bwd(dh, wfd, a, b)
    g_wfd = _mm([(hin, dh)], 'tn', F32, "g_ffn_down")
    g_wfg = _mm([(u2, da)], 'tn', F32, "g_ffn_gate")
    g_wfu = _mm([(u2, db)], 'tn', F32, "g_ffn_up")
    dr1, dmix, st_mid = _mid_bwd(da, db, wfg, wfu, x1, dr2, mix, x, ada8, lnp8, _pick(S, (256, 128)))
    dys, dyf, dls, dlf, gb_sb, gb_fx = _gate_bwd(dmix, wo, g_sb, g_fx, y_sb, y_fx)
    g_wo = _mm([(mix_in, dmix)], 'tn', F32, "g_w_o")
    do_sb = _mm([(dys, wsb)], 'nt', BF16, "d_o_sb")
    do_fx = _mm([(dyf, wfx)], 'nt', BF16, "d_o_fx")
    g_wsb = _mm([(o_sb, dys)], 'tn', F32, "g_sb_out")
    g_wfx = _mm([(o_fx, dyf)], 'tn', F32, "g_fox_out")
    scatter = pack_early(dict(sb=g_wsb, fx=g_wfx, o=g_wo, fg=g_wfg, fu=g_wfu, fd=g_wfd))
    dq_s, dk_s, dv_s, *zone_a = _sb_bwd(qkv, o_sb32, do_sb, n_pairs, 0, tq, scatter[0])
    dq_f, dk_f, dv_f, dfq, dfk, *zone_b = _fox_bwd(qkv, fcx, fcr, o_fx, lse, do_fx, n_pairs, 3 * n_pairs, tq,
                                                    scatter[1])
    early = [(scatter[0], zone_a[0] if zone_a else None), (scatter[1], zone_b[0] if zone_b else None)]
    dfc = dfq[:, ::HEAD_DIM] - dfk[:, :, :2, :].transpose(0, 2, 1, 3).reshape(n_heads, S).T
    dfc = jnp.pad(dfc, ((0, 0), (0, 128 - n_heads)))
    df, gb_f = _fgate_bwd(dfc, f, bf_pad, _pick(S, (512, 256, 128)))
    dqkv = jnp.concatenate([dq_s, dk_s, dv_s, dq_f, dk_f, dv_f], axis=1)
    g_wqkv = [_mm([(u1, dqkv)], 'tn', F32, "g_in_qkv")]
    g_wf = _mm([(u1, df)], 'tn', F32, "g_in_f")
    g_wgs = _mm([(u1, dls)], 'tn', F32, "g_in_gsb")
    g_wgf = _mm([(u1, dlf)], 'tn', F32, "g_in_gfx")
    wgrads = dict(qkv=g_wqkv, f=g_wf, gs=g_wgs, gf=g_wgf)
    last = pack_last(wgrads)
    gx, st_first, *last_zone = _first_bwd(
        [(dqkv, wqkv), (df, wf), (dls, wgs), (dlf, wgf)], x, dr1, ada8, last)
    last_zone = last_zone[0] if last_zone else None

    stats = dict(loss=st_loss, mid=st_mid, first=st_first, gb_sb=gb_sb, gb_fx=gb_fx, gb_f=gb_f)
    return gx, wgrads, stats, early, (last, last_zone)


def _position():
    x, y, c = lax.axis_index("x"), lax.axis_index("y"), lax.axis_index("c")
    return x, y, c, 4 * x + 2 * y + c


def _flip(x, y, c, k):
    px = 1 - x if k & 4 else x
    py = 1 - y if k & 2 else y
    pc = 1 - c if k & 1 else c
    return (px, py, pc), 4 * px + 2 * py + pc


def _all_gather_small(v, name):
    r, n = v.shape

    def body(x_ref, out_ref, send_sems, recv_sems, local_sem):
        x, y, c, me = _position()
        mine = pltpu.make_async_copy(x_ref, out_ref.at[me], local_sem)
        mine.start()
        sends = []
        for k in range(1, N_DEV):
            peer, _ = _flip(x, y, c, k)
            cp = pltpu.make_async_remote_copy(
                src_ref=x_ref, dst_ref=out_ref.at[me], send_sem=send_sems.at[k - 1], recv_sem=recv_sems.at[k - 1],
                device_id=peer, device_id_type=MESH)
            cp.start()
            sends.append(cp)
        for k in range(1, N_DEV):
            peer, slot = _flip(x, y, c, k)
            pltpu.make_async_remote_copy(
                src_ref=x_ref, dst_ref=out_ref.at[slot], send_sem=send_sems.at[k - 1], recv_sem=recv_sems.at[k - 1],
                device_id=peer, device_id_type=MESH).wait_recv()
        for cp in sends:
            cp.wait_send()
        mine.wait()

    return pl.pallas_call(
        body, name=name, out_shape=jax.ShapeDtypeStruct((N_DEV, r, n), v.dtype),
        in_specs=[pl.BlockSpec(memory_space=pltpu.VMEM)], out_specs=pl.BlockSpec(memory_space=pltpu.VMEM),
        scratch_shapes=[pltpu.SemaphoreType.DMA((N_DEV - 1,)), pltpu.SemaphoreType.DMA((N_DEV - 1,)),
                        pltpu.SemaphoreType.DMA],
    )(v)


def _all_gather_weights(packed):
    R, C = packed.shape

    def body(x_ref, out_ref, send_sems, recv_sems, local_sem):
        x, y, c, me = _position()
        sibling, sib_slot = _flip(x, y, c, 1)
        mine = pltpu.make_async_copy(x_ref, out_ref.at[me], local_sem)
        mine.start()

        def copy(k, slot, to, src=None):
            return pltpu.make_async_remote_copy(
                src_ref=out_ref.at[slot] if src is None else src, dst_ref=out_ref.at[slot],
                send_sem=send_sems.at[k], recv_sem=recv_sems.at[k], device_id=to, device_id_type=MESH)

        first = [copy(0, me, sibling, src=x_ref)]
        chips = (4, 2, 6)
        for n, k in enumerate(chips):
            peer, _ = _flip(x, y, c, k)
            first.append(copy(1 + n, me, peer, src=x_ref))
        for cp in first:
            cp.start()
        passed = []
        for n, k in enumerate(chips):
            peer, slot = _flip(x, y, c, k)
            copy(1 + n, slot, peer).wait_recv()
            cp = copy(4 + n, slot, sibling)
            cp.start()
            passed.append(cp)
        copy(0, sib_slot, sibling).wait_recv()
        for n, k in enumerate(chips):
            _, slot = _flip(x, y, c, k | 1)
            copy(4 + n, slot, sibling).wait_recv()
        for cp in first + passed:
            cp.wait_send()
        mine.wait()

    return pl.pallas_call(
        body, name="all_gather_weights", out_shape=jax.ShapeDtypeStruct((N_DEV, R, C), packed.dtype),
        in_specs=[pl.BlockSpec(memory_space=pl.ANY)], out_specs=pl.BlockSpec(memory_space=pl.ANY),
        scratch_shapes=[pltpu.SemaphoreType.DMA((7,)), pltpu.SemaphoreType.DMA((7,)), pltpu.SemaphoreType.DMA],
    )(packed)


def _own_slot(land, own):
    me = 4 * lax.axis_index("x") + 2 * lax.axis_index("y") + lax.axis_index("c")
    return lax.dynamic_update_slice(land, own[None], (me, 0, 0))


def _sum_slots(recv, name, tr):
    n, R, C = recv.shape

    def body(r_ref, o_ref):
        acc = r_ref[0].astype(F32)
        for s in range(1, n):
            acc = acc + r_ref[s].astype(F32)
        o_ref[...] = acc

    return pl.pallas_call(
        body, name=name, grid=(R // tr,), in_specs=[pl.BlockSpec((n, tr, C), lambda i: (0, i, 0))],
        out_specs=pl.BlockSpec((tr, C), lambda i: (i, 0)), out_shape=jax.ShapeDtypeStruct((R, C), F32),
        compiler_params=_params(("parallel",)),
    )(recv)


def _sum_stats(st_all, loss_row):
    n, r, D = st_all.shape

    def body(s_ref, o_ref, l_ref):
        acc = s_ref[0]
        for d in range(1, n):
            acc = acc + s_ref[d]
        o_ref[...] = acc
        l_ref[...] = jnp.zeros((8, 128), F32) + jnp.sum(acc[loss_row:loss_row + 1, :], axis=-1, keepdims=True)

    return pl.pallas_call(
        body, name="sum_stats", out_shape=[jax.ShapeDtypeStruct((r, D), F32), jax.ShapeDtypeStruct((8, 128), F32)],
    )(st_all)


def _adamw(w, g, m, v, name):
    R, C = w.shape
    tr = _pick(R, (256, 176, 128, 64, 32, 16, 8))
    c1 = 1.0 / (1.0 - ADAM_B1 ** ADAM_STEP)
    c2 = 1.0 / (1.0 - ADAM_B2 ** ADAM_STEP)

    def body(w_ref, g_ref, m_ref, v_ref, d_ref, nm_ref, nv_ref):
        gv = g_ref[...]
        nm = ADAM_B1 * m_ref[...] + (1.0 - ADAM_B1) * gv
        nv = ADAM_B2 * v_ref[...] + (1.0 - ADAM_B2) * (gv * gv)
        nm_ref[...] = nm
        nv_ref[...] = nv
        d_ref[...] = -ADAM_LR * ((nm * c1) / (jnp.sqrt(nv * c2) + ADAM_EPS) + ADAM_WD * w_ref[...])

    spec = pl.BlockSpec((tr, C), lambda i: (i, 0))
    return pl.pallas_call(
        body, name=name, grid=(R // tr,), in_specs=[spec] * 4, out_specs=[spec] * 3,
        out_shape=[jax.ShapeDtypeStruct((R, C), F32)] * 3, compiler_params=_params(("parallel",)),
    )(w, g, m, v)


def _round16(n):
    return -(-n // 16) * 16


def _pack_layout(D, in_cols, ff, W):
    parts = [("in", D * (in_cols // N_DEV) // D), ("fg", ff // N_DEV), ("fu", ff // N_DEV),
             ("sb", W * (D // N_DEV) // D), ("fx", W * (D // N_DEV) // D), ("o", D // N_DEV), ("fd", ff // N_DEV)]
    layout, off = {}, 0
    for nm, rows in parts:
        layout[nm] = (off, rows)
        off += _round16(rows)
    return layout, off


def _rows_of(a, D, rows):
    a = a.reshape(rows, D)
    return jnp.pad(a, ((0, _round16(rows) - rows), (0, 0)))


def _cols_to_dest(g, D):
    K, N = g.shape
    n = N // N_DEV
    return g.reshape(K, N_DEV, n).transpose(1, 0, 2).reshape(N_DEV, K * n // D, D)


def _cols_from_src(blocks, K, n):
    return blocks.reshape(N_DEV, K, n).transpose(1, 0, 2).reshape(K, N_DEV * n)


def _pad_rows16(a):
    rows = a.shape[1]
    return jnp.pad(a, ((0, 0), (0, _round16(rows) - rows), (0, 0)))


def kernel(x, c, w_ada, b_ada, w_in, b_gate, b_forget, w_sb_out, w_fox_out, w_o, ln1_g, ln1_b, w_ffn_gate, w_ffn_up, w_ffn_down, ln2_g, ln2_b, loss_target, m_w_ada, m_b_ada, m_w_in, m_b_gate, m_b_forget, m_w_sb_out, m_w_fox_out, m_w_o, m_ln1_g, m_ln1_b, m_w_ffn_gate, m_w_ffn_up, m_w_ffn_down, m_ln2_g, m_ln2_b, v_w_ada, v_b_ada, v_w_in, v_b_gate, v_b_forget, v_w_sb_out, v_w_fox_out, v_w_o, v_ln1_g, v_ln1_b, v_w_ffn_gate, v_w_ffn_up, v_w_ffn_down, v_ln2_g, v_ln2_b):
    S, D = x.shape[1], x.shape[2]
    W = w_sb_out.shape[1]
    n_heads = b_forget.shape[1]
    ff = w_ffn_down.shape[1] * N_DEV
    in_loc = w_in.shape[2]
    in_cols = in_loc * N_DEV
    ada_loc = w_ada.shape[2]
    n_cond = ada_loc * N_DEV // D
    assert w_ada.shape[0] == 1 and n_cond == 6 and in_cols == 6 * W + n_heads + 2 * D and n_heads <= 128
    me = 4 * lax.axis_index("x") + 2 * lax.axis_index("y") + lax.axis_index("c")

    c_all = _all_gather_small(c, "gather_c").reshape(N_DEV, D)
    c16 = jnp.pad(c_all, ((0, 16 - N_DEV), (0, 0)))
    b_cols = lax.dynamic_slice(b_ada, (0, me * ada_loc), (1, ada_loc))
    ada_cols = _mm([(c16, w_ada[0])], 'nn', F32, "ada_fwd", bias=b_cols, silu_a=True)[:N_DEV]
    ada_all = _all_gather_small(ada_cols, "gather_ada")
    ada_me = lax.dynamic_index_in_dim(ada_all, me, axis=1, keepdims=False)
    ada8 = jnp.pad(ada_me.reshape(n_cond, D), ((0, 8 - n_cond), (0, 0)))
    lnp8 = jnp.concatenate([ln1_g, ln1_b, ln2_g, ln2_b, jnp.zeros((4, D), F32)], axis=0)

    layout, R = _pack_layout(D, in_cols, ff, W)
    shards = dict(**{"in": w_in[0]}, fg=w_ffn_gate[0], fu=w_ffn_up[0], sb=w_sb_out[0], fx=w_fox_out[0], o=w_o[0],
                  fd=w_ffn_down[0])
    rows_fwd, rows_bwd = ("sb", "fx", "o"), ("fd", "o", "sb", "fx")
    rows_of = {nm: layout[nm][1] for nm in layout}

    def offsets(names):
        offs, off = {}, 0
        for nm in names:
            offs[nm] = off
            off += _round16(rows_of[nm])
        return offs

    def as_rows(names):
        return jnp.concatenate([_rows_of(shards[nm].astype(BF16), D, rows_of[nm]) for nm in names], axis=0)

    def whole_from(blocks):
        return blocks.transpose(1, 0, 2).reshape(blocks.shape[1], N_DEV * blocks.shape[2])

    def blocks_of(g):
        return g.reshape(g.shape[0], N_DEV, g.shape[1] // N_DEV).transpose(1, 0, 2)

    gather_src = [as_rows(rows_fwd),
                  jnp.concatenate([shards["fg"].astype(BF16), shards["fu"].astype(BF16)], axis=0), as_rows(("fd",))]
    gathered_in = _all_gather_weights(shards["in"].astype(BF16))

    w_in_full = whole_from(gathered_in)
    wqkv = w_in_full[:, :6 * W]
    wf = jnp.pad(w_in_full[:, 6 * W:6 * W + n_heads], ((0, 0), (0, 128 - n_heads)))
    wgs = w_in_full[:, 6 * W + n_heads:6 * W + n_heads + D]
    wgf = w_in_full[:, 6 * W + n_heads + D:]
    bf_pad = jnp.pad(b_forget, ((0, 0), (0, 128 - n_heads)))

    def later_weights(zone_rows, zone_gate_up, zone_down):
        rows, offs = _own_slot(zone_rows, gather_src[0]), offsets(rows_fwd)
        part = {nm: rows[:, offs[nm]:offs[nm] + rows_of[nm], :] for nm in rows_fwd}
        gate_up = _own_slot(zone_gate_up, gather_src[1])
        down = _own_slot(zone_down, gather_src[2])[:, :rows_of["fd"], :]
        return (_cols_from_src(part["sb"], W, D // N_DEV), _cols_from_src(part["fx"], W, D // N_DEV),
                part["o"].reshape(D, D), whole_from(gate_up[:, :D, :]), whole_from(gate_up[:, D:, :]),
                down.reshape(ff, D))

    def pack_early(g):
        dest = {"sb": _cols_to_dest(g["sb"], D), "fx": _cols_to_dest(g["fx"], D),
                "o": g["o"].reshape(N_DEV, D // N_DEV, D), "fd": g["fd"].reshape(N_DEV, ff // N_DEV, D)}
        rows = jnp.concatenate([_pad_rows16(dest[nm].astype(BF16)) for nm in rows_bwd], axis=1)
        gate_up = jnp.concatenate([blocks_of(g["fg"].astype(BF16)), blocks_of(g["fu"].astype(BF16))], axis=1)
        return [(rows, True), (gate_up, True)]

    def pack_last(g):
        g_in = jnp.concatenate(g["qkv"] + [g["f"][:, :n_heads], g["gs"], g["gf"]], axis=1)
        return blocks_of(g_in.astype(BF16)), True

    gx, wg, st, early, ((pack_in, _), land_in) = _local_step(
        x[0], loss_target[0], ada8, lnp8, b_gate[:, :D], b_gate[:, D:], bf_pad, wqkv, wf, wgs, wgf,
        [(gather_src[0], False), [(gather_src[1], False), (gather_src[2], False)]],
        later_weights, pack_early, pack_last)

    def summed(zone, sent, name):
        own = lax.dynamic_index_in_dim(sent, me, axis=0, keepdims=False)
        rows = zone.shape[1]
        block = max(t for t in range(16, 705, 16) if rows % t == 0)
        return _sum_slots(_own_slot(zone, own), name, block)

    ((sent_rows, _), zone_rows), ((sent_gate_up, _), zone_gate_up) = early
    gate_up = summed(zone_gate_up, sent_gate_up, "sum_grads_gate_up")
    gsum = {"in": summed(land_in, pack_in, "sum_grads_in"), "fg": gate_up[:D], "fu": gate_up[D:]}
    total, offs = summed(zone_rows, sent_rows, "sum_grads_rows"), offsets(rows_bwd)
    for nm in rows_bwd:
        gsum[nm] = total[offs[nm]:offs[nm] + rows_of[nm]]

    def gshard(nm, shape):
        return gsum[nm].reshape(shape)

    zrow = jnp.zeros((1, D), F32)
    gb_f_row = jnp.pad(st["gb_f"][0:1], ((0, 0), (0, D - 128)))
    stats16 = jnp.concatenate([
        st["first"][1:2], st["first"][0:1], st["mid"][4:5], st["mid"][1:2], st["mid"][0:1], st["loss"][3:4],
        st["mid"][2:3], st["mid"][3:4], st["loss"][1:2], st["loss"][2:3], st["gb_sb"][0:1], st["gb_fx"][0:1],
        st["loss"][0:1], gb_f_row, zrow, zrow], axis=0)
    st_all = _all_gather_small(stats16, "gather_stats")
    st_sum, loss_blk = _sum_stats(st_all, 12)
    loss = loss_blk[0, 0]

    d_ada_all = st_all[:, :n_cond, :].reshape(N_DEV, n_cond * D)
    d_cols = lax.dynamic_slice(d_ada_all, (0, me * ada_loc), (N_DEV, ada_loc))
    d16 = jnp.pad(d_cols, ((0, 16 - N_DEV), (0, 0)))
    g_w_ada = _mm([(c16, d16)], 'tn', F32, "ada_wgrad", silu_a=True)

    small_w = jnp.concatenate([b_ada.reshape(n_cond, D), ln1_g, ln1_b, ln2_g, ln2_b, b_gate.reshape(2, D), zrow,
                               jnp.pad(b_forget, ((0, 0), (0, D - n_heads))), zrow, zrow], axis=0)
    small_m = jnp.concatenate([m_b_ada.reshape(n_cond, D), m_ln1_g, m_ln1_b, m_ln2_g, m_ln2_b, m_b_gate.reshape(2, D),
                               zrow, jnp.pad(m_b_forget, ((0, 0), (0, D - n_heads))), zrow, zrow], axis=0)
    small_v = jnp.concatenate([v_b_ada.reshape(n_cond, D), v_ln1_g, v_ln1_b, v_ln2_g, v_ln2_b, v_b_gate.reshape(2, D),
                               zrow, jnp.pad(v_b_forget, ((0, 0), (0, D - n_heads))), zrow, zrow], axis=0)
    sm = _adamw(small_w, st_sum, small_m, small_v, "adamw_small")

    def small(a, nm):
        if nm == "b_ada":
            return a[0:n_cond].reshape(1, n_cond * D)
        if nm == "b_gate":
            return a[10:12].reshape(1, 2 * D)
        if nm == "b_forget":
            return a[13:14, :n_heads]
        row = {"ln1_g": 6, "ln1_b": 7, "ln2_g": 8, "ln2_b": 9}[nm]
        return a[row:row + 1]

    big = {
        "w_ada": (w_ada[0], g_w_ada, m_w_ada[0], v_w_ada[0]),
        "w_in": (w_in[0], gshard("in", w_in.shape[1:]), m_w_in[0], v_w_in[0]),
        "w_sb_out": (w_sb_out[0], gshard("sb", w_sb_out.shape[1:]), m_w_sb_out[0], v_w_sb_out[0]),
        "w_fox_out": (w_fox_out[0], gshard("fx", w_fox_out.shape[1:]), m_w_fox_out[0], v_w_fox_out[0]),
        "w_o": (w_o[0], gshard("o", w_o.shape[1:]), m_w_o[0], v_w_o[0]),
        "w_ffn_gate": (w_ffn_gate[0], gshard("fg", w_ffn_gate.shape[1:]), m_w_ffn_gate[0], v_w_ffn_gate[0]),
        "w_ffn_up": (w_ffn_up[0], gshard("fu", w_ffn_up.shape[1:]), m_w_ffn_up[0], v_w_ffn_up[0]),
        "w_ffn_down": (w_ffn_down[0], gshard("fd", w_ffn_down.shape[1:]), m_w_ffn_down[0], v_w_ffn_down[0]),
    }
    order = ["w_ada", "b_ada", "w_in", "b_gate", "b_forget", "w_sb_out", "w_fox_out", "w_o", "ln1_g", "ln1_b",
             "w_ffn_gate", "w_ffn_up", "w_ffn_down", "ln2_g", "ln2_b"]
    grads, deltas, new_ms, new_vs = [], [], [], []
    for nm in order:
        if nm in big:
            w, g, m, v = big[nm]
            d, nm_, nv_ = _adamw(w, g, m, v, "adamw_" + nm)
            grads.append(g[None])
            deltas.append(d[None])
            new_ms.append(nm_[None])
            new_vs.append(nv_[None])
        else:
            grads.append(small(st_sum, nm))
            deltas.append(small(sm[0], nm))
            new_ms.append(small(sm[1], nm))
            new_vs.append(small(sm[2], nm))
    return (loss, gx[None], *grads, *deltas, *new_ms, *new_vs)
```

```python
import jax
import jax.numpy as jnp
from jax import lax
from jax.experimental import pallas as pl
from jax.experimental.pallas import tpu as pltpu

F32 = jnp.float32
BF16 = jnp.bfloat16

HEAD_DIM = 64
PAIR = 2 * HEAD_DIM
LN_EPS = 1e-5
ALPHA = 2.0 ** 0.25
ADAM_LR, ADAM_B1, ADAM_B2, ADAM_EPS, ADAM_WD, ADAM_STEP = 0.001, 0.9, 0.999, 1e-08, 0.01, 10
N_DEV = 8
VMEM_LIMIT = 56 * 1024 * 1024
MESH = pl.DeviceIdType.MESH


def _dot(a, b, ca=1, cb=0):
    return lax.dot_general(a, b, (((ca,), (cb,)), ((), ())), preferred_element_type=F32)


def _pick(n, cands):
    for t in cands:
        if n % t == 0:
            return t
    return n


def _params(sem):
    return pltpu.CompilerParams(dimension_semantics=sem, vmem_limit_bytes=VMEM_LIMIT)


MM_BLOCK_BYTES = 40 * 1024 * 1024
LANES = 128


def _divisors(n, cap):
    ds = [d for d in range(LANES, min(n, cap) + 1, LANES) if n % d == 0]
    return sorted(ds, reverse=True) or [n]


def _mm_tiles(M, N, a_row_bytes, b_row_bytes, out_itemsize):
    best = None
    for tm in _divisors(M, 1024):
        for tn in _divisors(N, 2048):
            need = 2 * (tm * a_row_bytes + tn * b_row_bytes + tm * tn * out_itemsize) + tm * tn * 4
            if need <= MM_BLOCK_BYTES and (best is None or (tm * tn, tm) > (best[0] * best[1], best[0])):
                best = (tm, tn)
    assert best is not None, (M, N, a_row_bytes, b_row_bytes)
    return best


def _mm(pairs, mode, out_dtype, name, bias=None, act=None, silu_a=False, exchange=None):
    norm = []
    for p in pairs:
        a, b = p[0], p[1]
        kdim_a = a.shape[0] if mode == 'tn' else a.shape[1]
        K, ka, kb = (p[2], p[3], p[4]) if len(p) > 2 else (kdim_a, 0, 0)
        norm.append((a, b, K, ka, kb))
    a0, b0 = norm[0][0], norm[0][1]
    M = a0.shape[1] if mode == 'tn' else a0.shape[0]
    N = b0.shape[0] if mode == 'nt' else b0.shape[1]
    tm, tn = _mm_tiles(M, N, sum(K * a.dtype.itemsize for a, _, K, _, _ in norm),
                       sum(K * b.dtype.itemsize for _, b, K, _, _ in norm), jnp.dtype(out_dtype).itemsize)
    n_pairs = len(norm)

    in_specs, args = [], []
    for a, b, K, ka, kb in norm:
        if mode == 'tn':
            in_specs.append(pl.BlockSpec((K, tm), lambda i, j, ka=ka: (ka, i)))
        else:
            in_specs.append(pl.BlockSpec((tm, K), lambda i, j, ka=ka: (i, ka)))
        if mode == 'nt':
            in_specs.append(pl.BlockSpec((tn, K), lambda i, j, kb=kb: (j, kb)))
        else:
            in_specs.append(pl.BlockSpec((K, tn), lambda i, j, kb=kb: (kb, j)))
        args += [a, b]
    if bias is not None:
        in_specs.append(pl.BlockSpec((1, tn), lambda i, j: (0, j)))
        args.append(bias)

    ca = 0 if mode == 'tn' else 1
    cb = 1 if mode == 'nt' else 0

    def body(*refs):
        o_ref = refs[-1]
        acc = None
        for p in range(n_pairs):
            av = refs[2 * p][...]
            if silu_a:
                av = av / (1.0 + jnp.exp(-av))
            d = _dot(av.astype(BF16), refs[2 * p + 1][...].astype(BF16), ca, cb)
            acc = d if acc is None else acc + d
        if bias is not None:
            acc = acc + refs[2 * n_pairs][...]
        if act == 'sigmoid':
            acc = 1.0 / (1.0 + jnp.exp(-acc))
        o_ref[...] = acc.astype(out_dtype)

    out_spec = pl.BlockSpec((tm, tn), lambda i, j: (i, j))
    out_shape = jax.ShapeDtypeStruct((M, N), out_dtype)
    if exchange is not None:
        return _call_carrying(body, exchange, name, (M // tm, N // tn), in_specs, [out_spec], [out_shape], [], args)
    return pl.pallas_call(
        body, name=name, grid=(M // tm, N // tn), in_specs=in_specs, out_specs=out_spec, out_shape=out_shape,
        compiler_params=_params(("parallel", "parallel")),
    )(*args)


def _rows_call(body, name, row_ins, vec_ins, row_outs, acc_outs, ts):
    S = row_ins[0].shape[0]
    in_specs = [pl.BlockSpec((ts, a.shape[1]), lambda i: (i, 0)) for a in row_ins]
    in_specs += [pl.BlockSpec(a.shape, lambda i: (0, 0)) for a in vec_ins]
    out_specs = [pl.BlockSpec((ts, c), lambda i: (i, 0)) for c, _ in row_outs]
    out_specs += [pl.BlockSpec(s, lambda i: (0, 0)) for s in acc_outs]
    out_shape = [jax.ShapeDtypeStruct((S, c), dt) for c, dt in row_outs]
    out_shape += [jax.ShapeDtypeStruct(s, F32) for s in acc_outs]
    return pl.pallas_call(
        body, name=name, grid=(S // ts,), in_specs=in_specs, out_specs=out_specs, out_shape=out_shape,
        compiler_params=_params(("arbitrary",)),
    )(*row_ins, *vec_ins)


def _ln_stats(v):
    mu = jnp.mean(v, axis=-1, keepdims=True)
    d = v - mu
    var = jnp.mean(d * d, axis=-1, keepdims=True)
    rstd = lax.rsqrt(var + LN_EPS)
    return d * rstd, rstd


def _ln_bwd(dxhat, xhat, rstd):
    m1 = jnp.mean(dxhat, axis=-1, keepdims=True)
    m2 = jnp.mean(dxhat * xhat, axis=-1, keepdims=True)
    return rstd * (dxhat - m1 - xhat * m2)


def _colsum(v):
    return jnp.sum(v, axis=0, keepdims=True)


def _ln_mod(x, ada8, ts):
    D = x.shape[1]

    def body(x_ref, v_ref, u_ref):
        xhat, _ = _ln_stats(x_ref[...])
        u_ref[...] = (xhat * (1.0 + v_ref[1:2, :]) + v_ref[0:1, :]).astype(BF16)

    return _rows_call(body, "ln_mod", [x], [ada8], [(D, BF16)], [], ts)[0]


def _ln_mod_qkv(x, ada8, wqkv, ts):
    D = x.shape[1]

    def body(x_ref, v_ref, w_ref, u_ref, qkv_ref):
        xhat, _ = _ln_stats(x_ref[...])
        u = (xhat * (1.0 + v_ref[1:2, :]) + v_ref[0:1, :]).astype(BF16)
        u_ref[...] = u
        qkv_ref[...] = _dot(u, w_ref[...]).astype(BF16)

    return _rows_call(body, "ln_mod_qkv", [x], [ada8, wqkv], [(D, BF16), (wqkv.shape[1], BF16)], [], ts)


def _in_gates(u1, wf, wgs, wgf, bg_sb, bg_fx, ts):
    D = wgs.shape[1]

    def body(u_ref, wf_ref, ws_ref, wx_ref, bs_ref, bx_ref, f_ref, gs_ref, gx_ref):
        u = u_ref[...]
        f_ref[...] = _dot(u, wf_ref[...])
        gs_ref[...] = 1.0 / (1.0 + jnp.exp(-(_dot(u, ws_ref[...]) + bs_ref[...])))
        gx_ref[...] = 1.0 / (1.0 + jnp.exp(-(_dot(u, wx_ref[...]) + bx_ref[...])))

    return _rows_call(body, "in_gates", [u1], [wf, wgs, wgf, bg_sb, bg_fx],
                      [(wf.shape[1], F32), (D, F32), (D, F32)], [], ts)


def _gate_mix_out(g_sb, g_fx, y_sb, y_fx, x, wo, ada8, lnp8, ts):
    D = y_sb.shape[1]

    def body(gs, gf, ys, yf, x_ref, v_ref, p_ref, w_ref, mi_ref, mix_ref, x1_ref, u2_ref):
        mi = (gs[...] * ys[...] + gf[...] * yf[...]).astype(BF16)
        mi_ref[...] = mi
        mix = _dot(mi, w_ref[...])
        mix_ref[...] = mix
        r1 = ALPHA * x_ref[...] + v_ref[2:3, :] * mix
        xhat, _ = _ln_stats(r1)
        x1 = xhat * p_ref[0:1, :] + p_ref[1:2, :]
        x1_ref[...] = x1
        xh1, _ = _ln_stats(x1)
        u2_ref[...] = (xh1 * (1.0 + v_ref[4:5, :]) + v_ref[3:4, :]).astype(BF16)

    return _rows_call(body, "gate_mix_out", [g_sb, g_fx, y_sb, y_fx, x], [ada8, lnp8, wo],
                      [(D, BF16), (D, F32), (D, F32), (D, BF16)], [], ts)


def _loss_head(x1, hin, wfd, target, ada8, lnp8, ts):
    D = x1.shape[1]

    def body(x1_ref, hin_ref, t_ref, v_ref, p_ref, w_ref, dr2_ref, dh_ref, st_ref):
        @pl.when(pl.program_id(0) == 0)
        def _():
            st_ref[...] = jnp.zeros_like(st_ref)

        hv = _dot(hin_ref[...], w_ref[...])
        g2 = v_ref[5:6, :]
        r2 = ALPHA * x1_ref[...] + g2 * hv
        xhat, rstd = _ln_stats(r2)
        y = xhat * p_ref[2:3, :] + p_ref[3:4, :]
        err = y - t_ref[...]
        dy = err * (1.0 / D)
        dr2 = _ln_bwd(dy * p_ref[2:3, :], xhat, rstd)
        dr2_ref[...] = dr2
        dh_ref[...] = (dr2 * g2).astype(BF16)
        st_ref[0:1, :] += _colsum(err * err) * (0.5 / D)
        st_ref[1:2, :] += _colsum(dy * xhat)
        st_ref[2:3, :] += _colsum(dy)
        st_ref[3:4, :] += _colsum(dr2 * hv)

    return _rows_call(body, "loss_head", [x1, hin, target], [ada8, lnp8, wfd], [(D, F32), (D, BF16)], [(8, D)], ts)


def _mid_bwd(da, db, wfg, wfu, x1, dr2, mix, x, ada8, lnp8, ts):
    D = x.shape[1]

    def body(da_ref, db_ref, x1_ref, dr2_ref, mix_ref, x_ref, v_ref, p_ref, wg_ref, wu_ref, dr1_ref, dmix_ref, st_ref):
        @pl.when(pl.program_id(0) == 0)
        def _():
            st_ref[...] = jnp.zeros_like(st_ref)

        du2v = _dot(da_ref[...], wg_ref[...], 1, 1) + _dot(db_ref[...], wu_ref[...], 1, 1)
        xh1, rstd1 = _ln_stats(x1_ref[...])
        dx1 = ALPHA * dr2_ref[...] + _ln_bwd(du2v * (1.0 + v_ref[4:5, :]), xh1, rstd1)
        mixv = mix_ref[...]
        g1 = v_ref[2:3, :]
        r1 = ALPHA * x_ref[...] + g1 * mixv
        xhr, rstdr = _ln_stats(r1)
        dr1 = _ln_bwd(dx1 * p_ref[0:1, :], xhr, rstdr)
        dr1_ref[...] = dr1
        dmix_ref[...] = (dr1 * g1).astype(BF16)
        st_ref[0:1, :] += _colsum(du2v * xh1)
        st_ref[1:2, :] += _colsum(du2v)
        st_ref[2:3, :] += _colsum(dx1 * xhr)
        st_ref[3:4, :] += _colsum(dx1)
        st_ref[4:5, :] += _colsum(dr1 * mixv)

    return _rows_call(body, "mid_bwd", [da, db, x1, dr2, mix, x], [ada8, lnp8, wfg, wfu], [(D, F32), (D, BF16)],
                      [(8, D)], ts)


def _first_bwd(pairs, x, dr1, ada8, exchange):
    S, D = x.shape
    norm = [(p[0], p[1]) + ((p[2], p[3], p[4]) if len(p) > 2 else (p[0].shape[1], 0, 0)) for p in pairs]
    tm = _pick(S, (256, 128))
    in_specs, args = [], []
    for a, b, K, ka, kb in norm:
        in_specs += [pl.BlockSpec((tm, K), lambda i, j, ka=ka: (i, ka)), pl.BlockSpec((D, K), lambda i, j, kb=kb: (0, kb))]
        args += [a, b]
    rows = pl.BlockSpec((tm, D), lambda i, j: (i, 0))
    sums = pl.BlockSpec((8, D), lambda i, j: (0, 0))
    n_pairs = len(norm)

    def body(*refs):
        x_ref, dr1_ref, v_ref, gx_ref, st_ref = refs[2 * n_pairs:]

        @pl.when(pl.program_id(0) == 0)
        def _():
            st_ref[...] = jnp.zeros_like(st_ref)

        du1 = None
        for p in range(n_pairs):
            d = _dot(refs[2 * p][...].astype(BF16), refs[2 * p + 1][...].astype(BF16), 1, 1)
            du1 = d if du1 is None else du1 + d
        xh0, rstd0 = _ln_stats(x_ref[...])
        gx_ref[...] = ALPHA * dr1_ref[...] + _ln_bwd(du1 * (1.0 + v_ref[1:2, :]), xh0, rstd0)
        st_ref[0:1, :] += _colsum(du1 * xh0)
        st_ref[1:2, :] += _colsum(du1)

    return _call_carrying(
        body, exchange, "first_bwd", (S // tm, 1), in_specs + [rows, rows, sums], [rows, sums],
        [jax.ShapeDtypeStruct((S, D), F32), jax.ShapeDtypeStruct((8, D), F32)], [], (*args, x, dr1, ada8),
        semantics=("arbitrary", "arbitrary"))


def _split3(v):
    hi = v.astype(BF16)
    r = v - hi.astype(F32)
    mid = r.astype(BF16)
    lo = (r - mid.astype(F32)).astype(BF16)
    return hi, mid, lo


def _fgate_fwd(f, bf_pad, tb):
    S = f.shape[0]

    def body(f_ref, b_ref, fc_ref, carry):
        @pl.when(pl.program_id(0) == 0)
        def _():
            carry[...] = jnp.zeros_like(carry)

        z = f_ref[...] + b_ref[...]
        ls = jnp.minimum(z, 0.0) - jnp.log(1.0 + jnp.exp(-jnp.abs(z)))
        r = lax.broadcasted_iota(jnp.int32, (tb, tb), 0)
        c = lax.broadcasted_iota(jnp.int32, (tb, tb), 1)
        tri = (c <= r).astype(BF16)
        hi, mid, lo = _split3(ls)
        cs = _dot(tri, hi) + _dot(tri, mid) + _dot(tri, lo) + carry[...]
        fc_ref[...] = cs
        carry[...] = cs[tb - 1:tb, :]

    return pl.pallas_call(
        body, name="fgate_fwd", grid=(S // tb,),
        in_specs=[pl.BlockSpec((tb, 128), lambda i: (i, 0)), pl.BlockSpec((1, 128), lambda i: (0, 0))],
        out_specs=pl.BlockSpec((tb, 128), lambda i: (i, 0)),
        out_shape=jax.ShapeDtypeStruct((S, 128), F32),
        scratch_shapes=[pltpu.VMEM((1, 128), F32)],
        compiler_params=_params(("arbitrary",)),
    )(f, bf_pad)


def _fgate_bwd(dfc, f, bf_pad, tb):
    S = f.shape[0]
    nb = S // tb

    def body(d_ref, f_ref, b_ref, df_ref, gb_ref, carry):
        @pl.when(pl.program_id(0) == 0)
        def _():
            carry[...] = jnp.zeros_like(carry)
            gb_ref[...] = jnp.zeros_like(gb_ref)

        r = lax.broadcasted_iota(jnp.int32, (tb, tb), 0)
        c = lax.broadcasted_iota(jnp.int32, (tb, tb), 1)
        tri = (c >= r).astype(BF16)
        hi, mid, lo = _split3(d_ref[...])
        rs = _dot(tri, hi) + _dot(tri, mid) + _dot(tri, lo) + carry[...]
        carry[...] = rs[0:1, :]
        z = f_ref[...] + b_ref[...]
        df = rs * (1.0 / (1.0 + jnp.exp(z)))
        df_ref[...] = df
        gb_ref[0:1, :] += _colsum(df)

    return pl.pallas_call(
        body, name="fgate_bwd", grid=(nb,),
        in_specs=[pl.BlockSpec((tb, 128), lambda i: (nb - 1 - i, 0)),
                  pl.BlockSpec((tb, 128), lambda i: (nb - 1 - i, 0)),
                  pl.BlockSpec((1, 128), lambda i: (0, 0))],
        out_specs=[pl.BlockSpec((tb, 128), lambda i: (nb - 1 - i, 0)), pl.BlockSpec((8, 128), lambda i: (0, 0))],
        out_shape=[jax.ShapeDtypeStruct((S, 128), F32), jax.ShapeDtypeStruct((8, 128), F32)],
        scratch_shapes=[pltpu.VMEM((1, 128), F32)],
        compiler_params=_params(("arbitrary",)),
    )(dfc, f, bf_pad)


def _split2(v):
    hi = v.astype(BF16)
    lo = (v - hi.astype(F32)).astype(BF16)
    return hi, lo


def _head_masks():
    lane = lax.broadcasted_iota(jnp.int32, (1, PAIR), 1)
    m0 = lane < HEAD_DIM
    return m0, jnp.logical_not(m0)


def _sel(mask, v):
    return jnp.where(mask, v, jnp.zeros_like(v))


def _softplus(z):
    return jnp.maximum(z, 0.0) + jnp.log(1.0 + jnp.exp(-jnp.abs(z)))


def _qkv_specs(S, tq, n_pairs, base):
    return [pl.BlockSpec((tq, PAIR), lambda p, i: (i, base + p)),
            pl.BlockSpec((S, PAIR), lambda p, i: (0, base + n_pairs + p)),
            pl.BlockSpec((S, PAIR), lambda p, i: (0, base + 2 * n_pairs + p))]


NEG = -1e30


def _fox_specs(S, tq, n_pairs, base):
    return _qkv_specs(S, tq, n_pairs, base) + [
        pl.BlockSpec((tq, PAIR), lambda p, i: (i, p)),
        pl.BlockSpec((1, S // tq, 8, tq), lambda p, i: (p, 0, 0, 0))]


RC = 32
VANISH = -104.0


def _chunks(n_rows, fn):
    for ci in range(n_rows // RC):
        fn(ci * RC)


def _wide(v, tk):
    return v if tk == 128 else jnp.tile(v, (1, tk // 128))


def _rep(col):
    return jnp.broadcast_to(col, (col.shape[0], 128))


def _per_head(blk, masks):
    sw = pltpu.roll(blk, HEAD_DIM, axis=1)
    return jnp.where(masks[0], blk, sw), jnp.where(masks[0], sw, blk)


def _fill_masked(dst_ref, src_ref, masks, mul=None, ones_lane=None):
    v = src_ref[...]
    if mul is not None:
        v = v * mul
    lane = lax.broadcasted_iota(jnp.int32, (1, PAIR), 1)
    for h in range(2):
        m = _sel(masks[h], v)
        if ones_lane is not None:
            m = jnp.where(lane == ones_lane[h], jnp.ones_like(m), m)
        dst_ref[h] = m


def _tri(tk, cmp):
    r = lax.broadcasted_iota(jnp.int32, (tk, tk), 0)
    c = lax.broadcasted_iota(jnp.int32, (tk, tk), 1)
    return cmp(r, c).astype(BF16)


def _diag_mask(r0, tk, strict):
    row = r0 + lax.broadcasted_iota(jnp.int32, (RC, tk), 0)
    col = lax.broadcasted_iota(jnp.int32, (RC, tk), 1)
    return (col < row) if strict else (col <= row)


def _peer_copies(src_ref, land_ref, send_sems, recv_sems, scatter, receive_side):
    x, y, c = lax.axis_index("x"), lax.axis_index("y"), lax.axis_index("c")
    me = 4 * x + 2 * y + c
    copies = []
    for k in range(1, N_DEV):
        px, py, pc = (1 - x if k & 4 else x), (1 - y if k & 2 else y), (1 - c if k & 1 else c)
        slot = 4 * px + 2 * py + pc
        copies.append(pltpu.make_async_remote_copy(
            src_ref=src_ref.at[slot] if scatter else src_ref,
            dst_ref=land_ref.at[slot] if receive_side else land_ref.at[me],
            send_sem=send_sems.at[k - 1], recv_sem=recv_sems.at[k - 1], device_id=(px, py, pc), device_id_type=MESH))
    return copies


def _call_carrying(body, exchange, name, grid, in_specs, out_specs, out_shape, scratch_shapes, args,
                   semantics=("parallel", "arbitrary")):
    if exchange is None:
        return pl.pallas_call(body, name=name, grid=grid, in_specs=in_specs, out_specs=out_specs, out_shape=out_shape,
                              scratch_shapes=scratch_shapes, compiler_params=_params(semantics))(*args)
    exchanges = [exchange] if isinstance(exchange, tuple) else list(exchange)
    n_in, n_out, n_ex = len(in_specs), len(out_specs), len(exchanges)

    def carrying(*refs):
        srcs = refs[n_in:n_in + n_ex]
        lands = refs[n_in + n_ex + n_out:n_in + 2 * n_ex + n_out]
        sems = refs[len(refs) - 2 * n_ex:]

        def copies(receive_side):
            return [cp for e, (_, scatter) in enumerate(exchanges)
                    for cp in _peer_copies(srcs[e], lands[e], sems[2 * e], sems[2 * e + 1], scatter, receive_side)]

        first = jnp.logical_and(pl.program_id(0) == 0, pl.program_id(1) == 0)
        last = jnp.logical_and(pl.program_id(0) == grid[0] - 1, pl.program_id(1) == grid[1] - 1)

        @pl.when(first)
        def _():
            for cp in copies(False):
                cp.start()

        body(*refs[:n_in], *refs[n_in + n_ex:n_in + n_ex + n_out], *refs[n_in + 2 * n_ex + n_out:len(refs) - 2 * n_ex])

        @pl.when(last)
        def _():
            for cp in copies(True):
                cp.wait_send()
                cp.wait_recv()

    any_space = pl.BlockSpec(memory_space=pl.ANY)
    lands = [jax.ShapeDtypeStruct((N_DEV,) + src.shape[-2:], src.dtype) for src, _ in exchanges]
    return pl.pallas_call(
        carrying, name=name, grid=grid, in_specs=list(in_specs) + [any_space] * n_ex,
        out_specs=list(out_specs) + [any_space] * n_ex, out_shape=list(out_shape) + lands,
        scratch_shapes=list(scratch_shapes) + [pltpu.SemaphoreType.DMA((N_DEV - 1,))] * (2 * n_ex),
        compiler_params=_params(("arbitrary", "arbitrary")))(*args, *[src for src, _ in exchanges])


def _staggered(bodies):
    active, waiting = [], list(bodies)
    while waiting or active:
        if waiting:
            active.append(waiting.pop(0))
        for g in list(active):
            try:
                next(g)
            except StopIteration:
                active.remove(g)


def _streams(tile, j, diag, slot):
    return [tile(j, diag, slot, (0, 1))]


def _tiles(i, tile):
    def step(jj, carry):
        _staggered(_streams(tile, 2 * jj, False, 0) + _streams(tile, 2 * jj + 1, False, 1))
        return carry
    lax.fori_loop(0, (i - 1) // 2, step, 0)

    @pl.when(jnp.logical_and(i >= 1, (i - 1) % 2 == 1))
    def _():
        _staggered(_streams(tile, i - 2, False, 0))

    @pl.when(i >= 1)
    def _():
        _staggered(_streams(tile, i - 1, False, 0) + _streams(tile, i, True, 1))

    @pl.when(i == 0)
    def _():
        _staggered(_streams(tile, i, True, 1))


def _tiles_reversed(i, tile, keep_going):
    @pl.when(i == 0)
    def _():
        _staggered(_streams(tile, i, True, 0))

    @pl.when(i >= 1)
    def _():
        _staggered(_streams(tile, i, True, 0) + _streams(tile, i - 1, False, 1))

    pairs = (i - 1) // 2

    def cond(carry):
        jj, go = carry
        return jnp.logical_and(jj < pairs, go)

    def step(carry):
        jj, _ = carry
        _staggered(_streams(tile, i - 2 - 2 * jj, False, 0) + _streams(tile, i - 3 - 2 * jj, False, 1))
        return jj + 1, keep_going(jnp.maximum(i - 4 - 2 * jj, 0))

    jj, go = lax.while_loop(cond, step, (jnp.int32(0), keep_going(jnp.maximum(i - 2, 0))))

    @pl.when(jnp.logical_and(jnp.logical_and(i >= 1, (i - 1) % 2 == 1), jnp.logical_and(jj == pairs, go)))
    def _():
        _staggered(_streams(tile, 0, False, 0))


def _sb_fwd(qkv, n_pairs, base, tq, exchange=None):
    S = qkv.shape[0]
    tk = tq
    scale = HEAD_DIM ** -0.5

    def body(q_ref, k_ref, v_ref, o_ref, t_ref, z_ref, hi_ref, suf_ref, p_ref, r_ref, acc_ref, vm_ref):
        i = pl.program_id(1)
        masks = _head_masks()

        @pl.when(i == 0)
        def _():
            _fill_masked(vm_ref, v_ref, masks)

        q2 = q_ref[...] * scale
        qm = [_sel(m, q2) for m in masks]
        incl = _tri(tk, lambda r, c: r >= c)
        r_ref[...] = jnp.zeros_like(r_ref)
        acc_ref[...] = jnp.zeros_like(acc_ref)

        def tile(j, diag, slot, heads):
            off = pl.multiple_of(j * tk, tk)
            k2 = k_ref[pl.ds(off, tk), :]
            v2 = v_ref[pl.ds(off, tk), :]
            for h in heads:
                z_ref[2 * slot + h] = _dot(qm[h], k2, 1, 1)
            yield
            for h in heads:
                def split(r0, h=h):
                    rows = pl.ds(r0, RC)
                    lg = -_softplus(z_ref[2 * slot + h, rows, :])
                    if diag:
                        lg = jnp.where(_diag_mask(r0, tk, True), lg, 0.0)
                    hi_ref[2 * slot + h, rows, :] = lg.astype(BF16)
                _chunks(tq, split)
            yield
            for h in heads:
                suf_ref[2 * slot + h] = _dot(hi_ref[2 * slot + h], incl)
            yield
            for h in heads:
                def weights(r0, h=h):
                    rows = pl.ds(r0, RC)
                    a = jnp.exp(z_ref[2 * slot + h, rows, :] + suf_ref[2 * slot + h, rows, :] + _wide(r_ref[h, rows, :], tk))
                    if diag:
                        a = jnp.where(_diag_mask(r0, tk, True), a, 0.0)
                    p_ref[2 * slot + h, rows, :] = a.astype(BF16)
                _chunks(tq, weights)
            yield
            keys = pl.ds(off, tk)
            for h in heads:
                acc_ref[...] += _dot(p_ref[2 * slot + h], vm_ref[h, keys, :])
            for h in heads:
                r_ref[h] += _rep(suf_ref[2 * slot + h, :, 0:1])

        _tiles_reversed(i, tile, lambda nearest: jnp.max(r_ref[...]) >= VANISH)
        o_ref[...] = acc_ref[...].astype(BF16)
        t_ref[...] = acc_ref[...]

    W = n_pairs * PAIR
    return _call_carrying(
        body, exchange, "sb_fwd", (n_pairs, S // tq), _qkv_specs(S, tq, n_pairs, base),
        [pl.BlockSpec((tq, PAIR), lambda p, i: (i, p)), pl.BlockSpec((tq, PAIR), lambda p, i: (i, p))],
        [jax.ShapeDtypeStruct((S, W), BF16), jax.ShapeDtypeStruct((S, W), F32)],
        [pltpu.VMEM((4, tq, tk), F32), pltpu.VMEM((4, tq, tk), BF16),
         pltpu.VMEM((4, tq, tk), F32), pltpu.VMEM((4, tq, tk), BF16), pltpu.VMEM((2, tq, 128), F32),
         pltpu.VMEM((tq, PAIR), F32), pltpu.VMEM((2, S, PAIR), BF16)],
        (qkv, qkv, qkv))


def _sb_bwd(qkv, o32, do, n_pairs, base, tq, exchange=None):
    S = qkv.shape[0]
    tk = tq
    nq = S // tq
    scale = HEAD_DIM ** -0.5

    def body(q_ref, k_ref, v_ref, o_ref, do_ref, dq_ref, dk_ref, dv_ref,
             z_ref, g_ref, omb_ref, cum_ref, hi_ref, lo_ref, a_ref, dz_ref,
             r_ref, cg_ref, dl_ref, dq_acc, dk_acc, dv_acc, ks_ref):
        i = pl.program_id(1)
        masks = _head_masks()

        @pl.when(i == 0)
        def _():
            dk_acc[...] = jnp.zeros_like(dk_acc)
            dv_acc[...] = jnp.zeros_like(dv_acc)
            _fill_masked(ks_ref, k_ref, masks, mul=scale)

        q2 = q_ref[...] * scale
        do2 = do_ref[...]
        qm = [_sel(m, q2) for m in masks]
        dom = [_sel(m, do2) for m in masks]
        prod = do2.astype(F32) * o_ref[...]
        for h in range(2):
            dl_ref[h] = _rep(jnp.sum(jnp.where(masks[h], prod, 0.0), axis=-1, keepdims=True))
        suffix = _tri(tk, lambda r, c: r >= c)
        r_ref[...] = jnp.zeros_like(r_ref)
        cg_ref[...] = jnp.zeros_like(cg_ref)
        dq_acc[...] = jnp.zeros_like(dq_acc)

        def tile(j, diag, slot, heads):
            off = pl.multiple_of(j * tk, tk)
            k2 = k_ref[pl.ds(off, tk), :]
            v2 = v_ref[pl.ds(off, tk), :]
            for h in heads:
                z_ref[2 * slot + h] = _dot(qm[h], k2, 1, 1)
                g_ref[2 * slot + h] = _dot(dom[h], v2, 1, 1)
            yield
            for h in heads:
                def split(r0, h=h):
                    rows = pl.ds(r0, RC)
                    sp = _softplus(z_ref[2 * slot + h, rows, :])
                    omb_ref[2 * slot + h, rows, :] = jnp.exp(-sp)
                    lg = -sp
                    if diag:
                        lg = jnp.where(_diag_mask(r0, tk, True), lg, 0.0)
                    hi_ref[2 * slot + h, rows, :] = lg.astype(BF16)
                _chunks(tq, split)
            yield
            for h in heads:
                cum_ref[2 * slot + h] = _dot(hi_ref[2 * slot + h], suffix)
            yield
            for h in heads:
                def weights(r0, h=h):
                    rows = pl.ds(r0, RC)
                    a = jnp.exp(z_ref[2 * slot + h, rows, :] + cum_ref[2 * slot + h, rows, :] + _wide(r_ref[h, rows, :], tk))
                    if diag:
                        a = jnp.where(_diag_mask(r0, tk, True), a, 0.0)
                    ab = a.astype(BF16)
                    g = g_ref[2 * slot + h, rows, :] * ab.astype(F32)
                    g_ref[2 * slot + h, rows, :] = g
                    a_ref[2 * slot + h, rows, :] = ab
                    hi, lo = _split2(g)
                    hi_ref[2 * slot + h, rows, :] = hi
                    lo_ref[2 * slot + h, rows, :] = lo
                _chunks(tq, weights)
            for h in heads:
                r_ref[h] += _rep(cum_ref[2 * slot + h, :, 0:1])
            yield
            for h in heads:
                cum_ref[2 * slot + h] = _dot(hi_ref[2 * slot + h], suffix) + _dot(lo_ref[2 * slot + h], suffix)
            yield
            for h in heads:
                def dscore(r0, h=h):
                    rows = pl.ds(r0, RC)
                    g = g_ref[2 * slot + h, rows, :]
                    from_here = cum_ref[2 * slot + h, rows, :] + _wide(cg_ref[h, rows, :], tk)
                    before = _wide(dl_ref[h, rows, :], tk) - from_here
                    omb = omb_ref[2 * slot + h, rows, :]
                    dz = g * omb - (1.0 - omb) * before
                    if diag:
                        dz = jnp.where(_diag_mask(r0, tk, True), dz, 0.0)
                    dz_ref[2 * slot + h, rows, :] = dz.astype(BF16)
                _chunks(tq, dscore)
            for h in heads:
                cg_ref[h] += _rep(cum_ref[2 * slot + h, :, 0:1])
            yield
            keys = pl.ds(off, tk)
            for h in heads:
                dq_acc[...] += _dot(dz_ref[2 * slot + h], ks_ref[h, keys, :])
                dk_acc[keys, :] += _dot(dz_ref[2 * slot + h], qm[h], 0, 0)
                dv_acc[keys, :] += _dot(a_ref[2 * slot + h], dom[h], 0, 0)

        _tiles_reversed(i, tile, lambda nearest: jnp.max(r_ref[...]) >= VANISH)
        dq_ref[...] = dq_acc[...].astype(BF16)

        @pl.when(i == nq - 1)
        def _():
            dk_ref[...] = dk_acc[...].astype(BF16)
            dv_ref[...] = dv_acc[...].astype(BF16)

    W = n_pairs * PAIR
    in_specs = _qkv_specs(S, tq, n_pairs, base) + [
        pl.BlockSpec((tq, PAIR), lambda p, i: (i, p)),
        pl.BlockSpec((tq, PAIR), lambda p, i: (i, p))]
    out_specs = [pl.BlockSpec((tq, PAIR), lambda p, i: (i, p)),
                 pl.BlockSpec((S, PAIR), lambda p, i: (0, p)),
                 pl.BlockSpec((S, PAIR), lambda p, i: (0, p))]
    big, stat = (4, tq, tk), (2, tq, 128)
    return _call_carrying(
        body, exchange, "sb_bwd", (n_pairs, nq), in_specs, out_specs, [jax.ShapeDtypeStruct((S, W), BF16)] * 3,
        [pltpu.VMEM(big, F32)] * 4 + [pltpu.VMEM(big, BF16)] * 4 + [pltpu.VMEM(stat, F32)] * 3
        + [pltpu.VMEM((tq, PAIR), F32), pltpu.VMEM((S, PAIR), F32), pltpu.VMEM((S, PAIR), F32),
           pltpu.VMEM((2, S, PAIR), BF16)],
        (qkv, qkv, qkv, o32, do))


def _fox_fwd(qkv, fcx, fcr, n_pairs, base, tq, exchange=None):
    S = qkv.shape[0]
    tk = tq
    scale = HEAD_DIM ** -0.5
    spare = (HEAD_DIM, 0)

    def body(q_ref, k_ref, v_ref, fq_ref, fk_ref, o_ref, lse_ref, s_ref, p_ref, m_ref, al_ref, fqr_ref, acc_ref, vm_ref):
        i = pl.program_id(1)
        masks = _head_masks()

        @pl.when(i == 0)
        def _():
            _fill_masked(vm_ref, v_ref, masks, ones_lane=spare)

        q2 = q_ref[...] * scale
        qm = [_sel(m, q2) for m in masks]
        f0, f1 = _per_head(fq_ref[...], masks)
        fqr_ref[0] = f0
        fqr_ref[1] = f1
        m_ref[...] = jnp.full(m_ref.shape, NEG, F32)
        acc_ref[...] = jnp.zeros_like(acc_ref)

        def tile(j, diag, slot, heads):
            off = pl.multiple_of(j * tk, tk)
            k2 = k_ref[pl.ds(off, tk), :]
            v2 = v_ref[pl.ds(off, tk), :]
            fk2 = fk_ref[0, j]
            for h in heads:
                s_ref[2 * slot + h] = _dot(qm[h], k2, 1, 1)
            yield
            for h in heads:
                fk_row = fk2[h:h + 1, :]

                def probs(r0, h=h, fk_row=fk_row):
                    rows = pl.ds(r0, RC)
                    sv = s_ref[2 * slot + h, rows, :] - fk_row
                    if diag:
                        sv = jnp.where(_diag_mask(r0, tk, False), sv, NEG)
                    fq = fqr_ref[h, rows, :]
                    m_prev = m_ref[h, rows, :]
                    m_new = jnp.maximum(m_prev, jnp.max(sv, axis=-1, keepdims=True) + fq)
                    p_ref[2 * slot + h, rows, :] = jnp.exp(sv + _wide(fq - m_new, tk)).astype(BF16)
                    al_ref[2 * slot + h, rows, :] = jnp.exp(m_prev - m_new)
                    m_ref[h, rows, :] = m_new
                _chunks(tq, probs)
            yield
            for h in heads:
                acc_ref[h] = acc_ref[h] * al_ref[2 * slot + h] + _dot(p_ref[2 * slot + h], vm_ref[h, pl.ds(off, tk), :])

        _tiles(i, tile)
        a0, a1 = acc_ref[0], acc_ref[1]
        l0 = _rep(a0[:, spare[0]:spare[0] + 1])
        l1 = _rep(a1[:, spare[1]:spare[1] + 1])
        o_ref[...] = jnp.where(masks[0], a0 / l0, a1 / l1).astype(BF16)
        lse_ref[...] = jnp.where(masks[0], m_ref[0] + jnp.log(l0), m_ref[1] + jnp.log(l1))

    W = n_pairs * PAIR
    return _call_carrying(
        body, exchange, "fox_fwd", (n_pairs, S // tq), _fox_specs(S, tq, n_pairs, base),
        [pl.BlockSpec((tq, PAIR), lambda p, i: (i, p)), pl.BlockSpec((tq, PAIR), lambda p, i: (i, p))],
        [jax.ShapeDtypeStruct((S, W), BF16), jax.ShapeDtypeStruct((S, W), F32)],
        [pltpu.VMEM((4, tq, tk), F32), pltpu.VMEM((4, tq, tk), BF16), pltpu.VMEM((2, tq, 128), F32),
         pltpu.VMEM((4, tq, 128), F32), pltpu.VMEM((2, tq, 128), F32), pltpu.VMEM((2, tq, 128), F32),
         pltpu.VMEM((2, S, PAIR), BF16)],
        (qkv, qkv, qkv, fcx, fcr))


def _fox_bwd(qkv, fcx, fcr, o, lse, do, n_pairs, base, tq, exchange=None):
    S = qkv.shape[0]
    tk = tq
    nq = S // tq
    scale = HEAD_DIM ** -0.5

    def body(q_ref, k_ref, v_ref, fq_ref, fk_ref, o_ref, lse_ref, do_ref,
             dq_ref, dk_ref, dv_ref, dfq_ref, dfk_ref,
             s_ref, dp_ref, p_ref, ds_ref, row_ref, dl_ref, dfq_acc, col_ref, dq_acc, dk_acc, dv_acc, ks_ref):
        i = pl.program_id(1)
        masks = _head_masks()

        @pl.when(i == 0)
        def _():
            dk_acc[...] = jnp.zeros_like(dk_acc)
            dv_acc[...] = jnp.zeros_like(dv_acc)
            dfk_ref[...] = jnp.zeros_like(dfk_ref)
            _fill_masked(ks_ref, k_ref, masks, mul=scale)

        q2 = q_ref[...] * scale
        do2 = do_ref[...]
        qm = [_sel(m, q2) for m in masks]
        dom = [_sel(m, do2) for m in masks]
        f0, f1 = _per_head(fq_ref[...], masks)
        l0, l1 = _per_head(lse_ref[...], masks)
        row_ref[0] = f0 - l0
        row_ref[1] = f1 - l1
        prod = do2.astype(F32) * o_ref[...].astype(F32)
        for h in range(2):
            dl_ref[h] = _rep(jnp.sum(jnp.where(masks[h], prod, 0.0), axis=-1, keepdims=True))
        dfq_acc[...] = jnp.zeros_like(dfq_acc)
        dq_acc[...] = jnp.zeros_like(dq_acc)

        def tile(j, diag, slot, heads):
            off = pl.multiple_of(j * tk, tk)
            k2 = k_ref[pl.ds(off, tk), :]
            v2 = v_ref[pl.ds(off, tk), :]
            fk2 = fk_ref[0, j]
            for h in heads:
                s_ref[2 * slot + h] = _dot(qm[h], k2, 1, 1)
                dp_ref[2 * slot + h] = _dot(dom[h], v2, 1, 1)
            yield
            for h in heads:
                col_ref[2 * slot + h] = jnp.zeros((8, tk), F32)
                fk_row = fk2[h:h + 1, :]

                def dscore(r0, h=h, fk_row=fk_row):
                    rows = pl.ds(r0, RC)
                    p = jnp.exp(s_ref[2 * slot + h, rows, :] - fk_row + _wide(row_ref[h, rows, :], tk))
                    if diag:
                        p = jnp.where(_diag_mask(r0, tk, False), p, 0.0)
                    ds = p * (dp_ref[2 * slot + h, rows, :] - _wide(dl_ref[h, rows, :], tk))
                    p_ref[2 * slot + h, rows, :] = p.astype(BF16)
                    ds_ref[2 * slot + h, rows, :] = ds.astype(BF16)
                    dfq_acc[h, rows, :] += _rep(jnp.sum(ds, axis=-1, keepdims=True))
                    col_ref[2 * slot + h] += jnp.sum(ds.reshape(RC // 8, 8, tk), axis=0)
                _chunks(tq, dscore)
            yield
            keys = pl.ds(off, tk)
            for h in heads:
                dq_acc[...] += _dot(ds_ref[2 * slot + h], ks_ref[h, keys, :])
                dk_acc[keys, :] += _dot(ds_ref[2 * slot + h], qm[h], 0, 0)
                dv_acc[keys, :] += _dot(p_ref[2 * slot + h], dom[h], 0, 0)
            for h in heads:
                dfk_ref[0, j, h:h + 1, :] += jnp.sum(col_ref[2 * slot + h], axis=0, keepdims=True)

        _tiles(i, tile)
        dq_ref[...] = dq_acc[...].astype(BF16)
        dfq_ref[...] = jnp.where(masks[0], dfq_acc[0], dfq_acc[1])

        @pl.when(i == nq - 1)
        def _():
            dk_ref[...] = dk_acc[...].astype(BF16)
            dv_ref[...] = dv_acc[...].astype(BF16)

    W = n_pairs * PAIR
    in_specs = _fox_specs(S, tq, n_pairs, base) + [
        pl.BlockSpec((tq, PAIR), lambda p, i: (i, p)),
        pl.BlockSpec((tq, PAIR), lambda p, i: (i, p)),
        pl.BlockSpec((tq, PAIR), lambda p, i: (i, p))]
    out_specs = [pl.BlockSpec((tq, PAIR), lambda p, i: (i, p)),
                 pl.BlockSpec((S, PAIR), lambda p, i: (0, p)),
                 pl.BlockSpec((S, PAIR), lambda p, i: (0, p)),
                 pl.BlockSpec((tq, PAIR), lambda p, i: (i, p)),
                 pl.BlockSpec((1, nq, 8, tk), lambda p, i: (p, 0, 0, 0))]
    return _call_carrying(
        body, exchange, "fox_bwd", (n_pairs, nq), in_specs, out_specs,
        [jax.ShapeDtypeStruct((S, W), BF16)] * 3
        + [jax.ShapeDtypeStruct((S, W), F32), jax.ShapeDtypeStruct((n_pairs, nq, 8, tk), F32)],
        [pltpu.VMEM((4, tq, tk), F32)] * 2 + [pltpu.VMEM((4, tq, tk), BF16)] * 2
        + [pltpu.VMEM((2, tq, 128), F32)] * 3 + [pltpu.VMEM((4, 8, tk), F32)]
        + [pltpu.VMEM((tq, PAIR), F32), pltpu.VMEM((S, PAIR), F32), pltpu.VMEM((S, PAIR), F32),
           pltpu.VMEM((2, S, PAIR), BF16)],
        (qkv, qkv, qkv, fcx, fcr, o, lse, do))


def _col_chunks(n, width=256):
    return [(c, min(width, n - c)) for c in range(0, n, width)]


def _swiglu_fwd(u2, wg, wu):
    S, D = u2.shape
    FF = wg.shape[1]
    tm, tn = _pick(S, (512, 256, 128)), _divisors(FF, 1536)[0]

    def body(u_ref, g_ref, w_ref, a_ref, b_ref, h_ref):
        u = u_ref[...]
        for c, w in _col_chunks(tn):
            cols = slice(c, c + w)
            a = _dot(u, g_ref[:, cols])
            b = _dot(u, w_ref[:, cols])
            a_ref[:, cols] = a.astype(BF16)
            b_ref[:, cols] = b.astype(BF16)
            h_ref[:, cols] = (a / (1.0 + jnp.exp(-a)) * b).astype(BF16)

    spec_o = pl.BlockSpec((tm, tn), lambda i, j: (i, j))
    return pl.pallas_call(
        body, name="swiglu_fwd", grid=(S // tm, FF // tn),
        in_specs=[pl.BlockSpec((tm, D), lambda i, j: (i, 0)),
                  pl.BlockSpec((D, tn), lambda i, j: (0, j)),
                  pl.BlockSpec((D, tn), lambda i, j: (0, j))],
        out_specs=[spec_o] * 3, out_shape=[jax.ShapeDtypeStruct((S, FF), BF16)] * 3,
        compiler_params=_params(("parallel", "parallel")),
    )(u2, wg, wu)


def _swiglu_bwd(dh, wd, a, b):
    S, D = dh.shape
    FF = wd.shape[0]
    tm, tn = _pick(S, (512, 256, 128)), _divisors(FF, 1536)[0]

    def body(dh_ref, w_ref, a_ref, b_ref, da_ref, db_ref):
        dh_blk = dh_ref[...]
        for c, w in _col_chunks(tn):
            cols = slice(c, c + w)
            dhin = _dot(dh_blk, w_ref[cols, :], 1, 1)
            av = a_ref[:, cols].astype(F32)
            bv = b_ref[:, cols].astype(F32)
            sig = 1.0 / (1.0 + jnp.exp(-av))
            da_ref[:, cols] = (dhin * bv * (sig * (1.0 + av * (1.0 - sig)))).astype(BF16)
            db_ref[:, cols] = (dhin * (av * sig)).astype(BF16)

    spec_o = pl.BlockSpec((tm, tn), lambda i, j: (i, j))
    return pl.pallas_call(
        body, name="swiglu_bwd", grid=(S // tm, FF // tn),
        in_specs=[pl.BlockSpec((tm, D), lambda i, j: (i, 0)),
                  pl.BlockSpec((tn, D), lambda i, j: (j, 0)), spec_o, spec_o],
        out_specs=[spec_o] * 2, out_shape=[jax.ShapeDtypeStruct((S, FF), BF16)] * 2,
        compiler_params=_params(("parallel", "parallel")),
    )(dh, wd, a, b)


def _gate_bwd(dmix, wo, g_sb, g_fx, y_sb, y_fx):
    S, D = dmix.shape
    tm, tn = _pick(S, (512, 256, 128)), _divisors(D, 1024)[0]

    def body(dm_ref, w_ref, gs_ref, gf_ref, ys_ref, yf_ref, dys_ref, dyf_ref, dls_ref, dlf_ref, bs_ref, bf_ref):
        @pl.when(pl.program_id(1) == 0)
        def _():
            bs_ref[...] = jnp.zeros_like(bs_ref)
            bf_ref[...] = jnp.zeros_like(bf_ref)

        dm_blk = dm_ref[...]
        for c, w in _col_chunks(tn):
            cols = slice(c, c + w)
            dmi = _dot(dm_blk, w_ref[cols, :], 1, 1)
            gs, gf = gs_ref[:, cols], gf_ref[:, cols]
            dys_ref[:, cols] = (dmi * gs).astype(BF16)
            dyf_ref[:, cols] = (dmi * gf).astype(BF16)
            dls = dmi * ys_ref[:, cols] * gs * (1.0 - gs)
            dlf = dmi * yf_ref[:, cols] * gf * (1.0 - gf)
            dls_ref[:, cols] = dls.astype(BF16)
            dlf_ref[:, cols] = dlf.astype(BF16)
            bs_ref[0:1, cols] += _colsum(dls)
            bf_ref[0:1, cols] += _colsum(dlf)

    t = pl.BlockSpec((tm, tn), lambda j, i: (i, j))
    accs = pl.BlockSpec((8, tn), lambda j, i: (0, j))
    return pl.pallas_call(
        body, name="gate_bwd", grid=(D // tn, S // tm),
        in_specs=[pl.BlockSpec((tm, D), lambda j, i: (i, 0)),
                  pl.BlockSpec((tn, D), lambda j, i: (j, 0)), t, t, t, t],
        out_specs=[t, t, t, t, accs, accs],
        out_shape=[jax.ShapeDtypeStruct((S, D), BF16)] * 4 + [jax.ShapeDtypeStruct((8, D), F32)] * 2,
        compiler_params=_params(("parallel", "arbitrary")),
    )(dmix, wo, g_sb, g_fx, y_sb, y_fx)


def _local_step(x, target, ada8, lnp8, bg_sb, bg_fx, bf_pad, wqkv, wf, wgs, wgf, gather, later_weights, pack_early,
                pack_last):
    S, D = x.shape
    W = wqkv.shape[1] // 6
    n_pairs = W // PAIR
    n_heads = W // HEAD_DIM
    ts = _pick(S, (512, 256, 128))
    tq = _pick(S, (256, 128))

    u1, qkv = _ln_mod_qkv(x, ada8, wqkv, ts)
    f, g_sb, g_fx = _in_gates(u1, wf, wgs, wgf, bg_sb, bg_fx, ts)
    fc = _fgate_fwd(f, bf_pad, _pick(S, (512, 256, 128)))
    fch = fc[:, :n_heads]
    fcx = jnp.repeat(fch, HEAD_DIM, axis=1)
    nq = S // tq
    fcr = jnp.pad(fch.T.reshape(n_pairs, 2, nq, tq).transpose(0, 2, 1, 3),
                  ((0, 0), (0, 0), (0, 6), (0, 0)))
    o_sb, o_sb32, *zone_a = _sb_fwd(qkv, n_pairs, 0, tq, gather[0])
    o_fx, lse, *zone_b = _fox_fwd(qkv, fcx, fcr, n_pairs, 3 * n_pairs, tq, gather[1])
    wsb, wfx, wo, wfg, wfu, wfd = later_weights(*zone_a, *zone_b)
    y_sb =_mm([(o_sb, wsb)], 'nn', F32, "out_sb")
    y_fx = _mm([(o_fx, wfx)], 'nn', F32, "out_fx")
    mix_in, mix, x1, u2 = _gate_mix_out(g_sb, g_fx, y_sb, y_fx, x, wo, ada8, lnp8, ts)
    a, b, hin = _swiglu_fwd(u2, wfg, wfu)
    dr2, dh, st_loss = _loss_head(x1, hin, wfd, target, ada8, lnp8, ts)

    da, db = _swiglu_bwd(dh, wfd, a, b)
    g_wfd = _mm([(hin, dh)], 'tn', F32, "g_ffn_down")
    g_wfg = _mm([(u2, da)], 'tn', F32, "g_ffn_gate")
    g_wfu = _mm([(u2, db)], 'tn', F32, "g_ffn_up")
    dr1, dmix, st_mid = _mid_bwd(da, db, wfg, wfu, x1, dr2, mix, x, ada8, lnp8, _pick(S, (256, 128)))
    dys, dyf, dls, dlf, gb_sb, gb_fx = _gate_bwd(dmix, wo, g_sb, g_fx, y_sb, y_fx)
    g_wo = _mm([(mix_in, dmix)], 'tn', F32, "g_w_o")
    do_sb = _mm([(dys, wsb)], 'nt', BF16, "d_o_sb")
    do_fx = _mm([(dyf, wfx)], 'nt', BF16, "d_o_fx")
    g_wsb = _mm([(o_sb, dys)], 'tn', F32, "g_sb_out")
    g_wfx = _mm([(o_fx, dyf)], 'tn', F32, "g_fox_out")
    scatter = pack_early(dict(sb=g_wsb, fx=g_wfx, o=g_wo, fg=g_wfg, fu=g_wfu, fd=g_wfd))
    dq_s, dk_s, dv_s, *zone_a = _sb_bwd(qkv, o_sb32, do_sb, n_pairs, 0, tq, scatter[0])
    dq_f, dk_f, dv_f, dfq, dfk, *zone_b = _fox_bwd(qkv, fcx, fcr, o_fx, lse, do_fx, n_pairs, 3 * n_pairs, tq,
                                                    scatter[1])
    early = [(scatter[0], zone_a[0] if zone_a else None), (scatter[1], zone_b[0] if zone_b else None)]
    dfc = dfq[:, ::HEAD_DIM] - dfk[:, :, :2, :].transpose(0, 2, 1, 3).reshape(n_heads, S).T
    dfc = jnp.pad(dfc, ((0, 0), (0, 128 - n_heads)))
    df, gb_f = _fgate_bwd(dfc, f, bf_pad, _pick(S, (512, 256, 128)))
    dqkv = jnp.concatenate([dq_s, dk_s, dv_s, dq_f, dk_f, dv_f], axis=1)
    g_wqkv = [_mm([(u1, dqkv)], 'tn', F32, "g_in_qkv")]
    g_wf = _mm([(u1, df)], 'tn', F32, "g_in_f")
    g_wgs = _mm([(u1, dls)], 'tn', F32, "g_in_gsb")
    g_wgf = _mm([(u1, dlf)], 'tn', F32, "g_in_gfx")
    wgrads = dict(qkv=g_wqkv, f=g_wf, gs=g_wgs, gf=g_wgf)
    last = pack_last(wgrads)
    gx, st_first, *last_zone = _first_bwd(
        [(dqkv, wqkv), (df, wf), (dls, wgs), (dlf, wgf)], x, dr1, ada8, last)
    last_zone = last_zone[0] if last_zone else None

    stats = dict(loss=st_loss, mid=st_mid, first=st_first, gb_sb=gb_sb, gb_fx=gb_fx, gb_f=gb_f)
    return gx, wgrads, stats, early, (last, last_zone)


def _position():
    x, y, c = lax.axis_index("x"), lax.axis_index("y"), lax.axis_index("c")
    return x, y, c, 4 * x + 2 * y + c


def _flip(x, y, c, k):
    px = 1 - x if k & 4 else x
    py = 1 - y if k & 2 else y
    pc = 1 - c if k & 1 else c
    return (px, py, pc), 4 * px + 2 * py + pc


def _all_gather_small(v, name):
    r, n = v.shape

    def body(x_ref, out_ref, send_sems, recv_sems, local_sem):
        x, y, c, me = _position()
        mine = pltpu.make_async_copy(x_ref, out_ref.at[me], local_sem)
        mine.start()
        sends = []
        for k in range(1, N_DEV):
            peer, _ = _flip(x, y, c, k)
            cp = pltpu.make_async_remote_copy(
                src_ref=x_ref, dst_ref=out_ref.at[me], send_sem=send_sems.at[k - 1], recv_sem=recv_sems.at[k - 1],
                device_id=peer, device_id_type=MESH)
            cp.start()
            sends.append(cp)
        for k in range(1, N_DEV):
            peer, slot = _flip(x, y, c, k)
            pltpu.make_async_remote_copy(
                src_ref=x_ref, dst_ref=out_ref.at[slot], send_sem=send_sems.at[k - 1], recv_sem=recv_sems.at[k - 1],
                device_id=peer, device_id_type=MESH).wait_recv()
        for cp in sends:
            cp.wait_send()
        mine.wait()

    return pl.pallas_call(
        body, name=name, out_shape=jax.ShapeDtypeStruct((N_DEV, r, n), v.dtype),
        in_specs=[pl.BlockSpec(memory_space=pltpu.VMEM)], out_specs=pl.BlockSpec(memory_space=pltpu.VMEM),
        scratch_shapes=[pltpu.SemaphoreType.DMA((N_DEV - 1,)), pltpu.SemaphoreType.DMA((N_DEV - 1,)),
                        pltpu.SemaphoreType.DMA],
    )(v)


def _all_gather_weights(packed):
    R, C = packed.shape

    def body(x_ref, out_ref, send_sems, recv_sems, local_sem):
        x, y, c, me = _position()
        sibling, sib_slot = _flip(x, y, c, 1)
        mine = pltpu.make_async_copy(x_ref, out_ref.at[me], local_sem)
        mine.start()

        def copy(k, slot, to, src=None):
            return pltpu.make_async_remote_copy(
                src_ref=out_ref.at[slot] if src is None else src, dst_ref=out_ref.at[slot],
                send_sem=send_sems.at[k], recv_sem=recv_sems.at[k], device_id=to, device_id_type=MESH)

        first = [copy(0, me, sibling, src=x_ref)]
        chips = (4, 2, 6)
        for n, k in enumerate(chips):
            peer, _ = _flip(x, y, c, k)
            first.append(copy(1 + n, me, peer, src=x_ref))
        for cp in first:
            cp.start()
        passed = []
        for n, k in enumerate(chips):
            peer, slot = _flip(x, y, c, k)
            copy(1 + n, slot, peer).wait_recv()
            cp = copy(4 + n, slot, sibling)
            cp.start()
            passed.append(cp)
        copy(0, sib_slot, sibling).wait_recv()
        for n, k in enumerate(chips):
            _, slot = _flip(x, y, c, k | 1)
            copy(4 + n, slot, sibling).wait_recv()
        for cp in first + passed:
            cp.wait_send()
        mine.wait()

    return pl.pallas_call(
        body, name="all_gather_weights", out_shape=jax.ShapeDtypeStruct((N_DEV, R, C), packed.dtype),
        in_specs=[pl.BlockSpec(memory_space=pl.ANY)], out_specs=pl.BlockSpec(memory_space=pl.ANY),
        scratch_shapes=[pltpu.SemaphoreType.DMA((7,)), pltpu.SemaphoreType.DMA((7,)), pltpu.SemaphoreType.DMA],
    )(packed)


def _own_slot(land, own):
    me = 4 * lax.axis_index("x") + 2 * lax.axis_index("y") + lax.axis_index("c")
    return lax.dynamic_update_slice(land, own[None], (me, 0, 0))


def _sum_slots(recv, name, tr):
    n, R, C = recv.shape

    def body(r_ref, o_ref):
        acc = r_ref[0].astype(F32)
        for s in range(1, n):
            acc = acc + r_ref[s].astype(F32)
        o_ref[...] = acc

    return pl.pallas_call(
        body, name=name, grid=(R // tr,), in_specs=[pl.BlockSpec((n, tr, C), lambda i: (0, i, 0))],
        out_specs=pl.BlockSpec((tr, C), lambda i: (i, 0)), out_shape=jax.ShapeDtypeStruct((R, C), F32),
        compiler_params=_params(("parallel",)),
    )(recv)


def _sum_stats(st_all, loss_row):
    n, r, D = st_all.shape

    def body(s_ref, o_ref, l_ref):
        acc = s_ref[0]
        for d in range(1, n):
            acc = acc + s_ref[d]
        o_ref[...] = acc
        l_ref[...] = jnp.zeros((8, 128), F32) + jnp.sum(acc[loss_row:loss_row + 1, :], axis=-1, keepdims=True)

    return pl.pallas_call(
        body, name="sum_stats", out_shape=[jax.ShapeDtypeStruct((r, D), F32), jax.ShapeDtypeStruct((8, 128), F32)],
    )(st_all)


def _adamw(w, g, m, v, name):
    R, C = w.shape
    tr = _pick(R, (256, 176, 128, 64, 32, 16, 8))
    c1 = 1.0 / (1.0 - ADAM_B1 ** ADAM_STEP)
    c2 = 1.0 / (1.0 - ADAM_B2 ** ADAM_STEP)

    def body(w_ref, g_ref, m_ref, v_ref, d_ref, nm_ref, nv_ref):
        gv = g_ref[...]
        nm = ADAM_B1 * m_ref[...] + (1.0 - ADAM_B1) * gv
        nv = ADAM_B2 * v_ref[...] + (1.0 - ADAM_B2) * (gv * gv)
        nm_ref[...] = nm
        nv_ref[...] = nv
        d_ref[...] = -ADAM_LR * ((nm * c1) / (jnp.sqrt(nv * c2) + ADAM_EPS) + ADAM_WD * w_ref[...])

    spec = pl.BlockSpec((tr, C), lambda i: (i, 0))
    return pl.pallas_call(
        body, name=name, grid=(R // tr,), in_specs=[spec] * 4, out_specs=[spec] * 3,
        out_shape=[jax.ShapeDtypeStruct((R, C), F32)] * 3, compiler_params=_params(("parallel",)),
    )(w, g, m, v)


def _round16(n):
    return -(-n // 16) * 16


def _pack_layout(D, in_cols, ff, W):
    parts = [("in", D * (in_cols // N_DEV) // D), ("fg", ff // N_DEV), ("fu", ff // N_DEV),
             ("sb", W * (D // N_DEV) // D), ("fx", W * (D // N_DEV) // D), ("o", D // N_DEV), ("fd", ff // N_DEV)]
    layout, off = {}, 0
    for nm, rows in parts:
        layout[nm] = (off, rows)
        off += _round16(rows)
    return layout, off


def _rows_of(a, D, rows):
    a = a.reshape(rows, D)
    return jnp.pad(a, ((0, _round16(rows) - rows), (0, 0)))


def _cols_to_dest(g, D):
    K, N = g.shape
    n = N // N_DEV
    return g.reshape(K, N_DEV, n).transpose(1, 0, 2).reshape(N_DEV, K * n // D, D)


def _cols_from_src(blocks, K, n):
    return blocks.reshape(N_DEV, K, n).transpose(1, 0, 2).reshape(K, N_DEV * n)


def _pad_rows16(a):
    rows = a.shape[1]
    return jnp.pad(a, ((0, 0), (0, _round16(rows) - rows), (0, 0)))


def kernel(x, c, w_ada, b_ada, w_in, b_gate, b_forget, w_sb_out, w_fox_out, w_o, ln1_g, ln1_b, w_ffn_gate, w_ffn_up, w_ffn_down, ln2_g, ln2_b, loss_target, m_w_ada, m_b_ada, m_w_in, m_b_gate, m_b_forget, m_w_sb_out, m_w_fox_out, m_w_o, m_ln1_g, m_ln1_b, m_w_ffn_gate, m_w_ffn_up, m_w_ffn_down, m_ln2_g, m_ln2_b, v_w_ada, v_b_ada, v_w_in, v_b_gate, v_b_forget, v_w_sb_out, v_w_fox_out, v_w_o, v_ln1_g, v_ln1_b, v_w_ffn_gate, v_w_ffn_up, v_w_ffn_down, v_ln2_g, v_ln2_b):
    S, D = x.shape[1], x.shape[2]
    W = w_sb_out.shape[1]
    n_heads = b_forget.shape[1]
    ff = w_ffn_down.shape[1] * N_DEV
    in_loc = w_in.shape[2]
    in_cols = in_loc * N_DEV
    ada_loc = w_ada.shape[2]
    n_cond = ada_loc * N_DEV // D
    assert w_ada.shape[0] == 1 and n_cond == 6 and in_cols == 6 * W + n_heads + 2 * D and n_heads <= 128
    me = 4 * lax.axis_index("x") + 2 * lax.axis_index("y") + lax.axis_index("c")

    c_all = _all_gather_small(c, "gather_c").reshape(N_DEV, D)
    c16 = jnp.pad(c_all, ((0, 16 - N_DEV), (0, 0)))
    b_cols = lax.dynamic_slice(b_ada, (0, me * ada_loc), (1, ada_loc))
    ada_cols = _mm([(c16, w_ada[0])], 'nn', F32, "ada_fwd", bias=b_cols, silu_a=True)[:N_DEV]
    ada_all = _all_gather_small(ada_cols, "gather_ada")
    ada_me = lax.dynamic_index_in_dim(ada_all, me, axis=1, keepdims=False)
    ada8 = jnp.pad(ada_me.reshape(n_cond, D), ((0, 8 - n_cond), (0, 0)))
    lnp8 = jnp.concatenate([ln1_g, ln1_b, ln2_g, ln2_b, jnp.zeros((4, D), F32)], axis=0)

    layout, R = _pack_layout(D, in_cols, ff, W)
    shards = dict(**{"in": w_in[0]}, fg=w_ffn_gate[0], fu=w_ffn_up[0], sb=w_sb_out[0], fx=w_fox_out[0], o=w_o[0],
                  fd=w_ffn_down[0])
    rows_fwd, rows_bwd = ("sb", "fx", "o"), ("fd", "o", "sb", "fx")
    rows_of = {nm: layout[nm][1] for nm in layout}

    def offsets(names):
        offs, off = {}, 0
        for nm in names:
            offs[nm] = off
            off += _round16(rows_of[nm])
        return offs

    def as_rows(names):
        return jnp.concatenate([_rows_of(shards[nm].astype(BF16), D, rows_of[nm]) for nm in names], axis=0)

    def whole_from(blocks):
        return blocks.transpose(1, 0, 2).reshape(blocks.shape[1], N_DEV * blocks.shape[2])

    def blocks_of(g):
        return g.reshape(g.shape[0], N_DEV, g.shape[1] // N_DEV).transpose(1, 0, 2)

    gather_src = [as_rows(rows_fwd),
                  jnp.concatenate([shards["fg"].astype(BF16), shards["fu"].astype(BF16)], axis=0), as_rows(("fd",))]
    gathered_in = _all_gather_weights(shards["in"].astype(BF16))

    w_in_full = whole_from(gathered_in)
    wqkv = w_in_full[:, :6 * W]
    wf = jnp.pad(w_in_full[:, 6 * W:6 * W + n_heads], ((0, 0), (0, 128 - n_heads)))
    wgs = w_in_full[:, 6 * W + n_heads:6 * W + n_heads + D]
    wgf = w_in_full[:, 6 * W + n_heads + D:]
    bf_pad = jnp.pad(b_forget, ((0, 0), (0, 128 - n_heads)))

    def later_weights(zone_rows, zone_gate_up, zone_down):
        rows, offs = _own_slot(zone_rows, gather_src[0]), offsets(rows_fwd)
        part = {nm: rows[:, offs[nm]:offs[nm] + rows_of[nm], :] for nm in rows_fwd}
        gate_up = _own_slot(zone_gate_up, gather_src[1])
        down = _own_slot(zone_down, gather_src[2])[:, :rows_of["fd"], :]
        return (_cols_from_src(part["sb"], W, D // N_DEV), _cols_from_src(part["fx"], W, D // N_DEV),
                part["o"].reshape(D, D), whole_from(gate_up[:, :D, :]), whole_from(gate_up[:, D:, :]),
                down.reshape(ff, D))

    def pack_early(g):
        dest = {"sb": _cols_to_dest(g["sb"], D), "fx": _cols_to_dest(g["fx"], D),
                "o": g["o"].reshape(N_DEV, D // N_DEV, D), "fd": g["fd"].reshape(N_DEV, ff // N_DEV, D)}
        rows = jnp.concatenate([_pad_rows16(dest[nm].astype(BF16)) for nm in rows_bwd], axis=1)
        gate_up = jnp.concatenate([blocks_of(g["fg"].astype(BF16)), blocks_of(g["fu"].astype(BF16))], axis=1)
        return [(rows, True), (gate_up, True)]

    def pack_last(g):
        g_in = jnp.concatenate(g["qkv"] + [g["f"][:, :n_heads], g["gs"], g["gf"]], axis=1)
        return blocks_of(g_in.astype(BF16)), True

    gx, wg, st, early, ((pack_in, _), land_in) = _local_step(
        x[0], loss_target[0], ada8, lnp8, b_gate[:, :D], b_gate[:, D:], bf_pad, wqkv, wf, wgs, wgf,
        [(gather_src[0], False), [(gather_src[1], False), (gather_src[2], False)]],
        later_weights, pack_early, pack_last)

    def summed(zone, sent, name):
        own = lax.dynamic_index_in_dim(sent, me, axis=0, keepdims=False)
        rows = zone.shape[1]
        block = max(t for t in range(16, 705, 16) if rows % t == 0)
        return _sum_slots(_own_slot(zone, own), name, block)

    ((sent_rows, _), zone_rows), ((sent_gate_up, _), zone_gate_up) = early
    gate_up = summed(zone_gate_up, sent_gate_up, "sum_grads_gate_up")
    gsum = {"in": summed(land_in, pack_in, "sum_grads_in"), "fg": gate_up[:D], "fu": gate_up[D:]}
    total, offs = summed(zone_rows, sent_rows, "sum_grads_rows"), offsets(rows_bwd)
    for nm in rows_bwd:
        gsum[nm] = total[offs[nm]:offs[nm] + rows_of[nm]]

    def gshard(nm, shape):
        return gsum[nm].reshape(shape)

    zrow = jnp.zeros((1, D), F32)
    gb_f_row = jnp.pad(st["gb_f"][0:1], ((0, 0), (0, D - 128)))
    stats16 = jnp.concatenate([
        st["first"][1:2], st["first"][0:1], st["mid"][4:5], st["mid"][1:2], st["mid"][0:1], st["loss"][3:4],
        st["mid"][2:3], st["mid"][3:4], st["loss"][1:2], st["loss"][2:3], st["gb_sb"][0:1], st["gb_fx"][0:1],
        st["loss"][0:1], gb_f_row, zrow, zrow], axis=0)
    st_all = _all_gather_small(stats16, "gather_stats")
    st_sum, loss_blk = _sum_stats(st_all, 12)
    loss = loss_blk[0, 0]

    d_ada_all = st_all[:, :n_cond, :].reshape(N_DEV, n_cond * D)
    d_cols = lax.dynamic_slice(d_ada_all, (0, me * ada_loc), (N_DEV, ada_loc))
    d16 = jnp.pad(d_cols, ((0, 16 - N_DEV), (0, 0)))
    g_w_ada = _mm([(c16, d16)], 'tn', F32, "ada_wgrad", silu_a=True)

    small_w = jnp.concatenate([b_ada.reshape(n_cond, D), ln1_g, ln1_b, ln2_g, ln2_b, b_gate.reshape(2, D), zrow,
                               jnp.pad(b_forget, ((0, 0), (0, D - n_heads))), zrow, zrow], axis=0)
    small_m = jnp.concatenate([m_b_ada.reshape(n_cond, D), m_ln1_g, m_ln1_b, m_ln2_g, m_ln2_b, m_b_gate.reshape(2, D),
                               zrow, jnp.pad(m_b_forget, ((0, 0), (0, D - n_heads))), zrow, zrow], axis=0)
    small_v = jnp.concatenate([v_b_ada.reshape(n_cond, D), v_ln1_g, v_ln1_b, v_ln2_g, v_ln2_b, v_b_gate.reshape(2, D),
                               zrow, jnp.pad(v_b_forget, ((0, 0), (0, D - n_heads))), zrow, zrow], axis=0)
    sm = _adamw(small_w, st_sum, small_m, small_v, "adamw_small")

    def small(a, nm):
        if nm == "b_ada":
            return a[0:n_cond].reshape(1, n_cond * D)
        if nm == "b_gate":
            return a[10:12].reshape(1, 2 * D)
        if nm == "b_forget":
            return a[13:14, :n_heads]
        row = {"ln1_g": 6, "ln1_b": 7, "ln2_g": 8, "ln2_b": 9}[nm]
        return a[row:row + 1]

    big = {
        "w_ada": (w_ada[0], g_w_ada, m_w_ada[0], v_w_ada[0]),
        "w_in": (w_in[0], gshard("in", w_in.shape[1:]), m_w_in[0], v_w_in[0]),
        "w_sb_out": (w_sb_out[0], gshard("sb", w_sb_out.shape[1:]), m_w_sb_out[0], v_w_sb_out[0]),
        "w_fox_out": (w_fox_out[0], gshard("fx", w_fox_out.shape[1:]), m_w_fox_out[0], v_w_fox_out[0]),
        "w_o": (w_o[0], gshard("o", w_o.shape[1:]), m_w_o[0], v_w_o[0]),
        "w_ffn_gate": (w_ffn_gate[0], gshard("fg", w_ffn_gate.shape[1:]), m_w_ffn_gate[0], v_w_ffn_gate[0]),
        "w_ffn_up": (w_ffn_up[0], gshard("fu", w_ffn_up.shape[1:]), m_w_ffn_up[0], v_w_ffn_up[0]),
        "w_ffn_down": (w_ffn_down[0], gshard("fd", w_ffn_down.shape[1:]), m_w_ffn_down[0], v_w_ffn_down[0]),
    }
    order = ["w_ada", "b_ada", "w_in", "b_gate", "b_forget", "w_sb_out", "w_fox_out", "w_o", "ln1_g", "ln1_b",
             "w_ffn_gate", "w_ffn_up", "w_ffn_down", "ln2_g", "ln2_b"]
    grads, deltas, new_ms, new_vs = [], [], [], []
    for nm in order:
        if nm in big:
            w, g, m, v = big[nm]
            d, nm_, nv_ = _adamw(w, g, m, v, "adamw_" + nm)
            grads.append(g[None])
            deltas.append(d[None])
            new_ms.append(nm_[None])
            new_vs.append(nv_[None])
        else:
            grads.append(small(st_sum, nm))
            deltas.append(small(sm[0], nm))
            new_ms.append(small(sm[1], nm))
            new_vs.append(small(sm[2], nm))
    return (loss, gx[None], *grads, *deltas, *new_ms, *new_vs)
```
